```python
import jax, jax.numpy as jnp
from jax import lax
import numpy as np

D_MODEL = 2048
BATCH = 8
SEQ = 8192
DEPTH = 1

N_ATTN_HEADS = 8
HEAD_DIM = 128
D_ATTN = N_ATTN_HEADS * HEAD_DIM
D_CONV = D_MODEL - D_ATTN
D_MIX = D_ATTN + D_CONV
D_IN = 3 * D_ATTN + 2 * D_CONV
DILATED_BRANCHES = ((128, 1), (512, 4), (2048, 16))
BLK = 128
CONV_WIDTH = 31
D_FF = ((-(-8 * D_MODEL // 3) + 255) // 256) * 256
D_PLE = 256
EPS = 1e-6

kernel_name = "hymba_dilated_attn_conformer_conv_hybrid"


def rmsnorm(x, g):
    xf = x.astype(jnp.float32)
    y = xf * lax.rsqrt(jnp.mean(xf * xf, axis=-1, keepdims=True) + EPS)
    return (y * g.astype(jnp.float32)).astype(x.dtype)


def layernorm(x, g, b):
    xf = x.astype(jnp.float32)
    mu = jnp.mean(xf, axis=-1, keepdims=True)
    var = jnp.mean(jnp.square(xf - mu), axis=-1, keepdims=True)
    y = (xf - mu) * lax.rsqrt(var + EPS)
    return (y * g.astype(jnp.float32) + b.astype(jnp.float32)).astype(x.dtype)


def dilated_branch(q, k, v, window, dilation):
    B, S, H, Dh = q.shape
    span = dilation * BLK
    L = -(-S // span) * span
    nb = L // span
    pad = ((0, 0), (0, L - S), (0, 0), (0, 0))

    def blocks(a):
        return jnp.pad(a, pad).reshape(B, nb, BLK, dilation, H, Dh)

    def with_prev(a):
        prev = jnp.pad(a, ((0, 0), (1, 0), (0, 0), (0, 0), (0, 0), (0, 0)))[:, :-1]
        return jnp.concatenate([prev, a], axis=2)

    qb = blocks(q)
    kk = with_prev(blocks(k))
    vv = with_prev(blocks(v))
    s = jnp.einsum('bnqrhd,bnkrhd->bnrhqk', qb, kk).astype(jnp.float32) * (Dh ** -0.5)
    qi = jnp.arange(BLK)[:, None]
    kj = jnp.arange(2 * BLK)[None, :]
    dist = qi + BLK - kj
    band = (dist >= 0) & (dist <= window // dilation)
    first = (jnp.arange(nb)[:, None, None] > 0) | (kj[None] >= BLK)
    mask = band[None] & first
    s = jnp.where(mask[None, :, None, None], s, -jnp.inf)
    m = jnp.max(s, axis=-1, keepdims=True)
    e = jnp.exp(s - m)
    den = jnp.sum(e, axis=-1, keepdims=True)
    o = jnp.einsum('bnrhqk,bnkrhd->bnqrhd', (e / den).astype(v.dtype), vv)
    o = o.reshape(B, L, H, Dh)[:, :S]
    lse = (m + jnp.log(den))[..., 0]
    lse = lse.transpose(0, 1, 4, 2, 3).reshape(B, L, H)[:, :S]
    return o, lse


def dilated_attention(q, k, v):
    outs, lses = [], []
    for window, dilation in DILATED_BRANCHES:
        o, lse = dilated_branch(q, k, v, window, dilation)
        outs.append(o)
        lses.append(lse)
    w = jax.nn.softmax(jnp.stack(lses, axis=0), axis=0)
    o = jnp.sum(w[..., None] * jnp.stack(outs, axis=0).astype(jnp.float32), axis=0)
    return o.astype(q.dtype)


def conformer_conv(cv, cg, w_dw, b_dw, g_ln, b_ln):
    u = cv * jax.nn.sigmoid(cg)
    u = lax.conv_general_dilated(u, w_dw.astype(u.dtype), window_strides=(1,),
                                 padding=[(CONV_WIDTH - 1, 0)],
                                 dimension_numbers=('NWC', 'WIO', 'NWC'),
                                 feature_group_count=D_CONV) + b_dw
    return jax.nn.silu(layernorm(u, g_ln, b_ln))


def _fwd_setup_inputs(seed: int = 0) -> dict:
    key = jax.random.key(seed)
    ks = jax.random.split(key, 20)

    def nrm(k, shape, scale):
        return jax.random.normal(k, shape, jnp.float32) * scale

    def gain(k, n):
        return 1.0 + 0.02 * jax.random.normal(k, (DEPTH, n), jnp.float32)

    return {
        "x": nrm(ks[0], (BATCH, SEQ, D_MODEL), 1.0),
        "p": nrm(ks[1], (DEPTH, BATCH, SEQ, D_PLE), 1.0),
        "g_mix": gain(ks[2], D_MODEL),
        "w_in": nrm(ks[3], (DEPTH, D_MODEL, D_IN), D_MODEL ** -0.5),
        "w_dw": nrm(ks[4], (DEPTH, CONV_WIDTH, 1, D_CONV), CONV_WIDTH ** -0.5),
        "b_dw": nrm(ks[5], (DEPTH, D_CONV), 0.02),
        "g_conv_ln": gain(ks[6], D_CONV),
        "b_conv_ln": nrm(ks[7], (DEPTH, D_CONV), 0.02),
        "w_out": nrm(ks[8], (DEPTH, D_MIX, D_MODEL), D_MIX ** -0.5),
        "g_ffn": gain(ks[9], D_MODEL),
        "w_gate": nrm(ks[10], (DEPTH, D_MODEL, D_FF), D_MODEL ** -0.5),
        "w_up": nrm(ks[11], (DEPTH, D_MODEL, D_FF), D_MODEL ** -0.5),
        "w_down": nrm(ks[12], (DEPTH, D_FF, D_MODEL), D_FF ** -0.5),
        "g_ple": gain(ks[13], D_MODEL),
        "w_pgate": nrm(ks[14], (DEPTH, D_MODEL, D_MODEL), D_MODEL ** -0.5),
        "b_pgate": nrm(ks[15], (DEPTH, D_MODEL), 0.02),
        "w_ple": nrm(ks[16], (DEPTH, D_PLE, D_MODEL), D_PLE ** -0.5),
        "g_final": 1.0 + 0.02 * jax.random.normal(ks[17], (D_MODEL,), jnp.float32),
    }


def _fwd_reference(x, p, g_mix, w_in, w_dw, b_dw, g_conv_ln, b_conv_ln, w_out, g_ffn,
              w_gate, w_up, w_down, g_ple, w_pgate, b_pgate, w_ple, g_final):
    B, S, _ = x.shape
    h = x
    for i in range(DEPTH):
        a = rmsnorm(h, g_mix[i])
        z = a @ w_in[i]
        q, k, v, cv, cg = jnp.split(
            z, [D_ATTN, 2 * D_ATTN, 3 * D_ATTN, 3 * D_ATTN + D_CONV], axis=-1)
        hd = (B, S, N_ATTN_HEADS, HEAD_DIM)
        o_attn = dilated_attention(q.reshape(hd), k.reshape(hd), v.reshape(hd)).reshape(B, S, D_ATTN)
        o_conv = conformer_conv(cv, cg, w_dw[i], b_dw[i], g_conv_ln[i], b_conv_ln[i])
        h = h + jnp.concatenate([o_attn, o_conv], axis=-1) @ w_out[i]
        f = rmsnorm(h, g_ffn[i])
        h = h + (jax.nn.silu(f @ w_gate[i]) * (f @ w_up[i])) @ w_down[i]
        gte = jax.nn.sigmoid(rmsnorm(h, g_ple[i]) @ w_pgate[i] + b_pgate[i])
        h = h + (p[i] @ w_ple[i]) * gte
    return rmsnorm(h, g_final)


import jax as _jax
import jax.numpy as _jnp

TWIN_FORMAT = 'train_step'
FWD_PARAMS = ['x', 'p', 'g_mix', 'w_in', 'w_dw', 'b_dw', 'g_conv_ln', 'b_conv_ln', 'w_out', 'g_ffn', 'w_gate', 'w_up', 'w_down', 'g_ple', 'w_pgate', 'b_pgate', 'w_ple', 'g_final']
TWIN_WEIGHTS = ['g_mix', 'w_in', 'w_dw', 'b_dw', 'g_conv_ln', 'b_conv_ln', 'w_out', 'g_ffn', 'w_gate', 'w_up', 'w_down', 'g_ple', 'w_pgate', 'b_pgate', 'w_ple', 'g_final']
TWIN_DIFF_INPUT = 'x'
TWIN_INPUTS = ['x', 'p', 'g_mix', 'w_in', 'w_dw', 'b_dw', 'g_conv_ln', 'b_conv_ln', 'w_out', 'g_ffn', 'w_gate', 'w_up', 'w_down', 'g_ple', 'w_pgate', 'b_pgate', 'w_ple', 'g_final', 'loss_target', 'm_g_mix', 'm_w_in', 'm_w_dw', 'm_b_dw', 'm_g_conv_ln', 'm_b_conv_ln', 'm_w_out', 'm_g_ffn', 'm_w_gate', 'm_w_up', 'm_w_down', 'm_g_ple', 'm_w_pgate', 'm_b_pgate', 'm_w_ple', 'm_g_final', 'v_g_mix', 'v_w_in', 'v_w_dw', 'v_b_dw', 'v_g_conv_ln', 'v_b_conv_ln', 'v_w_out', 'v_g_ffn', 'v_w_gate', 'v_w_up', 'v_w_down', 'v_g_ple', 'v_w_pgate', 'v_b_pgate', 'v_w_ple', 'v_g_final']
TWIN_OUTPUTS = ['loss', 'grad_x', 'grad_g_mix', 'grad_w_in', 'grad_w_dw', 'grad_b_dw', 'grad_g_conv_ln', 'grad_b_conv_ln', 'grad_w_out', 'grad_g_ffn', 'grad_w_gate', 'grad_w_up', 'grad_w_down', 'grad_g_ple', 'grad_w_pgate', 'grad_b_pgate', 'grad_w_ple', 'grad_g_final', 'delta_g_mix', 'delta_w_in', 'delta_w_dw', 'delta_b_dw', 'delta_g_conv_ln', 'delta_b_conv_ln', 'delta_w_out', 'delta_g_ffn', 'delta_w_gate', 'delta_w_up', 'delta_w_down', 'delta_g_ple', 'delta_w_pgate', 'delta_b_pgate', 'delta_w_ple', 'delta_g_final', 'new_m_g_mix', 'new_m_w_in', 'new_m_w_dw', 'new_m_b_dw', 'new_m_g_conv_ln', 'new_m_b_conv_ln', 'new_m_w_out', 'new_m_g_ffn', 'new_m_w_gate', 'new_m_w_up', 'new_m_w_down', 'new_m_g_ple', 'new_m_w_pgate', 'new_m_b_pgate', 'new_m_w_ple', 'new_m_g_final', 'new_v_g_mix', 'new_v_w_in', 'new_v_w_dw', 'new_v_b_dw', 'new_v_g_conv_ln', 'new_v_b_conv_ln', 'new_v_w_out', 'new_v_g_ffn', 'new_v_w_gate', 'new_v_w_up', 'new_v_w_down', 'new_v_g_ple', 'new_v_w_pgate', 'new_v_b_pgate', 'new_v_w_ple', 'new_v_g_final']
TWIN_LEAF_KINDS = {'loss': 'loss', 'grad_x': 'grad_x', 'grad_g_mix': 'grad_w', 'grad_w_in': 'grad_w', 'grad_w_dw': 'grad_w', 'grad_b_dw': 'grad_w', 'grad_g_conv_ln': 'grad_w', 'grad_b_conv_ln': 'grad_w', 'grad_w_out': 'grad_w', 'grad_g_ffn': 'grad_w', 'grad_w_gate': 'grad_w', 'grad_w_up': 'grad_w', 'grad_w_down': 'grad_w', 'grad_g_ple': 'grad_w', 'grad_w_pgate': 'grad_w', 'grad_b_pgate': 'grad_w', 'grad_w_ple': 'grad_w', 'grad_g_final': 'grad_w', 'delta_g_mix': 'delta_w', 'delta_w_in': 'delta_w', 'delta_w_dw': 'delta_w', 'delta_b_dw': 'delta_w', 'delta_g_conv_ln': 'delta_w', 'delta_b_conv_ln': 'delta_w', 'delta_w_out': 'delta_w', 'delta_g_ffn': 'delta_w', 'delta_w_gate': 'delta_w', 'delta_w_up': 'delta_w', 'delta_w_down': 'delta_w', 'delta_g_ple': 'delta_w', 'delta_w_pgate': 'delta_w', 'delta_b_pgate': 'delta_w', 'delta_w_ple': 'delta_w', 'delta_g_final': 'delta_w', 'new_m_g_mix': 'new_m', 'new_m_w_in': 'new_m', 'new_m_w_dw': 'new_m', 'new_m_b_dw': 'new_m', 'new_m_g_conv_ln': 'new_m', 'new_m_b_conv_ln': 'new_m', 'new_m_w_out': 'new_m', 'new_m_g_ffn': 'new_m', 'new_m_w_gate': 'new_m', 'new_m_w_up': 'new_m', 'new_m_w_down': 'new_m', 'new_m_g_ple': 'new_m', 'new_m_w_pgate': 'new_m', 'new_m_b_pgate': 'new_m', 'new_m_w_ple': 'new_m', 'new_m_g_final': 'new_m', 'new_v_g_mix': 'new_v', 'new_v_w_in': 'new_v', 'new_v_w_dw': 'new_v', 'new_v_b_dw': 'new_v', 'new_v_g_conv_ln': 'new_v', 'new_v_b_conv_ln': 'new_v', 'new_v_w_out': 'new_v', 'new_v_g_ffn': 'new_v', 'new_v_w_gate': 'new_v', 'new_v_w_up': 'new_v', 'new_v_w_down': 'new_v', 'new_v_g_ple': 'new_v', 'new_v_w_pgate': 'new_v', 'new_v_b_pgate': 'new_v', 'new_v_w_ple': 'new_v', 'new_v_g_final': 'new_v'}


def _forward(args):
    return _fwd_reference(*[args[k] for k in FWD_PARAMS])


def _output_shape():
    def fwd():
        inp = _fwd_setup_inputs(0)
        return _fwd_reference(*[inp[k] for k in FWD_PARAMS])
    out = _jax.eval_shape(fwd)
    return out.shape, out.dtype

N_MICROBATCH = 1
ADAM_LR = 0.001
ADAM_B1 = 0.9
ADAM_B2 = 0.999
ADAM_EPS = 1e-08
ADAM_WD = 0.01
ADAM_STEP = 10
PER_EXAMPLE_BATCH_AXIS = {'x': 0, 'p': 1, 'loss_target': 0}
SHARED_INPUTS = []
_WEIGHT_DTYPES = {'g_mix': _jnp.float32, 'w_in': _jnp.float32, 'w_dw': _jnp.float32, 'b_dw': _jnp.float32, 'g_conv_ln': _jnp.float32, 'b_conv_ln': _jnp.float32, 'w_out': _jnp.float32, 'g_ffn': _jnp.float32, 'w_gate': _jnp.float32, 'w_up': _jnp.float32, 'w_down': _jnp.float32, 'g_ple': _jnp.float32, 'w_pgate': _jnp.float32, 'b_pgate': _jnp.float32, 'w_ple': _jnp.float32, 'g_final': _jnp.float32}
MOMENT_SCALE = {'g_mix': 6.240838e-02, 'w_in': 3.971256e-02, 'w_dw': 7.324180e-02, 'b_dw': 1.465793e-01, 'g_conv_ln': 8.726979e-02, 'b_conv_ln': 7.684419e-02, 'w_out': 5.363846e-02, 'g_ffn': 8.432258e-02, 'w_gate': 3.498763e-02, 'w_up': 3.393968e-02, 'w_down': 5.627547e-02, 'g_ple': 1.996521e-02, 'w_pgate': 2.019332e-02, 'b_pgate': 5.149109e-02, 'w_ple': 5.084177e-02, 'g_final': 3.197155e+01}


def _to_microbatches(a, axis):
    t = _jnp.moveaxis(a, axis, 0)
    t = t.reshape((N_MICROBATCH, t.shape[0] // N_MICROBATCH) + t.shape[1:])
    return _jnp.moveaxis(t, 1, axis + 1)


def setup_inputs(seed: int = 0) -> dict:
    inp = _fwd_setup_inputs(seed)
    key = _jax.random.fold_in(_jax.random.key(seed), 7919)
    shape, _ = _output_shape()
    out = dict(inp)
    out["loss_target"] = _jax.random.normal(_jax.random.fold_in(key, 0), shape, _jnp.float32)
    for i, name in enumerate(TWIN_WEIGHTS):
        w = inp[name].astype(_jnp.float32)
        if MOMENT_SCALE is None:
            s = _jnp.sqrt(_jnp.mean(_jnp.square(w)) + 1e-30)
        else:
            s = MOMENT_SCALE[name]
        km, kv = _jax.random.split(_jax.random.fold_in(key, i + 1))
        out[name] = w
        out["m_" + name] = s * _jax.random.normal(km, w.shape, _jnp.float32)
        out["v_" + name] = (s * s) * _jax.random.uniform(kv, w.shape, _jnp.float32, 0.5, 1.5)
    if N_MICROBATCH > 1:
        for name, axis in PER_EXAMPLE_BATCH_AXIS.items():
            out[name] = _to_microbatches(out[name], axis)
    return {'x': out['x'], 'p': out['p'], 'g_mix': out['g_mix'], 'w_in': out['w_in'], 'w_dw': out['w_dw'], 'b_dw': out['b_dw'], 'g_conv_ln': out['g_conv_ln'], 'b_conv_ln': out['b_conv_ln'], 'w_out': out['w_out'], 'g_ffn': out['g_ffn'], 'w_gate': out['w_gate'], 'w_up': out['w_up'], 'w_down': out['w_down'], 'g_ple': out['g_ple'], 'w_pgate': out['w_pgate'], 'b_pgate': out['b_pgate'], 'w_ple': out['w_ple'], 'g_final': out['g_final'], 'loss_target': out['loss_target'], 'm_g_mix': out['m_g_mix'], 'm_w_in': out['m_w_in'], 'm_w_dw': out['m_w_dw'], 'm_b_dw': out['m_b_dw'], 'm_g_conv_ln': out['m_g_conv_ln'], 'm_b_conv_ln': out['m_b_conv_ln'], 'm_w_out': out['m_w_out'], 'm_g_ffn': out['m_g_ffn'], 'm_w_gate': out['m_w_gate'], 'm_w_up': out['m_w_up'], 'm_w_down': out['m_w_down'], 'm_g_ple': out['m_g_ple'], 'm_w_pgate': out['m_w_pgate'], 'm_b_pgate': out['m_b_pgate'], 'm_w_ple': out['m_w_ple'], 'm_g_final': out['m_g_final'], 'v_g_mix': out['v_g_mix'], 'v_w_in': out['v_w_in'], 'v_w_dw': out['v_w_dw'], 'v_b_dw': out['v_b_dw'], 'v_g_conv_ln': out['v_g_conv_ln'], 'v_b_conv_ln': out['v_b_conv_ln'], 'v_w_out': out['v_w_out'], 'v_g_ffn': out['v_g_ffn'], 'v_w_gate': out['v_w_gate'], 'v_w_up': out['v_w_up'], 'v_w_down': out['v_w_down'], 'v_g_ple': out['v_g_ple'], 'v_w_pgate': out['v_w_pgate'], 'v_b_pgate': out['v_b_pgate'], 'v_w_ple': out['v_w_ple'], 'v_g_final': out['v_g_final']}


def _loss(weights, diff, rest, loss_target):
    with _jax.named_scope("forward"):
        args = {**rest, TWIN_DIFF_INPUT: diff, **{k: w.astype(_WEIGHT_DTYPES[k]) for k, w in weights.items()}}
        y = _forward(args)
    with _jax.named_scope("loss_head"):
        err = _jnp.square(y.astype(_jnp.float32) - loss_target)
        return 0.5 * _jnp.sum(_jnp.mean(err, axis=-1)) if err.ndim else 0.5 * err


def _adamw(w, g, m, v):
    m = ADAM_B1 * m + (1.0 - ADAM_B1) * g
    v = ADAM_B2 * v + (1.0 - ADAM_B2) * _jnp.square(g)
    m_hat = m / (1.0 - ADAM_B1 ** ADAM_STEP)
    v_hat = v / (1.0 - ADAM_B2 ** ADAM_STEP)
    delta = -ADAM_LR * (m_hat / (_jnp.sqrt(v_hat) + ADAM_EPS) + ADAM_WD * w)
    return delta, m, v


def reference(x, p, g_mix, w_in, w_dw, b_dw, g_conv_ln, b_conv_ln, w_out, g_ffn, w_gate, w_up, w_down, g_ple, w_pgate, b_pgate, w_ple, g_final, loss_target, m_g_mix, m_w_in, m_w_dw, m_b_dw, m_g_conv_ln, m_b_conv_ln, m_w_out, m_g_ffn, m_w_gate, m_w_up, m_w_down, m_g_ple, m_w_pgate, m_b_pgate, m_w_ple, m_g_final, v_g_mix, v_w_in, v_w_dw, v_b_dw, v_g_conv_ln, v_b_conv_ln, v_w_out, v_g_ffn, v_w_gate, v_w_up, v_w_down, v_g_ple, v_w_pgate, v_b_pgate, v_w_ple, v_g_final):
    given = dict(x=x, p=p, g_mix=g_mix, w_in=w_in, w_dw=w_dw, b_dw=b_dw, g_conv_ln=g_conv_ln, b_conv_ln=b_conv_ln, w_out=w_out, g_ffn=g_ffn, w_gate=w_gate, w_up=w_up, w_down=w_down, g_ple=g_ple, w_pgate=w_pgate, b_pgate=b_pgate, w_ple=w_ple, g_final=g_final, loss_target=loss_target, m_g_mix=m_g_mix, m_w_in=m_w_in, m_w_dw=m_w_dw, m_b_dw=m_b_dw, m_g_conv_ln=m_g_conv_ln, m_b_conv_ln=m_b_conv_ln, m_w_out=m_w_out, m_g_ffn=m_g_ffn, m_w_gate=m_w_gate, m_w_up=m_w_up, m_w_down=m_w_down, m_g_ple=m_g_ple, m_w_pgate=m_w_pgate, m_b_pgate=m_b_pgate, m_w_ple=m_w_ple, m_g_final=m_g_final, v_g_mix=v_g_mix, v_w_in=v_w_in, v_w_dw=v_w_dw, v_b_dw=v_b_dw, v_g_conv_ln=v_g_conv_ln, v_b_conv_ln=v_b_conv_ln, v_w_out=v_w_out, v_g_ffn=v_g_ffn, v_w_gate=v_w_gate, v_w_up=v_w_up, v_w_down=v_w_down, v_g_ple=v_g_ple, v_w_pgate=v_w_pgate, v_b_pgate=v_b_pgate, v_w_ple=v_w_ple, v_g_final=v_g_final)
    weights = {n: given[n] for n in TWIN_WEIGHTS}
    shared = {n: given[n] for n in SHARED_INPUTS}
    per_example = {n: given[n] for n in ['x', 'p']}
    grad_fn = _jax.value_and_grad(_loss, argnums=(0, 1))

    def one_microbatch(ex, loss_target):
        ex = dict(ex)
        diff = ex.pop(TWIN_DIFF_INPUT)
        return grad_fn(weights, diff, {**shared, **ex}, loss_target)

    if N_MICROBATCH == 1:
        loss, (grad_w, grad_x) = one_microbatch(per_example, given["loss_target"])
    else:
        def body(carry, xs):
            loss_sum, grad_sum = carry
            l_k, (gw_k, gx_k) = one_microbatch(xs[0], xs[1])
            with _jax.named_scope("update"):
                return (loss_sum + l_k, _jax.tree.map(_jnp.add, grad_sum, gw_k)), gx_k

        init = (_jnp.zeros((), _jnp.float32), _jax.tree.map(_jnp.zeros_like, weights))
        (loss, grad_w), grad_x = _jax.lax.scan(body, init, (per_example, given["loss_target"]))
    with _jax.named_scope("update"):
        delta_w, new_m, new_v = {}, {}, {}
        for n in TWIN_WEIGHTS:
            delta_w[n], new_m[n], new_v[n] = _adamw(weights[n], grad_w[n], given["m_" + n], given["v_" + n])
    return (loss, grad_x, *[grad_w[n] for n in TWIN_WEIGHTS], *[delta_w[n] for n in TWIN_WEIGHTS],
            *[new_m[n] for n in TWIN_WEIGHTS], *[new_v[n] for n in TWIN_WEIGHTS])
```

```python
import jax
import jax.numpy as jnp
from jax import lax
from jax.experimental import pallas as pl
from jax.experimental.pallas import tpu as pltpu

F32 = jnp.float32
BF16 = jnp.bfloat16

NDEV = 8
D = 2048
NH = 8
DH = 128
DA = NH * DH
DC = D - DA
DIN = 3 * DA + 2 * DC
DFF = 5632
DPLE = 256
BLK = 128
DILATIONS = (1, 4, 16)
CW = 31
EPS = 1e-6
N_IN = DIN // NDEV
N_FF = DFF // NDEV
NEG = -1e30

ADAM_LR = 0.001
ADAM_B1 = 0.9
ADAM_B2 = 0.999
ADAM_EPS = 1e-08
ADAM_WD = 0.01
ADAM_STEP = 10

VMEM_CAP_V7X = 64 * 1024 * 1024
VMEM_BIG = VMEM_CAP_V7X - 12 * 1024 * 1024
VMEM_MID = 40 * 1024 * 1024

SMALL_W = 2048
SMALL_ROWS = 40


def _sigmoid(v):
    return 1.0 / (1.0 + jnp.exp(-v))


def _dot(a, b, contract):
    return lax.dot_general(a, b, (contract, ((), ())), preferred_element_type=F32)


NN = ((1,), (0,))
NT = ((1,), (1,))
TN = ((0,), (0,))


def _exchange(name, items):
    n = len(items)
    out_shape = [
        jax.ShapeDtypeStruct((NDEV,) + (a.shape[1:] if sc else a.shape), a.dtype)
        for a, sc in items
    ]
    scat = [sc for _, sc in items]

    def body(*refs):
        srcs = refs[:n]
        dsts = refs[n:2 * n]
        send_sems, recv_sems, loc_sems = refs[2 * n:]
        x = lax.axis_index("x")
        y = lax.axis_index("y")
        c = lax.axis_index("c")
        me = 4 * x + 2 * y + c

        local = []
        for i in range(n):
            src = srcs[i].at[me] if scat[i] else srcs[i]
            cp = pltpu.make_async_copy(src, dsts[i].at[me], loc_sems.at[i])
            cp.start()
            local.append(cp)

        remote = []
        for k in range(1, NDEV):
            px = (1 - x) if (k >> 2) & 1 else x
            py = (1 - y) if (k >> 1) & 1 else y
            pc = (1 - c) if k & 1 else c
            peer = 4 * px + 2 * py + pc
            for i in range(n):
                sem = i * (NDEV - 1) + k - 1
                src = srcs[i].at[peer] if scat[i] else srcs[i]
                send = pltpu.make_async_remote_copy(
                    src_ref=src, dst_ref=dsts[i].at[me],
                    send_sem=send_sems.at[sem], recv_sem=recv_sems.at[sem],
                    device_id=(px, py, pc), device_id_type=pl.DeviceIdType.MESH)
                send.start()
                recv = pltpu.make_async_remote_copy(
                    src_ref=src, dst_ref=dsts[i].at[peer],
                    send_sem=send_sems.at[sem], recv_sem=recv_sems.at[sem],
                    device_id=(px, py, pc), device_id_type=pl.DeviceIdType.MESH)
                remote.append((send, recv))
        for send, recv in remote:
            recv.wait_recv()
            send.wait_send()
        for cp in local:
            cp.wait()

    any_spec = pl.BlockSpec(memory_space=pl.ANY)
    return pl.pallas_call(
        body, name=name,
        in_specs=[any_spec] * n, out_specs=[any_spec] * n, out_shape=out_shape,
        scratch_shapes=[
            pltpu.SemaphoreType.DMA((n * (NDEV - 1),)),
            pltpu.SemaphoreType.DMA((n * (NDEV - 1),)),
            pltpu.SemaphoreType.DMA((n,)),
        ],
    )(*[a for a, _ in items])


def _mm(name, grid, in_specs, operands, out_specs, out_shape, contract, n_pairs, epilogue,
        acc_shape=None, vmem=VMEM_BIG):
    nk = grid[2]
    n_extra = len(operands) - 2 * n_pairs
    n_out = len(out_shape)

    def body(*refs):
        ab = refs[:2 * n_pairs]
        extras = refs[2 * n_pairs:2 * n_pairs + n_extra]
        outs = refs[2 * n_pairs + n_extra:2 * n_pairs + n_extra + n_out]
        part = None
        for p in range(n_pairs):
            d = _dot(ab[2 * p][...], ab[2 * p + 1][...], contract)
            part = d if part is None else part + d
        if nk == 1:
            epilogue(part, extras, outs)
        else:
            acc_ref = refs[-1]
            k = pl.program_id(2)

            @pl.when(k == 0)
            def _():
                acc_ref[...] = part

            @pl.when(k > 0)
            def _():
                acc_ref[...] += part

            @pl.when(k == nk - 1)
            def _():
                epilogue(acc_ref[...], extras, outs)

    scratch = [pltpu.VMEM(acc_shape, F32)] if nk > 1 else []
    return pl.pallas_call(
        body, name=name, grid=grid, in_specs=in_specs, out_specs=out_specs, out_shape=out_shape,
        scratch_shapes=scratch,
        compiler_params=pltpu.CompilerParams(
            dimension_semantics=("parallel", "parallel", "arbitrary"), vmem_limit_bytes=vmem),
    )(*operands)


def _ep_cast(dtype):
    def ep(acc, extras, outs):
        outs[0][...] = acc.astype(dtype)
    return ep


def _ep_resid(acc, extras, outs):
    outs[0][...] = extras[0][...] + acc


def _ep_swiglu_bwd(acc, extras, outs):
    g = extras[0][...].astype(F32)
    u = extras[1][...].astype(F32)
    sg = _sigmoid(g)
    outs[0][...] = (acc * u * (sg * (1.0 + g * (1.0 - sg)))).astype(BF16)
    outs[1][...] = (acc * (g * sg)).astype(BF16)


def _row_tile(T):
    return min(1024, T)


def _mm_in(a, w_in):
    T = a.shape[0]
    tm = _row_tile(T)
    return _mm(
        "mm_in", (T // tm, NDEV, 1),
        [pl.BlockSpec((tm, D), lambda i, j, k: (i, 0)),
         pl.BlockSpec((None, D, N_IN), lambda i, j, k: (j, 0, 0))],
        [a, w_in],
        [pl.BlockSpec((tm, N_IN), lambda i, j, k: (i, j))],
        [jax.ShapeDtypeStruct((T, DIN), BF16)], NN, 1, _ep_cast(BF16))[0]


def _mm_out(o_attn, o_conv, w_out, x):
    T = x.shape[0]
    tm = _row_tile(T)
    tn = 1024
    return _mm(
        "mm_out", (T // tm, D // tn, 1),
        [pl.BlockSpec((tm, DA), lambda i, j, k: (i, 0)),
         pl.BlockSpec((DA, tn), lambda i, j, k: (0, j)),
         pl.BlockSpec((tm, DC), lambda i, j, k: (i, 0)),
         pl.BlockSpec((DC, tn), lambda i, j, k: (1, j)),
         pl.BlockSpec((tm, tn), lambda i, j, k: (i, j))],
        [o_attn, w_out, o_conv, w_out, x],
        [pl.BlockSpec((tm, tn), lambda i, j, k: (i, j))],
        [jax.ShapeDtypeStruct((T, D), F32)], NN, 2, _ep_resid)[0]


def _mm_gate_up(f, w_gate, w_up):
    T = f.shape[0]
    tm = _row_tile(T)

    def body(f_ref, wg_ref, wu_ref, g_ref, u_ref, a_ref):
        fv = f_ref[...]
        g = _dot(fv, wg_ref[...], NN)
        u = _dot(fv, wu_ref[...], NN)
        g_ref[...] = g.astype(BF16)
        u_ref[...] = u.astype(BF16)
        a_ref[...] = (g * _sigmoid(g) * u).astype(BF16)

    wspec = pl.BlockSpec((None, D, N_FF), lambda i, j: (j, 0, 0))
    ospec = pl.BlockSpec((None, tm, N_FF), lambda i, j: (j, i, 0))
    sh = jax.ShapeDtypeStruct((NDEV, T, N_FF), BF16)
    return pl.pallas_call(
        body, name="mm_gate_up", grid=(T // tm, NDEV),
        in_specs=[pl.BlockSpec((tm, D), lambda i, j: (i, 0)), wspec, wspec],
        out_specs=[ospec, ospec, ospec], out_shape=[sh, sh, sh],
        compiler_params=pltpu.CompilerParams(
            dimension_semantics=("parallel", "parallel"), vmem_limit_bytes=VMEM_BIG),
    )(f, w_gate, w_up)


def _mm_down(act, w_down, h1):
    T = h1.shape[0]
    tm = _row_tile(T)
    tn = 1024
    return _mm(
        "mm_down", (T // tm, D // tn, NDEV),
        [pl.BlockSpec((None, tm, N_FF), lambda i, j, k: (k, i, 0)),
         pl.BlockSpec((None, N_FF, tn), lambda i, j, k: (k, 0, j)),
         pl.BlockSpec((tm, tn), lambda i, j, k: (i, j))],
        [act, w_down, h1],
        [pl.BlockSpec((tm, tn), lambda i, j, k: (i, j))],
        [jax.ShapeDtypeStruct((T, D), F32)], NN, 1, _ep_resid, acc_shape=(tm, tn))[0]


def _mm_ple(r, w_pgate, b_pgate, p, w_ple, h2):
    T = h2.shape[0]
    tm = _row_tile(T)
    tn = 1024

    def body(r_ref, wg_ref, b_ref, p_ref, wp_ref, h2_ref, gte_ref, pe_ref, h3_ref):
        gte = _sigmoid(_dot(r_ref[...], wg_ref[...], NN) + b_ref[...])
        pe = _dot(p_ref[...], wp_ref[...], NN)
        gte_ref[...] = gte.astype(BF16)
        pe_ref[...] = pe.astype(BF16)
        h3_ref[...] = h2_ref[...] + pe * gte

    tile = pl.BlockSpec((tm, tn), lambda i, j: (i, j))
    return pl.pallas_call(
        body, name="mm_ple", grid=(T // tm, D // tn),
        in_specs=[pl.BlockSpec((tm, D), lambda i, j: (i, 0)),
                  pl.BlockSpec((D, tn), lambda i, j: (0, j)),
                  pl.BlockSpec((1, tn), lambda i, j: (0, j)),
                  pl.BlockSpec((tm, DPLE), lambda i, j: (i, 0)),
                  pl.BlockSpec((DPLE, tn), lambda i, j: (0, j)),
                  tile],
        out_specs=[tile, tile, tile],
        out_shape=[jax.ShapeDtypeStruct((T, D), BF16), jax.ShapeDtypeStruct((T, D), BF16),
                   jax.ShapeDtypeStruct((T, D), F32)],
        compiler_params=pltpu.CompilerParams(
            dimension_semantics=("parallel", "parallel"), vmem_limit_bytes=VMEM_BIG),
    )(r, w_pgate, b_pgate, p, w_ple, h2)


def _mm_nt(name, dy, w):
    T, n = dy.shape
    kdim = w.shape[0]
    tm = _row_tile(T)
    tn = 1024
    return _mm(
        name, (T // tm, kdim // tn, 1),
        [pl.BlockSpec((tm, n), lambda i, j, k: (i, 0)),
         pl.BlockSpec((tn, n), lambda i, j, k: (j, 0))],
        [dy, w],
        [pl.BlockSpec((tm, tn), lambda i, j, k: (i, j))],
        [jax.ShapeDtypeStruct((T, kdim), BF16)], NT, 1, _ep_cast(BF16))[0]


def _mm_down_bwd(dh2, w_down, g, u):
    T = dh2.shape[0]
    tm = _row_tile(T)
    gspec = pl.BlockSpec((None, tm, N_FF), lambda i, j, k: (j, i, 0))
    sh = jax.ShapeDtypeStruct((NDEV, T, N_FF), BF16)
    return _mm(
        "mm_down_bwd", (T // tm, NDEV, 1),
        [pl.BlockSpec((tm, D), lambda i, j, k: (i, 0)),
         pl.BlockSpec((None, N_FF, D), lambda i, j, k: (j, 0, 0)),
         gspec, gspec],
        [dh2, w_down, g, u],
        [gspec, gspec], [sh, sh], NT, 1, _ep_swiglu_bwd)


def _mm_ffn_in_bwd(dg, w_gate, du, w_up):
    T = dg.shape[1]
    tm = _row_tile(T)
    tn = 1024
    aspec = pl.BlockSpec((None, tm, N_FF), lambda i, j, k: (k, i, 0))
    wspec = pl.BlockSpec((None, tn, N_FF), lambda i, j, k: (k, j, 0))
    return _mm(
        "mm_ffn_in_bwd", (T // tm, D // tn, NDEV),
        [aspec, wspec, aspec, wspec], [dg, w_gate, du, w_up],
        [pl.BlockSpec((tm, tn), lambda i, j, k: (i, j))],
        [jax.ShapeDtypeStruct((T, D), BF16)], NT, 2, _ep_cast(BF16), acc_shape=(tm, tn))[0]


def _mm_in_bwd(dz, w_in):
    T = dz.shape[0]
    tm = _row_tile(T)
    tn = 1024
    return _mm(
        "mm_in_bwd", (T // tm, D // tn, NDEV),
        [pl.BlockSpec((tm, N_IN), lambda i, j, k: (i, k)),
         pl.BlockSpec((None, tn, N_IN), lambda i, j, k: (k, j, 0))],
        [dz, w_in],
        [pl.BlockSpec((tm, tn), lambda i, j, k: (i, j))],
        [jax.ShapeDtypeStruct((T, D), BF16)], NT, 1, _ep_cast(BF16), acc_shape=(tm, tn))[0]


def _mm_tn(name, a, b, tj=None):
    T, idim = a.shape
    jdim = b.shape[1]
    tt = _row_tile(T)
    ti = min(idim, 1024)
    tj = jdim if tj is None else tj
    return _mm(
        name, (idim // ti, jdim // tj, T // tt),
        [pl.BlockSpec((tt, ti), lambda i, j, k: (k, i)),
         pl.BlockSpec((tt, tj), lambda i, j, k: (k, j))],
        [a, b],
        [pl.BlockSpec((ti, tj), lambda i, j, k: (i, j))],
        [jax.ShapeDtypeStruct((idim, jdim), BF16)], TN, 1, _ep_cast(BF16), acc_shape=(ti, tj))[0]


def _mm_tn_cols(name, a, b, ncol):
    T, idim = a.shape
    tt = _row_tile(T)
    return _mm(
        name, (1, NDEV, T // tt),
        [pl.BlockSpec((tt, idim), lambda i, j, k: (k, 0)),
         pl.BlockSpec((tt, ncol), lambda i, j, k: (k, j))],
        [a, b],
        [pl.BlockSpec((None, idim, ncol), lambda i, j, k: (j, 0, 0))],
        [jax.ShapeDtypeStruct((NDEV, idim, ncol), BF16)], TN, 1, _ep_cast(BF16),
        acc_shape=(idim, ncol))[0]


def _mm_tn_ff_cols(name, a, b):
    T = a.shape[0]
    tt = _row_tile(T)
    return _mm(
        name, (1, NDEV, T // tt),
        [pl.BlockSpec((tt, D), lambda i, j, k: (k, 0)),
         pl.BlockSpec((None, tt, N_FF), lambda i, j, k: (j, k, 0))],
        [a, b],
        [pl.BlockSpec((None, D, N_FF), lambda i, j, k: (j, 0, 0))],
        [jax.ShapeDtypeStruct((NDEV, D, N_FF), BF16)], TN, 1, _ep_cast(BF16),
        acc_shape=(D, N_FF))[0]


def _mm_tn_ff_rows(name, a, b):
    T = b.shape[0]
    tt = _row_tile(T)
    return _mm(
        name, (NDEV, 1, T // tt),
        [pl.BlockSpec((None, tt, N_FF), lambda i, j, k: (i, k, 0)),
         pl.BlockSpec((tt, D), lambda i, j, k: (k, 0))],
        [a, b],
        [pl.BlockSpec((None, N_FF, D), lambda i, j, k: (i, 0, 0))],
        [jax.ShapeDtypeStruct((NDEV, N_FF, D), BF16)], TN, 1, _ep_cast(BF16),
        acc_shape=(N_FF, D))[0]


TR = 256


def _rows(T):
    return min(TR, T)


def _rms_fwd(name, h, g):
    T = h.shape[0]
    tr = _rows(T)

    def body(h_ref, g_ref, o_ref):
        v = h_ref[...]
        r = lax.rsqrt(jnp.mean(v * v, axis=-1, keepdims=True) + EPS)
        o_ref[...] = (v * r * g_ref[...]).astype(BF16)

    return pl.pallas_call(
        body, name=name, grid=(T // tr,),
        in_specs=[pl.BlockSpec((tr, D), lambda i: (i, 0)), pl.BlockSpec((1, D), lambda i: (0, 0))],
        out_specs=pl.BlockSpec((tr, D), lambda i: (i, 0)),
        out_shape=jax.ShapeDtypeStruct((T, D), BF16),
        compiler_params=pltpu.CompilerParams(dimension_semantics=("parallel",)),
    )(h, g)


def _fold8(v):
    return jnp.sum(v.reshape(v.shape[0] // 8, 8, v.shape[1]), axis=0)


def _rms_bwd(name, dn_out, h, g, dres, want_bf16):
    T = h.shape[0]
    tr = _rows(T)
    nt = T // tr

    def body(dy_ref, h_ref, g_ref, dres_ref, *rest):
        if want_bf16:
            dh_ref, dhb_ref, dg_ref, acc = rest
        else:
            dh_ref, dg_ref, acc = rest
        i = pl.program_id(0)
        v = h_ref[...]
        r = lax.rsqrt(jnp.mean(v * v, axis=-1, keepdims=True) + EPS)
        nrm = v * r
        dy = dy_ref[...].astype(F32)
        dn = dy * g_ref[...]
        dh = dres_ref[...] + r * (dn - nrm * jnp.mean(dn * nrm, axis=-1, keepdims=True))
        dh_ref[...] = dh
        if want_bf16:
            dhb_ref[...] = dh.astype(BF16)

        @pl.when(i == 0)
        def _():
            acc[...] = jnp.zeros_like(acc)

        acc[...] += _fold8(dy * nrm)

        @pl.when(i == nt - 1)
        def _():
            dg_ref[...] = jnp.sum(acc[...], axis=0, keepdims=True)

    tile = pl.BlockSpec((tr, D), lambda i: (i, 0))
    vec = pl.BlockSpec((1, D), lambda i: (0, 0))
    out_specs = [tile] + ([tile] if want_bf16 else []) + [vec]
    out_shape = ([jax.ShapeDtypeStruct((T, D), F32)]
                 + ([jax.ShapeDtypeStruct((T, D), BF16)] if want_bf16 else [])
                 + [jax.ShapeDtypeStruct((1, D), F32)])
    return pl.pallas_call(
        body, name=name, grid=(nt,),
        in_specs=[tile, tile, vec, tile], out_specs=out_specs, out_shape=out_shape,
        scratch_shapes=[pltpu.VMEM((8, D), F32)],
        compiler_params=pltpu.CompilerParams(dimension_semantics=("arbitrary",)),
    )(dn_out, h, g, dres)


def _loss_bwd(h3, target, g_final, pe, gte):
    T = h3.shape[0]
    tr = _rows(T)
    nt = T // tr

    def body(h_ref, t_ref, g_ref, pe_ref, gte_ref, loss_ref, dh_ref, dpe_ref, dpg_ref,
             dgf_ref, dbp_ref, lacc, gacc, bacc):
        i = pl.program_id(0)
        v = h_ref[...]
        r = lax.rsqrt(jnp.mean(v * v, axis=-1, keepdims=True) + EPS)
        nrm = v * r
        g = g_ref[...]
        err = nrm * g - t_ref[...]
        dy = err * (1.0 / D)
        dn = dy * g
        dh = r * (dn - nrm * jnp.mean(dn * nrm, axis=-1, keepdims=True))
        dh_ref[...] = dh
        gte = gte_ref[...].astype(F32)
        pe = pe_ref[...].astype(F32)
        dpe_ref[...] = (dh * gte).astype(BF16)
        dpg = dh * pe * gte * (1.0 - gte)
        dpg_ref[...] = dpg.astype(BF16)

        @pl.when(i == 0)
        def _():
            lacc[...] = jnp.zeros_like(lacc)
            gacc[...] = jnp.zeros_like(gacc)
            bacc[...] = jnp.zeros_like(bacc)

        lacc[...] += _fold8(err * err)
        gacc[...] += _fold8(dy * nrm)
        bacc[...] += _fold8(dpg)

        @pl.when(i == nt - 1)
        def _():
            tot = jnp.sum(jnp.sum(lacc[...], axis=0, keepdims=True), axis=1, keepdims=True)
            loss_ref[...] = jnp.broadcast_to(tot * (0.5 / D), (1, 128))
            dgf_ref[...] = jnp.sum(gacc[...], axis=0, keepdims=True)
            dbp_ref[...] = jnp.sum(bacc[...], axis=0, keepdims=True)

    tile = pl.BlockSpec((tr, D), lambda i: (i, 0))
    vec = pl.BlockSpec((1, D), lambda i: (0, 0))
    return pl.pallas_call(
        body, name="loss_bwd", grid=(nt,),
        in_specs=[tile, tile, vec, tile, tile],
        out_specs=[pl.BlockSpec((1, 128), lambda i: (0, 0)), tile, tile, tile, vec, vec],
        out_shape=[jax.ShapeDtypeStruct((1, 128), F32), jax.ShapeDtypeStruct((T, D), F32),
                   jax.ShapeDtypeStruct((T, D), BF16), jax.ShapeDtypeStruct((T, D), BF16),
                   jax.ShapeDtypeStruct((1, D), F32), jax.ShapeDtypeStruct((1, D), F32)],
        scratch_shapes=[pltpu.VMEM((8, D), F32)] * 3,
        compiler_params=pltpu.CompilerParams(dimension_semantics=("arbitrary",)),
    )(h3, target, g_final, pe, gte)


def _band_masks():
    qi = lax.broadcasted_iota(jnp.int32, (BLK, BLK), 0)
    kj = lax.broadcasted_iota(jnp.int32, (BLK, BLK), 1)
    return kj >= qi, kj <= qi


def _attn_fwd(name, z, d):
    T = z.shape[0]
    tv = T // d
    nb = tv // BLK
    zv = z.reshape(tv, d * DIN)
    scale = DH ** -0.5

    def body(q_ref, kp_ref, kc_ref, vp_ref, vc_ref, o_ref, l_ref):
        n = pl.program_id(1)
        prev_ok, cur_ok = _band_masks()
        prev_ok = prev_ok & (n > 0)
        for h in range(NH):
            sl = slice(h * DH, (h + 1) * DH)
            q = q_ref[:, sl]
            sp = jnp.where(prev_ok, _dot(q, kp_ref[:, sl], NT) * scale, NEG)
            sc = jnp.where(cur_ok, _dot(q, kc_ref[:, sl], NT) * scale, NEG)
            m = jnp.maximum(jnp.max(sp, axis=1, keepdims=True), jnp.max(sc, axis=1, keepdims=True))
            pp = jnp.exp(sp - m)
            pc = jnp.exp(sc - m)
            den = jnp.sum(pp, axis=1, keepdims=True) + jnp.sum(pc, axis=1, keepdims=True)
            o = _dot(pp.astype(BF16), vp_ref[:, sl], NN) + _dot(pc.astype(BF16), vc_ref[:, sl], NN)
            o_ref[:, sl] = (o / den).astype(BF16)
            l_ref[:, sl] = jnp.broadcast_to(m + jnp.log(den), (BLK, DH))

    blk = (BLK, DA)
    prev = lambda n: jnp.maximum(n - 1, 0)
    o, l = pl.pallas_call(
        body, name=name, grid=(d, nb),
        in_specs=[pl.BlockSpec(blk, lambda r, n: (n, 5 * r)),
                  pl.BlockSpec(blk, lambda r, n: (prev(n), 5 * r + 1)),
                  pl.BlockSpec(blk, lambda r, n: (n, 5 * r + 1)),
                  pl.BlockSpec(blk, lambda r, n: (prev(n), 5 * r + 2)),
                  pl.BlockSpec(blk, lambda r, n: (n, 5 * r + 2))],
        out_specs=[pl.BlockSpec(blk, lambda r, n: (n, r))] * 2,
        out_shape=[jax.ShapeDtypeStruct((tv, d * DA), BF16), jax.ShapeDtypeStruct((tv, d * DA), F32)],
        compiler_params=pltpu.CompilerParams(dimension_semantics=("parallel", "parallel")),
    )(zv, zv, zv, zv, zv)
    return o.reshape(T, DA), l.reshape(T, DA)


def _attn_combine(outs, lses):
    T = outs[0].shape[0]
    tr = _rows(T)

    def body(o1, o2, o3, l1, l2, l3, o_ref, l_ref):
        a, b, c = l1[...], l2[...], l3[...]
        m = jnp.maximum(jnp.maximum(a, b), c)
        ea, eb, ec = jnp.exp(a - m), jnp.exp(b - m), jnp.exp(c - m)
        s = ea + eb + ec
        o = (ea * o1[...].astype(F32) + eb * o2[...].astype(F32) + ec * o3[...].astype(F32)) / s
        o_ref[...] = o.astype(BF16)
        l_ref[...] = m + jnp.log(s)

    tile = pl.BlockSpec((tr, DA), lambda i: (i, 0))
    return pl.pallas_call(
        body, name="attn_combine", grid=(T // tr,),
        in_specs=[tile] * 6, out_specs=[tile, tile],
        out_shape=[jax.ShapeDtypeStruct((T, DA), BF16), jax.ShapeDtypeStruct((T, DA), F32)],
        compiler_params=pltpu.CompilerParams(dimension_semantics=("parallel",)),
    )(*outs, *lses)


def _attn_bwd_q(name, z, dom, o, lse, d, prev_dq, out_dtype):
    T = z.shape[0]
    tv = T // d
    nb = tv // BLK
    zv = z.reshape(tv, d * DIN)
    dov = dom.reshape(tv, d * D)
    ov = o.reshape(tv, d * DA)
    lv = lse.reshape(tv, d * DA)
    scale = DH ** -0.5
    has_prev = prev_dq is not None

    def body(q_ref, kp_ref, kc_ref, vp_ref, vc_ref, do_ref, o_ref, l_ref, *rest):
        if has_prev:
            acc_ref, dq_ref = rest
        else:
            (dq_ref,) = rest
        n = pl.program_id(1)
        prev_ok, cur_ok = _band_masks()
        prev_ok = prev_ok & (n > 0)
        for h in range(NH):
            sl = slice(h * DH, (h + 1) * DH)
            q = q_ref[:, sl]
            kp, kc = kp_ref[:, sl], kc_ref[:, sl]
            do = do_ref[:, sl]
            lrow = l_ref[:, sl]
            pp = jnp.exp(jnp.where(prev_ok, _dot(q, kp, NT) * scale - lrow, NEG))
            pc = jnp.exp(jnp.where(cur_ok, _dot(q, kc, NT) * scale - lrow, NEG))
            delta = jnp.sum(do.astype(F32) * o_ref[:, sl].astype(F32), axis=1, keepdims=True)
            dsp = (pp * (_dot(do, vp_ref[:, sl], NT) - delta) * scale).astype(BF16)
            dsc = (pc * (_dot(do, vc_ref[:, sl], NT) - delta) * scale).astype(BF16)
            dq = _dot(dsp, kp, NN) + _dot(dsc, kc, NN)
            if has_prev:
                dq = dq + acc_ref[:, sl]
            dq_ref[:, sl] = dq.astype(out_dtype)

    blk = (BLK, DA)
    prev = lambda n: jnp.maximum(n - 1, 0)
    own = pl.BlockSpec(blk, lambda r, n: (n, r))
    in_specs = [pl.BlockSpec(blk, lambda r, n: (n, 5 * r)),
                pl.BlockSpec(blk, lambda r, n: (prev(n), 5 * r + 1)),
                pl.BlockSpec(blk, lambda r, n: (n, 5 * r + 1)),
                pl.BlockSpec(blk, lambda r, n: (prev(n), 5 * r + 2)),
                pl.BlockSpec(blk, lambda r, n: (n, 5 * r + 2)),
                pl.BlockSpec(blk, lambda r, n: (n, 2 * r)),
                own, own]
    operands = [zv, zv, zv, zv, zv, dov, ov, lv]
    if has_prev:
        in_specs.append(own)
        operands.append(prev_dq.reshape(tv, d * DA))
    dq = pl.pallas_call(
        body, name=name, grid=(d, nb), in_specs=in_specs, out_specs=own,
        out_shape=jax.ShapeDtypeStruct((tv, d * DA), out_dtype),
        compiler_params=pltpu.CompilerParams(dimension_semantics=("parallel", "parallel")),
    )(*operands)
    return dq.reshape(T, DA)


def _attn_bwd_kv(name, z, dom, o, lse, d, prev_dk, prev_dv, out_dtype):
    T = z.shape[0]
    tv = T // d
    nb = tv // BLK
    zv = z.reshape(tv, d * DIN)
    dov = dom.reshape(tv, d * D)
    ov = o.reshape(tv, d * DA)
    lv = lse.reshape(tv, d * DA)
    scale = DH ** -0.5
    has_prev = prev_dk is not None

    def body(k_ref, v_ref, qa_ref, qb_ref, doa_ref, dob_ref, oa_ref, ob_ref, la_ref, lb_ref, *rest):
        if has_prev:
            pk_ref, pv_ref, dk_ref, dv_ref = rest
        else:
            dk_ref, dv_ref = rest
        j = pl.program_id(1)
        next_ok, own_ok = _band_masks()
        next_ok = next_ok & (j < nb - 1)
        for h in range(NH):
            sl = slice(h * DH, (h + 1) * DH)
            k, v = k_ref[:, sl], v_ref[:, sl]
            qa, qb = qa_ref[:, sl], qb_ref[:, sl]
            doa, dob = doa_ref[:, sl], dob_ref[:, sl]
            pa = jnp.exp(jnp.where(own_ok, _dot(qa, k, NT) * scale - la_ref[:, sl], NEG))
            pb = jnp.exp(jnp.where(next_ok, _dot(qb, k, NT) * scale - lb_ref[:, sl], NEG))
            da = jnp.sum(doa.astype(F32) * oa_ref[:, sl].astype(F32), axis=1, keepdims=True)
            db = jnp.sum(dob.astype(F32) * ob_ref[:, sl].astype(F32), axis=1, keepdims=True)
            dv = _dot(pa.astype(BF16), doa, TN) + _dot(pb.astype(BF16), dob, TN)
            dsa = (pa * (_dot(doa, v, NT) - da) * scale).astype(BF16)
            dsb = (pb * (_dot(dob, v, NT) - db) * scale).astype(BF16)
            dk = _dot(dsa, qa, TN) + _dot(dsb, qb, TN)
            if has_prev:
                dk = dk + pk_ref[:, sl]
                dv = dv + pv_ref[:, sl]
            dk_ref[:, sl] = dk.astype(out_dtype)
            dv_ref[:, sl] = dv.astype(out_dtype)

    blk = (BLK, DA)
    nxt = lambda j: jnp.minimum(j + 1, nb - 1)
    own = pl.BlockSpec(blk, lambda r, j: (j, r))
    own_n = pl.BlockSpec(blk, lambda r, j: (nxt(j), r))
    in_specs = [pl.BlockSpec(blk, lambda r, j: (j, 5 * r + 1)),
                pl.BlockSpec(blk, lambda r, j: (j, 5 * r + 2)),
                pl.BlockSpec(blk, lambda r, j: (j, 5 * r)),
                pl.BlockSpec(blk, lambda r, j: (nxt(j), 5 * r)),
                pl.BlockSpec(blk, lambda r, j: (j, 2 * r)),
                pl.BlockSpec(blk, lambda r, j: (nxt(j), 2 * r)),
                own, own_n, own, own_n]
    operands = [zv, zv, zv, zv, dov, dov, ov, ov, lv, lv]
    if has_prev:
        in_specs += [own, own]
        operands += [prev_dk.reshape(tv, d * DA), prev_dv.reshape(tv, d * DA)]
    sh = jax.ShapeDtypeStruct((tv, d * DA), out_dtype)
    dk, dv = pl.pallas_call(
        body, name=name, grid=(d, nb), in_specs=in_specs, out_specs=[own, own], out_shape=[sh, sh],
        compiler_params=pltpu.CompilerParams(dimension_semantics=("parallel", "parallel")),
    )(*operands)
    return dk.reshape(T, DA), dv.reshape(T, DA)


CT = 256
HALO = 32
RC = 32


def _conv_fwd(z, w_dw, b_dw, g_ln, b_ln):
    T = z.shape[0]
    ct = min(CT, T)
    nt = T // ct
    hb = ct // HALO

    def body(cv_ref, cg_ref, cvp_ref, cgp_ref, w_ref, bdw_ref, g_ref, b_ref, oc_ref, y_ref, ubuf, ush):
        i = pl.program_id(0)
        up = cvp_ref[...].astype(F32) * _sigmoid(cgp_ref[...].astype(F32))
        ubuf[0:HALO, :] = jnp.where(i > 0, up, 0.0)
        ubuf[HALO:, :] = cv_ref[...].astype(F32) * _sigmoid(cg_ref[...].astype(F32))
        for b in range(8):
            ush[b] = ubuf[pl.ds(8 - b, ct + 24), :]

        def chunk(ci, carry):
            r0 = pl.multiple_of(ci * RC, RC)
            acc = jnp.broadcast_to(bdw_ref[...], (RC, DC))
            for s in range(CW):
                a, b = divmod(s, 8)
                acc = acc + w_ref[CW - 1 - s:CW - s, :] * ush[b, pl.ds(r0 + 24 - 8 * a, RC), :]
            y_ref[pl.ds(r0, RC), :] = acc
            mu = jnp.mean(acc, axis=-1, keepdims=True)
            cen = acc - mu
            var = jnp.mean(cen * cen, axis=-1, keepdims=True)
            ln = cen * lax.rsqrt(var + EPS) * g_ref[...] + b_ref[...]
            oc_ref[pl.ds(r0, RC), :] = (ln * _sigmoid(ln)).astype(BF16)
            return carry

        lax.fori_loop(0, ct // RC, chunk, 0)

    cur = lambda col: pl.BlockSpec((ct, DC), lambda i: (i, col))
    prv = lambda col: pl.BlockSpec((HALO, DC), lambda i: (jnp.maximum(i * hb - 1, 0), col))
    vec = pl.BlockSpec((1, DC), lambda i: (0, 0))
    return pl.pallas_call(
        body, name="conv_fwd", grid=(nt,),
        in_specs=[cur(3), cur(4), prv(3), prv(4), pl.BlockSpec((CW, DC), lambda i: (0, 0)),
                  vec, vec, vec],
        out_specs=[pl.BlockSpec((ct, DC), lambda i: (i, 0))] * 2,
        out_shape=[jax.ShapeDtypeStruct((T, DC), BF16), jax.ShapeDtypeStruct((T, DC), F32)],
        scratch_shapes=[pltpu.VMEM((ct + HALO, DC), F32), pltpu.VMEM((8, ct + 24, DC), F32)],
        compiler_params=pltpu.CompilerParams(
            dimension_semantics=("parallel",), vmem_limit_bytes=VMEM_MID),
    )(z, z, z, z, w_dw, b_dw, g_ln, b_ln)


def _conv_bwd(z, dom, y, w_dw, g_ln, b_ln):
    T = z.shape[0]
    ct = min(CT, T)
    nt = T // ct
    hb = ct // HALO
    last_halo = T // HALO - 1

    def ln_bwd(yv, dov, g_ref, b_ref):
        mu = jnp.mean(yv, axis=-1, keepdims=True)
        cen = yv - mu
        rstd = lax.rsqrt(jnp.mean(cen * cen, axis=-1, keepdims=True) + EPS)
        xhat = cen * rstd
        ln = xhat * g_ref[...] + b_ref[...]
        sg = _sigmoid(ln)
        dln = dov * (sg * (1.0 + ln * (1.0 - sg)))
        dxh = dln * g_ref[...]
        dy = rstd * (dxh - jnp.mean(dxh, axis=-1, keepdims=True)
                     - xhat * jnp.mean(dxh * xhat, axis=-1, keepdims=True))
        return dy, dln, xhat

    def body(do_ref, don_ref, y_ref, yn_ref, cv_ref, cg_ref, cvp_ref, cgp_ref, w_ref, g_ref, b_ref,
             dcv_ref, dcg_ref, dw_ref, dbdw_ref, dg_ref, db_ref,
             dybuf, dysh, ubuf, ush, dwacc, vacc):
        i = pl.program_id(0)

        @pl.when(i == 0)
        def _():
            dwacc[...] = jnp.zeros_like(dwacc)
            vacc[...] = jnp.zeros_like(vacc)

        def ln_chunk(ci, carry):
            r0 = pl.multiple_of(ci * RC, RC)
            dy, dln, xhat = ln_bwd(y_ref[pl.ds(r0, RC), :], do_ref[pl.ds(r0, RC), :].astype(F32),
                                   g_ref, b_ref)
            dybuf[pl.ds(r0, RC), :] = dy
            vacc[0] += _fold8(dy)
            vacc[1] += _fold8(dln * xhat)
            vacc[2] += _fold8(dln)
            return carry

        lax.fori_loop(0, ct // RC, ln_chunk, 0)
        dyn, _, _ = ln_bwd(yn_ref[...], don_ref[...].astype(F32), g_ref, b_ref)
        dybuf[ct:, :] = jnp.where(i < nt - 1, dyn, 0.0)
        for b in range(8):
            dysh[b] = dybuf[pl.ds(b, ct + 24), :]

        up = cvp_ref[...].astype(F32) * _sigmoid(cgp_ref[...].astype(F32))
        ubuf[0:HALO, :] = jnp.where(i > 0, up, 0.0)
        ubuf[HALO:, :] = cv_ref[...].astype(F32) * _sigmoid(cg_ref[...].astype(F32))
        for b in range(8):
            ush[b] = ubuf[pl.ds(8 - b, ct + 24), :]

        def chunk(ci, carry):
            r0 = pl.multiple_of(ci * RC, RC)
            dy = dybuf[pl.ds(r0, RC), :]
            du = jnp.zeros((RC, DC), F32)
            for s in range(CW):
                a, b = divmod(s, 8)
                du = du + w_ref[CW - 1 - s:CW - s, :] * dysh[b, pl.ds(r0 + 8 * a, RC), :]
                dwacc[CW - 1 - s] += _fold8(dy * ush[b, pl.ds(r0 + 24 - 8 * a, RC), :])
            cv = cv_ref[pl.ds(r0, RC), :].astype(F32)
            sg = _sigmoid(cg_ref[pl.ds(r0, RC), :].astype(F32))
            dcv_ref[pl.ds(r0, RC), :] = (du * sg).astype(BF16)
            dcg_ref[pl.ds(r0, RC), :] = (du * cv * sg * (1.0 - sg)).astype(BF16)
            return carry

        lax.fori_loop(0, ct // RC, chunk, 0)

        @pl.when(i == nt - 1)
        def _():
            dw_ref[...] = jnp.sum(dwacc[...], axis=1)
            dbdw_ref[...] = jnp.sum(vacc[0], axis=0, keepdims=True)
            dg_ref[...] = jnp.sum(vacc[1], axis=0, keepdims=True)
            db_ref[...] = jnp.sum(vacc[2], axis=0, keepdims=True)

    cur = lambda col: pl.BlockSpec((ct, DC), lambda i: (i, col))
    prv = lambda col: pl.BlockSpec((HALO, DC), lambda i: (jnp.maximum(i * hb - 1, 0), col))
    nxt = lambda col: pl.BlockSpec((HALO, DC), lambda i: (jnp.minimum((i + 1) * hb, last_halo), col))
    vec = pl.BlockSpec((1, DC), lambda i: (0, 0))
    tile = pl.BlockSpec((ct, DC), lambda i: (i, 0))
    return pl.pallas_call(
        body, name="conv_bwd", grid=(nt,),
        in_specs=[cur(1), nxt(1), cur(0), nxt(0), cur(3), cur(4), prv(3), prv(4),
                  pl.BlockSpec((CW, DC), lambda i: (0, 0)), vec, vec],
        out_specs=[tile, tile, pl.BlockSpec((CW, DC), lambda i: (0, 0)), vec, vec, vec],
        out_shape=[jax.ShapeDtypeStruct((T, DC), BF16), jax.ShapeDtypeStruct((T, DC), BF16),
                   jax.ShapeDtypeStruct((CW, DC), F32), jax.ShapeDtypeStruct((1, DC), F32),
                   jax.ShapeDtypeStruct((1, DC), F32), jax.ShapeDtypeStruct((1, DC), F32)],
        scratch_shapes=[pltpu.VMEM((ct + HALO, DC), F32), pltpu.VMEM((8, ct + 24, DC), F32),
                        pltpu.VMEM((ct + HALO, DC), F32), pltpu.VMEM((8, ct + 24, DC), F32),
                        pltpu.VMEM((CW, 8, DC), F32), pltpu.VMEM((3, 8, DC), F32)],
        compiler_params=pltpu.CompilerParams(
            dimension_semantics=("arbitrary",), vmem_limit_bytes=VMEM_BIG),
    )(dom, dom, y, y, z, z, z, z, w_dw, g_ln, b_ln)


def _adam_math(w, g, m, v):
    m = ADAM_B1 * m + (1.0 - ADAM_B1) * g
    v = ADAM_B2 * v + (1.0 - ADAM_B2) * (g * g)
    m_hat = m / (1.0 - ADAM_B1 ** ADAM_STEP)
    v_hat = v / (1.0 - ADAM_B2 ** ADAM_STEP)
    delta = -ADAM_LR * (m_hat / (jnp.sqrt(v_hat) + ADAM_EPS) + ADAM_WD * w)
    return delta, m, v


def _adam(name, slots, w, m, v):
    rows, cols = w.shape
    tr = next(t for t in (256, 176, 128, 64, 32, 16, 8, rows) if rows % t == 0)

    def body(s_ref, w_ref, m_ref, v_ref, g_out, d_out, m_out, v_out):
        g = s_ref[0].astype(F32)
        for s in range(1, NDEV):
            g = g + s_ref[s].astype(F32)
        delta, mn, vn = _adam_math(w_ref[...], g, m_ref[...], v_ref[...])
        g_out[...] = g
        d_out[...] = delta
        m_out[...] = mn
        v_out[...] = vn

    tile = pl.BlockSpec((tr, cols), lambda i: (i, 0))
    sh = jax.ShapeDtypeStruct((rows, cols), F32)
    return pl.pallas_call(
        body, name=name, grid=(rows // tr,),
        in_specs=[pl.BlockSpec((NDEV, tr, cols), lambda i: (0, i, 0)), tile, tile, tile],
        out_specs=[tile] * 4, out_shape=[sh] * 4,
        compiler_params=pltpu.CompilerParams(
            dimension_semantics=("parallel",), vmem_limit_bytes=VMEM_MID),
    )(slots, w, m, v)


SMALL_NAMES = ("g_mix", "b_dw", "g_conv_ln", "b_conv_ln", "g_ffn", "g_ple", "b_pgate", "g_final")


def _pack_small(vecs, w_dw_full):
    rows = [jnp.pad(v.reshape(1, -1), ((0, 0), (0, SMALL_W - v.size))) for v in vecs]
    rows.append(jnp.pad(w_dw_full, ((0, 0), (0, SMALL_W - DC))))
    rows.append(jnp.zeros((SMALL_ROWS - len(vecs) - CW, SMALL_W), F32))
    return jnp.concatenate(rows, axis=0)


def kernel(x, p, g_mix, w_in, w_dw, b_dw, g_conv_ln, b_conv_ln, w_out, g_ffn, w_gate, w_up, w_down, g_ple, w_pgate, b_pgate, w_ple, g_final, loss_target, m_g_mix, m_w_in, m_w_dw, m_b_dw, m_g_conv_ln, m_b_conv_ln, m_w_out, m_g_ffn, m_w_gate, m_w_up, m_w_down, m_g_ple, m_w_pgate, m_b_pgate, m_w_ple, m_g_final, v_g_mix, v_w_in, v_w_dw, v_b_dw, v_g_conv_ln, v_b_conv_ln, v_w_out, v_g_ffn, v_w_gate, v_w_up, v_w_down, v_g_ple, v_w_pgate, v_b_pgate, v_w_ple, v_g_final):
    T = x.shape[1]
    me = 4 * lax.axis_index("x") + 2 * lax.axis_index("y") + lax.axis_index("c")
    xs = x.reshape(T, D)
    ps = p.reshape(T, DPLE).astype(BF16)
    tgt = loss_target.reshape(T, D)
    g_final2 = g_final.reshape(1, D)

    big = dict(w_in=w_in[0], w_out=w_out[0], w_gate=w_gate[0], w_up=w_up[0], w_down=w_down[0],
               w_pgate=w_pgate[0], w_ple=w_ple[0])
    order = ("w_in", "w_out", "w_gate", "w_up", "w_down", "w_pgate", "w_ple")
    gathered = _exchange(
        "gather_weights",
        [(big[n].astype(BF16), False) for n in order] + [(w_dw.reshape(CW, DC // NDEV), False)])
    W = dict(zip(order, gathered[:7]))
    w_in_f = W["w_in"]
    w_out_f = W["w_out"].reshape(D, D)
    w_gate_f = W["w_gate"]
    w_up_f = W["w_up"]
    w_down_f = W["w_down"]
    w_pgate_f = W["w_pgate"].reshape(D, D)
    w_ple_f = W["w_ple"].transpose(1, 0, 2).reshape(DPLE, D)
    w_dw_f = gathered[7].transpose(1, 0, 2).reshape(CW, DC)

    a = _rms_fwd("rms_mix", xs, g_mix)
    z = _mm_in(a, w_in_f)
    br = [_attn_fwd(f"attn_fwd_d{d}", z, d) for d in DILATIONS]
    o_attn, lse = _attn_combine([b[0] for b in br], [b[1] for b in br])
    o_conv, y_conv = _conv_fwd(z, w_dw_f, b_dw, g_conv_ln, b_conv_ln)
    h1 = _mm_out(o_attn, o_conv, w_out_f, xs)
    f = _rms_fwd("rms_ffn", h1, g_ffn)
    gate, up, act = _mm_gate_up(f, w_gate_f, w_up_f)
    h2 = _mm_down(act, w_down_f, h1)
    r = _rms_fwd("rms_ple", h2, g_ple)
    gte, pe, h3 = _mm_ple(r, w_pgate_f, b_pgate, ps, w_ple_f, h2)

    loss_part, dh3, dpe, dpg, d_g_final, d_b_pgate = _loss_bwd(h3, tgt, g_final2, pe, gte)
    gw_pgate = _mm_tn("gw_pgate", r, dpg)
    gw_ple = _mm_tn("gw_ple", ps, dpe)
    dr = _mm_nt("mm_pgate_bwd", dpg, w_pgate_f)
    dh2, dh2b, d_g_ple = _rms_bwd("rms_ple_bwd", dr, h2, g_ple, dh3, True)
    gw_down = _mm_tn_ff_rows("gw_down", act, dh2b)
    dgate, dup = _mm_down_bwd(dh2b, w_down_f, gate, up)
    gw_gate = _mm_tn_ff_cols("gw_gate", f, dgate)
    gw_up = _mm_tn_ff_cols("gw_up", f, dup)
    df = _mm_ffn_in_bwd(dgate, w_gate_f, dup, w_up_f)
    dh1, dh1b, d_g_ffn = _rms_bwd("rms_ffn_bwd", df, h1, g_ffn, dh2, True)
    gw_out = jnp.concatenate(
        [_mm_tn("gw_out_attn", o_attn, dh1b), _mm_tn("gw_out_conv", o_conv, dh1b)], axis=0)
    dom = _mm_nt("mm_out_bwd", dh1b, w_out_f)
    dcv, dcg, d_w_dw, d_b_dw, d_g_ln, d_b_ln = _conv_bwd(z, dom, y_conv, w_dw_f, g_conv_ln, b_conv_ln)
    dq = dk = dv = None
    for bi, d in enumerate(DILATIONS):
        dt = BF16 if bi == len(DILATIONS) - 1 else F32
        dq = _attn_bwd_q(f"attn_bwd_q_d{d}", z, dom, o_attn, lse, d, dq, dt)
        dk, dv = _attn_bwd_kv(f"attn_bwd_kv_d{d}", z, dom, o_attn, lse, d, dk, dv, dt)
    dz = jnp.concatenate([dq, dk, dv, dcv, dcg], axis=1)
    gw_in = _mm_tn_cols("gw_in", a, dz, N_IN)
    da = _mm_in_bwd(dz, w_in_f)
    grad_x, d_g_mix = _rms_bwd("rms_mix_bwd", da, xs, g_mix, dh1, False)

    small_part = _pack_small(
        [d_g_mix, d_b_dw, d_g_ln, d_b_ln, d_g_ffn, d_g_ple, d_b_pgate, d_g_final], d_w_dw)
    gw_ple_s = gw_ple.reshape(DPLE, NDEV, D // NDEV).transpose(1, 0, 2)
    parts = dict(w_in=gw_in, w_out=gw_out.reshape(NDEV, D // NDEV, D), w_gate=gw_gate, w_up=gw_up,
                 w_down=gw_down, w_pgate=gw_pgate.reshape(NDEV, D // NDEV, D), w_ple=gw_ple_s)
    slots = _exchange("exchange_grads", [(parts[n], True) for n in order] + [(small_part, False)])
    S = dict(zip(order, slots[:7]))
    small_slots = slots[7]

    mom = dict(w_in=(m_w_in, v_w_in), w_out=(m_w_out, v_w_out), w_gate=(m_w_gate, v_w_gate),
               w_up=(m_w_up, v_w_up), w_down=(m_w_down, v_w_down), w_pgate=(m_w_pgate, v_w_pgate),
               w_ple=(m_w_ple, v_w_ple))
    upd = {}
    for n in order:
        res = _adam(f"adam_{n}", S[n], big[n], mom[n][0][0], mom[n][1][0])
        upd[n] = [t[None] for t in res]

    def lanes(v):
        full = jnp.zeros((CW, NDEV, DC // NDEV), F32)
        full = lax.dynamic_update_slice(full, v.reshape(CW, 1, DC // NDEV), (0, me, 0))
        return full.reshape(CW, DC)

    small_w = _pack_small([g_mix, b_dw, g_conv_ln, b_conv_ln, g_ffn, g_ple, b_pgate, g_final2], lanes(w_dw))
    small_m = _pack_small([m_g_mix, m_b_dw, m_g_conv_ln, m_b_conv_ln, m_g_ffn, m_g_ple, m_b_pgate,
                           m_g_final.reshape(1, D)], lanes(m_w_dw))
    small_v = _pack_small([v_g_mix, v_b_dw, v_g_conv_ln, v_b_conv_ln, v_g_ffn, v_g_ple, v_b_pgate,
                           v_g_final.reshape(1, D)], lanes(v_w_dw))
    small_res = _adam("adam_small", small_slots, small_w, small_m, small_v)

    def unpack(t):
        out = {}
        widths = dict(g_mix=D, b_dw=DC, g_conv_ln=DC, b_conv_ln=DC, g_ffn=D, g_ple=D, b_pgate=D, g_final=D)
        for i, n in enumerate(SMALL_NAMES):
            out[n] = t[i:i + 1, :widths[n]]
        out["g_final"] = out["g_final"].reshape(D)
        taps = t[len(SMALL_NAMES):len(SMALL_NAMES) + CW, :DC].reshape(CW, NDEV, DC // NDEV)
        out["w_dw"] = lax.dynamic_slice(taps, (0, me, 0), (CW, 1, DC // NDEV))[None]
        return out

    small = [unpack(t) for t in small_res]

    loss = lax.psum(loss_part[0, 0], ("x", "y", "c"))
    names = ("g_mix", "w_in", "w_dw", "b_dw", "g_conv_ln", "b_conv_ln", "w_out", "g_ffn", "w_gate",
             "w_up", "w_down", "g_ple", "w_pgate", "b_pgate", "w_ple", "g_final")
    outs = [loss, grad_x.reshape(1, T, D)]
    for kind in range(4):
        for n in names:
            outs.append(upd[n][kind] if n in upd else small[kind][n])
    return tuple(outs)
```

```python
import jax
import jax.numpy as jnp
from jax import lax
from jax.experimental import pallas as pl
from jax.experimental.pallas import tpu as pltpu

F32 = jnp.float32
BF16 = jnp.bfloat16

NDEV = 8
D = 2048
NH = 8
DH = 128
DA = NH * DH
DC = D - DA
DIN = 3 * DA + 2 * DC
DFF = 5632
DPLE = 256
BLK = 128
DILATIONS = (1, 4, 16)
CW = 31
EPS = 1e-6
N_IN = DIN // NDEV
N_FF = DFF // NDEV
NEG = -1e30

ADAM_LR = 0.001
ADAM_B1 = 0.9
ADAM_B2 = 0.999
ADAM_EPS = 1e-08
ADAM_WD = 0.01
ADAM_STEP = 10

VMEM_CAP_V7X = 64 * 1024 * 1024
VMEM_BIG = VMEM_CAP_V7X - 12 * 1024 * 1024
VMEM_MID = 40 * 1024 * 1024

SMALL_W = 2048
SMALL_ROWS = 40


def _sigmoid(v):
    return 1.0 / (1.0 + jnp.exp(-v))


def _dot(a, b, contract):
    return lax.dot_general(a, b, (contract, ((), ())), preferred_element_type=F32)


NN = ((1,), (0,))
NT = ((1,), (1,))
TN = ((0,), (0,))


def _exchange(name, items):
    n = len(items)
    out_shape = [
        jax.ShapeDtypeStruct((NDEV,) + (a.shape[1:] if sc else a.shape), a.dtype)
        for a, sc in items
    ]
    scat = [sc for _, sc in items]

    def body(*refs):
        srcs = refs[:n]
        dsts = refs[n:2 * n]
        send_sems, recv_sems, loc_sems = refs[2 * n:]
        x = lax.axis_index("x")
        y = lax.axis_index("y")
        c = lax.axis_index("c")
        me = 4 * x + 2 * y + c

        local = []
        for i in range(n):
            src = srcs[i].at[me] if scat[i] else srcs[i]
            cp = pltpu.make_async_copy(src, dsts[i].at[me], loc_sems.at[i])
            cp.start()
            local.append(cp)

        remote = []
        for k in range(1, NDEV):
            px = (1 - x) if (k >> 2) & 1 else x
            py = (1 - y) if (k >> 1) & 1 else y
            pc = (1 - c) if k & 1 else c
            peer = 4 * px + 2 * py + pc
            for i in range(n):
                sem = i * (NDEV - 1) + k - 1
                src = srcs[i].at[peer] if scat[i] else srcs[i]
                send = pltpu.make_async_remote_copy(
                    src_ref=src, dst_ref=dsts[i].at[me],
                    send_sem=send_sems.at[sem], recv_sem=recv_sems.at[sem],
                    device_id=(px, py, pc), device_id_type=pl.DeviceIdType.MESH)
                send.start()
                recv = pltpu.make_async_remote_copy(
                    src_ref=src, dst_ref=dsts[i].at[peer],
                    send_sem=send_sems.at[sem], recv_sem=recv_sems.at[sem],
                    device_id=(px, py, pc), device_id_type=pl.DeviceIdType.MESH)
                remote.append((send, recv))
        for send, recv in remote:
            recv.wait_recv()
            send.wait_send()
        for cp in local:
            cp.wait()

    any_spec = pl.BlockSpec(memory_space=pl.ANY)
    return pl.pallas_call(
        body, name=name,
        in_specs=[any_spec] * n, out_specs=[any_spec] * n, out_shape=out_shape,
        scratch_shapes=[
            pltpu.SemaphoreType.DMA((n * (NDEV - 1),)),
            pltpu.SemaphoreType.DMA((n * (NDEV - 1),)),
            pltpu.SemaphoreType.DMA((n,)),
        ],
    )(*[a for a, _ in items])


HBM_SPEC = pl.BlockSpec(memory_space=pltpu.HBM)
SEM_SPEC = pl.BlockSpec(memory_space=pltpu.SEMAPHORE)
ANY_SPEC = pl.BlockSpec(memory_space=pl.ANY)
EFFECT = pltpu.SideEffectType.DATAFLOW_SIDE_EFFECTING


def _peer_of(k):
    x = lax.axis_index("x")
    y = lax.axis_index("y")
    c = lax.axis_index("c")
    px = (1 - x) if (k >> 2) & 1 else x
    py = (1 - y) if (k >> 1) & 1 else y
    pc = (1 - c) if k & 1 else c
    return (px, py, pc), 4 * px + 2 * py + pc


def _my_index():
    return 4 * lax.axis_index("x") + 2 * lax.axis_index("y") + lax.axis_index("c")


def _slot_shape(a, sc):
    return (NDEV,) + (a.shape[1:] if sc else a.shape)


def _place(name, items):
    n = len(items)
    scat = [sc for _, sc in items]

    def body(*refs):
        srcs = refs[:n]
        lands = refs[n:2 * n]
        sems = refs[2 * n]
        me = _my_index()
        cps = []
        for i in range(n):
            src = srcs[i].at[me] if scat[i] else srcs[i]
            cp = pltpu.make_async_copy(src, lands[i].at[me], sems.at[i])
            cp.start()
            cps.append(cp)
        for cp in cps:
            cp.wait()

    return pl.pallas_call(
        body, name=name, in_specs=[ANY_SPEC] * n, out_specs=[ANY_SPEC] * n,
        out_shape=[jax.ShapeDtypeStruct(_slot_shape(a, sc), a.dtype) for a, sc in items],
        scratch_shapes=[pltpu.SemaphoreType.DMA((n,))],
    )(*[a for a, _ in items])


def _xstart(name, items, lands):
    n = len(items)
    scat = [sc for _, sc in items]

    def body(*refs):
        srcs = refs[:n]
        lzs = refs[n:2 * n]
        send_sems = refs[2 * n:3 * n]
        recv_sems = refs[3 * n:4 * n]
        token = refs[-1]
        me = _my_index()
        for i in range(n):
            for k in range(1, NDEV):
                peer_id, peer = _peer_of(k)
                src = srcs[i].at[peer] if scat[i] else srcs[i]
                pltpu.make_async_remote_copy(
                    src_ref=src, dst_ref=lzs[i].at[me],
                    send_sem=send_sems[i].at[k - 1], recv_sem=recv_sems[i].at[k - 1],
                    device_id=peer_id, device_id_type=pl.DeviceIdType.MESH).start()
        token[...] = jnp.zeros_like(token)

    sem = pltpu.SemaphoreType.DMA((NDEV - 1,))
    hbm = [pltpu.HBM(a.shape, a.dtype) for a, _ in items] + [pltpu.HBM(l.shape, l.dtype) for l in lands]
    res = pl.pallas_call(
        body, name=name,
        in_specs=[HBM_SPEC] * (2 * n),
        out_specs=[SEM_SPEC] * (2 * n) + [HBM_SPEC] * (2 * n) + [pl.BlockSpec(memory_space=pltpu.VMEM)],
        out_shape=[sem] * (2 * n) + hbm + [jax.ShapeDtypeStruct((8, 128), F32)],
        input_output_aliases={i: 2 * n + i for i in range(2 * n)},
        compiler_params=pltpu.CompilerParams(has_side_effects=EFFECT),
    )(*[pltpu.with_memory_space_constraint(a, pltpu.HBM) for a, _ in items],
      *[pltpu.with_memory_space_constraint(l, pltpu.HBM) for l in lands])
    handles = [(res[i], res[n + i], res[2 * n + i], res[3 * n + i], scat[i]) for i in range(n)]
    return handles, res[-1]


def _xwait(name, handle, after):
    send_sem, recv_sem, src, land, sc = handle

    def body(src_ref, land_ref, send_ref, recv_ref, after_ref, src_dead, got_ref):
        for k in range(1, NDEV):
            peer_id, peer = _peer_of(k)
            cp = pltpu.make_async_remote_copy(
                src_ref=src_ref.at[peer] if sc else src_ref, dst_ref=land_ref.at[peer],
                send_sem=send_ref.at[k - 1], recv_sem=recv_ref.at[k - 1],
                device_id=peer_id, device_id_type=pl.DeviceIdType.MESH)
            cp.wait_send()
            cp.wait_recv()

    return pl.pallas_call(
        body, name=name,
        in_specs=[HBM_SPEC, HBM_SPEC, SEM_SPEC, SEM_SPEC, ANY_SPEC],
        out_specs=[HBM_SPEC, HBM_SPEC],
        out_shape=[pltpu.HBM(src.shape, src.dtype), pltpu.HBM(land.shape, land.dtype)],
        input_output_aliases={0: 0, 1: 1},
        compiler_params=pltpu.CompilerParams(has_side_effects=EFFECT),
    )(src, land, send_sem, recv_sem, after)[1]


def _mm(name, grid, in_specs, operands, out_specs, out_shape, contract, n_pairs, epilogue,
        acc_shape=None, vmem=VMEM_BIG, deps=()):
    nk = grid[2]
    n_extra = len(operands) - 2 * n_pairs
    n_out = len(out_shape)
    n_in = len(operands) + len(deps)
    in_specs = list(in_specs) + [ANY_SPEC] * len(deps)
    operands = list(operands) + list(deps)

    def body(*refs):
        ab = refs[:2 * n_pairs]
        extras = refs[2 * n_pairs:2 * n_pairs + n_extra]
        outs = refs[n_in:n_in + n_out]
        part = None
        for p in range(n_pairs):
            d = _dot(ab[2 * p][...], ab[2 * p + 1][...], contract)
            part = d if part is None else part + d
        if nk == 1:
            epilogue(part, extras, outs)
        else:
            acc_ref = refs[-1]
            k = pl.program_id(2)

            @pl.when(k == 0)
            def _():
                acc_ref[...] = part

            @pl.when(k > 0)
            def _():
                acc_ref[...] += part

            @pl.when(k == nk - 1)
            def _():
                epilogue(acc_ref[...], extras, outs)

    scratch = [pltpu.VMEM(acc_shape, F32)] if nk > 1 else []
    return pl.pallas_call(
        body, name=name, grid=grid, in_specs=in_specs, out_specs=out_specs, out_shape=out_shape,
        scratch_shapes=scratch,
        compiler_params=pltpu.CompilerParams(
            dimension_semantics=("parallel", "parallel", "arbitrary"), vmem_limit_bytes=vmem),
    )(*operands)


def _ep_cast(dtype):
    def ep(acc, extras, outs):
        outs[0][...] = acc.astype(dtype)
    return ep


def _ep_resid(acc, extras, outs):
    outs[0][...] = extras[0][...] + acc


def _ep_swiglu_bwd(acc, extras, outs):
    g = extras[0][...].astype(F32)
    u = extras[1][...].astype(F32)
    sg = _sigmoid(g)
    outs[0][...] = (acc * u * (sg * (1.0 + g * (1.0 - sg)))).astype(BF16)
    outs[1][...] = (acc * (g * sg)).astype(BF16)


def _row_tile(T):
    return min(1024, T)


def _mm_in(a, w_in):
    T = a.shape[0]
    tm = _row_tile(T)
    return _mm(
        "mm_in", (T // tm, NDEV, 1),
        [pl.BlockSpec((tm, D), lambda i, j, k: (i, 0)),
         pl.BlockSpec((None, D, N_IN), lambda i, j, k: (j, 0, 0))],
        [a, w_in],
        [pl.BlockSpec((tm, N_IN), lambda i, j, k: (i, j))],
        [jax.ShapeDtypeStruct((T, DIN), BF16)], NN, 1, _ep_cast(BF16))[0]


def _mm_out(o_attn, o_conv, w_out, x):
    T = x.shape[0]
    tm = _row_tile(T)
    tn = 1024
    return _mm(
        "mm_out", (T // tm, D // tn, 1),
        [pl.BlockSpec((tm, DA), lambda i, j, k: (i, 0)),
         pl.BlockSpec((DA, tn), lambda i, j, k: (0, j)),
         pl.BlockSpec((tm, DC), lambda i, j, k: (i, 0)),
         pl.BlockSpec((DC, tn), lambda i, j, k: (1, j)),
         pl.BlockSpec((tm, tn), lambda i, j, k: (i, j))],
        [o_attn, w_out, o_conv, w_out, x],
        [pl.BlockSpec((tm, tn), lambda i, j, k: (i, j))],
        [jax.ShapeDtypeStruct((T, D), F32)], NN, 2, _ep_resid)[0]


def _mm_gate_up(f, w_gate, w_up):
    T = f.shape[0]
    tm = _row_tile(T)

    def body(f_ref, wg_ref, wu_ref, g_ref, u_ref, a_ref):
        fv = f_ref[...]
        g = _dot(fv, wg_ref[...], NN)
        u = _dot(fv, wu_ref[...], NN)
        g_ref[...] = g.astype(BF16)
        u_ref[...] = u.astype(BF16)
        a_ref[...] = (g * _sigmoid(g) * u).astype(BF16)

    wspec = pl.BlockSpec((None, D, N_FF), lambda i, j: (j, 0, 0))
    ospec = pl.BlockSpec((None, tm, N_FF), lambda i, j: (j, i, 0))
    sh = jax.ShapeDtypeStruct((NDEV, T, N_FF), BF16)
    return pl.pallas_call(
        body, name="mm_gate_up", grid=(T // tm, NDEV),
        in_specs=[pl.BlockSpec((tm, D), lambda i, j: (i, 0)), wspec, wspec],
        out_specs=[ospec, ospec, ospec], out_shape=[sh, sh, sh],
        compiler_params=pltpu.CompilerParams(
            dimension_semantics=("parallel", "parallel"), vmem_limit_bytes=VMEM_BIG),
    )(f, w_gate, w_up)


def _mm_down(act, w_down, h1):
    T = h1.shape[0]
    tm = _row_tile(T)
    tn = 1024
    return _mm(
        "mm_down", (T // tm, D // tn, NDEV),
        [pl.BlockSpec((None, tm, N_FF), lambda i, j, k: (k, i, 0)),
         pl.BlockSpec((None, N_FF, tn), lambda i, j, k: (k, 0, j)),
         pl.BlockSpec((tm, tn), lambda i, j, k: (i, j))],
        [act, w_down, h1],
        [pl.BlockSpec((tm, tn), lambda i, j, k: (i, j))],
        [jax.ShapeDtypeStruct((T, D), F32)], NN, 1, _ep_resid, acc_shape=(tm, tn))[0]


def _mm_ple(r, w_pgate, b_pgate, p, w_ple, h2):
    T = h2.shape[0]
    tm = _row_tile(T)
    tn = 1024

    def body(r_ref, wg_ref, b_ref, p_ref, wp_ref, h2_ref, gte_ref, pe_ref, h3_ref):
        gte = _sigmoid(_dot(r_ref[...], wg_ref[...], NN) + b_ref[...])
        pe = _dot(p_ref[...], wp_ref[...], NN)
        gte_ref[...] = gte.astype(BF16)
        pe_ref[...] = pe.astype(BF16)
        h3_ref[...] = h2_ref[...] + pe * gte

    tile = pl.BlockSpec((tm, tn), lambda i, j: (i, j))
    return pl.pallas_call(
        body, name="mm_ple", grid=(T // tm, D // tn),
        in_specs=[pl.BlockSpec((tm, D), lambda i, j: (i, 0)),
                  pl.BlockSpec((D, tn), lambda i, j: (0, j)),
                  pl.BlockSpec((1, tn), lambda i, j: (0, j)),
                  pl.BlockSpec((tm, DPLE), lambda i, j: (i, 0)),
                  pl.BlockSpec((DPLE, tn), lambda i, j: (0, j)),
                  tile],
        out_specs=[tile, tile, tile],
        out_shape=[jax.ShapeDtypeStruct((T, D), BF16), jax.ShapeDtypeStruct((T, D), BF16),
                   jax.ShapeDtypeStruct((T, D), F32)],
        compiler_params=pltpu.CompilerParams(
            dimension_semantics=("parallel", "parallel"), vmem_limit_bytes=VMEM_BIG),
    )(r, w_pgate, b_pgate, p, w_ple, h2)


def _mm_nt(name, dy, w, deps=()):
    T, n = dy.shape
    kdim = w.shape[0]
    tm = _row_tile(T)
    tn = 1024
    return _mm(
        name, (T // tm, kdim // tn, 1),
        [pl.BlockSpec((tm, n), lambda i, j, k: (i, 0)),
         pl.BlockSpec((tn, n), lambda i, j, k: (j, 0))],
        [dy, w],
        [pl.BlockSpec((tm, tn), lambda i, j, k: (i, j))],
        [jax.ShapeDtypeStruct((T, kdim), BF16)], NT, 1, _ep_cast(BF16), deps=deps)[0]


def _mm_down_bwd(dh2, w_down, g, u, deps=()):
    T = dh2.shape[0]
    tm = _row_tile(T)
    gspec = pl.BlockSpec((None, tm, N_FF), lambda i, j, k: (j, i, 0))
    sh = jax.ShapeDtypeStruct((NDEV, T, N_FF), BF16)
    return _mm(
        "mm_down_bwd", (T // tm, NDEV, 1),
        [pl.BlockSpec((tm, D), lambda i, j, k: (i, 0)),
         pl.BlockSpec((None, N_FF, D), lambda i, j, k: (j, 0, 0)),
         gspec, gspec],
        [dh2, w_down, g, u],
        [gspec, gspec], [sh, sh], NT, 1, _ep_swiglu_bwd, deps=deps)


def _mm_ffn_in_bwd(dg, w_gate, du, w_up, deps=()):
    T = dg.shape[1]
    tm = _row_tile(T)
    tn = 1024
    aspec = pl.BlockSpec((None, tm, N_FF), lambda i, j, k: (k, i, 0))
    wspec = pl.BlockSpec((None, tn, N_FF), lambda i, j, k: (k, j, 0))
    return _mm(
        "mm_ffn_in_bwd", (T // tm, D // tn, NDEV),
        [aspec, wspec, aspec, wspec], [dg, w_gate, du, w_up],
        [pl.BlockSpec((tm, tn), lambda i, j, k: (i, j))],
        [jax.ShapeDtypeStruct((T, D), BF16)], NT, 2, _ep_cast(BF16), acc_shape=(tm, tn),
        deps=deps)[0]


def _mm_in_bwd(dz, w_in, deps=()):
    T = dz.shape[0]
    tm = _row_tile(T)
    tn = 1024
    return _mm(
        "mm_in_bwd", (T // tm, D // tn, NDEV),
        [pl.BlockSpec((tm, N_IN), lambda i, j, k: (i, k)),
         pl.BlockSpec((None, tn, N_IN), lambda i, j, k: (k, j, 0))],
        [dz, w_in],
        [pl.BlockSpec((tm, tn), lambda i, j, k: (i, j))],
        [jax.ShapeDtypeStruct((T, D), BF16)], NT, 1, _ep_cast(BF16), acc_shape=(tm, tn),
        deps=deps)[0]


def _mm_tn(name, a, b, tj=None):
    T, idim = a.shape
    jdim = b.shape[1]
    tt = _row_tile(T)
    ti = min(idim, 1024)
    tj = jdim if tj is None else tj
    return _mm(
        name, (idim // ti, jdim // tj, T // tt),
        [pl.BlockSpec((tt, ti), lambda i, j, k: (k, i)),
         pl.BlockSpec((tt, tj), lambda i, j, k: (k, j))],
        [a, b],
        [pl.BlockSpec((ti, tj), lambda i, j, k: (i, j))],
        [jax.ShapeDtypeStruct((idim, jdim), BF16)], TN, 1, _ep_cast(BF16), acc_shape=(ti, tj))[0]


def _mm_tn_cols(name, a, b, ncol):
    T, idim = a.shape
    tt = _row_tile(T)
    return _mm(
        name, (1, NDEV, T // tt),
        [pl.BlockSpec((tt, idim), lambda i, j, k: (k, 0)),
         pl.BlockSpec((tt, ncol), lambda i, j, k: (k, j))],
        [a, b],
        [pl.BlockSpec((None, idim, ncol), lambda i, j, k: (j, 0, 0))],
        [jax.ShapeDtypeStruct((NDEV, idim, ncol), BF16)], TN, 1, _ep_cast(BF16),
        acc_shape=(idim, ncol))[0]


def _mm_tn_ff_cols(name, a, b):
    T = a.shape[0]
    tt = _row_tile(T)
    return _mm(
        name, (1, NDEV, T // tt),
        [pl.BlockSpec((tt, D), lambda i, j, k: (k, 0)),
         pl.BlockSpec((None, tt, N_FF), lambda i, j, k: (j, k, 0))],
        [a, b],
        [pl.BlockSpec((None, D, N_FF), lambda i, j, k: (j, 0, 0))],
        [jax.ShapeDtypeStruct((NDEV, D, N_FF), BF16)], TN, 1, _ep_cast(BF16),
        acc_shape=(D, N_FF))[0]


def _mm_tn_ff_rows(name, a, b):
    T = b.shape[0]
    tt = _row_tile(T)
    return _mm(
        name, (NDEV, 1, T // tt),
        [pl.BlockSpec((None, tt, N_FF), lambda i, j, k: (i, k, 0)),
         pl.BlockSpec((tt, D), lambda i, j, k: (k, 0))],
        [a, b],
        [pl.BlockSpec((None, N_FF, D), lambda i, j, k: (i, 0, 0))],
        [jax.ShapeDtypeStruct((NDEV, N_FF, D), BF16)], TN, 1, _ep_cast(BF16),
        acc_shape=(N_FF, D))[0]


TR = 256


def _rows(T):
    return min(TR, T)


def _rms_fwd(name, h, g, deps=()):
    T = h.shape[0]
    tr = _rows(T)

    def body(h_ref, g_ref, *rest):
        o_ref = rest[-1]
        v = h_ref[...]
        r = lax.rsqrt(jnp.mean(v * v, axis=-1, keepdims=True) + EPS)
        o_ref[...] = (v * r * g_ref[...]).astype(BF16)

    return pl.pallas_call(
        body, name=name, grid=(T // tr,),
        in_specs=[pl.BlockSpec((tr, D), lambda i: (i, 0)), pl.BlockSpec((1, D), lambda i: (0, 0))]
        + [ANY_SPEC] * len(deps),
        out_specs=pl.BlockSpec((tr, D), lambda i: (i, 0)),
        out_shape=jax.ShapeDtypeStruct((T, D), BF16),
        compiler_params=pltpu.CompilerParams(dimension_semantics=("parallel",)),
    )(h, g, *deps)


def _fold8(v):
    return jnp.sum(v.reshape(v.shape[0] // 8, 8, v.shape[1]), axis=0)


def _rms_bwd(name, dn_out, h, g, dres, want_bf16):
    T = h.shape[0]
    tr = _rows(T)
    nt = T // tr

    def body(dy_ref, h_ref, g_ref, dres_ref, *rest):
        if want_bf16:
            dh_ref, dhb_ref, dg_ref, acc = rest
        else:
            dh_ref, dg_ref, acc = rest
        i = pl.program_id(0)
        v = h_ref[...]
        r = lax.rsqrt(jnp.mean(v * v, axis=-1, keepdims=True) + EPS)
        nrm = v * r
        dy = dy_ref[...].astype(F32)
        dn = dy * g_ref[...]
        dh = dres_ref[...] + r * (dn - nrm * jnp.mean(dn * nrm, axis=-1, keepdims=True))
        dh_ref[...] = dh
        if want_bf16:
            dhb_ref[...] = dh.astype(BF16)

        @pl.when(i == 0)
        def _():
            acc[...] = jnp.zeros_like(acc)

        acc[...] += _fold8(dy * nrm)

        @pl.when(i == nt - 1)
        def _():
            dg_ref[...] = jnp.sum(acc[...], axis=0, keepdims=True)

    tile = pl.BlockSpec((tr, D), lambda i: (i, 0))
    vec = pl.BlockSpec((1, D), lambda i: (0, 0))
    out_specs = [tile] + ([tile] if want_bf16 else []) + [vec]
    out_shape = ([jax.ShapeDtypeStruct((T, D), F32)]
                 + ([jax.ShapeDtypeStruct((T, D), BF16)] if want_bf16 else [])
                 + [jax.ShapeDtypeStruct((1, D), F32)])
    return pl.pallas_call(
        body, name=name, grid=(nt,),
        in_specs=[tile, tile, vec, tile], out_specs=out_specs, out_shape=out_shape,
        scratch_shapes=[pltpu.VMEM((8, D), F32)],
        compiler_params=pltpu.CompilerParams(dimension_semantics=("arbitrary",)),
    )(dn_out, h, g, dres)


def _loss_bwd(h3, target, g_final, pe, gte):
    T = h3.shape[0]
    tr = _rows(T)
    nt = T // tr

    def body(h_ref, t_ref, g_ref, pe_ref, gte_ref, loss_ref, dh_ref, dpe_ref, dpg_ref,
             dgf_ref, dbp_ref, lacc, gacc, bacc):
        i = pl.program_id(0)
        v = h_ref[...]
        r = lax.rsqrt(jnp.mean(v * v, axis=-1, keepdims=True) + EPS)
        nrm = v * r
        g = g_ref[...]
        err = nrm * g - t_ref[...]
        dy = err * (1.0 / D)
        dn = dy * g
        dh = r * (dn - nrm * jnp.mean(dn * nrm, axis=-1, keepdims=True))
        dh_ref[...] = dh
        gte = gte_ref[...].astype(F32)
        pe = pe_ref[...].astype(F32)
        dpe_ref[...] = (dh * gte).astype(BF16)
        dpg = dh * pe * gte * (1.0 - gte)
        dpg_ref[...] = dpg.astype(BF16)

        @pl.when(i == 0)
        def _():
            lacc[...] = jnp.zeros_like(lacc)
            gacc[...] = jnp.zeros_like(gacc)
            bacc[...] = jnp.zeros_like(bacc)

        lacc[...] += _fold8(err * err)
        gacc[...] += _fold8(dy * nrm)
        bacc[...] += _fold8(dpg)

        @pl.when(i == nt - 1)
        def _():
            tot = jnp.sum(jnp.sum(lacc[...], axis=0, keepdims=True), axis=1, keepdims=True)
            loss_ref[...] = jnp.broadcast_to(tot * (0.5 / D), (1, 128))
            dgf_ref[...] = jnp.sum(gacc[...], axis=0, keepdims=True)
            dbp_ref[...] = jnp.sum(bacc[...], axis=0, keepdims=True)

    tile = pl.BlockSpec((tr, D), lambda i: (i, 0))
    vec = pl.BlockSpec((1, D), lambda i: (0, 0))
    return pl.pallas_call(
        body, name="loss_bwd", grid=(nt,),
        in_specs=[tile, tile, vec, tile, tile],
        out_specs=[pl.BlockSpec((1, 128), lambda i: (0, 0)), tile, tile, tile, vec, vec],
        out_shape=[jax.ShapeDtypeStruct((1, 128), F32), jax.ShapeDtypeStruct((T, D), F32),
                   jax.ShapeDtypeStruct((T, D), BF16), jax.ShapeDtypeStruct((T, D), BF16),
                   jax.ShapeDtypeStruct((1, D), F32), jax.ShapeDtypeStruct((1, D), F32)],
        scratch_shapes=[pltpu.VMEM((8, D), F32)] * 3,
        compiler_params=pltpu.CompilerParams(dimension_semantics=("arbitrary",)),
    )(h3, target, g_final, pe, gte)


def _band_masks():
    qi = lax.broadcasted_iota(jnp.int32, (BLK, BLK), 0)
    kj = lax.broadcasted_iota(jnp.int32, (BLK, BLK), 1)
    return kj >= qi, kj <= qi


def _attn_fwd(name, z, d):
    T = z.shape[0]
    tv = T // d
    nb = tv // BLK
    zv = z.reshape(tv, d * DIN)
    scale = DH ** -0.5

    def body(q_ref, kp_ref, kc_ref, vp_ref, vc_ref, o_ref, l_ref):
        n = pl.program_id(1)
        prev_ok, cur_ok = _band_masks()
        prev_ok = prev_ok & (n > 0)
        for h in range(NH):
            sl = slice(h * DH, (h + 1) * DH)
            q = q_ref[:, sl]
            sp = jnp.where(prev_ok, _dot(q, kp_ref[:, sl], NT) * scale, NEG)
            sc = jnp.where(cur_ok, _dot(q, kc_ref[:, sl], NT) * scale, NEG)
            m = jnp.maximum(jnp.max(sp, axis=1, keepdims=True), jnp.max(sc, axis=1, keepdims=True))
            pp = jnp.exp(sp - m)
            pc = jnp.exp(sc - m)
            den = jnp.sum(pp, axis=1, keepdims=True) + jnp.sum(pc, axis=1, keepdims=True)
            o = _dot(pp.astype(BF16), vp_ref[:, sl], NN) + _dot(pc.astype(BF16), vc_ref[:, sl], NN)
            o_ref[:, sl] = (o / den).astype(BF16)
            l_ref[:, sl] = jnp.broadcast_to(m + jnp.log(den), (BLK, DH))

    blk = (BLK, DA)
    prev = lambda n: jnp.maximum(n - 1, 0)
    o, l = pl.pallas_call(
        body, name=name, grid=(d, nb),
        in_specs=[pl.BlockSpec(blk, lambda r, n: (n, 5 * r)),
                  pl.BlockSpec(blk, lambda r, n: (prev(n), 5 * r + 1)),
                  pl.BlockSpec(blk, lambda r, n: (n, 5 * r + 1)),
                  pl.BlockSpec(blk, lambda r, n: (prev(n), 5 * r + 2)),
                  pl.BlockSpec(blk, lambda r, n: (n, 5 * r + 2))],
        out_specs=[pl.BlockSpec(blk, lambda r, n: (n, r))] * 2,
        out_shape=[jax.ShapeDtypeStruct((tv, d * DA), BF16), jax.ShapeDtypeStruct((tv, d * DA), F32)],
        compiler_params=pltpu.CompilerParams(dimension_semantics=("parallel", "parallel")),
    )(zv, zv, zv, zv, zv)
    return o.reshape(T, DA), l.reshape(T, DA)


def _attn_combine(outs, lses):
    T = outs[0].shape[0]
    tr = _rows(T)

    def body(o1, o2, o3, l1, l2, l3, o_ref, l_ref):
        a, b, c = l1[...], l2[...], l3[...]
        m = jnp.maximum(jnp.maximum(a, b), c)
        ea, eb, ec = jnp.exp(a - m), jnp.exp(b - m), jnp.exp(c - m)
        s = ea + eb + ec
        o = (ea * o1[...].astype(F32) + eb * o2[...].astype(F32) + ec * o3[...].astype(F32)) / s
        o_ref[...] = o.astype(BF16)
        l_ref[...] = m + jnp.log(s)

    tile = pl.BlockSpec((tr, DA), lambda i: (i, 0))
    return pl.pallas_call(
        body, name="attn_combine", grid=(T // tr,),
        in_specs=[tile] * 6, out_specs=[tile, tile],
        out_shape=[jax.ShapeDtypeStruct((T, DA), BF16), jax.ShapeDtypeStruct((T, DA), F32)],
        compiler_params=pltpu.CompilerParams(dimension_semantics=("parallel",)),
    )(*outs, *lses)


def _attn_bwd_q(name, z, dom, o, lse, d, prev_dq, out_dtype):
    T = z.shape[0]
    tv = T // d
    nb = tv // BLK
    zv = z.reshape(tv, d * DIN)
    dov = dom.reshape(tv, d * D)
    ov = o.reshape(tv, d * DA)
    lv = lse.reshape(tv, d * DA)
    scale = DH ** -0.5
    has_prev = prev_dq is not None

    def body(q_ref, kp_ref, kc_ref, vp_ref, vc_ref, do_ref, o_ref, l_ref, *rest):
        if has_prev:
            acc_ref, dq_ref = rest
        else:
            (dq_ref,) = rest
        n = pl.program_id(1)
        prev_ok, cur_ok = _band_masks()
        prev_ok = prev_ok & (n > 0)
        for h in range(NH):
            sl = slice(h * DH, (h + 1) * DH)
            q = q_ref[:, sl]
            kp, kc = kp_ref[:, sl], kc_ref[:, sl]
            do = do_ref[:, sl]
            lrow = l_ref[:, sl]
            pp = jnp.exp(jnp.where(prev_ok, _dot(q, kp, NT) * scale - lrow, NEG))
            pc = jnp.exp(jnp.where(cur_ok, _dot(q, kc, NT) * scale - lrow, NEG))
            delta = jnp.sum(do.astype(F32) * o_ref[:, sl].astype(F32), axis=1, keepdims=True)
            dsp = (pp * (_dot(do, vp_ref[:, sl], NT) - delta) * scale).astype(BF16)
            dsc = (pc * (_dot(do, vc_ref[:, sl], NT) - delta) * scale).astype(BF16)
            dq = _dot(dsp, kp, NN) + _dot(dsc, kc, NN)
            if has_prev:
                dq = dq + acc_ref[:, sl]
            dq_ref[:, sl] = dq.astype(out_dtype)

    blk = (BLK, DA)
    prev = lambda n: jnp.maximum(n - 1, 0)
    own = pl.BlockSpec(blk, lambda r, n: (n, r))
    in_specs = [pl.BlockSpec(blk, lambda r, n: (n, 5 * r)),
                pl.BlockSpec(blk, lambda r, n: (prev(n), 5 * r + 1)),
                pl.BlockSpec(blk, lambda r, n: (n, 5 * r + 1)),
                pl.BlockSpec(blk, lambda r, n: (prev(n), 5 * r + 2)),
                pl.BlockSpec(blk, lambda r, n: (n, 5 * r + 2)),
                pl.BlockSpec(blk, lambda r, n: (n, 2 * r)),
                own, own]
    operands = [zv, zv, zv, zv, zv, dov, ov, lv]
    if has_prev:
        in_specs.append(own)
        operands.append(prev_dq.reshape(tv, d * DA))
    dq = pl.pallas_call(
        body, name=name, grid=(d, nb), in_specs=in_specs, out_specs=own,
        out_shape=jax.ShapeDtypeStruct((tv, d * DA), out_dtype),
        compiler_params=pltpu.CompilerParams(dimension_semantics=("parallel", "parallel")),
    )(*operands)
    return dq.reshape(T, DA)


def _attn_bwd_kv(name, z, dom, o, lse, d, prev_dk, prev_dv, out_dtype):
    T = z.shape[0]
    tv = T // d
    nb = tv // BLK
    zv = z.reshape(tv, d * DIN)
    dov = dom.reshape(tv, d * D)
    ov = o.reshape(tv, d * DA)
    lv = lse.reshape(tv, d * DA)
    scale = DH ** -0.5
    has_prev = prev_dk is not None

    def body(k_ref, v_ref, qa_ref, qb_ref, doa_ref, dob_ref, oa_ref, ob_ref, la_ref, lb_ref, *rest):
        if has_prev:
            pk_ref, pv_ref, dk_ref, dv_ref = rest
        else:
            dk_ref, dv_ref = rest
        j = pl.program_id(1)
        next_ok, own_ok = _band_masks()
        next_ok = next_ok & (j < nb - 1)
        for h in range(NH):
            sl = slice(h * DH, (h + 1) * DH)
            k, v = k_ref[:, sl], v_ref[:, sl]
            qa, qb = qa_ref[:, sl], qb_ref[:, sl]
            doa, dob = doa_ref[:, sl], dob_ref[:, sl]
            pa = jnp.exp(jnp.where(own_ok, _dot(qa, k, NT) * scale - la_ref[:, sl], NEG))
            pb = jnp.exp(jnp.where(next_ok, _dot(qb, k, NT) * scale - lb_ref[:, sl], NEG))
            da = jnp.sum(doa.astype(F32) * oa_ref[:, sl].astype(F32), axis=1, keepdims=True)
            db = jnp.sum(dob.astype(F32) * ob_ref[:, sl].astype(F32), axis=1, keepdims=True)
            dv = _dot(pa.astype(BF16), doa, TN) + _dot(pb.astype(BF16), dob, TN)
            dsa = (pa * (_dot(doa, v, NT) - da) * scale).astype(BF16)
            dsb = (pb * (_dot(dob, v, NT) - db) * scale).astype(BF16)
            dk = _dot(dsa, qa, TN) + _dot(dsb, qb, TN)
            if has_prev:
                dk = dk + pk_ref[:, sl]
                dv = dv + pv_ref[:, sl]
            dk_ref[:, sl] = dk.astype(out_dtype)
            dv_ref[:, sl] = dv.astype(out_dtype)

    blk = (BLK, DA)
    nxt = lambda j: jnp.minimum(j + 1, nb - 1)
    own = pl.BlockSpec(blk, lambda r, j: (j, r))
    own_n = pl.BlockSpec(blk, lambda r, j: (nxt(j), r))
    in_specs = [pl.BlockSpec(blk, lambda r, j: (j, 5 * r + 1)),
                pl.BlockSpec(blk, lambda r, j: (j, 5 * r + 2)),
                pl.BlockSpec(blk, lambda r, j: (j, 5 * r)),
                pl.BlockSpec(blk, lambda r, j: (nxt(j), 5 * r)),
                pl.BlockSpec(blk, lambda r, j: (j, 2 * r)),
                pl.BlockSpec(blk, lambda r, j: (nxt(j), 2 * r)),
                own, own_n, own, own_n]
    operands = [zv, zv, zv, zv, dov, dov, ov, ov, lv, lv]
    if has_prev:
        in_specs += [own, own]
        operands += [prev_dk.reshape(tv, d * DA), prev_dv.reshape(tv, d * DA)]
    sh = jax.ShapeDtypeStruct((tv, d * DA), out_dtype)
    dk, dv = pl.pallas_call(
        body, name=name, grid=(d, nb), in_specs=in_specs, out_specs=[own, own], out_shape=[sh, sh],
        compiler_params=pltpu.CompilerParams(dimension_semantics=("parallel", "parallel")),
    )(*operands)
    return dk.reshape(T, DA), dv.reshape(T, DA)


CT = 256
HALO = 32
RC = 32


def _conv_fwd(z, w_dw, b_dw, g_ln, b_ln):
    T = z.shape[0]
    ct = min(CT, T)
    nt = T // ct
    hb = ct // HALO

    def body(cv_ref, cg_ref, cvp_ref, cgp_ref, w_ref, bdw_ref, g_ref, b_ref, oc_ref, y_ref, ubuf, ush):
        i = pl.program_id(0)
        up = cvp_ref[...].astype(F32) * _sigmoid(cgp_ref[...].astype(F32))
        ubuf[0:HALO, :] = jnp.where(i > 0, up, 0.0)
        ubuf[HALO:, :] = cv_ref[...].astype(F32) * _sigmoid(cg_ref[...].astype(F32))
        for b in range(8):
            ush[b] = ubuf[pl.ds(8 - b, ct + 24), :]

        def chunk(ci, carry):
            r0 = pl.multiple_of(ci * RC, RC)
            acc = jnp.broadcast_to(bdw_ref[...], (RC, DC))
            for s in range(CW):
                a, b = divmod(s, 8)
                acc = acc + w_ref[CW - 1 - s:CW - s, :] * ush[b, pl.ds(r0 + 24 - 8 * a, RC), :]
            y_ref[pl.ds(r0, RC), :] = acc
            mu = jnp.mean(acc, axis=-1, keepdims=True)
            cen = acc - mu
            var = jnp.mean(cen * cen, axis=-1, keepdims=True)
            ln = cen * lax.rsqrt(var + EPS) * g_ref[...] + b_ref[...]
            oc_ref[pl.ds(r0, RC), :] = (ln * _sigmoid(ln)).astype(BF16)
            return carry

        lax.fori_loop(0, ct // RC, chunk, 0)

    cur = lambda col: pl.BlockSpec((ct, DC), lambda i: (i, col))
    prv = lambda col: pl.BlockSpec((HALO, DC), lambda i: (jnp.maximum(i * hb - 1, 0), col))
    vec = pl.BlockSpec((1, DC), lambda i: (0, 0))
    return pl.pallas_call(
        body, name="conv_fwd", grid=(nt,),
        in_specs=[cur(3), cur(4), prv(3), prv(4), pl.BlockSpec((CW, DC), lambda i: (0, 0)),
                  vec, vec, vec],
        out_specs=[pl.BlockSpec((ct, DC), lambda i: (i, 0))] * 2,
        out_shape=[jax.ShapeDtypeStruct((T, DC), BF16), jax.ShapeDtypeStruct((T, DC), F32)],
        scratch_shapes=[pltpu.VMEM((ct + HALO, DC), F32), pltpu.VMEM((8, ct + 24, DC), F32)],
        compiler_params=pltpu.CompilerParams(
            dimension_semantics=("parallel",), vmem_limit_bytes=VMEM_MID),
    )(z, z, z, z, w_dw, b_dw, g_ln, b_ln)


def _conv_bwd(z, dom, y, w_dw, g_ln, b_ln):
    T = z.shape[0]
    ct = min(CT, T)
    nt = T // ct
    hb = ct // HALO
    last_halo = T // HALO - 1

    def ln_bwd(yv, dov, g_ref, b_ref):
        mu = jnp.mean(yv, axis=-1, keepdims=True)
        cen = yv - mu
        rstd = lax.rsqrt(jnp.mean(cen * cen, axis=-1, keepdims=True) + EPS)
        xhat = cen * rstd
        ln = xhat * g_ref[...] + b_ref[...]
        sg = _sigmoid(ln)
        dln = dov * (sg * (1.0 + ln * (1.0 - sg)))
        dxh = dln * g_ref[...]
        dy = rstd * (dxh - jnp.mean(dxh, axis=-1, keepdims=True)
                     - xhat * jnp.mean(dxh * xhat, axis=-1, keepdims=True))
        return dy, dln, xhat

    def body(do_ref, don_ref, y_ref, yn_ref, cv_ref, cg_ref, cvp_ref, cgp_ref, w_ref, g_ref, b_ref,
             dcv_ref, dcg_ref, dw_ref, dbdw_ref, dg_ref, db_ref,
             dybuf, dysh, ubuf, ush, dwacc, vacc):
        i = pl.program_id(0)

        @pl.when(i == 0)
        def _():
            dwacc[...] = jnp.zeros_like(dwacc)
            vacc[...] = jnp.zeros_like(vacc)

        def ln_chunk(ci, carry):
            r0 = pl.multiple_of(ci * RC, RC)
            dy, dln, xhat = ln_bwd(y_ref[pl.ds(r0, RC), :], do_ref[pl.ds(r0, RC), :].astype(F32),
                                   g_ref, b_ref)
            dybuf[pl.ds(r0, RC), :] = dy
            vacc[0] += _fold8(dy)
            vacc[1] += _fold8(dln * xhat)
            vacc[2] += _fold8(dln)
            return carry

        lax.fori_loop(0, ct // RC, ln_chunk, 0)
        dyn, _, _ = ln_bwd(yn_ref[...], don_ref[...].astype(F32), g_ref, b_ref)
        dybuf[ct:, :] = jnp.where(i < nt - 1, dyn, 0.0)
        for b in range(8):
            dysh[b] = dybuf[pl.ds(b, ct + 24), :]

        up = cvp_ref[...].astype(F32) * _sigmoid(cgp_ref[...].astype(F32))
        ubuf[0:HALO, :] = jnp.where(i > 0, up, 0.0)
        ubuf[HALO:, :] = cv_ref[...].astype(F32) * _sigmoid(cg_ref[...].astype(F32))
        for b in range(8):
            ush[b] = ubuf[pl.ds(8 - b, ct + 24), :]

        def chunk(ci, carry):
            r0 = pl.multiple_of(ci * RC, RC)
            dy = dybuf[pl.ds(r0, RC), :]
            du = jnp.zeros((RC, DC), F32)
            for s in range(CW):
                a, b = divmod(s, 8)
                du = du + w_ref[CW - 1 - s:CW - s, :] * dysh[b, pl.ds(r0 + 8 * a, RC), :]
                dwacc[CW - 1 - s] += _fold8(dy * ush[b, pl.ds(r0 + 24 - 8 * a, RC), :])
            cv = cv_ref[pl.ds(r0, RC), :].astype(F32)
            sg = _sigmoid(cg_ref[pl.ds(r0, RC), :].astype(F32))
            dcv_ref[pl.ds(r0, RC), :] = (du * sg).astype(BF16)
            dcg_ref[pl.ds(r0, RC), :] = (du * cv * sg * (1.0 - sg)).astype(BF16)
            return carry

        lax.fori_loop(0, ct // RC, chunk, 0)

        @pl.when(i == nt - 1)
        def _():
            dw_ref[...] = jnp.sum(dwacc[...], axis=1)
            dbdw_ref[...] = jnp.sum(vacc[0], axis=0, keepdims=True)
            dg_ref[...] = jnp.sum(vacc[1], axis=0, keepdims=True)
            db_ref[...] = jnp.sum(vacc[2], axis=0, keepdims=True)

    cur = lambda col: pl.BlockSpec((ct, DC), lambda i: (i, col))
    prv = lambda col: pl.BlockSpec((HALO, DC), lambda i: (jnp.maximum(i * hb - 1, 0), col))
    nxt = lambda col: pl.BlockSpec((HALO, DC), lambda i: (jnp.minimum((i + 1) * hb, last_halo), col))
    vec = pl.BlockSpec((1, DC), lambda i: (0, 0))
    tile = pl.BlockSpec((ct, DC), lambda i: (i, 0))
    return pl.pallas_call(
        body, name="conv_bwd", grid=(nt,),
        in_specs=[cur(1), nxt(1), cur(0), nxt(0), cur(3), cur(4), prv(3), prv(4),
                  pl.BlockSpec((CW, DC), lambda i: (0, 0)), vec, vec],
        out_specs=[tile, tile, pl.BlockSpec((CW, DC), lambda i: (0, 0)), vec, vec, vec],
        out_shape=[jax.ShapeDtypeStruct((T, DC), BF16), jax.ShapeDtypeStruct((T, DC), BF16),
                   jax.ShapeDtypeStruct((CW, DC), F32), jax.ShapeDtypeStruct((1, DC), F32),
                   jax.ShapeDtypeStruct((1, DC), F32), jax.ShapeDtypeStruct((1, DC), F32)],
        scratch_shapes=[pltpu.VMEM((ct + HALO, DC), F32), pltpu.VMEM((8, ct + 24, DC), F32),
                        pltpu.VMEM((ct + HALO, DC), F32), pltpu.VMEM((8, ct + 24, DC), F32),
                        pltpu.VMEM((CW, 8, DC), F32), pltpu.VMEM((3, 8, DC), F32)],
        compiler_params=pltpu.CompilerParams(
            dimension_semantics=("arbitrary",), vmem_limit_bytes=VMEM_BIG),
    )(dom, dom, y, y, z, z, z, z, w_dw, g_ln, b_ln)


def _adam_math(w, g, m, v):
    m = ADAM_B1 * m + (1.0 - ADAM_B1) * g
    v = ADAM_B2 * v + (1.0 - ADAM_B2) * (g * g)
    m_hat = m / (1.0 - ADAM_B1 ** ADAM_STEP)
    v_hat = v / (1.0 - ADAM_B2 ** ADAM_STEP)
    delta = -ADAM_LR * (m_hat / (jnp.sqrt(v_hat) + ADAM_EPS) + ADAM_WD * w)
    return delta, m, v


def _adam(name, slots, w, m, v):
    rows, cols = w.shape
    tr = next(t for t in (256, 176, 128, 64, 32, 16, 8, rows) if rows % t == 0)

    def body(s_ref, w_ref, m_ref, v_ref, g_out, d_out, m_out, v_out):
        g = s_ref[0].astype(F32)
        for s in range(1, NDEV):
            g = g + s_ref[s].astype(F32)
        delta, mn, vn = _adam_math(w_ref[...], g, m_ref[...], v_ref[...])
        g_out[...] = g
        d_out[...] = delta
        m_out[...] = mn
        v_out[...] = vn

    tile = pl.BlockSpec((tr, cols), lambda i: (i, 0))
    sh = jax.ShapeDtypeStruct((rows, cols), F32)
    return pl.pallas_call(
        body, name=name, grid=(rows // tr,),
        in_specs=[pl.BlockSpec((NDEV, tr, cols), lambda i: (0, i, 0)), tile, tile, tile],
        out_specs=[tile] * 4, out_shape=[sh] * 4,
        compiler_params=pltpu.CompilerParams(
            dimension_semantics=("parallel",), vmem_limit_bytes=VMEM_MID),
    )(slots, w, m, v)


SMALL_NAMES = ("g_mix", "b_dw", "g_conv_ln", "b_conv_ln", "g_ffn", "g_ple", "b_pgate", "g_final")


def _pack_small(vecs, w_dw_full):
    rows = [jnp.pad(v.reshape(1, -1), ((0, 0), (0, SMALL_W - v.size))) for v in vecs]
    rows.append(jnp.pad(w_dw_full, ((0, 0), (0, SMALL_W - DC))))
    rows.append(jnp.zeros((SMALL_ROWS - len(vecs) - CW, SMALL_W), F32))
    return jnp.concatenate(rows, axis=0)


def kernel(x, p, g_mix, w_in, w_dw, b_dw, g_conv_ln, b_conv_ln, w_out, g_ffn, w_gate, w_up, w_down, g_ple, w_pgate, b_pgate, w_ple, g_final, loss_target, m_g_mix, m_w_in, m_w_dw, m_b_dw, m_g_conv_ln, m_b_conv_ln, m_w_out, m_g_ffn, m_w_gate, m_w_up, m_w_down, m_g_ple, m_w_pgate, m_b_pgate, m_w_ple, m_g_final, v_g_mix, v_w_in, v_w_dw, v_b_dw, v_g_conv_ln, v_b_conv_ln, v_w_out, v_g_ffn, v_w_gate, v_w_up, v_w_down, v_g_ple, v_w_pgate, v_b_pgate, v_w_ple, v_g_final):
    T = x.shape[1]
    me = 4 * lax.axis_index("x") + 2 * lax.axis_index("y") + lax.axis_index("c")
    xs = x.reshape(T, D)
    ps = p.reshape(T, DPLE).astype(BF16)
    tgt = loss_target.reshape(T, D)
    g_final2 = g_final.reshape(1, D)

    big = dict(w_in=w_in[0], w_out=w_out[0], w_gate=w_gate[0], w_up=w_up[0], w_down=w_down[0],
               w_pgate=w_pgate[0], w_ple=w_ple[0])
    order = ("w_in", "w_out", "w_gate", "w_up", "w_down", "w_pgate", "w_ple")
    g_order = ("w_dw",) + order
    g_items = [(w_dw.reshape(CW, DC // NDEV), False)] + [(big[n].astype(BF16), False) for n in order]
    g_handles, g_token = _xstart("gather_start", g_items, _place("gather_place", g_items))
    G = dict(zip(g_order, g_handles))

    a = _rms_fwd("rms_mix", xs, g_mix, deps=[g_token])
    w_dw_f = _xwait("gather_wait_w_dw", G["w_dw"], a).transpose(1, 0, 2).reshape(CW, DC)
    w_in_f = _xwait("gather_wait_w_in", G["w_in"], a)
    z = _mm_in(a, w_in_f)
    br = [_attn_fwd(f"attn_fwd_d{d}", z, d) for d in DILATIONS]
    o_attn, lse = _attn_combine([b[0] for b in br], [b[1] for b in br])
    o_conv, y_conv = _conv_fwd(z, w_dw_f, b_dw, g_conv_ln, b_conv_ln)
    w_out_f = _xwait("gather_wait_w_out", G["w_out"], o_conv).reshape(D, D)
    h1 = _mm_out(o_attn, o_conv, w_out_f, xs)
    f = _rms_fwd("rms_ffn", h1, g_ffn)
    w_gate_f = _xwait("gather_wait_w_gate", G["w_gate"], f)
    w_up_f = _xwait("gather_wait_w_up", G["w_up"], f)
    gate, up, act = _mm_gate_up(f, w_gate_f, w_up_f)
    w_down_f = _xwait("gather_wait_w_down", G["w_down"], act)
    h2 = _mm_down(act, w_down_f, h1)
    r = _rms_fwd("rms_ple", h2, g_ple)
    w_pgate_f = _xwait("gather_wait_w_pgate", G["w_pgate"], r).reshape(D, D)
    w_ple_f = _xwait("gather_wait_w_ple", G["w_ple"], r).transpose(1, 0, 2).reshape(DPLE, D)
    gte, pe, h3 = _mm_ple(r, w_pgate_f, b_pgate, ps, w_ple_f, h2)

    loss_part, dh3, dpe, dpg, d_g_final, d_b_pgate = _loss_bwd(h3, tgt, g_final2, pe, gte)
    H = {}

    def send_grads(tag, named):
        items = [(v, True) for _, v in named]
        handles, token = _xstart(f"grads_start_{tag}", items, _place(f"grads_place_{tag}", items))
        H.update(zip([n for n, _ in named], handles))
        return token

    gw_pgate = _mm_tn("gw_pgate", r, dpg).reshape(NDEV, D // NDEV, D)
    gw_ple = _mm_tn("gw_ple", ps, dpe).reshape(DPLE, NDEV, D // NDEV).transpose(1, 0, 2)
    tok = send_grads("ple", [("w_pgate", gw_pgate), ("w_ple", gw_ple)])
    dr = _mm_nt("mm_pgate_bwd", dpg, w_pgate_f, deps=[tok])
    dh2, dh2b, d_g_ple = _rms_bwd("rms_ple_bwd", dr, h2, g_ple, dh3, True)
    gw_down = _mm_tn_ff_rows("gw_down", act, dh2b)
    tok = send_grads("down", [("w_down", gw_down)])
    dgate, dup = _mm_down_bwd(dh2b, w_down_f, gate, up, deps=[tok])
    gw_gate = _mm_tn_ff_cols("gw_gate", f, dgate)
    gw_up = _mm_tn_ff_cols("gw_up", f, dup)
    tok = send_grads("ffn", [("w_gate", gw_gate), ("w_up", gw_up)])
    df = _mm_ffn_in_bwd(dgate, w_gate_f, dup, w_up_f, deps=[tok])
    dh1, dh1b, d_g_ffn = _rms_bwd("rms_ffn_bwd", df, h1, g_ffn, dh2, True)
    gw_out = jnp.concatenate(
        [_mm_tn("gw_out_attn", o_attn, dh1b), _mm_tn("gw_out_conv", o_conv, dh1b)], axis=0)
    tok = send_grads("out", [("w_out", gw_out.reshape(NDEV, D // NDEV, D))])
    dom = _mm_nt("mm_out_bwd", dh1b, w_out_f, deps=[tok])
    dcv, dcg, d_w_dw, d_b_dw, d_g_ln, d_b_ln = _conv_bwd(z, dom, y_conv, w_dw_f, g_conv_ln, b_conv_ln)
    dq = dk = dv = None
    for bi, d in enumerate(DILATIONS):
        dt = BF16 if bi == len(DILATIONS) - 1 else F32
        dq = _attn_bwd_q(f"attn_bwd_q_d{d}", z, dom, o_attn, lse, d, dq, dt)
        dk, dv = _attn_bwd_kv(f"attn_bwd_kv_d{d}", z, dom, o_attn, lse, d, dk, dv, dt)
    dz = jnp.concatenate([dq, dk, dv, dcv, dcg], axis=1)
    gw_in = _mm_tn_cols("gw_in", a, dz, N_IN)
    tok = send_grads("in", [("w_in", gw_in)])
    da = _mm_in_bwd(dz, w_in_f, deps=[tok])
    grad_x, d_g_mix = _rms_bwd("rms_mix_bwd", da, xs, g_mix, dh1, False)

    small_part = _pack_small(
        [d_g_mix, d_b_dw, d_g_ln, d_b_ln, d_g_ffn, d_g_ple, d_b_pgate, d_g_final], d_w_dw)
    small_slots = _exchange("exchange_small_grads", [(small_part, False)])[0]
    S = {n: _xwait(f"grads_wait_{n}", H[n], small_slots)
         for n in ("w_pgate", "w_ple", "w_down", "w_gate", "w_up", "w_out", "w_in")}

    mom = dict(w_in=(m_w_in, v_w_in), w_out=(m_w_out, v_w_out), w_gate=(m_w_gate, v_w_gate),
               w_up=(m_w_up, v_w_up), w_down=(m_w_down, v_w_down), w_pgate=(m_w_pgate, v_w_pgate),
               w_ple=(m_w_ple, v_w_ple))
    upd = {}
    for n in order:
        res = _adam(f"adam_{n}", S[n], big[n], mom[n][0][0], mom[n][1][0])
        upd[n] = [t[None] for t in res]

    def lanes(v):
        full = jnp.zeros((CW, NDEV, DC // NDEV), F32)
        full = lax.dynamic_update_slice(full, v.reshape(CW, 1, DC // NDEV), (0, me, 0))
        return full.reshape(CW, DC)

    small_w = _pack_small([g_mix, b_dw, g_conv_ln, b_conv_ln, g_ffn, g_ple, b_pgate, g_final2], lanes(w_dw))
    small_m = _pack_small([m_g_mix, m_b_dw, m_g_conv_ln, m_b_conv_ln, m_g_ffn, m_g_ple, m_b_pgate,
                           m_g_final.reshape(1, D)], lanes(m_w_dw))
    small_v = _pack_small([v_g_mix, v_b_dw, v_g_conv_ln, v_b_conv_ln, v_g_ffn, v_g_ple, v_b_pgate,
                           v_g_final.reshape(1, D)], lanes(v_w_dw))
    small_res = _adam("adam_small", small_slots, small_w, small_m, small_v)

    def unpack(t):
        out = {}
        widths = dict(g_mix=D, b_dw=DC, g_conv_ln=DC, b_conv_ln=DC, g_ffn=D, g_ple=D, b_pgate=D, g_final=D)
        for i, n in enumerate(SMALL_NAMES):
            out[n] = t[i:i + 1, :widths[n]]
        out["g_final"] = out["g_final"].reshape(D)
        taps = t[len(SMALL_NAMES):len(SMALL_NAMES) + CW, :DC].reshape(CW, NDEV, DC // NDEV)
        out["w_dw"] = lax.dynamic_slice(taps, (0, me, 0), (CW, 1, DC // NDEV))[None]
        return out

    small = [unpack(t) for t in small_res]

    loss = lax.psum(loss_part[0, 0], ("x", "y", "c"))
    names = ("g_mix", "w_in", "w_dw", "b_dw", "g_conv_ln", "b_conv_ln", "w_out", "g_ffn", "w_gate",
             "w_up", "w_down", "g_ple", "w_pgate", "b_pgate", "w_ple", "g_final")
    outs = [loss, grad_x.reshape(1, T, D)]
    for kind in range(4):
        for n in names:
            outs.append(upd[n][kind] if n in upd else small[kind][n])
    return tuple(outs)
```

```python
import jax
import jax.numpy as jnp
from jax import lax
from jax.experimental import pallas as pl
from jax.experimental.pallas import tpu as pltpu

F32 = jnp.float32
BF16 = jnp.bfloat16

NDEV = 8
D = 2048
NH = 8
DH = 128
DA = NH * DH
DC = D - DA
DIN = 3 * DA + 2 * DC
DFF = 5632
DPLE = 256
BLK = 128
DILATIONS = (1, 4, 16)
CW = 31
EPS = 1e-6
N_IN = DIN // NDEV
N_FF = DFF // NDEV
NEG = -1e30

ADAM_LR = 0.001
ADAM_B1 = 0.9
ADAM_B2 = 0.999
ADAM_EPS = 1e-08
ADAM_WD = 0.01
ADAM_STEP = 10

VMEM_CAP_V7X = 64 * 1024 * 1024
VMEM_BIG = VMEM_CAP_V7X - 12 * 1024 * 1024
VMEM_MID = 40 * 1024 * 1024

SMALL_W = 2048
SMALL_ROWS = 40


def _sigmoid(v):
    return 1.0 / (1.0 + jnp.exp(-v))


def _dot(a, b, contract):
    return lax.dot_general(a, b, (contract, ((), ())), preferred_element_type=F32)


NN = ((1,), (0,))
NT = ((1,), (1,))
TN = ((0,), (0,))


def _exchange(name, items):
    n = len(items)
    out_shape = [
        jax.ShapeDtypeStruct((NDEV,) + (a.shape[1:] if sc else a.shape), a.dtype)
        for a, sc in items
    ]
    scat = [sc for _, sc in items]

    def body(*refs):
        srcs = refs[:n]
        dsts = refs[n:2 * n]
        send_sems, recv_sems, loc_sems = refs[2 * n:]
        x = lax.axis_index("x")
        y = lax.axis_index("y")
        c = lax.axis_index("c")
        me = 4 * x + 2 * y + c

        local = []
        for i in range(n):
            src = srcs[i].at[me] if scat[i] else srcs[i]
            cp = pltpu.make_async_copy(src, dsts[i].at[me], loc_sems.at[i])
            cp.start()
            local.append(cp)

        remote = []
        for k in range(1, NDEV):
            px = (1 - x) if (k >> 2) & 1 else x
            py = (1 - y) if (k >> 1) & 1 else y
            pc = (1 - c) if k & 1 else c
            peer = 4 * px + 2 * py + pc
            for i in range(n):
                sem = i * (NDEV - 1) + k - 1
                src = srcs[i].at[peer] if scat[i] else srcs[i]
                send = pltpu.make_async_remote_copy(
                    src_ref=src, dst_ref=dsts[i].at[me],
                    send_sem=send_sems.at[sem], recv_sem=recv_sems.at[sem],
                    device_id=(px, py, pc), device_id_type=pl.DeviceIdType.MESH)
                send.start()
                recv = pltpu.make_async_remote_copy(
                    src_ref=src, dst_ref=dsts[i].at[peer],
                    send_sem=send_sems.at[sem], recv_sem=recv_sems.at[sem],
                    device_id=(px, py, pc), device_id_type=pl.DeviceIdType.MESH)
                remote.append((send, recv))
        for send, recv in remote:
            recv.wait_recv()
            send.wait_send()
        for cp in local:
            cp.wait()

    any_spec = pl.BlockSpec(memory_space=pl.ANY)
    return pl.pallas_call(
        body, name=name,
        in_specs=[any_spec] * n, out_specs=[any_spec] * n, out_shape=out_shape,
        scratch_shapes=[
            pltpu.SemaphoreType.DMA((n * (NDEV - 1),)),
            pltpu.SemaphoreType.DMA((n * (NDEV - 1),)),
            pltpu.SemaphoreType.DMA((n,)),
        ],
    )(*[a for a, _ in items])


HBM_SPEC = pl.BlockSpec(memory_space=pltpu.HBM)
SEM_SPEC = pl.BlockSpec(memory_space=pltpu.SEMAPHORE)
ANY_SPEC = pl.BlockSpec(memory_space=pl.ANY)
EFFECT = pltpu.SideEffectType.DATAFLOW_SIDE_EFFECTING


def _peer_of(k):
    x = lax.axis_index("x")
    y = lax.axis_index("y")
    c = lax.axis_index("c")
    px = (1 - x) if (k >> 2) & 1 else x
    py = (1 - y) if (k >> 1) & 1 else y
    pc = (1 - c) if k & 1 else c
    return (px, py, pc), 4 * px + 2 * py + pc


def _my_index():
    return 4 * lax.axis_index("x") + 2 * lax.axis_index("y") + lax.axis_index("c")


def _slot_shape(a, sc):
    return (NDEV,) + (a.shape[1:] if sc else a.shape)


def _divisor_tile(rows):
    return next((t for t in (512, 256, 176, 128, 64, 32, 16) if rows % t == 0), rows)


def _place(name, items):
    lands = []
    for idx, (a, sc) in enumerate(items):
        rows, cols = a.shape[-2:]
        tr = _divisor_tile(rows)

        def body(s_ref, o_ref):
            o_ref[...] = s_ref[...]

        mine = pl.BlockSpec((None, tr, cols), lambda i: (_my_index(), i, 0))
        lands.append(pl.pallas_call(
            body, name=f"{name}_{idx}", grid=(rows // tr,),
            in_specs=[mine if sc else pl.BlockSpec((tr, cols), lambda i: (i, 0))],
            out_specs=mine,
            out_shape=jax.ShapeDtypeStruct(_slot_shape(a, sc), a.dtype),
            compiler_params=pltpu.CompilerParams(dimension_semantics=("parallel",)),
        )(a))
    return lands


def _xstart(name, items, lands):
    n = len(items)
    scat = [sc for _, sc in items]

    def body(*refs):
        srcs = refs[:n]
        lzs = refs[n:2 * n]
        send_sems = refs[2 * n:3 * n]
        recv_sems = refs[3 * n:4 * n]
        token = refs[-1]
        me = _my_index()
        for i in range(n):
            for k in range(1, NDEV):
                peer_id, peer = _peer_of(k)
                src = srcs[i].at[peer] if scat[i] else srcs[i]
                pltpu.make_async_remote_copy(
                    src_ref=src, dst_ref=lzs[i].at[me],
                    send_sem=send_sems[i].at[k - 1], recv_sem=recv_sems[i].at[k - 1],
                    device_id=peer_id, device_id_type=pl.DeviceIdType.MESH).start()
        token[...] = jnp.zeros_like(token)

    sem = pltpu.SemaphoreType.DMA((NDEV - 1,))
    hbm = [pltpu.HBM(a.shape, a.dtype) for a, _ in items] + [pltpu.HBM(l.shape, l.dtype) for l in lands]
    res = pl.pallas_call(
        body, name=name,
        in_specs=[HBM_SPEC] * (2 * n),
        out_specs=[SEM_SPEC] * (2 * n) + [HBM_SPEC] * (2 * n) + [pl.BlockSpec(memory_space=pltpu.VMEM)],
        out_shape=[sem] * (2 * n) + hbm + [jax.ShapeDtypeStruct((8, 128), F32)],
        input_output_aliases={i: 2 * n + i for i in range(2 * n)},
        compiler_params=pltpu.CompilerParams(has_side_effects=EFFECT),
    )(*[pltpu.with_memory_space_constraint(a, pltpu.HBM) for a, _ in items],
      *[pltpu.with_memory_space_constraint(l, pltpu.HBM) for l in lands])
    handles = [(res[i], res[n + i], res[2 * n + i], res[3 * n + i], scat[i]) for i in range(n)]
    return handles, res[-1]


def _xwait(name, handle, after):
    send_sem, recv_sem, src, land, sc = handle

    def body(src_ref, land_ref, send_ref, recv_ref, after_ref, src_dead, got_ref):
        for k in range(1, NDEV):
            peer_id, peer = _peer_of(k)
            cp = pltpu.make_async_remote_copy(
                src_ref=src_ref.at[peer] if sc else src_ref, dst_ref=land_ref.at[peer],
                send_sem=send_ref.at[k - 1], recv_sem=recv_ref.at[k - 1],
                device_id=peer_id, device_id_type=pl.DeviceIdType.MESH)
            cp.wait_send()
            cp.wait_recv()

    return pl.pallas_call(
        body, name=name,
        in_specs=[HBM_SPEC, HBM_SPEC, SEM_SPEC, SEM_SPEC, ANY_SPEC],
        out_specs=[HBM_SPEC, HBM_SPEC],
        out_shape=[pltpu.HBM(src.shape, src.dtype), pltpu.HBM(land.shape, land.dtype)],
        input_output_aliases={0: 0, 1: 1},
        compiler_params=pltpu.CompilerParams(has_side_effects=EFFECT),
    )(src, land, send_sem, recv_sem, after)[1]


def _mm(name, grid, in_specs, operands, out_specs, out_shape, contract, n_pairs, epilogue,
        acc_shape=None, vmem=VMEM_BIG, deps=()):
    nk = grid[2]
    n_extra = len(operands) - 2 * n_pairs
    n_out = len(out_shape)
    n_in = len(operands) + len(deps)
    in_specs = list(in_specs) + [ANY_SPEC] * len(deps)
    operands = list(operands) + list(deps)

    def body(*refs):
        ab = refs[:2 * n_pairs]
        extras = refs[2 * n_pairs:2 * n_pairs + n_extra]
        outs = refs[n_in:n_in + n_out]
        part = None
        for p in range(n_pairs):
            d = _dot(ab[2 * p][...], ab[2 * p + 1][...], contract)
            part = d if part is None else part + d
        if nk == 1:
            epilogue(part, extras, outs)
        else:
            acc_ref = refs[-1]
            k = pl.program_id(2)

            @pl.when(k == 0)
            def _():
                acc_ref[...] = part

            @pl.when(k > 0)
            def _():
                acc_ref[...] += part

            @pl.when(k == nk - 1)
            def _():
                epilogue(acc_ref[...], extras, outs)

    scratch = [pltpu.VMEM(acc_shape, F32)] if nk > 1 else []
    return pl.pallas_call(
        body, name=name, grid=grid, in_specs=in_specs, out_specs=out_specs, out_shape=out_shape,
        scratch_shapes=scratch,
        compiler_params=pltpu.CompilerParams(
            dimension_semantics=("parallel", "parallel", "arbitrary"), vmem_limit_bytes=vmem),
    )(*operands)


def _ep_cast(dtype):
    def ep(acc, extras, outs):
        outs[0][...] = acc.astype(dtype)
    return ep


def _ep_resid(acc, extras, outs):
    outs[0][...] = extras[0][...] + acc


def _ep_swiglu_bwd(acc, extras, outs):
    g = extras[0][...].astype(F32)
    u = extras[1][...].astype(F32)
    sg = _sigmoid(g)
    outs[0][...] = (acc * u * (sg * (1.0 + g * (1.0 - sg)))).astype(BF16)
    outs[1][...] = (acc * (g * sg)).astype(BF16)


def _row_tile(T):
    return min(1024, T)


WIDE = tuple(d for d in DILATIONS if d > 1)


LANES = 128


def _lane_tile(c):
    return slice(c * LANES, (c + 1) * LANES)


def _to_lane_tiles(scr, val):
    for c in range(scr.shape[0]):
        scr[c] = val[:, _lane_tile(c)]


def _emit_class_major(scr, refs, rows):
    for d, ref in zip(WIDE, refs):
        for r in range(d):
            for c in range(scr.shape[0]):
                ref[r, :, _lane_tile(c)] = scr[c, pl.ds(r, rows // d, stride=d), :].astype(ref.dtype)


def _mm_in(a, w_in):
    T = a.shape[0]
    tm = _row_tile(T)
    nq = -(-3 * DA // N_IN)

    def body(a_ref, w_ref, z_ref, *rest):
        scr = rest[-1]
        j = pl.program_id(1)
        acc = _dot(a_ref[...], w_ref[...], NN)
        z_ref[...] = acc.astype(BF16)

        @pl.when(j < nq)
        def _():
            _to_lane_tiles(scr, acc)
            _emit_class_major(scr, rest[:-1], tm)

    cm_spec = lambda d: pl.BlockSpec((d, tm // d, N_IN), lambda i, j: (0, i, jnp.minimum(j, nq - 1)))
    return pl.pallas_call(
        body, name="mm_in", grid=(T // tm, NDEV),
        in_specs=[pl.BlockSpec((tm, D), lambda i, j: (i, 0)),
                  pl.BlockSpec((None, D, N_IN), lambda i, j: (j, 0, 0))],
        out_specs=[pl.BlockSpec((tm, N_IN), lambda i, j: (i, j))] + [cm_spec(d) for d in WIDE],
        out_shape=[jax.ShapeDtypeStruct((T, DIN), BF16)]
        + [jax.ShapeDtypeStruct((d, T // d, nq * N_IN), BF16) for d in WIDE],
        scratch_shapes=[pltpu.VMEM((N_IN // LANES, tm, LANES), F32)],
        compiler_params=pltpu.CompilerParams(
            dimension_semantics=("parallel", "arbitrary"), vmem_limit_bytes=VMEM_BIG),
    )(a, w_in)


def _mm_out_bwd(dh1b, w_out, deps=()):
    T = dh1b.shape[0]
    tm = _row_tile(T)

    def body(dy_ref, w_ref, *rest):
        rest = rest[len(deps):]
        dom_ref, scr = rest[0], rest[-1]
        acc = _dot(dy_ref[...], w_ref[...], NT)
        dom_ref[...] = acc.astype(BF16)

        @pl.when(pl.program_id(1) == 0)
        def _():
            _to_lane_tiles(scr, acc)
            _emit_class_major(scr, rest[1:-1], tm)

    return pl.pallas_call(
        body, name="mm_out_bwd", grid=(T // tm, D // DA),
        in_specs=[pl.BlockSpec((tm, D), lambda i, j: (i, 0)),
                  pl.BlockSpec((DA, D), lambda i, j: (j, 0))] + [ANY_SPEC] * len(deps),
        out_specs=[pl.BlockSpec((tm, DA), lambda i, j: (i, j))]
        + [pl.BlockSpec((d, tm // d, DA), lambda i, j: (0, i, 0)) for d in WIDE],
        out_shape=[jax.ShapeDtypeStruct((T, D), BF16)]
        + [jax.ShapeDtypeStruct((d, T // d, DA), BF16) for d in WIDE],
        scratch_shapes=[pltpu.VMEM((DA // LANES, tm, LANES), F32)],
        compiler_params=pltpu.CompilerParams(
            dimension_semantics=("parallel", "arbitrary"), vmem_limit_bytes=VMEM_BIG),
    )(dh1b, w_out, *deps)


def _mm_out(o_attn, o_conv, w_out, x):
    T = x.shape[0]
    tm = _row_tile(T)
    tn = 1024
    return _mm(
        "mm_out", (T // tm, D // tn, 1),
        [pl.BlockSpec((tm, DA), lambda i, j, k: (i, 0)),
         pl.BlockSpec((DA, tn), lambda i, j, k: (0, j)),
         pl.BlockSpec((tm, DC), lambda i, j, k: (i, 0)),
         pl.BlockSpec((DC, tn), lambda i, j, k: (1, j)),
         pl.BlockSpec((tm, tn), lambda i, j, k: (i, j))],
        [o_attn, w_out, o_conv, w_out, x],
        [pl.BlockSpec((tm, tn), lambda i, j, k: (i, j))],
        [jax.ShapeDtypeStruct((T, D), F32)], NN, 2, _ep_resid)[0]


def _mm_gate_up(f, w_gate, w_up):
    T = f.shape[0]
    tm = _row_tile(T)

    def body(f_ref, wg_ref, wu_ref, g_ref, u_ref, a_ref):
        fv = f_ref[...]
        g = _dot(fv, wg_ref[...], NN)
        u = _dot(fv, wu_ref[...], NN)
        g_ref[...] = g.astype(BF16)
        u_ref[...] = u.astype(BF16)
        a_ref[...] = (g * _sigmoid(g) * u).astype(BF16)

    wspec = pl.BlockSpec((None, D, N_FF), lambda i, j: (j, 0, 0))
    ospec = pl.BlockSpec((None, tm, N_FF), lambda i, j: (j, i, 0))
    sh = jax.ShapeDtypeStruct((NDEV, T, N_FF), BF16)
    return pl.pallas_call(
        body, name="mm_gate_up", grid=(T // tm, NDEV),
        in_specs=[pl.BlockSpec((tm, D), lambda i, j: (i, 0)), wspec, wspec],
        out_specs=[ospec, ospec, ospec], out_shape=[sh, sh, sh],
        compiler_params=pltpu.CompilerParams(
            dimension_semantics=("parallel", "parallel"), vmem_limit_bytes=VMEM_BIG),
    )(f, w_gate, w_up)


def _mm_down(act, w_down, h1):
    T = h1.shape[0]
    tm = _row_tile(T)
    tn = 1024
    return _mm(
        "mm_down", (T // tm, D // tn, NDEV),
        [pl.BlockSpec((None, tm, N_FF), lambda i, j, k: (k, i, 0)),
         pl.BlockSpec((None, N_FF, tn), lambda i, j, k: (k, 0, j)),
         pl.BlockSpec((tm, tn), lambda i, j, k: (i, j))],
        [act, w_down, h1],
        [pl.BlockSpec((tm, tn), lambda i, j, k: (i, j))],
        [jax.ShapeDtypeStruct((T, D), F32)], NN, 1, _ep_resid, acc_shape=(tm, tn))[0]


def _mm_ple(r, w_pgate, b_pgate, p, w_ple, h2):
    T = h2.shape[0]
    tm = _row_tile(T)
    tn = 1024

    def body(r_ref, wg_ref, b_ref, p_ref, wp_ref, h2_ref, gte_ref, pe_ref, h3_ref):
        gte = _sigmoid(_dot(r_ref[...], wg_ref[...], NN) + b_ref[...])
        pe = _dot(p_ref[...], wp_ref[...], NN)
        gte_ref[...] = gte.astype(BF16)
        pe_ref[...] = pe.astype(BF16)
        h3_ref[...] = h2_ref[...] + pe * gte

    tile = pl.BlockSpec((tm, tn), lambda i, j: (i, j))
    return pl.pallas_call(
        body, name="mm_ple", grid=(T // tm, D // tn),
        in_specs=[pl.BlockSpec((tm, D), lambda i, j: (i, 0)),
                  pl.BlockSpec((D, tn), lambda i, j: (0, j)),
                  pl.BlockSpec((1, tn), lambda i, j: (0, j)),
                  pl.BlockSpec((tm, DPLE), lambda i, j: (i, 0)),
                  pl.BlockSpec((DPLE, tn), lambda i, j: (0, j)),
                  tile],
        out_specs=[tile, tile, tile],
        out_shape=[jax.ShapeDtypeStruct((T, D), BF16), jax.ShapeDtypeStruct((T, D), BF16),
                   jax.ShapeDtypeStruct((T, D), F32)],
        compiler_params=pltpu.CompilerParams(
            dimension_semantics=("parallel", "parallel"), vmem_limit_bytes=VMEM_BIG),
    )(r, w_pgate, b_pgate, p, w_ple, h2)


def _mm_nt(name, dy, w, deps=()):
    T, n = dy.shape
    kdim = w.shape[0]
    tm = _row_tile(T)
    tn = 1024
    return _mm(
        name, (T // tm, kdim // tn, 1),
        [pl.BlockSpec((tm, n), lambda i, j, k: (i, 0)),
         pl.BlockSpec((tn, n), lambda i, j, k: (j, 0))],
        [dy, w],
        [pl.BlockSpec((tm, tn), lambda i, j, k: (i, j))],
        [jax.ShapeDtypeStruct((T, kdim), BF16)], NT, 1, _ep_cast(BF16), deps=deps)[0]


def _mm_down_bwd(dh2, w_down, g, u, deps=()):
    T = dh2.shape[0]
    tm = _row_tile(T)
    gspec = pl.BlockSpec((None, tm, N_FF), lambda i, j, k: (j, i, 0))
    sh = jax.ShapeDtypeStruct((NDEV, T, N_FF), BF16)
    return _mm(
        "mm_down_bwd", (T // tm, NDEV, 1),
        [pl.BlockSpec((tm, D), lambda i, j, k: (i, 0)),
         pl.BlockSpec((None, N_FF, D), lambda i, j, k: (j, 0, 0)),
         gspec, gspec],
        [dh2, w_down, g, u],
        [gspec, gspec], [sh, sh], NT, 1, _ep_swiglu_bwd, deps=deps)


def _mm_ffn_in_bwd(dg, w_gate, du, w_up, deps=()):
    T = dg.shape[1]
    tm = _row_tile(T)
    tn = 1024
    aspec = pl.BlockSpec((None, tm, N_FF), lambda i, j, k: (k, i, 0))
    wspec = pl.BlockSpec((None, tn, N_FF), lambda i, j, k: (k, j, 0))
    return _mm(
        "mm_ffn_in_bwd", (T // tm, D // tn, NDEV),
        [aspec, wspec, aspec, wspec], [dg, w_gate, du, w_up],
        [pl.BlockSpec((tm, tn), lambda i, j, k: (i, j))],
        [jax.ShapeDtypeStruct((T, D), BF16)], NT, 2, _ep_cast(BF16), acc_shape=(tm, tn),
        deps=deps)[0]


def _mm_in_bwd(dz, w_in, deps=()):
    T = dz.shape[0]
    tm = _row_tile(T)
    tn = 1024
    return _mm(
        "mm_in_bwd", (T // tm, D // tn, NDEV),
        [pl.BlockSpec((tm, N_IN), lambda i, j, k: (i, k)),
         pl.BlockSpec((None, tn, N_IN), lambda i, j, k: (k, j, 0))],
        [dz, w_in],
        [pl.BlockSpec((tm, tn), lambda i, j, k: (i, j))],
        [jax.ShapeDtypeStruct((T, D), BF16)], NT, 1, _ep_cast(BF16), acc_shape=(tm, tn),
        deps=deps)[0]


def _mm_tn(name, a, b, tj=None):
    T, idim = a.shape
    jdim = b.shape[1]
    tt = _row_tile(T)
    ti = min(idim, 1024)
    tj = jdim if tj is None else tj
    return _mm(
        name, (idim // ti, jdim // tj, T // tt),
        [pl.BlockSpec((tt, ti), lambda i, j, k: (k, i)),
         pl.BlockSpec((tt, tj), lambda i, j, k: (k, j))],
        [a, b],
        [pl.BlockSpec((ti, tj), lambda i, j, k: (i, j))],
        [jax.ShapeDtypeStruct((idim, jdim), BF16)], TN, 1, _ep_cast(BF16), acc_shape=(ti, tj))[0]


def _mm_tn_cols(name, a, b, ncol):
    T, idim = a.shape
    tt = _row_tile(T)
    return _mm(
        name, (1, NDEV, T // tt),
        [pl.BlockSpec((tt, idim), lambda i, j, k: (k, 0)),
         pl.BlockSpec((tt, ncol), lambda i, j, k: (k, j))],
        [a, b],
        [pl.BlockSpec((None, idim, ncol), lambda i, j, k: (j, 0, 0))],
        [jax.ShapeDtypeStruct((NDEV, idim, ncol), BF16)], TN, 1, _ep_cast(BF16),
        acc_shape=(idim, ncol))[0]


def _mm_tn_ff_cols(name, a, b):
    T = a.shape[0]
    tt = _row_tile(T)
    return _mm(
        name, (1, NDEV, T // tt),
        [pl.BlockSpec((tt, D), lambda i, j, k: (k, 0)),
         pl.BlockSpec((None, tt, N_FF), lambda i, j, k: (j, k, 0))],
        [a, b],
        [pl.BlockSpec((None, D, N_FF), lambda i, j, k: (j, 0, 0))],
        [jax.ShapeDtypeStruct((NDEV, D, N_FF), BF16)], TN, 1, _ep_cast(BF16),
        acc_shape=(D, N_FF))[0]


def _mm_tn_ff_rows(name, a, b):
    T = b.shape[0]
    tt = _row_tile(T)
    return _mm(
        name, (NDEV, 1, T // tt),
        [pl.BlockSpec((None, tt, N_FF), lambda i, j, k: (i, k, 0)),
         pl.BlockSpec((tt, D), lambda i, j, k: (k, 0))],
        [a, b],
        [pl.BlockSpec((None, N_FF, D), lambda i, j, k: (i, 0, 0))],
        [jax.ShapeDtypeStruct((NDEV, N_FF, D), BF16)], TN, 1, _ep_cast(BF16),
        acc_shape=(N_FF, D))[0]


TR = 256


def _rows(T):
    return min(TR, T)


def _rms_fwd(name, h, g, deps=()):
    T = h.shape[0]
    tr = _rows(T)

    def body(h_ref, g_ref, *rest):
        o_ref = rest[-1]
        v = h_ref[...]
        r = lax.rsqrt(jnp.mean(v * v, axis=-1, keepdims=True) + EPS)
        o_ref[...] = (v * r * g_ref[...]).astype(BF16)

    return pl.pallas_call(
        body, name=name, grid=(T // tr,),
        in_specs=[pl.BlockSpec((tr, D), lambda i: (i, 0)), pl.BlockSpec((1, D), lambda i: (0, 0))]
        + [ANY_SPEC] * len(deps),
        out_specs=pl.BlockSpec((tr, D), lambda i: (i, 0)),
        out_shape=jax.ShapeDtypeStruct((T, D), BF16),
        compiler_params=pltpu.CompilerParams(dimension_semantics=("parallel",)),
    )(h, g, *deps)


def _fold8(v):
    return jnp.sum(v.reshape(v.shape[0] // 8, 8, v.shape[1]), axis=0)


def _rms_bwd(name, dn_out, h, g, dres, want_bf16):
    T = h.shape[0]
    tr = _rows(T)
    nt = T // tr

    def body(dy_ref, h_ref, g_ref, dres_ref, *rest):
        if want_bf16:
            dh_ref, dhb_ref, dg_ref, acc = rest
        else:
            dh_ref, dg_ref, acc = rest
        i = pl.program_id(0)
        v = h_ref[...]
        r = lax.rsqrt(jnp.mean(v * v, axis=-1, keepdims=True) + EPS)
        nrm = v * r
        dy = dy_ref[...].astype(F32)
        dn = dy * g_ref[...]
        dh = dres_ref[...] + r * (dn - nrm * jnp.mean(dn * nrm, axis=-1, keepdims=True))
        dh_ref[...] = dh
        if want_bf16:
            dhb_ref[...] = dh.astype(BF16)

        @pl.when(i == 0)
        def _():
            acc[...] = jnp.zeros_like(acc)

        acc[...] += _fold8(dy * nrm)

        @pl.when(i == nt - 1)
        def _():
            dg_ref[...] = jnp.sum(acc[...], axis=0, keepdims=True)

    tile = pl.BlockSpec((tr, D), lambda i: (i, 0))
    vec = pl.BlockSpec((1, D), lambda i: (0, 0))
    out_specs = [tile] + ([tile] if want_bf16 else []) + [vec]
    out_shape = ([jax.ShapeDtypeStruct((T, D), F32)]
                 + ([jax.ShapeDtypeStruct((T, D), BF16)] if want_bf16 else [])
                 + [jax.ShapeDtypeStruct((1, D), F32)])
    return pl.pallas_call(
        body, name=name, grid=(nt,),
        in_specs=[tile, tile, vec, tile], out_specs=out_specs, out_shape=out_shape,
        scratch_shapes=[pltpu.VMEM((8, D), F32)],
        compiler_params=pltpu.CompilerParams(dimension_semantics=("arbitrary",)),
    )(dn_out, h, g, dres)


def _loss_bwd(h3, target, g_final, pe, gte):
    T = h3.shape[0]
    tr = _rows(T)
    nt = T // tr

    def body(h_ref, t_ref, g_ref, pe_ref, gte_ref, loss_ref, dh_ref, dpe_ref, dpg_ref,
             dgf_ref, dbp_ref, lacc, gacc, bacc):
        i = pl.program_id(0)
        v = h_ref[...]
        r = lax.rsqrt(jnp.mean(v * v, axis=-1, keepdims=True) + EPS)
        nrm = v * r
        g = g_ref[...]
        err = nrm * g - t_ref[...]
        dy = err * (1.0 / D)
        dn = dy * g
        dh = r * (dn - nrm * jnp.mean(dn * nrm, axis=-1, keepdims=True))
        dh_ref[...] = dh
        gte = gte_ref[...].astype(F32)
        pe = pe_ref[...].astype(F32)
        dpe_ref[...] = (dh * gte).astype(BF16)
        dpg = dh * pe * gte * (1.0 - gte)
        dpg_ref[...] = dpg.astype(BF16)

        @pl.when(i == 0)
        def _():
            lacc[...] = jnp.zeros_like(lacc)
            gacc[...] = jnp.zeros_like(gacc)
            bacc[...] = jnp.zeros_like(bacc)

        lacc[...] += _fold8(err * err)
        gacc[...] += _fold8(dy * nrm)
        bacc[...] += _fold8(dpg)

        @pl.when(i == nt - 1)
        def _():
            tot = jnp.sum(jnp.sum(lacc[...], axis=0, keepdims=True), axis=1, keepdims=True)
            loss_ref[...] = jnp.broadcast_to(tot * (0.5 / D), (1, 128))
            dgf_ref[...] = jnp.sum(gacc[...], axis=0, keepdims=True)
            dbp_ref[...] = jnp.sum(bacc[...], axis=0, keepdims=True)

    tile = pl.BlockSpec((tr, D), lambda i: (i, 0))
    vec = pl.BlockSpec((1, D), lambda i: (0, 0))
    return pl.pallas_call(
        body, name="loss_bwd", grid=(nt,),
        in_specs=[tile, tile, vec, tile, tile],
        out_specs=[pl.BlockSpec((1, 128), lambda i: (0, 0)), tile, tile, tile, vec, vec],
        out_shape=[jax.ShapeDtypeStruct((1, 128), F32), jax.ShapeDtypeStruct((T, D), F32),
                   jax.ShapeDtypeStruct((T, D), BF16), jax.ShapeDtypeStruct((T, D), BF16),
                   jax.ShapeDtypeStruct((1, D), F32), jax.ShapeDtypeStruct((1, D), F32)],
        scratch_shapes=[pltpu.VMEM((8, D), F32)] * 3,
        compiler_params=pltpu.CompilerParams(dimension_semantics=("arbitrary",)),
    )(h3, target, g_final, pe, gte)


def _band_masks():
    qi = lax.broadcasted_iota(jnp.int32, (BLK, BLK), 0)
    kj = lax.broadcasted_iota(jnp.int32, (BLK, BLK), 1)
    return kj >= qi, kj <= qi


def _cm_spec(d, col, rowmap=lambda n: n):
    if d == 1:
        return pl.BlockSpec((BLK, DA), lambda r, n: (rowmap(n), col))
    return pl.BlockSpec((None, BLK, DA), lambda r, n: (r, rowmap(n), col))


def _cm_shape(d, T, dtype):
    return jax.ShapeDtypeStruct((T, DA) if d == 1 else (d, T // d, DA), dtype)


def _attn_fwd(name, zsrc, d, T):
    nb = T // d // BLK
    scale = DH ** -0.5

    def body(q_ref, kp_ref, kc_ref, vp_ref, vc_ref, o_ref, l_ref):
        n = pl.program_id(1)
        prev_ok, cur_ok = _band_masks()
        prev_ok = prev_ok & (n > 0)
        for h in range(NH):
            sl = slice(h * DH, (h + 1) * DH)
            q = q_ref[:, sl]
            sp = jnp.where(prev_ok, _dot(q, kp_ref[:, sl], NT) * scale, NEG)
            sc = jnp.where(cur_ok, _dot(q, kc_ref[:, sl], NT) * scale, NEG)
            m = jnp.maximum(jnp.max(sp, axis=1, keepdims=True), jnp.max(sc, axis=1, keepdims=True))
            pp = jnp.exp(sp - m)
            pc = jnp.exp(sc - m)
            den = jnp.sum(pp, axis=1, keepdims=True) + jnp.sum(pc, axis=1, keepdims=True)
            o = _dot(pp.astype(BF16), vp_ref[:, sl], NN) + _dot(pc.astype(BF16), vc_ref[:, sl], NN)
            o_ref[:, sl] = (o / den).astype(BF16)
            l_ref[:, sl] = jnp.broadcast_to(m + jnp.log(den), (BLK, DH))

    prev = lambda n: jnp.maximum(n - 1, 0)
    return pl.pallas_call(
        body, name=name, grid=(d, nb),
        in_specs=[_cm_spec(d, 0), _cm_spec(d, 1, prev), _cm_spec(d, 1), _cm_spec(d, 2, prev),
                  _cm_spec(d, 2)],
        out_specs=[_cm_spec(d, 0)] * 2,
        out_shape=[_cm_shape(d, T, BF16), _cm_shape(d, T, F32)],
        compiler_params=pltpu.CompilerParams(dimension_semantics=("parallel", "parallel")),
    )(zsrc, zsrc, zsrc, zsrc, zsrc)


def _cm_tile(d, tr):
    if d == 1:
        return pl.BlockSpec((tr, DA), lambda i: (i, 0))
    return pl.BlockSpec((d, tr // d, DA), lambda i: (0, i, 0))


def _attn_combine(outs, lses, T):
    tr = _rows(T)

    def body(*refs):
        o_in, l_in = refs[:3], refs[3:6]
        o_ref, l_ref = refs[6:8]
        o_cm, l_cm = refs[8:8 + len(WIDE)], refs[8 + len(WIDE):8 + 2 * len(WIDE)]
        so, sl, so_all, sl_all = refs[8 + 2 * len(WIDE):]
        for c in range(DA // LANES):
            lt = _lane_tile(c)
            os_, ls_ = [o_in[0][:, lt].astype(F32)], [l_in[0][:, lt]]
            for w, d in enumerate(WIDE):
                for r in range(d):
                    so[w, c, pl.ds(r, tr // d, stride=d), :] = o_in[1 + w][r, :, lt].astype(F32)
                    sl[w, c, pl.ds(r, tr // d, stride=d), :] = l_in[1 + w][r, :, lt]
                os_.append(so[w, c])
                ls_.append(sl[w, c])
            la, lb, lc = ls_
            m = jnp.maximum(jnp.maximum(la, lb), lc)
            ea, eb, ec = jnp.exp(la - m), jnp.exp(lb - m), jnp.exp(lc - m)
            s = ea + eb + ec
            o = (ea * os_[0] + eb * os_[1] + ec * os_[2]) / s
            lse = m + jnp.log(s)
            o_ref[:, lt] = o.astype(BF16)
            l_ref[:, lt] = lse
            so_all[c] = o
            sl_all[c] = lse
        _emit_class_major(so_all, o_cm, tr)
        _emit_class_major(sl_all, l_cm, tr)

    specs = [_cm_tile(d, tr) for d in DILATIONS]
    wide = [_cm_tile(d, tr) for d in WIDE]
    return pl.pallas_call(
        body, name="attn_combine", grid=(T // tr,),
        in_specs=specs + specs,
        out_specs=[specs[0], specs[0]] + wide + wide,
        out_shape=[_cm_shape(1, T, BF16), _cm_shape(1, T, F32)]
        + [_cm_shape(d, T, BF16) for d in WIDE] + [_cm_shape(d, T, F32) for d in WIDE],
        scratch_shapes=[pltpu.VMEM((len(WIDE), DA // LANES, tr, LANES), F32)] * 2
        + [pltpu.VMEM((DA // LANES, tr, LANES), F32)] * 2,
        compiler_params=pltpu.CompilerParams(
            dimension_semantics=("parallel",), vmem_limit_bytes=VMEM_MID),
    )(*outs, *lses)


def _attn_bwd_q(name, zsrc, dosrc, osrc, lsrc, d, T):
    nb = T // d // BLK
    scale = DH ** -0.5

    def body(q_ref, kp_ref, kc_ref, vp_ref, vc_ref, do_ref, o_ref, l_ref, dq_ref):
        n = pl.program_id(1)
        prev_ok, cur_ok = _band_masks()
        prev_ok = prev_ok & (n > 0)
        for h in range(NH):
            sl = slice(h * DH, (h + 1) * DH)
            q = q_ref[:, sl]
            kp, kc = kp_ref[:, sl], kc_ref[:, sl]
            do = do_ref[:, sl]
            lrow = l_ref[:, sl]
            pp = jnp.exp(jnp.where(prev_ok, _dot(q, kp, NT) * scale - lrow, NEG))
            pc = jnp.exp(jnp.where(cur_ok, _dot(q, kc, NT) * scale - lrow, NEG))
            delta = jnp.sum(do.astype(F32) * o_ref[:, sl].astype(F32), axis=1, keepdims=True)
            dsp = (pp * (_dot(do, vp_ref[:, sl], NT) - delta) * scale).astype(BF16)
            dsc = (pc * (_dot(do, vc_ref[:, sl], NT) - delta) * scale).astype(BF16)
            dq = _dot(dsp, kp, NN) + _dot(dsc, kc, NN)
            dq_ref[:, sl] = dq.astype(BF16)

    prev = lambda n: jnp.maximum(n - 1, 0)
    own = _cm_spec(d, 0)
    return pl.pallas_call(
        body, name=name, grid=(d, nb),
        in_specs=[own, _cm_spec(d, 1, prev), _cm_spec(d, 1), _cm_spec(d, 2, prev), _cm_spec(d, 2),
                  own, own, own],
        out_specs=own, out_shape=_cm_shape(d, T, BF16),
        compiler_params=pltpu.CompilerParams(dimension_semantics=("parallel", "parallel")),
    )(zsrc, zsrc, zsrc, zsrc, zsrc, dosrc, osrc, lsrc)


def _attn_bwd_kv(name, zsrc, dosrc, osrc, lsrc, d, T):
    nb = T // d // BLK
    scale = DH ** -0.5

    def body(k_ref, v_ref, qa_ref, qb_ref, doa_ref, dob_ref, oa_ref, ob_ref, la_ref, lb_ref,
             dk_ref, dv_ref):
        j = pl.program_id(1)
        next_ok, own_ok = _band_masks()
        next_ok = next_ok & (j < nb - 1)
        for h in range(NH):
            sl = slice(h * DH, (h + 1) * DH)
            k, v = k_ref[:, sl], v_ref[:, sl]
            qa, qb = qa_ref[:, sl], qb_ref[:, sl]
            doa, dob = doa_ref[:, sl], dob_ref[:, sl]
            pa = jnp.exp(jnp.where(own_ok, _dot(qa, k, NT) * scale - la_ref[:, sl], NEG))
            pb = jnp.exp(jnp.where(next_ok, _dot(qb, k, NT) * scale - lb_ref[:, sl], NEG))
            da = jnp.sum(doa.astype(F32) * oa_ref[:, sl].astype(F32), axis=1, keepdims=True)
            db = jnp.sum(dob.astype(F32) * ob_ref[:, sl].astype(F32), axis=1, keepdims=True)
            dv = _dot(pa.astype(BF16), doa, TN) + _dot(pb.astype(BF16), dob, TN)
            dsa = (pa * (_dot(doa, v, NT) - da) * scale).astype(BF16)
            dsb = (pb * (_dot(dob, v, NT) - db) * scale).astype(BF16)
            dk = _dot(dsa, qa, TN) + _dot(dsb, qb, TN)
            dk_ref[:, sl] = dk.astype(BF16)
            dv_ref[:, sl] = dv.astype(BF16)

    nxt = lambda j: jnp.minimum(j + 1, nb - 1)
    own, own_n = _cm_spec(d, 0), _cm_spec(d, 0, nxt)
    sh = _cm_shape(d, T, BF16)
    return pl.pallas_call(
        body, name=name, grid=(d, nb),
        in_specs=[_cm_spec(d, 1), _cm_spec(d, 2), own, own_n, own, own_n, own, own_n, own, own_n],
        out_specs=[own, own], out_shape=[sh, sh],
        compiler_params=pltpu.CompilerParams(dimension_semantics=("parallel", "parallel")),
    )(zsrc, zsrc, zsrc, zsrc, dosrc, dosrc, osrc, osrc, lsrc, lsrc)


def _dz_assemble(dqs, dks, dvs, dcvg, T):
    tr = _rows(T)
    nb = len(DILATIONS)

    def body(*refs):
        cvg_ref, dz_ref, scr = refs[3 * nb], refs[3 * nb + 1], refs[3 * nb + 2]
        for g in range(3):
            parts = refs[g * nb:(g + 1) * nb]
            for c in range(DA // LANES):
                lt = _lane_tile(c)
                scr[g, c] = parts[0][:, lt].astype(F32)
                for w, d in enumerate(WIDE):
                    for r in range(d):
                        rows = pl.ds(r, tr // d, stride=d)
                        scr[g, c, rows, :] = scr[g, c, rows, :] + parts[1 + w][r, :, lt].astype(F32)
                dz_ref[:, g * DA + c * LANES:g * DA + (c + 1) * LANES] = scr[g, c].astype(BF16)
        dz_ref[:, 3 * DA:] = cvg_ref[...]

    specs = [_cm_tile(d, tr) for d in DILATIONS]
    return pl.pallas_call(
        body, name="dz_assemble", grid=(T // tr,),
        in_specs=specs * 3 + [pl.BlockSpec((tr, 2 * DC), lambda i: (i, 0))],
        out_specs=pl.BlockSpec((tr, DIN), lambda i: (i, 0)),
        out_shape=jax.ShapeDtypeStruct((T, DIN), BF16),
        scratch_shapes=[pltpu.VMEM((3, DA // LANES, tr, LANES), F32)],
        compiler_params=pltpu.CompilerParams(
            dimension_semantics=("parallel",), vmem_limit_bytes=VMEM_MID),
    )(*dqs, *dks, *dvs, dcvg)


CT = 256
HALO = 32
RC = 32


def _conv_fwd(z, w_dw, b_dw, g_ln, b_ln):
    T = z.shape[0]
    ct = min(CT, T)
    nt = T // ct
    hb = ct // HALO

    def body(cv_ref, cg_ref, cvp_ref, cgp_ref, w_ref, bdw_ref, g_ref, b_ref, oc_ref, y_ref, ubuf, ush):
        i = pl.program_id(0)
        up = cvp_ref[...].astype(F32) * _sigmoid(cgp_ref[...].astype(F32))
        ubuf[0:HALO, :] = jnp.where(i > 0, up, 0.0)
        ubuf[HALO:, :] = cv_ref[...].astype(F32) * _sigmoid(cg_ref[...].astype(F32))
        for b in range(8):
            ush[b] = ubuf[pl.ds(8 - b, ct + 24), :]

        def chunk(ci, carry):
            r0 = pl.multiple_of(ci * RC, RC)
            acc = jnp.broadcast_to(bdw_ref[...], (RC, DC))
            for s in range(CW):
                a, b = divmod(s, 8)
                acc = acc + w_ref[CW - 1 - s:CW - s, :] * ush[b, pl.ds(r0 + 24 - 8 * a, RC), :]
            y_ref[pl.ds(r0, RC), :] = acc
            mu = jnp.mean(acc, axis=-1, keepdims=True)
            cen = acc - mu
            var = jnp.mean(cen * cen, axis=-1, keepdims=True)
            ln = cen * lax.rsqrt(var + EPS) * g_ref[...] + b_ref[...]
            oc_ref[pl.ds(r0, RC), :] = (ln * _sigmoid(ln)).astype(BF16)
            return carry

        lax.fori_loop(0, ct // RC, chunk, 0)

    cur = lambda col: pl.BlockSpec((ct, DC), lambda i: (i, col))
    prv = lambda col: pl.BlockSpec((HALO, DC), lambda i: (jnp.maximum(i * hb - 1, 0), col))
    vec = pl.BlockSpec((1, DC), lambda i: (0, 0))
    return pl.pallas_call(
        body, name="conv_fwd", grid=(nt,),
        in_specs=[cur(3), cur(4), prv(3), prv(4), pl.BlockSpec((CW, DC), lambda i: (0, 0)),
                  vec, vec, vec],
        out_specs=[pl.BlockSpec((ct, DC), lambda i: (i, 0))] * 2,
        out_shape=[jax.ShapeDtypeStruct((T, DC), BF16), jax.ShapeDtypeStruct((T, DC), F32)],
        scratch_shapes=[pltpu.VMEM((ct + HALO, DC), F32), pltpu.VMEM((8, ct + 24, DC), F32)],
        compiler_params=pltpu.CompilerParams(
            dimension_semantics=("parallel",), vmem_limit_bytes=VMEM_MID),
    )(z, z, z, z, w_dw, b_dw, g_ln, b_ln)


def _conv_bwd(z, dom, y, w_dw, g_ln, b_ln):
    T = z.shape[0]
    ct = min(CT, T)
    nt = T // ct
    hb = ct // HALO
    last_halo = T // HALO - 1

    def ln_bwd(yv, dov, g_ref, b_ref):
        mu = jnp.mean(yv, axis=-1, keepdims=True)
        cen = yv - mu
        rstd = lax.rsqrt(jnp.mean(cen * cen, axis=-1, keepdims=True) + EPS)
        xhat = cen * rstd
        ln = xhat * g_ref[...] + b_ref[...]
        sg = _sigmoid(ln)
        dln = dov * (sg * (1.0 + ln * (1.0 - sg)))
        dxh = dln * g_ref[...]
        dy = rstd * (dxh - jnp.mean(dxh, axis=-1, keepdims=True)
                     - xhat * jnp.mean(dxh * xhat, axis=-1, keepdims=True))
        return dy, dln, xhat

    def body(do_ref, don_ref, y_ref, yn_ref, cv_ref, cg_ref, cvp_ref, cgp_ref, w_ref, g_ref, b_ref,
             dcvg_ref, dw_ref, dbdw_ref, dg_ref, db_ref,
             dybuf, dysh, ubuf, ush, dwacc, vacc):
        i = pl.program_id(0)

        @pl.when(i == 0)
        def _():
            dwacc[...] = jnp.zeros_like(dwacc)
            vacc[...] = jnp.zeros_like(vacc)

        def ln_chunk(ci, carry):
            r0 = pl.multiple_of(ci * RC, RC)
            dy, dln, xhat = ln_bwd(y_ref[pl.ds(r0, RC), :], do_ref[pl.ds(r0, RC), :].astype(F32),
                                   g_ref, b_ref)
            dybuf[pl.ds(r0, RC), :] = dy
            vacc[0] += _fold8(dy)
            vacc[1] += _fold8(dln * xhat)
            vacc[2] += _fold8(dln)
            return carry

        lax.fori_loop(0, ct // RC, ln_chunk, 0)
        dyn, _, _ = ln_bwd(yn_ref[...], don_ref[...].astype(F32), g_ref, b_ref)
        dybuf[ct:, :] = jnp.where(i < nt - 1, dyn, 0.0)
        for b in range(8):
            dysh[b] = dybuf[pl.ds(b, ct + 24), :]

        up = cvp_ref[...].astype(F32) * _sigmoid(cgp_ref[...].astype(F32))
        ubuf[0:HALO, :] = jnp.where(i > 0, up, 0.0)
        ubuf[HALO:, :] = cv_ref[...].astype(F32) * _sigmoid(cg_ref[...].astype(F32))
        for b in range(8):
            ush[b] = ubuf[pl.ds(8 - b, ct + 24), :]

        def chunk(ci, carry):
            r0 = pl.multiple_of(ci * RC, RC)
            dy = dybuf[pl.ds(r0, RC), :]
            du = jnp.zeros((RC, DC), F32)
            for s in range(CW):
                a, b = divmod(s, 8)
                du = du + w_ref[CW - 1 - s:CW - s, :] * dysh[b, pl.ds(r0 + 8 * a, RC), :]
                dwacc[CW - 1 - s] += _fold8(dy * ush[b, pl.ds(r0 + 24 - 8 * a, RC), :])
            cv = cv_ref[pl.ds(r0, RC), :].astype(F32)
            sg = _sigmoid(cg_ref[pl.ds(r0, RC), :].astype(F32))
            dcvg_ref[pl.ds(r0, RC), 0:DC] = (du * sg).astype(BF16)
            dcvg_ref[pl.ds(r0, RC), DC:2 * DC] = (du * cv * sg * (1.0 - sg)).astype(BF16)
            return carry

        lax.fori_loop(0, ct // RC, chunk, 0)

        @pl.when(i == nt - 1)
        def _():
            dw_ref[...] = jnp.sum(dwacc[...], axis=1)
            dbdw_ref[...] = jnp.sum(vacc[0], axis=0, keepdims=True)
            dg_ref[...] = jnp.sum(vacc[1], axis=0, keepdims=True)
            db_ref[...] = jnp.sum(vacc[2], axis=0, keepdims=True)

    cur = lambda col: pl.BlockSpec((ct, DC), lambda i: (i, col))
    prv = lambda col: pl.BlockSpec((HALO, DC), lambda i: (jnp.maximum(i * hb - 1, 0), col))
    nxt = lambda col: pl.BlockSpec((HALO, DC), lambda i: (jnp.minimum((i + 1) * hb, last_halo), col))
    vec = pl.BlockSpec((1, DC), lambda i: (0, 0))
    tile = pl.BlockSpec((ct, DC), lambda i: (i, 0))
    return pl.pallas_call(
        body, name="conv_bwd", grid=(nt,),
        in_specs=[cur(1), nxt(1), cur(0), nxt(0), cur(3), cur(4), prv(3), prv(4),
                  pl.BlockSpec((CW, DC), lambda i: (0, 0)), vec, vec],
        out_specs=[pl.BlockSpec((ct, 2 * DC), lambda i: (i, 0)),
                   pl.BlockSpec((CW, DC), lambda i: (0, 0)), vec, vec, vec],
        out_shape=[jax.ShapeDtypeStruct((T, 2 * DC), BF16),
                   jax.ShapeDtypeStruct((CW, DC), F32), jax.ShapeDtypeStruct((1, DC), F32),
                   jax.ShapeDtypeStruct((1, DC), F32), jax.ShapeDtypeStruct((1, DC), F32)],
        scratch_shapes=[pltpu.VMEM((ct + HALO, DC), F32), pltpu.VMEM((8, ct + 24, DC), F32),
                        pltpu.VMEM((ct + HALO, DC), F32), pltpu.VMEM((8, ct + 24, DC), F32),
                        pltpu.VMEM((CW, 8, DC), F32), pltpu.VMEM((3, 8, DC), F32)],
        compiler_params=pltpu.CompilerParams(
            dimension_semantics=("arbitrary",), vmem_limit_bytes=VMEM_BIG),
    )(dom, dom, y, y, z, z, z, z, w_dw, g_ln, b_ln)


def _adam_math(w, g, m, v):
    m = ADAM_B1 * m + (1.0 - ADAM_B1) * g
    v = ADAM_B2 * v + (1.0 - ADAM_B2) * (g * g)
    m_hat = m / (1.0 - ADAM_B1 ** ADAM_STEP)
    v_hat = v / (1.0 - ADAM_B2 ** ADAM_STEP)
    delta = -ADAM_LR * (m_hat / (jnp.sqrt(v_hat) + ADAM_EPS) + ADAM_WD * w)
    return delta, m, v


def _adam(name, slots, w, m, v):
    rows, cols = w.shape
    tr = next(t for t in (256, 176, 128, 64, 32, 16, 8, rows) if rows % t == 0)

    def body(s_ref, w_ref, m_ref, v_ref, g_out, d_out, m_out, v_out):
        g = s_ref[0].astype(F32)
        for s in range(1, NDEV):
            g = g + s_ref[s].astype(F32)
        delta, mn, vn = _adam_math(w_ref[...], g, m_ref[...], v_ref[...])
        g_out[...] = g
        d_out[...] = delta
        m_out[...] = mn
        v_out[...] = vn

    tile = pl.BlockSpec((tr, cols), lambda i: (i, 0))
    sh = jax.ShapeDtypeStruct((rows, cols), F32)
    return pl.pallas_call(
        body, name=name, grid=(rows // tr,),
        in_specs=[pl.BlockSpec((NDEV, tr, cols), lambda i: (0, i, 0)), tile, tile, tile],
        out_specs=[tile] * 4, out_shape=[sh] * 4,
        compiler_params=pltpu.CompilerParams(
            dimension_semantics=("parallel",), vmem_limit_bytes=VMEM_MID),
    )(slots, w, m, v)


SMALL_NAMES = ("g_mix", "b_dw", "g_conv_ln", "b_conv_ln", "g_ffn", "g_ple", "b_pgate", "g_final")


def _pack_small(vecs, w_dw_full):
    rows = [jnp.pad(v.reshape(1, -1), ((0, 0), (0, SMALL_W - v.size))) for v in vecs]
    rows.append(jnp.pad(w_dw_full, ((0, 0), (0, SMALL_W - DC))))
    rows.append(jnp.zeros((SMALL_ROWS - len(vecs) - CW, SMALL_W), F32))
    return jnp.concatenate(rows, axis=0)


def kernel(x, p, g_mix, w_in, w_dw, b_dw, g_conv_ln, b_conv_ln, w_out, g_ffn, w_gate, w_up, w_down, g_ple, w_pgate, b_pgate, w_ple, g_final, loss_target, m_g_mix, m_w_in, m_w_dw, m_b_dw, m_g_conv_ln, m_b_conv_ln, m_w_out, m_g_ffn, m_w_gate, m_w_up, m_w_down, m_g_ple, m_w_pgate, m_b_pgate, m_w_ple, m_g_final, v_g_mix, v_w_in, v_w_dw, v_b_dw, v_g_conv_ln, v_b_conv_ln, v_w_out, v_g_ffn, v_w_gate, v_w_up, v_w_down, v_g_ple, v_w_pgate, v_b_pgate, v_w_ple, v_g_final):
    T = x.shape[1]
    me = 4 * lax.axis_index("x") + 2 * lax.axis_index("y") + lax.axis_index("c")
    xs = x.reshape(T, D)
    ps = p.reshape(T, DPLE).astype(BF16)
    tgt = loss_target.reshape(T, D)
    g_final2 = g_final.reshape(1, D)

    big = dict(w_in=w_in[0], w_out=w_out[0], w_gate=w_gate[0], w_up=w_up[0], w_down=w_down[0],
               w_pgate=w_pgate[0], w_ple=w_ple[0])
    order = ("w_in", "w_out", "w_gate", "w_up", "w_down", "w_pgate", "w_ple")
    g_order = ("w_dw",) + order
    g_items = [(w_dw.reshape(CW, DC // NDEV), False)] + [(big[n].astype(BF16), False) for n in order]
    g_handles, g_token = _xstart("gather_start", g_items, _place("gather_place", g_items))
    G = dict(zip(g_order, g_handles))

    a = _rms_fwd("rms_mix", xs, g_mix, deps=[g_token])
    w_dw_f = _xwait("gather_wait_w_dw", G["w_dw"], a).transpose(1, 0, 2).reshape(CW, DC)
    w_in_f = _xwait("gather_wait_w_in", G["w_in"], a)
    z, *z_wide = _mm_in(a, w_in_f)
    zsrc = dict(zip(DILATIONS, [z] + z_wide))
    br = [_attn_fwd(f"attn_fwd_d{d}", zsrc[d], d, T) for d in DILATIONS]
    comb = list(_attn_combine([b[0] for b in br], [b[1] for b in br], T))
    o_attn, lse = comb[0], comb[1]
    osrc = dict(zip(DILATIONS, [o_attn] + comb[2:2 + len(WIDE)]))
    lsrc = dict(zip(DILATIONS, [lse] + comb[2 + len(WIDE):]))
    o_conv, y_conv = _conv_fwd(z, w_dw_f, b_dw, g_conv_ln, b_conv_ln)
    w_out_f = _xwait("gather_wait_w_out", G["w_out"], o_conv).reshape(D, D)
    h1 = _mm_out(o_attn, o_conv, w_out_f, xs)
    f = _rms_fwd("rms_ffn", h1, g_ffn)
    w_gate_f = _xwait("gather_wait_w_gate", G["w_gate"], f)
    w_up_f = _xwait("gather_wait_w_up", G["w_up"], f)
    gate, up, act = _mm_gate_up(f, w_gate_f, w_up_f)
    w_down_f = _xwait("gather_wait_w_down", G["w_down"], act)
    h2 = _mm_down(act, w_down_f, h1)
    r = _rms_fwd("rms_ple", h2, g_ple)
    w_pgate_f = _xwait("gather_wait_w_pgate", G["w_pgate"], r).reshape(D, D)
    w_ple_f = _xwait("gather_wait_w_ple", G["w_ple"], r).transpose(1, 0, 2).reshape(DPLE, D)
    gte, pe, h3 = _mm_ple(r, w_pgate_f, b_pgate, ps, w_ple_f, h2)

    loss_part, dh3, dpe, dpg, d_g_final, d_b_pgate = _loss_bwd(h3, tgt, g_final2, pe, gte)
    H = {}

    def send_grads(tag, named):
        items = [(v, True) for _, v in named]
        handles, token = _xstart(f"grads_start_{tag}", items, _place(f"grads_place_{tag}", items))
        H.update(zip([n for n, _ in named], handles))
        return token

    gw_pgate = _mm_tn("gw_pgate", r, dpg).reshape(NDEV, D // NDEV, D)
    gw_ple = _mm_tn("gw_ple", ps, dpe).reshape(DPLE, NDEV, D // NDEV).transpose(1, 0, 2)
    tok = send_grads("ple", [("w_pgate", gw_pgate), ("w_ple", gw_ple)])
    dr = _mm_nt("mm_pgate_bwd", dpg, w_pgate_f, deps=[tok])
    dh2, dh2b, d_g_ple = _rms_bwd("rms_ple_bwd", dr, h2, g_ple, dh3, True)
    gw_down = _mm_tn_ff_rows("gw_down", act, dh2b)
    tok = send_grads("down", [("w_down", gw_down)])
    dgate, dup = _mm_down_bwd(dh2b, w_down_f, gate, up, deps=[tok])
    gw_gate = _mm_tn_ff_cols("gw_gate", f, dgate)
    gw_up = _mm_tn_ff_cols("gw_up", f, dup)
    tok = send_grads("ffn", [("w_gate", gw_gate), ("w_up", gw_up)])
    df = _mm_ffn_in_bwd(dgate, w_gate_f, dup, w_up_f, deps=[tok])
    dh1, dh1b, d_g_ffn = _rms_bwd("rms_ffn_bwd", df, h1, g_ffn, dh2, True)
    gw_out = jnp.concatenate(
        [_mm_tn("gw_out_attn", o_attn, dh1b), _mm_tn("gw_out_conv", o_conv, dh1b)], axis=0)
    tok = send_grads("out", [("w_out", gw_out.reshape(NDEV, D // NDEV, D))])
    dom, *do_wide = _mm_out_bwd(dh1b, w_out_f, deps=[tok])
    dosrc = dict(zip(DILATIONS, [dom] + do_wide))
    dcvg, d_w_dw, d_b_dw, d_g_ln, d_b_ln = _conv_bwd(z, dom, y_conv, w_dw_f, g_conv_ln, b_conv_ln)
    dqs, dks, dvs = [], [], []
    for d in DILATIONS:
        dqs.append(_attn_bwd_q(f"attn_bwd_q_d{d}", zsrc[d], dosrc[d], osrc[d], lsrc[d], d, T))
        dk, dv = _attn_bwd_kv(f"attn_bwd_kv_d{d}", zsrc[d], dosrc[d], osrc[d], lsrc[d], d, T)
        dks.append(dk)
        dvs.append(dv)
    dz = _dz_assemble(dqs, dks, dvs, dcvg, T)
    gw_in = _mm_tn_cols("gw_in", a, dz, N_IN)
    tok = send_grads("in", [("w_in", gw_in)])
    da = _mm_in_bwd(dz, w_in_f, deps=[tok])
    grad_x, d_g_mix = _rms_bwd("rms_mix_bwd", da, xs, g_mix, dh1, False)

    small_part = _pack_small(
        [d_g_mix, d_b_dw, d_g_ln, d_b_ln, d_g_ffn, d_g_ple, d_b_pgate, d_g_final], d_w_dw)
    small_slots = _exchange("exchange_small_grads", [(small_part, False)])[0]
    S = {n: _xwait(f"grads_wait_{n}", H[n], small_slots)
         for n in ("w_pgate", "w_ple", "w_down", "w_gate", "w_up", "w_out", "w_in")}

    mom = dict(w_in=(m_w_in, v_w_in), w_out=(m_w_out, v_w_out), w_gate=(m_w_gate, v_w_gate),
               w_up=(m_w_up, v_w_up), w_down=(m_w_down, v_w_down), w_pgate=(m_w_pgate, v_w_pgate),
               w_ple=(m_w_ple, v_w_ple))
    upd = {}
    for n in order:
        res = _adam(f"adam_{n}", S[n], big[n], mom[n][0][0], mom[n][1][0])
        upd[n] = [t[None] for t in res]

    def lanes(v):
        full = jnp.zeros((CW, NDEV, DC // NDEV), F32)
        full = lax.dynamic_update_slice(full, v.reshape(CW, 1, DC // NDEV), (0, me, 0))
        return full.reshape(CW, DC)

    small_w = _pack_small([g_mix, b_dw, g_conv_ln, b_conv_ln, g_ffn, g_ple, b_pgate, g_final2], lanes(w_dw))
    small_m = _pack_small([m_g_mix, m_b_dw, m_g_conv_ln, m_b_conv_ln, m_g_ffn, m_g_ple, m_b_pgate,
                           m_g_final.reshape(1, D)], lanes(m_w_dw))
    small_v = _pack_small([v_g_mix, v_b_dw, v_g_conv_ln, v_b_conv_ln, v_g_ffn, v_g_ple, v_b_pgate,
                           v_g_final.reshape(1, D)], lanes(v_w_dw))
    small_res = _adam("adam_small", small_slots, small_w, small_m, small_v)

    def unpack(t):
        out = {}
        widths = dict(g_mix=D, b_dw=DC, g_conv_ln=DC, b_conv_ln=DC, g_ffn=D, g_ple=D, b_pgate=D, g_final=D)
        for i, n in enumerate(SMALL_NAMES):
            out[n] = t[i:i + 1, :widths[n]]
        out["g_final"] = out["g_final"].reshape(D)
        taps = t[len(SMALL_NAMES):len(SMALL_NAMES) + CW, :DC].reshape(CW, NDEV, DC // NDEV)
        out["w_dw"] = lax.dynamic_slice(taps, (0, me, 0), (CW, 1, DC // NDEV))[None]
        return out

    small = [unpack(t) for t in small_res]

    loss = lax.psum(loss_part[0, 0], ("x", "y", "c"))
    names = ("g_mix", "w_in", "w_dw", "b_dw", "g_conv_ln", "b_conv_ln", "w_out", "g_ffn", "w_gate",
             "w_up", "w_down", "g_ple", "w_pgate", "b_pgate", "w_ple", "g_final")
    outs = [loss, grad_x.reshape(1, T, D)]
    for kind in range(4):
        for n in names:
            outs.append(upd[n][kind] if n in upd else small[kind][n])
    return tuple(outs)
```

```python
import jax
import jax.numpy as jnp
from jax import lax
from jax.experimental import pallas as pl
from jax.experimental.pallas import tpu as pltpu

F32 = jnp.float32
BF16 = jnp.bfloat16

NDEV = 8
D = 2048
NH = 8
DH = 128
DA = NH * DH
DC = D - DA
DIN = 3 * DA + 2 * DC
DFF = 5632
DPLE = 256
BLK = 128
DILATIONS = (1, 4, 16)
CW = 31
EPS = 1e-6
N_IN = DIN // NDEV
N_FF = DFF // NDEV
NEG = -1e30

ADAM_LR = 0.001
ADAM_B1 = 0.9
ADAM_B2 = 0.999
ADAM_EPS = 1e-08
ADAM_WD = 0.01
ADAM_STEP = 10

VMEM_CAP_V7X = 64 * 1024 * 1024
VMEM_BIG = VMEM_CAP_V7X - 12 * 1024 * 1024
VMEM_MID = 40 * 1024 * 1024

SMALL_W = 2048
SMALL_ROWS = 40


def _sigmoid(v):
    return 1.0 / (1.0 + jnp.exp(-v))


def _dot(a, b, contract):
    return lax.dot_general(a, b, (contract, ((), ())), preferred_element_type=F32)


NN = ((1,), (0,))
NT = ((1,), (1,))
TN = ((0,), (0,))


def _exchange(name, items):
    n = len(items)
    out_shape = [
        jax.ShapeDtypeStruct((NDEV,) + (a.shape[1:] if sc else a.shape), a.dtype)
        for a, sc in items
    ]
    scat = [sc for _, sc in items]

    def body(*refs):
        srcs = refs[:n]
        dsts = refs[n:2 * n]
        send_sems, recv_sems, loc_sems = refs[2 * n:]
        x = lax.axis_index("x")
        y = lax.axis_index("y")
        c = lax.axis_index("c")
        me = 4 * x + 2 * y + c

        local = []
        for i in range(n):
            src = srcs[i].at[me] if scat[i] else srcs[i]
            cp = pltpu.make_async_copy(src, dsts[i].at[me], loc_sems.at[i])
            cp.start()
            local.append(cp)

        remote = []
        for k in range(1, NDEV):
            px = (1 - x) if (k >> 2) & 1 else x
            py = (1 - y) if (k >> 1) & 1 else y
            pc = (1 - c) if k & 1 else c
            peer = 4 * px + 2 * py + pc
            for i in range(n):
                sem = i * (NDEV - 1) + k - 1
                src = srcs[i].at[peer] if scat[i] else srcs[i]
                send = pltpu.make_async_remote_copy(
                    src_ref=src, dst_ref=dsts[i].at[me],
                    send_sem=send_sems.at[sem], recv_sem=recv_sems.at[sem],
                    device_id=(px, py, pc), device_id_type=pl.DeviceIdType.MESH)
                send.start()
                recv = pltpu.make_async_remote_copy(
                    src_ref=src, dst_ref=dsts[i].at[peer],
                    send_sem=send_sems.at[sem], recv_sem=recv_sems.at[sem],
                    device_id=(px, py, pc), device_id_type=pl.DeviceIdType.MESH)
                remote.append((send, recv))
        for send, recv in remote:
            recv.wait_recv()
            send.wait_send()
        for cp in local:
            cp.wait()

    any_spec = pl.BlockSpec(memory_space=pl.ANY)
    return pl.pallas_call(
        body, name=name,
        in_specs=[any_spec] * n, out_specs=[any_spec] * n, out_shape=out_shape,
        scratch_shapes=[
            pltpu.SemaphoreType.DMA((n * (NDEV - 1),)),
            pltpu.SemaphoreType.DMA((n * (NDEV - 1),)),
            pltpu.SemaphoreType.DMA((n,)),
        ],
    )(*[a for a, _ in items])


HBM_SPEC = pl.BlockSpec(memory_space=pltpu.HBM)
SEM_SPEC = pl.BlockSpec(memory_space=pltpu.SEMAPHORE)
ANY_SPEC = pl.BlockSpec(memory_space=pl.ANY)
EFFECT = pltpu.SideEffectType.DATAFLOW_SIDE_EFFECTING


def _peer_of(k):
    x = lax.axis_index("x")
    y = lax.axis_index("y")
    c = lax.axis_index("c")
    px = (1 - x) if (k >> 2) & 1 else x
    py = (1 - y) if (k >> 1) & 1 else y
    pc = (1 - c) if k & 1 else c
    return (px, py, pc), 4 * px + 2 * py + pc


def _my_index():
    return 4 * lax.axis_index("x") + 2 * lax.axis_index("y") + lax.axis_index("c")


def _slot_shape(a, sc):
    return (NDEV,) + (a.shape[1:] if sc else a.shape)


def _divisor_tile(rows):
    return next((t for t in (512, 256, 176, 128, 64, 32, 16) if rows % t == 0), rows)


def _place(name, items):
    lands = []
    for idx, (a, sc) in enumerate(items):
        rows, cols = a.shape[-2:]
        tr = _divisor_tile(rows)

        def body(s_ref, o_ref):
            o_ref[...] = s_ref[...]

        mine = pl.BlockSpec((None, tr, cols), lambda i: (_my_index(), i, 0))
        lands.append(pl.pallas_call(
            body, name=f"{name}_{idx}", grid=(rows // tr,),
            in_specs=[mine if sc else pl.BlockSpec((tr, cols), lambda i: (i, 0))],
            out_specs=mine,
            out_shape=jax.ShapeDtypeStruct(_slot_shape(a, sc), a.dtype),
            compiler_params=pltpu.CompilerParams(dimension_semantics=("parallel",)),
        )(a))
    return lands


def _xstart(name, items, lands):
    n = len(items)
    scat = [sc for _, sc in items]

    def body(*refs):
        srcs = refs[:n]
        lzs = refs[n:2 * n]
        send_sems = refs[2 * n:3 * n]
        recv_sems = refs[3 * n:4 * n]
        token = refs[-1]
        me = _my_index()
        for i in range(n):
            for k in range(1, NDEV):
                peer_id, peer = _peer_of(k)
                src = srcs[i].at[peer] if scat[i] else srcs[i]
                pltpu.make_async_remote_copy(
                    src_ref=src, dst_ref=lzs[i].at[me],
                    send_sem=send_sems[i].at[k - 1], recv_sem=recv_sems[i].at[k - 1],
                    device_id=peer_id, device_id_type=pl.DeviceIdType.MESH).start()
        token[...] = jnp.zeros_like(token)

    sem = pltpu.SemaphoreType.DMA((NDEV - 1,))
    hbm = [pltpu.HBM(a.shape, a.dtype) for a, _ in items] + [pltpu.HBM(l.shape, l.dtype) for l in lands]
    res = pl.pallas_call(
        body, name=name,
        in_specs=[HBM_SPEC] * (2 * n),
        out_specs=[SEM_SPEC] * (2 * n) + [HBM_SPEC] * (2 * n) + [pl.BlockSpec(memory_space=pltpu.VMEM)],
        out_shape=[sem] * (2 * n) + hbm + [jax.ShapeDtypeStruct((8, 128), F32)],
        input_output_aliases={i: 2 * n + i for i in range(2 * n)},
        compiler_params=pltpu.CompilerParams(has_side_effects=EFFECT),
    )(*[pltpu.with_memory_space_constraint(a, pltpu.HBM) for a, _ in items],
      *[pltpu.with_memory_space_constraint(l, pltpu.HBM) for l in lands])
    handles = [(res[i], res[n + i], res[2 * n + i], res[3 * n + i], scat[i]) for i in range(n)]
    return handles, res[-1]


def _xwait(name, handle, after):
    send_sem, recv_sem, src, land, sc = handle

    def body(src_ref, land_ref, send_ref, recv_ref, after_ref, src_dead, got_ref):
        for k in range(1, NDEV):
            peer_id, peer = _peer_of(k)
            cp = pltpu.make_async_remote_copy(
                src_ref=src_ref.at[peer] if sc else src_ref, dst_ref=land_ref.at[peer],
                send_sem=send_ref.at[k - 1], recv_sem=recv_ref.at[k - 1],
                device_id=peer_id, device_id_type=pl.DeviceIdType.MESH)
            cp.wait_send()
            cp.wait_recv()

    return pl.pallas_call(
        body, name=name,
        in_specs=[HBM_SPEC, HBM_SPEC, SEM_SPEC, SEM_SPEC, ANY_SPEC],
        out_specs=[HBM_SPEC, HBM_SPEC],
        out_shape=[pltpu.HBM(src.shape, src.dtype), pltpu.HBM(land.shape, land.dtype)],
        input_output_aliases={0: 0, 1: 1},
        compiler_params=pltpu.CompilerParams(has_side_effects=EFFECT),
    )(src, land, send_sem, recv_sem, after)[1]


def _mm(name, grid, in_specs, operands, out_specs, out_shape, contract, n_pairs, epilogue,
        acc_shape=None, vmem=VMEM_BIG, deps=(), group=1, a_cols=None):
    nk = grid[2]

    def shard(ref, s, is_a):
        if group == 1:
            return ref[...]
        if is_a and a_cols is not None:
            return ref[:, s * a_cols:(s + 1) * a_cols]
        return ref[s]
    n_extra = len(operands) - 2 * n_pairs
    n_out = len(out_shape)
    n_in = len(operands) + len(deps)
    in_specs = list(in_specs) + [ANY_SPEC] * len(deps)
    operands = list(operands) + list(deps)

    def body(*refs):
        ab = refs[:2 * n_pairs]
        extras = refs[2 * n_pairs:2 * n_pairs + n_extra]
        outs = refs[n_in:n_in + n_out]
        part = None
        for p in range(n_pairs):
            for s in range(group):
                d = _dot(shard(ab[2 * p], s, True), shard(ab[2 * p + 1], s, False), contract)
                part = d if part is None else part + d
        if nk == 1:
            epilogue(part, extras, outs)
        else:
            acc_ref = refs[-1]
            k = pl.program_id(2)

            @pl.when(k == 0)
            def _():
                acc_ref[...] = part

            @pl.when(k > 0)
            def _():
                acc_ref[...] += part

            @pl.when(k == nk - 1)
            def _():
                epilogue(acc_ref[...], extras, outs)

    scratch = [pltpu.VMEM(acc_shape, F32)] if nk > 1 else []
    return pl.pallas_call(
        body, name=name, grid=grid, in_specs=in_specs, out_specs=out_specs, out_shape=out_shape,
        scratch_shapes=scratch,
        compiler_params=pltpu.CompilerParams(
            dimension_semantics=("parallel", "parallel", "arbitrary"), vmem_limit_bytes=vmem),
    )(*operands)


def _ep_cast(dtype):
    def ep(acc, extras, outs):
        outs[0][...] = acc.astype(dtype)
    return ep


def _ep_resid(acc, extras, outs):
    outs[0][...] = extras[0][...] + acc


def _ep_swiglu_bwd(acc, extras, outs):
    g = extras[0][...].astype(F32)
    u = extras[1][...].astype(F32)
    sg = _sigmoid(g)
    outs[0][...] = (acc * u * (sg * (1.0 + g * (1.0 - sg)))).astype(BF16)
    outs[1][...] = (acc * (g * sg)).astype(BF16)


def _row_tile(T):
    return min(1024, T)


def _tn_rows(T):
    return min(2048, T)


WIDE = tuple(d for d in DILATIONS if d > 1)


LANES = 128


def _lane_tile(c):
    return slice(c * LANES, (c + 1) * LANES)


def _to_lane_tiles(scr, val):
    for c in range(scr.shape[0]):
        scr[c] = val[:, _lane_tile(c)]


def _emit_class_major(scr, refs, rows):
    for d, ref in zip(WIDE, refs):
        for r in range(d):
            for c in range(scr.shape[0]):
                ref[r, :, _lane_tile(c)] = scr[c, pl.ds(r, rows // d, stride=d), :].astype(ref.dtype)


def _mm_in(a, w_in):
    T = a.shape[0]
    tm = _row_tile(T)
    nq = -(-3 * DA // N_IN)

    def body(a_ref, w_ref, z_ref, *rest):
        scr = rest[-1]
        j = pl.program_id(1)
        acc = _dot(a_ref[...], w_ref[...], NN)
        z_ref[...] = acc.astype(BF16)

        @pl.when(j < nq)
        def _():
            _to_lane_tiles(scr, acc)
            _emit_class_major(scr, rest[:-1], tm)

    cm_spec = lambda d: pl.BlockSpec((d, tm // d, N_IN), lambda i, j: (0, i, jnp.minimum(j, nq - 1)))
    return pl.pallas_call(
        body, name="mm_in", grid=(T // tm, NDEV),
        in_specs=[pl.BlockSpec((tm, D), lambda i, j: (i, 0)),
                  pl.BlockSpec((None, D, N_IN), lambda i, j: (j, 0, 0))],
        out_specs=[pl.BlockSpec((tm, N_IN), lambda i, j: (i, j))] + [cm_spec(d) for d in WIDE],
        out_shape=[jax.ShapeDtypeStruct((T, DIN), BF16)]
        + [jax.ShapeDtypeStruct((d, T // d, nq * N_IN), BF16) for d in WIDE],
        scratch_shapes=[pltpu.VMEM((N_IN // LANES, tm, LANES), F32)],
        compiler_params=pltpu.CompilerParams(
            dimension_semantics=("parallel", "arbitrary"), vmem_limit_bytes=VMEM_BIG),
    )(a, w_in)


def _mm_out_bwd(dh1b, w_out, deps=()):
    T = dh1b.shape[0]
    tm = _row_tile(T)

    def body(dy_ref, w_ref, *rest):
        rest = rest[len(deps):]
        dom_ref, scr = rest[0], rest[-1]
        acc = _dot(dy_ref[...], w_ref[...], NT)
        dom_ref[...] = acc.astype(BF16)

        @pl.when(pl.program_id(1) == 0)
        def _():
            _to_lane_tiles(scr, acc)
            _emit_class_major(scr, rest[1:-1], tm)

    return pl.pallas_call(
        body, name="mm_out_bwd", grid=(T // tm, D // DA),
        in_specs=[pl.BlockSpec((tm, D), lambda i, j: (i, 0)),
                  pl.BlockSpec((DA, D), lambda i, j: (j, 0))] + [ANY_SPEC] * len(deps),
        out_specs=[pl.BlockSpec((tm, DA), lambda i, j: (i, j))]
        + [pl.BlockSpec((d, tm // d, DA), lambda i, j: (0, i, 0)) for d in WIDE],
        out_shape=[jax.ShapeDtypeStruct((T, D), BF16)]
        + [jax.ShapeDtypeStruct((d, T // d, DA), BF16) for d in WIDE],
        scratch_shapes=[pltpu.VMEM((DA // LANES, tm, LANES), F32)],
        compiler_params=pltpu.CompilerParams(
            dimension_semantics=("parallel", "arbitrary"), vmem_limit_bytes=VMEM_BIG),
    )(dh1b, w_out, *deps)


def _mm_out(o_attn, o_conv, w_out, x):
    T = x.shape[0]
    tm = _row_tile(T)
    tn = 1024
    return _mm(
        "mm_out", (T // tm, D // tn, 1),
        [pl.BlockSpec((tm, DA), lambda i, j, k: (i, 0)),
         pl.BlockSpec((DA, tn), lambda i, j, k: (0, j)),
         pl.BlockSpec((tm, DC), lambda i, j, k: (i, 0)),
         pl.BlockSpec((DC, tn), lambda i, j, k: (1, j)),
         pl.BlockSpec((tm, tn), lambda i, j, k: (i, j))],
        [o_attn, w_out, o_conv, w_out, x],
        [pl.BlockSpec((tm, tn), lambda i, j, k: (i, j))],
        [jax.ShapeDtypeStruct((T, D), F32)], NN, 2, _ep_resid)[0]


def _mm_gate_up(f, w_gate, w_up):
    T = f.shape[0]
    tm = _row_tile(T)

    def body(f_ref, wg_ref, wu_ref, g_ref, u_ref, a_ref):
        fv = f_ref[...]
        g = _dot(fv, wg_ref[...], NN)
        u = _dot(fv, wu_ref[...], NN)
        g_ref[...] = g.astype(BF16)
        u_ref[...] = u.astype(BF16)
        a_ref[...] = (g * _sigmoid(g) * u).astype(BF16)

    wspec = pl.BlockSpec((None, D, N_FF), lambda i, j: (j, 0, 0))
    ospec = pl.BlockSpec((None, tm, N_FF), lambda i, j: (j, i, 0))
    sh = jax.ShapeDtypeStruct((NDEV, T, N_FF), BF16)
    return pl.pallas_call(
        body, name="mm_gate_up", grid=(T // tm, NDEV),
        in_specs=[pl.BlockSpec((tm, D), lambda i, j: (i, 0)), wspec, wspec],
        out_specs=[ospec, ospec, ospec], out_shape=[sh, sh, sh],
        compiler_params=pltpu.CompilerParams(
            dimension_semantics=("parallel", "parallel"), vmem_limit_bytes=VMEM_BIG),
    )(f, w_gate, w_up)


def _mm_down(act, w_down, h1):
    T = h1.shape[0]
    tm = _row_tile(T)
    tn = 1024
    sg = 2
    return _mm(
        "mm_down", (T // tm, D // tn, NDEV // sg),
        [pl.BlockSpec((sg, tm, N_FF), lambda i, j, k: (k, i, 0)),
         pl.BlockSpec((sg, N_FF, tn), lambda i, j, k: (k, 0, j)),
         pl.BlockSpec((tm, tn), lambda i, j, k: (i, j))],
        [act, w_down, h1],
        [pl.BlockSpec((tm, tn), lambda i, j, k: (i, j))],
        [jax.ShapeDtypeStruct((T, D), F32)], NN, 1, _ep_resid, acc_shape=(tm, tn), group=sg)[0]


def _mm_ple(r, w_pgate, b_pgate, p, w_ple, h2):
    T = h2.shape[0]
    tm = _row_tile(T)
    tn = 1024

    def body(r_ref, wg_ref, b_ref, p_ref, wp_ref, h2_ref, gte_ref, pe_ref, h3_ref):
        gte = _sigmoid(_dot(r_ref[...], wg_ref[...], NN) + b_ref[...])
        pe = _dot(p_ref[...], wp_ref[...], NN)
        gte_ref[...] = gte.astype(BF16)
        pe_ref[...] = pe.astype(BF16)
        h3_ref[...] = h2_ref[...] + pe * gte

    tile = pl.BlockSpec((tm, tn), lambda i, j: (i, j))
    return pl.pallas_call(
        body, name="mm_ple", grid=(T // tm, D // tn),
        in_specs=[pl.BlockSpec((tm, D), lambda i, j: (i, 0)),
                  pl.BlockSpec((D, tn), lambda i, j: (0, j)),
                  pl.BlockSpec((1, tn), lambda i, j: (0, j)),
                  pl.BlockSpec((tm, DPLE), lambda i, j: (i, 0)),
                  pl.BlockSpec((DPLE, tn), lambda i, j: (0, j)),
                  tile],
        out_specs=[tile, tile, tile],
        out_shape=[jax.ShapeDtypeStruct((T, D), BF16), jax.ShapeDtypeStruct((T, D), BF16),
                   jax.ShapeDtypeStruct((T, D), F32)],
        compiler_params=pltpu.CompilerParams(
            dimension_semantics=("parallel", "parallel"), vmem_limit_bytes=VMEM_BIG),
    )(r, w_pgate, b_pgate, p, w_ple, h2)


def _mm_nt(name, dy, w, deps=()):
    T, n = dy.shape
    kdim = w.shape[0]
    tm = _row_tile(T)
    tn = 1024
    return _mm(
        name, (T // tm, kdim // tn, 1),
        [pl.BlockSpec((tm, n), lambda i, j, k: (i, 0)),
         pl.BlockSpec((tn, n), lambda i, j, k: (j, 0))],
        [dy, w],
        [pl.BlockSpec((tm, tn), lambda i, j, k: (i, j))],
        [jax.ShapeDtypeStruct((T, kdim), BF16)], NT, 1, _ep_cast(BF16), deps=deps)[0]


def _mm_down_bwd(dh2, w_down, g, u, deps=()):
    T = dh2.shape[0]
    tm = _row_tile(T)
    gspec = pl.BlockSpec((None, tm, N_FF), lambda i, j, k: (j, i, 0))
    sh = jax.ShapeDtypeStruct((NDEV, T, N_FF), BF16)
    return _mm(
        "mm_down_bwd", (T // tm, NDEV, 1),
        [pl.BlockSpec((tm, D), lambda i, j, k: (i, 0)),
         pl.BlockSpec((None, N_FF, D), lambda i, j, k: (j, 0, 0)),
         gspec, gspec],
        [dh2, w_down, g, u],
        [gspec, gspec], [sh, sh], NT, 1, _ep_swiglu_bwd, deps=deps)


def _mm_ffn_in_bwd(dg, w_gate, du, w_up, deps=()):
    T = dg.shape[1]
    tm = _row_tile(T)
    tn = 1024
    sg = 2
    aspec = pl.BlockSpec((sg, tm, N_FF), lambda i, j, k: (k, i, 0))
    wspec = pl.BlockSpec((sg, tn, N_FF), lambda i, j, k: (k, j, 0))
    return _mm(
        "mm_ffn_in_bwd", (T // tm, D // tn, NDEV // sg),
        [aspec, wspec, aspec, wspec], [dg, w_gate, du, w_up],
        [pl.BlockSpec((tm, tn), lambda i, j, k: (i, j))],
        [jax.ShapeDtypeStruct((T, D), BF16)], NT, 2, _ep_cast(BF16), acc_shape=(tm, tn),
        deps=deps, group=sg)[0]


def _mm_in_bwd(dz, w_in, deps=()):
    T = dz.shape[0]
    tm = _row_tile(T)
    tn = 1024
    sg = 4
    return _mm(
        "mm_in_bwd", (T // tm, D // tn, NDEV // sg),
        [pl.BlockSpec((tm, sg * N_IN), lambda i, j, k: (i, k)),
         pl.BlockSpec((sg, tn, N_IN), lambda i, j, k: (k, j, 0))],
        [dz, w_in],
        [pl.BlockSpec((tm, tn), lambda i, j, k: (i, j))],
        [jax.ShapeDtypeStruct((T, D), BF16)], NT, 1, _ep_cast(BF16), acc_shape=(tm, tn),
        deps=deps, group=sg, a_cols=N_IN)[0]


def _mm_tn(name, a, b, tj=None):
    T, idim = a.shape
    jdim = b.shape[1]
    tt = _row_tile(T)
    ti = min(idim, 1024)
    tj = jdim if tj is None else tj
    return _mm(
        name, (idim // ti, jdim // tj, T // tt),
        [pl.BlockSpec((tt, ti), lambda i, j, k: (k, i)),
         pl.BlockSpec((tt, tj), lambda i, j, k: (k, j))],
        [a, b],
        [pl.BlockSpec((ti, tj), lambda i, j, k: (i, j))],
        [jax.ShapeDtypeStruct((idim, jdim), BF16)], TN, 1, _ep_cast(BF16), acc_shape=(ti, tj))[0]


def _mm_tn_cols(name, a, b, ncol):
    T, idim = a.shape
    tt = _tn_rows(T)
    return _mm(
        name, (1, NDEV, T // tt),
        [pl.BlockSpec((tt, idim), lambda i, j, k: (k, 0)),
         pl.BlockSpec((tt, ncol), lambda i, j, k: (k, j))],
        [a, b],
        [pl.BlockSpec((None, idim, ncol), lambda i, j, k: (j, 0, 0))],
        [jax.ShapeDtypeStruct((NDEV, idim, ncol), BF16)], TN, 1, _ep_cast(BF16),
        acc_shape=(idim, ncol))[0]


def _mm_tn_ff_cols(name, a, b):
    T = a.shape[0]
    tt = _tn_rows(T)
    return _mm(
        name, (1, NDEV, T // tt),
        [pl.BlockSpec((tt, D), lambda i, j, k: (k, 0)),
         pl.BlockSpec((None, tt, N_FF), lambda i, j, k: (j, k, 0))],
        [a, b],
        [pl.BlockSpec((None, D, N_FF), lambda i, j, k: (j, 0, 0))],
        [jax.ShapeDtypeStruct((NDEV, D, N_FF), BF16)], TN, 1, _ep_cast(BF16),
        acc_shape=(D, N_FF))[0]


def _mm_tn_ff_rows(name, a, b):
    T = b.shape[0]
    tt = _tn_rows(T)
    return _mm(
        name, (NDEV, 1, T // tt),
        [pl.BlockSpec((None, tt, N_FF), lambda i, j, k: (i, k, 0)),
         pl.BlockSpec((tt, D), lambda i, j, k: (k, 0))],
        [a, b],
        [pl.BlockSpec((None, N_FF, D), lambda i, j, k: (i, 0, 0))],
        [jax.ShapeDtypeStruct((NDEV, N_FF, D), BF16)], TN, 1, _ep_cast(BF16),
        acc_shape=(N_FF, D))[0]


TR = 256


def _rows(T):
    return min(TR, T)


def _rms_fwd(name, h, g, deps=()):
    T = h.shape[0]
    tr = _rows(T)

    def body(h_ref, g_ref, *rest):
        o_ref = rest[-1]
        v = h_ref[...]
        r = lax.rsqrt(jnp.mean(v * v, axis=-1, keepdims=True) + EPS)
        o_ref[...] = (v * r * g_ref[...]).astype(BF16)

    return pl.pallas_call(
        body, name=name, grid=(T // tr,),
        in_specs=[pl.BlockSpec((tr, D), lambda i: (i, 0)), pl.BlockSpec((1, D), lambda i: (0, 0))]
        + [ANY_SPEC] * len(deps),
        out_specs=pl.BlockSpec((tr, D), lambda i: (i, 0)),
        out_shape=jax.ShapeDtypeStruct((T, D), BF16),
        compiler_params=pltpu.CompilerParams(dimension_semantics=("parallel",)),
    )(h, g, *deps)


def _fold8(v):
    return jnp.sum(v.reshape(v.shape[0] // 8, 8, v.shape[1]), axis=0)


def _rms_bwd(name, dn_out, h, g, dres, want_bf16):
    T = h.shape[0]
    tr = _rows(T)
    nt = T // tr

    def body(dy_ref, h_ref, g_ref, dres_ref, *rest):
        if want_bf16:
            dh_ref, dhb_ref, dg_ref, acc = rest
        else:
            dh_ref, dg_ref, acc = rest
        i = pl.program_id(0)
        v = h_ref[...]
        r = lax.rsqrt(jnp.mean(v * v, axis=-1, keepdims=True) + EPS)
        nrm = v * r
        dy = dy_ref[...].astype(F32)
        dn = dy * g_ref[...]
        dh = dres_ref[...] + r * (dn - nrm * jnp.mean(dn * nrm, axis=-1, keepdims=True))
        dh_ref[...] = dh
        if want_bf16:
            dhb_ref[...] = dh.astype(BF16)

        @pl.when(i == 0)
        def _():
            acc[...] = jnp.zeros_like(acc)

        acc[...] += _fold8(dy * nrm)

        @pl.when(i == nt - 1)
        def _():
            dg_ref[...] = jnp.sum(acc[...], axis=0, keepdims=True)

    tile = pl.BlockSpec((tr, D), lambda i: (i, 0))
    vec = pl.BlockSpec((1, D), lambda i: (0, 0))
    out_specs = [tile] + ([tile] if want_bf16 else []) + [vec]
    out_shape = ([jax.ShapeDtypeStruct((T, D), F32)]
                 + ([jax.ShapeDtypeStruct((T, D), BF16)] if want_bf16 else [])
                 + [jax.ShapeDtypeStruct((1, D), F32)])
    return pl.pallas_call(
        body, name=name, grid=(nt,),
        in_specs=[tile, tile, vec, tile], out_specs=out_specs, out_shape=out_shape,
        scratch_shapes=[pltpu.VMEM((8, D), F32)],
        compiler_params=pltpu.CompilerParams(dimension_semantics=("arbitrary",)),
    )(dn_out, h, g, dres)


def _loss_bwd(h3, target, g_final, pe, gte):
    T = h3.shape[0]
    tr = _rows(T)
    nt = T // tr

    def body(h_ref, t_ref, g_ref, pe_ref, gte_ref, loss_ref, dh_ref, dpe_ref, dpg_ref,
             dgf_ref, dbp_ref, lacc, gacc, bacc):
        i = pl.program_id(0)
        v = h_ref[...]
        r = lax.rsqrt(jnp.mean(v * v, axis=-1, keepdims=True) + EPS)
        nrm = v * r
        g = g_ref[...]
        err = nrm * g - t_ref[...]
        dy = err * (1.0 / D)
        dn = dy * g
        dh = r * (dn - nrm * jnp.mean(dn * nrm, axis=-1, keepdims=True))
        dh_ref[...] = dh
        gte = gte_ref[...].astype(F32)
        pe = pe_ref[...].astype(F32)
        dpe_ref[...] = (dh * gte).astype(BF16)
        dpg = dh * pe * gte * (1.0 - gte)
        dpg_ref[...] = dpg.astype(BF16)

        @pl.when(i == 0)
        def _():
            lacc[...] = jnp.zeros_like(lacc)
            gacc[...] = jnp.zeros_like(gacc)
            bacc[...] = jnp.zeros_like(bacc)

        lacc[...] += _fold8(err * err)
        gacc[...] += _fold8(dy * nrm)
        bacc[...] += _fold8(dpg)

        @pl.when(i == nt - 1)
        def _():
            tot = jnp.sum(jnp.sum(lacc[...], axis=0, keepdims=True), axis=1, keepdims=True)
            loss_ref[...] = jnp.broadcast_to(tot * (0.5 / D), (1, 128))
            dgf_ref[...] = jnp.sum(gacc[...], axis=0, keepdims=True)
            dbp_ref[...] = jnp.sum(bacc[...], axis=0, keepdims=True)

    tile = pl.BlockSpec((tr, D), lambda i: (i, 0))
    vec = pl.BlockSpec((1, D), lambda i: (0, 0))
    return pl.pallas_call(
        body, name="loss_bwd", grid=(nt,),
        in_specs=[tile, tile, vec, tile, tile],
        out_specs=[pl.BlockSpec((1, 128), lambda i: (0, 0)), tile, tile, tile, vec, vec],
        out_shape=[jax.ShapeDtypeStruct((1, 128), F32), jax.ShapeDtypeStruct((T, D), F32),
                   jax.ShapeDtypeStruct((T, D), BF16), jax.ShapeDtypeStruct((T, D), BF16),
                   jax.ShapeDtypeStruct((1, D), F32), jax.ShapeDtypeStruct((1, D), F32)],
        scratch_shapes=[pltpu.VMEM((8, D), F32)] * 3,
        compiler_params=pltpu.CompilerParams(dimension_semantics=("arbitrary",)),
    )(h3, target, g_final, pe, gte)


def _band_masks():
    qi = lax.broadcasted_iota(jnp.int32, (BLK, BLK), 0)
    kj = lax.broadcasted_iota(jnp.int32, (BLK, BLK), 1)
    return kj >= qi, kj <= qi


def _cm_spec(d, col, rowmap=lambda n: n):
    if d == 1:
        return pl.BlockSpec((BLK, DA), lambda r, n: (rowmap(n), col))
    return pl.BlockSpec((None, BLK, DA), lambda r, n: (r, rowmap(n), col))


def _cm_shape(d, T, dtype):
    return jax.ShapeDtypeStruct((T, DA) if d == 1 else (d, T // d, DA), dtype)


def _attn_fwd(name, zsrc, d, T):
    nb = T // d // BLK
    scale = DH ** -0.5

    def body(q_ref, kp_ref, kc_ref, vp_ref, vc_ref, o_ref, l_ref):
        n = pl.program_id(1)
        prev_ok, cur_ok = _band_masks()
        prev_ok = prev_ok & (n > 0)
        heads = [slice(h * DH, (h + 1) * DH) for h in range(NH)]
        s = [(jnp.where(prev_ok, _dot(q_ref[:, sl], kp_ref[:, sl], NT) * scale, NEG),
              jnp.where(cur_ok, _dot(q_ref[:, sl], kc_ref[:, sl], NT) * scale, NEG)) for sl in heads]
        m = [jnp.maximum(jnp.max(sp, axis=1, keepdims=True), jnp.max(sc, axis=1, keepdims=True))
             for sp, sc in s]
        p = [(jnp.exp(sp - mh), jnp.exp(sc - mh)) for (sp, sc), mh in zip(s, m)]
        den = [jnp.sum(pp, axis=1, keepdims=True) + jnp.sum(pc, axis=1, keepdims=True) for pp, pc in p]
        o = [_dot(pp.astype(BF16), vp_ref[:, sl], NN) + _dot(pc.astype(BF16), vc_ref[:, sl], NN)
             for (pp, pc), sl in zip(p, heads)]
        o_ref[...] = jnp.concatenate([(oh / dh).astype(BF16) for oh, dh in zip(o, den)], axis=1)
        l_ref[...] = jnp.concatenate(
            [jnp.broadcast_to(mh + jnp.log(dh), (BLK, DH)) for mh, dh in zip(m, den)], axis=1)

    prev = lambda n: jnp.maximum(n - 1, 0)
    return pl.pallas_call(
        body, name=name, grid=(d, nb),
        in_specs=[_cm_spec(d, 0), _cm_spec(d, 1, prev), _cm_spec(d, 1), _cm_spec(d, 2, prev),
                  _cm_spec(d, 2)],
        out_specs=[_cm_spec(d, 0)] * 2,
        out_shape=[_cm_shape(d, T, BF16), _cm_shape(d, T, F32)],
        compiler_params=pltpu.CompilerParams(dimension_semantics=("parallel", "parallel")),
    )(zsrc, zsrc, zsrc, zsrc, zsrc)


def _cm_tile(d, tr):
    if d == 1:
        return pl.BlockSpec((tr, DA), lambda i: (i, 0))
    return pl.BlockSpec((d, tr // d, DA), lambda i: (0, i, 0))


def _attn_combine(outs, lses, T):
    tr = _rows(T)

    def body(*refs):
        o_in, l_in = refs[:3], refs[3:6]
        o_ref, l_ref = refs[6:8]
        o_cm, l_cm = refs[8:8 + len(WIDE)], refs[8 + len(WIDE):8 + 2 * len(WIDE)]
        so, sl, so_all, sl_all = refs[8 + 2 * len(WIDE):]
        for c in range(DA // LANES):
            lt = _lane_tile(c)
            os_, ls_ = [o_in[0][:, lt].astype(F32)], [l_in[0][:, lt]]
            for w, d in enumerate(WIDE):
                for r in range(d):
                    so[w, c, pl.ds(r, tr // d, stride=d), :] = o_in[1 + w][r, :, lt].astype(F32)
                    sl[w, c, pl.ds(r, tr // d, stride=d), :] = l_in[1 + w][r, :, lt]
                os_.append(so[w, c])
                ls_.append(sl[w, c])
            la, lb, lc = ls_
            m = jnp.maximum(jnp.maximum(la, lb), lc)
            ea, eb, ec = jnp.exp(la - m), jnp.exp(lb - m), jnp.exp(lc - m)
            s = ea + eb + ec
            o = (ea * os_[0] + eb * os_[1] + ec * os_[2]) / s
            lse = m + jnp.log(s)
            o_ref[:, lt] = o.astype(BF16)
            l_ref[:, lt] = lse
            so_all[c] = o
            sl_all[c] = lse
        _emit_class_major(so_all, o_cm, tr)
        _emit_class_major(sl_all, l_cm, tr)

    specs = [_cm_tile(d, tr) for d in DILATIONS]
    wide = [_cm_tile(d, tr) for d in WIDE]
    return pl.pallas_call(
        body, name="attn_combine", grid=(T // tr,),
        in_specs=specs + specs,
        out_specs=[specs[0], specs[0]] + wide + wide,
        out_shape=[_cm_shape(1, T, BF16), _cm_shape(1, T, F32)]
        + [_cm_shape(d, T, BF16) for d in WIDE] + [_cm_shape(d, T, F32) for d in WIDE],
        scratch_shapes=[pltpu.VMEM((len(WIDE), DA // LANES, tr, LANES), F32)] * 2
        + [pltpu.VMEM((DA // LANES, tr, LANES), F32)] * 2,
        compiler_params=pltpu.CompilerParams(
            dimension_semantics=("parallel",), vmem_limit_bytes=VMEM_MID),
    )(*outs, *lses)


def _attn_bwd_q(name, zsrc, dosrc, osrc, lsrc, d, T):
    nb = T // d // BLK
    scale = DH ** -0.5

    def body(q_ref, kp_ref, kc_ref, vp_ref, vc_ref, do_ref, o_ref, l_ref, dq_ref):
        n = pl.program_id(1)
        prev_ok, cur_ok = _band_masks()
        prev_ok = prev_ok & (n > 0)
        heads = [slice(h * DH, (h + 1) * DH) for h in range(NH)]
        s = [(_dot(q_ref[:, sl], kp_ref[:, sl], NT), _dot(q_ref[:, sl], kc_ref[:, sl], NT))
             for sl in heads]
        dp = [(_dot(do_ref[:, sl], vp_ref[:, sl], NT), _dot(do_ref[:, sl], vc_ref[:, sl], NT))
              for sl in heads]
        delta = [jnp.sum(do_ref[:, sl].astype(F32) * o_ref[:, sl].astype(F32), axis=1, keepdims=True)
                 for sl in heads]
        p = [(jnp.exp(jnp.where(prev_ok, sp * scale - l_ref[:, sl], NEG)),
              jnp.exp(jnp.where(cur_ok, sc * scale - l_ref[:, sl], NEG)))
             for (sp, sc), sl in zip(s, heads)]
        ds = [((pp * (dpp - dl) * scale).astype(BF16), (pc * (dpc - dl) * scale).astype(BF16))
              for (pp, pc), (dpp, dpc), dl in zip(p, dp, delta)]
        dq = [_dot(dsp, kp_ref[:, sl], NN) + _dot(dsc, kc_ref[:, sl], NN)
              for (dsp, dsc), sl in zip(ds, heads)]
        dq_ref[...] = jnp.concatenate([v.astype(BF16) for v in dq], axis=1)

    prev = lambda n: jnp.maximum(n - 1, 0)
    own = _cm_spec(d, 0)
    return pl.pallas_call(
        body, name=name, grid=(d, nb),
        in_specs=[own, _cm_spec(d, 1, prev), _cm_spec(d, 1), _cm_spec(d, 2, prev), _cm_spec(d, 2),
                  own, own, own],
        out_specs=own, out_shape=_cm_shape(d, T, BF16),
        compiler_params=pltpu.CompilerParams(dimension_semantics=("parallel", "parallel")),
    )(zsrc, zsrc, zsrc, zsrc, zsrc, dosrc, osrc, lsrc)


def _attn_bwd_kv(name, zsrc, dosrc, osrc, lsrc, d, T):
    nb = T // d // BLK
    scale = DH ** -0.5

    def body(k_ref, v_ref, qa_ref, qb_ref, doa_ref, dob_ref, oa_ref, ob_ref, la_ref, lb_ref,
             dk_ref, dv_ref):
        j = pl.program_id(1)
        next_ok, own_ok = _band_masks()
        next_ok = next_ok & (j < nb - 1)
        heads = [slice(h * DH, (h + 1) * DH) for h in range(NH)]
        s = [(_dot(qa_ref[:, sl], k_ref[:, sl], NT), _dot(qb_ref[:, sl], k_ref[:, sl], NT))
             for sl in heads]
        dp = [(_dot(doa_ref[:, sl], v_ref[:, sl], NT), _dot(dob_ref[:, sl], v_ref[:, sl], NT))
              for sl in heads]
        delta = [(jnp.sum(doa_ref[:, sl].astype(F32) * oa_ref[:, sl].astype(F32), axis=1, keepdims=True),
                  jnp.sum(dob_ref[:, sl].astype(F32) * ob_ref[:, sl].astype(F32), axis=1, keepdims=True))
                 for sl in heads]
        p = [(jnp.exp(jnp.where(own_ok, sa * scale - la_ref[:, sl], NEG)),
              jnp.exp(jnp.where(next_ok, sb * scale - lb_ref[:, sl], NEG)))
             for (sa, sb), sl in zip(s, heads)]
        dv = [_dot(pa.astype(BF16), doa_ref[:, sl], TN) + _dot(pb.astype(BF16), dob_ref[:, sl], TN)
              for (pa, pb), sl in zip(p, heads)]
        ds = [((pa * (dpa - da) * scale).astype(BF16), (pb * (dpb - db) * scale).astype(BF16))
              for (pa, pb), (dpa, dpb), (da, db) in zip(p, dp, delta)]
        dk = [_dot(dsa, qa_ref[:, sl], TN) + _dot(dsb, qb_ref[:, sl], TN)
              for (dsa, dsb), sl in zip(ds, heads)]
        dk_ref[...] = jnp.concatenate([v.astype(BF16) for v in dk], axis=1)
        dv_ref[...] = jnp.concatenate([v.astype(BF16) for v in dv], axis=1)

    nxt = lambda j: jnp.minimum(j + 1, nb - 1)
    own, own_n = _cm_spec(d, 0), _cm_spec(d, 0, nxt)
    sh = _cm_shape(d, T, BF16)
    return pl.pallas_call(
        body, name=name, grid=(d, nb),
        in_specs=[_cm_spec(d, 1), _cm_spec(d, 2), own, own_n, own, own_n, own, own_n, own, own_n],
        out_specs=[own, own], out_shape=[sh, sh],
        compiler_params=pltpu.CompilerParams(dimension_semantics=("parallel", "parallel")),
    )(zsrc, zsrc, zsrc, zsrc, dosrc, dosrc, osrc, osrc, lsrc, lsrc)


def _dz_assemble(dqs, dks, dvs, dcvg, T):
    tr = _rows(T)
    nb = len(DILATIONS)

    def body(*refs):
        cvg_ref, dz_ref, scr = refs[3 * nb], refs[3 * nb + 1], refs[3 * nb + 2]
        for g in range(3):
            parts = refs[g * nb:(g + 1) * nb]
            for c in range(DA // LANES):
                lt = _lane_tile(c)
                scr[g, c] = parts[0][:, lt].astype(F32)
                for w, d in enumerate(WIDE):
                    for r in range(d):
                        rows = pl.ds(r, tr // d, stride=d)
                        scr[g, c, rows, :] = scr[g, c, rows, :] + parts[1 + w][r, :, lt].astype(F32)
                dz_ref[:, g * DA + c * LANES:g * DA + (c + 1) * LANES] = scr[g, c].astype(BF16)
        dz_ref[:, 3 * DA:] = cvg_ref[...]

    specs = [_cm_tile(d, tr) for d in DILATIONS]
    return pl.pallas_call(
        body, name="dz_assemble", grid=(T // tr,),
        in_specs=specs * 3 + [pl.BlockSpec((tr, 2 * DC), lambda i: (i, 0))],
        out_specs=pl.BlockSpec((tr, DIN), lambda i: (i, 0)),
        out_shape=jax.ShapeDtypeStruct((T, DIN), BF16),
        scratch_shapes=[pltpu.VMEM((3, DA // LANES, tr, LANES), F32)],
        compiler_params=pltpu.CompilerParams(
            dimension_semantics=("parallel",), vmem_limit_bytes=VMEM_MID),
    )(*dqs, *dks, *dvs, dcvg)


CT = 256
HALO = 32
RC = 32


def _conv_fwd(z, w_dw, b_dw, g_ln, b_ln):
    T = z.shape[0]
    ct = min(CT, T)
    nt = T // ct
    hb = ct // HALO

    def body(cv_ref, cg_ref, cvp_ref, cgp_ref, w_ref, bdw_ref, g_ref, b_ref, oc_ref, y_ref, ubuf, ush):
        i = pl.program_id(0)
        up = cvp_ref[...].astype(F32) * _sigmoid(cgp_ref[...].astype(F32))
        ubuf[0:HALO, :] = jnp.where(i > 0, up, 0.0)
        ubuf[HALO:, :] = cv_ref[...].astype(F32) * _sigmoid(cg_ref[...].astype(F32))
        for b in range(8):
            ush[b] = ubuf[pl.ds(8 - b, ct + 24), :]

        def chunk(ci, carry):
            r0 = pl.multiple_of(ci * RC, RC)
            acc = jnp.broadcast_to(bdw_ref[...], (RC, DC))
            for s in range(CW):
                a, b = divmod(s, 8)
                acc = acc + w_ref[CW - 1 - s:CW - s, :] * ush[b, pl.ds(r0 + 24 - 8 * a, RC), :]
            y_ref[pl.ds(r0, RC), :] = acc
            mu = jnp.mean(acc, axis=-1, keepdims=True)
            cen = acc - mu
            var = jnp.mean(cen * cen, axis=-1, keepdims=True)
            ln = cen * lax.rsqrt(var + EPS) * g_ref[...] + b_ref[...]
            oc_ref[pl.ds(r0, RC), :] = (ln * _sigmoid(ln)).astype(BF16)
            return carry

        lax.fori_loop(0, ct // RC, chunk, 0)

    cur = lambda col: pl.BlockSpec((ct, DC), lambda i: (i, col))
    prv = lambda col: pl.BlockSpec((HALO, DC), lambda i: (jnp.maximum(i * hb - 1, 0), col))
    vec = pl.BlockSpec((1, DC), lambda i: (0, 0))
    return pl.pallas_call(
        body, name="conv_fwd", grid=(nt,),
        in_specs=[cur(3), cur(4), prv(3), prv(4), pl.BlockSpec((CW, DC), lambda i: (0, 0)),
                  vec, vec, vec],
        out_specs=[pl.BlockSpec((ct, DC), lambda i: (i, 0))] * 2,
        out_shape=[jax.ShapeDtypeStruct((T, DC), BF16), jax.ShapeDtypeStruct((T, DC), F32)],
        scratch_shapes=[pltpu.VMEM((ct + HALO, DC), F32), pltpu.VMEM((8, ct + 24, DC), F32)],
        compiler_params=pltpu.CompilerParams(
            dimension_semantics=("parallel",), vmem_limit_bytes=VMEM_MID),
    )(z, z, z, z, w_dw, b_dw, g_ln, b_ln)


def _conv_bwd(z, dom, y, w_dw, g_ln, b_ln):
    T = z.shape[0]
    ct = min(CT, T)
    nt = T // ct
    hb = ct // HALO
    last_halo = T // HALO - 1

    def ln_bwd(yv, dov, g_ref, b_ref):
        mu = jnp.mean(yv, axis=-1, keepdims=True)
        cen = yv - mu
        rstd = lax.rsqrt(jnp.mean(cen * cen, axis=-1, keepdims=True) + EPS)
        xhat = cen * rstd
        ln = xhat * g_ref[...] + b_ref[...]
        sg = _sigmoid(ln)
        dln = dov * (sg * (1.0 + ln * (1.0 - sg)))
        dxh = dln * g_ref[...]
        dy = rstd * (dxh - jnp.mean(dxh, axis=-1, keepdims=True)
                     - xhat * jnp.mean(dxh * xhat, axis=-1, keepdims=True))
        return dy, dln, xhat

    def body(do_ref, don_ref, y_ref, yn_ref, cv_ref, cg_ref, cvp_ref, cgp_ref, w_ref, g_ref, b_ref,
             dcvg_ref, dw_ref, dbdw_ref, dg_ref, db_ref,
             dybuf, dysh, ubuf, ush, dwacc, vacc):
        i = pl.program_id(0)

        @pl.when(i == 0)
        def _():
            dwacc[...] = jnp.zeros_like(dwacc)
            vacc[...] = jnp.zeros_like(vacc)

        def ln_chunk(ci, carry):
            r0 = pl.multiple_of(ci * RC, RC)
            dy, dln, xhat = ln_bwd(y_ref[pl.ds(r0, RC), :], do_ref[pl.ds(r0, RC), :].astype(F32),
                                   g_ref, b_ref)
            dybuf[pl.ds(r0, RC), :] = dy
            vacc[0] += _fold8(dy)
            vacc[1] += _fold8(dln * xhat)
            vacc[2] += _fold8(dln)
            return carry

        lax.fori_loop(0, ct // RC, ln_chunk, 0)
        dyn, _, _ = ln_bwd(yn_ref[...], don_ref[...].astype(F32), g_ref, b_ref)
        dybuf[ct:, :] = jnp.where(i < nt - 1, dyn, 0.0)
        for b in range(8):
            dysh[b] = dybuf[pl.ds(b, ct + 24), :]

        up = cvp_ref[...].astype(F32) * _sigmoid(cgp_ref[...].astype(F32))
        ubuf[0:HALO, :] = jnp.where(i > 0, up, 0.0)
        ubuf[HALO:, :] = cv_ref[...].astype(F32) * _sigmoid(cg_ref[...].astype(F32))
        for b in range(8):
            ush[b] = ubuf[pl.ds(8 - b, ct + 24), :]

        def chunk(ci, carry):
            r0 = pl.multiple_of(ci * RC, RC)
            dy = dybuf[pl.ds(r0, RC), :]
            du = jnp.zeros((RC, DC), F32)
            for s in range(CW):
                a, b = divmod(s, 8)
                du = du + w_ref[CW - 1 - s:CW - s, :] * dysh[b, pl.ds(r0 + 8 * a, RC), :]
                dwacc[CW - 1 - s] += _fold8(dy * ush[b, pl.ds(r0 + 24 - 8 * a, RC), :])
            cv = cv_ref[pl.ds(r0, RC), :].astype(F32)
            sg = _sigmoid(cg_ref[pl.ds(r0, RC), :].astype(F32))
            dcvg_ref[pl.ds(r0, RC), 0:DC] = (du * sg).astype(BF16)
            dcvg_ref[pl.ds(r0, RC), DC:2 * DC] = (du * cv * sg * (1.0 - sg)).astype(BF16)
            return carry

        lax.fori_loop(0, ct // RC, chunk, 0)

        @pl.when(i == nt - 1)
        def _():
            dw_ref[...] = jnp.sum(dwacc[...], axis=1)
            dbdw_ref[...] = jnp.sum(vacc[0], axis=0, keepdims=True)
            dg_ref[...] = jnp.sum(vacc[1], axis=0, keepdims=True)
            db_ref[...] = jnp.sum(vacc[2], axis=0, keepdims=True)

    cur = lambda col: pl.BlockSpec((ct, DC), lambda i: (i, col))
    prv = lambda col: pl.BlockSpec((HALO, DC), lambda i: (jnp.maximum(i * hb - 1, 0), col))
    nxt = lambda col: pl.BlockSpec((HALO, DC), lambda i: (jnp.minimum((i + 1) * hb, last_halo), col))
    vec = pl.BlockSpec((1, DC), lambda i: (0, 0))
    tile = pl.BlockSpec((ct, DC), lambda i: (i, 0))
    return pl.pallas_call(
        body, name="conv_bwd", grid=(nt,),
        in_specs=[cur(1), nxt(1), cur(0), nxt(0), cur(3), cur(4), prv(3), prv(4),
                  pl.BlockSpec((CW, DC), lambda i: (0, 0)), vec, vec],
        out_specs=[pl.BlockSpec((ct, 2 * DC), lambda i: (i, 0)),
                   pl.BlockSpec((CW, DC), lambda i: (0, 0)), vec, vec, vec],
        out_shape=[jax.ShapeDtypeStruct((T, 2 * DC), BF16),
                   jax.ShapeDtypeStruct((CW, DC), F32), jax.ShapeDtypeStruct((1, DC), F32),
                   jax.ShapeDtypeStruct((1, DC), F32), jax.ShapeDtypeStruct((1, DC), F32)],
        scratch_shapes=[pltpu.VMEM((ct + HALO, DC), F32), pltpu.VMEM((8, ct + 24, DC), F32),
                        pltpu.VMEM((ct + HALO, DC), F32), pltpu.VMEM((8, ct + 24, DC), F32),
                        pltpu.VMEM((CW, 8, DC), F32), pltpu.VMEM((3, 8, DC), F32)],
        compiler_params=pltpu.CompilerParams(
            dimension_semantics=("arbitrary",), vmem_limit_bytes=VMEM_BIG),
    )(dom, dom, y, y, z, z, z, z, w_dw, g_ln, b_ln)


def _adam_math(w, g, m, v):
    m = ADAM_B1 * m + (1.0 - ADAM_B1) * g
    v = ADAM_B2 * v + (1.0 - ADAM_B2) * (g * g)
    m_hat = m / (1.0 - ADAM_B1 ** ADAM_STEP)
    v_hat = v / (1.0 - ADAM_B2 ** ADAM_STEP)
    delta = -ADAM_LR * (m_hat / (jnp.sqrt(v_hat) + ADAM_EPS) + ADAM_WD * w)
    return delta, m, v


def _adam(name, slots, w, m, v):
    rows, cols = w.shape
    tr = next(t for t in (256, 176, 128, 64, 32, 16, 8, rows) if rows % t == 0)

    def body(s_ref, w_ref, m_ref, v_ref, g_out, d_out, m_out, v_out):
        g = s_ref[0].astype(F32)
        for s in range(1, NDEV):
            g = g + s_ref[s].astype(F32)
        delta, mn, vn = _adam_math(w_ref[...], g, m_ref[...], v_ref[...])
        g_out[...] = g
        d_out[...] = delta
        m_out[...] = mn
        v_out[...] = vn

    tile = pl.BlockSpec((tr, cols), lambda i: (i, 0))
    sh = jax.ShapeDtypeStruct((rows, cols), F32)
    return pl.pallas_call(
        body, name=name, grid=(rows // tr,),
        in_specs=[pl.BlockSpec((NDEV, tr, cols), lambda i: (0, i, 0)), tile, tile, tile],
        out_specs=[tile] * 4, out_shape=[sh] * 4,
        compiler_params=pltpu.CompilerParams(
            dimension_semantics=("parallel",), vmem_limit_bytes=VMEM_MID),
    )(slots, w, m, v)


SMALL_NAMES = ("g_mix", "b_dw", "g_conv_ln", "b_conv_ln", "g_ffn", "g_ple", "b_pgate", "g_final")


def _pack_small(vecs, w_dw_full):
    rows = [jnp.pad(v.reshape(1, -1), ((0, 0), (0, SMALL_W - v.size))) for v in vecs]
    rows.append(jnp.pad(w_dw_full, ((0, 0), (0, SMALL_W - DC))))
    rows.append(jnp.zeros((SMALL_ROWS - len(vecs) - CW, SMALL_W), F32))
    return jnp.concatenate(rows, axis=0)


def kernel(x, p, g_mix, w_in, w_dw, b_dw, g_conv_ln, b_conv_ln, w_out, g_ffn, w_gate, w_up, w_down, g_ple, w_pgate, b_pgate, w_ple, g_final, loss_target, m_g_mix, m_w_in, m_w_dw, m_b_dw, m_g_conv_ln, m_b_conv_ln, m_w_out, m_g_ffn, m_w_gate, m_w_up, m_w_down, m_g_ple, m_w_pgate, m_b_pgate, m_w_ple, m_g_final, v_g_mix, v_w_in, v_w_dw, v_b_dw, v_g_conv_ln, v_b_conv_ln, v_w_out, v_g_ffn, v_w_gate, v_w_up, v_w_down, v_g_ple, v_w_pgate, v_b_pgate, v_w_ple, v_g_final):
    T = x.shape[1]
    me = 4 * lax.axis_index("x") + 2 * lax.axis_index("y") + lax.axis_index("c")
    xs = x.reshape(T, D)
    ps = p.reshape(T, DPLE).astype(BF16)
    tgt = loss_target.reshape(T, D)
    g_final2 = g_final.reshape(1, D)

    big = dict(w_in=w_in[0], w_out=w_out[0], w_gate=w_gate[0], w_up=w_up[0], w_down=w_down[0],
               w_pgate=w_pgate[0], w_ple=w_ple[0])
    order = ("w_in", "w_out", "w_gate", "w_up", "w_down", "w_pgate", "w_ple")
    g_order = ("w_dw",) + order
    g_items = [(w_dw.reshape(CW, DC // NDEV), False)] + [(big[n].astype(BF16), False) for n in order]
    g_handles, g_token = _xstart("gather_start", g_items, _place("gather_place", g_items))
    G = dict(zip(g_order, g_handles))

    a = _rms_fwd("rms_mix", xs, g_mix, deps=[g_token])
    w_dw_f = _xwait("gather_wait_w_dw", G["w_dw"], a).transpose(1, 0, 2).reshape(CW, DC)
    w_in_f = _xwait("gather_wait_w_in", G["w_in"], a)
    z, *z_wide = _mm_in(a, w_in_f)
    zsrc = dict(zip(DILATIONS, [z] + z_wide))
    br = [_attn_fwd(f"attn_fwd_d{d}", zsrc[d], d, T) for d in DILATIONS]
    comb = list(_attn_combine([b[0] for b in br], [b[1] for b in br], T))
    o_attn, lse = comb[0], comb[1]
    osrc = dict(zip(DILATIONS, [o_attn] + comb[2:2 + len(WIDE)]))
    lsrc = dict(zip(DILATIONS, [lse] + comb[2 + len(WIDE):]))
    o_conv, y_conv = _conv_fwd(z, w_dw_f, b_dw, g_conv_ln, b_conv_ln)
    w_out_f = _xwait("gather_wait_w_out", G["w_out"], o_conv).reshape(D, D)
    h1 = _mm_out(o_attn, o_conv, w_out_f, xs)
    f = _rms_fwd("rms_ffn", h1, g_ffn)
    w_gate_f = _xwait("gather_wait_w_gate", G["w_gate"], f)
    w_up_f = _xwait("gather_wait_w_up", G["w_up"], f)
    gate, up, act = _mm_gate_up(f, w_gate_f, w_up_f)
    w_down_f = _xwait("gather_wait_w_down", G["w_down"], act)
    h2 = _mm_down(act, w_down_f, h1)
    r = _rms_fwd("rms_ple", h2, g_ple)
    w_pgate_f = _xwait("gather_wait_w_pgate", G["w_pgate"], r).reshape(D, D)
    w_ple_f = _xwait("gather_wait_w_ple", G["w_ple"], r).transpose(1, 0, 2).reshape(DPLE, D)
    gte, pe, h3 = _mm_ple(r, w_pgate_f, b_pgate, ps, w_ple_f, h2)

    loss_part, dh3, dpe, dpg, d_g_final, d_b_pgate = _loss_bwd(h3, tgt, g_final2, pe, gte)
    H = {}

    def send_grads(tag, named):
        items = [(v, True) for _, v in named]
        handles, token = _xstart(f"grads_start_{tag}", items, _place(f"grads_place_{tag}", items))
        H.update(zip([n for n, _ in named], handles))
        return token

    gw_pgate = _mm_tn("gw_pgate", r, dpg).reshape(NDEV, D // NDEV, D)
    gw_ple = _mm_tn("gw_ple", ps, dpe).reshape(DPLE, NDEV, D // NDEV).transpose(1, 0, 2)
    tok = send_grads("ple", [("w_pgate", gw_pgate), ("w_ple", gw_ple)])
    dr = _mm_nt("mm_pgate_bwd", dpg, w_pgate_f, deps=[tok])
    dh2, dh2b, d_g_ple = _rms_bwd("rms_ple_bwd", dr, h2, g_ple, dh3, True)
    gw_down = _mm_tn_ff_rows("gw_down", act, dh2b)
    tok = send_grads("down", [("w_down", gw_down)])
    dgate, dup = _mm_down_bwd(dh2b, w_down_f, gate, up, deps=[tok])
    gw_gate = _mm_tn_ff_cols("gw_gate", f, dgate)
    gw_up = _mm_tn_ff_cols("gw_up", f, dup)
    tok = send_grads("ffn", [("w_gate", gw_gate), ("w_up", gw_up)])
    df = _mm_ffn_in_bwd(dgate, w_gate_f, dup, w_up_f, deps=[tok])
    dh1, dh1b, d_g_ffn = _rms_bwd("rms_ffn_bwd", df, h1, g_ffn, dh2, True)
    gw_out = jnp.concatenate(
        [_mm_tn("gw_out_attn", o_attn, dh1b), _mm_tn("gw_out_conv", o_conv, dh1b)], axis=0)
    tok = send_grads("out", [("w_out", gw_out.reshape(NDEV, D // NDEV, D))])
    dom, *do_wide = _mm_out_bwd(dh1b, w_out_f, deps=[tok])
    dosrc = dict(zip(DILATIONS, [dom] + do_wide))
    dcvg, d_w_dw, d_b_dw, d_g_ln, d_b_ln = _conv_bwd(z, dom, y_conv, w_dw_f, g_conv_ln, b_conv_ln)
    dqs, dks, dvs = [], [], []
    for d in DILATIONS:
        dqs.append(_attn_bwd_q(f"attn_bwd_q_d{d}", zsrc[d], dosrc[d], osrc[d], lsrc[d], d, T))
        dk, dv = _attn_bwd_kv(f"attn_bwd_kv_d{d}", zsrc[d], dosrc[d], osrc[d], lsrc[d], d, T)
        dks.append(dk)
        dvs.append(dv)
    dz = _dz_assemble(dqs, dks, dvs, dcvg, T)
    gw_in = _mm_tn_cols("gw_in", a, dz, N_IN)
    tok = send_grads("in", [("w_in", gw_in)])
    da = _mm_in_bwd(dz, w_in_f, deps=[tok])
    grad_x, d_g_mix = _rms_bwd("rms_mix_bwd", da, xs, g_mix, dh1, False)

    small_part = _pack_small(
        [d_g_mix, d_b_dw, d_g_ln, d_b_ln, d_g_ffn, d_g_ple, d_b_pgate, d_g_final], d_w_dw)
    small_slots = _exchange("exchange_small_grads", [(small_part, False)])[0]
    S = {n: _xwait(f"grads_wait_{n}", H[n], small_slots)
         for n in ("w_pgate", "w_ple", "w_down", "w_gate", "w_up", "w_out", "w_in")}

    mom = dict(w_in=(m_w_in, v_w_in), w_out=(m_w_out, v_w_out), w_gate=(m_w_gate, v_w_gate),
               w_up=(m_w_up, v_w_up), w_down=(m_w_down, v_w_down), w_pgate=(m_w_pgate, v_w_pgate),
               w_ple=(m_w_ple, v_w_ple))
    upd = {}
    for n in order:
        res = _adam(f"adam_{n}", S[n], big[n], mom[n][0][0], mom[n][1][0])
        upd[n] = [t[None] for t in res]

    def lanes(v):
        full = jnp.zeros((CW, NDEV, DC // NDEV), F32)
        full = lax.dynamic_update_slice(full, v.reshape(CW, 1, DC // NDEV), (0, me, 0))
        return full.reshape(CW, DC)

    small_w = _pack_small([g_mix, b_dw, g_conv_ln, b_conv_ln, g_ffn, g_ple, b_pgate, g_final2], lanes(w_dw))
    small_m = _pack_small([m_g_mix, m_b_dw, m_g_conv_ln, m_b_conv_ln, m_g_ffn, m_g_ple, m_b_pgate,
                           m_g_final.reshape(1, D)], lanes(m_w_dw))
    small_v = _pack_small([v_g_mix, v_b_dw, v_g_conv_ln, v_b_conv_ln, v_g_ffn, v_g_ple, v_b_pgate,
                           v_g_final.reshape(1, D)], lanes(v_w_dw))
    small_res = _adam("adam_small", small_slots, small_w, small_m, small_v)

    def unpack(t):
        out = {}
        widths = dict(g_mix=D, b_dw=DC, g_conv_ln=DC, b_conv_ln=DC, g_ffn=D, g_ple=D, b_pgate=D, g_final=D)
        for i, n in enumerate(SMALL_NAMES):
            out[n] = t[i:i + 1, :widths[n]]
        out["g_final"] = out["g_final"].reshape(D)
        taps = t[len(SMALL_NAMES):len(SMALL_NAMES) + CW, :DC].reshape(CW, NDEV, DC // NDEV)
        out["w_dw"] = lax.dynamic_slice(taps, (0, me, 0), (CW, 1, DC // NDEV))[None]
        return out

    small = [unpack(t) for t in small_res]

    loss = lax.psum(loss_part[0, 0], ("x", "y", "c"))
    names = ("g_mix", "w_in", "w_dw", "b_dw", "g_conv_ln", "b_conv_ln", "w_out", "g_ffn", "w_gate",
             "w_up", "w_down", "g_ple", "w_pgate", "b_pgate", "w_ple", "g_final")
    outs = [loss, grad_x.reshape(1, T, D)]
    for kind in range(4):
        for n in names:
            outs.append(upd[n][kind] if n in upd else small[kind][n])
    return tuple(outs)
```

```python
import jax
import jax.numpy as jnp
from jax import lax
from jax.experimental import pallas as pl
from jax.experimental.pallas import tpu as pltpu

F32 = jnp.float32
BF16 = jnp.bfloat16

NDEV = 8
D = 2048
NH = 8
DH = 128
DA = NH * DH
DC = D - DA
DIN = 3 * DA + 2 * DC
DFF = 5632
DPLE = 256
BLK = 128
DILATIONS = (1, 4, 16)
CW = 31
EPS = 1e-6
N_IN = DIN // NDEV
N_FF = DFF // NDEV
NEG = -1e30

ADAM_LR = 0.001
ADAM_B1 = 0.9
ADAM_B2 = 0.999
ADAM_EPS = 1e-08
ADAM_WD = 0.01
ADAM_STEP = 10

VMEM_CAP_V7X = 64 * 1024 * 1024
VMEM_BIG = VMEM_CAP_V7X - 12 * 1024 * 1024
VMEM_MID = 40 * 1024 * 1024

SMALL_W = 2048
SMALL_ROWS = 40


def _sigmoid(v):
    return 1.0 / (1.0 + jnp.exp(-v))


def _dot(a, b, contract):
    return lax.dot_general(a, b, (contract, ((), ())), preferred_element_type=F32)


NN = ((1,), (0,))
NT = ((1,), (1,))
TN = ((0,), (0,))


def _exchange(name, items):
    n = len(items)
    out_shape = [
        jax.ShapeDtypeStruct((NDEV,) + (a.shape[1:] if sc else a.shape), a.dtype)
        for a, sc in items
    ]
    scat = [sc for _, sc in items]

    def body(*refs):
        srcs = refs[:n]
        dsts = refs[n:2 * n]
        send_sems, recv_sems, loc_sems = refs[2 * n:]
        x = lax.axis_index("x")
        y = lax.axis_index("y")
        c = lax.axis_index("c")
        me = 4 * x + 2 * y + c

        local = []
        for i in range(n):
            src = srcs[i].at[me] if scat[i] else srcs[i]
            cp = pltpu.make_async_copy(src, dsts[i].at[me], loc_sems.at[i])
            cp.start()
            local.append(cp)

        remote = []
        for k in range(1, NDEV):
            px = (1 - x) if (k >> 2) & 1 else x
            py = (1 - y) if (k >> 1) & 1 else y
            pc = (1 - c) if k & 1 else c
            peer = 4 * px + 2 * py + pc
            for i in range(n):
                sem = i * (NDEV - 1) + k - 1
                src = srcs[i].at[peer] if scat[i] else srcs[i]
                send = pltpu.make_async_remote_copy(
                    src_ref=src, dst_ref=dsts[i].at[me],
                    send_sem=send_sems.at[sem], recv_sem=recv_sems.at[sem],
                    device_id=(px, py, pc), device_id_type=pl.DeviceIdType.MESH)
                send.start()
                recv = pltpu.make_async_remote_copy(
                    src_ref=src, dst_ref=dsts[i].at[peer],
                    send_sem=send_sems.at[sem], recv_sem=recv_sems.at[sem],
                    device_id=(px, py, pc), device_id_type=pl.DeviceIdType.MESH)
                remote.append((send, recv))
        for send, recv in remote:
            recv.wait_recv()
            send.wait_send()
        for cp in local:
            cp.wait()

    any_spec = pl.BlockSpec(memory_space=pl.ANY)
    return pl.pallas_call(
        body, name=name,
        in_specs=[any_spec] * n, out_specs=[any_spec] * n, out_shape=out_shape,
        scratch_shapes=[
            pltpu.SemaphoreType.DMA((n * (NDEV - 1),)),
            pltpu.SemaphoreType.DMA((n * (NDEV - 1),)),
            pltpu.SemaphoreType.DMA((n,)),
        ],
    )(*[a for a, _ in items])


HBM_SPEC = pl.BlockSpec(memory_space=pltpu.HBM)
SEM_SPEC = pl.BlockSpec(memory_space=pltpu.SEMAPHORE)
ANY_SPEC = pl.BlockSpec(memory_space=pl.ANY)
EFFECT = pltpu.SideEffectType.DATAFLOW_SIDE_EFFECTING


def _peer_of(k):
    x = lax.axis_index("x")
    y = lax.axis_index("y")
    c = lax.axis_index("c")
    px = (1 - x) if (k >> 2) & 1 else x
    py = (1 - y) if (k >> 1) & 1 else y
    pc = (1 - c) if k & 1 else c
    return (px, py, pc), 4 * px + 2 * py + pc


def _my_index():
    return 4 * lax.axis_index("x") + 2 * lax.axis_index("y") + lax.axis_index("c")


def _slot_shape(a, sc):
    return (NDEV,) + (a.shape[1:] if sc else a.shape)


def _divisor_tile(rows):
    return next((t for t in (512, 256, 176, 128, 64, 32, 16) if rows % t == 0), rows)


def _place(name, items):
    lands = []
    for idx, (a, sc) in enumerate(items):
        rows, cols = a.shape[-2:]
        tr = _divisor_tile(rows)

        def body(s_ref, o_ref):
            o_ref[...] = s_ref[...]

        mine = pl.BlockSpec((None, tr, cols), lambda i: (_my_index(), i, 0))
        lands.append(pl.pallas_call(
            body, name=f"{name}_{idx}", grid=(rows // tr,),
            in_specs=[mine if sc else pl.BlockSpec((tr, cols), lambda i: (i, 0))],
            out_specs=mine,
            out_shape=jax.ShapeDtypeStruct(_slot_shape(a, sc), a.dtype),
            compiler_params=pltpu.CompilerParams(dimension_semantics=("parallel",)),
        )(a))
    return lands


def _xstart(name, items, lands):
    n = len(items)
    scat = [sc for _, sc in items]

    def body(*refs):
        srcs = refs[:n]
        lzs = refs[n:2 * n]
        send_sems = refs[2 * n:3 * n]
        recv_sems = refs[3 * n:4 * n]
        token = refs[-1]
        me = _my_index()
        for i in range(n):
            for k in range(1, NDEV):
                peer_id, peer = _peer_of(k)
                src = srcs[i].at[peer] if scat[i] else srcs[i]
                pltpu.make_async_remote_copy(
                    src_ref=src, dst_ref=lzs[i].at[me],
                    send_sem=send_sems[i].at[k - 1], recv_sem=recv_sems[i].at[k - 1],
                    device_id=peer_id, device_id_type=pl.DeviceIdType.MESH).start()
        token[...] = jnp.zeros_like(token)

    sem = pltpu.SemaphoreType.DMA((NDEV - 1,))
    hbm = [pltpu.HBM(a.shape, a.dtype) for a, _ in items] + [pltpu.HBM(l.shape, l.dtype) for l in lands]
    res = pl.pallas_call(
        body, name=name,
        in_specs=[HBM_SPEC] * (2 * n),
        out_specs=[SEM_SPEC] * (2 * n) + [HBM_SPEC] * (2 * n) + [pl.BlockSpec(memory_space=pltpu.VMEM)],
        out_shape=[sem] * (2 * n) + hbm + [jax.ShapeDtypeStruct((8, 128), F32)],
        input_output_aliases={i: 2 * n + i for i in range(2 * n)},
        compiler_params=pltpu.CompilerParams(has_side_effects=EFFECT),
    )(*[pltpu.with_memory_space_constraint(a, pltpu.HBM) for a, _ in items],
      *[pltpu.with_memory_space_constraint(l, pltpu.HBM) for l in lands])
    handles = [(res[i], res[n + i], res[2 * n + i], res[3 * n + i], scat[i]) for i in range(n)]
    return handles, res[-1]


def _xwait(name, handle, after):
    send_sem, recv_sem, src, land, sc = handle

    def body(src_ref, land_ref, send_ref, recv_ref, after_ref, src_dead, got_ref):
        for k in range(1, NDEV):
            peer_id, peer = _peer_of(k)
            cp = pltpu.make_async_remote_copy(
                src_ref=src_ref.at[peer] if sc else src_ref, dst_ref=land_ref.at[peer],
                send_sem=send_ref.at[k - 1], recv_sem=recv_ref.at[k - 1],
                device_id=peer_id, device_id_type=pl.DeviceIdType.MESH)
            cp.wait_send()
            cp.wait_recv()

    return pl.pallas_call(
        body, name=name,
        in_specs=[HBM_SPEC, HBM_SPEC, SEM_SPEC, SEM_SPEC, ANY_SPEC],
        out_specs=[HBM_SPEC, HBM_SPEC],
        out_shape=[pltpu.HBM(src.shape, src.dtype), pltpu.HBM(land.shape, land.dtype)],
        input_output_aliases={0: 0, 1: 1},
        compiler_params=pltpu.CompilerParams(has_side_effects=EFFECT),
    )(src, land, send_sem, recv_sem, after)[1]


def _mm(name, grid, in_specs, operands, out_specs, out_shape, contract, n_pairs, epilogue,
        acc_shape=None, vmem=VMEM_BIG, deps=(), group=1, a_cols=None):
    nk = grid[2]

    def shard(ref, s, is_a):
        if group == 1:
            return ref[...]
        if is_a and a_cols is not None:
            return ref[:, s * a_cols:(s + 1) * a_cols]
        return ref[s]
    n_extra = len(operands) - 2 * n_pairs
    n_out = len(out_shape)
    n_in = len(operands) + len(deps)
    in_specs = list(in_specs) + [ANY_SPEC] * len(deps)
    operands = list(operands) + list(deps)

    def body(*refs):
        ab = refs[:2 * n_pairs]
        extras = refs[2 * n_pairs:2 * n_pairs + n_extra]
        outs = refs[n_in:n_in + n_out]
        dots = [(ab[2 * p], ab[2 * p + 1], s) for p in range(n_pairs) for s in range(group)]
        if nk == 1:
            part = None
            for a_ref, b_ref, s in dots:
                d = _dot(shard(a_ref, s, True), shard(b_ref, s, False), contract)
                part = d if part is None else part + d
            epilogue(part, extras, outs)
        else:
            acc_ref = refs[-1]
            k = pl.program_id(2)

            @pl.when(k == 0)
            def _():
                acc_ref[...] = jnp.zeros_like(acc_ref)

            for a_ref, b_ref, s in dots:
                acc_ref[...] += _dot(shard(a_ref, s, True), shard(b_ref, s, False), contract)

            @pl.when(k == nk - 1)
            def _():
                epilogue(acc_ref[...], extras, outs)

    scratch = [pltpu.VMEM(acc_shape, F32)] if nk > 1 else []
    return pl.pallas_call(
        body, name=name, grid=grid, in_specs=in_specs, out_specs=out_specs, out_shape=out_shape,
        scratch_shapes=scratch,
        compiler_params=pltpu.CompilerParams(
            dimension_semantics=("parallel", "parallel", "arbitrary"), vmem_limit_bytes=vmem),
    )(*operands)


def _ep_cast(dtype):
    def ep(acc, extras, outs):
        outs[0][...] = acc.astype(dtype)
    return ep


def _ep_resid(acc, extras, outs):
    outs[0][...] = extras[0][...] + acc


def _ep_swiglu_bwd(acc, extras, outs):
    g = extras[0][...].astype(F32)
    u = extras[1][...].astype(F32)
    sg = _sigmoid(g)
    outs[0][...] = (acc * u * (sg * (1.0 + g * (1.0 - sg)))).astype(BF16)
    outs[1][...] = (acc * (g * sg)).astype(BF16)


def _row_tile(T):
    return min(1024, T)


def _tn_rows(T):
    return min(2048, T)


WIDE = tuple(d for d in DILATIONS if d > 1)


LANES = 128


def _lane_tile(c):
    return slice(c * LANES, (c + 1) * LANES)


def _to_lane_tiles(scr, val):
    for c in range(scr.shape[0]):
        scr[c] = val[:, _lane_tile(c)]


def _emit_class_major(scr, refs, rows):
    for d, ref in zip(WIDE, refs):
        for r in range(d):
            for c in range(scr.shape[0]):
                ref[r, :, _lane_tile(c)] = scr[c, pl.ds(r, rows // d, stride=d), :].astype(ref.dtype)


def _mm_in(a, w_in):
    T = a.shape[0]
    tm = _row_tile(T)
    nq = -(-3 * DA // N_IN)

    def body(a_ref, w_ref, z_ref, *rest):
        scr = rest[-1]
        j = pl.program_id(1)
        acc = _dot(a_ref[...], w_ref[...], NN)
        z_ref[...] = acc.astype(BF16)

        @pl.when(j < nq)
        def _():
            _to_lane_tiles(scr, acc)
            _emit_class_major(scr, rest[:-1], tm)

    cm_spec = lambda d: pl.BlockSpec((d, tm // d, N_IN), lambda i, j: (0, i, jnp.minimum(j, nq - 1)))
    return pl.pallas_call(
        body, name="mm_in", grid=(T // tm, NDEV),
        in_specs=[pl.BlockSpec((tm, D), lambda i, j: (i, 0)),
                  pl.BlockSpec((None, D, N_IN), lambda i, j: (j, 0, 0))],
        out_specs=[pl.BlockSpec((tm, N_IN), lambda i, j: (i, j))] + [cm_spec(d) for d in WIDE],
        out_shape=[jax.ShapeDtypeStruct((T, DIN), BF16)]
        + [jax.ShapeDtypeStruct((d, T // d, nq * N_IN), BF16) for d in WIDE],
        scratch_shapes=[pltpu.VMEM((N_IN // LANES, tm, LANES), F32)],
        compiler_params=pltpu.CompilerParams(
            dimension_semantics=("parallel", "arbitrary"), vmem_limit_bytes=VMEM_BIG),
    )(a, w_in)


def _mm_out_bwd(dh1b, w_out, deps=()):
    T = dh1b.shape[0]
    tm = _row_tile(T)

    def body(dy_ref, w_ref, *rest):
        rest = rest[len(deps):]
        dom_ref, scr = rest[0], rest[-1]
        acc = _dot(dy_ref[...], w_ref[...], NT)
        dom_ref[...] = acc.astype(BF16)

        @pl.when(pl.program_id(1) == 0)
        def _():
            _to_lane_tiles(scr, acc)
            _emit_class_major(scr, rest[1:-1], tm)

    return pl.pallas_call(
        body, name="mm_out_bwd", grid=(T // tm, D // DA),
        in_specs=[pl.BlockSpec((tm, D), lambda i, j: (i, 0)),
                  pl.BlockSpec((DA, D), lambda i, j: (j, 0))] + [ANY_SPEC] * len(deps),
        out_specs=[pl.BlockSpec((tm, DA), lambda i, j: (i, j))]
        + [pl.BlockSpec((d, tm // d, DA), lambda i, j: (0, i, 0)) for d in WIDE],
        out_shape=[jax.ShapeDtypeStruct((T, D), BF16)]
        + [jax.ShapeDtypeStruct((d, T // d, DA), BF16) for d in WIDE],
        scratch_shapes=[pltpu.VMEM((DA // LANES, tm, LANES), F32)],
        compiler_params=pltpu.CompilerParams(
            dimension_semantics=("parallel", "arbitrary"), vmem_limit_bytes=VMEM_BIG),
    )(dh1b, w_out, *deps)


def _mm_out(o_attn, o_conv, w_out, x):
    T = x.shape[0]
    tm = _row_tile(T)
    tn = 1024
    return _mm(
        "mm_out", (T // tm, D // tn, 1),
        [pl.BlockSpec((tm, DA), lambda i, j, k: (i, 0)),
         pl.BlockSpec((DA, tn), lambda i, j, k: (0, j)),
         pl.BlockSpec((tm, DC), lambda i, j, k: (i, 0)),
         pl.BlockSpec((DC, tn), lambda i, j, k: (1, j)),
         pl.BlockSpec((tm, tn), lambda i, j, k: (i, j))],
        [o_attn, w_out, o_conv, w_out, x],
        [pl.BlockSpec((tm, tn), lambda i, j, k: (i, j))],
        [jax.ShapeDtypeStruct((T, D), F32)], NN, 2, _ep_resid)[0]


def _mm_gate_up(f, w_gate, w_up):
    T = f.shape[0]
    tm = _row_tile(T)

    def body(f_ref, wg_ref, wu_ref, g_ref, u_ref, a_ref):
        fv = f_ref[...]
        g = _dot(fv, wg_ref[...], NN)
        u = _dot(fv, wu_ref[...], NN)
        g_ref[...] = g.astype(BF16)
        u_ref[...] = u.astype(BF16)
        a_ref[...] = (g * _sigmoid(g) * u).astype(BF16)

    wspec = pl.BlockSpec((None, D, N_FF), lambda i, j: (j, 0, 0))
    ospec = pl.BlockSpec((None, tm, N_FF), lambda i, j: (j, i, 0))
    sh = jax.ShapeDtypeStruct((NDEV, T, N_FF), BF16)
    return pl.pallas_call(
        body, name="mm_gate_up", grid=(T // tm, NDEV),
        in_specs=[pl.BlockSpec((tm, D), lambda i, j: (i, 0)), wspec, wspec],
        out_specs=[ospec, ospec, ospec], out_shape=[sh, sh, sh],
        compiler_params=pltpu.CompilerParams(
            dimension_semantics=("parallel", "parallel"), vmem_limit_bytes=VMEM_BIG),
    )(f, w_gate, w_up)


def _mm_down(act, w_down, h1):
    T = h1.shape[0]
    tm = _row_tile(T)
    tn = 1024
    sg = 2
    return _mm(
        "mm_down", (T // tm, D // tn, NDEV // sg),
        [pl.BlockSpec((sg, tm, N_FF), lambda i, j, k: (k, i, 0)),
         pl.BlockSpec((sg, N_FF, tn), lambda i, j, k: (k, 0, j)),
         pl.BlockSpec((tm, tn), lambda i, j, k: (i, j))],
        [act, w_down, h1],
        [pl.BlockSpec((tm, tn), lambda i, j, k: (i, j))],
        [jax.ShapeDtypeStruct((T, D), F32)], NN, 1, _ep_resid, acc_shape=(tm, tn), group=sg)[0]


def _mm_ple(r, w_pgate, b_pgate, p, w_ple, h2):
    T = h2.shape[0]
    tm = _row_tile(T)
    tn = 1024

    def body(r_ref, wg_ref, b_ref, p_ref, wp_ref, h2_ref, gte_ref, pe_ref, h3_ref):
        gte = _sigmoid(_dot(r_ref[...], wg_ref[...], NN) + b_ref[...])
        pe = _dot(p_ref[...], wp_ref[...], NN)
        gte_ref[...] = gte.astype(BF16)
        pe_ref[...] = pe.astype(BF16)
        h3_ref[...] = h2_ref[...] + pe * gte

    tile = pl.BlockSpec((tm, tn), lambda i, j: (i, j))
    return pl.pallas_call(
        body, name="mm_ple", grid=(T // tm, D // tn),
        in_specs=[pl.BlockSpec((tm, D), lambda i, j: (i, 0)),
                  pl.BlockSpec((D, tn), lambda i, j: (0, j)),
                  pl.BlockSpec((1, tn), lambda i, j: (0, j)),
                  pl.BlockSpec((tm, DPLE), lambda i, j: (i, 0)),
                  pl.BlockSpec((DPLE, tn), lambda i, j: (0, j)),
                  tile],
        out_specs=[tile, tile, tile],
        out_shape=[jax.ShapeDtypeStruct((T, D), BF16), jax.ShapeDtypeStruct((T, D), BF16),
                   jax.ShapeDtypeStruct((T, D), F32)],
        compiler_params=pltpu.CompilerParams(
            dimension_semantics=("parallel", "parallel"), vmem_limit_bytes=VMEM_BIG),
    )(r, w_pgate, b_pgate, p, w_ple, h2)


def _mm_nt(name, dy, w, deps=()):
    T, n = dy.shape
    kdim = w.shape[0]
    tm = _row_tile(T)
    tn = 1024
    return _mm(
        name, (T // tm, kdim // tn, 1),
        [pl.BlockSpec((tm, n), lambda i, j, k: (i, 0)),
         pl.BlockSpec((tn, n), lambda i, j, k: (j, 0))],
        [dy, w],
        [pl.BlockSpec((tm, tn), lambda i, j, k: (i, j))],
        [jax.ShapeDtypeStruct((T, kdim), BF16)], NT, 1, _ep_cast(BF16), deps=deps)[0]


def _mm_down_bwd(dh2, w_down, g, u, deps=()):
    T = dh2.shape[0]
    tm = _row_tile(T)
    gspec = pl.BlockSpec((None, tm, N_FF), lambda i, j, k: (j, i, 0))
    sh = jax.ShapeDtypeStruct((NDEV, T, N_FF), BF16)
    return _mm(
        "mm_down_bwd", (T // tm, NDEV, 1),
        [pl.BlockSpec((tm, D), lambda i, j, k: (i, 0)),
         pl.BlockSpec((None, N_FF, D), lambda i, j, k: (j, 0, 0)),
         gspec, gspec],
        [dh2, w_down, g, u],
        [gspec, gspec], [sh, sh], NT, 1, _ep_swiglu_bwd, deps=deps)


def _mm_ffn_in_bwd(dg, w_gate, du, w_up, deps=()):
    T = dg.shape[1]
    tm = _row_tile(T)
    tn = 1024
    sg = 2
    aspec = pl.BlockSpec((sg, tm, N_FF), lambda i, j, k: (k, i, 0))
    wspec = pl.BlockSpec((sg, tn, N_FF), lambda i, j, k: (k, j, 0))
    return _mm(
        "mm_ffn_in_bwd", (T // tm, D // tn, NDEV // sg),
        [aspec, wspec, aspec, wspec], [dg, w_gate, du, w_up],
        [pl.BlockSpec((tm, tn), lambda i, j, k: (i, j))],
        [jax.ShapeDtypeStruct((T, D), BF16)], NT, 2, _ep_cast(BF16), acc_shape=(tm, tn),
        deps=deps, group=sg)[0]


def _mm_in_bwd(dz, w_in, deps=()):
    T = dz.shape[0]
    tm = _row_tile(T)
    tn = 1024
    sg = 4
    return _mm(
        "mm_in_bwd", (T // tm, D // tn, NDEV // sg),
        [pl.BlockSpec((tm, sg * N_IN), lambda i, j, k: (i, k)),
         pl.BlockSpec((sg, tn, N_IN), lambda i, j, k: (k, j, 0))],
        [dz, w_in],
        [pl.BlockSpec((tm, tn), lambda i, j, k: (i, j))],
        [jax.ShapeDtypeStruct((T, D), BF16)], NT, 1, _ep_cast(BF16), acc_shape=(tm, tn),
        deps=deps, group=sg, a_cols=N_IN)[0]


def _mm_tn(name, a, b, tj=None):
    T, idim = a.shape
    jdim = b.shape[1]
    tt = _row_tile(T)
    ti = min(idim, 1024)
    tj = jdim if tj is None else tj
    return _mm(
        name, (idim // ti, jdim // tj, T // tt),
        [pl.BlockSpec((tt, ti), lambda i, j, k: (k, i)),
         pl.BlockSpec((tt, tj), lambda i, j, k: (k, j))],
        [a, b],
        [pl.BlockSpec((ti, tj), lambda i, j, k: (i, j))],
        [jax.ShapeDtypeStruct((idim, jdim), BF16)], TN, 1, _ep_cast(BF16), acc_shape=(ti, tj))[0]


def _mm_tn_cols(name, a, b, ncol):
    T, idim = a.shape
    tt = _tn_rows(T)
    return _mm(
        name, (1, NDEV, T // tt),
        [pl.BlockSpec((tt, idim), lambda i, j, k: (k, 0)),
         pl.BlockSpec((tt, ncol), lambda i, j, k: (k, j))],
        [a, b],
        [pl.BlockSpec((None, idim, ncol), lambda i, j, k: (j, 0, 0))],
        [jax.ShapeDtypeStruct((NDEV, idim, ncol), BF16)], TN, 1, _ep_cast(BF16),
        acc_shape=(idim, ncol))[0]


def _mm_tn_ff_cols(name, a, b):
    T = a.shape[0]
    tt = _tn_rows(T)
    return _mm(
        name, (1, NDEV, T // tt),
        [pl.BlockSpec((tt, D), lambda i, j, k: (k, 0)),
         pl.BlockSpec((None, tt, N_FF), lambda i, j, k: (j, k, 0))],
        [a, b],
        [pl.BlockSpec((None, D, N_FF), lambda i, j, k: (j, 0, 0))],
        [jax.ShapeDtypeStruct((NDEV, D, N_FF), BF16)], TN, 1, _ep_cast(BF16),
        acc_shape=(D, N_FF))[0]


def _mm_tn_ff_rows(name, a, b):
    T = b.shape[0]
    tt = _tn_rows(T)
    return _mm(
        name, (NDEV, 1, T // tt),
        [pl.BlockSpec((None, tt, N_FF), lambda i, j, k: (i, k, 0)),
         pl.BlockSpec((tt, D), lambda i, j, k: (k, 0))],
        [a, b],
        [pl.BlockSpec((None, N_FF, D), lambda i, j, k: (i, 0, 0))],
        [jax.ShapeDtypeStruct((NDEV, N_FF, D), BF16)], TN, 1, _ep_cast(BF16),
        acc_shape=(N_FF, D))[0]


TR = 256


def _rows(T):
    return min(TR, T)


def _rms_fwd(name, h, g, deps=()):
    T = h.shape[0]
    tr = _rows(T)

    def body(h_ref, g_ref, *rest):
        o_ref = rest[-1]
        v = h_ref[...]
        r = lax.rsqrt(jnp.mean(v * v, axis=-1, keepdims=True) + EPS)
        o_ref[...] = (v * r * g_ref[...]).astype(BF16)

    return pl.pallas_call(
        body, name=name, grid=(T // tr,),
        in_specs=[pl.BlockSpec((tr, D), lambda i: (i, 0)), pl.BlockSpec((1, D), lambda i: (0, 0))]
        + [ANY_SPEC] * len(deps),
        out_specs=pl.BlockSpec((tr, D), lambda i: (i, 0)),
        out_shape=jax.ShapeDtypeStruct((T, D), BF16),
        compiler_params=pltpu.CompilerParams(dimension_semantics=("parallel",)),
    )(h, g, *deps)


def _fold8(v):
    return jnp.sum(v.reshape(v.shape[0] // 8, 8, v.shape[1]), axis=0)


def _rms_bwd(name, dn_out, h, g, dres, want_bf16):
    T = h.shape[0]
    tr = _rows(T)
    nt = T // tr

    def body(dy_ref, h_ref, g_ref, dres_ref, *rest):
        if want_bf16:
            dh_ref, dhb_ref, dg_ref, acc = rest
        else:
            dh_ref, dg_ref, acc = rest
        i = pl.program_id(0)
        v = h_ref[...]
        r = lax.rsqrt(jnp.mean(v * v, axis=-1, keepdims=True) + EPS)
        nrm = v * r
        dy = dy_ref[...].astype(F32)
        dn = dy * g_ref[...]
        dh = dres_ref[...] + r * (dn - nrm * jnp.mean(dn * nrm, axis=-1, keepdims=True))
        dh_ref[...] = dh
        if want_bf16:
            dhb_ref[...] = dh.astype(BF16)

        @pl.when(i == 0)
        def _():
            acc[...] = jnp.zeros_like(acc)

        acc[...] += _fold8(dy * nrm)

        @pl.when(i == nt - 1)
        def _():
            dg_ref[...] = jnp.sum(acc[...], axis=0, keepdims=True)

    tile = pl.BlockSpec((tr, D), lambda i: (i, 0))
    vec = pl.BlockSpec((1, D), lambda i: (0, 0))
    out_specs = [tile] + ([tile] if want_bf16 else []) + [vec]
    out_shape = ([jax.ShapeDtypeStruct((T, D), F32)]
                 + ([jax.ShapeDtypeStruct((T, D), BF16)] if want_bf16 else [])
                 + [jax.ShapeDtypeStruct((1, D), F32)])
    return pl.pallas_call(
        body, name=name, grid=(nt,),
        in_specs=[tile, tile, vec, tile], out_specs=out_specs, out_shape=out_shape,
        scratch_shapes=[pltpu.VMEM((8, D), F32)],
        compiler_params=pltpu.CompilerParams(dimension_semantics=("arbitrary",)),
    )(dn_out, h, g, dres)


def _loss_bwd(h3, target, g_final, pe, gte):
    T = h3.shape[0]
    tr = _rows(T)
    nt = T // tr

    def body(h_ref, t_ref, g_ref, pe_ref, gte_ref, loss_ref, dh_ref, dpe_ref, dpg_ref,
             dgf_ref, dbp_ref, lacc, gacc, bacc):
        i = pl.program_id(0)
        v = h_ref[...]
        r = lax.rsqrt(jnp.mean(v * v, axis=-1, keepdims=True) + EPS)
        nrm = v * r
        g = g_ref[...]
        err = nrm * g - t_ref[...]
        dy = err * (1.0 / D)
        dn = dy * g
        dh = r * (dn - nrm * jnp.mean(dn * nrm, axis=-1, keepdims=True))
        dh_ref[...] = dh
        gte = gte_ref[...].astype(F32)
        pe = pe_ref[...].astype(F32)
        dpe_ref[...] = (dh * gte).astype(BF16)
        dpg = dh * pe * gte * (1.0 - gte)
        dpg_ref[...] = dpg.astype(BF16)

        @pl.when(i == 0)
        def _():
            lacc[...] = jnp.zeros_like(lacc)
            gacc[...] = jnp.zeros_like(gacc)
            bacc[...] = jnp.zeros_like(bacc)

        lacc[...] += _fold8(err * err)
        gacc[...] += _fold8(dy * nrm)
        bacc[...] += _fold8(dpg)

        @pl.when(i == nt - 1)
        def _():
            tot = jnp.sum(jnp.sum(lacc[...], axis=0, keepdims=True), axis=1, keepdims=True)
            loss_ref[...] = jnp.broadcast_to(tot * (0.5 / D), (1, 128))
            dgf_ref[...] = jnp.sum(gacc[...], axis=0, keepdims=True)
            dbp_ref[...] = jnp.sum(bacc[...], axis=0, keepdims=True)

    tile = pl.BlockSpec((tr, D), lambda i: (i, 0))
    vec = pl.BlockSpec((1, D), lambda i: (0, 0))
    return pl.pallas_call(
        body, name="loss_bwd", grid=(nt,),
        in_specs=[tile, tile, vec, tile, tile],
        out_specs=[pl.BlockSpec((1, 128), lambda i: (0, 0)), tile, tile, tile, vec, vec],
        out_shape=[jax.ShapeDtypeStruct((1, 128), F32), jax.ShapeDtypeStruct((T, D), F32),
                   jax.ShapeDtypeStruct((T, D), BF16), jax.ShapeDtypeStruct((T, D), BF16),
                   jax.ShapeDtypeStruct((1, D), F32), jax.ShapeDtypeStruct((1, D), F32)],
        scratch_shapes=[pltpu.VMEM((8, D), F32)] * 3,
        compiler_params=pltpu.CompilerParams(dimension_semantics=("arbitrary",)),
    )(h3, target, g_final, pe, gte)


def _band_masks():
    qi = lax.broadcasted_iota(jnp.int32, (BLK, BLK), 0)
    kj = lax.broadcasted_iota(jnp.int32, (BLK, BLK), 1)
    return kj >= qi, kj <= qi


def _cm_spec(d, col, rowmap=lambda n: n):
    if d == 1:
        return pl.BlockSpec((BLK, DA), lambda r, n: (rowmap(n), col))
    return pl.BlockSpec((None, BLK, DA), lambda r, n: (r, rowmap(n), col))


def _cm_shape(d, T, dtype):
    return jax.ShapeDtypeStruct((T, DA) if d == 1 else (d, T // d, DA), dtype)


def _attn_fwd(name, zsrc, d, T):
    nb = T // d // BLK
    scale = DH ** -0.5

    def body(q_ref, kp_ref, kc_ref, vp_ref, vc_ref, o_ref, l_ref):
        n = pl.program_id(1)
        prev_ok, cur_ok = _band_masks()
        prev_ok = prev_ok & (n > 0)
        heads = [slice(h * DH, (h + 1) * DH) for h in range(NH)]
        s = [(jnp.where(prev_ok, _dot(q_ref[:, sl], kp_ref[:, sl], NT) * scale, NEG),
              jnp.where(cur_ok, _dot(q_ref[:, sl], kc_ref[:, sl], NT) * scale, NEG)) for sl in heads]
        m = [jnp.maximum(jnp.max(sp, axis=1, keepdims=True), jnp.max(sc, axis=1, keepdims=True))
             for sp, sc in s]
        p = [(jnp.exp(sp - mh), jnp.exp(sc - mh)) for (sp, sc), mh in zip(s, m)]
        den = [jnp.sum(pp, axis=1, keepdims=True) + jnp.sum(pc, axis=1, keepdims=True) for pp, pc in p]
        o = [_dot(pp.astype(BF16), vp_ref[:, sl], NN) + _dot(pc.astype(BF16), vc_ref[:, sl], NN)
             for (pp, pc), sl in zip(p, heads)]
        o_ref[...] = jnp.concatenate([(oh / dh).astype(BF16) for oh, dh in zip(o, den)], axis=1)
        l_ref[...] = jnp.concatenate(
            [jnp.broadcast_to(mh + jnp.log(dh), (BLK, DH)) for mh, dh in zip(m, den)], axis=1)

    prev = lambda n: jnp.maximum(n - 1, 0)
    return pl.pallas_call(
        body, name=name, grid=(d, nb),
        in_specs=[_cm_spec(d, 0), _cm_spec(d, 1, prev), _cm_spec(d, 1), _cm_spec(d, 2, prev),
                  _cm_spec(d, 2)],
        out_specs=[_cm_spec(d, 0)] * 2,
        out_shape=[_cm_shape(d, T, BF16), _cm_shape(d, T, F32)],
        compiler_params=pltpu.CompilerParams(dimension_semantics=("parallel", "parallel")),
    )(zsrc, zsrc, zsrc, zsrc, zsrc)


def _cm_tile(d, tr):
    if d == 1:
        return pl.BlockSpec((tr, DA), lambda i: (i, 0))
    return pl.BlockSpec((d, tr // d, DA), lambda i: (0, i, 0))


def _attn_combine(outs, lses, T):
    tr = _rows(T)

    def body(*refs):
        o_in, l_in = refs[:3], refs[3:6]
        o_ref, l_ref = refs[6:8]
        o_cm, l_cm = refs[8:8 + len(WIDE)], refs[8 + len(WIDE):8 + 2 * len(WIDE)]
        so, sl, so_all, sl_all = refs[8 + 2 * len(WIDE):]
        for c in range(DA // LANES):
            lt = _lane_tile(c)
            os_, ls_ = [o_in[0][:, lt].astype(F32)], [l_in[0][:, lt]]
            for w, d in enumerate(WIDE):
                for r in range(d):
                    so[w, c, pl.ds(r, tr // d, stride=d), :] = o_in[1 + w][r, :, lt].astype(F32)
                    sl[w, c, pl.ds(r, tr // d, stride=d), :] = l_in[1 + w][r, :, lt]
                os_.append(so[w, c])
                ls_.append(sl[w, c])
            la, lb, lc = ls_
            m = jnp.maximum(jnp.maximum(la, lb), lc)
            ea, eb, ec = jnp.exp(la - m), jnp.exp(lb - m), jnp.exp(lc - m)
            s = ea + eb + ec
            o = (ea * os_[0] + eb * os_[1] + ec * os_[2]) / s
            lse = m + jnp.log(s)
            o_ref[:, lt] = o.astype(BF16)
            l_ref[:, lt] = lse
            so_all[c] = o
            sl_all[c] = lse
        _emit_class_major(so_all, o_cm, tr)
        _emit_class_major(sl_all, l_cm, tr)

    specs = [_cm_tile(d, tr) for d in DILATIONS]
    wide = [_cm_tile(d, tr) for d in WIDE]
    return pl.pallas_call(
        body, name="attn_combine", grid=(T // tr,),
        in_specs=specs + specs,
        out_specs=[specs[0], specs[0]] + wide + wide,
        out_shape=[_cm_shape(1, T, BF16), _cm_shape(1, T, F32)]
        + [_cm_shape(d, T, BF16) for d in WIDE] + [_cm_shape(d, T, F32) for d in WIDE],
        scratch_shapes=[pltpu.VMEM((len(WIDE), DA // LANES, tr, LANES), F32)] * 2
        + [pltpu.VMEM((DA // LANES, tr, LANES), F32)] * 2,
        compiler_params=pltpu.CompilerParams(
            dimension_semantics=("parallel",), vmem_limit_bytes=VMEM_MID),
    )(*outs, *lses)


def _attn_bwd_q(name, zsrc, dosrc, osrc, lsrc, d, T):
    nb = T // d // BLK
    scale = DH ** -0.5

    def body(q_ref, kp_ref, kc_ref, vp_ref, vc_ref, do_ref, o_ref, l_ref, dq_ref):
        n = pl.program_id(1)
        prev_ok, cur_ok = _band_masks()
        prev_ok = prev_ok & (n > 0)
        heads = [slice(h * DH, (h + 1) * DH) for h in range(NH)]
        s = [(_dot(q_ref[:, sl], kp_ref[:, sl], NT), _dot(q_ref[:, sl], kc_ref[:, sl], NT))
             for sl in heads]
        dp = [(_dot(do_ref[:, sl], vp_ref[:, sl], NT), _dot(do_ref[:, sl], vc_ref[:, sl], NT))
              for sl in heads]
        delta = [jnp.sum(do_ref[:, sl].astype(F32) * o_ref[:, sl].astype(F32), axis=1, keepdims=True)
                 for sl in heads]
        p = [(jnp.exp(jnp.where(prev_ok, sp * scale - l_ref[:, sl], NEG)),
              jnp.exp(jnp.where(cur_ok, sc * scale - l_ref[:, sl], NEG)))
             for (sp, sc), sl in zip(s, heads)]
        ds = [((pp * (dpp - dl) * scale).astype(BF16), (pc * (dpc - dl) * scale).astype(BF16))
              for (pp, pc), (dpp, dpc), dl in zip(p, dp, delta)]
        dq = [_dot(dsp, kp_ref[:, sl], NN) + _dot(dsc, kc_ref[:, sl], NN)
              for (dsp, dsc), sl in zip(ds, heads)]
        dq_ref[...] = jnp.concatenate([v.astype(BF16) for v in dq], axis=1)

    prev = lambda n: jnp.maximum(n - 1, 0)
    own = _cm_spec(d, 0)
    return pl.pallas_call(
        body, name=name, grid=(d, nb),
        in_specs=[own, _cm_spec(d, 1, prev), _cm_spec(d, 1), _cm_spec(d, 2, prev), _cm_spec(d, 2),
                  own, own, own],
        out_specs=own, out_shape=_cm_shape(d, T, BF16),
        compiler_params=pltpu.CompilerParams(dimension_semantics=("parallel", "parallel")),
    )(zsrc, zsrc, zsrc, zsrc, zsrc, dosrc, osrc, lsrc)


def _attn_bwd_kv(name, zsrc, dosrc, osrc, lsrc, d, T):
    nb = T // d // BLK
    scale = DH ** -0.5

    def body(k_ref, v_ref, qa_ref, qb_ref, doa_ref, dob_ref, oa_ref, ob_ref, la_ref, lb_ref,
             dk_ref, dv_ref):
        j = pl.program_id(1)
        next_ok, own_ok = _band_masks()
        next_ok = next_ok & (j < nb - 1)
        heads = [slice(h * DH, (h + 1) * DH) for h in range(NH)]
        s = [(_dot(qa_ref[:, sl], k_ref[:, sl], NT), _dot(qb_ref[:, sl], k_ref[:, sl], NT))
             for sl in heads]
        dp = [(_dot(doa_ref[:, sl], v_ref[:, sl], NT), _dot(dob_ref[:, sl], v_ref[:, sl], NT))
              for sl in heads]
        delta = [(jnp.sum(doa_ref[:, sl].astype(F32) * oa_ref[:, sl].astype(F32), axis=1, keepdims=True),
                  jnp.sum(dob_ref[:, sl].astype(F32) * ob_ref[:, sl].astype(F32), axis=1, keepdims=True))
                 for sl in heads]
        p = [(jnp.exp(jnp.where(own_ok, sa * scale - la_ref[:, sl], NEG)),
              jnp.exp(jnp.where(next_ok, sb * scale - lb_ref[:, sl], NEG)))
             for (sa, sb), sl in zip(s, heads)]
        dv = [_dot(pa.astype(BF16), doa_ref[:, sl], TN) + _dot(pb.astype(BF16), dob_ref[:, sl], TN)
              for (pa, pb), sl in zip(p, heads)]
        ds = [((pa * (dpa - da) * scale).astype(BF16), (pb * (dpb - db) * scale).astype(BF16))
              for (pa, pb), (dpa, dpb), (da, db) in zip(p, dp, delta)]
        dk = [_dot(dsa, qa_ref[:, sl], TN) + _dot(dsb, qb_ref[:, sl], TN)
              for (dsa, dsb), sl in zip(ds, heads)]
        dk_ref[...] = jnp.concatenate([v.astype(BF16) for v in dk], axis=1)
        dv_ref[...] = jnp.concatenate([v.astype(BF16) for v in dv], axis=1)

    nxt = lambda j: jnp.minimum(j + 1, nb - 1)
    own, own_n = _cm_spec(d, 0), _cm_spec(d, 0, nxt)
    sh = _cm_shape(d, T, BF16)
    return pl.pallas_call(
        body, name=name, grid=(d, nb),
        in_specs=[_cm_spec(d, 1), _cm_spec(d, 2), own, own_n, own, own_n, own, own_n, own, own_n],
        out_specs=[own, own], out_shape=[sh, sh],
        compiler_params=pltpu.CompilerParams(dimension_semantics=("parallel", "parallel")),
    )(zsrc, zsrc, zsrc, zsrc, dosrc, dosrc, osrc, osrc, lsrc, lsrc)


def _dz_assemble(dqs, dks, dvs, dcvg, T):
    tr = _rows(T)
    nb = len(DILATIONS)

    def body(*refs):
        cvg_ref, dz_ref, scr = refs[3 * nb], refs[3 * nb + 1], refs[3 * nb + 2]
        for g in range(3):
            parts = refs[g * nb:(g + 1) * nb]
            for c in range(DA // LANES):
                lt = _lane_tile(c)
                scr[g, c] = parts[0][:, lt].astype(F32)
                for w, d in enumerate(WIDE):
                    for r in range(d):
                        rows = pl.ds(r, tr // d, stride=d)
                        scr[g, c, rows, :] = scr[g, c, rows, :] + parts[1 + w][r, :, lt].astype(F32)
                dz_ref[:, g * DA + c * LANES:g * DA + (c + 1) * LANES] = scr[g, c].astype(BF16)
        dz_ref[:, 3 * DA:] = cvg_ref[...]

    specs = [_cm_tile(d, tr) for d in DILATIONS]
    return pl.pallas_call(
        body, name="dz_assemble", grid=(T // tr,),
        in_specs=specs * 3 + [pl.BlockSpec((tr, 2 * DC), lambda i: (i, 0))],
        out_specs=pl.BlockSpec((tr, DIN), lambda i: (i, 0)),
        out_shape=jax.ShapeDtypeStruct((T, DIN), BF16),
        scratch_shapes=[pltpu.VMEM((3, DA // LANES, tr, LANES), F32)],
        compiler_params=pltpu.CompilerParams(
            dimension_semantics=("parallel",), vmem_limit_bytes=VMEM_MID),
    )(*dqs, *dks, *dvs, dcvg)


CT = 256
HALO = 32
RC = 32


def _conv_fwd(z, w_dw, b_dw, g_ln, b_ln):
    T = z.shape[0]
    ct = min(CT, T)
    nt = T // ct
    hb = ct // HALO

    def body(cv_ref, cg_ref, cvp_ref, cgp_ref, w_ref, bdw_ref, g_ref, b_ref, oc_ref, y_ref, ubuf, ush):
        i = pl.program_id(0)
        up = cvp_ref[...].astype(F32) * _sigmoid(cgp_ref[...].astype(F32))
        ubuf[0:HALO, :] = jnp.where(i > 0, up, 0.0)
        ubuf[HALO:, :] = cv_ref[...].astype(F32) * _sigmoid(cg_ref[...].astype(F32))
        for b in range(8):
            ush[b] = ubuf[pl.ds(8 - b, ct + 24), :]

        def chunk(ci, carry):
            r0 = pl.multiple_of(ci * RC, RC)
            acc = jnp.broadcast_to(bdw_ref[...], (RC, DC))
            for s in range(CW):
                a, b = divmod(s, 8)
                acc = acc + w_ref[CW - 1 - s:CW - s, :] * ush[b, pl.ds(r0 + 24 - 8 * a, RC), :]
            y_ref[pl.ds(r0, RC), :] = acc
            mu = jnp.mean(acc, axis=-1, keepdims=True)
            cen = acc - mu
            var = jnp.mean(cen * cen, axis=-1, keepdims=True)
            ln = cen * lax.rsqrt(var + EPS) * g_ref[...] + b_ref[...]
            oc_ref[pl.ds(r0, RC), :] = (ln * _sigmoid(ln)).astype(BF16)
            return carry

        lax.fori_loop(0, ct // RC, chunk, 0)

    cur = lambda col: pl.BlockSpec((ct, DC), lambda i: (i, col))
    prv = lambda col: pl.BlockSpec((HALO, DC), lambda i: (jnp.maximum(i * hb - 1, 0), col))
    vec = pl.BlockSpec((1, DC), lambda i: (0, 0))
    return pl.pallas_call(
        body, name="conv_fwd", grid=(nt,),
        in_specs=[cur(3), cur(4), prv(3), prv(4), pl.BlockSpec((CW, DC), lambda i: (0, 0)),
                  vec, vec, vec],
        out_specs=[pl.BlockSpec((ct, DC), lambda i: (i, 0))] * 2,
        out_shape=[jax.ShapeDtypeStruct((T, DC), BF16), jax.ShapeDtypeStruct((T, DC), F32)],
        scratch_shapes=[pltpu.VMEM((ct + HALO, DC), F32), pltpu.VMEM((8, ct + 24, DC), F32)],
        compiler_params=pltpu.CompilerParams(
            dimension_semantics=("parallel",), vmem_limit_bytes=VMEM_MID),
    )(z, z, z, z, w_dw, b_dw, g_ln, b_ln)


def _conv_bwd(z, dom, y, w_dw, g_ln, b_ln):
    T = z.shape[0]
    ct = min(CT, T)
    nt = T // ct
    hb = ct // HALO
    last_halo = T // HALO - 1

    def ln_bwd(yv, dov, g_ref, b_ref):
        mu = jnp.mean(yv, axis=-1, keepdims=True)
        cen = yv - mu
        rstd = lax.rsqrt(jnp.mean(cen * cen, axis=-1, keepdims=True) + EPS)
        xhat = cen * rstd
        ln = xhat * g_ref[...] + b_ref[...]
        sg = _sigmoid(ln)
        dln = dov * (sg * (1.0 + ln * (1.0 - sg)))
        dxh = dln * g_ref[...]
        dy = rstd * (dxh - jnp.mean(dxh, axis=-1, keepdims=True)
                     - xhat * jnp.mean(dxh * xhat, axis=-1, keepdims=True))
        return dy, dln, xhat

    def body(do_ref, don_ref, y_ref, yn_ref, cv_ref, cg_ref, cvp_ref, cgp_ref, w_ref, g_ref, b_ref,
             dcvg_ref, dw_ref, dbdw_ref, dg_ref, db_ref,
             dybuf, dysh, ubuf, ush, dwacc, vacc):
        i = pl.program_id(0)

        @pl.when(i == 0)
        def _():
            dwacc[...] = jnp.zeros_like(dwacc)
            vacc[...] = jnp.zeros_like(vacc)

        def ln_chunk(ci, carry):
            r0 = pl.multiple_of(ci * RC, RC)
            dy, dln, xhat = ln_bwd(y_ref[pl.ds(r0, RC), :], do_ref[pl.ds(r0, RC), :].astype(F32),
                                   g_ref, b_ref)
            dybuf[pl.ds(r0, RC), :] = dy
            vacc[0] += _fold8(dy)
            vacc[1] += _fold8(dln * xhat)
            vacc[2] += _fold8(dln)
            return carry

        lax.fori_loop(0, ct // RC, ln_chunk, 0)
        dyn, _, _ = ln_bwd(yn_ref[...], don_ref[...].astype(F32), g_ref, b_ref)
        dybuf[ct:, :] = jnp.where(i < nt - 1, dyn, 0.0)
        for b in range(8):
            dysh[b] = dybuf[pl.ds(b, ct + 24), :]

        up = cvp_ref[...].astype(F32) * _sigmoid(cgp_ref[...].astype(F32))
        ubuf[0:HALO, :] = jnp.where(i > 0, up, 0.0)
        ubuf[HALO:, :] = cv_ref[...].astype(F32) * _sigmoid(cg_ref[...].astype(F32))
        for b in range(8):
            ush[b] = ubuf[pl.ds(8 - b, ct + 24), :]

        def chunk(ci, carry):
            r0 = pl.multiple_of(ci * RC, RC)
            dy = dybuf[pl.ds(r0, RC), :]
            du = jnp.zeros((RC, DC), F32)
            for s in range(CW):
                a, b = divmod(s, 8)
                du = du + w_ref[CW - 1 - s:CW - s, :] * dysh[b, pl.ds(r0 + 8 * a, RC), :]
                dwacc[CW - 1 - s] += _fold8(dy * ush[b, pl.ds(r0 + 24 - 8 * a, RC), :])
            cv = cv_ref[pl.ds(r0, RC), :].astype(F32)
            sg = _sigmoid(cg_ref[pl.ds(r0, RC), :].astype(F32))
            dcvg_ref[pl.ds(r0, RC), 0:DC] = (du * sg).astype(BF16)
            dcvg_ref[pl.ds(r0, RC), DC:2 * DC] = (du * cv * sg * (1.0 - sg)).astype(BF16)
            return carry

        lax.fori_loop(0, ct // RC, chunk, 0)

        @pl.when(i == nt - 1)
        def _():
            dw_ref[...] = jnp.sum(dwacc[...], axis=1)
            dbdw_ref[...] = jnp.sum(vacc[0], axis=0, keepdims=True)
            dg_ref[...] = jnp.sum(vacc[1], axis=0, keepdims=True)
            db_ref[...] = jnp.sum(vacc[2], axis=0, keepdims=True)

    cur = lambda col: pl.BlockSpec((ct, DC), lambda i: (i, col))
    prv = lambda col: pl.BlockSpec((HALO, DC), lambda i: (jnp.maximum(i * hb - 1, 0), col))
    nxt = lambda col: pl.BlockSpec((HALO, DC), lambda i: (jnp.minimum((i + 1) * hb, last_halo), col))
    vec = pl.BlockSpec((1, DC), lambda i: (0, 0))
    tile = pl.BlockSpec((ct, DC), lambda i: (i, 0))
    return pl.pallas_call(
        body, name="conv_bwd", grid=(nt,),
        in_specs=[cur(1), nxt(1), cur(0), nxt(0), cur(3), cur(4), prv(3), prv(4),
                  pl.BlockSpec((CW, DC), lambda i: (0, 0)), vec, vec],
        out_specs=[pl.BlockSpec((ct, 2 * DC), lambda i: (i, 0)),
                   pl.BlockSpec((CW, DC), lambda i: (0, 0)), vec, vec, vec],
        out_shape=[jax.ShapeDtypeStruct((T, 2 * DC), BF16),
                   jax.ShapeDtypeStruct((CW, DC), F32), jax.ShapeDtypeStruct((1, DC), F32),
                   jax.ShapeDtypeStruct((1, DC), F32), jax.ShapeDtypeStruct((1, DC), F32)],
        scratch_shapes=[pltpu.VMEM((ct + HALO, DC), F32), pltpu.VMEM((8, ct + 24, DC), F32),
                        pltpu.VMEM((ct + HALO, DC), F32), pltpu.VMEM((8, ct + 24, DC), F32),
                        pltpu.VMEM((CW, 8, DC), F32), pltpu.VMEM((3, 8, DC), F32)],
        compiler_params=pltpu.CompilerParams(
            dimension_semantics=("arbitrary",), vmem_limit_bytes=VMEM_BIG),
    )(dom, dom, y, y, z, z, z, z, w_dw, g_ln, b_ln)


def _adam_math(w, g, m, v):
    m = ADAM_B1 * m + (1.0 - ADAM_B1) * g
    v = ADAM_B2 * v + (1.0 - ADAM_B2) * (g * g)
    m_hat = m / (1.0 - ADAM_B1 ** ADAM_STEP)
    v_hat = v / (1.0 - ADAM_B2 ** ADAM_STEP)
    delta = -ADAM_LR * (m_hat / (jnp.sqrt(v_hat) + ADAM_EPS) + ADAM_WD * w)
    return delta, m, v


def _adam(name, slots, w, m, v):
    rows, cols = w.shape
    tr = next(t for t in (256, 176, 128, 64, 32, 16, 8, rows) if rows % t == 0)

    def body(s_ref, w_ref, m_ref, v_ref, g_out, d_out, m_out, v_out):
        g = s_ref[0].astype(F32)
        for s in range(1, NDEV):
            g = g + s_ref[s].astype(F32)
        delta, mn, vn = _adam_math(w_ref[...], g, m_ref[...], v_ref[...])
        g_out[...] = g
        d_out[...] = delta
        m_out[...] = mn
        v_out[...] = vn

    tile = pl.BlockSpec((tr, cols), lambda i: (i, 0))
    sh = jax.ShapeDtypeStruct((rows, cols), F32)
    return pl.pallas_call(
        body, name=name, grid=(rows // tr,),
        in_specs=[pl.BlockSpec((NDEV, tr, cols), lambda i: (0, i, 0)), tile, tile, tile],
        out_specs=[tile] * 4, out_shape=[sh] * 4,
        compiler_params=pltpu.CompilerParams(
            dimension_semantics=("parallel",), vmem_limit_bytes=VMEM_MID),
    )(slots, w, m, v)


SMALL_NAMES = ("g_mix", "b_dw", "g_conv_ln", "b_conv_ln", "g_ffn", "g_ple", "b_pgate", "g_final")


def _pack_small(vecs, w_dw_full):
    rows = [jnp.pad(v.reshape(1, -1), ((0, 0), (0, SMALL_W - v.size))) for v in vecs]
    rows.append(jnp.pad(w_dw_full, ((0, 0), (0, SMALL_W - DC))))
    rows.append(jnp.zeros((SMALL_ROWS - len(vecs) - CW, SMALL_W), F32))
    return jnp.concatenate(rows, axis=0)


def kernel(x, p, g_mix, w_in, w_dw, b_dw, g_conv_ln, b_conv_ln, w_out, g_ffn, w_gate, w_up, w_down, g_ple, w_pgate, b_pgate, w_ple, g_final, loss_target, m_g_mix, m_w_in, m_w_dw, m_b_dw, m_g_conv_ln, m_b_conv_ln, m_w_out, m_g_ffn, m_w_gate, m_w_up, m_w_down, m_g_ple, m_w_pgate, m_b_pgate, m_w_ple, m_g_final, v_g_mix, v_w_in, v_w_dw, v_b_dw, v_g_conv_ln, v_b_conv_ln, v_w_out, v_g_ffn, v_w_gate, v_w_up, v_w_down, v_g_ple, v_w_pgate, v_b_pgate, v_w_ple, v_g_final):
    T = x.shape[1]
    me = 4 * lax.axis_index("x") + 2 * lax.axis_index("y") + lax.axis_index("c")
    xs = x.reshape(T, D)
    ps = p.reshape(T, DPLE).astype(BF16)
    tgt = loss_target.reshape(T, D)
    g_final2 = g_final.reshape(1, D)

    big = dict(w_in=w_in[0], w_out=w_out[0], w_gate=w_gate[0], w_up=w_up[0], w_down=w_down[0],
               w_pgate=w_pgate[0], w_ple=w_ple[0])
    order = ("w_in", "w_out", "w_gate", "w_up", "w_down", "w_pgate", "w_ple")
    g_order = ("w_dw",) + order
    g_items = [(w_dw.reshape(CW, DC // NDEV), False)] + [(big[n].astype(BF16), False) for n in order]
    g_handles, g_token = _xstart("gather_start", g_items, _place("gather_place", g_items))
    G = dict(zip(g_order, g_handles))

    a = _rms_fwd("rms_mix", xs, g_mix, deps=[g_token])
    w_dw_f = _xwait("gather_wait_w_dw", G["w_dw"], a).transpose(1, 0, 2).reshape(CW, DC)
    w_in_f = _xwait("gather_wait_w_in", G["w_in"], a)
    z, *z_wide = _mm_in(a, w_in_f)
    zsrc = dict(zip(DILATIONS, [z] + z_wide))
    br = [_attn_fwd(f"attn_fwd_d{d}", zsrc[d], d, T) for d in DILATIONS]
    comb = list(_attn_combine([b[0] for b in br], [b[1] for b in br], T))
    o_attn, lse = comb[0], comb[1]
    osrc = dict(zip(DILATIONS, [o_attn] + comb[2:2 + len(WIDE)]))
    lsrc = dict(zip(DILATIONS, [lse] + comb[2 + len(WIDE):]))
    o_conv, y_conv = _conv_fwd(z, w_dw_f, b_dw, g_conv_ln, b_conv_ln)
    w_out_f = _xwait("gather_wait_w_out", G["w_out"], o_conv).reshape(D, D)
    h1 = _mm_out(o_attn, o_conv, w_out_f, xs)
    f = _rms_fwd("rms_ffn", h1, g_ffn)
    w_gate_f = _xwait("gather_wait_w_gate", G["w_gate"], f)
    w_up_f = _xwait("gather_wait_w_up", G["w_up"], f)
    gate, up, act = _mm_gate_up(f, w_gate_f, w_up_f)
    w_down_f = _xwait("gather_wait_w_down", G["w_down"], act)
    h2 = _mm_down(act, w_down_f, h1)
    r = _rms_fwd("rms_ple", h2, g_ple)
    w_pgate_f = _xwait("gather_wait_w_pgate", G["w_pgate"], r).reshape(D, D)
    w_ple_f = _xwait("gather_wait_w_ple", G["w_ple"], r).transpose(1, 0, 2).reshape(DPLE, D)
    gte, pe, h3 = _mm_ple(r, w_pgate_f, b_pgate, ps, w_ple_f, h2)

    loss_part, dh3, dpe, dpg, d_g_final, d_b_pgate = _loss_bwd(h3, tgt, g_final2, pe, gte)
    H = {}

    def send_grads(tag, named):
        items = [(v, True) for _, v in named]
        handles, token = _xstart(f"grads_start_{tag}", items, _place(f"grads_place_{tag}", items))
        H.update(zip([n for n, _ in named], handles))
        return token

    gw_pgate = _mm_tn("gw_pgate", r, dpg).reshape(NDEV, D // NDEV, D)
    gw_ple = _mm_tn("gw_ple", ps, dpe).reshape(DPLE, NDEV, D // NDEV).transpose(1, 0, 2)
    tok = send_grads("ple", [("w_pgate", gw_pgate), ("w_ple", gw_ple)])
    dr = _mm_nt("mm_pgate_bwd", dpg, w_pgate_f, deps=[tok])
    dh2, dh2b, d_g_ple = _rms_bwd("rms_ple_bwd", dr, h2, g_ple, dh3, True)
    gw_down = _mm_tn_ff_rows("gw_down", act, dh2b)
    tok = send_grads("down", [("w_down", gw_down)])
    dgate, dup = _mm_down_bwd(dh2b, w_down_f, gate, up, deps=[tok])
    gw_gate = _mm_tn_ff_cols("gw_gate", f, dgate)
    gw_up = _mm_tn_ff_cols("gw_up", f, dup)
    tok = send_grads("ffn", [("w_gate", gw_gate), ("w_up", gw_up)])
    df = _mm_ffn_in_bwd(dgate, w_gate_f, dup, w_up_f, deps=[tok])
    dh1, dh1b, d_g_ffn = _rms_bwd("rms_ffn_bwd", df, h1, g_ffn, dh2, True)
    gw_out = jnp.concatenate(
        [_mm_tn("gw_out_attn", o_attn, dh1b), _mm_tn("gw_out_conv", o_conv, dh1b)], axis=0)
    tok = send_grads("out", [("w_out", gw_out.reshape(NDEV, D // NDEV, D))])
    dom, *do_wide = _mm_out_bwd(dh1b, w_out_f, deps=[tok])
    dosrc = dict(zip(DILATIONS, [dom] + do_wide))
    dcvg, d_w_dw, d_b_dw, d_g_ln, d_b_ln = _conv_bwd(z, dom, y_conv, w_dw_f, g_conv_ln, b_conv_ln)
    dqs, dks, dvs = [], [], []
    for d in DILATIONS:
        dqs.append(_attn_bwd_q(f"attn_bwd_q_d{d}", zsrc[d], dosrc[d], osrc[d], lsrc[d], d, T))
        dk, dv = _attn_bwd_kv(f"attn_bwd_kv_d{d}", zsrc[d], dosrc[d], osrc[d], lsrc[d], d, T)
        dks.append(dk)
        dvs.append(dv)
    dz = _dz_assemble(dqs, dks, dvs, dcvg, T)
    gw_in = _mm_tn_cols("gw_in", a, dz, N_IN)
    tok = send_grads("in", [("w_in", gw_in)])
    da = _mm_in_bwd(dz, w_in_f, deps=[tok])
    grad_x, d_g_mix = _rms_bwd("rms_mix_bwd", da, xs, g_mix, dh1, False)

    small_part = _pack_small(
        [d_g_mix, d_b_dw, d_g_ln, d_b_ln, d_g_ffn, d_g_ple, d_b_pgate, d_g_final], d_w_dw)
    small_slots = _exchange("exchange_small_grads", [(small_part, False)])[0]
    S = {n: _xwait(f"grads_wait_{n}", H[n], small_slots)
         for n in ("w_pgate", "w_ple", "w_down", "w_gate", "w_up", "w_out", "w_in")}

    mom = dict(w_in=(m_w_in, v_w_in), w_out=(m_w_out, v_w_out), w_gate=(m_w_gate, v_w_gate),
               w_up=(m_w_up, v_w_up), w_down=(m_w_down, v_w_down), w_pgate=(m_w_pgate, v_w_pgate),
               w_ple=(m_w_ple, v_w_ple))
    upd = {}
    for n in order:
        res = _adam(f"adam_{n}", S[n], big[n], mom[n][0][0], mom[n][1][0])
        upd[n] = [t[None] for t in res]

    def lanes(v):
        full = jnp.zeros((CW, NDEV, DC // NDEV), F32)
        full = lax.dynamic_update_slice(full, v.reshape(CW, 1, DC // NDEV), (0, me, 0))
        return full.reshape(CW, DC)

    small_w = _pack_small([g_mix, b_dw, g_conv_ln, b_conv_ln, g_ffn, g_ple, b_pgate, g_final2], lanes(w_dw))
    small_m = _pack_small([m_g_mix, m_b_dw, m_g_conv_ln, m_b_conv_ln, m_g_ffn, m_g_ple, m_b_pgate,
                           m_g_final.reshape(1, D)], lanes(m_w_dw))
    small_v = _pack_small([v_g_mix, v_b_dw, v_g_conv_ln, v_b_conv_ln, v_g_ffn, v_g_ple, v_b_pgate,
                           v_g_final.reshape(1, D)], lanes(v_w_dw))
    small_res = _adam("adam_small", small_slots, small_w, small_m, small_v)

    def unpack(t):
        out = {}
        widths = dict(g_mix=D, b_dw=DC, g_conv_ln=DC, b_conv_ln=DC, g_ffn=D, g_ple=D, b_pgate=D, g_final=D)
        for i, n in enumerate(SMALL_NAMES):
            out[n] = t[i:i + 1, :widths[n]]
        out["g_final"] = out["g_final"].reshape(D)
        taps = t[len(SMALL_NAMES):len(SMALL_NAMES) + CW, :DC].reshape(CW, NDEV, DC // NDEV)
        out["w_dw"] = lax.dynamic_slice(taps, (0, me, 0), (CW, 1, DC // NDEV))[None]
        return out

    small = [unpack(t) for t in small_res]

    loss = lax.psum(loss_part[0, 0], ("x", "y", "c"))
    names = ("g_mix", "w_in", "w_dw", "b_dw", "g_conv_ln", "b_conv_ln", "w_out", "g_ffn", "w_gate",
             "w_up", "w_down", "g_ple", "w_pgate", "b_pgate", "w_ple", "g_final")
    outs = [loss, grad_x.reshape(1, T, D)]
    for kind in range(4):
        for n in names:
            outs.append(upd[n][kind] if n in upd else small[kind][n])
    return tuple(outs)
```

```python
import jax
import jax.numpy as jnp
from jax import lax
from jax.experimental import pallas as pl
from jax.experimental.pallas import tpu as pltpu

F32 = jnp.float32
BF16 = jnp.bfloat16

NDEV = 8
D = 2048
NH = 8
DH = 128
DA = NH * DH
DC = D - DA
DIN = 3 * DA + 2 * DC
DFF = 5632
DPLE = 256
BLK = 128
DILATIONS = (1, 4, 16)
CW = 31
EPS = 1e-6
N_IN = DIN // NDEV
N_FF = DFF // NDEV
NEG = -1e30

ADAM_LR = 0.001
ADAM_B1 = 0.9
ADAM_B2 = 0.999
ADAM_EPS = 1e-08
ADAM_WD = 0.01
ADAM_STEP = 10

VMEM_CAP_V7X = 64 * 1024 * 1024
VMEM_BIG = VMEM_CAP_V7X - 12 * 1024 * 1024
VMEM_MID = 40 * 1024 * 1024

SMALL_W = 2048
SMALL_ROWS = 40


def _sigmoid(v):
    return 1.0 / (1.0 + jnp.exp(-v))


def _dot(a, b, contract):
    return lax.dot_general(a, b, (contract, ((), ())), preferred_element_type=F32)


NN = ((1,), (0,))
NT = ((1,), (1,))
TN = ((0,), (0,))


def _exchange(name, items):
    n = len(items)
    out_shape = [
        jax.ShapeDtypeStruct((NDEV,) + (a.shape[1:] if sc else a.shape), a.dtype)
        for a, sc in items
    ]
    scat = [sc for _, sc in items]

    def body(*refs):
        srcs = refs[:n]
        dsts = refs[n:2 * n]
        send_sems, recv_sems, loc_sems = refs[2 * n:]
        x = lax.axis_index("x")
        y = lax.axis_index("y")
        c = lax.axis_index("c")
        me = 4 * x + 2 * y + c

        local = []
        for i in range(n):
            src = srcs[i].at[me] if scat[i] else srcs[i]
            cp = pltpu.make_async_copy(src, dsts[i].at[me], loc_sems.at[i])
            cp.start()
            local.append(cp)

        remote = []
        for k in range(1, NDEV):
            px = (1 - x) if (k >> 2) & 1 else x
            py = (1 - y) if (k >> 1) & 1 else y
            pc = (1 - c) if k & 1 else c
            peer = 4 * px + 2 * py + pc
            for i in range(n):
                sem = i * (NDEV - 1) + k - 1
                src = srcs[i].at[peer] if scat[i] else srcs[i]
                send = pltpu.make_async_remote_copy(
                    src_ref=src, dst_ref=dsts[i].at[me],
                    send_sem=send_sems.at[sem], recv_sem=recv_sems.at[sem],
                    device_id=(px, py, pc), device_id_type=pl.DeviceIdType.MESH)
                send.start()
                recv = pltpu.make_async_remote_copy(
                    src_ref=src, dst_ref=dsts[i].at[peer],
                    send_sem=send_sems.at[sem], recv_sem=recv_sems.at[sem],
                    device_id=(px, py, pc), device_id_type=pl.DeviceIdType.MESH)
                remote.append((send, recv))
        for send, recv in remote:
            recv.wait_recv()
            send.wait_send()
        for cp in local:
            cp.wait()

    any_spec = pl.BlockSpec(memory_space=pl.ANY)
    return pl.pallas_call(
        body, name=name,
        in_specs=[any_spec] * n, out_specs=[any_spec] * n, out_shape=out_shape,
        scratch_shapes=[
            pltpu.SemaphoreType.DMA((n * (NDEV - 1),)),
            pltpu.SemaphoreType.DMA((n * (NDEV - 1),)),
            pltpu.SemaphoreType.DMA((n,)),
        ],
    )(*[a for a, _ in items])


HBM_SPEC = pl.BlockSpec(memory_space=pltpu.HBM)
SEM_SPEC = pl.BlockSpec(memory_space=pltpu.SEMAPHORE)
ANY_SPEC = pl.BlockSpec(memory_space=pl.ANY)
EFFECT = pltpu.SideEffectType.DATAFLOW_SIDE_EFFECTING


def _peer_of(k):
    x = lax.axis_index("x")
    y = lax.axis_index("y")
    c = lax.axis_index("c")
    px = (1 - x) if (k >> 2) & 1 else x
    py = (1 - y) if (k >> 1) & 1 else y
    pc = (1 - c) if k & 1 else c
    return (px, py, pc), 4 * px + 2 * py + pc


def _my_index():
    return 4 * lax.axis_index("x") + 2 * lax.axis_index("y") + lax.axis_index("c")


def _slot_shape(a, sc):
    return (NDEV,) + (a.shape[1:] if sc else a.shape)


def _divisor_tile(rows):
    return next((t for t in (512, 256, 176, 128, 64, 32, 16) if rows % t == 0), rows)


def _place(name, items):
    lands = []
    for idx, (a, sc) in enumerate(items):
        rows, cols = a.shape[-2:]
        tr = _divisor_tile(rows)

        def body(s_ref, o_ref):
            o_ref[...] = s_ref[...]

        mine = pl.BlockSpec((None, tr, cols), lambda i: (_my_index(), i, 0))
        lands.append(pl.pallas_call(
            body, name=f"{name}_{idx}", grid=(rows // tr,),
            in_specs=[mine if sc else pl.BlockSpec((tr, cols), lambda i: (i, 0))],
            out_specs=mine,
            out_shape=jax.ShapeDtypeStruct(_slot_shape(a, sc), a.dtype),
            compiler_params=pltpu.CompilerParams(dimension_semantics=("parallel",)),
        )(a))
    return lands


def _xstart(name, items, lands):
    n = len(items)
    scat = [sc for _, sc in items]

    def body(*refs):
        srcs = refs[:n]
        lzs = refs[n:2 * n]
        send_sems = refs[2 * n:3 * n]
        recv_sems = refs[3 * n:4 * n]
        token = refs[-1]
        me = _my_index()
        for i in range(n):
            for k in range(1, NDEV):
                peer_id, peer = _peer_of(k)
                src = srcs[i].at[peer] if scat[i] else srcs[i]
                pltpu.make_async_remote_copy(
                    src_ref=src, dst_ref=lzs[i].at[me],
                    send_sem=send_sems[i].at[k - 1], recv_sem=recv_sems[i].at[k - 1],
                    device_id=peer_id, device_id_type=pl.DeviceIdType.MESH).start()
        token[...] = jnp.zeros_like(token)

    sem = pltpu.SemaphoreType.DMA((NDEV - 1,))
    hbm = [pltpu.HBM(a.shape, a.dtype) for a, _ in items] + [pltpu.HBM(l.shape, l.dtype) for l in lands]
    res = pl.pallas_call(
        body, name=name,
        in_specs=[HBM_SPEC] * (2 * n),
        out_specs=[SEM_SPEC] * (2 * n) + [HBM_SPEC] * (2 * n) + [pl.BlockSpec(memory_space=pltpu.VMEM)],
        out_shape=[sem] * (2 * n) + hbm + [jax.ShapeDtypeStruct((8, 128), F32)],
        input_output_aliases={i: 2 * n + i for i in range(2 * n)},
        compiler_params=pltpu.CompilerParams(has_side_effects=EFFECT),
    )(*[pltpu.with_memory_space_constraint(a, pltpu.HBM) for a, _ in items],
      *[pltpu.with_memory_space_constraint(l, pltpu.HBM) for l in lands])
    handles = [(res[i], res[n + i], res[2 * n + i], res[3 * n + i], scat[i]) for i in range(n)]
    return handles, res[-1]


def _xwait(name, handle, after):
    send_sem, recv_sem, src, land, sc = handle

    def body(src_ref, land_ref, send_ref, recv_ref, after_ref, src_dead, got_ref):
        for k in range(1, NDEV):
            peer_id, peer = _peer_of(k)
            cp = pltpu.make_async_remote_copy(
                src_ref=src_ref.at[peer] if sc else src_ref, dst_ref=land_ref.at[peer],
                send_sem=send_ref.at[k - 1], recv_sem=recv_ref.at[k - 1],
                device_id=peer_id, device_id_type=pl.DeviceIdType.MESH)
            cp.wait_send()
            cp.wait_recv()

    return pl.pallas_call(
        body, name=name,
        in_specs=[HBM_SPEC, HBM_SPEC, SEM_SPEC, SEM_SPEC, ANY_SPEC],
        out_specs=[HBM_SPEC, HBM_SPEC],
        out_shape=[pltpu.HBM(src.shape, src.dtype), pltpu.HBM(land.shape, land.dtype)],
        input_output_aliases={0: 0, 1: 1},
        compiler_params=pltpu.CompilerParams(has_side_effects=EFFECT),
    )(src, land, send_sem, recv_sem, after)[1]


def _mm(name, grid, in_specs, operands, out_specs, out_shape, contract, n_pairs, epilogue,
        acc_shape=None, vmem=VMEM_BIG, deps=(), group=1, a_cols=None):
    nk = grid[2]

    def shard(ref, s, is_a):
        if group == 1:
            return ref[...]
        if is_a and a_cols is not None:
            return ref[:, s * a_cols:(s + 1) * a_cols]
        return ref[s]
    n_extra = len(operands) - 2 * n_pairs
    n_out = len(out_shape)
    n_in = len(operands) + len(deps)
    in_specs = list(in_specs) + [ANY_SPEC] * len(deps)
    operands = list(operands) + list(deps)

    def body(*refs):
        ab = refs[:2 * n_pairs]
        extras = refs[2 * n_pairs:2 * n_pairs + n_extra]
        outs = refs[n_in:n_in + n_out]
        dots = [(ab[2 * p], ab[2 * p + 1], s) for p in range(n_pairs) for s in range(group)]
        if nk == 1:
            part = None
            for a_ref, b_ref, s in dots:
                d = _dot(shard(a_ref, s, True), shard(b_ref, s, False), contract)
                part = d if part is None else part + d
            epilogue(part, extras, outs)
        else:
            acc_ref = refs[-1]
            k = pl.program_id(2)

            @pl.when(k == 0)
            def _():
                acc_ref[...] = jnp.zeros_like(acc_ref)

            for a_ref, b_ref, s in dots:
                acc_ref[...] += _dot(shard(a_ref, s, True), shard(b_ref, s, False), contract)

            @pl.when(k == nk - 1)
            def _():
                epilogue(acc_ref[...], extras, outs)

    scratch = [pltpu.VMEM(acc_shape, F32)] if nk > 1 else []
    return pl.pallas_call(
        body, name=name, grid=grid, in_specs=in_specs, out_specs=out_specs, out_shape=out_shape,
        scratch_shapes=scratch,
        compiler_params=pltpu.CompilerParams(
            dimension_semantics=("parallel", "parallel", "arbitrary"), vmem_limit_bytes=vmem),
    )(*operands)


def _ep_cast(dtype):
    def ep(acc, extras, outs):
        outs[0][...] = acc.astype(dtype)
    return ep


def _ep_resid(acc, extras, outs):
    outs[0][...] = extras[0][...] + acc


def _ep_swiglu_bwd(acc, extras, outs):
    g = extras[0][...].astype(F32)
    u = extras[1][...].astype(F32)
    sg = _sigmoid(g)
    outs[0][...] = (acc * u * (sg * (1.0 + g * (1.0 - sg)))).astype(BF16)
    outs[1][...] = (acc * (g * sg)).astype(BF16)


MXU_COLS_V7X = 256


def _col_chunks(n):
    return [slice(c, min(c + MXU_COLS_V7X, n)) for c in range(0, n, MXU_COLS_V7X)]


def _row_tile(T):
    return min(1024, T)


def _tn_rows(T):
    return min(2048, T)


WIDE = tuple(d for d in DILATIONS if d > 1)


LANES = 128


def _lane_tile(c):
    return slice(c * LANES, (c + 1) * LANES)


def _to_lane_tiles(scr, val):
    for c in range(scr.shape[0]):
        scr[c] = val[:, _lane_tile(c)]


def _emit_class_major(scr, refs, rows):
    for d, ref in zip(WIDE, refs):
        for r in range(d):
            for c in range(scr.shape[0]):
                ref[r, :, _lane_tile(c)] = scr[c, pl.ds(r, rows // d, stride=d), :].astype(ref.dtype)


def _mm_in(a, w_in):
    T = a.shape[0]
    tm = _row_tile(T)
    nq = -(-3 * DA // N_IN)

    def body(a_ref, w_ref, z_ref, *rest):
        scr = rest[-1]
        j = pl.program_id(1)

        @pl.when(j >= nq)
        def _():
            z_ref[...] = _dot(a_ref[...], w_ref[...], NN).astype(BF16)

        @pl.when(j < nq)
        def _():
            av = a_ref[...]
            chunks = _col_chunks(N_IN)
            pending = _dot(av, w_ref[:, chunks[0]], NN)
            for ci, cols in enumerate(chunks):
                nxt = _dot(av, w_ref[:, chunks[ci + 1]], NN) if ci + 1 < len(chunks) else None
                z_ref[:, cols] = pending.astype(BF16)
                for c in range(cols.start // LANES, cols.stop // LANES):
                    scr[c] = pending[:, c * LANES - cols.start:(c + 1) * LANES - cols.start]
                    for d, ref in zip(WIDE, rest[:-1]):
                        for r in range(d):
                            ref[r, :, _lane_tile(c)] = scr[c, pl.ds(r, tm // d, stride=d), :].astype(BF16)
                pending = nxt

    cm_spec = lambda d: pl.BlockSpec((d, tm // d, N_IN), lambda i, j: (0, i, jnp.minimum(j, nq - 1)))
    return pl.pallas_call(
        body, name="mm_in", grid=(T // tm, NDEV),
        in_specs=[pl.BlockSpec((tm, D), lambda i, j: (i, 0)),
                  pl.BlockSpec((None, D, N_IN), lambda i, j: (j, 0, 0))],
        out_specs=[pl.BlockSpec((tm, N_IN), lambda i, j: (i, j))] + [cm_spec(d) for d in WIDE],
        out_shape=[jax.ShapeDtypeStruct((T, DIN), BF16)]
        + [jax.ShapeDtypeStruct((d, T // d, nq * N_IN), BF16) for d in WIDE],
        scratch_shapes=[pltpu.VMEM((N_IN // LANES, tm, LANES), F32)],
        compiler_params=pltpu.CompilerParams(
            dimension_semantics=("parallel", "arbitrary"), vmem_limit_bytes=VMEM_BIG),
    )(a, w_in)


def _mm_out_bwd(dh1b, w_out, deps=()):
    T = dh1b.shape[0]
    tm = _row_tile(T)

    def body(dy_ref, w_ref, *rest):
        rest = rest[len(deps):]
        dom_ref, scr = rest[0], rest[-1]
        acc = _dot(dy_ref[...], w_ref[...], NT)
        dom_ref[...] = acc.astype(BF16)

        @pl.when(pl.program_id(1) == 0)
        def _():
            _to_lane_tiles(scr, acc)
            _emit_class_major(scr, rest[1:-1], tm)

    return pl.pallas_call(
        body, name="mm_out_bwd", grid=(T // tm, D // DA),
        in_specs=[pl.BlockSpec((tm, D), lambda i, j: (i, 0)),
                  pl.BlockSpec((DA, D), lambda i, j: (j, 0))] + [ANY_SPEC] * len(deps),
        out_specs=[pl.BlockSpec((tm, DA), lambda i, j: (i, j))]
        + [pl.BlockSpec((d, tm // d, DA), lambda i, j: (0, i, 0)) for d in WIDE],
        out_shape=[jax.ShapeDtypeStruct((T, D), BF16)]
        + [jax.ShapeDtypeStruct((d, T // d, DA), BF16) for d in WIDE],
        scratch_shapes=[pltpu.VMEM((DA // LANES, tm, LANES), F32)],
        compiler_params=pltpu.CompilerParams(
            dimension_semantics=("parallel", "arbitrary"), vmem_limit_bytes=VMEM_BIG),
    )(dh1b, w_out, *deps)


def _mm_out(o_attn, o_conv, w_out, x):
    T = x.shape[0]
    tm = _row_tile(T)
    tn = 1024
    return _mm(
        "mm_out", (T // tm, D // tn, 1),
        [pl.BlockSpec((tm, DA), lambda i, j, k: (i, 0)),
         pl.BlockSpec((DA, tn), lambda i, j, k: (0, j)),
         pl.BlockSpec((tm, DC), lambda i, j, k: (i, 0)),
         pl.BlockSpec((DC, tn), lambda i, j, k: (1, j)),
         pl.BlockSpec((tm, tn), lambda i, j, k: (i, j))],
        [o_attn, w_out, o_conv, w_out, x],
        [pl.BlockSpec((tm, tn), lambda i, j, k: (i, j))],
        [jax.ShapeDtypeStruct((T, D), F32)], NN, 2, _ep_resid)[0]


FF_TN = 512
FF_TK = 2 * N_FF


def _mm_gate_up(f, wg_t, wu_t):
    T = f.shape[0]
    tm = _row_tile(T)

    def body(f_ref, wg_ref, wu_ref, g_ref, u_ref, a_ref):
        fv = f_ref[...]
        g = _dot(fv, wg_ref[...], NT)
        u = _dot(fv, wu_ref[...], NT)
        g_ref[...] = g.astype(BF16)
        u_ref[...] = u.astype(BF16)
        a_ref[...] = (g * _sigmoid(g) * u).astype(BF16)

    wspec = pl.BlockSpec((FF_TN, D), lambda i, j: (j, 0))
    ospec = pl.BlockSpec((tm, FF_TN), lambda i, j: (i, j))
    sh = jax.ShapeDtypeStruct((T, DFF), BF16)
    return pl.pallas_call(
        body, name="mm_gate_up", grid=(T // tm, DFF // FF_TN),
        in_specs=[pl.BlockSpec((tm, D), lambda i, j: (i, 0)), wspec, wspec],
        out_specs=[ospec, ospec, ospec], out_shape=[sh, sh, sh],
        compiler_params=pltpu.CompilerParams(
            dimension_semantics=("parallel", "parallel"), vmem_limit_bytes=VMEM_BIG),
    )(f, wg_t, wu_t)


def _mm_down(act, w_down, h1):
    T = h1.shape[0]
    tm = _row_tile(T)
    tn = 1024
    return _mm(
        "mm_down", (T // tm, D // tn, DFF // FF_TK),
        [pl.BlockSpec((tm, FF_TK), lambda i, j, k: (i, k)),
         pl.BlockSpec((FF_TK, tn), lambda i, j, k: (k, j)),
         pl.BlockSpec((tm, tn), lambda i, j, k: (i, j))],
        [act, w_down, h1],
        [pl.BlockSpec((tm, tn), lambda i, j, k: (i, j))],
        [jax.ShapeDtypeStruct((T, D), F32)], NN, 1, _ep_resid, acc_shape=(tm, tn))[0]


def _mm_ple(r, w_pgate, b_pgate, p, w_ple, h2):
    T = h2.shape[0]
    tm = _row_tile(T)
    tn = 1024

    def body(r_ref, wg_ref, b_ref, p_ref, wp_ref, h2_ref, gte_ref, pe_ref, h3_ref):
        gte = _sigmoid(_dot(r_ref[...], wg_ref[...], NN) + b_ref[...])
        pe = _dot(p_ref[...], wp_ref[...], NN)
        gte_ref[...] = gte.astype(BF16)
        pe_ref[...] = pe.astype(BF16)
        h3_ref[...] = h2_ref[...] + pe * gte

    tile = pl.BlockSpec((tm, tn), lambda i, j: (i, j))
    return pl.pallas_call(
        body, name="mm_ple", grid=(T // tm, D // tn),
        in_specs=[pl.BlockSpec((tm, D), lambda i, j: (i, 0)),
                  pl.BlockSpec((D, tn), lambda i, j: (0, j)),
                  pl.BlockSpec((1, tn), lambda i, j: (0, j)),
                  pl.BlockSpec((tm, DPLE), lambda i, j: (i, 0)),
                  pl.BlockSpec((DPLE, tn), lambda i, j: (0, j)),
                  tile],
        out_specs=[tile, tile, tile],
        out_shape=[jax.ShapeDtypeStruct((T, D), BF16), jax.ShapeDtypeStruct((T, D), BF16),
                   jax.ShapeDtypeStruct((T, D), F32)],
        compiler_params=pltpu.CompilerParams(
            dimension_semantics=("parallel", "parallel"), vmem_limit_bytes=VMEM_BIG),
    )(r, w_pgate, b_pgate, p, w_ple, h2)


def _mm_nt(name, dy, w, deps=()):
    T, n = dy.shape
    kdim = w.shape[0]
    tm = _row_tile(T)
    tn = 1024
    return _mm(
        name, (T // tm, kdim // tn, 1),
        [pl.BlockSpec((tm, n), lambda i, j, k: (i, 0)),
         pl.BlockSpec((tn, n), lambda i, j, k: (j, 0))],
        [dy, w],
        [pl.BlockSpec((tm, tn), lambda i, j, k: (i, j))],
        [jax.ShapeDtypeStruct((T, kdim), BF16)], NT, 1, _ep_cast(BF16), deps=deps)[0]


def _mm_down_bwd(dh2, w_down, g, u, deps=()):
    T = dh2.shape[0]
    tm = _row_tile(T)
    gspec = pl.BlockSpec((tm, FF_TN), lambda i, j, k: (i, j))
    sh = jax.ShapeDtypeStruct((T, DFF), BF16)
    return _mm(
        "mm_down_bwd", (T // tm, DFF // FF_TN, 1),
        [pl.BlockSpec((tm, D), lambda i, j, k: (i, 0)),
         pl.BlockSpec((FF_TN, D), lambda i, j, k: (j, 0)),
         gspec, gspec],
        [dh2, w_down, g, u],
        [gspec, gspec], [sh, sh], NT, 1, _ep_swiglu_bwd, deps=deps)


def _mm_ffn_in_bwd(dg, wg_t, du, wu_t, deps=()):
    T = dg.shape[0]
    tm = _row_tile(T)
    tn = 1024
    aspec = pl.BlockSpec((tm, FF_TK), lambda i, j, k: (i, k))
    wspec = pl.BlockSpec((FF_TK, tn), lambda i, j, k: (k, j))
    return _mm(
        "mm_ffn_in_bwd", (T // tm, D // tn, DFF // FF_TK),
        [aspec, wspec, aspec, wspec], [dg, wg_t, du, wu_t],
        [pl.BlockSpec((tm, tn), lambda i, j, k: (i, j))],
        [jax.ShapeDtypeStruct((T, D), BF16)], NN, 2, _ep_cast(BF16), acc_shape=(tm, tn),
        deps=deps)[0]


def _mm_in_bwd(dz, w_in, deps=()):
    T = dz.shape[0]
    tm = _row_tile(T)
    tn = 1024
    sg = 4
    return _mm(
        "mm_in_bwd", (T // tm, D // tn, NDEV // sg),
        [pl.BlockSpec((tm, sg * N_IN), lambda i, j, k: (i, k)),
         pl.BlockSpec((sg, tn, N_IN), lambda i, j, k: (k, j, 0))],
        [dz, w_in],
        [pl.BlockSpec((tm, tn), lambda i, j, k: (i, j))],
        [jax.ShapeDtypeStruct((T, D), BF16)], NT, 1, _ep_cast(BF16), acc_shape=(tm, tn),
        deps=deps, group=sg, a_cols=N_IN)[0]


def _mm_tn(name, a, b, tj=None):
    T, idim = a.shape
    jdim = b.shape[1]
    tt = _row_tile(T)
    ti = min(idim, 1024)
    tj = jdim if tj is None else tj
    return _mm(
        name, (idim // ti, jdim // tj, T // tt),
        [pl.BlockSpec((tt, ti), lambda i, j, k: (k, i)),
         pl.BlockSpec((tt, tj), lambda i, j, k: (k, j))],
        [a, b],
        [pl.BlockSpec((ti, tj), lambda i, j, k: (i, j))],
        [jax.ShapeDtypeStruct((idim, jdim), BF16)], TN, 1, _ep_cast(BF16), acc_shape=(ti, tj))[0]


def _mm_tn_cols(name, a, b, ncol):
    T, idim = a.shape
    tt = _tn_rows(T)
    return _mm(
        name, (1, NDEV, T // tt),
        [pl.BlockSpec((tt, idim), lambda i, j, k: (k, 0)),
         pl.BlockSpec((tt, ncol), lambda i, j, k: (k, j))],
        [a, b],
        [pl.BlockSpec((None, idim, ncol), lambda i, j, k: (j, 0, 0))],
        [jax.ShapeDtypeStruct((NDEV, idim, ncol), BF16)], TN, 1, _ep_cast(BF16),
        acc_shape=(idim, ncol))[0]


def _mm_tn_ff(name, a, b):
    T = b.shape[0]
    tt = _tn_rows(T)
    return _mm(
        name, (DFF // FF_TN, 1, T // tt),
        [pl.BlockSpec((tt, FF_TN), lambda i, j, k: (k, i)),
         pl.BlockSpec((tt, D), lambda i, j, k: (k, 0))],
        [a, b],
        [pl.BlockSpec((FF_TN, D), lambda i, j, k: (i, 0))],
        [jax.ShapeDtypeStruct((DFF, D), BF16)], TN, 1, _ep_cast(BF16),
        acc_shape=(FF_TN, D))[0]


TR = 256


def _rows(T):
    return min(TR, T)


def _rms_fwd(name, h, g, deps=()):
    T = h.shape[0]
    tr = _rows(T)

    def body(h_ref, g_ref, *rest):
        o_ref = rest[-1]
        v = h_ref[...]
        r = lax.rsqrt(jnp.mean(v * v, axis=-1, keepdims=True) + EPS)
        o_ref[...] = (v * r * g_ref[...]).astype(BF16)

    return pl.pallas_call(
        body, name=name, grid=(T // tr,),
        in_specs=[pl.BlockSpec((tr, D), lambda i: (i, 0)), pl.BlockSpec((1, D), lambda i: (0, 0))]
        + [ANY_SPEC] * len(deps),
        out_specs=pl.BlockSpec((tr, D), lambda i: (i, 0)),
        out_shape=jax.ShapeDtypeStruct((T, D), BF16),
        compiler_params=pltpu.CompilerParams(dimension_semantics=("parallel",)),
    )(h, g, *deps)


def _fold8(v):
    return jnp.sum(v.reshape(v.shape[0] // 8, 8, v.shape[1]), axis=0)


def _rms_bwd(name, dn_out, h, g, dres, want_bf16):
    T = h.shape[0]
    tr = _rows(T)
    nt = T // tr

    def body(dy_ref, h_ref, g_ref, dres_ref, *rest):
        if want_bf16:
            dh_ref, dhb_ref, dg_ref, acc = rest
        else:
            dh_ref, dg_ref, acc = rest
        i = pl.program_id(0)
        v = h_ref[...]
        r = lax.rsqrt(jnp.mean(v * v, axis=-1, keepdims=True) + EPS)
        nrm = v * r
        dy = dy_ref[...].astype(F32)
        dn = dy * g_ref[...]
        dh = dres_ref[...] + r * (dn - nrm * jnp.mean(dn * nrm, axis=-1, keepdims=True))
        dh_ref[...] = dh
        if want_bf16:
            dhb_ref[...] = dh.astype(BF16)

        @pl.when(i == 0)
        def _():
            acc[...] = jnp.zeros_like(acc)

        acc[...] += _fold8(dy * nrm)

        @pl.when(i == nt - 1)
        def _():
            dg_ref[...] = jnp.sum(acc[...], axis=0, keepdims=True)

    tile = pl.BlockSpec((tr, D), lambda i: (i, 0))
    vec = pl.BlockSpec((1, D), lambda i: (0, 0))
    out_specs = [tile] + ([tile] if want_bf16 else []) + [vec]
    out_shape = ([jax.ShapeDtypeStruct((T, D), F32)]
                 + ([jax.ShapeDtypeStruct((T, D), BF16)] if want_bf16 else [])
                 + [jax.ShapeDtypeStruct((1, D), F32)])
    return pl.pallas_call(
        body, name=name, grid=(nt,),
        in_specs=[tile, tile, vec, tile], out_specs=out_specs, out_shape=out_shape,
        scratch_shapes=[pltpu.VMEM((8, D), F32)],
        compiler_params=pltpu.CompilerParams(dimension_semantics=("arbitrary",)),
    )(dn_out, h, g, dres)


def _loss_bwd(h3, target, g_final, pe, gte):
    T = h3.shape[0]
    tr = _rows(T)
    nt = T // tr

    def body(h_ref, t_ref, g_ref, pe_ref, gte_ref, loss_ref, dh_ref, dpe_ref, dpg_ref,
             dgf_ref, dbp_ref, lacc, gacc, bacc):
        i = pl.program_id(0)
        v = h_ref[...]
        r = lax.rsqrt(jnp.mean(v * v, axis=-1, keepdims=True) + EPS)
        nrm = v * r
        g = g_ref[...]
        err = nrm * g - t_ref[...]
        dy = err * (1.0 / D)
        dn = dy * g
        dh = r * (dn - nrm * jnp.mean(dn * nrm, axis=-1, keepdims=True))
        dh_ref[...] = dh
        gte = gte_ref[...].astype(F32)
        pe = pe_ref[...].astype(F32)
        dpe_ref[...] = (dh * gte).astype(BF16)
        dpg = dh * pe * gte * (1.0 - gte)
        dpg_ref[...] = dpg.astype(BF16)

        @pl.when(i == 0)
        def _():
            lacc[...] = jnp.zeros_like(lacc)
            gacc[...] = jnp.zeros_like(gacc)
            bacc[...] = jnp.zeros_like(bacc)

        lacc[...] += _fold8(err * err)
        gacc[...] += _fold8(dy * nrm)
        bacc[...] += _fold8(dpg)

        @pl.when(i == nt - 1)
        def _():
            tot = jnp.sum(jnp.sum(lacc[...], axis=0, keepdims=True), axis=1, keepdims=True)
            loss_ref[...] = jnp.broadcast_to(tot * (0.5 / D), (1, 128))
            dgf_ref[...] = jnp.sum(gacc[...], axis=0, keepdims=True)
            dbp_ref[...] = jnp.sum(bacc[...], axis=0, keepdims=True)

    tile = pl.BlockSpec((tr, D), lambda i: (i, 0))
    vec = pl.BlockSpec((1, D), lambda i: (0, 0))
    return pl.pallas_call(
        body, name="loss_bwd", grid=(nt,),
        in_specs=[tile, tile, vec, tile, tile],
        out_specs=[pl.BlockSpec((1, 128), lambda i: (0, 0)), tile, tile, tile, vec, vec],
        out_shape=[jax.ShapeDtypeStruct((1, 128), F32), jax.ShapeDtypeStruct((T, D), F32),
                   jax.ShapeDtypeStruct((T, D), BF16), jax.ShapeDtypeStruct((T, D), BF16),
                   jax.ShapeDtypeStruct((1, D), F32), jax.ShapeDtypeStruct((1, D), F32)],
        scratch_shapes=[pltpu.VMEM((8, D), F32)] * 3,
        compiler_params=pltpu.CompilerParams(dimension_semantics=("arbitrary",)),
    )(h3, target, g_final, pe, gte)


def _band_masks():
    qi = lax.broadcasted_iota(jnp.int32, (BLK, BLK), 0)
    kj = lax.broadcasted_iota(jnp.int32, (BLK, BLK), 1)
    return kj >= qi, kj <= qi


def _cm_spec(d, col, rowmap=lambda n: n):
    if d == 1:
        return pl.BlockSpec((BLK, DA), lambda r, n: (rowmap(n), col))
    return pl.BlockSpec((None, BLK, DA), lambda r, n: (r, rowmap(n), col))


def _cm_shape(d, T, dtype):
    return jax.ShapeDtypeStruct((T, DA) if d == 1 else (d, T // d, DA), dtype)


def _attn_fwd(name, zsrc, d, T):
    nb = T // d // BLK
    scale = DH ** -0.5

    def body(q_ref, kp_ref, kc_ref, vp_ref, vc_ref, o_ref, l_ref):
        n = pl.program_id(1)
        prev_ok, cur_ok = _band_masks()
        prev_ok = prev_ok & (n > 0)
        heads = [slice(h * DH, (h + 1) * DH) for h in range(NH)]
        s = [(jnp.where(prev_ok, _dot(q_ref[:, sl], kp_ref[:, sl], NT) * scale, NEG),
              jnp.where(cur_ok, _dot(q_ref[:, sl], kc_ref[:, sl], NT) * scale, NEG)) for sl in heads]
        m = [jnp.maximum(jnp.max(sp, axis=1, keepdims=True), jnp.max(sc, axis=1, keepdims=True))
             for sp, sc in s]
        p = [(jnp.exp(sp - mh), jnp.exp(sc - mh)) for (sp, sc), mh in zip(s, m)]
        den = [jnp.sum(pp, axis=1, keepdims=True) + jnp.sum(pc, axis=1, keepdims=True) for pp, pc in p]
        o = [_dot(pp.astype(BF16), vp_ref[:, sl], NN) + _dot(pc.astype(BF16), vc_ref[:, sl], NN)
             for (pp, pc), sl in zip(p, heads)]
        o_ref[...] = jnp.concatenate([(oh / dh).astype(BF16) for oh, dh in zip(o, den)], axis=1)
        l_ref[...] = jnp.concatenate(
            [jnp.broadcast_to(mh + jnp.log(dh), (BLK, DH)) for mh, dh in zip(m, den)], axis=1)

    prev = lambda n: jnp.maximum(n - 1, 0)
    return pl.pallas_call(
        body, name=name, grid=(d, nb),
        in_specs=[_cm_spec(d, 0), _cm_spec(d, 1, prev), _cm_spec(d, 1), _cm_spec(d, 2, prev),
                  _cm_spec(d, 2)],
        out_specs=[_cm_spec(d, 0)] * 2,
        out_shape=[_cm_shape(d, T, BF16), _cm_shape(d, T, F32)],
        compiler_params=pltpu.CompilerParams(dimension_semantics=("parallel", "parallel")),
    )(zsrc, zsrc, zsrc, zsrc, zsrc)


def _cm_tile(d, tr):
    if d == 1:
        return pl.BlockSpec((tr, DA), lambda i: (i, 0))
    return pl.BlockSpec((d, tr // d, DA), lambda i: (0, i, 0))


def _attn_combine(outs, lses, T):
    tr = _rows(T)

    def body(*refs):
        o_in, l_in = refs[:3], refs[3:6]
        o_ref, l_ref = refs[6:8]
        o_cm, l_cm = refs[8:8 + len(WIDE)], refs[8 + len(WIDE):8 + 2 * len(WIDE)]
        so, sl, so_all, sl_all = refs[8 + 2 * len(WIDE):]
        for c in range(DA // LANES):
            lt = _lane_tile(c)
            os_, ls_ = [o_in[0][:, lt].astype(F32)], [l_in[0][:, lt]]
            for w, d in enumerate(WIDE):
                for r in range(d):
                    so[w, c, pl.ds(r, tr // d, stride=d), :] = o_in[1 + w][r, :, lt].astype(F32)
                    sl[w, c, pl.ds(r, tr // d, stride=d), :] = l_in[1 + w][r, :, lt]
                os_.append(so[w, c])
                ls_.append(sl[w, c])
            la, lb, lc = ls_
            m = jnp.maximum(jnp.maximum(la, lb), lc)
            ea, eb, ec = jnp.exp(la - m), jnp.exp(lb - m), jnp.exp(lc - m)
            s = ea + eb + ec
            o = (ea * os_[0] + eb * os_[1] + ec * os_[2]) / s
            lse = m + jnp.log(s)
            o_ref[:, lt] = o.astype(BF16)
            l_ref[:, lt] = lse
            so_all[c] = o
            sl_all[c] = lse
        _emit_class_major(so_all, o_cm, tr)
        _emit_class_major(sl_all, l_cm, tr)

    specs = [_cm_tile(d, tr) for d in DILATIONS]
    wide = [_cm_tile(d, tr) for d in WIDE]
    return pl.pallas_call(
        body, name="attn_combine", grid=(T // tr,),
        in_specs=specs + specs,
        out_specs=[specs[0], specs[0]] + wide + wide,
        out_shape=[_cm_shape(1, T, BF16), _cm_shape(1, T, F32)]
        + [_cm_shape(d, T, BF16) for d in WIDE] + [_cm_shape(d, T, F32) for d in WIDE],
        scratch_shapes=[pltpu.VMEM((len(WIDE), DA // LANES, tr, LANES), F32)] * 2
        + [pltpu.VMEM((DA // LANES, tr, LANES), F32)] * 2,
        compiler_params=pltpu.CompilerParams(
            dimension_semantics=("parallel",), vmem_limit_bytes=VMEM_MID),
    )(*outs, *lses)


def _attn_bwd_q(name, zsrc, dosrc, osrc, lsrc, d, T):
    nb = T // d // BLK
    scale = DH ** -0.5

    def body(q_ref, kp_ref, kc_ref, vp_ref, vc_ref, do_ref, o_ref, l_ref, dq_ref):
        n = pl.program_id(1)
        prev_ok, cur_ok = _band_masks()
        prev_ok = prev_ok & (n > 0)
        heads = [slice(h * DH, (h + 1) * DH) for h in range(NH)]
        s = [(_dot(q_ref[:, sl], kp_ref[:, sl], NT), _dot(q_ref[:, sl], kc_ref[:, sl], NT))
             for sl in heads]
        dp = [(_dot(do_ref[:, sl], vp_ref[:, sl], NT), _dot(do_ref[:, sl], vc_ref[:, sl], NT))
              for sl in heads]
        delta = [jnp.sum(do_ref[:, sl].astype(F32) * o_ref[:, sl].astype(F32), axis=1, keepdims=True)
                 for sl in heads]
        p = [(jnp.exp(jnp.where(prev_ok, sp * scale - l_ref[:, sl], NEG)),
              jnp.exp(jnp.where(cur_ok, sc * scale - l_ref[:, sl], NEG)))
             for (sp, sc), sl in zip(s, heads)]
        ds = [((pp * (dpp - dl) * scale).astype(BF16), (pc * (dpc - dl) * scale).astype(BF16))
              for (pp, pc), (dpp, dpc), dl in zip(p, dp, delta)]
        dq = [_dot(dsp, kp_ref[:, sl], NN) + _dot(dsc, kc_ref[:, sl], NN)
              for (dsp, dsc), sl in zip(ds, heads)]
        dq_ref[...] = jnp.concatenate([v.astype(BF16) for v in dq], axis=1)

    prev = lambda n: jnp.maximum(n - 1, 0)
    own = _cm_spec(d, 0)
    return pl.pallas_call(
        body, name=name, grid=(d, nb),
        in_specs=[own, _cm_spec(d, 1, prev), _cm_spec(d, 1), _cm_spec(d, 2, prev), _cm_spec(d, 2),
                  own, own, own],
        out_specs=own, out_shape=_cm_shape(d, T, BF16),
        compiler_params=pltpu.CompilerParams(dimension_semantics=("parallel", "parallel")),
    )(zsrc, zsrc, zsrc, zsrc, zsrc, dosrc, osrc, lsrc)


def _attn_bwd_kv(name, zsrc, dosrc, osrc, lsrc, d, T):
    nb = T // d // BLK
    scale = DH ** -0.5

    def body(k_ref, v_ref, qa_ref, qb_ref, doa_ref, dob_ref, oa_ref, ob_ref, la_ref, lb_ref,
             dk_ref, dv_ref):
        j = pl.program_id(1)
        next_ok, own_ok = _band_masks()
        next_ok = next_ok & (j < nb - 1)
        heads = [slice(h * DH, (h + 1) * DH) for h in range(NH)]
        s = [(_dot(qa_ref[:, sl], k_ref[:, sl], NT), _dot(qb_ref[:, sl], k_ref[:, sl], NT))
             for sl in heads]
        dp = [(_dot(doa_ref[:, sl], v_ref[:, sl], NT), _dot(dob_ref[:, sl], v_ref[:, sl], NT))
              for sl in heads]
        delta = [(jnp.sum(doa_ref[:, sl].astype(F32) * oa_ref[:, sl].astype(F32), axis=1, keepdims=True),
                  jnp.sum(dob_ref[:, sl].astype(F32) * ob_ref[:, sl].astype(F32), axis=1, keepdims=True))
                 for sl in heads]
        p = [(jnp.exp(jnp.where(own_ok, sa * scale - la_ref[:, sl], NEG)),
              jnp.exp(jnp.where(next_ok, sb * scale - lb_ref[:, sl], NEG)))
             for (sa, sb), sl in zip(s, heads)]
        dv = [_dot(pa.astype(BF16), doa_ref[:, sl], TN) + _dot(pb.astype(BF16), dob_ref[:, sl], TN)
              for (pa, pb), sl in zip(p, heads)]
        ds = [((pa * (dpa - da) * scale).astype(BF16), (pb * (dpb - db) * scale).astype(BF16))
              for (pa, pb), (dpa, dpb), (da, db) in zip(p, dp, delta)]
        dk = [_dot(dsa, qa_ref[:, sl], TN) + _dot(dsb, qb_ref[:, sl], TN)
              for (dsa, dsb), sl in zip(ds, heads)]
        dk_ref[...] = jnp.concatenate([v.astype(BF16) for v in dk], axis=1)
        dv_ref[...] = jnp.concatenate([v.astype(BF16) for v in dv], axis=1)

    nxt = lambda j: jnp.minimum(j + 1, nb - 1)
    own, own_n = _cm_spec(d, 0), _cm_spec(d, 0, nxt)
    sh = _cm_shape(d, T, BF16)
    return pl.pallas_call(
        body, name=name, grid=(d, nb),
        in_specs=[_cm_spec(d, 1), _cm_spec(d, 2), own, own_n, own, own_n, own, own_n, own, own_n],
        out_specs=[own, own], out_shape=[sh, sh],
        compiler_params=pltpu.CompilerParams(dimension_semantics=("parallel", "parallel")),
    )(zsrc, zsrc, zsrc, zsrc, dosrc, dosrc, osrc, osrc, lsrc, lsrc)


def _dz_assemble(dqs, dks, dvs, dcvg, T):
    tr = _rows(T)
    nb = len(DILATIONS)

    def body(*refs):
        cvg_ref, dz_ref, scr = refs[3 * nb], refs[3 * nb + 1], refs[3 * nb + 2]
        for g in range(3):
            parts = refs[g * nb:(g + 1) * nb]
            for c in range(DA // LANES):
                lt = _lane_tile(c)
                scr[g, c] = parts[0][:, lt].astype(F32)
                for w, d in enumerate(WIDE):
                    for r in range(d):
                        rows = pl.ds(r, tr // d, stride=d)
                        scr[g, c, rows, :] = scr[g, c, rows, :] + parts[1 + w][r, :, lt].astype(F32)
                dz_ref[:, g * DA + c * LANES:g * DA + (c + 1) * LANES] = scr[g, c].astype(BF16)
        dz_ref[:, 3 * DA:] = cvg_ref[...]

    specs = [_cm_tile(d, tr) for d in DILATIONS]
    return pl.pallas_call(
        body, name="dz_assemble", grid=(T // tr,),
        in_specs=specs * 3 + [pl.BlockSpec((tr, 2 * DC), lambda i: (i, 0))],
        out_specs=pl.BlockSpec((tr, DIN), lambda i: (i, 0)),
        out_shape=jax.ShapeDtypeStruct((T, DIN), BF16),
        scratch_shapes=[pltpu.VMEM((3, DA // LANES, tr, LANES), F32)],
        compiler_params=pltpu.CompilerParams(
            dimension_semantics=("parallel",), vmem_limit_bytes=VMEM_MID),
    )(*dqs, *dks, *dvs, dcvg)


CT = 256
HALO = 32
RC = 32


def _conv_fwd(z, w_dw, b_dw, g_ln, b_ln):
    T = z.shape[0]
    ct = min(CT, T)
    nt = T // ct
    hb = ct // HALO

    def body(cv_ref, cg_ref, cvp_ref, cgp_ref, w_ref, bdw_ref, g_ref, b_ref, oc_ref, y_ref, ubuf, ush):
        i = pl.program_id(0)
        up = cvp_ref[...].astype(F32) * _sigmoid(cgp_ref[...].astype(F32))
        ubuf[0:HALO, :] = jnp.where(i > 0, up, 0.0)
        ubuf[HALO:, :] = cv_ref[...].astype(F32) * _sigmoid(cg_ref[...].astype(F32))
        for b in range(8):
            ush[b] = ubuf[pl.ds(8 - b, ct + 24), :]

        def chunk(ci, carry):
            r0 = pl.multiple_of(ci * RC, RC)
            acc = jnp.broadcast_to(bdw_ref[...], (RC, DC))
            for s in range(CW):
                a, b = divmod(s, 8)
                acc = acc + w_ref[CW - 1 - s:CW - s, :] * ush[b, pl.ds(r0 + 24 - 8 * a, RC), :]
            y_ref[pl.ds(r0, RC), :] = acc
            mu = jnp.mean(acc, axis=-1, keepdims=True)
            cen = acc - mu
            var = jnp.mean(cen * cen, axis=-1, keepdims=True)
            ln = cen * lax.rsqrt(var + EPS) * g_ref[...] + b_ref[...]
            oc_ref[pl.ds(r0, RC), :] = (ln * _sigmoid(ln)).astype(BF16)
            return carry

        lax.fori_loop(0, ct // RC, chunk, 0)

    cur = lambda col: pl.BlockSpec((ct, DC), lambda i: (i, col))
    prv = lambda col: pl.BlockSpec((HALO, DC), lambda i: (jnp.maximum(i * hb - 1, 0), col))
    vec = pl.BlockSpec((1, DC), lambda i: (0, 0))
    return pl.pallas_call(
        body, name="conv_fwd", grid=(nt,),
        in_specs=[cur(3), cur(4), prv(3), prv(4), pl.BlockSpec((CW, DC), lambda i: (0, 0)),
                  vec, vec, vec],
        out_specs=[pl.BlockSpec((ct, DC), lambda i: (i, 0))] * 2,
        out_shape=[jax.ShapeDtypeStruct((T, DC), BF16), jax.ShapeDtypeStruct((T, DC), F32)],
        scratch_shapes=[pltpu.VMEM((ct + HALO, DC), F32), pltpu.VMEM((8, ct + 24, DC), F32)],
        compiler_params=pltpu.CompilerParams(
            dimension_semantics=("parallel",), vmem_limit_bytes=VMEM_MID),
    )(z, z, z, z, w_dw, b_dw, g_ln, b_ln)


def _conv_bwd(z, dom, y, w_dw, g_ln, b_ln):
    T = z.shape[0]
    ct = min(CT, T)
    nt = T // ct
    hb = ct // HALO
    last_halo = T // HALO - 1

    def ln_bwd(yv, dov, g_ref, b_ref):
        mu = jnp.mean(yv, axis=-1, keepdims=True)
        cen = yv - mu
        rstd = lax.rsqrt(jnp.mean(cen * cen, axis=-1, keepdims=True) + EPS)
        xhat = cen * rstd
        ln = xhat * g_ref[...] + b_ref[...]
        sg = _sigmoid(ln)
        dln = dov * (sg * (1.0 + ln * (1.0 - sg)))
        dxh = dln * g_ref[...]
        dy = rstd * (dxh - jnp.mean(dxh, axis=-1, keepdims=True)
                     - xhat * jnp.mean(dxh * xhat, axis=-1, keepdims=True))
        return dy, dln, xhat

    def body(do_ref, don_ref, y_ref, yn_ref, cv_ref, cg_ref, cvp_ref, cgp_ref, w_ref, g_ref, b_ref,
             dcvg_ref, dw_ref, dbdw_ref, dg_ref, db_ref,
             dybuf, dysh, ubuf, ush, dwacc, vacc):
        i = pl.program_id(0)

        @pl.when(i == 0)
        def _():
            dwacc[...] = jnp.zeros_like(dwacc)
            vacc[...] = jnp.zeros_like(vacc)

        def ln_chunk(ci, carry):
            r0 = pl.multiple_of(ci * RC, RC)
            dy, dln, xhat = ln_bwd(y_ref[pl.ds(r0, RC), :], do_ref[pl.ds(r0, RC), :].astype(F32),
                                   g_ref, b_ref)
            dybuf[pl.ds(r0, RC), :] = dy
            vacc[0] += _fold8(dy)
            vacc[1] += _fold8(dln * xhat)
            vacc[2] += _fold8(dln)
            return carry

        lax.fori_loop(0, ct // RC, ln_chunk, 0)
        dyn, _, _ = ln_bwd(yn_ref[...], don_ref[...].astype(F32), g_ref, b_ref)
        dybuf[ct:, :] = jnp.where(i < nt - 1, dyn, 0.0)
        for b in range(8):
            dysh[b] = dybuf[pl.ds(b, ct + 24), :]

        up = cvp_ref[...].astype(F32) * _sigmoid(cgp_ref[...].astype(F32))
        ubuf[0:HALO, :] = jnp.where(i > 0, up, 0.0)
        ubuf[HALO:, :] = cv_ref[...].astype(F32) * _sigmoid(cg_ref[...].astype(F32))
        for b in range(8):
            ush[b] = ubuf[pl.ds(8 - b, ct + 24), :]

        def chunk(ci, carry):
            r0 = pl.multiple_of(ci * RC, RC)
            dy = dybuf[pl.ds(r0, RC), :]
            du = jnp.zeros((RC, DC), F32)
            for s in range(CW):
                a, b = divmod(s, 8)
                du = du + w_ref[CW - 1 - s:CW - s, :] * dysh[b, pl.ds(r0 + 8 * a, RC), :]
                dwacc[CW - 1 - s] += _fold8(dy * ush[b, pl.ds(r0 + 24 - 8 * a, RC), :])
            cv = cv_ref[pl.ds(r0, RC), :].astype(F32)
            sg = _sigmoid(cg_ref[pl.ds(r0, RC), :].astype(F32))
            dcvg_ref[pl.ds(r0, RC), 0:DC] = (du * sg).astype(BF16)
            dcvg_ref[pl.ds(r0, RC), DC:2 * DC] = (du * cv * sg * (1.0 - sg)).astype(BF16)
            return carry

        lax.fori_loop(0, ct // RC, chunk, 0)

        @pl.when(i == nt - 1)
        def _():
            dw_ref[...] = jnp.sum(dwacc[...], axis=1)
            dbdw_ref[...] = jnp.sum(vacc[0], axis=0, keepdims=True)
            dg_ref[...] = jnp.sum(vacc[1], axis=0, keepdims=True)
            db_ref[...] = jnp.sum(vacc[2], axis=0, keepdims=True)

    cur = lambda col: pl.BlockSpec((ct, DC), lambda i: (i, col))
    prv = lambda col: pl.BlockSpec((HALO, DC), lambda i: (jnp.maximum(i * hb - 1, 0), col))
    nxt = lambda col: pl.BlockSpec((HALO, DC), lambda i: (jnp.minimum((i + 1) * hb, last_halo), col))
    vec = pl.BlockSpec((1, DC), lambda i: (0, 0))
    tile = pl.BlockSpec((ct, DC), lambda i: (i, 0))
    return pl.pallas_call(
        body, name="conv_bwd", grid=(nt,),
        in_specs=[cur(1), nxt(1), cur(0), nxt(0), cur(3), cur(4), prv(3), prv(4),
                  pl.BlockSpec((CW, DC), lambda i: (0, 0)), vec, vec],
        out_specs=[pl.BlockSpec((ct, 2 * DC), lambda i: (i, 0)),
                   pl.BlockSpec((CW, DC), lambda i: (0, 0)), vec, vec, vec],
        out_shape=[jax.ShapeDtypeStruct((T, 2 * DC), BF16),
                   jax.ShapeDtypeStruct((CW, DC), F32), jax.ShapeDtypeStruct((1, DC), F32),
                   jax.ShapeDtypeStruct((1, DC), F32), jax.ShapeDtypeStruct((1, DC), F32)],
        scratch_shapes=[pltpu.VMEM((ct + HALO, DC), F32), pltpu.VMEM((8, ct + 24, DC), F32),
                        pltpu.VMEM((ct + HALO, DC), F32), pltpu.VMEM((8, ct + 24, DC), F32),
                        pltpu.VMEM((CW, 8, DC), F32), pltpu.VMEM((3, 8, DC), F32)],
        compiler_params=pltpu.CompilerParams(
            dimension_semantics=("arbitrary",), vmem_limit_bytes=VMEM_BIG),
    )(dom, dom, y, y, z, z, z, z, w_dw, g_ln, b_ln)


def _adam_math(w, g, m, v):
    m = ADAM_B1 * m + (1.0 - ADAM_B1) * g
    v = ADAM_B2 * v + (1.0 - ADAM_B2) * (g * g)
    m_hat = m / (1.0 - ADAM_B1 ** ADAM_STEP)
    v_hat = v / (1.0 - ADAM_B2 ** ADAM_STEP)
    delta = -ADAM_LR * (m_hat / (jnp.sqrt(v_hat) + ADAM_EPS) + ADAM_WD * w)
    return delta, m, v


def _adam(name, slots, w, m, v):
    rows, cols = w.shape
    tr = next(t for t in (256, 176, 128, 64, 32, 16, 8, rows) if rows % t == 0)

    def body(s_ref, w_ref, m_ref, v_ref, g_out, d_out, m_out, v_out):
        g = s_ref[0].astype(F32)
        for s in range(1, NDEV):
            g = g + s_ref[s].astype(F32)
        delta, mn, vn = _adam_math(w_ref[...], g, m_ref[...], v_ref[...])
        g_out[...] = g
        d_out[...] = delta
        m_out[...] = mn
        v_out[...] = vn

    tile = pl.BlockSpec((tr, cols), lambda i: (i, 0))
    sh = jax.ShapeDtypeStruct((rows, cols), F32)
    return pl.pallas_call(
        body, name=name, grid=(rows // tr,),
        in_specs=[pl.BlockSpec((NDEV, tr, cols), lambda i: (0, i, 0)), tile, tile, tile],
        out_specs=[tile] * 4, out_shape=[sh] * 4,
        compiler_params=pltpu.CompilerParams(
            dimension_semantics=("parallel",), vmem_limit_bytes=VMEM_MID),
    )(slots, w, m, v)


SMALL_NAMES = ("g_mix", "b_dw", "g_conv_ln", "b_conv_ln", "g_ffn", "g_ple", "b_pgate", "g_final")


def _pack_small(vecs, w_dw_full):
    rows = [jnp.pad(v.reshape(1, -1), ((0, 0), (0, SMALL_W - v.size))) for v in vecs]
    rows.append(jnp.pad(w_dw_full, ((0, 0), (0, SMALL_W - DC))))
    rows.append(jnp.zeros((SMALL_ROWS - len(vecs) - CW, SMALL_W), F32))
    return jnp.concatenate(rows, axis=0)


def kernel(x, p, g_mix, w_in, w_dw, b_dw, g_conv_ln, b_conv_ln, w_out, g_ffn, w_gate, w_up, w_down, g_ple, w_pgate, b_pgate, w_ple, g_final, loss_target, m_g_mix, m_w_in, m_w_dw, m_b_dw, m_g_conv_ln, m_b_conv_ln, m_w_out, m_g_ffn, m_w_gate, m_w_up, m_w_down, m_g_ple, m_w_pgate, m_b_pgate, m_w_ple, m_g_final, v_g_mix, v_w_in, v_w_dw, v_b_dw, v_g_conv_ln, v_b_conv_ln, v_w_out, v_g_ffn, v_w_gate, v_w_up, v_w_down, v_g_ple, v_w_pgate, v_b_pgate, v_w_ple, v_g_final):
    T = x.shape[1]
    me = 4 * lax.axis_index("x") + 2 * lax.axis_index("y") + lax.axis_index("c")
    xs = x.reshape(T, D)
    ps = p.reshape(T, DPLE).astype(BF16)
    tgt = loss_target.reshape(T, D)
    g_final2 = g_final.reshape(1, D)

    tr_names = ("w_gate", "w_up")
    big = dict(w_in=w_in[0], w_out=w_out[0], w_gate=w_gate[0].T, w_up=w_up[0].T, w_down=w_down[0],
               w_pgate=w_pgate[0], w_ple=w_ple[0])
    order = ("w_in", "w_out", "w_gate", "w_up", "w_down", "w_pgate", "w_ple")
    g_order = ("w_dw",) + order
    g_items = [(w_dw.reshape(CW, DC // NDEV), False)] + [(big[n].astype(BF16), False) for n in order]
    g_handles, g_token = _xstart("gather_start", g_items, _place("gather_place", g_items))
    G = dict(zip(g_order, g_handles))

    a = _rms_fwd("rms_mix", xs, g_mix, deps=[g_token])
    w_dw_f = _xwait("gather_wait_w_dw", G["w_dw"], a).transpose(1, 0, 2).reshape(CW, DC)
    w_in_f = _xwait("gather_wait_w_in", G["w_in"], a)
    z, *z_wide = _mm_in(a, w_in_f)
    zsrc = dict(zip(DILATIONS, [z] + z_wide))
    br = [_attn_fwd(f"attn_fwd_d{d}", zsrc[d], d, T) for d in DILATIONS]
    comb = list(_attn_combine([b[0] for b in br], [b[1] for b in br], T))
    o_attn, lse = comb[0], comb[1]
    osrc = dict(zip(DILATIONS, [o_attn] + comb[2:2 + len(WIDE)]))
    lsrc = dict(zip(DILATIONS, [lse] + comb[2 + len(WIDE):]))
    o_conv, y_conv = _conv_fwd(z, w_dw_f, b_dw, g_conv_ln, b_conv_ln)
    w_out_f = _xwait("gather_wait_w_out", G["w_out"], o_conv).reshape(D, D)
    h1 = _mm_out(o_attn, o_conv, w_out_f, xs)
    f = _rms_fwd("rms_ffn", h1, g_ffn)
    w_gate_f = _xwait("gather_wait_w_gate", G["w_gate"], f).reshape(DFF, D)
    w_up_f = _xwait("gather_wait_w_up", G["w_up"], f).reshape(DFF, D)
    gate, up, act = _mm_gate_up(f, w_gate_f, w_up_f)
    w_down_f = _xwait("gather_wait_w_down", G["w_down"], act).reshape(DFF, D)
    h2 = _mm_down(act, w_down_f, h1)
    r = _rms_fwd("rms_ple", h2, g_ple)
    w_pgate_f = _xwait("gather_wait_w_pgate", G["w_pgate"], r).reshape(D, D)
    w_ple_f = _xwait("gather_wait_w_ple", G["w_ple"], r).transpose(1, 0, 2).reshape(DPLE, D)
    gte, pe, h3 = _mm_ple(r, w_pgate_f, b_pgate, ps, w_ple_f, h2)

    loss_part, dh3, dpe, dpg, d_g_final, d_b_pgate = _loss_bwd(h3, tgt, g_final2, pe, gte)
    H = {}

    def send_grads(tag, named):
        items = [(v, True) for _, v in named]
        handles, token = _xstart(f"grads_start_{tag}", items, _place(f"grads_place_{tag}", items))
        H.update(zip([n for n, _ in named], handles))
        return token

    gw_pgate = _mm_tn("gw_pgate", r, dpg).reshape(NDEV, D // NDEV, D)
    gw_ple = _mm_tn("gw_ple", ps, dpe).reshape(DPLE, NDEV, D // NDEV).transpose(1, 0, 2)
    tok = send_grads("ple", [("w_pgate", gw_pgate), ("w_ple", gw_ple)])
    dr = _mm_nt("mm_pgate_bwd", dpg, w_pgate_f, deps=[tok])
    dh2, dh2b, d_g_ple = _rms_bwd("rms_ple_bwd", dr, h2, g_ple, dh3, True)
    ff_shards = lambda g: g.reshape(NDEV, N_FF, D)
    gw_down = ff_shards(_mm_tn_ff("gw_down", act, dh2b))
    tok = send_grads("down", [("w_down", gw_down)])
    dgate, dup = _mm_down_bwd(dh2b, w_down_f, gate, up, deps=[tok])
    gw_gate = ff_shards(_mm_tn_ff("gw_gate", dgate, f))
    gw_up = ff_shards(_mm_tn_ff("gw_up", dup, f))
    tok = send_grads("ffn", [("w_gate", gw_gate), ("w_up", gw_up)])
    df = _mm_ffn_in_bwd(dgate, w_gate_f, dup, w_up_f, deps=[tok])
    dh1, dh1b, d_g_ffn = _rms_bwd("rms_ffn_bwd", df, h1, g_ffn, dh2, True)
    gw_out = jnp.concatenate(
        [_mm_tn("gw_out_attn", o_attn, dh1b), _mm_tn("gw_out_conv", o_conv, dh1b)], axis=0)
    tok = send_grads("out", [("w_out", gw_out.reshape(NDEV, D // NDEV, D))])
    dom, *do_wide = _mm_out_bwd(dh1b, w_out_f, deps=[tok])
    dosrc = dict(zip(DILATIONS, [dom] + do_wide))
    dcvg, d_w_dw, d_b_dw, d_g_ln, d_b_ln = _conv_bwd(z, dom, y_conv, w_dw_f, g_conv_ln, b_conv_ln)
    dqs, dks, dvs = [], [], []
    for d in DILATIONS:
        dqs.append(_attn_bwd_q(f"attn_bwd_q_d{d}", zsrc[d], dosrc[d], osrc[d], lsrc[d], d, T))
        dk, dv = _attn_bwd_kv(f"attn_bwd_kv_d{d}", zsrc[d], dosrc[d], osrc[d], lsrc[d], d, T)
        dks.append(dk)
        dvs.append(dv)
    dz = _dz_assemble(dqs, dks, dvs, dcvg, T)
    gw_in = _mm_tn_cols("gw_in", a, dz, N_IN)
    tok = send_grads("in", [("w_in", gw_in)])
    da = _mm_in_bwd(dz, w_in_f, deps=[tok])
    grad_x, d_g_mix = _rms_bwd("rms_mix_bwd", da, xs, g_mix, dh1, False)

    small_part = _pack_small(
        [d_g_mix, d_b_dw, d_g_ln, d_b_ln, d_g_ffn, d_g_ple, d_b_pgate, d_g_final], d_w_dw)
    small_slots = _exchange("exchange_small_grads", [(small_part, False)])[0]
    S = {n: _xwait(f"grads_wait_{n}", H[n], small_slots)
         for n in ("w_pgate", "w_ple", "w_down", "w_gate", "w_up", "w_out", "w_in")}

    mom = dict(w_in=(m_w_in, v_w_in), w_out=(m_w_out, v_w_out), w_gate=(m_w_gate, v_w_gate),
               w_up=(m_w_up, v_w_up), w_down=(m_w_down, v_w_down), w_pgate=(m_w_pgate, v_w_pgate),
               w_ple=(m_w_ple, v_w_ple))
    upd = {}
    for n in order:
        m_n, v_n = mom[n][0][0], mom[n][1][0]
        if n in tr_names:
            res = _adam(f"adam_{n}", S[n], big[n], m_n.T, v_n.T)
            upd[n] = [t.T[None] for t in res]
        else:
            res = _adam(f"adam_{n}", S[n], big[n], m_n, v_n)
            upd[n] = [t[None] for t in res]

    def lanes(v):
        full = jnp.zeros((CW, NDEV, DC // NDEV), F32)
        full = lax.dynamic_update_slice(full, v.reshape(CW, 1, DC // NDEV), (0, me, 0))
        return full.reshape(CW, DC)

    small_w = _pack_small([g_mix, b_dw, g_conv_ln, b_conv_ln, g_ffn, g_ple, b_pgate, g_final2], lanes(w_dw))
    small_m = _pack_small([m_g_mix, m_b_dw, m_g_conv_ln, m_b_conv_ln, m_g_ffn, m_g_ple, m_b_pgate,
                           m_g_final.reshape(1, D)], lanes(m_w_dw))
    small_v = _pack_small([v_g_mix, v_b_dw, v_g_conv_ln, v_b_conv_ln, v_g_ffn, v_g_ple, v_b_pgate,
                           v_g_final.reshape(1, D)], lanes(v_w_dw))
    small_res = _adam("adam_small", small_slots, small_w, small_m, small_v)

    def unpack(t):
        out = {}
        widths = dict(g_mix=D, b_dw=DC, g_conv_ln=DC, b_conv_ln=DC, g_ffn=D, g_ple=D, b_pgate=D, g_final=D)
        for i, n in enumerate(SMALL_NAMES):
            out[n] = t[i:i + 1, :widths[n]]
        out["g_final"] = out["g_final"].reshape(D)
        taps = t[len(SMALL_NAMES):len(SMALL_NAMES) + CW, :DC].reshape(CW, NDEV, DC // NDEV)
        out["w_dw"] = lax.dynamic_slice(taps, (0, me, 0), (CW, 1, DC // NDEV))[None]
        return out

    small = [unpack(t) for t in small_res]

    loss = lax.psum(loss_part[0, 0], ("x", "y", "c"))
    names = ("g_mix", "w_in", "w_dw", "b_dw", "g_conv_ln", "b_conv_ln", "w_out", "g_ffn", "w_gate",
             "w_up", "w_down", "g_ple", "w_pgate", "b_pgate", "w_ple", "g_final")
    outs = [loss, grad_x.reshape(1, T, D)]
    for kind in range(4):
        for n in names:
            outs.append(upd[n][kind] if n in upd else small[kind][n])
    return tuple(outs)
```

```python
import jax
import jax.numpy as jnp
from jax import lax
from jax.experimental import pallas as pl
from jax.experimental.pallas import tpu as pltpu

F32 = jnp.float32
BF16 = jnp.bfloat16

NDEV = 8
D = 2048
NH = 8
DH = 128
DA = NH * DH
DC = D - DA
DIN = 3 * DA + 2 * DC
DFF = 5632
DPLE = 256
BLK = 128
DILATIONS = (1, 4, 16)
CW = 31
EPS = 1e-6
N_IN = DIN // NDEV
N_FF = DFF // NDEV
NEG = -1e30

ADAM_LR = 0.001
ADAM_B1 = 0.9
ADAM_B2 = 0.999
ADAM_EPS = 1e-08
ADAM_WD = 0.01
ADAM_STEP = 10

VMEM_CAP_V7X = 64 * 1024 * 1024
VMEM_BIG = VMEM_CAP_V7X - 12 * 1024 * 1024
VMEM_MID = 40 * 1024 * 1024

SMALL_W = 2048
SMALL_ROWS = 40


def _sigmoid(v):
    return 1.0 / (1.0 + jnp.exp(-v))


def _dot(a, b, contract):
    return lax.dot_general(a, b, (contract, ((), ())), preferred_element_type=F32)


NN = ((1,), (0,))
NT = ((1,), (1,))
TN = ((0,), (0,))


def _exchange(name, items):
    n = len(items)
    out_shape = [
        jax.ShapeDtypeStruct((NDEV,) + (a.shape[1:] if sc else a.shape), a.dtype)
        for a, sc in items
    ]
    scat = [sc for _, sc in items]

    def body(*refs):
        srcs = refs[:n]
        dsts = refs[n:2 * n]
        send_sems, recv_sems, loc_sems = refs[2 * n:]
        x = lax.axis_index("x")
        y = lax.axis_index("y")
        c = lax.axis_index("c")
        me = 4 * x + 2 * y + c

        local = []
        for i in range(n):
            src = srcs[i].at[me] if scat[i] else srcs[i]
            cp = pltpu.make_async_copy(src, dsts[i].at[me], loc_sems.at[i])
            cp.start()
            local.append(cp)

        remote = []
        for k in range(1, NDEV):
            px = (1 - x) if (k >> 2) & 1 else x
            py = (1 - y) if (k >> 1) & 1 else y
            pc = (1 - c) if k & 1 else c
            peer = 4 * px + 2 * py + pc
            for i in range(n):
                sem = i * (NDEV - 1) + k - 1
                src = srcs[i].at[peer] if scat[i] else srcs[i]
                send = pltpu.make_async_remote_copy(
                    src_ref=src, dst_ref=dsts[i].at[me],
                    send_sem=send_sems.at[sem], recv_sem=recv_sems.at[sem],
                    device_id=(px, py, pc), device_id_type=pl.DeviceIdType.MESH)
                send.start()
                recv = pltpu.make_async_remote_copy(
                    src_ref=src, dst_ref=dsts[i].at[peer],
                    send_sem=send_sems.at[sem], recv_sem=recv_sems.at[sem],
                    device_id=(px, py, pc), device_id_type=pl.DeviceIdType.MESH)
                remote.append((send, recv))
        for send, recv in remote:
            recv.wait_recv()
            send.wait_send()
        for cp in local:
            cp.wait()

    any_spec = pl.BlockSpec(memory_space=pl.ANY)
    return pl.pallas_call(
        body, name=name,
        in_specs=[any_spec] * n, out_specs=[any_spec] * n, out_shape=out_shape,
        scratch_shapes=[
            pltpu.SemaphoreType.DMA((n * (NDEV - 1),)),
            pltpu.SemaphoreType.DMA((n * (NDEV - 1),)),
            pltpu.SemaphoreType.DMA((n,)),
        ],
    )(*[a for a, _ in items])


HBM_SPEC = pl.BlockSpec(memory_space=pltpu.HBM)
SEM_SPEC = pl.BlockSpec(memory_space=pltpu.SEMAPHORE)
ANY_SPEC = pl.BlockSpec(memory_space=pl.ANY)
EFFECT = pltpu.SideEffectType.DATAFLOW_SIDE_EFFECTING


def _peer_of(k):
    x = lax.axis_index("x")
    y = lax.axis_index("y")
    c = lax.axis_index("c")
    px = (1 - x) if (k >> 2) & 1 else x
    py = (1 - y) if (k >> 1) & 1 else y
    pc = (1 - c) if k & 1 else c
    return (px, py, pc), 4 * px + 2 * py + pc


def _my_index():
    return 4 * lax.axis_index("x") + 2 * lax.axis_index("y") + lax.axis_index("c")


def _slot_shape(a, sc):
    return (NDEV,) + (a.shape[1:] if sc else a.shape)


def _divisor_tile(rows):
    return next((t for t in (512, 256, 176, 128, 64, 32, 16) if rows % t == 0), rows)


def _place(name, items):
    lands = []
    for idx, (a, sc) in enumerate(items):
        rows, cols = a.shape[-2:]
        tr = _divisor_tile(rows)

        def body(s_ref, o_ref):
            o_ref[...] = s_ref[...]

        mine = pl.BlockSpec((None, tr, cols), lambda i: (_my_index(), i, 0))
        lands.append(pl.pallas_call(
            body, name=f"{name}_{idx}", grid=(rows // tr,),
            in_specs=[mine if sc else pl.BlockSpec((tr, cols), lambda i: (i, 0))],
            out_specs=mine,
            out_shape=jax.ShapeDtypeStruct(_slot_shape(a, sc), a.dtype),
            compiler_params=pltpu.CompilerParams(dimension_semantics=("parallel",)),
        )(a))
    return lands


def _xstart(name, items, lands):
    n = len(items)
    scat = [sc for _, sc in items]

    def body(*refs):
        srcs = refs[:n]
        lzs = refs[n:2 * n]
        send_sems = refs[2 * n:3 * n]
        recv_sems = refs[3 * n:4 * n]
        token = refs[-1]
        me = _my_index()
        for i in range(n):
            for k in range(1, NDEV):
                peer_id, peer = _peer_of(k)
                src = srcs[i].at[peer] if scat[i] else srcs[i]
                pltpu.make_async_remote_copy(
                    src_ref=src, dst_ref=lzs[i].at[me],
                    send_sem=send_sems[i].at[k - 1], recv_sem=recv_sems[i].at[k - 1],
                    device_id=peer_id, device_id_type=pl.DeviceIdType.MESH).start()
        token[...] = jnp.zeros_like(token)

    sem = pltpu.SemaphoreType.DMA((NDEV - 1,))
    hbm = [pltpu.HBM(a.shape, a.dtype) for a, _ in items] + [pltpu.HBM(l.shape, l.dtype) for l in lands]
    res = pl.pallas_call(
        body, name=name,
        in_specs=[HBM_SPEC] * (2 * n),
        out_specs=[SEM_SPEC] * (2 * n) + [HBM_SPEC] * (2 * n) + [pl.BlockSpec(memory_space=pltpu.VMEM)],
        out_shape=[sem] * (2 * n) + hbm + [jax.ShapeDtypeStruct((8, 128), F32)],
        input_output_aliases={i: 2 * n + i for i in range(2 * n)},
        compiler_params=pltpu.CompilerParams(has_side_effects=EFFECT),
    )(*[pltpu.with_memory_space_constraint(a, pltpu.HBM) for a, _ in items],
      *[pltpu.with_memory_space_constraint(l, pltpu.HBM) for l in lands])
    handles = [(res[i], res[n + i], res[2 * n + i], res[3 * n + i], scat[i]) for i in range(n)]
    return handles, res[-1]


def _xwait(name, handle, after):
    send_sem, recv_sem, src, land, sc = handle

    def body(src_ref, land_ref, send_ref, recv_ref, after_ref, src_dead, got_ref):
        for k in range(1, NDEV):
            peer_id, peer = _peer_of(k)
            cp = pltpu.make_async_remote_copy(
                src_ref=src_ref.at[peer] if sc else src_ref, dst_ref=land_ref.at[peer],
                send_sem=send_ref.at[k - 1], recv_sem=recv_ref.at[k - 1],
                device_id=peer_id, device_id_type=pl.DeviceIdType.MESH)
            cp.wait_send()
            cp.wait_recv()

    return pl.pallas_call(
        body, name=name,
        in_specs=[HBM_SPEC, HBM_SPEC, SEM_SPEC, SEM_SPEC, ANY_SPEC],
        out_specs=[HBM_SPEC, HBM_SPEC],
        out_shape=[pltpu.HBM(src.shape, src.dtype), pltpu.HBM(land.shape, land.dtype)],
        input_output_aliases={0: 0, 1: 1},
        compiler_params=pltpu.CompilerParams(has_side_effects=EFFECT),
    )(src, land, send_sem, recv_sem, after)[1]


def _mm(name, grid, in_specs, operands, out_specs, out_shape, contract, n_pairs, epilogue,
        acc_shape=None, vmem=VMEM_BIG, deps=(), group=1, a_cols=None):
    nk = grid[2]

    def shard(ref, s, is_a):
        if group == 1:
            return ref[...]
        if is_a and a_cols is not None:
            return ref[:, s * a_cols:(s + 1) * a_cols]
        return ref[s]
    n_extra = len(operands) - 2 * n_pairs
    n_out = len(out_shape)
    n_in = len(operands) + len(deps)
    in_specs = list(in_specs) + [ANY_SPEC] * len(deps)
    operands = list(operands) + list(deps)

    def body(*refs):
        ab = refs[:2 * n_pairs]
        extras = refs[2 * n_pairs:2 * n_pairs + n_extra]
        outs = refs[n_in:n_in + n_out]
        dots = [(ab[2 * p], ab[2 * p + 1], s) for p in range(n_pairs) for s in range(group)]
        if nk == 1:
            part = None
            for a_ref, b_ref, s in dots:
                d = _dot(shard(a_ref, s, True), shard(b_ref, s, False), contract)
                part = d if part is None else part + d
            epilogue(part, extras, outs)
        else:
            acc_ref = refs[-1]
            k = pl.program_id(2)

            @pl.when(k == 0)
            def _():
                acc_ref[...] = jnp.zeros_like(acc_ref)

            for a_ref, b_ref, s in dots:
                acc_ref[...] += _dot(shard(a_ref, s, True), shard(b_ref, s, False), contract)

            @pl.when(k == nk - 1)
            def _():
                epilogue(acc_ref[...], extras, outs)

    scratch = [pltpu.VMEM(acc_shape, F32)] if nk > 1 else []
    return pl.pallas_call(
        body, name=name, grid=grid, in_specs=in_specs, out_specs=out_specs, out_shape=out_shape,
        scratch_shapes=scratch,
        compiler_params=pltpu.CompilerParams(
            dimension_semantics=("parallel", "parallel", "arbitrary"), vmem_limit_bytes=vmem),
    )(*operands)


def _ep_cast(dtype):
    def ep(acc, extras, outs):
        outs[0][...] = acc.astype(dtype)
    return ep


def _ep_resid(acc, extras, outs):
    outs[0][...] = extras[0][...] + acc


def _ep_swiglu_bwd(acc, extras, outs):
    outs[0][...] = (acc * extras[0][...].astype(F32)).astype(BF16)
    outs[1][...] = (acc * extras[1][...].astype(F32)).astype(BF16)


MXU_COLS_V7X = 256


def _col_chunks(n):
    return [slice(c, min(c + MXU_COLS_V7X, n)) for c in range(0, n, MXU_COLS_V7X)]


def _row_tile(T):
    return min(1024, T)


def _tn_rows(T):
    return min(2048, T)


WIDE = tuple(d for d in DILATIONS if d > 1)


LANES = 128


def _lane_tile(c):
    return slice(c * LANES, (c + 1) * LANES)


def _to_lane_tiles(scr, val):
    for c in range(scr.shape[0]):
        scr[c] = val[:, _lane_tile(c)]


def _emit_class_major(scr, refs, rows):
    for d, ref in zip(WIDE, refs):
        for r in range(d):
            for c in range(scr.shape[0]):
                ref[r, :, _lane_tile(c)] = scr[c, pl.ds(r, rows // d, stride=d), :].astype(ref.dtype)


def _mm_in(a, w_in):
    T = a.shape[0]
    tm = _row_tile(T)
    nq = -(-3 * DA // N_IN)

    def body(a_ref, w_ref, z_ref, *rest):
        scr = rest[-1]
        j = pl.program_id(1)

        @pl.when(j >= nq)
        def _():
            z_ref[...] = _dot(a_ref[...], w_ref[...], NN).astype(BF16)

        @pl.when(j < nq)
        def _():
            av = a_ref[...]
            chunks = _col_chunks(N_IN)
            pending = _dot(av, w_ref[:, chunks[0]], NN)
            for ci, cols in enumerate(chunks):
                nxt = _dot(av, w_ref[:, chunks[ci + 1]], NN) if ci + 1 < len(chunks) else None
                z_ref[:, cols] = pending.astype(BF16)
                for c in range(cols.start // LANES, cols.stop // LANES):
                    scr[c] = pending[:, c * LANES - cols.start:(c + 1) * LANES - cols.start]
                    for d, ref in zip(WIDE, rest[:-1]):
                        for r in range(d):
                            ref[r, :, _lane_tile(c)] = scr[c, pl.ds(r, tm // d, stride=d), :].astype(BF16)
                pending = nxt

    cm_spec = lambda d: pl.BlockSpec((d, tm // d, N_IN), lambda i, j: (0, i, jnp.minimum(j, nq - 1)))
    return pl.pallas_call(
        body, name="mm_in", grid=(T // tm, NDEV),
        in_specs=[pl.BlockSpec((tm, D), lambda i, j: (i, 0)),
                  pl.BlockSpec((None, D, N_IN), lambda i, j: (j, 0, 0))],
        out_specs=[pl.BlockSpec((tm, N_IN), lambda i, j: (i, j))] + [cm_spec(d) for d in WIDE],
        out_shape=[jax.ShapeDtypeStruct((T, DIN), BF16)]
        + [jax.ShapeDtypeStruct((d, T // d, nq * N_IN), BF16) for d in WIDE],
        scratch_shapes=[pltpu.VMEM((N_IN // LANES, tm, LANES), F32)],
        compiler_params=pltpu.CompilerParams(
            dimension_semantics=("parallel", "arbitrary"), vmem_limit_bytes=VMEM_BIG),
    )(a, w_in)


def _mm_out_bwd(dh1b, w_out, deps=()):
    T = dh1b.shape[0]
    tm = _row_tile(T)

    def body(dy_ref, w_ref, *rest):
        rest = rest[len(deps):]
        dom_ref, scr = rest[0], rest[-1]
        acc = _dot(dy_ref[...], w_ref[...], NT)
        dom_ref[...] = acc.astype(BF16)

        @pl.when(pl.program_id(1) == 0)
        def _():
            _to_lane_tiles(scr, acc)
            _emit_class_major(scr, rest[1:-1], tm)

    return pl.pallas_call(
        body, name="mm_out_bwd", grid=(T // tm, D // DA),
        in_specs=[pl.BlockSpec((tm, D), lambda i, j: (i, 0)),
                  pl.BlockSpec((DA, D), lambda i, j: (j, 0))] + [ANY_SPEC] * len(deps),
        out_specs=[pl.BlockSpec((tm, DA), lambda i, j: (i, j))]
        + [pl.BlockSpec((d, tm // d, DA), lambda i, j: (0, i, 0)) for d in WIDE],
        out_shape=[jax.ShapeDtypeStruct((T, D), BF16)]
        + [jax.ShapeDtypeStruct((d, T // d, DA), BF16) for d in WIDE],
        scratch_shapes=[pltpu.VMEM((DA // LANES, tm, LANES), F32)],
        compiler_params=pltpu.CompilerParams(
            dimension_semantics=("parallel", "arbitrary"), vmem_limit_bytes=VMEM_BIG),
    )(dh1b, w_out, *deps)


def _mm_out(o_attn, o_conv, w_out, x):
    T = x.shape[0]
    tm = _row_tile(T)
    tn = 1024
    return _mm(
        "mm_out", (T // tm, D // tn, 1),
        [pl.BlockSpec((tm, DA), lambda i, j, k: (i, 0)),
         pl.BlockSpec((DA, tn), lambda i, j, k: (0, j)),
         pl.BlockSpec((tm, DC), lambda i, j, k: (i, 0)),
         pl.BlockSpec((DC, tn), lambda i, j, k: (1, j)),
         pl.BlockSpec((tm, tn), lambda i, j, k: (i, j))],
        [o_attn, w_out, o_conv, w_out, x],
        [pl.BlockSpec((tm, tn), lambda i, j, k: (i, j))],
        [jax.ShapeDtypeStruct((T, D), F32)], NN, 2, _ep_resid)[0]


FF_TN = 512
FF_TK = 2 * N_FF


def _mm_gate_up(f, wg_t, wu_t):
    T = f.shape[0]
    tm = _row_tile(T)

    def body(f_ref, wg_ref, wu_ref, dg_ref, du_ref, a_ref):
        fv = f_ref[...]
        g = _dot(fv, wg_ref[...], NT)
        u = _dot(fv, wu_ref[...], NT)
        sg = _sigmoid(g)
        silu = g * sg
        dg_ref[...] = (u * (sg * (1.0 + g * (1.0 - sg)))).astype(BF16)
        du_ref[...] = silu.astype(BF16)
        a_ref[...] = (silu * u).astype(BF16)

    wspec = pl.BlockSpec((FF_TN, D), lambda i, j: (j, 0))
    ospec = pl.BlockSpec((tm, FF_TN), lambda i, j: (i, j))
    sh = jax.ShapeDtypeStruct((T, DFF), BF16)
    return pl.pallas_call(
        body, name="mm_gate_up", grid=(T // tm, DFF // FF_TN),
        in_specs=[pl.BlockSpec((tm, D), lambda i, j: (i, 0)), wspec, wspec],
        out_specs=[ospec, ospec, ospec], out_shape=[sh, sh, sh],
        compiler_params=pltpu.CompilerParams(
            dimension_semantics=("parallel", "parallel"), vmem_limit_bytes=VMEM_BIG),
    )(f, wg_t, wu_t)


def _mm_down(act, w_down, h1):
    T = h1.shape[0]
    tm = _row_tile(T)
    tn = 1024
    return _mm(
        "mm_down", (T // tm, D // tn, DFF // FF_TK),
        [pl.BlockSpec((tm, FF_TK), lambda i, j, k: (i, k)),
         pl.BlockSpec((FF_TK, tn), lambda i, j, k: (k, j)),
         pl.BlockSpec((tm, tn), lambda i, j, k: (i, j))],
        [act, w_down, h1],
        [pl.BlockSpec((tm, tn), lambda i, j, k: (i, j))],
        [jax.ShapeDtypeStruct((T, D), F32)], NN, 1, _ep_resid, acc_shape=(tm, tn))[0]


def _mm_ple(r, w_pgate, b_pgate, p, w_ple, h2):
    T = h2.shape[0]
    tm = _row_tile(T)
    tn = 1024

    def body(r_ref, wg_ref, b_ref, p_ref, wp_ref, h2_ref, gte_ref, pe_ref, h3_ref):
        gte = _sigmoid(_dot(r_ref[...], wg_ref[...], NN) + b_ref[...])
        pe = _dot(p_ref[...], wp_ref[...], NN)
        gte_ref[...] = gte.astype(BF16)
        pe_ref[...] = pe.astype(BF16)
        h3_ref[...] = h2_ref[...] + pe * gte

    tile = pl.BlockSpec((tm, tn), lambda i, j: (i, j))
    return pl.pallas_call(
        body, name="mm_ple", grid=(T // tm, D // tn),
        in_specs=[pl.BlockSpec((tm, D), lambda i, j: (i, 0)),
                  pl.BlockSpec((D, tn), lambda i, j: (0, j)),
                  pl.BlockSpec((1, tn), lambda i, j: (0, j)),
                  pl.BlockSpec((tm, DPLE), lambda i, j: (i, 0)),
                  pl.BlockSpec((DPLE, tn), lambda i, j: (0, j)),
                  tile],
        out_specs=[tile, tile, tile],
        out_shape=[jax.ShapeDtypeStruct((T, D), BF16), jax.ShapeDtypeStruct((T, D), BF16),
                   jax.ShapeDtypeStruct((T, D), F32)],
        compiler_params=pltpu.CompilerParams(
            dimension_semantics=("parallel", "parallel"), vmem_limit_bytes=VMEM_BIG),
    )(r, w_pgate, b_pgate, p, w_ple, h2)


def _mm_nt(name, dy, w, deps=()):
    T, n = dy.shape
    kdim = w.shape[0]
    tm = _row_tile(T)
    tn = 1024
    return _mm(
        name, (T // tm, kdim // tn, 1),
        [pl.BlockSpec((tm, n), lambda i, j, k: (i, 0)),
         pl.BlockSpec((tn, n), lambda i, j, k: (j, 0))],
        [dy, w],
        [pl.BlockSpec((tm, tn), lambda i, j, k: (i, j))],
        [jax.ShapeDtypeStruct((T, kdim), BF16)], NT, 1, _ep_cast(BF16), deps=deps)[0]


def _mm_down_bwd(dh2, w_down, g, u, deps=()):
    T = dh2.shape[0]
    tm = _row_tile(T)
    gspec = pl.BlockSpec((tm, FF_TN), lambda i, j, k: (i, j))
    sh = jax.ShapeDtypeStruct((T, DFF), BF16)
    return _mm(
        "mm_down_bwd", (T // tm, DFF // FF_TN, 1),
        [pl.BlockSpec((tm, D), lambda i, j, k: (i, 0)),
         pl.BlockSpec((FF_TN, D), lambda i, j, k: (j, 0)),
         gspec, gspec],
        [dh2, w_down, g, u],
        [gspec, gspec], [sh, sh], NT, 1, _ep_swiglu_bwd, deps=deps)


def _mm_ffn_in_bwd(dg, wg_t, du, wu_t, deps=()):
    T = dg.shape[0]
    tm = _row_tile(T)
    tn = 1024
    aspec = pl.BlockSpec((tm, FF_TK), lambda i, j, k: (i, k))
    wspec = pl.BlockSpec((FF_TK, tn), lambda i, j, k: (k, j))
    return _mm(
        "mm_ffn_in_bwd", (T // tm, D // tn, DFF // FF_TK),
        [aspec, wspec, aspec, wspec], [dg, wg_t, du, wu_t],
        [pl.BlockSpec((tm, tn), lambda i, j, k: (i, j))],
        [jax.ShapeDtypeStruct((T, D), BF16)], NN, 2, _ep_cast(BF16), acc_shape=(tm, tn),
        deps=deps)[0]


def _mm_in_bwd(dz, w_in, deps=()):
    T = dz.shape[0]
    tm = _row_tile(T)
    tn = 1024
    sg = 4
    return _mm(
        "mm_in_bwd", (T // tm, D // tn, NDEV // sg),
        [pl.BlockSpec((tm, sg * N_IN), lambda i, j, k: (i, k)),
         pl.BlockSpec((sg, tn, N_IN), lambda i, j, k: (k, j, 0))],
        [dz, w_in],
        [pl.BlockSpec((tm, tn), lambda i, j, k: (i, j))],
        [jax.ShapeDtypeStruct((T, D), BF16)], NT, 1, _ep_cast(BF16), acc_shape=(tm, tn),
        deps=deps, group=sg, a_cols=N_IN)[0]


def _mm_tn(name, a, b, tj=None):
    T, idim = a.shape
    jdim = b.shape[1]
    tt = _row_tile(T)
    ti = min(idim, 1024)
    tj = jdim if tj is None else tj
    return _mm(
        name, (idim // ti, jdim // tj, T // tt),
        [pl.BlockSpec((tt, ti), lambda i, j, k: (k, i)),
         pl.BlockSpec((tt, tj), lambda i, j, k: (k, j))],
        [a, b],
        [pl.BlockSpec((ti, tj), lambda i, j, k: (i, j))],
        [jax.ShapeDtypeStruct((idim, jdim), BF16)], TN, 1, _ep_cast(BF16), acc_shape=(ti, tj))[0]


def _mm_tn_cols(name, a, b, ncol):
    T, idim = a.shape
    tt = _tn_rows(T)
    return _mm(
        name, (1, NDEV, T // tt),
        [pl.BlockSpec((tt, idim), lambda i, j, k: (k, 0)),
         pl.BlockSpec((tt, ncol), lambda i, j, k: (k, j))],
        [a, b],
        [pl.BlockSpec((None, idim, ncol), lambda i, j, k: (j, 0, 0))],
        [jax.ShapeDtypeStruct((NDEV, idim, ncol), BF16)], TN, 1, _ep_cast(BF16),
        acc_shape=(idim, ncol))[0]


def _mm_tn_ff(name, a, b):
    T = b.shape[0]
    tt = _tn_rows(T)
    return _mm(
        name, (DFF // FF_TN, 1, T // tt),
        [pl.BlockSpec((tt, FF_TN), lambda i, j, k: (k, i)),
         pl.BlockSpec((tt, D), lambda i, j, k: (k, 0))],
        [a, b],
        [pl.BlockSpec((FF_TN, D), lambda i, j, k: (i, 0))],
        [jax.ShapeDtypeStruct((DFF, D), BF16)], TN, 1, _ep_cast(BF16),
        acc_shape=(FF_TN, D))[0]


TR = 256


def _rows(T):
    return min(TR, T)


def _rms_fwd(name, h, g, deps=()):
    T = h.shape[0]
    tr = _rows(T)

    def body(h_ref, g_ref, *rest):
        o_ref = rest[-1]
        v = h_ref[...]
        r = lax.rsqrt(jnp.mean(v * v, axis=-1, keepdims=True) + EPS)
        o_ref[...] = (v * r * g_ref[...]).astype(BF16)

    return pl.pallas_call(
        body, name=name, grid=(T // tr,),
        in_specs=[pl.BlockSpec((tr, D), lambda i: (i, 0)), pl.BlockSpec((1, D), lambda i: (0, 0))]
        + [ANY_SPEC] * len(deps),
        out_specs=pl.BlockSpec((tr, D), lambda i: (i, 0)),
        out_shape=jax.ShapeDtypeStruct((T, D), BF16),
        compiler_params=pltpu.CompilerParams(dimension_semantics=("parallel",)),
    )(h, g, *deps)


def _fold8(v):
    return jnp.sum(v.reshape(v.shape[0] // 8, 8, v.shape[1]), axis=0)


def _rms_bwd(name, dn_out, h, g, dres, want_bf16):
    T = h.shape[0]
    tr = _rows(T)
    nt = T // tr

    def body(dy_ref, h_ref, g_ref, dres_ref, *rest):
        if want_bf16:
            dh_ref, dhb_ref, dg_ref, acc = rest
        else:
            dh_ref, dg_ref, acc = rest
        i = pl.program_id(0)
        v = h_ref[...]
        r = lax.rsqrt(jnp.mean(v * v, axis=-1, keepdims=True) + EPS)
        nrm = v * r
        dy = dy_ref[...].astype(F32)
        dn = dy * g_ref[...]
        dh = dres_ref[...] + r * (dn - nrm * jnp.mean(dn * nrm, axis=-1, keepdims=True))
        dh_ref[...] = dh
        if want_bf16:
            dhb_ref[...] = dh.astype(BF16)

        @pl.when(i == 0)
        def _():
            acc[...] = jnp.zeros_like(acc)

        acc[...] += _fold8(dy * nrm)

        @pl.when(i == nt - 1)
        def _():
            dg_ref[...] = jnp.sum(acc[...], axis=0, keepdims=True)

    tile = pl.BlockSpec((tr, D), lambda i: (i, 0))
    vec = pl.BlockSpec((1, D), lambda i: (0, 0))
    out_specs = [tile] + ([tile] if want_bf16 else []) + [vec]
    out_shape = ([jax.ShapeDtypeStruct((T, D), F32)]
                 + ([jax.ShapeDtypeStruct((T, D), BF16)] if want_bf16 else [])
                 + [jax.ShapeDtypeStruct((1, D), F32)])
    return pl.pallas_call(
        body, name=name, grid=(nt,),
        in_specs=[tile, tile, vec, tile], out_specs=out_specs, out_shape=out_shape,
        scratch_shapes=[pltpu.VMEM((8, D), F32)],
        compiler_params=pltpu.CompilerParams(dimension_semantics=("arbitrary",)),
    )(dn_out, h, g, dres)


def _loss_bwd(h3, target, g_final, pe, gte):
    T = h3.shape[0]
    tr = _rows(T)
    nt = T // tr

    def body(h_ref, t_ref, g_ref, pe_ref, gte_ref, loss_ref, dh_ref, dpe_ref, dpg_ref,
             dgf_ref, dbp_ref, lacc, gacc, bacc):
        i = pl.program_id(0)
        v = h_ref[...]
        r = lax.rsqrt(jnp.mean(v * v, axis=-1, keepdims=True) + EPS)
        nrm = v * r
        g = g_ref[...]
        err = nrm * g - t_ref[...]
        dy = err * (1.0 / D)
        dn = dy * g
        dh = r * (dn - nrm * jnp.mean(dn * nrm, axis=-1, keepdims=True))
        dh_ref[...] = dh
        gte = gte_ref[...].astype(F32)
        pe = pe_ref[...].astype(F32)
        dpe_ref[...] = (dh * gte).astype(BF16)
        dpg = dh * pe * gte * (1.0 - gte)
        dpg_ref[...] = dpg.astype(BF16)

        @pl.when(i == 0)
        def _():
            lacc[...] = jnp.zeros_like(lacc)
            gacc[...] = jnp.zeros_like(gacc)
            bacc[...] = jnp.zeros_like(bacc)

        lacc[...] += _fold8(err * err)
        gacc[...] += _fold8(dy * nrm)
        bacc[...] += _fold8(dpg)

        @pl.when(i == nt - 1)
        def _():
            tot = jnp.sum(jnp.sum(lacc[...], axis=0, keepdims=True), axis=1, keepdims=True)
            loss_ref[...] = jnp.broadcast_to(tot * (0.5 / D), (1, 128))
            dgf_ref[...] = jnp.sum(gacc[...], axis=0, keepdims=True)
            dbp_ref[...] = jnp.sum(bacc[...], axis=0, keepdims=True)

    tile = pl.BlockSpec((tr, D), lambda i: (i, 0))
    vec = pl.BlockSpec((1, D), lambda i: (0, 0))
    return pl.pallas_call(
        body, name="loss_bwd", grid=(nt,),
        in_specs=[tile, tile, vec, tile, tile],
        out_specs=[pl.BlockSpec((1, 128), lambda i: (0, 0)), tile, tile, tile, vec, vec],
        out_shape=[jax.ShapeDtypeStruct((1, 128), F32), jax.ShapeDtypeStruct((T, D), F32),
                   jax.ShapeDtypeStruct((T, D), BF16), jax.ShapeDtypeStruct((T, D), BF16),
                   jax.ShapeDtypeStruct((1, D), F32), jax.ShapeDtypeStruct((1, D), F32)],
        scratch_shapes=[pltpu.VMEM((8, D), F32)] * 3,
        compiler_params=pltpu.CompilerParams(dimension_semantics=("arbitrary",)),
    )(h3, target, g_final, pe, gte)


def _band_masks():
    qi = lax.broadcasted_iota(jnp.int32, (BLK, BLK), 0)
    kj = lax.broadcasted_iota(jnp.int32, (BLK, BLK), 1)
    return kj >= qi, kj <= qi


AQ = 4


def _cm_spec(d, col, nblk, rowmap=lambda n: n):
    if d == 1:
        return pl.BlockSpec((nblk * BLK, DA), lambda r, n: (rowmap(n), col))
    return pl.BlockSpec((None, nblk * BLK, DA), lambda r, n: (r, rowmap(n), col))


def _cm_shape(d, T, dtype):
    return jax.ShapeDtypeStruct((T, DA) if d == 1 else (d, T // d, DA), dtype)


def _blk(b):
    return slice(b * BLK, (b + 1) * BLK)


HEADS = tuple(slice(h * DH, (h + 1) * DH) for h in range(NH))


def _attn_fwd(name, zsrc, d, T):
    nb = T // d // BLK
    aq = min(AQ, nb)
    scale = DH ** -0.5

    def body(q_ref, kp_ref, kc_ref, vp_ref, vc_ref, o_ref, l_ref):
        n = pl.program_id(1)
        band_prev, cur_ok = _band_masks()
        for b in range(aq):
            kp = (lambda sl: kp_ref[:, sl]) if b == 0 else (lambda sl, b=b: kc_ref[_blk(b - 1), sl])
            vp = (lambda sl: vp_ref[:, sl]) if b == 0 else (lambda sl, b=b: vc_ref[_blk(b - 1), sl])
            prev_ok = band_prev & (n > 0) if b == 0 else band_prev
            rows = _blk(b)
            s = [(jnp.where(prev_ok, _dot(q_ref[rows, sl], kp(sl), NT) * scale, NEG),
                  jnp.where(cur_ok, _dot(q_ref[rows, sl], kc_ref[rows, sl], NT) * scale, NEG))
                 for sl in HEADS]
            m = [jnp.maximum(jnp.max(sp, axis=1, keepdims=True), jnp.max(sc, axis=1, keepdims=True))
                 for sp, sc in s]
            p = [(jnp.exp(sp - mh), jnp.exp(sc - mh)) for (sp, sc), mh in zip(s, m)]
            den = [jnp.sum(pp, axis=1, keepdims=True) + jnp.sum(pc, axis=1, keepdims=True)
                   for pp, pc in p]
            o = [_dot(pp.astype(BF16), vp(sl), NN) + _dot(pc.astype(BF16), vc_ref[rows, sl], NN)
                 for (pp, pc), sl in zip(p, HEADS)]
            o_ref[rows, :] = jnp.concatenate(
                [(oh / dh).astype(BF16) for oh, dh in zip(o, den)], axis=1)
            l_ref[rows, :] = jnp.concatenate(
                [jnp.broadcast_to(mh + jnp.log(dh), (BLK, DH)) for mh, dh in zip(m, den)], axis=1)

    halo = lambda n: jnp.maximum(aq * n - 1, 0)
    return pl.pallas_call(
        body, name=name, grid=(d, nb // aq),
        in_specs=[_cm_spec(d, 0, aq), _cm_spec(d, 1, 1, halo), _cm_spec(d, 1, aq),
                  _cm_spec(d, 2, 1, halo), _cm_spec(d, 2, aq)],
        out_specs=[_cm_spec(d, 0, aq)] * 2,
        out_shape=[_cm_shape(d, T, BF16), _cm_shape(d, T, F32)],
        compiler_params=pltpu.CompilerParams(dimension_semantics=("parallel", "parallel")),
    )(zsrc, zsrc, zsrc, zsrc, zsrc)


def _cm_tile(d, tr):
    if d == 1:
        return pl.BlockSpec((tr, DA), lambda i: (i, 0))
    return pl.BlockSpec((d, tr // d, DA), lambda i: (0, i, 0))


def _attn_combine(outs, lses, T):
    tr = _rows(T)

    def body(*refs):
        o_in, l_in = refs[:3], refs[3:6]
        o_ref, l_ref = refs[6:8]
        o_cm, l_cm = refs[8:8 + len(WIDE)], refs[8 + len(WIDE):8 + 2 * len(WIDE)]
        so, sl, so_all, sl_all = refs[8 + 2 * len(WIDE):]
        for c in range(DA // LANES):
            lt = _lane_tile(c)
            os_, ls_ = [o_in[0][:, lt].astype(F32)], [l_in[0][:, lt]]
            for w, d in enumerate(WIDE):
                for r in range(d):
                    so[w, c, pl.ds(r, tr // d, stride=d), :] = o_in[1 + w][r, :, lt].astype(F32)
                    sl[w, c, pl.ds(r, tr // d, stride=d), :] = l_in[1 + w][r, :, lt]
                os_.append(so[w, c])
                ls_.append(sl[w, c])
            la, lb, lc = ls_
            m = jnp.maximum(jnp.maximum(la, lb), lc)
            ea, eb, ec = jnp.exp(la - m), jnp.exp(lb - m), jnp.exp(lc - m)
            s = ea + eb + ec
            o = (ea * os_[0] + eb * os_[1] + ec * os_[2]) / s
            lse = m + jnp.log(s)
            o_ref[:, lt] = o.astype(BF16)
            l_ref[:, lt] = lse
            so_all[c] = o
            sl_all[c] = lse
        _emit_class_major(so_all, o_cm, tr)
        _emit_class_major(sl_all, l_cm, tr)

    specs = [_cm_tile(d, tr) for d in DILATIONS]
    wide = [_cm_tile(d, tr) for d in WIDE]
    return pl.pallas_call(
        body, name="attn_combine", grid=(T // tr,),
        in_specs=specs + specs,
        out_specs=[specs[0], specs[0]] + wide + wide,
        out_shape=[_cm_shape(1, T, BF16), _cm_shape(1, T, F32)]
        + [_cm_shape(d, T, BF16) for d in WIDE] + [_cm_shape(d, T, F32) for d in WIDE],
        scratch_shapes=[pltpu.VMEM((len(WIDE), DA // LANES, tr, LANES), F32)] * 2
        + [pltpu.VMEM((DA // LANES, tr, LANES), F32)] * 2,
        compiler_params=pltpu.CompilerParams(
            dimension_semantics=("parallel",), vmem_limit_bytes=VMEM_MID),
    )(*outs, *lses)


def _attn_bwd_q(name, zsrc, dosrc, osrc, lsrc, d, T):
    nb = T // d // BLK
    aq = min(AQ, nb)
    scale = DH ** -0.5

    def body(q_ref, kp_ref, kc_ref, vp_ref, vc_ref, do_ref, o_ref, l_ref, dq_ref):
        n = pl.program_id(1)
        band_prev, cur_ok = _band_masks()
        for b in range(aq):
            kp = (lambda sl: kp_ref[:, sl]) if b == 0 else (lambda sl, b=b: kc_ref[_blk(b - 1), sl])
            vp = (lambda sl: vp_ref[:, sl]) if b == 0 else (lambda sl, b=b: vc_ref[_blk(b - 1), sl])
            prev_ok = band_prev & (n > 0) if b == 0 else band_prev
            rows = _blk(b)
            s = [(_dot(q_ref[rows, sl], kp(sl), NT), _dot(q_ref[rows, sl], kc_ref[rows, sl], NT))
                 for sl in HEADS]
            dp = [(_dot(do_ref[rows, sl], vp(sl), NT), _dot(do_ref[rows, sl], vc_ref[rows, sl], NT))
                  for sl in HEADS]
            delta = [jnp.sum(do_ref[rows, sl].astype(F32) * o_ref[rows, sl].astype(F32), axis=1,
                             keepdims=True) for sl in HEADS]
            p = [(jnp.exp(jnp.where(prev_ok, sp * scale - l_ref[rows, sl], NEG)),
                  jnp.exp(jnp.where(cur_ok, sc * scale - l_ref[rows, sl], NEG)))
                 for (sp, sc), sl in zip(s, HEADS)]
            ds = [((pp * (dpp - dl) * scale).astype(BF16), (pc * (dpc - dl) * scale).astype(BF16))
                  for (pp, pc), (dpp, dpc), dl in zip(p, dp, delta)]
            dq = [_dot(dsp, kp(sl), NN) + _dot(dsc, kc_ref[rows, sl], NN)
                  for (dsp, dsc), sl in zip(ds, HEADS)]
            dq_ref[rows, :] = jnp.concatenate([v.astype(BF16) for v in dq], axis=1)

    halo = lambda n: jnp.maximum(aq * n - 1, 0)
    own = _cm_spec(d, 0, aq)
    return pl.pallas_call(
        body, name=name, grid=(d, nb // aq),
        in_specs=[own, _cm_spec(d, 1, 1, halo), _cm_spec(d, 1, aq), _cm_spec(d, 2, 1, halo),
                  _cm_spec(d, 2, aq), own, own, own],
        out_specs=own, out_shape=_cm_shape(d, T, BF16),
        compiler_params=pltpu.CompilerParams(dimension_semantics=("parallel", "parallel")),
    )(zsrc, zsrc, zsrc, zsrc, zsrc, dosrc, osrc, lsrc)


def _attn_bwd_kv(name, zsrc, dosrc, osrc, lsrc, d, T):
    nb = T // d // BLK
    aq = min(AQ, nb)
    nsteps = nb // aq
    scale = DH ** -0.5

    def body(k_ref, v_ref, q_ref, qn_ref, do_ref, don_ref, o_ref, on_ref, l_ref, ln_ref,
             dk_ref, dv_ref):
        j = pl.program_id(1)
        band_next, own_ok = _band_masks()
        for b in range(aq):
            rows = _blk(b)
            last = b == aq - 1
            pick = lambda cur, halo: ((lambda sl: halo[:, sl]) if last
                                      else (lambda sl, b=b: cur[_blk(b + 1), sl]))
            qb, dob, ob, lb = (pick(q_ref, qn_ref), pick(do_ref, don_ref), pick(o_ref, on_ref),
                               pick(l_ref, ln_ref))
            next_ok = band_next & (j < nsteps - 1) if last else band_next
            s = [(_dot(q_ref[rows, sl], k_ref[rows, sl], NT), _dot(qb(sl), k_ref[rows, sl], NT))
                 for sl in HEADS]
            dp = [(_dot(do_ref[rows, sl], v_ref[rows, sl], NT), _dot(dob(sl), v_ref[rows, sl], NT))
                  for sl in HEADS]
            delta = [(jnp.sum(do_ref[rows, sl].astype(F32) * o_ref[rows, sl].astype(F32), axis=1,
                              keepdims=True),
                      jnp.sum(dob(sl).astype(F32) * ob(sl).astype(F32), axis=1, keepdims=True))
                     for sl in HEADS]
            p = [(jnp.exp(jnp.where(own_ok, sa * scale - l_ref[rows, sl], NEG)),
                  jnp.exp(jnp.where(next_ok, sb * scale - lb(sl), NEG)))
                 for (sa, sb), sl in zip(s, HEADS)]
            dv = [_dot(pa.astype(BF16), do_ref[rows, sl], TN) + _dot(pb.astype(BF16), dob(sl), TN)
                  for (pa, pb), sl in zip(p, HEADS)]
            ds = [((pa * (dpa - da) * scale).astype(BF16), (pb * (dpb - db) * scale).astype(BF16))
                  for (pa, pb), (dpa, dpb), (da, db) in zip(p, dp, delta)]
            dk = [_dot(dsa, q_ref[rows, sl], TN) + _dot(dsb, qb(sl), TN)
                  for (dsa, dsb), sl in zip(ds, HEADS)]
            dk_ref[rows, :] = jnp.concatenate([v.astype(BF16) for v in dk], axis=1)
            dv_ref[rows, :] = jnp.concatenate([v.astype(BF16) for v in dv], axis=1)

    halo = lambda j: jnp.minimum(aq * (j + 1), nb - 1)
    own, own_n = _cm_spec(d, 0, aq), _cm_spec(d, 0, 1, halo)
    sh = _cm_shape(d, T, BF16)
    return pl.pallas_call(
        body, name=name, grid=(d, nsteps),
        in_specs=[_cm_spec(d, 1, aq), _cm_spec(d, 2, aq), own, own_n, own, own_n, own, own_n,
                  own, own_n],
        out_specs=[own, own], out_shape=[sh, sh],
        compiler_params=pltpu.CompilerParams(dimension_semantics=("parallel", "parallel")),
    )(zsrc, zsrc, zsrc, zsrc, dosrc, dosrc, osrc, osrc, lsrc, lsrc)


def _dz_assemble(dqs, dks, dvs, dcvg, T):
    tr = _rows(T)
    nb = len(DILATIONS)

    def body(*refs):
        cvg_ref, dz_ref, scr = refs[3 * nb], refs[3 * nb + 1], refs[3 * nb + 2]
        for g in range(3):
            parts = refs[g * nb:(g + 1) * nb]
            for c in range(DA // LANES):
                lt = _lane_tile(c)
                scr[g, c] = parts[0][:, lt].astype(F32)
                for w, d in enumerate(WIDE):
                    for r in range(d):
                        rows = pl.ds(r, tr // d, stride=d)
                        scr[g, c, rows, :] = scr[g, c, rows, :] + parts[1 + w][r, :, lt].astype(F32)
                dz_ref[:, g * DA + c * LANES:g * DA + (c + 1) * LANES] = scr[g, c].astype(BF16)
        dz_ref[:, 3 * DA:] = cvg_ref[...]

    specs = [_cm_tile(d, tr) for d in DILATIONS]
    return pl.pallas_call(
        body, name="dz_assemble", grid=(T // tr,),
        in_specs=specs * 3 + [pl.BlockSpec((tr, 2 * DC), lambda i: (i, 0))],
        out_specs=pl.BlockSpec((tr, DIN), lambda i: (i, 0)),
        out_shape=jax.ShapeDtypeStruct((T, DIN), BF16),
        scratch_shapes=[pltpu.VMEM((3, DA // LANES, tr, LANES), F32)],
        compiler_params=pltpu.CompilerParams(
            dimension_semantics=("parallel",), vmem_limit_bytes=VMEM_MID),
    )(*dqs, *dks, *dvs, dcvg)


CT = 256
HALO = 32
RC = 32


def _conv_fwd(z, w_dw, b_dw, g_ln, b_ln):
    T = z.shape[0]
    ct = min(CT, T)
    nt = T // ct
    hb = ct // HALO

    def body(cv_ref, cg_ref, cvp_ref, cgp_ref, w_ref, bdw_ref, g_ref, b_ref, oc_ref, y_ref, ubuf, ush):
        i = pl.program_id(0)
        up = cvp_ref[...].astype(F32) * _sigmoid(cgp_ref[...].astype(F32))
        ubuf[0:HALO, :] = jnp.where(i > 0, up, 0.0)
        ubuf[HALO:, :] = cv_ref[...].astype(F32) * _sigmoid(cg_ref[...].astype(F32))
        for b in range(8):
            ush[b] = ubuf[pl.ds(8 - b, ct + 24), :]

        def chunk(ci, carry):
            r0 = pl.multiple_of(ci * RC, RC)
            acc = jnp.broadcast_to(bdw_ref[...], (RC, DC))
            for s in range(CW):
                a, b = divmod(s, 8)
                acc = acc + w_ref[CW - 1 - s:CW - s, :] * ush[b, pl.ds(r0 + 24 - 8 * a, RC), :]
            y_ref[pl.ds(r0, RC), :] = acc
            mu = jnp.mean(acc, axis=-1, keepdims=True)
            cen = acc - mu
            var = jnp.mean(cen * cen, axis=-1, keepdims=True)
            ln = cen * lax.rsqrt(var + EPS) * g_ref[...] + b_ref[...]
            oc_ref[pl.ds(r0, RC), :] = (ln * _sigmoid(ln)).astype(BF16)
            return carry

        lax.fori_loop(0, ct // RC, chunk, 0)

    cur = lambda col: pl.BlockSpec((ct, DC), lambda i: (i, col))
    prv = lambda col: pl.BlockSpec((HALO, DC), lambda i: (jnp.maximum(i * hb - 1, 0), col))
    vec = pl.BlockSpec((1, DC), lambda i: (0, 0))
    return pl.pallas_call(
        body, name="conv_fwd", grid=(nt,),
        in_specs=[cur(3), cur(4), prv(3), prv(4), pl.BlockSpec((CW, DC), lambda i: (0, 0)),
                  vec, vec, vec],
        out_specs=[pl.BlockSpec((ct, DC), lambda i: (i, 0))] * 2,
        out_shape=[jax.ShapeDtypeStruct((T, DC), BF16), jax.ShapeDtypeStruct((T, DC), F32)],
        scratch_shapes=[pltpu.VMEM((ct + HALO, DC), F32), pltpu.VMEM((8, ct + 24, DC), F32)],
        compiler_params=pltpu.CompilerParams(
            dimension_semantics=("parallel",), vmem_limit_bytes=VMEM_MID),
    )(z, z, z, z, w_dw, b_dw, g_ln, b_ln)


def _conv_bwd(z, dom, y, w_dw, g_ln, b_ln):
    T = z.shape[0]
    ct = min(CT, T)
    nt = T // ct
    hb = ct // HALO
    last_halo = T // HALO - 1

    def ln_bwd(yv, dov, g_ref, b_ref):
        mu = jnp.mean(yv, axis=-1, keepdims=True)
        cen = yv - mu
        rstd = lax.rsqrt(jnp.mean(cen * cen, axis=-1, keepdims=True) + EPS)
        xhat = cen * rstd
        ln = xhat * g_ref[...] + b_ref[...]
        sg = _sigmoid(ln)
        dln = dov * (sg * (1.0 + ln * (1.0 - sg)))
        dxh = dln * g_ref[...]
        dy = rstd * (dxh - jnp.mean(dxh, axis=-1, keepdims=True)
                     - xhat * jnp.mean(dxh * xhat, axis=-1, keepdims=True))
        return dy, dln, xhat

    def body(do_ref, don_ref, y_ref, yn_ref, cv_ref, cg_ref, cvp_ref, cgp_ref, w_ref, g_ref, b_ref,
             dcvg_ref, dw_ref, dbdw_ref, dg_ref, db_ref,
             dybuf, dysh, ubuf, ush, dwacc, vacc):
        i = pl.program_id(0)

        @pl.when(i == 0)
        def _():
            dwacc[...] = jnp.zeros_like(dwacc)
            vacc[...] = jnp.zeros_like(vacc)

        def ln_chunk(ci, carry):
            r0 = pl.multiple_of(ci * RC, RC)
            dy, dln, xhat = ln_bwd(y_ref[pl.ds(r0, RC), :], do_ref[pl.ds(r0, RC), :].astype(F32),
                                   g_ref, b_ref)
            dybuf[pl.ds(r0, RC), :] = dy
            vacc[0] += _fold8(dy)
            vacc[1] += _fold8(dln * xhat)
            vacc[2] += _fold8(dln)
            return carry

        lax.fori_loop(0, ct // RC, ln_chunk, 0)
        dyn, _, _ = ln_bwd(yn_ref[...], don_ref[...].astype(F32), g_ref, b_ref)
        dybuf[ct:, :] = jnp.where(i < nt - 1, dyn, 0.0)
        for b in range(8):
            dysh[b] = dybuf[pl.ds(b, ct + 24), :]

        up = cvp_ref[...].astype(F32) * _sigmoid(cgp_ref[...].astype(F32))
        ubuf[0:HALO, :] = jnp.where(i > 0, up, 0.0)
        ubuf[HALO:, :] = cv_ref[...].astype(F32) * _sigmoid(cg_ref[...].astype(F32))
        for b in range(8):
            ush[b] = ubuf[pl.ds(8 - b, ct + 24), :]

        def chunk(ci, carry):
            r0 = pl.multiple_of(ci * RC, RC)
            dy = dybuf[pl.ds(r0, RC), :]
            du = jnp.zeros((RC, DC), F32)
            for s in range(CW):
                a, b = divmod(s, 8)
                du = du + w_ref[CW - 1 - s:CW - s, :] * dysh[b, pl.ds(r0 + 8 * a, RC), :]
                dwacc[CW - 1 - s] += _fold8(dy * ush[b, pl.ds(r0 + 24 - 8 * a, RC), :])
            cv = cv_ref[pl.ds(r0, RC), :].astype(F32)
            sg = _sigmoid(cg_ref[pl.ds(r0, RC), :].astype(F32))
            dcvg_ref[pl.ds(r0, RC), 0:DC] = (du * sg).astype(BF16)
            dcvg_ref[pl.ds(r0, RC), DC:2 * DC] = (du * cv * sg * (1.0 - sg)).astype(BF16)
            return carry

        lax.fori_loop(0, ct // RC, chunk, 0)

        @pl.when(i == nt - 1)
        def _():
            dw_ref[...] = jnp.sum(dwacc[...], axis=1)
            dbdw_ref[...] = jnp.sum(vacc[0], axis=0, keepdims=True)
            dg_ref[...] = jnp.sum(vacc[1], axis=0, keepdims=True)
            db_ref[...] = jnp.sum(vacc[2], axis=0, keepdims=True)

    cur = lambda col: pl.BlockSpec((ct, DC), lambda i: (i, col))
    prv = lambda col: pl.BlockSpec((HALO, DC), lambda i: (jnp.maximum(i * hb - 1, 0), col))
    nxt = lambda col: pl.BlockSpec((HALO, DC), lambda i: (jnp.minimum((i + 1) * hb, last_halo), col))
    vec = pl.BlockSpec((1, DC), lambda i: (0, 0))
    tile = pl.BlockSpec((ct, DC), lambda i: (i, 0))
    return pl.pallas_call(
        body, name="conv_bwd", grid=(nt,),
        in_specs=[cur(1), nxt(1), cur(0), nxt(0), cur(3), cur(4), prv(3), prv(4),
                  pl.BlockSpec((CW, DC), lambda i: (0, 0)), vec, vec],
        out_specs=[pl.BlockSpec((ct, 2 * DC), lambda i: (i, 0)),
                   pl.BlockSpec((CW, DC), lambda i: (0, 0)), vec, vec, vec],
        out_shape=[jax.ShapeDtypeStruct((T, 2 * DC), BF16),
                   jax.ShapeDtypeStruct((CW, DC), F32), jax.ShapeDtypeStruct((1, DC), F32),
                   jax.ShapeDtypeStruct((1, DC), F32), jax.ShapeDtypeStruct((1, DC), F32)],
        scratch_shapes=[pltpu.VMEM((ct + HALO, DC), F32), pltpu.VMEM((8, ct + 24, DC), F32),
                        pltpu.VMEM((ct + HALO, DC), F32), pltpu.VMEM((8, ct + 24, DC), F32),
                        pltpu.VMEM((CW, 8, DC), F32), pltpu.VMEM((3, 8, DC), F32)],
        compiler_params=pltpu.CompilerParams(
            dimension_semantics=("arbitrary",), vmem_limit_bytes=VMEM_BIG),
    )(dom, dom, y, y, z, z, z, z, w_dw, g_ln, b_ln)


def _adam_math(w, g, m, v):
    m = ADAM_B1 * m + (1.0 - ADAM_B1) * g
    v = ADAM_B2 * v + (1.0 - ADAM_B2) * (g * g)
    m_hat = m / (1.0 - ADAM_B1 ** ADAM_STEP)
    v_hat = v / (1.0 - ADAM_B2 ** ADAM_STEP)
    delta = -ADAM_LR * (m_hat / (jnp.sqrt(v_hat) + ADAM_EPS) + ADAM_WD * w)
    return delta, m, v


def _adam(name, slots, w, m, v):
    rows, cols = w.shape
    tr = next(t for t in (256, 176, 128, 64, 32, 16, 8, rows) if rows % t == 0)

    def body(s_ref, w_ref, m_ref, v_ref, g_out, d_out, m_out, v_out):
        g = s_ref[0].astype(F32)
        for s in range(1, NDEV):
            g = g + s_ref[s].astype(F32)
        delta, mn, vn = _adam_math(w_ref[...], g, m_ref[...], v_ref[...])
        g_out[...] = g
        d_out[...] = delta
        m_out[...] = mn
        v_out[...] = vn

    tile = pl.BlockSpec((tr, cols), lambda i: (i, 0))
    sh = jax.ShapeDtypeStruct((rows, cols), F32)
    return pl.pallas_call(
        body, name=name, grid=(rows // tr,),
        in_specs=[pl.BlockSpec((NDEV, tr, cols), lambda i: (0, i, 0)), tile, tile, tile],
        out_specs=[tile] * 4, out_shape=[sh] * 4,
        compiler_params=pltpu.CompilerParams(
            dimension_semantics=("parallel",), vmem_limit_bytes=VMEM_MID),
    )(slots, w, m, v)


SMALL_NAMES = ("g_mix", "b_dw", "g_conv_ln", "b_conv_ln", "g_ffn", "g_ple", "b_pgate", "g_final")


def _pack_small(vecs, w_dw_full):
    rows = [jnp.pad(v.reshape(1, -1), ((0, 0), (0, SMALL_W - v.size))) for v in vecs]
    rows.append(jnp.pad(w_dw_full, ((0, 0), (0, SMALL_W - DC))))
    rows.append(jnp.zeros((SMALL_ROWS - len(vecs) - CW, SMALL_W), F32))
    return jnp.concatenate(rows, axis=0)


def kernel(x, p, g_mix, w_in, w_dw, b_dw, g_conv_ln, b_conv_ln, w_out, g_ffn, w_gate, w_up, w_down, g_ple, w_pgate, b_pgate, w_ple, g_final, loss_target, m_g_mix, m_w_in, m_w_dw, m_b_dw, m_g_conv_ln, m_b_conv_ln, m_w_out, m_g_ffn, m_w_gate, m_w_up, m_w_down, m_g_ple, m_w_pgate, m_b_pgate, m_w_ple, m_g_final, v_g_mix, v_w_in, v_w_dw, v_b_dw, v_g_conv_ln, v_b_conv_ln, v_w_out, v_g_ffn, v_w_gate, v_w_up, v_w_down, v_g_ple, v_w_pgate, v_b_pgate, v_w_ple, v_g_final):
    T = x.shape[1]
    me = 4 * lax.axis_index("x") + 2 * lax.axis_index("y") + lax.axis_index("c")
    xs = x.reshape(T, D)
    ps = p.reshape(T, DPLE).astype(BF16)
    tgt = loss_target.reshape(T, D)
    g_final2 = g_final.reshape(1, D)

    tr_names = ("w_gate", "w_up")
    big = dict(w_in=w_in[0], w_out=w_out[0], w_gate=w_gate[0].T, w_up=w_up[0].T, w_down=w_down[0],
               w_pgate=w_pgate[0], w_ple=w_ple[0])
    order = ("w_in", "w_out", "w_gate", "w_up", "w_down", "w_pgate", "w_ple")
    g_order = ("w_dw",) + order
    g_items = [(w_dw.reshape(CW, DC // NDEV), False)] + [(big[n].astype(BF16), False) for n in order]
    g_handles, g_token = _xstart("gather_start", g_items, _place("gather_place", g_items))
    G = dict(zip(g_order, g_handles))

    a = _rms_fwd("rms_mix", xs, g_mix, deps=[g_token])
    w_dw_f = _xwait("gather_wait_w_dw", G["w_dw"], a).transpose(1, 0, 2).reshape(CW, DC)
    w_in_f = _xwait("gather_wait_w_in", G["w_in"], a)
    z, *z_wide = _mm_in(a, w_in_f)
    zsrc = dict(zip(DILATIONS, [z] + z_wide))
    br = [_attn_fwd(f"attn_fwd_d{d}", zsrc[d], d, T) for d in DILATIONS]
    comb = list(_attn_combine([b[0] for b in br], [b[1] for b in br], T))
    o_attn, lse = comb[0], comb[1]
    osrc = dict(zip(DILATIONS, [o_attn] + comb[2:2 + len(WIDE)]))
    lsrc = dict(zip(DILATIONS, [lse] + comb[2 + len(WIDE):]))
    o_conv, y_conv = _conv_fwd(z, w_dw_f, b_dw, g_conv_ln, b_conv_ln)
    w_out_f = _xwait("gather_wait_w_out", G["w_out"], o_conv).reshape(D, D)
    h1 = _mm_out(o_attn, o_conv, w_out_f, xs)
    f = _rms_fwd("rms_ffn", h1, g_ffn)
    w_gate_f = _xwait("gather_wait_w_gate", G["w_gate"], f).reshape(DFF, D)
    w_up_f = _xwait("gather_wait_w_up", G["w_up"], f).reshape(DFF, D)
    gate, up, act = _mm_gate_up(f, w_gate_f, w_up_f)
    w_down_f = _xwait("gather_wait_w_down", G["w_down"], act).reshape(DFF, D)
    h2 = _mm_down(act, w_down_f, h1)
    r = _rms_fwd("rms_ple", h2, g_ple)
    w_pgate_f = _xwait("gather_wait_w_pgate", G["w_pgate"], r).reshape(D, D)
    w_ple_f = _xwait("gather_wait_w_ple", G["w_ple"], r).transpose(1, 0, 2).reshape(DPLE, D)
    gte, pe, h3 = _mm_ple(r, w_pgate_f, b_pgate, ps, w_ple_f, h2)

    loss_part, dh3, dpe, dpg, d_g_final, d_b_pgate = _loss_bwd(h3, tgt, g_final2, pe, gte)
    H = {}

    def send_grads(tag, named):
        items = [(v, True) for _, v in named]
        handles, token = _xstart(f"grads_start_{tag}", items, _place(f"grads_place_{tag}", items))
        H.update(zip([n for n, _ in named], handles))
        return token

    gw_pgate = _mm_tn("gw_pgate", r, dpg).reshape(NDEV, D // NDEV, D)
    gw_ple = _mm_tn("gw_ple", ps, dpe).reshape(DPLE, NDEV, D // NDEV).transpose(1, 0, 2)
    tok = send_grads("ple", [("w_pgate", gw_pgate), ("w_ple", gw_ple)])
    dr = _mm_nt("mm_pgate_bwd", dpg, w_pgate_f, deps=[tok])
    dh2, dh2b, d_g_ple = _rms_bwd("rms_ple_bwd", dr, h2, g_ple, dh3, True)
    ff_shards = lambda g: g.reshape(NDEV, N_FF, D)
    gw_down = ff_shards(_mm_tn_ff("gw_down", act, dh2b))
    tok = send_grads("down", [("w_down", gw_down)])
    dgate, dup = _mm_down_bwd(dh2b, w_down_f, gate, up, deps=[tok])
    gw_gate = ff_shards(_mm_tn_ff("gw_gate", dgate, f))
    gw_up = ff_shards(_mm_tn_ff("gw_up", dup, f))
    tok = send_grads("ffn", [("w_gate", gw_gate), ("w_up", gw_up)])
    df = _mm_ffn_in_bwd(dgate, w_gate_f, dup, w_up_f, deps=[tok])
    dh1, dh1b, d_g_ffn = _rms_bwd("rms_ffn_bwd", df, h1, g_ffn, dh2, True)
    gw_out = jnp.concatenate(
        [_mm_tn("gw_out_attn", o_attn, dh1b), _mm_tn("gw_out_conv", o_conv, dh1b)], axis=0)
    tok = send_grads("out", [("w_out", gw_out.reshape(NDEV, D // NDEV, D))])
    dom, *do_wide = _mm_out_bwd(dh1b, w_out_f, deps=[tok])
    dosrc = dict(zip(DILATIONS, [dom] + do_wide))
    dcvg, d_w_dw, d_b_dw, d_g_ln, d_b_ln = _conv_bwd(z, dom, y_conv, w_dw_f, g_conv_ln, b_conv_ln)
    dqs, dks, dvs = [], [], []
    for d in DILATIONS:
        dqs.append(_attn_bwd_q(f"attn_bwd_q_d{d}", zsrc[d], dosrc[d], osrc[d], lsrc[d], d, T))
        dk, dv = _attn_bwd_kv(f"attn_bwd_kv_d{d}", zsrc[d], dosrc[d], osrc[d], lsrc[d], d, T)
        dks.append(dk)
        dvs.append(dv)
    dz = _dz_assemble(dqs, dks, dvs, dcvg, T)
    gw_in = _mm_tn_cols("gw_in", a, dz, N_IN)
    tok = send_grads("in", [("w_in", gw_in)])
    da = _mm_in_bwd(dz, w_in_f, deps=[tok])
    grad_x, d_g_mix = _rms_bwd("rms_mix_bwd", da, xs, g_mix, dh1, False)

    small_part = _pack_small(
        [d_g_mix, d_b_dw, d_g_ln, d_b_ln, d_g_ffn, d_g_ple, d_b_pgate, d_g_final], d_w_dw)
    small_slots = _exchange("exchange_small_grads", [(small_part, False)])[0]
    S = {n: _xwait(f"grads_wait_{n}", H[n], small_slots)
         for n in ("w_pgate", "w_ple", "w_down", "w_gate", "w_up", "w_out", "w_in")}

    mom = dict(w_in=(m_w_in, v_w_in), w_out=(m_w_out, v_w_out), w_gate=(m_w_gate, v_w_gate),
               w_up=(m_w_up, v_w_up), w_down=(m_w_down, v_w_down), w_pgate=(m_w_pgate, v_w_pgate),
               w_ple=(m_w_ple, v_w_ple))
    upd = {}
    for n in order:
        m_n, v_n = mom[n][0][0], mom[n][1][0]
        if n in tr_names:
            res = _adam(f"adam_{n}", S[n], big[n], m_n.T, v_n.T)
            upd[n] = [t.T[None] for t in res]
        else:
            res = _adam(f"adam_{n}", S[n], big[n], m_n, v_n)
            upd[n] = [t[None] for t in res]

    def lanes(v):
        full = jnp.zeros((CW, NDEV, DC // NDEV), F32)
        full = lax.dynamic_update_slice(full, v.reshape(CW, 1, DC // NDEV), (0, me, 0))
        return full.reshape(CW, DC)

    small_w = _pack_small([g_mix, b_dw, g_conv_ln, b_conv_ln, g_ffn, g_ple, b_pgate, g_final2], lanes(w_dw))
    small_m = _pack_small([m_g_mix, m_b_dw, m_g_conv_ln, m_b_conv_ln, m_g_ffn, m_g_ple, m_b_pgate,
                           m_g_final.reshape(1, D)], lanes(m_w_dw))
    small_v = _pack_small([v_g_mix, v_b_dw, v_g_conv_ln, v_b_conv_ln, v_g_ffn, v_g_ple, v_b_pgate,
                           v_g_final.reshape(1, D)], lanes(v_w_dw))
    small_res = _adam("adam_small", small_slots, small_w, small_m, small_v)

    def unpack(t):
        out = {}
        widths = dict(g_mix=D, b_dw=DC, g_conv_ln=DC, b_conv_ln=DC, g_ffn=D, g_ple=D, b_pgate=D, g_final=D)
        for i, n in enumerate(SMALL_NAMES):
            out[n] = t[i:i + 1, :widths[n]]
        out["g_final"] = out["g_final"].reshape(D)
        taps = t[len(SMALL_NAMES):len(SMALL_NAMES) + CW, :DC].reshape(CW, NDEV, DC // NDEV)
        out["w_dw"] = lax.dynamic_slice(taps, (0, me, 0), (CW, 1, DC // NDEV))[None]
        return out

    small = [unpack(t) for t in small_res]

    loss = lax.psum(loss_part[0, 0], ("x", "y", "c"))
    names = ("g_mix", "w_in", "w_dw", "b_dw", "g_conv_ln", "b_conv_ln", "w_out", "g_ffn", "w_gate",
             "w_up", "w_down", "g_ple", "w_pgate", "b_pgate", "w_ple", "g_final")
    outs = [loss, grad_x.reshape(1, T, D)]
    for kind in range(4):
        for n in names:
            outs.append(upd[n][kind] if n in upd else small[kind][n])
    return tuple(outs)
```

```python
import jax
import jax.numpy as jnp
from jax import lax
from jax.experimental import pallas as pl
from jax.experimental.pallas import tpu as pltpu

F32 = jnp.float32
BF16 = jnp.bfloat16

NDEV = 8
D = 2048
NH = 8
DH = 128
DA = NH * DH
DC = D - DA
DIN = 3 * DA + 2 * DC
DFF = 5632
DPLE = 256
BLK = 128
DILATIONS = (1, 4, 16)
CW = 31
EPS = 1e-6
N_IN = DIN // NDEV
N_FF = DFF // NDEV
NEG = -1e30

ADAM_LR = 0.001
ADAM_B1 = 0.9
ADAM_B2 = 0.999
ADAM_EPS = 1e-08
ADAM_WD = 0.01
ADAM_STEP = 10

VMEM_CAP_V7X = 64 * 1024 * 1024
VMEM_BIG = VMEM_CAP_V7X - 12 * 1024 * 1024
VMEM_MID = 40 * 1024 * 1024

SMALL_W = 2048
SMALL_ROWS = 40


def _sigmoid(v):
    return 1.0 / (1.0 + jnp.exp(-v))


def _dot(a, b, contract):
    return lax.dot_general(a, b, (contract, ((), ())), preferred_element_type=F32)


NN = ((1,), (0,))
NT = ((1,), (1,))
TN = ((0,), (0,))


def _exchange(name, items):
    n = len(items)
    out_shape = [
        jax.ShapeDtypeStruct((NDEV,) + (a.shape[1:] if sc else a.shape), a.dtype)
        for a, sc in items
    ]
    scat = [sc for _, sc in items]

    def body(*refs):
        srcs = refs[:n]
        dsts = refs[n:2 * n]
        send_sems, recv_sems, loc_sems = refs[2 * n:]
        x = lax.axis_index("x")
        y = lax.axis_index("y")
        c = lax.axis_index("c")
        me = 4 * x + 2 * y + c

        local = []
        for i in range(n):
            src = srcs[i].at[me] if scat[i] else srcs[i]
            cp = pltpu.make_async_copy(src, dsts[i].at[me], loc_sems.at[i])
            cp.start()
            local.append(cp)

        remote = []
        for k in range(1, NDEV):
            px = (1 - x) if (k >> 2) & 1 else x
            py = (1 - y) if (k >> 1) & 1 else y
            pc = (1 - c) if k & 1 else c
            peer = 4 * px + 2 * py + pc
            for i in range(n):
                sem = i * (NDEV - 1) + k - 1
                src = srcs[i].at[peer] if scat[i] else srcs[i]
                send = pltpu.make_async_remote_copy(
                    src_ref=src, dst_ref=dsts[i].at[me],
                    send_sem=send_sems.at[sem], recv_sem=recv_sems.at[sem],
                    device_id=(px, py, pc), device_id_type=pl.DeviceIdType.MESH)
                send.start()
                recv = pltpu.make_async_remote_copy(
                    src_ref=src, dst_ref=dsts[i].at[peer],
                    send_sem=send_sems.at[sem], recv_sem=recv_sems.at[sem],
                    device_id=(px, py, pc), device_id_type=pl.DeviceIdType.MESH)
                remote.append((send, recv))
        for send, recv in remote:
            recv.wait_recv()
            send.wait_send()
        for cp in local:
            cp.wait()

    any_spec = pl.BlockSpec(memory_space=pl.ANY)
    return pl.pallas_call(
        body, name=name,
        in_specs=[any_spec] * n, out_specs=[any_spec] * n, out_shape=out_shape,
        scratch_shapes=[
            pltpu.SemaphoreType.DMA((n * (NDEV - 1),)),
            pltpu.SemaphoreType.DMA((n * (NDEV - 1),)),
            pltpu.SemaphoreType.DMA((n,)),
        ],
    )(*[a for a, _ in items])


def _gather_two_level(name, arrays):
    n = len(arrays)
    per = NDEV - 1

    def body(*refs):
        srcs = refs[:n]
        dsts = refs[n:2 * n]
        send_sems, recv_sems, loc_sems = refs[2 * n:]
        x = lax.axis_index("x")
        y = lax.axis_index("y")
        c = lax.axis_index("c")
        idx = lambda px, py, pc: 4 * px + 2 * py + pc
        me, sibling = (x, y, c), (x, y, 1 - c)
        chips = [(1 - x, y), (x, 1 - y), (1 - x, 1 - y)]

        def copy(i, k, block, to, src=None):
            slot = dsts[i].at[idx(*block)]
            return pltpu.make_async_remote_copy(
                src_ref=slot if src is None else src, dst_ref=slot,
                send_sem=send_sems.at[i * per + k], recv_sem=recv_sems.at[i * per + k],
                device_id=to, device_id_type=pl.DeviceIdType.MESH)

        mine, sent = [], []
        for i in range(n):
            cp = pltpu.make_async_copy(srcs[i], dsts[i].at[idx(*me)], loc_sems.at[i])
            cp.start()
            mine.append(cp)
            first = [copy(i, 0, me, sibling, src=srcs[i])]
            first += [copy(i, 1 + j, me, (*chip, c), src=srcs[i]) for j, chip in enumerate(chips)]
            for cp in first:
                cp.start()
            sent += first
        for j, chip in enumerate(chips):
            for i in range(n):
                copy(i, 1 + j, (*chip, c), me).wait_recv()
                fwd = copy(i, 4 + j, (*chip, c), sibling)
                fwd.start()
                sent.append(fwd)
        for i in range(n):
            copy(i, 0, sibling, me).wait_recv()
            for j, chip in enumerate(chips):
                copy(i, 4 + j, (*chip, 1 - c), me).wait_recv()
        for cp in sent:
            cp.wait_send()
        for cp in mine:
            cp.wait()

    any_spec = pl.BlockSpec(memory_space=pl.ANY)
    return pl.pallas_call(
        body, name=name, in_specs=[any_spec] * n, out_specs=[any_spec] * n,
        out_shape=[jax.ShapeDtypeStruct((NDEV,) + a.shape, a.dtype) for a in arrays],
        scratch_shapes=[pltpu.SemaphoreType.DMA((n * per,)), pltpu.SemaphoreType.DMA((n * per,)),
                        pltpu.SemaphoreType.DMA((n,))],
    )(*arrays)


HBM_SPEC = pl.BlockSpec(memory_space=pltpu.HBM)
SEM_SPEC = pl.BlockSpec(memory_space=pltpu.SEMAPHORE)
ANY_SPEC = pl.BlockSpec(memory_space=pl.ANY)
EFFECT = pltpu.SideEffectType.DATAFLOW_SIDE_EFFECTING


def _peer_of(k):
    x = lax.axis_index("x")
    y = lax.axis_index("y")
    c = lax.axis_index("c")
    px = (1 - x) if (k >> 2) & 1 else x
    py = (1 - y) if (k >> 1) & 1 else y
    pc = (1 - c) if k & 1 else c
    return (px, py, pc), 4 * px + 2 * py + pc


def _my_index():
    return 4 * lax.axis_index("x") + 2 * lax.axis_index("y") + lax.axis_index("c")


def _slot_shape(a, sc):
    return (NDEV,) + (a.shape[1:] if sc else a.shape)


def _divisor_tile(rows):
    return next((t for t in (512, 256, 176, 128, 64, 32, 16) if rows % t == 0), rows)


def _place(name, items, dtype=None):
    lands = []
    for idx, (a, sc) in enumerate(items):
        rows, cols = a.shape[-2:]
        tr = _divisor_tile(rows)
        out_dtype = a.dtype if dtype is None else dtype

        def body(s_ref, o_ref):
            o_ref[...] = s_ref[...].astype(o_ref.dtype)

        mine = pl.BlockSpec((None, tr, cols), lambda i: (_my_index(), i, 0))
        lands.append(pl.pallas_call(
            body, name=f"{name}_{idx}", grid=(rows // tr,),
            in_specs=[mine if sc else pl.BlockSpec((tr, cols), lambda i: (i, 0))],
            out_specs=mine,
            out_shape=jax.ShapeDtypeStruct(_slot_shape(a, sc), out_dtype),
            compiler_params=pltpu.CompilerParams(dimension_semantics=("parallel",)),
        )(a))
    return lands


def _xstart(name, items, lands, deps=()):
    n = len(items)
    scat = [sc for _, sc in items]
    srcs_in = [a for a, sc in items if sc]
    n_src = len(srcs_in)
    src_pos = {i: p for p, i in enumerate(i for i in range(n) if scat[i])}

    def body(*refs):
        srcs = refs[:n_src]
        lzs = refs[n_src:n_src + n]
        outs = refs[n_src + n + len(deps):]
        send_sems, recv_sems, token = outs[:n], outs[n:2 * n], outs[-1]
        me = _my_index()
        for i in range(n):
            for k in range(1, NDEV):
                peer_id, peer = _peer_of(k)
                src = srcs[src_pos[i]].at[peer] if scat[i] else lzs[i].at[me]
                pltpu.make_async_remote_copy(
                    src_ref=src, dst_ref=lzs[i].at[me],
                    send_sem=send_sems[i].at[k - 1], recv_sem=recv_sems[i].at[k - 1],
                    device_id=peer_id, device_id_type=pl.DeviceIdType.MESH).start()
        token[...] = jnp.zeros_like(token)

    sem = pltpu.SemaphoreType.DMA((NDEV - 1,))
    thru = srcs_in + list(lands)
    res = pl.pallas_call(
        body, name=name,
        in_specs=[HBM_SPEC] * len(thru) + [ANY_SPEC] * len(deps),
        out_specs=[SEM_SPEC] * (2 * n) + [HBM_SPEC] * len(thru) + [pl.BlockSpec(memory_space=pltpu.VMEM)],
        out_shape=[sem] * (2 * n) + [pltpu.HBM(t.shape, t.dtype) for t in thru]
        + [jax.ShapeDtypeStruct((8, 128), F32)],
        input_output_aliases={i: 2 * n + i for i in range(len(thru))},
        compiler_params=pltpu.CompilerParams(has_side_effects=EFFECT),
    )(*[pltpu.with_memory_space_constraint(t, pltpu.HBM) for t in thru], *deps)
    handles = [(res[i], res[n + i], res[2 * n + src_pos[i]] if scat[i] else None,
                res[2 * n + n_src + i]) for i in range(n)]
    return handles, res[-1]


def _xwait(name, handle, after):
    send_sem, recv_sem, src, land = handle
    sc = src is not None

    def body(*refs):
        land_ref = refs[1] if sc else refs[0]
        send_ref, recv_ref = (refs[2], refs[3]) if sc else (refs[1], refs[2])
        me = _my_index()
        for k in range(1, NDEV):
            peer_id, peer = _peer_of(k)
            cp = pltpu.make_async_remote_copy(
                src_ref=refs[0].at[peer] if sc else land_ref.at[me], dst_ref=land_ref.at[peer],
                send_sem=send_ref.at[k - 1], recv_sem=recv_ref.at[k - 1],
                device_id=peer_id, device_id_type=pl.DeviceIdType.MESH)
            cp.wait_send()
            cp.wait_recv()

    thru = ([src] if sc else []) + [land]
    return pl.pallas_call(
        body, name=name,
        in_specs=[HBM_SPEC] * len(thru) + [SEM_SPEC, SEM_SPEC, ANY_SPEC],
        out_specs=[HBM_SPEC] * len(thru),
        out_shape=[pltpu.HBM(t.shape, t.dtype) for t in thru],
        input_output_aliases={i: i for i in range(len(thru))},
        compiler_params=pltpu.CompilerParams(has_side_effects=EFFECT),
    )(*thru, send_sem, recv_sem, after)[-1]


def _mm(name, grid, in_specs, operands, out_specs, out_shape, contract, n_pairs, epilogue,
        acc_shape=None, vmem=VMEM_BIG, deps=(), group=1, a_cols=None):
    nk = grid[2]

    def shard(ref, s, is_a):
        if group == 1:
            return ref[...]
        if is_a and a_cols is not None:
            return ref[:, s * a_cols:(s + 1) * a_cols]
        return ref[s]
    n_extra = len(operands) - 2 * n_pairs
    n_out = len(out_shape)
    n_in = len(operands) + len(deps)
    in_specs = list(in_specs) + [ANY_SPEC] * len(deps)
    operands = list(operands) + list(deps)

    def body(*refs):
        ab = refs[:2 * n_pairs]
        extras = refs[2 * n_pairs:2 * n_pairs + n_extra]
        outs = refs[n_in:n_in + n_out]
        dots = [(ab[2 * p], ab[2 * p + 1], s) for p in range(n_pairs) for s in range(group)]
        if nk == 1:
            part = None
            for a_ref, b_ref, s in dots:
                d = _dot(shard(a_ref, s, True), shard(b_ref, s, False), contract)
                part = d if part is None else part + d
            epilogue(part, extras, outs)
        else:
            acc_ref = refs[-1]
            k = pl.program_id(2)

            @pl.when(k == 0)
            def _():
                acc_ref[...] = jnp.zeros_like(acc_ref)

            for a_ref, b_ref, s in dots:
                acc_ref[...] += _dot(shard(a_ref, s, True), shard(b_ref, s, False), contract)

            @pl.when(k == nk - 1)
            def _():
                epilogue(acc_ref[...], extras, outs)

    scratch = [pltpu.VMEM(acc_shape, F32)] if nk > 1 else []
    return pl.pallas_call(
        body, name=name, grid=grid, in_specs=in_specs, out_specs=out_specs, out_shape=out_shape,
        scratch_shapes=scratch,
        compiler_params=pltpu.CompilerParams(
            dimension_semantics=("parallel", "parallel", "arbitrary"), vmem_limit_bytes=vmem),
    )(*operands)


def _ep_cast(dtype):
    def ep(acc, extras, outs):
        outs[0][...] = acc.astype(dtype)
    return ep


def _ep_resid_norm(acc, extras, outs):
    h = extras[0][...] + acc
    outs[0][...] = h
    r = lax.rsqrt(jnp.mean(h * h, axis=-1, keepdims=True) + EPS)
    outs[1][...] = (h * r * extras[1][...]).astype(BF16)


def _ep_swiglu_bwd(acc, extras, outs):
    outs[0][...] = (acc * extras[0][...].astype(F32)).astype(BF16)
    outs[1][...] = (acc * extras[1][...].astype(F32)).astype(BF16)


MXU_COLS_V7X = 256


def _col_chunks(n):
    return [slice(c, min(c + MXU_COLS_V7X, n)) for c in range(0, n, MXU_COLS_V7X)]


def _row_tile(T):
    return min(1024, T)


def _tn_rows(T):
    return min(2048, T)


WIDE = tuple(d for d in DILATIONS if d > 1)


LANES = 128


def _lane_tile(c):
    return slice(c * LANES, (c + 1) * LANES)


def _to_lane_tiles(scr, val):
    for c in range(scr.shape[0]):
        scr[c] = val[:, _lane_tile(c)]


def _emit_class_major(scr, refs, rows):
    for d, ref in zip(WIDE, refs):
        for r in range(d):
            for c in range(scr.shape[0]):
                ref[r, :, _lane_tile(c)] = scr[c, pl.ds(r, rows // d, stride=d), :].astype(ref.dtype)


def _mm_in(x, g, w_in, deps=()):
    T = x.shape[0]
    tm = _row_tile(T)
    nq = -(-3 * DA // N_IN)

    def body(x_ref, g_ref, w_ref, *rest):
        a_ref, z_ref, *rest = rest[len(deps):]
        scr = rest[-1]
        j = pl.program_id(1)

        @pl.when(j == 0)
        def _():
            v = x_ref[...]
            r = lax.rsqrt(jnp.mean(v * v, axis=-1, keepdims=True) + EPS)
            a_ref[...] = (v * r * g_ref[...]).astype(BF16)

        @pl.when(j >= nq)
        def _():
            z_ref[...] = _dot(a_ref[...], w_ref[...], NN).astype(BF16)

        @pl.when(j < nq)
        def _():
            av = a_ref[...]
            chunks = _col_chunks(N_IN)
            pending = _dot(av, w_ref[:, chunks[0]], NN)
            for ci, cols in enumerate(chunks):
                nxt = _dot(av, w_ref[:, chunks[ci + 1]], NN) if ci + 1 < len(chunks) else None
                z_ref[:, cols] = pending.astype(BF16)
                for c in range(cols.start // LANES, cols.stop // LANES):
                    scr[c] = pending[:, c * LANES - cols.start:(c + 1) * LANES - cols.start]
                    for d, ref in zip(WIDE, rest[:-1]):
                        for r in range(d):
                            ref[r, :, _lane_tile(c)] = scr[c, pl.ds(r, tm // d, stride=d), :].astype(BF16)
                pending = nxt

    cm_spec = lambda d: pl.BlockSpec((d, tm // d, N_IN), lambda i, j: (0, i, jnp.minimum(j, nq - 1)))
    row = pl.BlockSpec((tm, D), lambda i, j: (i, 0))
    return pl.pallas_call(
        body, name="mm_in", grid=(T // tm, NDEV),
        in_specs=[row, pl.BlockSpec((1, D), lambda i, j: (0, 0)),
                  pl.BlockSpec((None, D, N_IN), lambda i, j: (j, 0, 0))] + [ANY_SPEC] * len(deps),
        out_specs=[row, pl.BlockSpec((tm, N_IN), lambda i, j: (i, j))] + [cm_spec(d) for d in WIDE],
        out_shape=[jax.ShapeDtypeStruct((T, D), BF16), jax.ShapeDtypeStruct((T, DIN), BF16)]
        + [jax.ShapeDtypeStruct((d, T // d, nq * N_IN), BF16) for d in WIDE],
        scratch_shapes=[pltpu.VMEM((N_IN // LANES, tm, LANES), F32)],
        compiler_params=pltpu.CompilerParams(
            dimension_semantics=("parallel", "arbitrary"), vmem_limit_bytes=VMEM_BIG),
    )(x, g, w_in, *deps)


def _mm_out_bwd(dh1b, w_out, deps=()):
    T = dh1b.shape[0]
    tm = _row_tile(T)

    def body(dy_ref, w_ref, *rest):
        rest = rest[len(deps):]
        dom_ref, scr = rest[0], rest[-1]
        acc = _dot(dy_ref[...], w_ref[...], NT)
        dom_ref[...] = acc.astype(BF16)

        @pl.when(pl.program_id(1) == 0)
        def _():
            _to_lane_tiles(scr, acc)
            _emit_class_major(scr, rest[1:-1], tm)

    return pl.pallas_call(
        body, name="mm_out_bwd", grid=(T // tm, D // DA),
        in_specs=[pl.BlockSpec((tm, D), lambda i, j: (i, 0)),
                  pl.BlockSpec((DA, D), lambda i, j: (j, 0))] + [ANY_SPEC] * len(deps),
        out_specs=[pl.BlockSpec((tm, DA), lambda i, j: (i, j))]
        + [pl.BlockSpec((d, tm // d, DA), lambda i, j: (0, i, 0)) for d in WIDE],
        out_shape=[jax.ShapeDtypeStruct((T, D), BF16)]
        + [jax.ShapeDtypeStruct((d, T // d, DA), BF16) for d in WIDE],
        scratch_shapes=[pltpu.VMEM((DA // LANES, tm, LANES), F32)],
        compiler_params=pltpu.CompilerParams(
            dimension_semantics=("parallel", "arbitrary"), vmem_limit_bytes=VMEM_BIG),
    )(dh1b, w_out, *deps)


TM_FULL_ROW = 512


def _full_row_specs(tm):
    row = pl.BlockSpec((tm, D), lambda i, j, k: (i, 0))
    return row, pl.BlockSpec((1, D), lambda i, j, k: (0, 0))


def _mm_out(o_attn, o_conv, w_out, x, g_next):
    T = x.shape[0]
    tm = min(TM_FULL_ROW, T)
    row, vec = _full_row_specs(tm)
    return _mm(
        "mm_out", (T // tm, 1, 1),
        [pl.BlockSpec((tm, DA), lambda i, j, k: (i, 0)),
         pl.BlockSpec((DA, D), lambda i, j, k: (0, 0)),
         pl.BlockSpec((tm, DC), lambda i, j, k: (i, 0)),
         pl.BlockSpec((DC, D), lambda i, j, k: (1, 0)),
         row, vec],
        [o_attn, w_out, o_conv, w_out, x, g_next],
        [row, row],
        [jax.ShapeDtypeStruct((T, D), F32), jax.ShapeDtypeStruct((T, D), BF16)], NN, 2,
        _ep_resid_norm)


FF_TN = 512
FF_TK = 2 * N_FF


def _mm_gate_up(f, wg_t, wu_t):
    T = f.shape[0]
    tm = _row_tile(T)

    def body(f_ref, wg_ref, wu_ref, dg_ref, du_ref, a_ref):
        fv = f_ref[...]
        g = _dot(fv, wg_ref[...], NT)
        u = _dot(fv, wu_ref[...], NT)
        sg = _sigmoid(g)
        silu = g * sg
        dg_ref[...] = (u * (sg * (1.0 + g * (1.0 - sg)))).astype(BF16)
        du_ref[...] = silu.astype(BF16)
        a_ref[...] = (silu * u).astype(BF16)

    wspec = pl.BlockSpec((FF_TN, D), lambda i, j: (j, 0))
    ospec = pl.BlockSpec((tm, FF_TN), lambda i, j: (i, j))
    sh = jax.ShapeDtypeStruct((T, DFF), BF16)
    return pl.pallas_call(
        body, name="mm_gate_up", grid=(T // tm, DFF // FF_TN),
        in_specs=[pl.BlockSpec((tm, D), lambda i, j: (i, 0)), wspec, wspec],
        out_specs=[ospec, ospec, ospec], out_shape=[sh, sh, sh],
        compiler_params=pltpu.CompilerParams(
            dimension_semantics=("parallel", "parallel"), vmem_limit_bytes=VMEM_BIG),
    )(f, wg_t, wu_t)


def _mm_down(act, w_down, h1, g_next):
    T = h1.shape[0]
    tm = min(TM_FULL_ROW, T)
    row, vec = _full_row_specs(tm)
    return _mm(
        "mm_down", (T // tm, 1, DFF // FF_TK),
        [pl.BlockSpec((tm, FF_TK), lambda i, j, k: (i, k)),
         pl.BlockSpec((FF_TK, D), lambda i, j, k: (k, 0)),
         row, vec],
        [act, w_down, h1, g_next],
        [row, row],
        [jax.ShapeDtypeStruct((T, D), F32), jax.ShapeDtypeStruct((T, D), BF16)], NN, 1,
        _ep_resid_norm, acc_shape=(tm, D))


def _mm_ple(r, w_pgate, b_pgate, p, w_ple, h2):
    T = h2.shape[0]
    tm = _row_tile(T)
    tn = 1024

    def body(r_ref, wg_ref, b_ref, p_ref, wp_ref, h2_ref, gte_ref, pe_ref, h3_ref):
        gte = _sigmoid(_dot(r_ref[...], wg_ref[...], NN) + b_ref[...])
        pe = _dot(p_ref[...], wp_ref[...], NN)
        gte_ref[...] = gte.astype(BF16)
        pe_ref[...] = pe.astype(BF16)
        h3_ref[...] = h2_ref[...] + pe * gte

    tile = pl.BlockSpec((tm, tn), lambda i, j: (i, j))
    return pl.pallas_call(
        body, name="mm_ple", grid=(T // tm, D // tn),
        in_specs=[pl.BlockSpec((tm, D), lambda i, j: (i, 0)),
                  pl.BlockSpec((D, tn), lambda i, j: (0, j)),
                  pl.BlockSpec((1, tn), lambda i, j: (0, j)),
                  pl.BlockSpec((tm, DPLE), lambda i, j: (i, 0)),
                  pl.BlockSpec((DPLE, tn), lambda i, j: (0, j)),
                  tile],
        out_specs=[tile, tile, tile],
        out_shape=[jax.ShapeDtypeStruct((T, D), BF16), jax.ShapeDtypeStruct((T, D), BF16),
                   jax.ShapeDtypeStruct((T, D), F32)],
        compiler_params=pltpu.CompilerParams(
            dimension_semantics=("parallel", "parallel"), vmem_limit_bytes=VMEM_BIG),
    )(r, w_pgate, b_pgate, p, w_ple, h2)


def _mm_nt(name, dy, w, deps=()):
    T, n = dy.shape
    kdim = w.shape[0]
    tm = _row_tile(T)
    tn = 1024
    return _mm(
        name, (T // tm, kdim // tn, 1),
        [pl.BlockSpec((tm, n), lambda i, j, k: (i, 0)),
         pl.BlockSpec((tn, n), lambda i, j, k: (j, 0))],
        [dy, w],
        [pl.BlockSpec((tm, tn), lambda i, j, k: (i, j))],
        [jax.ShapeDtypeStruct((T, kdim), BF16)], NT, 1, _ep_cast(BF16), deps=deps)[0]


def _mm_down_bwd(dh2, w_down, g, u, deps=()):
    T = dh2.shape[0]
    tm = _tn_rows(T)
    gspec = pl.BlockSpec((tm, FF_TN), lambda i, j, k: (i, j))
    sh = jax.ShapeDtypeStruct((T, DFF), BF16)
    return _mm(
        "mm_down_bwd", (T // tm, DFF // FF_TN, 1),
        [pl.BlockSpec((tm, D), lambda i, j, k: (i, 0)),
         pl.BlockSpec((FF_TN, D), lambda i, j, k: (j, 0)),
         gspec, gspec],
        [dh2, w_down, g, u],
        [gspec, gspec], [sh, sh], NT, 1, _ep_swiglu_bwd, deps=deps)


def _mm_ffn_in_bwd(dg, wg_t, du, wu_t, deps=()):
    T = dg.shape[0]
    tm = _row_tile(T)
    tn = 1024
    aspec = pl.BlockSpec((tm, FF_TK), lambda i, j, k: (i, k))
    wspec = pl.BlockSpec((FF_TK, tn), lambda i, j, k: (k, j))
    return _mm(
        "mm_ffn_in_bwd", (T // tm, D // tn, DFF // FF_TK),
        [aspec, wspec, aspec, wspec], [dg, wg_t, du, wu_t],
        [pl.BlockSpec((tm, tn), lambda i, j, k: (i, j))],
        [jax.ShapeDtypeStruct((T, D), BF16)], NN, 2, _ep_cast(BF16), acc_shape=(tm, tn),
        deps=deps)[0]


def _mm_in_bwd(dz, w_in, deps=()):
    T = dz.shape[0]
    tm = _row_tile(T)
    tn = 1024
    sg = 4
    return _mm(
        "mm_in_bwd", (T // tm, D // tn, NDEV // sg),
        [pl.BlockSpec((tm, sg * N_IN), lambda i, j, k: (i, k)),
         pl.BlockSpec((sg, tn, N_IN), lambda i, j, k: (k, j, 0))],
        [dz, w_in],
        [pl.BlockSpec((tm, tn), lambda i, j, k: (i, j))],
        [jax.ShapeDtypeStruct((T, D), BF16)], NT, 1, _ep_cast(BF16), acc_shape=(tm, tn),
        deps=deps, group=sg, a_cols=N_IN)[0]


def _mm_tn(name, a, b, tj=None):
    T, idim = a.shape
    jdim = b.shape[1]
    tt = _row_tile(T)
    ti = min(idim, 1024)
    tj = jdim if tj is None else tj
    return _mm(
        name, (idim // ti, jdim // tj, T // tt),
        [pl.BlockSpec((tt, ti), lambda i, j, k: (k, i)),
         pl.BlockSpec((tt, tj), lambda i, j, k: (k, j))],
        [a, b],
        [pl.BlockSpec((ti, tj), lambda i, j, k: (i, j))],
        [jax.ShapeDtypeStruct((idim, jdim), BF16)], TN, 1, _ep_cast(BF16), acc_shape=(ti, tj))[0]


def _mm_tn_cols(name, a, b, ncol):
    T, idim = a.shape
    tt = _tn_rows(T)
    return _mm(
        name, (1, NDEV, T // tt),
        [pl.BlockSpec((tt, idim), lambda i, j, k: (k, 0)),
         pl.BlockSpec((tt, ncol), lambda i, j, k: (k, j))],
        [a, b],
        [pl.BlockSpec((None, idim, ncol), lambda i, j, k: (j, 0, 0))],
        [jax.ShapeDtypeStruct((NDEV, idim, ncol), BF16)], TN, 1, _ep_cast(BF16),
        acc_shape=(idim, ncol))[0]


def _mm_tn_ff(name, a, b):
    T = b.shape[0]
    tt = _tn_rows(T)
    return _mm(
        name, (DFF // FF_TN, 1, T // tt),
        [pl.BlockSpec((tt, FF_TN), lambda i, j, k: (k, i)),
         pl.BlockSpec((tt, D), lambda i, j, k: (k, 0))],
        [a, b],
        [pl.BlockSpec((FF_TN, D), lambda i, j, k: (i, 0))],
        [jax.ShapeDtypeStruct((DFF, D), BF16)], TN, 1, _ep_cast(BF16),
        acc_shape=(FF_TN, D))[0]


TR = 256


def _rows(T):
    return min(TR, T)


def _rms_fwd(name, h, g, deps=()):
    T = h.shape[0]
    tr = _rows(T)

    def body(h_ref, g_ref, *rest):
        o_ref = rest[-1]
        v = h_ref[...]
        r = lax.rsqrt(jnp.mean(v * v, axis=-1, keepdims=True) + EPS)
        o_ref[...] = (v * r * g_ref[...]).astype(BF16)

    return pl.pallas_call(
        body, name=name, grid=(T // tr,),
        in_specs=[pl.BlockSpec((tr, D), lambda i: (i, 0)), pl.BlockSpec((1, D), lambda i: (0, 0))]
        + [ANY_SPEC] * len(deps),
        out_specs=pl.BlockSpec((tr, D), lambda i: (i, 0)),
        out_shape=jax.ShapeDtypeStruct((T, D), BF16),
        compiler_params=pltpu.CompilerParams(dimension_semantics=("parallel",)),
    )(h, g, *deps)


def _fold8(v):
    return jnp.sum(v.reshape(v.shape[0] // 8, 8, v.shape[1]), axis=0)


def _rms_bwd(name, dn_out, h, g, dres, want_bf16):
    T = h.shape[0]
    tr = _rows(T)
    nt = T // tr

    def body(dy_ref, h_ref, g_ref, dres_ref, *rest):
        if want_bf16:
            dh_ref, dhb_ref, dg_ref, acc = rest
        else:
            dh_ref, dg_ref, acc = rest
        i = pl.program_id(0)
        v = h_ref[...]
        r = lax.rsqrt(jnp.mean(v * v, axis=-1, keepdims=True) + EPS)
        nrm = v * r
        dy = dy_ref[...].astype(F32)
        dn = dy * g_ref[...]
        dh = dres_ref[...] + r * (dn - nrm * jnp.mean(dn * nrm, axis=-1, keepdims=True))
        dh_ref[...] = dh
        if want_bf16:
            dhb_ref[...] = dh.astype(BF16)

        @pl.when(i == 0)
        def _():
            acc[...] = jnp.zeros_like(acc)

        acc[...] += _fold8(dy * nrm)

        @pl.when(i == nt - 1)
        def _():
            dg_ref[...] = jnp.sum(acc[...], axis=0, keepdims=True)

    tile = pl.BlockSpec((tr, D), lambda i: (i, 0))
    vec = pl.BlockSpec((1, D), lambda i: (0, 0))
    out_specs = [tile] + ([tile] if want_bf16 else []) + [vec]
    out_shape = ([jax.ShapeDtypeStruct((T, D), F32)]
                 + ([jax.ShapeDtypeStruct((T, D), BF16)] if want_bf16 else [])
                 + [jax.ShapeDtypeStruct((1, D), F32)])
    return pl.pallas_call(
        body, name=name, grid=(nt,),
        in_specs=[tile, tile, vec, tile], out_specs=out_specs, out_shape=out_shape,
        scratch_shapes=[pltpu.VMEM((8, D), F32)],
        compiler_params=pltpu.CompilerParams(dimension_semantics=("arbitrary",)),
    )(dn_out, h, g, dres)


def _loss_bwd(h3, target, g_final, pe, gte):
    T = h3.shape[0]
    tr = _rows(T)
    nt = T // tr

    def body(h_ref, t_ref, g_ref, pe_ref, gte_ref, loss_ref, dh_ref, dpe_ref, dpg_ref,
             dgf_ref, dbp_ref, lacc, gacc, bacc):
        i = pl.program_id(0)
        v = h_ref[...]
        r = lax.rsqrt(jnp.mean(v * v, axis=-1, keepdims=True) + EPS)
        nrm = v * r
        g = g_ref[...]
        err = nrm * g - t_ref[...]
        dy = err * (1.0 / D)
        dn = dy * g
        dh = r * (dn - nrm * jnp.mean(dn * nrm, axis=-1, keepdims=True))
        dh_ref[...] = dh
        gte = gte_ref[...].astype(F32)
        pe = pe_ref[...].astype(F32)
        dpe_ref[...] = (dh * gte).astype(BF16)
        dpg = dh * pe * gte * (1.0 - gte)
        dpg_ref[...] = dpg.astype(BF16)

        @pl.when(i == 0)
        def _():
            lacc[...] = jnp.zeros_like(lacc)
            gacc[...] = jnp.zeros_like(gacc)
            bacc[...] = jnp.zeros_like(bacc)

        lacc[...] += _fold8(err * err)
        gacc[...] += _fold8(dy * nrm)
        bacc[...] += _fold8(dpg)

        @pl.when(i == nt - 1)
        def _():
            tot = jnp.sum(jnp.sum(lacc[...], axis=0, keepdims=True), axis=1, keepdims=True)
            loss_ref[...] = jnp.broadcast_to(tot * (0.5 / D), (1, 128))
            dgf_ref[...] = jnp.sum(gacc[...], axis=0, keepdims=True)
            dbp_ref[...] = jnp.sum(bacc[...], axis=0, keepdims=True)

    tile = pl.BlockSpec((tr, D), lambda i: (i, 0))
    vec = pl.BlockSpec((1, D), lambda i: (0, 0))
    return pl.pallas_call(
        body, name="loss_bwd", grid=(nt,),
        in_specs=[tile, tile, vec, tile, tile],
        out_specs=[pl.BlockSpec((1, 128), lambda i: (0, 0)), tile, tile, tile, vec, vec],
        out_shape=[jax.ShapeDtypeStruct((1, 128), F32), jax.ShapeDtypeStruct((T, D), F32),
                   jax.ShapeDtypeStruct((T, D), BF16), jax.ShapeDtypeStruct((T, D), BF16),
                   jax.ShapeDtypeStruct((1, D), F32), jax.ShapeDtypeStruct((1, D), F32)],
        scratch_shapes=[pltpu.VMEM((8, D), F32)] * 3,
        compiler_params=pltpu.CompilerParams(dimension_semantics=("arbitrary",)),
    )(h3, target, g_final, pe, gte)


def _band_masks():
    qi = lax.broadcasted_iota(jnp.int32, (BLK, BLK), 0)
    kj = lax.broadcasted_iota(jnp.int32, (BLK, BLK), 1)
    return kj >= qi, kj <= qi


AQ = 4


def _cm_spec(d, col, nblk, rowmap=lambda n: n):
    if d == 1:
        return pl.BlockSpec((nblk * BLK, DA), lambda r, n: (rowmap(n), col))
    return pl.BlockSpec((None, nblk * BLK, DA), lambda r, n: (r, rowmap(n), col))


def _cm_shape(d, T, dtype):
    return jax.ShapeDtypeStruct((T, DA) if d == 1 else (d, T // d, DA), dtype)


def _blk(b):
    return slice(b * BLK, (b + 1) * BLK)


HEADS = tuple(slice(h * DH, (h + 1) * DH) for h in range(NH))


def _attn_fwd(name, zsrc, d, T):
    nb = T // d // BLK
    aq = min(AQ, nb)
    scale = DH ** -0.5

    def body(q_ref, kp_ref, kc_ref, vp_ref, vc_ref, o_ref, l_ref):
        n = pl.program_id(1)
        band_prev, cur_ok = _band_masks()
        for b in range(aq):
            kp = (lambda sl: kp_ref[:, sl]) if b == 0 else (lambda sl, b=b: kc_ref[_blk(b - 1), sl])
            vp = (lambda sl: vp_ref[:, sl]) if b == 0 else (lambda sl, b=b: vc_ref[_blk(b - 1), sl])
            prev_ok = band_prev & (n > 0) if b == 0 else band_prev
            rows = _blk(b)
            s = [(jnp.where(prev_ok, _dot(q_ref[rows, sl], kp(sl), NT) * scale, NEG),
                  jnp.where(cur_ok, _dot(q_ref[rows, sl], kc_ref[rows, sl], NT) * scale, NEG))
                 for sl in HEADS]
            m = [jnp.maximum(jnp.max(sp, axis=1, keepdims=True), jnp.max(sc, axis=1, keepdims=True))
                 for sp, sc in s]
            p = [(jnp.exp(sp - mh), jnp.exp(sc - mh)) for (sp, sc), mh in zip(s, m)]
            den = [jnp.sum(pp, axis=1, keepdims=True) + jnp.sum(pc, axis=1, keepdims=True)
                   for pp, pc in p]
            o = [_dot(pp.astype(BF16), vp(sl), NN) + _dot(pc.astype(BF16), vc_ref[rows, sl], NN)
                 for (pp, pc), sl in zip(p, HEADS)]
            o_ref[rows, :] = jnp.concatenate(
                [(oh / dh).astype(BF16) for oh, dh in zip(o, den)], axis=1)
            l_ref[rows, :] = jnp.concatenate(
                [jnp.broadcast_to(mh + jnp.log(dh), (BLK, DH)) for mh, dh in zip(m, den)], axis=1)

    halo = lambda n: jnp.maximum(aq * n - 1, 0)
    return pl.pallas_call(
        body, name=name, grid=(d, nb // aq),
        in_specs=[_cm_spec(d, 0, aq), _cm_spec(d, 1, 1, halo), _cm_spec(d, 1, aq),
                  _cm_spec(d, 2, 1, halo), _cm_spec(d, 2, aq)],
        out_specs=[_cm_spec(d, 0, aq)] * 2,
        out_shape=[_cm_shape(d, T, BF16), _cm_shape(d, T, F32)],
        compiler_params=pltpu.CompilerParams(dimension_semantics=("parallel", "parallel")),
    )(zsrc, zsrc, zsrc, zsrc, zsrc)


def _cm_tile(d, tr):
    if d == 1:
        return pl.BlockSpec((tr, DA), lambda i: (i, 0))
    return pl.BlockSpec((d, tr // d, DA), lambda i: (0, i, 0))


def _attn_combine(outs, lses, T):
    tr = _rows(T)

    def body(*refs):
        o_in, l_in = refs[:3], refs[3:6]
        o_ref, l_ref = refs[6:8]
        o_cm, l_cm = refs[8:8 + len(WIDE)], refs[8 + len(WIDE):8 + 2 * len(WIDE)]
        so, sl, so_all, sl_all = refs[8 + 2 * len(WIDE):]
        for c in range(DA // LANES):
            lt = _lane_tile(c)
            os_, ls_ = [o_in[0][:, lt].astype(F32)], [l_in[0][:, lt]]
            for w, d in enumerate(WIDE):
                for r in range(d):
                    so[w, c, pl.ds(r, tr // d, stride=d), :] = o_in[1 + w][r, :, lt].astype(F32)
                    sl[w, c, pl.ds(r, tr // d, stride=d), :] = l_in[1 + w][r, :, lt]
                os_.append(so[w, c])
                ls_.append(sl[w, c])
            la, lb, lc = ls_
            m = jnp.maximum(jnp.maximum(la, lb), lc)
            ea, eb, ec = jnp.exp(la - m), jnp.exp(lb - m), jnp.exp(lc - m)
            s = ea + eb + ec
            o = (ea * os_[0] + eb * os_[1] + ec * os_[2]) / s
            lse = m + jnp.log(s)
            o_ref[:, lt] = o.astype(BF16)
            l_ref[:, lt] = lse
            so_all[c] = o
            sl_all[c] = lse
        _emit_class_major(so_all, o_cm, tr)
        _emit_class_major(sl_all, l_cm, tr)

    specs = [_cm_tile(d, tr) for d in DILATIONS]
    wide = [_cm_tile(d, tr) for d in WIDE]
    return pl.pallas_call(
        body, name="attn_combine", grid=(T // tr,),
        in_specs=specs + specs,
        out_specs=[specs[0], specs[0]] + wide + wide,
        out_shape=[_cm_shape(1, T, BF16), _cm_shape(1, T, F32)]
        + [_cm_shape(d, T, BF16) for d in WIDE] + [_cm_shape(d, T, F32) for d in WIDE],
        scratch_shapes=[pltpu.VMEM((len(WIDE), DA // LANES, tr, LANES), F32)] * 2
        + [pltpu.VMEM((DA // LANES, tr, LANES), F32)] * 2,
        compiler_params=pltpu.CompilerParams(
            dimension_semantics=("parallel",), vmem_limit_bytes=VMEM_MID),
    )(*outs, *lses)


def _attn_bwd_q(name, zsrc, dosrc, osrc, lsrc, d, T):
    nb = T // d // BLK
    aq = min(AQ, nb)
    scale = DH ** -0.5

    def body(q_ref, kp_ref, kc_ref, vp_ref, vc_ref, do_ref, o_ref, l_ref, dq_ref):
        n = pl.program_id(1)
        band_prev, cur_ok = _band_masks()
        for b in range(aq):
            kp = (lambda sl: kp_ref[:, sl]) if b == 0 else (lambda sl, b=b: kc_ref[_blk(b - 1), sl])
            vp = (lambda sl: vp_ref[:, sl]) if b == 0 else (lambda sl, b=b: vc_ref[_blk(b - 1), sl])
            prev_ok = band_prev & (n > 0) if b == 0 else band_prev
            rows = _blk(b)
            s = [(_dot(q_ref[rows, sl], kp(sl), NT), _dot(q_ref[rows, sl], kc_ref[rows, sl], NT))
                 for sl in HEADS]
            dp = [(_dot(do_ref[rows, sl], vp(sl), NT), _dot(do_ref[rows, sl], vc_ref[rows, sl], NT))
                  for sl in HEADS]
            delta = [jnp.sum(do_ref[rows, sl].astype(F32) * o_ref[rows, sl].astype(F32), axis=1,
                             keepdims=True) for sl in HEADS]
            p = [(jnp.exp(jnp.where(prev_ok, sp * scale - l_ref[rows, sl], NEG)),
                  jnp.exp(jnp.where(cur_ok, sc * scale - l_ref[rows, sl], NEG)))
                 for (sp, sc), sl in zip(s, HEADS)]
            ds = [((pp * (dpp - dl) * scale).astype(BF16), (pc * (dpc - dl) * scale).astype(BF16))
                  for (pp, pc), (dpp, dpc), dl in zip(p, dp, delta)]
            dq = [_dot(dsp, kp(sl), NN) + _dot(dsc, kc_ref[rows, sl], NN)
                  for (dsp, dsc), sl in zip(ds, HEADS)]
            dq_ref[rows, :] = jnp.concatenate([v.astype(BF16) for v in dq], axis=1)

    halo = lambda n: jnp.maximum(aq * n - 1, 0)
    own = _cm_spec(d, 0, aq)
    return pl.pallas_call(
        body, name=name, grid=(d, nb // aq),
        in_specs=[own, _cm_spec(d, 1, 1, halo), _cm_spec(d, 1, aq), _cm_spec(d, 2, 1, halo),
                  _cm_spec(d, 2, aq), own, own, own],
        out_specs=own, out_shape=_cm_shape(d, T, BF16),
        compiler_params=pltpu.CompilerParams(dimension_semantics=("parallel", "parallel")),
    )(zsrc, zsrc, zsrc, zsrc, zsrc, dosrc, osrc, lsrc)


def _attn_bwd_kv(name, zsrc, dosrc, osrc, lsrc, d, T):
    nb = T // d // BLK
    aq = min(AQ, nb)
    nsteps = nb // aq
    scale = DH ** -0.5

    def body(k_ref, v_ref, q_ref, qn_ref, do_ref, don_ref, o_ref, on_ref, l_ref, ln_ref,
             dk_ref, dv_ref):
        j = pl.program_id(1)
        band_next, own_ok = _band_masks()
        for b in range(aq):
            rows = _blk(b)
            last = b == aq - 1
            pick = lambda cur, halo: ((lambda sl: halo[:, sl]) if last
                                      else (lambda sl, b=b: cur[_blk(b + 1), sl]))
            qb, dob, ob, lb = (pick(q_ref, qn_ref), pick(do_ref, don_ref), pick(o_ref, on_ref),
                               pick(l_ref, ln_ref))
            next_ok = band_next & (j < nsteps - 1) if last else band_next
            s = [(_dot(q_ref[rows, sl], k_ref[rows, sl], NT), _dot(qb(sl), k_ref[rows, sl], NT))
                 for sl in HEADS]
            dp = [(_dot(do_ref[rows, sl], v_ref[rows, sl], NT), _dot(dob(sl), v_ref[rows, sl], NT))
                  for sl in HEADS]
            delta = [(jnp.sum(do_ref[rows, sl].astype(F32) * o_ref[rows, sl].astype(F32), axis=1,
                              keepdims=True),
                      jnp.sum(dob(sl).astype(F32) * ob(sl).astype(F32), axis=1, keepdims=True))
                     for sl in HEADS]
            p = [(jnp.exp(jnp.where(own_ok, sa * scale - l_ref[rows, sl], NEG)),
                  jnp.exp(jnp.where(next_ok, sb * scale - lb(sl), NEG)))
                 for (sa, sb), sl in zip(s, HEADS)]
            dv = [_dot(pa.astype(BF16), do_ref[rows, sl], TN) + _dot(pb.astype(BF16), dob(sl), TN)
                  for (pa, pb), sl in zip(p, HEADS)]
            ds = [((pa * (dpa - da) * scale).astype(BF16), (pb * (dpb - db) * scale).astype(BF16))
                  for (pa, pb), (dpa, dpb), (da, db) in zip(p, dp, delta)]
            dk = [_dot(dsa, q_ref[rows, sl], TN) + _dot(dsb, qb(sl), TN)
                  for (dsa, dsb), sl in zip(ds, HEADS)]
            dk_ref[rows, :] = jnp.concatenate([v.astype(BF16) for v in dk], axis=1)
            dv_ref[rows, :] = jnp.concatenate([v.astype(BF16) for v in dv], axis=1)

    halo = lambda j: jnp.minimum(aq * (j + 1), nb - 1)
    own, own_n = _cm_spec(d, 0, aq), _cm_spec(d, 0, 1, halo)
    sh = _cm_shape(d, T, BF16)
    return pl.pallas_call(
        body, name=name, grid=(d, nsteps),
        in_specs=[_cm_spec(d, 1, aq), _cm_spec(d, 2, aq), own, own_n, own, own_n, own, own_n,
                  own, own_n],
        out_specs=[own, own], out_shape=[sh, sh],
        compiler_params=pltpu.CompilerParams(dimension_semantics=("parallel", "parallel")),
    )(zsrc, zsrc, zsrc, zsrc, dosrc, dosrc, osrc, osrc, lsrc, lsrc)


def _dz_assemble(dqs, dks, dvs, dcvg, T):
    tr = _rows(T)
    nb = len(DILATIONS)

    def body(*refs):
        cvg_ref, dz_ref, scr = refs[3 * nb], refs[3 * nb + 1], refs[3 * nb + 2]
        for g in range(3):
            parts = refs[g * nb:(g + 1) * nb]
            for c in range(DA // LANES):
                lt = _lane_tile(c)
                scr[g, c] = parts[0][:, lt].astype(F32)
                for w, d in enumerate(WIDE):
                    for r in range(d):
                        rows = pl.ds(r, tr // d, stride=d)
                        scr[g, c, rows, :] = scr[g, c, rows, :] + parts[1 + w][r, :, lt].astype(F32)
                dz_ref[:, g * DA + c * LANES:g * DA + (c + 1) * LANES] = scr[g, c].astype(BF16)
        dz_ref[:, 3 * DA:] = cvg_ref[...]

    specs = [_cm_tile(d, tr) for d in DILATIONS]
    return pl.pallas_call(
        body, name="dz_assemble", grid=(T // tr,),
        in_specs=specs * 3 + [pl.BlockSpec((tr, 2 * DC), lambda i: (i, 0))],
        out_specs=pl.BlockSpec((tr, DIN), lambda i: (i, 0)),
        out_shape=jax.ShapeDtypeStruct((T, DIN), BF16),
        scratch_shapes=[pltpu.VMEM((3, DA // LANES, tr, LANES), F32)],
        compiler_params=pltpu.CompilerParams(
            dimension_semantics=("parallel",), vmem_limit_bytes=VMEM_MID),
    )(*dqs, *dks, *dvs, dcvg)


CT = 256
HALO = 32
RC = 32


def _conv_fwd(z, w_dw, b_dw, g_ln, b_ln):
    T = z.shape[0]
    ct = min(CT, T)
    nt = T // ct
    hb = ct // HALO

    def body(cv_ref, cg_ref, cvp_ref, cgp_ref, w_ref, bdw_ref, g_ref, b_ref, oc_ref, y_ref, ubuf, ush):
        i = pl.program_id(0)
        up = cvp_ref[...].astype(F32) * _sigmoid(cgp_ref[...].astype(F32))
        ubuf[0:HALO, :] = jnp.where(i > 0, up, 0.0)
        ubuf[HALO:, :] = cv_ref[...].astype(F32) * _sigmoid(cg_ref[...].astype(F32))
        for b in range(8):
            ush[b] = ubuf[pl.ds(8 - b, ct + 24), :]

        def chunk(ci, carry):
            r0 = pl.multiple_of(ci * RC, RC)
            acc = jnp.broadcast_to(bdw_ref[...], (RC, DC))
            for s in range(CW):
                a, b = divmod(s, 8)
                acc = acc + w_ref[CW - 1 - s:CW - s, :] * ush[b, pl.ds(r0 + 24 - 8 * a, RC), :]
            y_ref[pl.ds(r0, RC), :] = acc
            mu = jnp.mean(acc, axis=-1, keepdims=True)
            cen = acc - mu
            var = jnp.mean(cen * cen, axis=-1, keepdims=True)
            ln = cen * lax.rsqrt(var + EPS) * g_ref[...] + b_ref[...]
            oc_ref[pl.ds(r0, RC), :] = (ln * _sigmoid(ln)).astype(BF16)
            return carry

        lax.fori_loop(0, ct // RC, chunk, 0)

    cur = lambda col: pl.BlockSpec((ct, DC), lambda i: (i, col))
    prv = lambda col: pl.BlockSpec((HALO, DC), lambda i: (jnp.maximum(i * hb - 1, 0), col))
    vec = pl.BlockSpec((1, DC), lambda i: (0, 0))
    return pl.pallas_call(
        body, name="conv_fwd", grid=(nt,),
        in_specs=[cur(3), cur(4), prv(3), prv(4), pl.BlockSpec((CW, DC), lambda i: (0, 0)),
                  vec, vec, vec],
        out_specs=[pl.BlockSpec((ct, DC), lambda i: (i, 0))] * 2,
        out_shape=[jax.ShapeDtypeStruct((T, DC), BF16), jax.ShapeDtypeStruct((T, DC), F32)],
        scratch_shapes=[pltpu.VMEM((ct + HALO, DC), F32), pltpu.VMEM((8, ct + 24, DC), F32)],
        compiler_params=pltpu.CompilerParams(
            dimension_semantics=("parallel",), vmem_limit_bytes=VMEM_MID),
    )(z, z, z, z, w_dw, b_dw, g_ln, b_ln)


def _conv_bwd(z, dom, y, w_dw, g_ln, b_ln):
    T = z.shape[0]
    ct = min(CT, T)
    nt = T // ct
    hb = ct // HALO
    last_halo = T // HALO - 1

    def ln_bwd(yv, dov, g_ref, b_ref):
        mu = jnp.mean(yv, axis=-1, keepdims=True)
        cen = yv - mu
        rstd = lax.rsqrt(jnp.mean(cen * cen, axis=-1, keepdims=True) + EPS)
        xhat = cen * rstd
        ln = xhat * g_ref[...] + b_ref[...]
        sg = _sigmoid(ln)
        dln = dov * (sg * (1.0 + ln * (1.0 - sg)))
        dxh = dln * g_ref[...]
        dy = rstd * (dxh - jnp.mean(dxh, axis=-1, keepdims=True)
                     - xhat * jnp.mean(dxh * xhat, axis=-1, keepdims=True))
        return dy, dln, xhat

    def body(do_ref, don_ref, y_ref, yn_ref, cv_ref, cg_ref, cvp_ref, cgp_ref, w_ref, g_ref, b_ref,
             dcvg_ref, dw_ref, dbdw_ref, dg_ref, db_ref,
             dybuf, dysh, ubuf, ush, dwacc, vacc):
        i = pl.program_id(0)

        @pl.when(i == 0)
        def _():
            dwacc[...] = jnp.zeros_like(dwacc)
            vacc[...] = jnp.zeros_like(vacc)

        def ln_chunk(ci, carry):
            r0 = pl.multiple_of(ci * RC, RC)
            dy, dln, xhat = ln_bwd(y_ref[pl.ds(r0, RC), :], do_ref[pl.ds(r0, RC), :].astype(F32),
                                   g_ref, b_ref)
            dybuf[pl.ds(r0, RC), :] = dy
            vacc[0] += _fold8(dy)
            vacc[1] += _fold8(dln * xhat)
            vacc[2] += _fold8(dln)
            return carry

        lax.fori_loop(0, ct // RC, ln_chunk, 0)
        dyn, _, _ = ln_bwd(yn_ref[...], don_ref[...].astype(F32), g_ref, b_ref)
        dybuf[ct:, :] = jnp.where(i < nt - 1, dyn, 0.0)
        for b in range(8):
            dysh[b] = dybuf[pl.ds(b, ct + 24), :]

        up = cvp_ref[...].astype(F32) * _sigmoid(cgp_ref[...].astype(F32))
        ubuf[0:HALO, :] = jnp.where(i > 0, up, 0.0)
        ubuf[HALO:, :] = cv_ref[...].astype(F32) * _sigmoid(cg_ref[...].astype(F32))
        for b in range(8):
            ush[b] = ubuf[pl.ds(8 - b, ct + 24), :]

        def chunk(ci, carry):
            r0 = pl.multiple_of(ci * RC, RC)
            dy = dybuf[pl.ds(r0, RC), :]
            du = jnp.zeros((RC, DC), F32)
            for s in range(CW):
                a, b = divmod(s, 8)
                du = du + w_ref[CW - 1 - s:CW - s, :] * dysh[b, pl.ds(r0 + 8 * a, RC), :]
                dwacc[CW - 1 - s] += _fold8(dy * ush[b, pl.ds(r0 + 24 - 8 * a, RC), :])
            cv = cv_ref[pl.ds(r0, RC), :].astype(F32)
            sg = _sigmoid(cg_ref[pl.ds(r0, RC), :].astype(F32))
            dcvg_ref[pl.ds(r0, RC), 0:DC] = (du * sg).astype(BF16)
            dcvg_ref[pl.ds(r0, RC), DC:2 * DC] = (du * cv * sg * (1.0 - sg)).astype(BF16)
            return carry

        lax.fori_loop(0, ct // RC, chunk, 0)

        @pl.when(i == nt - 1)
        def _():
            dw_ref[...] = jnp.sum(dwacc[...], axis=1)
            dbdw_ref[...] = jnp.sum(vacc[0], axis=0, keepdims=True)
            dg_ref[...] = jnp.sum(vacc[1], axis=0, keepdims=True)
            db_ref[...] = jnp.sum(vacc[2], axis=0, keepdims=True)

    cur = lambda col: pl.BlockSpec((ct, DC), lambda i: (i, col))
    prv = lambda col: pl.BlockSpec((HALO, DC), lambda i: (jnp.maximum(i * hb - 1, 0), col))
    nxt = lambda col: pl.BlockSpec((HALO, DC), lambda i: (jnp.minimum((i + 1) * hb, last_halo), col))
    vec = pl.BlockSpec((1, DC), lambda i: (0, 0))
    tile = pl.BlockSpec((ct, DC), lambda i: (i, 0))
    return pl.pallas_call(
        body, name="conv_bwd", grid=(nt,),
        in_specs=[cur(1), nxt(1), cur(0), nxt(0), cur(3), cur(4), prv(3), prv(4),
                  pl.BlockSpec((CW, DC), lambda i: (0, 0)), vec, vec],
        out_specs=[pl.BlockSpec((ct, 2 * DC), lambda i: (i, 0)),
                   pl.BlockSpec((CW, DC), lambda i: (0, 0)), vec, vec, vec],
        out_shape=[jax.ShapeDtypeStruct((T, 2 * DC), BF16),
                   jax.ShapeDtypeStruct((CW, DC), F32), jax.ShapeDtypeStruct((1, DC), F32),
                   jax.ShapeDtypeStruct((1, DC), F32), jax.ShapeDtypeStruct((1, DC), F32)],
        scratch_shapes=[pltpu.VMEM((ct + HALO, DC), F32), pltpu.VMEM((8, ct + 24, DC), F32),
                        pltpu.VMEM((ct + HALO, DC), F32), pltpu.VMEM((8, ct + 24, DC), F32),
                        pltpu.VMEM((CW, 8, DC), F32), pltpu.VMEM((3, 8, DC), F32)],
        compiler_params=pltpu.CompilerParams(
            dimension_semantics=("arbitrary",), vmem_limit_bytes=VMEM_BIG),
    )(dom, dom, y, y, z, z, z, z, w_dw, g_ln, b_ln)


def _adam_math(w, g, m, v):
    m = ADAM_B1 * m + (1.0 - ADAM_B1) * g
    v = ADAM_B2 * v + (1.0 - ADAM_B2) * (g * g)
    m_hat = m / (1.0 - ADAM_B1 ** ADAM_STEP)
    v_hat = v / (1.0 - ADAM_B2 ** ADAM_STEP)
    delta = -ADAM_LR * (m_hat / (jnp.sqrt(v_hat) + ADAM_EPS) + ADAM_WD * w)
    return delta, m, v


def _adam(name, slots, w, m, v):
    rows, cols = w.shape
    tr = next(t for t in (256, 176, 128, 64, 32, 16, 8, rows) if rows % t == 0)

    def body(s_ref, w_ref, m_ref, v_ref, g_out, d_out, m_out, v_out):
        g = s_ref[0].astype(F32)
        for s in range(1, NDEV):
            g = g + s_ref[s].astype(F32)
        delta, mn, vn = _adam_math(w_ref[...], g, m_ref[...], v_ref[...])
        g_out[...] = g
        d_out[...] = delta
        m_out[...] = mn
        v_out[...] = vn

    tile = pl.BlockSpec((tr, cols), lambda i: (i, 0))
    sh = jax.ShapeDtypeStruct((rows, cols), F32)
    return pl.pallas_call(
        body, name=name, grid=(rows // tr,),
        in_specs=[pl.BlockSpec((NDEV, tr, cols), lambda i: (0, i, 0)), tile, tile, tile],
        out_specs=[tile] * 4, out_shape=[sh] * 4,
        compiler_params=pltpu.CompilerParams(
            dimension_semantics=("parallel",), vmem_limit_bytes=VMEM_MID),
    )(slots, w, m, v)


SMALL_NAMES = ("g_mix", "b_dw", "g_conv_ln", "b_conv_ln", "g_ffn", "g_ple", "b_pgate", "g_final")


def _pack_small(vecs, w_dw_full):
    rows = [jnp.pad(v.reshape(1, -1), ((0, 0), (0, SMALL_W - v.size))) for v in vecs]
    rows.append(jnp.pad(w_dw_full, ((0, 0), (0, SMALL_W - DC))))
    rows.append(jnp.zeros((SMALL_ROWS - len(vecs) - CW, SMALL_W), F32))
    return jnp.concatenate(rows, axis=0)


def kernel(x, p, g_mix, w_in, w_dw, b_dw, g_conv_ln, b_conv_ln, w_out, g_ffn, w_gate, w_up, w_down, g_ple, w_pgate, b_pgate, w_ple, g_final, loss_target, m_g_mix, m_w_in, m_w_dw, m_b_dw, m_g_conv_ln, m_b_conv_ln, m_w_out, m_g_ffn, m_w_gate, m_w_up, m_w_down, m_g_ple, m_w_pgate, m_b_pgate, m_w_ple, m_g_final, v_g_mix, v_w_in, v_w_dw, v_b_dw, v_g_conv_ln, v_b_conv_ln, v_w_out, v_g_ffn, v_w_gate, v_w_up, v_w_down, v_g_ple, v_w_pgate, v_b_pgate, v_w_ple, v_g_final):
    T = x.shape[1]
    me = 4 * lax.axis_index("x") + 2 * lax.axis_index("y") + lax.axis_index("c")
    xs = x.reshape(T, D)
    ps = p.reshape(T, DPLE).astype(BF16)
    tgt = loss_target.reshape(T, D)
    g_final2 = g_final.reshape(1, D)

    tr_names = ("w_gate", "w_up")
    big = dict(w_in=w_in[0], w_out=w_out[0], w_gate=w_gate[0].T, w_up=w_up[0].T, w_down=w_down[0],
               w_pgate=w_pgate[0], w_ple=w_ple[0])
    order = ("w_in", "w_out", "w_gate", "w_up", "w_down", "w_pgate", "w_ple")
    w_dw_g, w_in_f = _gather_two_level(
        "gather_first", [w_dw.reshape(CW, DC // NDEV), big["w_in"].astype(BF16)])
    w_dw_f = w_dw_g.transpose(1, 0, 2).reshape(CW, DC)
    later = order[1:]
    lands = _place("gather_place", [(big[n], False) for n in later], dtype=BF16)
    g_handles, g_token = _xstart("gather_start", [(None, False)] * len(later), lands, deps=[w_in_f])
    G = dict(zip(later, g_handles))

    a, z, *z_wide = _mm_in(xs, g_mix, w_in_f, deps=[g_token])
    zsrc = dict(zip(DILATIONS, [z] + z_wide))
    br = [_attn_fwd(f"attn_fwd_d{d}", zsrc[d], d, T) for d in DILATIONS]
    comb = list(_attn_combine([b[0] for b in br], [b[1] for b in br], T))
    o_attn, lse = comb[0], comb[1]
    osrc = dict(zip(DILATIONS, [o_attn] + comb[2:2 + len(WIDE)]))
    lsrc = dict(zip(DILATIONS, [lse] + comb[2 + len(WIDE):]))
    o_conv, y_conv = _conv_fwd(z, w_dw_f, b_dw, g_conv_ln, b_conv_ln)
    w_out_f = _xwait("gather_wait_w_out", G["w_out"], o_conv).reshape(D, D)
    h1, f = _mm_out(o_attn, o_conv, w_out_f, xs, g_ffn)
    w_gate_f = _xwait("gather_wait_w_gate", G["w_gate"], f).reshape(DFF, D)
    w_up_f = _xwait("gather_wait_w_up", G["w_up"], f).reshape(DFF, D)
    gate, up, act = _mm_gate_up(f, w_gate_f, w_up_f)
    w_down_f = _xwait("gather_wait_w_down", G["w_down"], act).reshape(DFF, D)
    h2, r = _mm_down(act, w_down_f, h1, g_ple)
    w_pgate_f = _xwait("gather_wait_w_pgate", G["w_pgate"], r).reshape(D, D)
    w_ple_f = _xwait("gather_wait_w_ple", G["w_ple"], r).transpose(1, 0, 2).reshape(DPLE, D)
    gte, pe, h3 = _mm_ple(r, w_pgate_f, b_pgate, ps, w_ple_f, h2)

    loss_part, dh3, dpe, dpg, d_g_final, d_b_pgate = _loss_bwd(h3, tgt, g_final2, pe, gte)
    H = {}

    def send_grads(tag, named):
        items = [(v, True) for _, v in named]
        handles, token = _xstart(f"grads_start_{tag}", items, _place(f"grads_place_{tag}", items))
        H.update(zip([n for n, _ in named], handles))
        return token

    gw_pgate = _mm_tn("gw_pgate", r, dpg).reshape(NDEV, D // NDEV, D)
    gw_ple = _mm_tn("gw_ple", ps, dpe).reshape(DPLE, NDEV, D // NDEV).transpose(1, 0, 2)
    tok = send_grads("ple", [("w_pgate", gw_pgate), ("w_ple", gw_ple)])
    dr = _mm_nt("mm_pgate_bwd", dpg, w_pgate_f, deps=[tok])
    dh2, dh2b, d_g_ple = _rms_bwd("rms_ple_bwd", dr, h2, g_ple, dh3, True)
    ff_shards = lambda g: g.reshape(NDEV, N_FF, D)
    gw_down = ff_shards(_mm_tn_ff("gw_down", act, dh2b))
    tok = send_grads("down", [("w_down", gw_down)])
    dgate, dup = _mm_down_bwd(dh2b, w_down_f, gate, up, deps=[tok])
    gw_gate = ff_shards(_mm_tn_ff("gw_gate", dgate, f))
    gw_up = ff_shards(_mm_tn_ff("gw_up", dup, f))
    tok = send_grads("ffn", [("w_gate", gw_gate), ("w_up", gw_up)])
    df = _mm_ffn_in_bwd(dgate, w_gate_f, dup, w_up_f, deps=[tok])
    dh1, dh1b, d_g_ffn = _rms_bwd("rms_ffn_bwd", df, h1, g_ffn, dh2, True)
    gw_out = jnp.concatenate(
        [_mm_tn("gw_out_attn", o_attn, dh1b), _mm_tn("gw_out_conv", o_conv, dh1b)], axis=0)
    tok = send_grads("out", [("w_out", gw_out.reshape(NDEV, D // NDEV, D))])
    dom, *do_wide = _mm_out_bwd(dh1b, w_out_f, deps=[tok])
    dosrc = dict(zip(DILATIONS, [dom] + do_wide))
    dcvg, d_w_dw, d_b_dw, d_g_ln, d_b_ln = _conv_bwd(z, dom, y_conv, w_dw_f, g_conv_ln, b_conv_ln)
    dqs, dks, dvs = [], [], []
    for d in DILATIONS:
        dqs.append(_attn_bwd_q(f"attn_bwd_q_d{d}", zsrc[d], dosrc[d], osrc[d], lsrc[d], d, T))
        dk, dv = _attn_bwd_kv(f"attn_bwd_kv_d{d}", zsrc[d], dosrc[d], osrc[d], lsrc[d], d, T)
        dks.append(dk)
        dvs.append(dv)
    dz = _dz_assemble(dqs, dks, dvs, dcvg, T)
    gw_in = _mm_tn_cols("gw_in", a, dz, N_IN)
    tok = send_grads("in", [("w_in", gw_in)])
    da = _mm_in_bwd(dz, w_in_f, deps=[tok])
    grad_x, d_g_mix = _rms_bwd("rms_mix_bwd", da, xs, g_mix, dh1, False)

    small_part = _pack_small(
        [d_g_mix, d_b_dw, d_g_ln, d_b_ln, d_g_ffn, d_g_ple, d_b_pgate, d_g_final], d_w_dw)
    small_slots = _exchange("exchange_small_grads", [(small_part, False)])[0]
    S = {n: _xwait(f"grads_wait_{n}", H[n], small_slots)
         for n in ("w_pgate", "w_ple", "w_down", "w_gate", "w_up", "w_out", "w_in")}

    mom = dict(w_in=(m_w_in, v_w_in), w_out=(m_w_out, v_w_out), w_gate=(m_w_gate, v_w_gate),
               w_up=(m_w_up, v_w_up), w_down=(m_w_down, v_w_down), w_pgate=(m_w_pgate, v_w_pgate),
               w_ple=(m_w_ple, v_w_ple))
    upd = {}
    for n in order:
        m_n, v_n = mom[n][0][0], mom[n][1][0]
        if n in tr_names:
            res = _adam(f"adam_{n}", S[n], big[n], m_n.T, v_n.T)
            upd[n] = [t.T[None] for t in res]
        else:
            res = _adam(f"adam_{n}", S[n], big[n], m_n, v_n)
            upd[n] = [t[None] for t in res]

    def lanes(v):
        full = jnp.zeros((CW, NDEV, DC // NDEV), F32)
        full = lax.dynamic_update_slice(full, v.reshape(CW, 1, DC // NDEV), (0, me, 0))
        return full.reshape(CW, DC)

    small_w = _pack_small([g_mix, b_dw, g_conv_ln, b_conv_ln, g_ffn, g_ple, b_pgate, g_final2], lanes(w_dw))
    small_m = _pack_small([m_g_mix, m_b_dw, m_g_conv_ln, m_b_conv_ln, m_g_ffn, m_g_ple, m_b_pgate,
                           m_g_final.reshape(1, D)], lanes(m_w_dw))
    small_v = _pack_small([v_g_mix, v_b_dw, v_g_conv_ln, v_b_conv_ln, v_g_ffn, v_g_ple, v_b_pgate,
                           v_g_final.reshape(1, D)], lanes(v_w_dw))
    small_res = _adam("adam_small", small_slots, small_w, small_m, small_v)

    def unpack(t):
        out = {}
        widths = dict(g_mix=D, b_dw=DC, g_conv_ln=DC, b_conv_ln=DC, g_ffn=D, g_ple=D, b_pgate=D, g_final=D)
        for i, n in enumerate(SMALL_NAMES):
            out[n] = t[i:i + 1, :widths[n]]
        out["g_final"] = out["g_final"].reshape(D)
        taps = t[len(SMALL_NAMES):len(SMALL_NAMES) + CW, :DC].reshape(CW, NDEV, DC // NDEV)
        out["w_dw"] = lax.dynamic_slice(taps, (0, me, 0), (CW, 1, DC // NDEV))[None]
        return out

    small = [unpack(t) for t in small_res]

    loss = lax.psum(loss_part[0, 0], ("x", "y", "c"))
    names = ("g_mix", "w_in", "w_dw", "b_dw", "g_conv_ln", "b_conv_ln", "w_out", "g_ffn", "w_gate",
             "w_up", "w_down", "g_ple", "w_pgate", "b_pgate", "w_ple", "g_final")
    outs = [loss, grad_x.reshape(1, T, D)]
    for kind in range(4):
        for n in names:
            outs.append(upd[n][kind] if n in upd else small[kind][n])
    return tuple(outs)
```

```python
import jax
import jax.numpy as jnp
from jax import lax
from jax.experimental import pallas as pl
from jax.experimental.pallas import tpu as pltpu

F32 = jnp.float32
BF16 = jnp.bfloat16

NDEV = 8
D = 2048
NH = 8
DH = 128
DA = NH * DH
DC = D - DA
DIN = 3 * DA + 2 * DC
DFF = 5632
DPLE = 256
BLK = 128
DILATIONS = (1, 4, 16)
CW = 31
EPS = 1e-6
N_IN = DIN // NDEV
N_FF = DFF // NDEV
NEG = -1e30

ADAM_LR = 0.001
ADAM_B1 = 0.9
ADAM_B2 = 0.999
ADAM_EPS = 1e-08
ADAM_WD = 0.01
ADAM_STEP = 10

VMEM_CAP_V7X = 64 * 1024 * 1024
VMEM_BIG = VMEM_CAP_V7X - 12 * 1024 * 1024
VMEM_MID = 40 * 1024 * 1024

SMALL_W = 2048
SMALL_ROWS = 40


def _sigmoid(v):
    return 1.0 / (1.0 + jnp.exp(-v))


def _dot(a, b, contract):
    return lax.dot_general(a, b, (contract, ((), ())), preferred_element_type=F32)


NN = ((1,), (0,))
NT = ((1,), (1,))
TN = ((0,), (0,))


def _exchange(name, items):
    n = len(items)
    out_shape = [
        jax.ShapeDtypeStruct((NDEV,) + (a.shape[1:] if sc else a.shape), a.dtype)
        for a, sc in items
    ]
    scat = [sc for _, sc in items]

    def body(*refs):
        srcs = refs[:n]
        dsts = refs[n:2 * n]
        send_sems, recv_sems, loc_sems = refs[2 * n:]
        x = lax.axis_index("x")
        y = lax.axis_index("y")
        c = lax.axis_index("c")
        me = 4 * x + 2 * y + c

        local = []
        for i in range(n):
            src = srcs[i].at[me] if scat[i] else srcs[i]
            cp = pltpu.make_async_copy(src, dsts[i].at[me], loc_sems.at[i])
            cp.start()
            local.append(cp)

        remote = []
        for k in range(1, NDEV):
            px = (1 - x) if (k >> 2) & 1 else x
            py = (1 - y) if (k >> 1) & 1 else y
            pc = (1 - c) if k & 1 else c
            peer = 4 * px + 2 * py + pc
            for i in range(n):
                sem = i * (NDEV - 1) + k - 1
                src = srcs[i].at[peer] if scat[i] else srcs[i]
                send = pltpu.make_async_remote_copy(
                    src_ref=src, dst_ref=dsts[i].at[me],
                    send_sem=send_sems.at[sem], recv_sem=recv_sems.at[sem],
                    device_id=(px, py, pc), device_id_type=pl.DeviceIdType.MESH)
                send.start()
                recv = pltpu.make_async_remote_copy(
                    src_ref=src, dst_ref=dsts[i].at[peer],
                    send_sem=send_sems.at[sem], recv_sem=recv_sems.at[sem],
                    device_id=(px, py, pc), device_id_type=pl.DeviceIdType.MESH)
                remote.append((send, recv))
        for send, recv in remote:
            recv.wait_recv()
            send.wait_send()
        for cp in local:
            cp.wait()

    any_spec = pl.BlockSpec(memory_space=pl.ANY)
    return pl.pallas_call(
        body, name=name,
        in_specs=[any_spec] * n, out_specs=[any_spec] * n, out_shape=out_shape,
        scratch_shapes=[
            pltpu.SemaphoreType.DMA((n * (NDEV - 1),)),
            pltpu.SemaphoreType.DMA((n * (NDEV - 1),)),
            pltpu.SemaphoreType.DMA((n,)),
        ],
    )(*[a for a, _ in items])


def _gather_two_level(name, arrays):
    n = len(arrays)
    per = NDEV - 1

    def body(*refs):
        srcs = refs[:n]
        dsts = refs[n:2 * n]
        send_sems, recv_sems, loc_sems = refs[2 * n:]
        x = lax.axis_index("x")
        y = lax.axis_index("y")
        c = lax.axis_index("c")
        idx = lambda px, py, pc: 4 * px + 2 * py + pc
        me, sibling = (x, y, c), (x, y, 1 - c)
        chips = [(1 - x, y), (x, 1 - y), (1 - x, 1 - y)]

        def copy(i, k, block, to, src=None):
            slot = dsts[i].at[idx(*block)]
            return pltpu.make_async_remote_copy(
                src_ref=slot if src is None else src, dst_ref=slot,
                send_sem=send_sems.at[i * per + k], recv_sem=recv_sems.at[i * per + k],
                device_id=to, device_id_type=pl.DeviceIdType.MESH)

        mine, sent = [], []
        for i in range(n):
            cp = pltpu.make_async_copy(srcs[i], dsts[i].at[idx(*me)], loc_sems.at[i])
            cp.start()
            mine.append(cp)
            first = [copy(i, 0, me, sibling, src=srcs[i])]
            first += [copy(i, 1 + j, me, (*chip, c), src=srcs[i]) for j, chip in enumerate(chips)]
            for cp in first:
                cp.start()
            sent += first
        for j, chip in enumerate(chips):
            for i in range(n):
                copy(i, 1 + j, (*chip, c), me).wait_recv()
                fwd = copy(i, 4 + j, (*chip, c), sibling)
                fwd.start()
                sent.append(fwd)
        for i in range(n):
            copy(i, 0, sibling, me).wait_recv()
            for j, chip in enumerate(chips):
                copy(i, 4 + j, (*chip, 1 - c), me).wait_recv()
        for cp in sent:
            cp.wait_send()
        for cp in mine:
            cp.wait()

    any_spec = pl.BlockSpec(memory_space=pl.ANY)
    return pl.pallas_call(
        body, name=name, in_specs=[any_spec] * n, out_specs=[any_spec] * n,
        out_shape=[jax.ShapeDtypeStruct((NDEV,) + a.shape, a.dtype) for a in arrays],
        scratch_shapes=[pltpu.SemaphoreType.DMA((n * per,)), pltpu.SemaphoreType.DMA((n * per,)),
                        pltpu.SemaphoreType.DMA((n,))],
    )(*arrays)


HBM_SPEC = pl.BlockSpec(memory_space=pltpu.HBM)
SEM_SPEC = pl.BlockSpec(memory_space=pltpu.SEMAPHORE)
ANY_SPEC = pl.BlockSpec(memory_space=pl.ANY)
EFFECT = pltpu.SideEffectType.DATAFLOW_SIDE_EFFECTING


def _peer_of(k):
    x = lax.axis_index("x")
    y = lax.axis_index("y")
    c = lax.axis_index("c")
    px = (1 - x) if (k >> 2) & 1 else x
    py = (1 - y) if (k >> 1) & 1 else y
    pc = (1 - c) if k & 1 else c
    return (px, py, pc), 4 * px + 2 * py + pc


def _my_index():
    return 4 * lax.axis_index("x") + 2 * lax.axis_index("y") + lax.axis_index("c")


def _slot_shape(a, sc):
    return (NDEV,) + (a.shape[1:] if sc else a.shape)


def _divisor_tile(rows):
    return next((t for t in (512, 256, 176, 128, 64, 32, 16) if rows % t == 0), rows)


def _place(name, items, dtype=None):
    lands = []
    for idx, (a, sc) in enumerate(items):
        rows, cols = a.shape[-2:]
        tr = _divisor_tile(rows)
        out_dtype = a.dtype if dtype is None else dtype

        def body(s_ref, o_ref):
            o_ref[...] = s_ref[...].astype(o_ref.dtype)

        mine = pl.BlockSpec((None, tr, cols), lambda i: (_my_index(), i, 0))
        lands.append(pl.pallas_call(
            body, name=f"{name}_{idx}", grid=(rows // tr,),
            in_specs=[mine if sc else pl.BlockSpec((tr, cols), lambda i: (i, 0))],
            out_specs=mine,
            out_shape=jax.ShapeDtypeStruct(_slot_shape(a, sc), out_dtype),
            compiler_params=pltpu.CompilerParams(dimension_semantics=("parallel",)),
        )(a))
    return lands


def _xstart(name, items, lands, deps=()):
    n = len(items)
    scat = [sc for _, sc in items]
    srcs_in = [a for a, sc in items if sc]
    n_src = len(srcs_in)
    src_pos = {i: p for p, i in enumerate(i for i in range(n) if scat[i])}

    def body(*refs):
        srcs = refs[:n_src]
        lzs = refs[n_src:n_src + n]
        outs = refs[n_src + n + len(deps):]
        send_sems, recv_sems, token = outs[:n], outs[n:2 * n], outs[-1]
        me = _my_index()
        for i in range(n):
            for k in range(1, NDEV):
                peer_id, peer = _peer_of(k)
                src = srcs[src_pos[i]].at[peer] if scat[i] else lzs[i].at[me]
                pltpu.make_async_remote_copy(
                    src_ref=src, dst_ref=lzs[i].at[me],
                    send_sem=send_sems[i].at[k - 1], recv_sem=recv_sems[i].at[k - 1],
                    device_id=peer_id, device_id_type=pl.DeviceIdType.MESH).start()
        token[...] = jnp.zeros_like(token)

    sem = pltpu.SemaphoreType.DMA((NDEV - 1,))
    thru = srcs_in + list(lands)
    res = pl.pallas_call(
        body, name=name,
        in_specs=[HBM_SPEC] * len(thru) + [ANY_SPEC] * len(deps),
        out_specs=[SEM_SPEC] * (2 * n) + [HBM_SPEC] * len(thru) + [pl.BlockSpec(memory_space=pltpu.VMEM)],
        out_shape=[sem] * (2 * n) + [pltpu.HBM(t.shape, t.dtype) for t in thru]
        + [jax.ShapeDtypeStruct((8, 128), F32)],
        input_output_aliases={i: 2 * n + i for i in range(len(thru))},
        compiler_params=pltpu.CompilerParams(has_side_effects=EFFECT),
    )(*[pltpu.with_memory_space_constraint(t, pltpu.HBM) for t in thru], *deps)
    handles = [(res[i], res[n + i], res[2 * n + src_pos[i]] if scat[i] else None,
                res[2 * n + n_src + i]) for i in range(n)]
    return handles, res[-1]


def _xwait(name, handle, after):
    send_sem, recv_sem, src, land = handle
    sc = src is not None

    def body(*refs):
        land_ref = refs[1] if sc else refs[0]
        send_ref, recv_ref = (refs[2], refs[3]) if sc else (refs[1], refs[2])
        me = _my_index()
        for k in range(1, NDEV):
            peer_id, peer = _peer_of(k)
            cp = pltpu.make_async_remote_copy(
                src_ref=refs[0].at[peer] if sc else land_ref.at[me], dst_ref=land_ref.at[peer],
                send_sem=send_ref.at[k - 1], recv_sem=recv_ref.at[k - 1],
                device_id=peer_id, device_id_type=pl.DeviceIdType.MESH)
            cp.wait_send()
            cp.wait_recv()

    thru = ([src] if sc else []) + [land]
    return pl.pallas_call(
        body, name=name,
        in_specs=[HBM_SPEC] * len(thru) + [SEM_SPEC, SEM_SPEC, ANY_SPEC],
        out_specs=[HBM_SPEC] * len(thru),
        out_shape=[pltpu.HBM(t.shape, t.dtype) for t in thru],
        input_output_aliases={i: i for i in range(len(thru))},
        compiler_params=pltpu.CompilerParams(has_side_effects=EFFECT),
    )(*thru, send_sem, recv_sem, after)[-1]


def _mm(name, grid, in_specs, operands, out_specs, out_shape, contract, n_pairs, epilogue,
        acc_shape=None, vmem=VMEM_BIG, deps=(), group=1, a_cols=None, carry=()):
    nk = grid[2]
    n_carry = len(carry)

    def shard(ref, s, is_a):
        if group == 1:
            return ref[...]
        if is_a and a_cols is not None:
            return ref[:, s * a_cols:(s + 1) * a_cols]
        return ref[s]
    n_extra = len(operands) - 2 * n_pairs
    n_out = len(out_shape)
    n_in = len(operands) + len(deps)
    in_specs = list(in_specs) + [ANY_SPEC] * len(deps)
    operands = list(operands) + list(deps)

    def body(*refs):
        ab = refs[:2 * n_pairs]
        extras = refs[2 * n_pairs:2 * n_pairs + n_extra]
        outs = refs[n_in:n_in + n_out]
        kept = refs[n_in + n_out:n_in + n_out + n_carry]
        finish = (lambda acc: epilogue(acc, extras, outs, kept)) if n_carry else (
            lambda acc: epilogue(acc, extras, outs))
        dots = [(ab[2 * p], ab[2 * p + 1], s) for p in range(n_pairs) for s in range(group)]
        if nk == 1:
            part = None
            for a_ref, b_ref, s in dots:
                d = _dot(shard(a_ref, s, True), shard(b_ref, s, False), contract)
                part = d if part is None else part + d
            finish(part)
        else:
            acc_ref = refs[-1]
            k = pl.program_id(2)

            @pl.when(k == 0)
            def _():
                acc_ref[...] = jnp.zeros_like(acc_ref)

            for a_ref, b_ref, s in dots:
                acc_ref[...] += _dot(shard(a_ref, s, True), shard(b_ref, s, False), contract)

            @pl.when(k == nk - 1)
            def _():
                finish(acc_ref[...])

    scratch = list(carry) + ([pltpu.VMEM(acc_shape, F32)] if nk > 1 else [])
    semantics = ("arbitrary",) * 3 if n_carry else ("parallel", "parallel", "arbitrary")
    return pl.pallas_call(
        body, name=name, grid=grid, in_specs=in_specs, out_specs=out_specs, out_shape=out_shape,
        scratch_shapes=scratch,
        compiler_params=pltpu.CompilerParams(dimension_semantics=semantics, vmem_limit_bytes=vmem),
    )(*operands)


def _ep_cast(dtype):
    def ep(acc, extras, outs):
        outs[0][...] = acc.astype(dtype)
    return ep


def _ep_rms_bwd(want_bf16, n_rows_steps):
    def ep(acc, extras, outs, kept):
        h_ref, g_ref, dres_ref = extras
        gacc = kept[0]
        i = pl.program_id(0)
        v = h_ref[...]
        r = lax.rsqrt(jnp.mean(v * v, axis=-1, keepdims=True) + EPS)
        nrm = v * r
        dn = acc * g_ref[...]
        dh = dres_ref[...] + r * (dn - nrm * jnp.mean(dn * nrm, axis=-1, keepdims=True))
        outs[0][...] = dh
        if want_bf16:
            outs[1][...] = dh.astype(BF16)

        @pl.when(i == 0)
        def _():
            gacc[...] = jnp.zeros_like(gacc)

        gacc[...] += _fold8(acc * nrm)

        @pl.when(i == n_rows_steps - 1)
        def _():
            outs[-1][...] = jnp.sum(gacc[...], axis=0, keepdims=True)
    return ep


def _ep_resid_norm(acc, extras, outs):
    h = extras[0][...] + acc
    outs[0][...] = h
    r = lax.rsqrt(jnp.mean(h * h, axis=-1, keepdims=True) + EPS)
    outs[1][...] = (h * r * extras[1][...]).astype(BF16)


def _ep_swiglu_bwd(acc, extras, outs):
    outs[0][...] = (acc * extras[0][...].astype(F32)).astype(BF16)
    outs[1][...] = (acc * extras[1][...].astype(F32)).astype(BF16)


MXU_COLS_V7X = 256


def _col_chunks(n):
    return [slice(c, min(c + MXU_COLS_V7X, n)) for c in range(0, n, MXU_COLS_V7X)]


def _row_tile(T):
    return min(1024, T)


def _tn_rows(T):
    return min(2048, T)


WIDE = tuple(d for d in DILATIONS if d > 1)


LANES = 128


def _lane_tile(c):
    return slice(c * LANES, (c + 1) * LANES)


def _to_lane_tiles(scr, val):
    for c in range(scr.shape[0]):
        scr[c] = val[:, _lane_tile(c)]


def _emit_class_major(scr, refs, rows):
    for d, ref in zip(WIDE, refs):
        for r in range(d):
            for c in range(scr.shape[0]):
                ref[r, :, _lane_tile(c)] = scr[c, pl.ds(r, rows // d, stride=d), :].astype(ref.dtype)


def _mm_in(x, g, w_in, deps=()):
    T = x.shape[0]
    tm = _row_tile(T)
    nq = -(-3 * DA // N_IN)

    def body(x_ref, g_ref, w_ref, *rest):
        a_ref, z_ref, *rest = rest[len(deps):]
        scr = rest[-1]
        j = pl.program_id(1)

        @pl.when(j == 0)
        def _():
            v = x_ref[...]
            r = lax.rsqrt(jnp.mean(v * v, axis=-1, keepdims=True) + EPS)
            a_ref[...] = (v * r * g_ref[...]).astype(BF16)

        @pl.when(j >= nq)
        def _():
            z_ref[...] = _dot(a_ref[...], w_ref[...], NN).astype(BF16)

        @pl.when(j < nq)
        def _():
            av = a_ref[...]
            chunks = _col_chunks(N_IN)
            pending = _dot(av, w_ref[:, chunks[0]], NN)
            for ci, cols in enumerate(chunks):
                nxt = _dot(av, w_ref[:, chunks[ci + 1]], NN) if ci + 1 < len(chunks) else None
                z_ref[:, cols] = pending.astype(BF16)
                for c in range(cols.start // LANES, cols.stop // LANES):
                    scr[c] = pending[:, c * LANES - cols.start:(c + 1) * LANES - cols.start]
                    for d, ref in zip(WIDE, rest[:-1]):
                        for r in range(d):
                            ref[r, :, _lane_tile(c)] = scr[c, pl.ds(r, tm // d, stride=d), :].astype(BF16)
                pending = nxt

    cm_spec = lambda d: pl.BlockSpec((d, tm // d, N_IN), lambda i, j: (0, i, jnp.minimum(j, nq - 1)))
    row = pl.BlockSpec((tm, D), lambda i, j: (i, 0))
    return pl.pallas_call(
        body, name="mm_in", grid=(T // tm, NDEV),
        in_specs=[row, pl.BlockSpec((1, D), lambda i, j: (0, 0)),
                  pl.BlockSpec((None, D, N_IN), lambda i, j: (j, 0, 0))] + [ANY_SPEC] * len(deps),
        out_specs=[row, pl.BlockSpec((tm, N_IN), lambda i, j: (i, j))] + [cm_spec(d) for d in WIDE],
        out_shape=[jax.ShapeDtypeStruct((T, D), BF16), jax.ShapeDtypeStruct((T, DIN), BF16)]
        + [jax.ShapeDtypeStruct((d, T // d, nq * N_IN), BF16) for d in WIDE],
        scratch_shapes=[pltpu.VMEM((N_IN // LANES, tm, LANES), F32)],
        compiler_params=pltpu.CompilerParams(
            dimension_semantics=("parallel", "arbitrary"), vmem_limit_bytes=VMEM_BIG),
    )(x, g, w_in, *deps)


def _mm_out_bwd(dh1b, w_out, deps=()):
    T = dh1b.shape[0]
    tm = _row_tile(T)

    def body(dy_ref, w_ref, *rest):
        rest = rest[len(deps):]
        dom_ref, scr = rest[0], rest[-1]
        acc = _dot(dy_ref[...], w_ref[...], NT)
        dom_ref[...] = acc.astype(BF16)

        @pl.when(pl.program_id(1) == 0)
        def _():
            _to_lane_tiles(scr, acc)
            _emit_class_major(scr, rest[1:-1], tm)

    return pl.pallas_call(
        body, name="mm_out_bwd", grid=(T // tm, D // DA),
        in_specs=[pl.BlockSpec((tm, D), lambda i, j: (i, 0)),
                  pl.BlockSpec((DA, D), lambda i, j: (j, 0))] + [ANY_SPEC] * len(deps),
        out_specs=[pl.BlockSpec((tm, DA), lambda i, j: (i, j))]
        + [pl.BlockSpec((d, tm // d, DA), lambda i, j: (0, i, 0)) for d in WIDE],
        out_shape=[jax.ShapeDtypeStruct((T, D), BF16)]
        + [jax.ShapeDtypeStruct((d, T // d, DA), BF16) for d in WIDE],
        scratch_shapes=[pltpu.VMEM((DA // LANES, tm, LANES), F32)],
        compiler_params=pltpu.CompilerParams(
            dimension_semantics=("parallel", "arbitrary"), vmem_limit_bytes=VMEM_BIG),
    )(dh1b, w_out, *deps)


TM_FULL_ROW = 512


def _full_row_specs(tm):
    row = pl.BlockSpec((tm, D), lambda i, j, k: (i, 0))
    return row, pl.BlockSpec((1, D), lambda i, j, k: (0, 0))


def _mm_out(o_attn, o_conv, w_out, x, g_next):
    T = x.shape[0]
    tm = min(TM_FULL_ROW, T)
    row, vec = _full_row_specs(tm)
    return _mm(
        "mm_out", (T // tm, 1, 1),
        [pl.BlockSpec((tm, DA), lambda i, j, k: (i, 0)),
         pl.BlockSpec((DA, D), lambda i, j, k: (0, 0)),
         pl.BlockSpec((tm, DC), lambda i, j, k: (i, 0)),
         pl.BlockSpec((DC, D), lambda i, j, k: (1, 0)),
         row, vec],
        [o_attn, w_out, o_conv, w_out, x, g_next],
        [row, row],
        [jax.ShapeDtypeStruct((T, D), F32), jax.ShapeDtypeStruct((T, D), BF16)], NN, 2,
        _ep_resid_norm)


FF_TN = 512
FF_TK = 2 * N_FF


def _mm_gate_up(f, wg_t, wu_t):
    T = f.shape[0]
    tm = _row_tile(T)

    def body(f_ref, wg_ref, wu_ref, dg_ref, du_ref, a_ref):
        fv = f_ref[...]
        g = _dot(fv, wg_ref[...], NT)
        u = _dot(fv, wu_ref[...], NT)
        sg = _sigmoid(g)
        silu = g * sg
        dg_ref[...] = (u * (sg * (1.0 + g * (1.0 - sg)))).astype(BF16)
        du_ref[...] = silu.astype(BF16)
        a_ref[...] = (silu * u).astype(BF16)

    wspec = pl.BlockSpec((FF_TN, D), lambda i, j: (j, 0))
    ospec = pl.BlockSpec((tm, FF_TN), lambda i, j: (i, j))
    sh = jax.ShapeDtypeStruct((T, DFF), BF16)
    return pl.pallas_call(
        body, name="mm_gate_up", grid=(T // tm, DFF // FF_TN),
        in_specs=[pl.BlockSpec((tm, D), lambda i, j: (i, 0)), wspec, wspec],
        out_specs=[ospec, ospec, ospec], out_shape=[sh, sh, sh],
        compiler_params=pltpu.CompilerParams(
            dimension_semantics=("parallel", "parallel"), vmem_limit_bytes=VMEM_BIG),
    )(f, wg_t, wu_t)


def _mm_down(act, w_down, h1, g_next):
    T = h1.shape[0]
    tm = min(TM_FULL_ROW, T)
    row, vec = _full_row_specs(tm)
    return _mm(
        "mm_down", (T // tm, 1, DFF // FF_TK),
        [pl.BlockSpec((tm, FF_TK), lambda i, j, k: (i, k)),
         pl.BlockSpec((FF_TK, D), lambda i, j, k: (k, 0)),
         row, vec],
        [act, w_down, h1, g_next],
        [row, row],
        [jax.ShapeDtypeStruct((T, D), F32), jax.ShapeDtypeStruct((T, D), BF16)], NN, 1,
        _ep_resid_norm, acc_shape=(tm, D))


def _ple_loss(r, w_pgate, b_pgate, p, w_ple, h2, target, g_final):
    T = h2.shape[0]
    tm = min(256, T)
    nt = T // tm

    def body(r_ref, wg_ref, b_ref, p_ref, wp_ref, h2_ref, t_ref, g_ref,
             loss_ref, dh_ref, dpe_ref, dpg_ref, dgf_ref, dbp_ref, lacc, gacc, bacc):
        i = pl.program_id(0)
        gte = _sigmoid(_dot(r_ref[...], wg_ref[...], NN) + b_ref[...])
        pe = _dot(p_ref[...], wp_ref[...], NN)
        v = h2_ref[...] + pe * gte
        rr = lax.rsqrt(jnp.mean(v * v, axis=-1, keepdims=True) + EPS)
        nrm = v * rr
        g = g_ref[...]
        err = nrm * g - t_ref[...]
        dy = err * (1.0 / D)
        dn = dy * g
        dh = rr * (dn - nrm * jnp.mean(dn * nrm, axis=-1, keepdims=True))
        dh_ref[...] = dh
        dpe_ref[...] = (dh * gte).astype(BF16)
        dpg = dh * pe * gte * (1.0 - gte)
        dpg_ref[...] = dpg.astype(BF16)

        @pl.when(i == 0)
        def _():
            lacc[...] = jnp.zeros_like(lacc)
            gacc[...] = jnp.zeros_like(gacc)
            bacc[...] = jnp.zeros_like(bacc)

        lacc[...] += _fold8(err * err)
        gacc[...] += _fold8(dy * nrm)
        bacc[...] += _fold8(dpg)

        @pl.when(i == nt - 1)
        def _():
            tot = jnp.sum(jnp.sum(lacc[...], axis=0, keepdims=True), axis=1, keepdims=True)
            loss_ref[...] = jnp.broadcast_to(tot * (0.5 / D), (1, 128))
            dgf_ref[...] = jnp.sum(gacc[...], axis=0, keepdims=True)
            dbp_ref[...] = jnp.sum(bacc[...], axis=0, keepdims=True)

    row = pl.BlockSpec((tm, D), lambda i: (i, 0))
    vec = pl.BlockSpec((1, D), lambda i: (0, 0))
    return pl.pallas_call(
        body, name="ple_loss", grid=(nt,),
        in_specs=[row, pl.BlockSpec((D, D), lambda i: (0, 0)), vec,
                  pl.BlockSpec((tm, DPLE), lambda i: (i, 0)),
                  pl.BlockSpec((DPLE, D), lambda i: (0, 0)), row, row, vec],
        out_specs=[pl.BlockSpec((1, 128), lambda i: (0, 0)), row, row, row, vec, vec],
        out_shape=[jax.ShapeDtypeStruct((1, 128), F32), jax.ShapeDtypeStruct((T, D), F32),
                   jax.ShapeDtypeStruct((T, D), BF16), jax.ShapeDtypeStruct((T, D), BF16),
                   jax.ShapeDtypeStruct((1, D), F32), jax.ShapeDtypeStruct((1, D), F32)],
        scratch_shapes=[pltpu.VMEM((8, D), F32)] * 3,
        compiler_params=pltpu.CompilerParams(
            dimension_semantics=("arbitrary",), vmem_limit_bytes=VMEM_BIG),
    )(r, w_pgate, b_pgate, p, w_ple, h2, target, g_final)


def _mm_nt(name, dy, w, deps=()):
    T, n = dy.shape
    kdim = w.shape[0]
    tm = _row_tile(T)
    tn = 1024
    return _mm(
        name, (T // tm, kdim // tn, 1),
        [pl.BlockSpec((tm, n), lambda i, j, k: (i, 0)),
         pl.BlockSpec((tn, n), lambda i, j, k: (j, 0))],
        [dy, w],
        [pl.BlockSpec((tm, tn), lambda i, j, k: (i, j))],
        [jax.ShapeDtypeStruct((T, kdim), BF16)], NT, 1, _ep_cast(BF16), deps=deps)[0]


def _mm_down_bwd(dh2, w_down, g, u, deps=()):
    T = dh2.shape[0]
    tm = _tn_rows(T)
    gspec = pl.BlockSpec((tm, FF_TN), lambda i, j, k: (i, j))
    sh = jax.ShapeDtypeStruct((T, DFF), BF16)
    return _mm(
        "mm_down_bwd", (T // tm, DFF // FF_TN, 1),
        [pl.BlockSpec((tm, D), lambda i, j, k: (i, 0)),
         pl.BlockSpec((FF_TN, D), lambda i, j, k: (j, 0)),
         gspec, gspec],
        [dh2, w_down, g, u],
        [gspec, gspec], [sh, sh], NT, 1, _ep_swiglu_bwd, deps=deps)


def _mm_ffn_in_bwd(dg, wg_t, du, wu_t, deps=()):
    T = dg.shape[0]
    tm = _row_tile(T)
    tn = 1024
    aspec = pl.BlockSpec((tm, FF_TK), lambda i, j, k: (i, k))
    wspec = pl.BlockSpec((FF_TK, tn), lambda i, j, k: (k, j))
    return _mm(
        "mm_ffn_in_bwd", (T // tm, D // tn, DFF // FF_TK),
        [aspec, wspec, aspec, wspec], [dg, wg_t, du, wu_t],
        [pl.BlockSpec((tm, tn), lambda i, j, k: (i, j))],
        [jax.ShapeDtypeStruct((T, D), BF16)], NN, 2, _ep_cast(BF16), acc_shape=(tm, tn),
        deps=deps)[0]


def _mm_in_bwd(dz, w_in, x, g, dres, deps=()):
    T = dz.shape[0]
    tm = min(TM_FULL_ROW, T)
    sg = 2
    row, vec = _full_row_specs(tm)
    return _mm(
        "mm_in_bwd", (T // tm, 1, NDEV // sg),
        [pl.BlockSpec((tm, sg * N_IN), lambda i, j, k: (i, k)),
         pl.BlockSpec((sg, D, N_IN), lambda i, j, k: (k, 0, 0)),
         row, vec, row],
        [dz, w_in, x, g, dres],
        [row, vec],
        [jax.ShapeDtypeStruct((T, D), F32), jax.ShapeDtypeStruct((1, D), F32)], NT, 1,
        _ep_rms_bwd(False, T // tm), acc_shape=(tm, D), deps=deps, group=sg, a_cols=N_IN,
        carry=[pltpu.VMEM((8, D), F32)])


def _mm_pgate_bwd(dpg, w_pgate, h2, g, dres, deps=()):
    T = dpg.shape[0]
    tm = min(256, T)
    row, vec = _full_row_specs(tm)
    return _mm(
        "mm_pgate_bwd", (T // tm, 1, 1),
        [row, pl.BlockSpec((D, D), lambda i, j, k: (0, 0)), row, vec, row],
        [dpg, w_pgate, h2, g, dres],
        [row, row, vec],
        [jax.ShapeDtypeStruct((T, D), F32), jax.ShapeDtypeStruct((T, D), BF16),
         jax.ShapeDtypeStruct((1, D), F32)], NT, 1,
        _ep_rms_bwd(True, T // tm), deps=deps, carry=[pltpu.VMEM((8, D), F32)])


def _mm_tn(name, a, b, tj=None):
    T, idim = a.shape
    jdim = b.shape[1]
    tt = _row_tile(T)
    ti = min(idim, 1024)
    tj = jdim if tj is None else tj
    return _mm(
        name, (idim // ti, jdim // tj, T // tt),
        [pl.BlockSpec((tt, ti), lambda i, j, k: (k, i)),
         pl.BlockSpec((tt, tj), lambda i, j, k: (k, j))],
        [a, b],
        [pl.BlockSpec((ti, tj), lambda i, j, k: (i, j))],
        [jax.ShapeDtypeStruct((idim, jdim), BF16)], TN, 1, _ep_cast(BF16), acc_shape=(ti, tj))[0]


def _mm_tn_cols(name, a, b, ncol):
    T, idim = a.shape
    tt = _tn_rows(T)
    return _mm(
        name, (1, NDEV, T // tt),
        [pl.BlockSpec((tt, idim), lambda i, j, k: (k, 0)),
         pl.BlockSpec((tt, ncol), lambda i, j, k: (k, j))],
        [a, b],
        [pl.BlockSpec((None, idim, ncol), lambda i, j, k: (j, 0, 0))],
        [jax.ShapeDtypeStruct((NDEV, idim, ncol), BF16)], TN, 1, _ep_cast(BF16),
        acc_shape=(idim, ncol))[0]


def _mm_tn_ff(name, a, b):
    T = b.shape[0]
    tt = _tn_rows(T)
    return _mm(
        name, (DFF // FF_TN, 1, T // tt),
        [pl.BlockSpec((tt, FF_TN), lambda i, j, k: (k, i)),
         pl.BlockSpec((tt, D), lambda i, j, k: (k, 0))],
        [a, b],
        [pl.BlockSpec((FF_TN, D), lambda i, j, k: (i, 0))],
        [jax.ShapeDtypeStruct((DFF, D), BF16)], TN, 1, _ep_cast(BF16),
        acc_shape=(FF_TN, D))[0]


TR = 256


def _rows(T):
    return min(TR, T)


def _fold8(v):
    return jnp.sum(v.reshape(v.shape[0] // 8, 8, v.shape[1]), axis=0)


def _rms_bwd(name, dn_out, h, g, dres, want_bf16):
    T = h.shape[0]
    tr = _rows(T)
    nt = T // tr

    def body(dy_ref, h_ref, g_ref, dres_ref, *rest):
        if want_bf16:
            dh_ref, dhb_ref, dg_ref, acc = rest
        else:
            dh_ref, dg_ref, acc = rest
        i = pl.program_id(0)
        v = h_ref[...]
        r = lax.rsqrt(jnp.mean(v * v, axis=-1, keepdims=True) + EPS)
        nrm = v * r
        dy = dy_ref[...].astype(F32)
        dn = dy * g_ref[...]
        dh = dres_ref[...] + r * (dn - nrm * jnp.mean(dn * nrm, axis=-1, keepdims=True))
        dh_ref[...] = dh
        if want_bf16:
            dhb_ref[...] = dh.astype(BF16)

        @pl.when(i == 0)
        def _():
            acc[...] = jnp.zeros_like(acc)

        acc[...] += _fold8(dy * nrm)

        @pl.when(i == nt - 1)
        def _():
            dg_ref[...] = jnp.sum(acc[...], axis=0, keepdims=True)

    tile = pl.BlockSpec((tr, D), lambda i: (i, 0))
    vec = pl.BlockSpec((1, D), lambda i: (0, 0))
    out_specs = [tile] + ([tile] if want_bf16 else []) + [vec]
    out_shape = ([jax.ShapeDtypeStruct((T, D), F32)]
                 + ([jax.ShapeDtypeStruct((T, D), BF16)] if want_bf16 else [])
                 + [jax.ShapeDtypeStruct((1, D), F32)])
    return pl.pallas_call(
        body, name=name, grid=(nt,),
        in_specs=[tile, tile, vec, tile], out_specs=out_specs, out_shape=out_shape,
        scratch_shapes=[pltpu.VMEM((8, D), F32)],
        compiler_params=pltpu.CompilerParams(dimension_semantics=("arbitrary",)),
    )(dn_out, h, g, dres)


def _band_masks():
    qi = lax.broadcasted_iota(jnp.int32, (BLK, BLK), 0)
    kj = lax.broadcasted_iota(jnp.int32, (BLK, BLK), 1)
    return kj >= qi, kj <= qi


AQ = 4


def _cm_spec(d, col, nblk, rowmap=lambda n: n):
    if d == 1:
        return pl.BlockSpec((nblk * BLK, DA), lambda r, n: (rowmap(n), col))
    return pl.BlockSpec((None, nblk * BLK, DA), lambda r, n: (r, rowmap(n), col))


def _cm_shape(d, T, dtype):
    return jax.ShapeDtypeStruct((T, DA) if d == 1 else (d, T // d, DA), dtype)


def _blk(b):
    return slice(b * BLK, (b + 1) * BLK)


HEADS = tuple(slice(h * DH, (h + 1) * DH) for h in range(NH))


def _attn_fwd(name, zsrc, d, T):
    nb = T // d // BLK
    aq = min(AQ, nb)
    scale = DH ** -0.5

    def body(q_ref, kp_ref, kc_ref, vp_ref, vc_ref, o_ref, l_ref):
        n = pl.program_id(1)
        band_prev, cur_ok = _band_masks()
        for b in range(aq):
            kp = (lambda sl: kp_ref[:, sl]) if b == 0 else (lambda sl, b=b: kc_ref[_blk(b - 1), sl])
            vp = (lambda sl: vp_ref[:, sl]) if b == 0 else (lambda sl, b=b: vc_ref[_blk(b - 1), sl])
            prev_ok = band_prev & (n > 0) if b == 0 else band_prev
            rows = _blk(b)
            s = [(jnp.where(prev_ok, _dot(q_ref[rows, sl], kp(sl), NT) * scale, NEG),
                  jnp.where(cur_ok, _dot(q_ref[rows, sl], kc_ref[rows, sl], NT) * scale, NEG))
                 for sl in HEADS]
            m = [jnp.maximum(jnp.max(sp, axis=1, keepdims=True), jnp.max(sc, axis=1, keepdims=True))
                 for sp, sc in s]
            p = [(jnp.exp(sp - mh), jnp.exp(sc - mh)) for (sp, sc), mh in zip(s, m)]
            den = [jnp.sum(pp, axis=1, keepdims=True) + jnp.sum(pc, axis=1, keepdims=True)
                   for pp, pc in p]
            o = [_dot(pp.astype(BF16), vp(sl), NN) + _dot(pc.astype(BF16), vc_ref[rows, sl], NN)
                 for (pp, pc), sl in zip(p, HEADS)]
            o_ref[rows, :] = jnp.concatenate(
                [(oh / dh).astype(BF16) for oh, dh in zip(o, den)], axis=1)
            l_ref[rows, :] = jnp.concatenate(
                [jnp.broadcast_to(mh + jnp.log(dh), (BLK, DH)) for mh, dh in zip(m, den)], axis=1)

    halo = lambda n: jnp.maximum(aq * n - 1, 0)
    return pl.pallas_call(
        body, name=name, grid=(d, nb // aq),
        in_specs=[_cm_spec(d, 0, aq), _cm_spec(d, 1, 1, halo), _cm_spec(d, 1, aq),
                  _cm_spec(d, 2, 1, halo), _cm_spec(d, 2, aq)],
        out_specs=[_cm_spec(d, 0, aq)] * 2,
        out_shape=[_cm_shape(d, T, BF16), _cm_shape(d, T, F32)],
        compiler_params=pltpu.CompilerParams(dimension_semantics=("parallel", "parallel")),
    )(zsrc, zsrc, zsrc, zsrc, zsrc)


def _cm_tile(d, tr):
    if d == 1:
        return pl.BlockSpec((tr, DA), lambda i: (i, 0))
    return pl.BlockSpec((d, tr // d, DA), lambda i: (0, i, 0))


def _attn_combine(outs, lses, T):
    tr = _rows(T)

    def body(*refs):
        o_in, l_in = refs[:3], refs[3:6]
        o_ref, l_ref = refs[6:8]
        o_cm, l_cm = refs[8:8 + len(WIDE)], refs[8 + len(WIDE):8 + 2 * len(WIDE)]
        so, sl, so_all, sl_all = refs[8 + 2 * len(WIDE):]
        for c in range(DA // LANES):
            lt = _lane_tile(c)
            os_, ls_ = [o_in[0][:, lt].astype(F32)], [l_in[0][:, lt]]
            for w, d in enumerate(WIDE):
                for r in range(d):
                    so[w, c, pl.ds(r, tr // d, stride=d), :] = o_in[1 + w][r, :, lt].astype(F32)
                    sl[w, c, pl.ds(r, tr // d, stride=d), :] = l_in[1 + w][r, :, lt]
                os_.append(so[w, c])
                ls_.append(sl[w, c])
            la, lb, lc = ls_
            m = jnp.maximum(jnp.maximum(la, lb), lc)
            ea, eb, ec = jnp.exp(la - m), jnp.exp(lb - m), jnp.exp(lc - m)
            s = ea + eb + ec
            o = (ea * os_[0] + eb * os_[1] + ec * os_[2]) / s
            lse = m + jnp.log(s)
            o_ref[:, lt] = o.astype(BF16)
            l_ref[:, lt] = lse
            so_all[c] = o
            sl_all[c] = lse
        _emit_class_major(so_all, o_cm, tr)
        _emit_class_major(sl_all, l_cm, tr)

    specs = [_cm_tile(d, tr) for d in DILATIONS]
    wide = [_cm_tile(d, tr) for d in WIDE]
    return pl.pallas_call(
        body, name="attn_combine", grid=(T // tr,),
        in_specs=specs + specs,
        out_specs=[specs[0], specs[0]] + wide + wide,
        out_shape=[_cm_shape(1, T, BF16), _cm_shape(1, T, F32)]
        + [_cm_shape(d, T, BF16) for d in WIDE] + [_cm_shape(d, T, F32) for d in WIDE],
        scratch_shapes=[pltpu.VMEM((len(WIDE), DA // LANES, tr, LANES), F32)] * 2
        + [pltpu.VMEM((DA // LANES, tr, LANES), F32)] * 2,
        compiler_params=pltpu.CompilerParams(
            dimension_semantics=("parallel",), vmem_limit_bytes=VMEM_MID),
    )(*outs, *lses)


def _attn_bwd_q(name, zsrc, dosrc, osrc, lsrc, d, T):
    nb = T // d // BLK
    aq = min(AQ, nb)
    scale = DH ** -0.5

    def body(q_ref, kp_ref, kc_ref, vp_ref, vc_ref, do_ref, o_ref, l_ref, dq_ref):
        n = pl.program_id(1)
        band_prev, cur_ok = _band_masks()
        for b in range(aq):
            kp = (lambda sl: kp_ref[:, sl]) if b == 0 else (lambda sl, b=b: kc_ref[_blk(b - 1), sl])
            vp = (lambda sl: vp_ref[:, sl]) if b == 0 else (lambda sl, b=b: vc_ref[_blk(b - 1), sl])
            prev_ok = band_prev & (n > 0) if b == 0 else band_prev
            rows = _blk(b)
            s = [(_dot(q_ref[rows, sl], kp(sl), NT), _dot(q_ref[rows, sl], kc_ref[rows, sl], NT))
                 for sl in HEADS]
            dp = [(_dot(do_ref[rows, sl], vp(sl), NT), _dot(do_ref[rows, sl], vc_ref[rows, sl], NT))
                  for sl in HEADS]
            delta = [jnp.sum(do_ref[rows, sl].astype(F32) * o_ref[rows, sl].astype(F32), axis=1,
                             keepdims=True) for sl in HEADS]
            p = [(jnp.exp(jnp.where(prev_ok, sp * scale - l_ref[rows, sl], NEG)),
                  jnp.exp(jnp.where(cur_ok, sc * scale - l_ref[rows, sl], NEG)))
                 for (sp, sc), sl in zip(s, HEADS)]
            ds = [((pp * (dpp - dl) * scale).astype(BF16), (pc * (dpc - dl) * scale).astype(BF16))
                  for (pp, pc), (dpp, dpc), dl in zip(p, dp, delta)]
            dq = [_dot(dsp, kp(sl), NN) + _dot(dsc, kc_ref[rows, sl], NN)
                  for (dsp, dsc), sl in zip(ds, HEADS)]
            dq_ref[rows, :] = jnp.concatenate([v.astype(BF16) for v in dq], axis=1)

    halo = lambda n: jnp.maximum(aq * n - 1, 0)
    own = _cm_spec(d, 0, aq)
    return pl.pallas_call(
        body, name=name, grid=(d, nb // aq),
        in_specs=[own, _cm_spec(d, 1, 1, halo), _cm_spec(d, 1, aq), _cm_spec(d, 2, 1, halo),
                  _cm_spec(d, 2, aq), own, own, own],
        out_specs=own, out_shape=_cm_shape(d, T, BF16),
        compiler_params=pltpu.CompilerParams(dimension_semantics=("parallel", "parallel")),
    )(zsrc, zsrc, zsrc, zsrc, zsrc, dosrc, osrc, lsrc)


def _attn_bwd_kv(name, zsrc, dosrc, osrc, lsrc, d, T):
    nb = T // d // BLK
    aq = min(AQ, nb)
    nsteps = nb // aq
    scale = DH ** -0.5

    def body(k_ref, v_ref, q_ref, qn_ref, do_ref, don_ref, o_ref, on_ref, l_ref, ln_ref,
             dk_ref, dv_ref):
        j = pl.program_id(1)
        band_next, own_ok = _band_masks()
        for b in range(aq):
            rows = _blk(b)
            last = b == aq - 1
            pick = lambda cur, halo: ((lambda sl: halo[:, sl]) if last
                                      else (lambda sl, b=b: cur[_blk(b + 1), sl]))
            qb, dob, ob, lb = (pick(q_ref, qn_ref), pick(do_ref, don_ref), pick(o_ref, on_ref),
                               pick(l_ref, ln_ref))
            next_ok = band_next & (j < nsteps - 1) if last else band_next
            s = [(_dot(q_ref[rows, sl], k_ref[rows, sl], NT), _dot(qb(sl), k_ref[rows, sl], NT))
                 for sl in HEADS]
            dp = [(_dot(do_ref[rows, sl], v_ref[rows, sl], NT), _dot(dob(sl), v_ref[rows, sl], NT))
                  for sl in HEADS]
            delta = [(jnp.sum(do_ref[rows, sl].astype(F32) * o_ref[rows, sl].astype(F32), axis=1,
                              keepdims=True),
                      jnp.sum(dob(sl).astype(F32) * ob(sl).astype(F32), axis=1, keepdims=True))
                     for sl in HEADS]
            p = [(jnp.exp(jnp.where(own_ok, sa * scale - l_ref[rows, sl], NEG)),
                  jnp.exp(jnp.where(next_ok, sb * scale - lb(sl), NEG)))
                 for (sa, sb), sl in zip(s, HEADS)]
            dv = [_dot(pa.astype(BF16), do_ref[rows, sl], TN) + _dot(pb.astype(BF16), dob(sl), TN)
                  for (pa, pb), sl in zip(p, HEADS)]
            ds = [((pa * (dpa - da) * scale).astype(BF16), (pb * (dpb - db) * scale).astype(BF16))
                  for (pa, pb), (dpa, dpb), (da, db) in zip(p, dp, delta)]
            dk = [_dot(dsa, q_ref[rows, sl], TN) + _dot(dsb, qb(sl), TN)
                  for (dsa, dsb), sl in zip(ds, HEADS)]
            dk_ref[rows, :] = jnp.concatenate([v.astype(BF16) for v in dk], axis=1)
            dv_ref[rows, :] = jnp.concatenate([v.astype(BF16) for v in dv], axis=1)

    halo = lambda j: jnp.minimum(aq * (j + 1), nb - 1)
    own, own_n = _cm_spec(d, 0, aq), _cm_spec(d, 0, 1, halo)
    sh = _cm_shape(d, T, BF16)
    return pl.pallas_call(
        body, name=name, grid=(d, nsteps),
        in_specs=[_cm_spec(d, 1, aq), _cm_spec(d, 2, aq), own, own_n, own, own_n, own, own_n,
                  own, own_n],
        out_specs=[own, own], out_shape=[sh, sh],
        compiler_params=pltpu.CompilerParams(dimension_semantics=("parallel", "parallel")),
    )(zsrc, zsrc, zsrc, zsrc, dosrc, dosrc, osrc, osrc, lsrc, lsrc)


def _dz_assemble(dqs, dks, dvs, dcvg, T):
    tr = _rows(T)
    nb = len(DILATIONS)

    def body(*refs):
        cvg_ref, dz_ref, scr = refs[3 * nb], refs[3 * nb + 1], refs[3 * nb + 2]
        for g in range(3):
            parts = refs[g * nb:(g + 1) * nb]
            for c in range(DA // LANES):
                lt = _lane_tile(c)
                scr[g, c] = parts[0][:, lt].astype(F32)
                for w, d in enumerate(WIDE):
                    for r in range(d):
                        rows = pl.ds(r, tr // d, stride=d)
                        scr[g, c, rows, :] = scr[g, c, rows, :] + parts[1 + w][r, :, lt].astype(F32)
                dz_ref[:, g * DA + c * LANES:g * DA + (c + 1) * LANES] = scr[g, c].astype(BF16)
        dz_ref[:, 3 * DA:] = cvg_ref[...]

    specs = [_cm_tile(d, tr) for d in DILATIONS]
    return pl.pallas_call(
        body, name="dz_assemble", grid=(T // tr,),
        in_specs=specs * 3 + [pl.BlockSpec((tr, 2 * DC), lambda i: (i, 0))],
        out_specs=pl.BlockSpec((tr, DIN), lambda i: (i, 0)),
        out_shape=jax.ShapeDtypeStruct((T, DIN), BF16),
        scratch_shapes=[pltpu.VMEM((3, DA // LANES, tr, LANES), F32)],
        compiler_params=pltpu.CompilerParams(
            dimension_semantics=("parallel",), vmem_limit_bytes=VMEM_MID),
    )(*dqs, *dks, *dvs, dcvg)


CT = 256
HALO = 32
RC = 32


def _conv_fwd(z, w_dw, b_dw, g_ln, b_ln):
    T = z.shape[0]
    ct = min(CT, T)
    nt = T // ct
    hb = ct // HALO

    def body(cv_ref, cg_ref, cvp_ref, cgp_ref, w_ref, bdw_ref, g_ref, b_ref, oc_ref, y_ref, ubuf, ush):
        i = pl.program_id(0)
        up = cvp_ref[...].astype(F32) * _sigmoid(cgp_ref[...].astype(F32))
        ubuf[0:HALO, :] = jnp.where(i > 0, up, 0.0)
        ubuf[HALO:, :] = cv_ref[...].astype(F32) * _sigmoid(cg_ref[...].astype(F32))
        for b in range(8):
            ush[b] = ubuf[pl.ds(8 - b, ct + 24), :]

        def chunk(ci, carry):
            r0 = pl.multiple_of(ci * RC, RC)
            acc = jnp.broadcast_to(bdw_ref[...], (RC, DC))
            for s in range(CW):
                a, b = divmod(s, 8)
                acc = acc + w_ref[CW - 1 - s:CW - s, :] * ush[b, pl.ds(r0 + 24 - 8 * a, RC), :]
            y_ref[pl.ds(r0, RC), :] = acc
            mu = jnp.mean(acc, axis=-1, keepdims=True)
            cen = acc - mu
            var = jnp.mean(cen * cen, axis=-1, keepdims=True)
            ln = cen * lax.rsqrt(var + EPS) * g_ref[...] + b_ref[...]
            oc_ref[pl.ds(r0, RC), :] = (ln * _sigmoid(ln)).astype(BF16)
            return carry

        lax.fori_loop(0, ct // RC, chunk, 0)

    cur = lambda col: pl.BlockSpec((ct, DC), lambda i: (i, col))
    prv = lambda col: pl.BlockSpec((HALO, DC), lambda i: (jnp.maximum(i * hb - 1, 0), col))
    vec = pl.BlockSpec((1, DC), lambda i: (0, 0))
    return pl.pallas_call(
        body, name="conv_fwd", grid=(nt,),
        in_specs=[cur(3), cur(4), prv(3), prv(4), pl.BlockSpec((CW, DC), lambda i: (0, 0)),
                  vec, vec, vec],
        out_specs=[pl.BlockSpec((ct, DC), lambda i: (i, 0))] * 2,
        out_shape=[jax.ShapeDtypeStruct((T, DC), BF16), jax.ShapeDtypeStruct((T, DC), F32)],
        scratch_shapes=[pltpu.VMEM((ct + HALO, DC), F32), pltpu.VMEM((8, ct + 24, DC), F32)],
        compiler_params=pltpu.CompilerParams(
            dimension_semantics=("parallel",), vmem_limit_bytes=VMEM_MID),
    )(z, z, z, z, w_dw, b_dw, g_ln, b_ln)


def _conv_bwd(z, dom, y, w_dw, g_ln, b_ln):
    T = z.shape[0]
    ct = min(CT, T)
    nt = T // ct
    hb = ct // HALO
    last_halo = T // HALO - 1

    def ln_bwd(yv, dov, g_ref, b_ref):
        mu = jnp.mean(yv, axis=-1, keepdims=True)
        cen = yv - mu
        rstd = lax.rsqrt(jnp.mean(cen * cen, axis=-1, keepdims=True) + EPS)
        xhat = cen * rstd
        ln = xhat * g_ref[...] + b_ref[...]
        sg = _sigmoid(ln)
        dln = dov * (sg * (1.0 + ln * (1.0 - sg)))
        dxh = dln * g_ref[...]
        dy = rstd * (dxh - jnp.mean(dxh, axis=-1, keepdims=True)
                     - xhat * jnp.mean(dxh * xhat, axis=-1, keepdims=True))
        return dy, dln, xhat

    def body(do_ref, don_ref, y_ref, yn_ref, cv_ref, cg_ref, cvp_ref, cgp_ref, w_ref, g_ref, b_ref,
             dcvg_ref, dw_ref, dbdw_ref, dg_ref, db_ref,
             dybuf, dysh, ubuf, ush, dwacc, vacc):
        i = pl.program_id(0)

        @pl.when(i == 0)
        def _():
            dwacc[...] = jnp.zeros_like(dwacc)
            vacc[...] = jnp.zeros_like(vacc)

        def ln_chunk(ci, carry):
            r0 = pl.multiple_of(ci * RC, RC)
            dy, dln, xhat = ln_bwd(y_ref[pl.ds(r0, RC), :], do_ref[pl.ds(r0, RC), :].astype(F32),
                                   g_ref, b_ref)
            dybuf[pl.ds(r0, RC), :] = dy
            vacc[0] += _fold8(dy)
            vacc[1] += _fold8(dln * xhat)
            vacc[2] += _fold8(dln)
            return carry

        lax.fori_loop(0, ct // RC, ln_chunk, 0)
        dyn, _, _ = ln_bwd(yn_ref[...], don_ref[...].astype(F32), g_ref, b_ref)
        dybuf[ct:, :] = jnp.where(i < nt - 1, dyn, 0.0)
        for b in range(8):
            dysh[b] = dybuf[pl.ds(b, ct + 24), :]

        up = cvp_ref[...].astype(F32) * _sigmoid(cgp_ref[...].astype(F32))
        ubuf[0:HALO, :] = jnp.where(i > 0, up, 0.0)
        ubuf[HALO:, :] = cv_ref[...].astype(F32) * _sigmoid(cg_ref[...].astype(F32))
        for b in range(8):
            ush[b] = ubuf[pl.ds(8 - b, ct + 24), :]

        def chunk(ci, carry):
            r0 = pl.multiple_of(ci * RC, RC)
            dy = dybuf[pl.ds(r0, RC), :]
            du = jnp.zeros((RC, DC), F32)
            for s in range(CW):
                a, b = divmod(s, 8)
                du = du + w_ref[CW - 1 - s:CW - s, :] * dysh[b, pl.ds(r0 + 8 * a, RC), :]
                dwacc[CW - 1 - s] += _fold8(dy * ush[b, pl.ds(r0 + 24 - 8 * a, RC), :])
            cv = cv_ref[pl.ds(r0, RC), :].astype(F32)
            sg = _sigmoid(cg_ref[pl.ds(r0, RC), :].astype(F32))
            dcvg_ref[pl.ds(r0, RC), 0:DC] = (du * sg).astype(BF16)
            dcvg_ref[pl.ds(r0, RC), DC:2 * DC] = (du * cv * sg * (1.0 - sg)).astype(BF16)
            return carry

        lax.fori_loop(0, ct // RC, chunk, 0)

        @pl.when(i == nt - 1)
        def _():
            dw_ref[...] = jnp.sum(dwacc[...], axis=1)
            dbdw_ref[...] = jnp.sum(vacc[0], axis=0, keepdims=True)
            dg_ref[...] = jnp.sum(vacc[1], axis=0, keepdims=True)
            db_ref[...] = jnp.sum(vacc[2], axis=0, keepdims=True)

    cur = lambda col: pl.BlockSpec((ct, DC), lambda i: (i, col))
    prv = lambda col: pl.BlockSpec((HALO, DC), lambda i: (jnp.maximum(i * hb - 1, 0), col))
    nxt = lambda col: pl.BlockSpec((HALO, DC), lambda i: (jnp.minimum((i + 1) * hb, last_halo), col))
    vec = pl.BlockSpec((1, DC), lambda i: (0, 0))
    tile = pl.BlockSpec((ct, DC), lambda i: (i, 0))
    return pl.pallas_call(
        body, name="conv_bwd", grid=(nt,),
        in_specs=[cur(1), nxt(1), cur(0), nxt(0), cur(3), cur(4), prv(3), prv(4),
                  pl.BlockSpec((CW, DC), lambda i: (0, 0)), vec, vec],
        out_specs=[pl.BlockSpec((ct, 2 * DC), lambda i: (i, 0)),
                   pl.BlockSpec((CW, DC), lambda i: (0, 0)), vec, vec, vec],
        out_shape=[jax.ShapeDtypeStruct((T, 2 * DC), BF16),
                   jax.ShapeDtypeStruct((CW, DC), F32), jax.ShapeDtypeStruct((1, DC), F32),
                   jax.ShapeDtypeStruct((1, DC), F32), jax.ShapeDtypeStruct((1, DC), F32)],
        scratch_shapes=[pltpu.VMEM((ct + HALO, DC), F32), pltpu.VMEM((8, ct + 24, DC), F32),
                        pltpu.VMEM((ct + HALO, DC), F32), pltpu.VMEM((8, ct + 24, DC), F32),
                        pltpu.VMEM((CW, 8, DC), F32), pltpu.VMEM((3, 8, DC), F32)],
        compiler_params=pltpu.CompilerParams(
            dimension_semantics=("arbitrary",), vmem_limit_bytes=VMEM_BIG),
    )(dom, dom, y, y, z, z, z, z, w_dw, g_ln, b_ln)


def _adam_math(w, g, m, v):
    m = ADAM_B1 * m + (1.0 - ADAM_B1) * g
    v = ADAM_B2 * v + (1.0 - ADAM_B2) * (g * g)
    m_hat = m / (1.0 - ADAM_B1 ** ADAM_STEP)
    v_hat = v / (1.0 - ADAM_B2 ** ADAM_STEP)
    delta = -ADAM_LR * (m_hat / (jnp.sqrt(v_hat) + ADAM_EPS) + ADAM_WD * w)
    return delta, m, v


def _adam(name, slots, w, m, v):
    rows, cols = w.shape
    tr = next(t for t in (256, 176, 128, 64, 32, 16, 8, rows) if rows % t == 0)

    def body(s_ref, w_ref, m_ref, v_ref, g_out, d_out, m_out, v_out):
        g = s_ref[0].astype(F32)
        for s in range(1, NDEV):
            g = g + s_ref[s].astype(F32)
        delta, mn, vn = _adam_math(w_ref[...], g, m_ref[...], v_ref[...])
        g_out[...] = g
        d_out[...] = delta
        m_out[...] = mn
        v_out[...] = vn

    tile = pl.BlockSpec((tr, cols), lambda i: (i, 0))
    sh = jax.ShapeDtypeStruct((rows, cols), F32)
    return pl.pallas_call(
        body, name=name, grid=(rows // tr,),
        in_specs=[pl.BlockSpec((NDEV, tr, cols), lambda i: (0, i, 0)), tile, tile, tile],
        out_specs=[tile] * 4, out_shape=[sh] * 4,
        compiler_params=pltpu.CompilerParams(
            dimension_semantics=("parallel",), vmem_limit_bytes=VMEM_MID),
    )(slots, w, m, v)


SMALL_NAMES = ("g_mix", "b_dw", "g_conv_ln", "b_conv_ln", "g_ffn", "g_ple", "b_pgate", "g_final")


def _pack_small(vecs, w_dw_full):
    rows = [jnp.pad(v.reshape(1, -1), ((0, 0), (0, SMALL_W - v.size))) for v in vecs]
    rows.append(jnp.pad(w_dw_full, ((0, 0), (0, SMALL_W - DC))))
    rows.append(jnp.zeros((SMALL_ROWS - len(vecs) - CW, SMALL_W), F32))
    return jnp.concatenate(rows, axis=0)


def kernel(x, p, g_mix, w_in, w_dw, b_dw, g_conv_ln, b_conv_ln, w_out, g_ffn, w_gate, w_up, w_down, g_ple, w_pgate, b_pgate, w_ple, g_final, loss_target, m_g_mix, m_w_in, m_w_dw, m_b_dw, m_g_conv_ln, m_b_conv_ln, m_w_out, m_g_ffn, m_w_gate, m_w_up, m_w_down, m_g_ple, m_w_pgate, m_b_pgate, m_w_ple, m_g_final, v_g_mix, v_w_in, v_w_dw, v_b_dw, v_g_conv_ln, v_b_conv_ln, v_w_out, v_g_ffn, v_w_gate, v_w_up, v_w_down, v_g_ple, v_w_pgate, v_b_pgate, v_w_ple, v_g_final):
    T = x.shape[1]
    me = 4 * lax.axis_index("x") + 2 * lax.axis_index("y") + lax.axis_index("c")
    xs = x.reshape(T, D)
    ps = p.reshape(T, DPLE).astype(BF16)
    tgt = loss_target.reshape(T, D)
    g_final2 = g_final.reshape(1, D)

    tr_names = ("w_gate", "w_up")
    big = dict(w_in=w_in[0], w_out=w_out[0], w_gate=w_gate[0].T, w_up=w_up[0].T, w_down=w_down[0],
               w_pgate=w_pgate[0], w_ple=w_ple[0])
    order = ("w_in", "w_out", "w_gate", "w_up", "w_down", "w_pgate", "w_ple")
    w_dw_g, w_in_f = _gather_two_level(
        "gather_first", [w_dw.reshape(CW, DC // NDEV), big["w_in"].astype(BF16)])
    w_dw_f = w_dw_g.transpose(1, 0, 2).reshape(CW, DC)
    later = order[1:]
    lands = _place("gather_place", [(big[n], False) for n in later], dtype=BF16)
    g_handles, g_token = _xstart("gather_start", [(None, False)] * len(later), lands, deps=[w_in_f])
    G = dict(zip(later, g_handles))

    a, z, *z_wide = _mm_in(xs, g_mix, w_in_f, deps=[g_token])
    zsrc = dict(zip(DILATIONS, [z] + z_wide))
    br = [_attn_fwd(f"attn_fwd_d{d}", zsrc[d], d, T) for d in DILATIONS]
    comb = list(_attn_combine([b[0] for b in br], [b[1] for b in br], T))
    o_attn, lse = comb[0], comb[1]
    osrc = dict(zip(DILATIONS, [o_attn] + comb[2:2 + len(WIDE)]))
    lsrc = dict(zip(DILATIONS, [lse] + comb[2 + len(WIDE):]))
    o_conv, y_conv = _conv_fwd(z, w_dw_f, b_dw, g_conv_ln, b_conv_ln)
    w_out_f = _xwait("gather_wait_w_out", G["w_out"], o_conv).reshape(D, D)
    h1, f = _mm_out(o_attn, o_conv, w_out_f, xs, g_ffn)
    w_gate_f = _xwait("gather_wait_w_gate", G["w_gate"], f).reshape(DFF, D)
    w_up_f = _xwait("gather_wait_w_up", G["w_up"], f).reshape(DFF, D)
    gate, up, act = _mm_gate_up(f, w_gate_f, w_up_f)
    w_down_f = _xwait("gather_wait_w_down", G["w_down"], act).reshape(DFF, D)
    h2, r = _mm_down(act, w_down_f, h1, g_ple)
    w_pgate_f = _xwait("gather_wait_w_pgate", G["w_pgate"], r).reshape(D, D)
    w_ple_f = _xwait("gather_wait_w_ple", G["w_ple"], r).transpose(1, 0, 2).reshape(DPLE, D)

    loss_part, dh3, dpe, dpg, d_g_final, d_b_pgate = _ple_loss(
        r, w_pgate_f, b_pgate, ps, w_ple_f, h2, tgt, g_final2)
    H = {}

    def send_grads(tag, named):
        items = [(v, True) for _, v in named]
        handles, token = _xstart(f"grads_start_{tag}", items, _place(f"grads_place_{tag}", items))
        H.update(zip([n for n, _ in named], handles))
        return token

    gw_pgate = _mm_tn("gw_pgate", r, dpg).reshape(NDEV, D // NDEV, D)
    gw_ple = _mm_tn("gw_ple", ps, dpe).reshape(DPLE, NDEV, D // NDEV).transpose(1, 0, 2)
    tok = send_grads("ple", [("w_pgate", gw_pgate), ("w_ple", gw_ple)])
    dh2, dh2b, d_g_ple = _mm_pgate_bwd(dpg, w_pgate_f, h2, g_ple, dh3, deps=[tok])
    ff_shards = lambda g: g.reshape(NDEV, N_FF, D)
    gw_down = ff_shards(_mm_tn_ff("gw_down", act, dh2b))
    tok = send_grads("down", [("w_down", gw_down)])
    dgate, dup = _mm_down_bwd(dh2b, w_down_f, gate, up, deps=[tok])
    gw_gate = ff_shards(_mm_tn_ff("gw_gate", dgate, f))
    gw_up = ff_shards(_mm_tn_ff("gw_up", dup, f))
    tok = send_grads("ffn", [("w_gate", gw_gate), ("w_up", gw_up)])
    df = _mm_ffn_in_bwd(dgate, w_gate_f, dup, w_up_f, deps=[tok])
    dh1, dh1b, d_g_ffn = _rms_bwd("rms_ffn_bwd", df, h1, g_ffn, dh2, True)
    gw_out = jnp.concatenate(
        [_mm_tn("gw_out_attn", o_attn, dh1b), _mm_tn("gw_out_conv", o_conv, dh1b)], axis=0)
    tok = send_grads("out", [("w_out", gw_out.reshape(NDEV, D // NDEV, D))])
    dom, *do_wide = _mm_out_bwd(dh1b, w_out_f, deps=[tok])
    dosrc = dict(zip(DILATIONS, [dom] + do_wide))
    dcvg, d_w_dw, d_b_dw, d_g_ln, d_b_ln = _conv_bwd(z, dom, y_conv, w_dw_f, g_conv_ln, b_conv_ln)
    dqs, dks, dvs = [], [], []
    for d in DILATIONS:
        dqs.append(_attn_bwd_q(f"attn_bwd_q_d{d}", zsrc[d], dosrc[d], osrc[d], lsrc[d], d, T))
        dk, dv = _attn_bwd_kv(f"attn_bwd_kv_d{d}", zsrc[d], dosrc[d], osrc[d], lsrc[d], d, T)
        dks.append(dk)
        dvs.append(dv)
    dz = _dz_assemble(dqs, dks, dvs, dcvg, T)
    gw_in = _mm_tn_cols("gw_in", a, dz, N_IN)
    tok = send_grads("in", [("w_in", gw_in)])
    grad_x, d_g_mix = _mm_in_bwd(dz, w_in_f, xs, g_mix, dh1, deps=[tok])

    small_part = _pack_small(
        [d_g_mix, d_b_dw, d_g_ln, d_b_ln, d_g_ffn, d_g_ple, d_b_pgate, d_g_final], d_w_dw)
    small_slots = _exchange("exchange_small_grads", [(small_part, False)])[0]
    S = {n: _xwait(f"grads_wait_{n}", H[n], small_slots)
         for n in ("w_pgate", "w_ple", "w_down", "w_gate", "w_up", "w_out", "w_in")}

    mom = dict(w_in=(m_w_in, v_w_in), w_out=(m_w_out, v_w_out), w_gate=(m_w_gate, v_w_gate),
               w_up=(m_w_up, v_w_up), w_down=(m_w_down, v_w_down), w_pgate=(m_w_pgate, v_w_pgate),
               w_ple=(m_w_ple, v_w_ple))
    upd = {}
    for n in order:
        m_n, v_n = mom[n][0][0], mom[n][1][0]
        if n in tr_names:
            res = _adam(f"adam_{n}", S[n], big[n], m_n.T, v_n.T)
            upd[n] = [t.T[None] for t in res]
        else:
            res = _adam(f"adam_{n}", S[n], big[n], m_n, v_n)
            upd[n] = [t[None] for t in res]

    def lanes(v):
        full = jnp.zeros((CW, NDEV, DC // NDEV), F32)
        full = lax.dynamic_update_slice(full, v.reshape(CW, 1, DC // NDEV), (0, me, 0))
        return full.reshape(CW, DC)

    small_w = _pack_small([g_mix, b_dw, g_conv_ln, b_conv_ln, g_ffn, g_ple, b_pgate, g_final2], lanes(w_dw))
    small_m = _pack_small([m_g_mix, m_b_dw, m_g_conv_ln, m_b_conv_ln, m_g_ffn, m_g_ple, m_b_pgate,
                           m_g_final.reshape(1, D)], lanes(m_w_dw))
    small_v = _pack_small([v_g_mix, v_b_dw, v_g_conv_ln, v_b_conv_ln, v_g_ffn, v_g_ple, v_b_pgate,
                           v_g_final.reshape(1, D)], lanes(v_w_dw))
    small_res = _adam("adam_small", small_slots, small_w, small_m, small_v)

    def unpack(t):
        out = {}
        widths = dict(g_mix=D, b_dw=DC, g_conv_ln=DC, b_conv_ln=DC, g_ffn=D, g_ple=D, b_pgate=D, g_final=D)
        for i, n in enumerate(SMALL_NAMES):
            out[n] = t[i:i + 1, :widths[n]]
        out["g_final"] = out["g_final"].reshape(D)
        taps = t[len(SMALL_NAMES):len(SMALL_NAMES) + CW, :DC].reshape(CW, NDEV, DC // NDEV)
        out["w_dw"] = lax.dynamic_slice(taps, (0, me, 0), (CW, 1, DC // NDEV))[None]
        return out

    small = [unpack(t) for t in small_res]

    loss = lax.psum(loss_part[0, 0], ("x", "y", "c"))
    names = ("g_mix", "w_in", "w_dw", "b_dw", "g_conv_ln", "b_conv_ln", "w_out", "g_ffn", "w_gate",
             "w_up", "w_down", "g_ple", "w_pgate", "b_pgate", "w_ple", "g_final")
    outs = [loss, grad_x.reshape(1, T, D)]
    for kind in range(4):
        for n in names:
            outs.append(upd[n][kind] if n in upd else small[kind][n])
    return tuple(outs)
```

```python
import jax
import jax.numpy as jnp
from jax import lax
from jax.experimental import pallas as pl
from jax.experimental.pallas import tpu as pltpu

F32 = jnp.float32
BF16 = jnp.bfloat16

NDEV = 8
D = 2048
NH = 8
DH = 128
DA = NH * DH
DC = D - DA
DIN = 3 * DA + 2 * DC
DFF = 5632
DPLE = 256
BLK = 128
DILATIONS = (1, 4, 16)
CW = 31
EPS = 1e-6
N_IN = DIN // NDEV
N_FF = DFF // NDEV
NEG = -1e30

ADAM_LR = 0.001
ADAM_B1 = 0.9
ADAM_B2 = 0.999
ADAM_EPS = 1e-08
ADAM_WD = 0.01
ADAM_STEP = 10

VMEM_CAP_V7X = 64 * 1024 * 1024
VMEM_BIG = VMEM_CAP_V7X - 12 * 1024 * 1024
VMEM_MID = 40 * 1024 * 1024

SMALL_W = 2048
SMALL_ROWS = 40


def _sigmoid(v):
    return 1.0 / (1.0 + jnp.exp(-v))


def _dot(a, b, contract):
    return lax.dot_general(a, b, (contract, ((), ())), preferred_element_type=F32)


NN = ((1,), (0,))
NT = ((1,), (1,))
TN = ((0,), (0,))


def _exchange(name, items):
    n = len(items)
    out_shape = [
        jax.ShapeDtypeStruct((NDEV,) + (a.shape[1:] if sc else a.shape), a.dtype)
        for a, sc in items
    ]
    scat = [sc for _, sc in items]

    def body(*refs):
        srcs = refs[:n]
        dsts = refs[n:2 * n]
        send_sems, recv_sems, loc_sems = refs[2 * n:]
        x = lax.axis_index("x")
        y = lax.axis_index("y")
        c = lax.axis_index("c")
        me = 4 * x + 2 * y + c

        local = []
        for i in range(n):
            src = srcs[i].at[me] if scat[i] else srcs[i]
            cp = pltpu.make_async_copy(src, dsts[i].at[me], loc_sems.at[i])
            cp.start()
            local.append(cp)

        remote = []
        for k in range(1, NDEV):
            px = (1 - x) if (k >> 2) & 1 else x
            py = (1 - y) if (k >> 1) & 1 else y
            pc = (1 - c) if k & 1 else c
            peer = 4 * px + 2 * py + pc
            for i in range(n):
                sem = i * (NDEV - 1) + k - 1
                src = srcs[i].at[peer] if scat[i] else srcs[i]
                send = pltpu.make_async_remote_copy(
                    src_ref=src, dst_ref=dsts[i].at[me],
                    send_sem=send_sems.at[sem], recv_sem=recv_sems.at[sem],
                    device_id=(px, py, pc), device_id_type=pl.DeviceIdType.MESH)
                send.start()
                recv = pltpu.make_async_remote_copy(
                    src_ref=src, dst_ref=dsts[i].at[peer],
                    send_sem=send_sems.at[sem], recv_sem=recv_sems.at[sem],
                    device_id=(px, py, pc), device_id_type=pl.DeviceIdType.MESH)
                remote.append((send, recv))
        for send, recv in remote:
            recv.wait_recv()
            send.wait_send()
        for cp in local:
            cp.wait()

    any_spec = pl.BlockSpec(memory_space=pl.ANY)
    return pl.pallas_call(
        body, name=name,
        in_specs=[any_spec] * n, out_specs=[any_spec] * n, out_shape=out_shape,
        scratch_shapes=[
            pltpu.SemaphoreType.DMA((n * (NDEV - 1),)),
            pltpu.SemaphoreType.DMA((n * (NDEV - 1),)),
            pltpu.SemaphoreType.DMA((n,)),
        ],
    )(*[a for a, _ in items])


def _gather_two_level(name, arrays):
    n = len(arrays)
    per = NDEV - 1

    def body(*refs):
        srcs = refs[:n]
        dsts = refs[n:2 * n]
        send_sems, recv_sems, loc_sems = refs[2 * n:]
        x = lax.axis_index("x")
        y = lax.axis_index("y")
        c = lax.axis_index("c")
        idx = lambda px, py, pc: 4 * px + 2 * py + pc
        me, sibling = (x, y, c), (x, y, 1 - c)
        chips = [(1 - x, y), (x, 1 - y), (1 - x, 1 - y)]

        def copy(i, k, block, to, src=None):
            slot = dsts[i].at[idx(*block)]
            return pltpu.make_async_remote_copy(
                src_ref=slot if src is None else src, dst_ref=slot,
                send_sem=send_sems.at[i * per + k], recv_sem=recv_sems.at[i * per + k],
                device_id=to, device_id_type=pl.DeviceIdType.MESH)

        mine, sent = [], []
        for i in range(n):
            cp = pltpu.make_async_copy(srcs[i], dsts[i].at[idx(*me)], loc_sems.at[i])
            cp.start()
            mine.append(cp)
            first = [copy(i, 0, me, sibling, src=srcs[i])]
            first += [copy(i, 1 + j, me, (*chip, c), src=srcs[i]) for j, chip in enumerate(chips)]
            for cp in first:
                cp.start()
            sent += first
        for j, chip in enumerate(chips):
            for i in range(n):
                copy(i, 1 + j, (*chip, c), me).wait_recv()
                fwd = copy(i, 4 + j, (*chip, c), sibling)
                fwd.start()
                sent.append(fwd)
        for i in range(n):
            copy(i, 0, sibling, me).wait_recv()
            for j, chip in enumerate(chips):
                copy(i, 4 + j, (*chip, 1 - c), me).wait_recv()
        for cp in sent:
            cp.wait_send()
        for cp in mine:
            cp.wait()

    any_spec = pl.BlockSpec(memory_space=pl.ANY)
    return pl.pallas_call(
        body, name=name, in_specs=[any_spec] * n, out_specs=[any_spec] * n,
        out_shape=[jax.ShapeDtypeStruct((NDEV,) + a.shape, a.dtype) for a in arrays],
        scratch_shapes=[pltpu.SemaphoreType.DMA((n * per,)), pltpu.SemaphoreType.DMA((n * per,)),
                        pltpu.SemaphoreType.DMA((n,))],
    )(*arrays)


HBM_SPEC = pl.BlockSpec(memory_space=pltpu.HBM)
SEM_SPEC = pl.BlockSpec(memory_space=pltpu.SEMAPHORE)
ANY_SPEC = pl.BlockSpec(memory_space=pl.ANY)
EFFECT = pltpu.SideEffectType.DATAFLOW_SIDE_EFFECTING


def _peer_of(k):
    x = lax.axis_index("x")
    y = lax.axis_index("y")
    c = lax.axis_index("c")
    px = (1 - x) if (k >> 2) & 1 else x
    py = (1 - y) if (k >> 1) & 1 else y
    pc = (1 - c) if k & 1 else c
    return (px, py, pc), 4 * px + 2 * py + pc


def _my_index():
    return 4 * lax.axis_index("x") + 2 * lax.axis_index("y") + lax.axis_index("c")


def _slot_shape(a, sc):
    return (NDEV,) + (a.shape[1:] if sc else a.shape)


def _divisor_tile(rows):
    return next((t for t in (512, 256, 176, 128, 64, 32, 16) if rows % t == 0), rows)


def _place(name, items, dtype=None):
    lands = []
    for idx, (a, sc) in enumerate(items):
        rows, cols = a.shape[-2:]
        tr = _divisor_tile(rows)
        out_dtype = a.dtype if dtype is None else dtype

        def body(s_ref, o_ref):
            o_ref[...] = s_ref[...].astype(o_ref.dtype)

        mine = pl.BlockSpec((None, tr, cols), lambda i: (_my_index(), i, 0))
        lands.append(pl.pallas_call(
            body, name=f"{name}_{idx}", grid=(rows // tr,),
            in_specs=[mine if sc else pl.BlockSpec((tr, cols), lambda i: (i, 0))],
            out_specs=mine,
            out_shape=jax.ShapeDtypeStruct(_slot_shape(a, sc), out_dtype),
            compiler_params=pltpu.CompilerParams(dimension_semantics=("parallel",)),
        )(a))
    return lands


def _xstart(name, items, lands, deps=()):
    n = len(items)
    scat = [sc for _, sc in items]
    srcs_in = [a for a, sc in items if sc]
    n_src = len(srcs_in)
    src_pos = {i: p for p, i in enumerate(i for i in range(n) if scat[i])}

    def body(*refs):
        srcs = refs[:n_src]
        lzs = refs[n_src:n_src + n]
        outs = refs[n_src + n + len(deps):]
        send_sems, recv_sems, token = outs[:n], outs[n:2 * n], outs[-1]
        me = _my_index()
        for i in range(n):
            for k in range(1, NDEV):
                peer_id, peer = _peer_of(k)
                src = srcs[src_pos[i]].at[peer] if scat[i] else lzs[i].at[me]
                pltpu.make_async_remote_copy(
                    src_ref=src, dst_ref=lzs[i].at[me],
                    send_sem=send_sems[i].at[k - 1], recv_sem=recv_sems[i].at[k - 1],
                    device_id=peer_id, device_id_type=pl.DeviceIdType.MESH).start()
        token[...] = jnp.zeros_like(token)

    sem = pltpu.SemaphoreType.DMA((NDEV - 1,))
    thru = srcs_in + list(lands)
    res = pl.pallas_call(
        body, name=name,
        in_specs=[HBM_SPEC] * len(thru) + [ANY_SPEC] * len(deps),
        out_specs=[SEM_SPEC] * (2 * n) + [HBM_SPEC] * len(thru) + [pl.BlockSpec(memory_space=pltpu.VMEM)],
        out_shape=[sem] * (2 * n) + [pltpu.HBM(t.shape, t.dtype) for t in thru]
        + [jax.ShapeDtypeStruct((8, 128), F32)],
        input_output_aliases={i: 2 * n + i for i in range(len(thru))},
        compiler_params=pltpu.CompilerParams(has_side_effects=EFFECT),
    )(*[pltpu.with_memory_space_constraint(t, pltpu.HBM) for t in thru], *deps)
    handles = [(res[i], res[n + i], res[2 * n + src_pos[i]] if scat[i] else None,
                res[2 * n + n_src + i]) for i in range(n)]
    return handles, res[-1]


def _xwait(name, handle, after):
    send_sem, recv_sem, src, land = handle
    sc = src is not None

    def body(*refs):
        land_ref = refs[1] if sc else refs[0]
        send_ref, recv_ref = (refs[2], refs[3]) if sc else (refs[1], refs[2])
        me = _my_index()
        for k in range(1, NDEV):
            peer_id, peer = _peer_of(k)
            cp = pltpu.make_async_remote_copy(
                src_ref=refs[0].at[peer] if sc else land_ref.at[me], dst_ref=land_ref.at[peer],
                send_sem=send_ref.at[k - 1], recv_sem=recv_ref.at[k - 1],
                device_id=peer_id, device_id_type=pl.DeviceIdType.MESH)
            cp.wait_send()
            cp.wait_recv()

    thru = ([src] if sc else []) + [land]
    return pl.pallas_call(
        body, name=name,
        in_specs=[HBM_SPEC] * len(thru) + [SEM_SPEC, SEM_SPEC, ANY_SPEC],
        out_specs=[HBM_SPEC] * len(thru),
        out_shape=[pltpu.HBM(t.shape, t.dtype) for t in thru],
        input_output_aliases={i: i for i in range(len(thru))},
        compiler_params=pltpu.CompilerParams(has_side_effects=EFFECT),
    )(*thru, send_sem, recv_sem, after)[-1]


def _mm(name, grid, in_specs, operands, out_specs, out_shape, contract, n_pairs, epilogue,
        acc_shape=None, vmem=VMEM_BIG, deps=(), group=1, a_cols=None, carry=()):
    nk = grid[2]
    n_carry = len(carry)

    def shard(ref, s, is_a):
        if group == 1:
            return ref[...]
        if is_a and a_cols is not None:
            return ref[:, s * a_cols:(s + 1) * a_cols]
        return ref[s]
    n_extra = len(operands) - 2 * n_pairs
    n_out = len(out_shape)
    n_in = len(operands) + len(deps)
    in_specs = list(in_specs) + [ANY_SPEC] * len(deps)
    operands = list(operands) + list(deps)

    def body(*refs):
        ab = refs[:2 * n_pairs]
        extras = refs[2 * n_pairs:2 * n_pairs + n_extra]
        outs = refs[n_in:n_in + n_out]
        kept = refs[n_in + n_out:n_in + n_out + n_carry]
        finish = (lambda acc: epilogue(acc, extras, outs, kept)) if n_carry else (
            lambda acc: epilogue(acc, extras, outs))
        dots = [(ab[2 * p], ab[2 * p + 1], s) for p in range(n_pairs) for s in range(group)]
        if nk == 1:
            part = None
            for a_ref, b_ref, s in dots:
                d = _dot(shard(a_ref, s, True), shard(b_ref, s, False), contract)
                part = d if part is None else part + d
            finish(part)
        else:
            acc_ref = refs[-1]
            k = pl.program_id(2)

            @pl.when(k == 0)
            def _():
                acc_ref[...] = jnp.zeros_like(acc_ref)

            for a_ref, b_ref, s in dots:
                acc_ref[...] += _dot(shard(a_ref, s, True), shard(b_ref, s, False), contract)

            @pl.when(k == nk - 1)
            def _():
                finish(acc_ref[...])

    scratch = list(carry) + ([pltpu.VMEM(acc_shape, F32)] if nk > 1 else [])
    semantics = ("arbitrary",) * 3 if n_carry else ("parallel", "parallel", "arbitrary")
    return pl.pallas_call(
        body, name=name, grid=grid, in_specs=in_specs, out_specs=out_specs, out_shape=out_shape,
        scratch_shapes=scratch,
        compiler_params=pltpu.CompilerParams(dimension_semantics=semantics, vmem_limit_bytes=vmem),
    )(*operands)


def _ep_cast(dtype):
    def ep(acc, extras, outs):
        outs[0][...] = acc.astype(dtype)
    return ep


def _ep_rms_bwd(want_bf16, n_rows_steps):
    def ep(acc, extras, outs, kept):
        h_ref, g_ref, dres_ref = extras
        gacc = kept[0]
        i = pl.program_id(0)
        v = h_ref[...]
        r = lax.rsqrt(jnp.mean(v * v, axis=-1, keepdims=True) + EPS)
        nrm = v * r
        dn = acc * g_ref[...]
        dh = dres_ref[...] + r * (dn - nrm * jnp.mean(dn * nrm, axis=-1, keepdims=True))
        outs[0][...] = dh
        if want_bf16:
            outs[1][...] = dh.astype(BF16)

        @pl.when(i == 0)
        def _():
            gacc[...] = jnp.zeros_like(gacc)

        gacc[...] += _fold8(acc * nrm)

        @pl.when(i == n_rows_steps - 1)
        def _():
            outs[-1][...] = jnp.sum(gacc[...], axis=0, keepdims=True)
    return ep


def _ep_resid_norm(acc, extras, outs):
    h = extras[0][...] + acc
    outs[0][...] = h
    r = lax.rsqrt(jnp.mean(h * h, axis=-1, keepdims=True) + EPS)
    outs[1][...] = (h * r * extras[1][...]).astype(BF16)


def _ep_swiglu_bwd(acc, extras, outs):
    outs[0][...] = (acc * extras[0][...].astype(F32)).astype(BF16)
    outs[1][...] = (acc * extras[1][...].astype(F32)).astype(BF16)


MXU_COLS_V7X = 256


def _col_chunks(n):
    return [slice(c, min(c + MXU_COLS_V7X, n)) for c in range(0, n, MXU_COLS_V7X)]


def _row_tile(T):
    return min(1024, T)


def _tn_rows(T):
    return min(2048, T)


WIDE = tuple(d for d in DILATIONS if d > 1)


LANES = 128


def _lane_tile(c):
    return slice(c * LANES, (c + 1) * LANES)


def _to_lane_tiles(scr, val):
    for c in range(scr.shape[0]):
        scr[c] = val[:, _lane_tile(c)]


def _emit_class_major(scr, refs, rows):
    for d, ref in zip(WIDE, refs):
        for r in range(d):
            for c in range(scr.shape[0]):
                ref[r, :, _lane_tile(c)] = scr[c, pl.ds(r, rows // d, stride=d), :].astype(ref.dtype)


def _shards_to_cols(name, w):
    _, rows, n = w.shape
    tr = rows

    def body(s_ref, o_ref):
        o_ref[...] = s_ref[...]

    return pl.pallas_call(
        body, name=name, grid=(rows // tr, NDEV),
        in_specs=[pl.BlockSpec((None, tr, n), lambda i, j: (j, i, 0))],
        out_specs=pl.BlockSpec((tr, n), lambda i, j: (i, j)),
        out_shape=jax.ShapeDtypeStruct((rows, NDEV * n), w.dtype),
        compiler_params=pltpu.CompilerParams(dimension_semantics=("parallel", "parallel")),
    )(w)


def _cols_to_shards(name, g, n):
    rows = g.shape[0]
    tr = rows

    def body(s_ref, o_ref):
        o_ref[...] = s_ref[...]

    return pl.pallas_call(
        body, name=name, grid=(rows // tr, NDEV),
        in_specs=[pl.BlockSpec((tr, n), lambda i, j: (i, j))],
        out_specs=pl.BlockSpec((None, tr, n), lambda i, j: (j, i, 0)),
        out_shape=jax.ShapeDtypeStruct((NDEV, rows, n), g.dtype),
        compiler_params=pltpu.CompilerParams(dimension_semantics=("parallel", "parallel")),
    )(g)


IN_TN = DA


def _mm_in(x, g, w_in, deps=()):
    T = x.shape[0]
    tm = min(TM_FULL_ROW, T)
    nq = 3

    def body(x_ref, g_ref, w_ref, *rest):
        a_ref, z_ref, *rest = rest[len(deps):]
        scr = rest[-1]
        j = pl.program_id(1)

        @pl.when(j == 0)
        def _():
            v = x_ref[...]
            r = lax.rsqrt(jnp.mean(v * v, axis=-1, keepdims=True) + EPS)
            a_ref[...] = (v * r * g_ref[...]).astype(BF16)

        @pl.when(j >= nq)
        def _():
            z_ref[...] = _dot(a_ref[...], w_ref[...], NN).astype(BF16)

        @pl.when(j < nq)
        def _():
            av = a_ref[...]
            chunks = _col_chunks(IN_TN)
            pending = _dot(av, w_ref[:, chunks[0]], NN)
            for ci, cols in enumerate(chunks):
                nxt = _dot(av, w_ref[:, chunks[ci + 1]], NN) if ci + 1 < len(chunks) else None
                z_ref[:, cols] = pending.astype(BF16)
                for c in range(cols.start // LANES, cols.stop // LANES):
                    scr[c] = pending[:, c * LANES - cols.start:(c + 1) * LANES - cols.start]
                    for d, ref in zip(WIDE, rest[:-1]):
                        for r in range(d):
                            ref[r, :, _lane_tile(c)] = scr[c, pl.ds(r, tm // d, stride=d), :].astype(BF16)
                pending = nxt

    cm_spec = lambda d: pl.BlockSpec((d, tm // d, IN_TN), lambda i, j: (0, i, jnp.minimum(j, nq - 1)))
    row = pl.BlockSpec((tm, D), lambda i, j: (i, 0))
    return pl.pallas_call(
        body, name="mm_in", grid=(T // tm, DIN // IN_TN),
        in_specs=[row, pl.BlockSpec((1, D), lambda i, j: (0, 0)),
                  pl.BlockSpec((D, IN_TN), lambda i, j: (0, j))] + [ANY_SPEC] * len(deps),
        out_specs=[row, pl.BlockSpec((tm, IN_TN), lambda i, j: (i, j))] + [cm_spec(d) for d in WIDE],
        out_shape=[jax.ShapeDtypeStruct((T, D), BF16), jax.ShapeDtypeStruct((T, DIN), BF16)]
        + [jax.ShapeDtypeStruct((d, T // d, nq * IN_TN), BF16) for d in WIDE],
        scratch_shapes=[pltpu.VMEM((IN_TN // LANES, tm, LANES), F32)],
        compiler_params=pltpu.CompilerParams(
            dimension_semantics=("parallel", "arbitrary"), vmem_limit_bytes=VMEM_BIG),
    )(x, g, w_in, *deps)


def _mm_out_bwd(dh1b, w_out, deps=()):
    T = dh1b.shape[0]
    tm = _row_tile(T)

    def body(dy_ref, w_ref, *rest):
        rest = rest[len(deps):]
        dom_ref, scr = rest[0], rest[-1]
        acc = _dot(dy_ref[...], w_ref[...], NT)
        dom_ref[...] = acc.astype(BF16)

        @pl.when(pl.program_id(1) == 0)
        def _():
            _to_lane_tiles(scr, acc)
            _emit_class_major(scr, rest[1:-1], tm)

    return pl.pallas_call(
        body, name="mm_out_bwd", grid=(T // tm, D // DA),
        in_specs=[pl.BlockSpec((tm, D), lambda i, j: (i, 0)),
                  pl.BlockSpec((DA, D), lambda i, j: (j, 0))] + [ANY_SPEC] * len(deps),
        out_specs=[pl.BlockSpec((tm, DA), lambda i, j: (i, j))]
        + [pl.BlockSpec((d, tm // d, DA), lambda i, j: (0, i, 0)) for d in WIDE],
        out_shape=[jax.ShapeDtypeStruct((T, D), BF16)]
        + [jax.ShapeDtypeStruct((d, T // d, DA), BF16) for d in WIDE],
        scratch_shapes=[pltpu.VMEM((DA // LANES, tm, LANES), F32)],
        compiler_params=pltpu.CompilerParams(
            dimension_semantics=("parallel", "arbitrary"), vmem_limit_bytes=VMEM_BIG),
    )(dh1b, w_out, *deps)


TM_FULL_ROW = 512


def _full_row_specs(tm):
    row = pl.BlockSpec((tm, D), lambda i, j, k: (i, 0))
    return row, pl.BlockSpec((1, D), lambda i, j, k: (0, 0))


def _mm_out(o_attn, o_conv, w_out, x, g_next):
    T = x.shape[0]
    tm = min(TM_FULL_ROW, T)
    row, vec = _full_row_specs(tm)
    return _mm(
        "mm_out", (T // tm, 1, 1),
        [pl.BlockSpec((tm, DA), lambda i, j, k: (i, 0)),
         pl.BlockSpec((DA, D), lambda i, j, k: (0, 0)),
         pl.BlockSpec((tm, DC), lambda i, j, k: (i, 0)),
         pl.BlockSpec((DC, D), lambda i, j, k: (1, 0)),
         row, vec],
        [o_attn, w_out, o_conv, w_out, x, g_next],
        [row, row],
        [jax.ShapeDtypeStruct((T, D), F32), jax.ShapeDtypeStruct((T, D), BF16)], NN, 2,
        _ep_resid_norm)


FF_TN = 512
FF_TK = 2 * N_FF


def _mm_gate_up(f, wg_t, wu_t):
    T = f.shape[0]
    tm = _row_tile(T)

    def body(f_ref, wg_ref, wu_ref, dg_ref, du_ref, a_ref):
        fv = f_ref[...]
        g = _dot(fv, wg_ref[...], NT)
        u = _dot(fv, wu_ref[...], NT)
        sg = _sigmoid(g)
        silu = g * sg
        dg_ref[...] = (u * (sg * (1.0 + g * (1.0 - sg)))).astype(BF16)
        du_ref[...] = silu.astype(BF16)
        a_ref[...] = (silu * u).astype(BF16)

    wspec = pl.BlockSpec((FF_TN, D), lambda i, j: (j, 0))
    ospec = pl.BlockSpec((tm, FF_TN), lambda i, j: (i, j))
    sh = jax.ShapeDtypeStruct((T, DFF), BF16)
    return pl.pallas_call(
        body, name="mm_gate_up", grid=(T // tm, DFF // FF_TN),
        in_specs=[pl.BlockSpec((tm, D), lambda i, j: (i, 0)), wspec, wspec],
        out_specs=[ospec, ospec, ospec], out_shape=[sh, sh, sh],
        compiler_params=pltpu.CompilerParams(
            dimension_semantics=("parallel", "parallel"), vmem_limit_bytes=VMEM_BIG),
    )(f, wg_t, wu_t)


def _mm_down(act, w_down, h1, g_next):
    T = h1.shape[0]
    tm = min(TM_FULL_ROW, T)
    row, vec = _full_row_specs(tm)
    return _mm(
        "mm_down", (T // tm, 1, DFF // FF_TK),
        [pl.BlockSpec((tm, FF_TK), lambda i, j, k: (i, k)),
         pl.BlockSpec((FF_TK, D), lambda i, j, k: (k, 0)),
         row, vec],
        [act, w_down, h1, g_next],
        [row, row],
        [jax.ShapeDtypeStruct((T, D), F32), jax.ShapeDtypeStruct((T, D), BF16)], NN, 1,
        _ep_resid_norm, acc_shape=(tm, D))


def _ple_loss(r, w_pgate, b_pgate, p, w_ple, h2, target, g_final):
    T = h2.shape[0]
    tm = min(256, T)
    nt = T // tm

    def body(r_ref, wg_ref, b_ref, p_ref, wp_ref, h2_ref, t_ref, g_ref,
             loss_ref, dh_ref, dpe_ref, dpg_ref, dgf_ref, dbp_ref, lacc, gacc, bacc):
        i = pl.program_id(0)
        gte = _sigmoid(_dot(r_ref[...], wg_ref[...], NN) + b_ref[...])
        pe = _dot(p_ref[...], wp_ref[...], NN)
        v = h2_ref[...] + pe * gte
        rr = lax.rsqrt(jnp.mean(v * v, axis=-1, keepdims=True) + EPS)
        nrm = v * rr
        g = g_ref[...]
        err = nrm * g - t_ref[...]
        dy = err * (1.0 / D)
        dn = dy * g
        dh = rr * (dn - nrm * jnp.mean(dn * nrm, axis=-1, keepdims=True))
        dh_ref[...] = dh
        dpe_ref[...] = (dh * gte).astype(BF16)
        dpg = dh * pe * gte * (1.0 - gte)
        dpg_ref[...] = dpg.astype(BF16)

        @pl.when(i == 0)
        def _():
            lacc[...] = jnp.zeros_like(lacc)
            gacc[...] = jnp.zeros_like(gacc)
            bacc[...] = jnp.zeros_like(bacc)

        lacc[...] += _fold8(err * err)
        gacc[...] += _fold8(dy * nrm)
        bacc[...] += _fold8(dpg)

        @pl.when(i == nt - 1)
        def _():
            tot = jnp.sum(jnp.sum(lacc[...], axis=0, keepdims=True), axis=1, keepdims=True)
            loss_ref[...] = jnp.broadcast_to(tot * (0.5 / D), (1, 128))
            dgf_ref[...] = jnp.sum(gacc[...], axis=0, keepdims=True)
            dbp_ref[...] = jnp.sum(bacc[...], axis=0, keepdims=True)

    row = pl.BlockSpec((tm, D), lambda i: (i, 0))
    vec = pl.BlockSpec((1, D), lambda i: (0, 0))
    return pl.pallas_call(
        body, name="ple_loss", grid=(nt,),
        in_specs=[row, pl.BlockSpec((D, D), lambda i: (0, 0)), vec,
                  pl.BlockSpec((tm, DPLE), lambda i: (i, 0)),
                  pl.BlockSpec((DPLE, D), lambda i: (0, 0)), row, row, vec],
        out_specs=[pl.BlockSpec((1, 128), lambda i: (0, 0)), row, row, row, vec, vec],
        out_shape=[jax.ShapeDtypeStruct((1, 128), F32), jax.ShapeDtypeStruct((T, D), F32),
                   jax.ShapeDtypeStruct((T, D), BF16), jax.ShapeDtypeStruct((T, D), BF16),
                   jax.ShapeDtypeStruct((1, D), F32), jax.ShapeDtypeStruct((1, D), F32)],
        scratch_shapes=[pltpu.VMEM((8, D), F32)] * 3,
        compiler_params=pltpu.CompilerParams(
            dimension_semantics=("arbitrary",), vmem_limit_bytes=VMEM_BIG),
    )(r, w_pgate, b_pgate, p, w_ple, h2, target, g_final)


def _mm_down_bwd(dh2, w_down, g, u, deps=()):
    T = dh2.shape[0]
    tm = _tn_rows(T)
    gspec = pl.BlockSpec((tm, FF_TN), lambda i, j, k: (i, j))
    sh = jax.ShapeDtypeStruct((T, DFF), BF16)
    return _mm(
        "mm_down_bwd", (T // tm, DFF // FF_TN, 1),
        [pl.BlockSpec((tm, D), lambda i, j, k: (i, 0)),
         pl.BlockSpec((FF_TN, D), lambda i, j, k: (j, 0)),
         gspec, gspec],
        [dh2, w_down, g, u],
        [gspec, gspec], [sh, sh], NT, 1, _ep_swiglu_bwd, deps=deps)


def _mm_ffn_in_bwd(dg, wg_t, du, wu_t, deps=()):
    T = dg.shape[0]
    tm = _row_tile(T)
    tn = 1024
    aspec = pl.BlockSpec((tm, FF_TK), lambda i, j, k: (i, k))
    wspec = pl.BlockSpec((FF_TK, tn), lambda i, j, k: (k, j))
    return _mm(
        "mm_ffn_in_bwd", (T // tm, D // tn, DFF // FF_TK),
        [aspec, wspec, aspec, wspec], [dg, wg_t, du, wu_t],
        [pl.BlockSpec((tm, tn), lambda i, j, k: (i, j))],
        [jax.ShapeDtypeStruct((T, D), BF16)], NN, 2, _ep_cast(BF16), acc_shape=(tm, tn),
        deps=deps)[0]


def _mm_in_bwd(dz, w_in, x, g, dres, deps=()):
    T = dz.shape[0]
    tm = min(TM_FULL_ROW, T)
    tk = DIN // 4
    row, vec = _full_row_specs(tm)
    return _mm(
        "mm_in_bwd", (T // tm, 1, DIN // tk),
        [pl.BlockSpec((tm, tk), lambda i, j, k: (i, k)),
         pl.BlockSpec((D, tk), lambda i, j, k: (0, k)),
         row, vec, row],
        [dz, w_in, x, g, dres],
        [row, vec],
        [jax.ShapeDtypeStruct((T, D), F32), jax.ShapeDtypeStruct((1, D), F32)], NT, 1,
        _ep_rms_bwd(False, T // tm), acc_shape=(tm, D), deps=deps,
        carry=[pltpu.VMEM((8, D), F32)])


def _mm_pgate_bwd(dpg, w_pgate, h2, g, dres, deps=()):
    T = dpg.shape[0]
    tm = min(256, T)
    row, vec = _full_row_specs(tm)
    return _mm(
        "mm_pgate_bwd", (T // tm, 1, 1),
        [row, pl.BlockSpec((D, D), lambda i, j, k: (0, 0)), row, vec, row],
        [dpg, w_pgate, h2, g, dres],
        [row, row, vec],
        [jax.ShapeDtypeStruct((T, D), F32), jax.ShapeDtypeStruct((T, D), BF16),
         jax.ShapeDtypeStruct((1, D), F32)], NT, 1,
        _ep_rms_bwd(True, T // tm), deps=deps, carry=[pltpu.VMEM((8, D), F32)])


def _mm_tn(name, a, b, tj=None):
    T, idim = a.shape
    jdim = b.shape[1]
    tt = _row_tile(T)
    ti = min(idim, 1024)
    tj = jdim if tj is None else tj
    return _mm(
        name, (idim // ti, jdim // tj, T // tt),
        [pl.BlockSpec((tt, ti), lambda i, j, k: (k, i)),
         pl.BlockSpec((tt, tj), lambda i, j, k: (k, j))],
        [a, b],
        [pl.BlockSpec((ti, tj), lambda i, j, k: (i, j))],
        [jax.ShapeDtypeStruct((idim, jdim), BF16)], TN, 1, _ep_cast(BF16), acc_shape=(ti, tj))[0]


def _mm_tn_wide(name, a, b):
    T = a.shape[0]
    jdim = b.shape[1]
    tt = _row_tile(T)
    return _mm(
        name, (1, jdim // IN_TN, T // tt),
        [pl.BlockSpec((tt, D), lambda i, j, k: (k, 0)),
         pl.BlockSpec((tt, IN_TN), lambda i, j, k: (k, j))],
        [a, b],
        [pl.BlockSpec((D, IN_TN), lambda i, j, k: (0, j))],
        [jax.ShapeDtypeStruct((D, jdim), BF16)], TN, 1, _ep_cast(BF16),
        acc_shape=(D, IN_TN))[0]


def _mm_tn_ff(name, a, b):
    T = b.shape[0]
    tt = _tn_rows(T)
    return _mm(
        name, (DFF // FF_TN, 1, T // tt),
        [pl.BlockSpec((tt, FF_TN), lambda i, j, k: (k, i)),
         pl.BlockSpec((tt, D), lambda i, j, k: (k, 0))],
        [a, b],
        [pl.BlockSpec((FF_TN, D), lambda i, j, k: (i, 0))],
        [jax.ShapeDtypeStruct((DFF, D), BF16)], TN, 1, _ep_cast(BF16),
        acc_shape=(FF_TN, D))[0]


TR = 256


def _rows(T):
    return min(TR, T)


def _fold8(v):
    return jnp.sum(v.reshape(v.shape[0] // 8, 8, v.shape[1]), axis=0)


def _rms_bwd(name, dn_out, h, g, dres, want_bf16):
    T = h.shape[0]
    tr = _rows(T)
    nt = T // tr

    def body(dy_ref, h_ref, g_ref, dres_ref, *rest):
        if want_bf16:
            dh_ref, dhb_ref, dg_ref, acc = rest
        else:
            dh_ref, dg_ref, acc = rest
        i = pl.program_id(0)
        v = h_ref[...]
        r = lax.rsqrt(jnp.mean(v * v, axis=-1, keepdims=True) + EPS)
        nrm = v * r
        dy = dy_ref[...].astype(F32)
        dn = dy * g_ref[...]
        dh = dres_ref[...] + r * (dn - nrm * jnp.mean(dn * nrm, axis=-1, keepdims=True))
        dh_ref[...] = dh
        if want_bf16:
            dhb_ref[...] = dh.astype(BF16)

        @pl.when(i == 0)
        def _():
            acc[...] = jnp.zeros_like(acc)

        acc[...] += _fold8(dy * nrm)

        @pl.when(i == nt - 1)
        def _():
            dg_ref[...] = jnp.sum(acc[...], axis=0, keepdims=True)

    tile = pl.BlockSpec((tr, D), lambda i: (i, 0))
    vec = pl.BlockSpec((1, D), lambda i: (0, 0))
    out_specs = [tile] + ([tile] if want_bf16 else []) + [vec]
    out_shape = ([jax.ShapeDtypeStruct((T, D), F32)]
                 + ([jax.ShapeDtypeStruct((T, D), BF16)] if want_bf16 else [])
                 + [jax.ShapeDtypeStruct((1, D), F32)])
    return pl.pallas_call(
        body, name=name, grid=(nt,),
        in_specs=[tile, tile, vec, tile], out_specs=out_specs, out_shape=out_shape,
        scratch_shapes=[pltpu.VMEM((8, D), F32)],
        compiler_params=pltpu.CompilerParams(dimension_semantics=("arbitrary",)),
    )(dn_out, h, g, dres)


def _band_masks():
    qi = lax.broadcasted_iota(jnp.int32, (BLK, BLK), 0)
    kj = lax.broadcasted_iota(jnp.int32, (BLK, BLK), 1)
    return kj >= qi, kj <= qi


AQ = 8
ATTN_PARAMS = pltpu.CompilerParams(
    dimension_semantics=("parallel", "parallel"), vmem_limit_bytes=VMEM_BIG)


def _cm_spec(d, col, nblk, rowmap=lambda n: n):
    if d == 1:
        return pl.BlockSpec((nblk * BLK, DA), lambda r, n: (rowmap(n), col))
    return pl.BlockSpec((None, nblk * BLK, DA), lambda r, n: (r, rowmap(n), col))


def _cm_shape(d, T, dtype):
    return jax.ShapeDtypeStruct((T, DA) if d == 1 else (d, T // d, DA), dtype)


def _blk(b):
    return slice(b * BLK, (b + 1) * BLK)


HEADS = tuple(slice(h * DH, (h + 1) * DH) for h in range(NH))


def _attn_fwd(name, zsrc, d, T):
    nb = T // d // BLK
    aq = min(AQ, nb)
    scale = DH ** -0.5

    def body(q_ref, kp_ref, kc_ref, vp_ref, vc_ref, o_ref, l_ref):
        n = pl.program_id(1)
        band_prev, cur_ok = _band_masks()
        for b in range(aq):
            kp = (lambda sl: kp_ref[:, sl]) if b == 0 else (lambda sl, b=b: kc_ref[_blk(b - 1), sl])
            vp = (lambda sl: vp_ref[:, sl]) if b == 0 else (lambda sl, b=b: vc_ref[_blk(b - 1), sl])
            prev_ok = band_prev & (n > 0) if b == 0 else band_prev
            rows = _blk(b)
            s = [(jnp.where(prev_ok, _dot(q_ref[rows, sl], kp(sl), NT) * scale, NEG),
                  jnp.where(cur_ok, _dot(q_ref[rows, sl], kc_ref[rows, sl], NT) * scale, NEG))
                 for sl in HEADS]
            m = [jnp.maximum(jnp.max(sp, axis=1, keepdims=True), jnp.max(sc, axis=1, keepdims=True))
                 for sp, sc in s]
            p = [(jnp.exp(sp - mh), jnp.exp(sc - mh)) for (sp, sc), mh in zip(s, m)]
            den = [jnp.sum(pp, axis=1, keepdims=True) + jnp.sum(pc, axis=1, keepdims=True)
                   for pp, pc in p]
            o = [_dot(pp.astype(BF16), vp(sl), NN) + _dot(pc.astype(BF16), vc_ref[rows, sl], NN)
                 for (pp, pc), sl in zip(p, HEADS)]
            o_ref[rows, :] = jnp.concatenate(
                [(oh / dh).astype(BF16) for oh, dh in zip(o, den)], axis=1)
            l_ref[rows, :] = jnp.concatenate(
                [jnp.broadcast_to(mh + jnp.log(dh), (BLK, DH)) for mh, dh in zip(m, den)], axis=1)

    halo = lambda n: jnp.maximum(aq * n - 1, 0)
    return pl.pallas_call(
        body, name=name, grid=(d, nb // aq),
        in_specs=[_cm_spec(d, 0, aq), _cm_spec(d, 1, 1, halo), _cm_spec(d, 1, aq),
                  _cm_spec(d, 2, 1, halo), _cm_spec(d, 2, aq)],
        out_specs=[_cm_spec(d, 0, aq)] * 2,
        out_shape=[_cm_shape(d, T, BF16), _cm_shape(d, T, F32)],
        compiler_params=ATTN_PARAMS,
    )(zsrc, zsrc, zsrc, zsrc, zsrc)


def _cm_tile(d, tr):
    if d == 1:
        return pl.BlockSpec((tr, DA), lambda i: (i, 0))
    return pl.BlockSpec((d, tr // d, DA), lambda i: (0, i, 0))


def _attn_combine(outs, lses, T):
    tr = _rows(T)

    def body(*refs):
        o_in, l_in = refs[:3], refs[3:6]
        o_ref, l_ref = refs[6:8]
        o_cm, l_cm = refs[8:8 + len(WIDE)], refs[8 + len(WIDE):8 + 2 * len(WIDE)]
        so, sl, so_all, sl_all = refs[8 + 2 * len(WIDE):]
        for c in range(DA // LANES):
            lt = _lane_tile(c)
            os_, ls_ = [o_in[0][:, lt].astype(F32)], [l_in[0][:, lt]]
            for w, d in enumerate(WIDE):
                for r in range(d):
                    so[w, c, pl.ds(r, tr // d, stride=d), :] = o_in[1 + w][r, :, lt].astype(F32)
                    sl[w, c, pl.ds(r, tr // d, stride=d), :] = l_in[1 + w][r, :, lt]
                os_.append(so[w, c])
                ls_.append(sl[w, c])
            la, lb, lc = ls_
            m = jnp.maximum(jnp.maximum(la, lb), lc)
            ea, eb, ec = jnp.exp(la - m), jnp.exp(lb - m), jnp.exp(lc - m)
            s = ea + eb + ec
            o = (ea * os_[0] + eb * os_[1] + ec * os_[2]) / s
            lse = m + jnp.log(s)
            o_ref[:, lt] = o.astype(BF16)
            l_ref[:, lt] = lse
            so_all[c] = o
            sl_all[c] = lse
        _emit_class_major(so_all, o_cm, tr)
        _emit_class_major(sl_all, l_cm, tr)

    specs = [_cm_tile(d, tr) for d in DILATIONS]
    wide = [_cm_tile(d, tr) for d in WIDE]
    return pl.pallas_call(
        body, name="attn_combine", grid=(T // tr,),
        in_specs=specs + specs,
        out_specs=[specs[0], specs[0]] + wide + wide,
        out_shape=[_cm_shape(1, T, BF16), _cm_shape(1, T, F32)]
        + [_cm_shape(d, T, BF16) for d in WIDE] + [_cm_shape(d, T, F32) for d in WIDE],
        scratch_shapes=[pltpu.VMEM((len(WIDE), DA // LANES, tr, LANES), F32)] * 2
        + [pltpu.VMEM((DA // LANES, tr, LANES), F32)] * 2,
        compiler_params=pltpu.CompilerParams(
            dimension_semantics=("parallel",), vmem_limit_bytes=VMEM_MID),
    )(*outs, *lses)


def _attn_bwd_q(name, zsrc, dosrc, osrc, lsrc, d, T):
    nb = T // d // BLK
    aq = min(AQ, nb)
    scale = DH ** -0.5

    def body(q_ref, kp_ref, kc_ref, vp_ref, vc_ref, do_ref, o_ref, l_ref, dq_ref):
        n = pl.program_id(1)
        band_prev, cur_ok = _band_masks()
        for b in range(aq):
            kp = (lambda sl: kp_ref[:, sl]) if b == 0 else (lambda sl, b=b: kc_ref[_blk(b - 1), sl])
            vp = (lambda sl: vp_ref[:, sl]) if b == 0 else (lambda sl, b=b: vc_ref[_blk(b - 1), sl])
            prev_ok = band_prev & (n > 0) if b == 0 else band_prev
            rows = _blk(b)
            s = [(_dot(q_ref[rows, sl], kp(sl), NT), _dot(q_ref[rows, sl], kc_ref[rows, sl], NT))
                 for sl in HEADS]
            dp = [(_dot(do_ref[rows, sl], vp(sl), NT), _dot(do_ref[rows, sl], vc_ref[rows, sl], NT))
                  for sl in HEADS]
            delta = [jnp.sum(do_ref[rows, sl].astype(F32) * o_ref[rows, sl].astype(F32), axis=1,
                             keepdims=True) for sl in HEADS]
            p = [(jnp.exp(jnp.where(prev_ok, sp * scale - l_ref[rows, sl], NEG)),
                  jnp.exp(jnp.where(cur_ok, sc * scale - l_ref[rows, sl], NEG)))
                 for (sp, sc), sl in zip(s, HEADS)]
            ds = [((pp * (dpp - dl) * scale).astype(BF16), (pc * (dpc - dl) * scale).astype(BF16))
                  for (pp, pc), (dpp, dpc), dl in zip(p, dp, delta)]
            dq = [_dot(dsp, kp(sl), NN) + _dot(dsc, kc_ref[rows, sl], NN)
                  for (dsp, dsc), sl in zip(ds, HEADS)]
            dq_ref[rows, :] = jnp.concatenate([v.astype(BF16) for v in dq], axis=1)

    halo = lambda n: jnp.maximum(aq * n - 1, 0)
    own = _cm_spec(d, 0, aq)
    return pl.pallas_call(
        body, name=name, grid=(d, nb // aq),
        in_specs=[own, _cm_spec(d, 1, 1, halo), _cm_spec(d, 1, aq), _cm_spec(d, 2, 1, halo),
                  _cm_spec(d, 2, aq), own, own, own],
        out_specs=own, out_shape=_cm_shape(d, T, BF16),
        compiler_params=ATTN_PARAMS,
    )(zsrc, zsrc, zsrc, zsrc, zsrc, dosrc, osrc, lsrc)


def _attn_bwd_kv(name, zsrc, dosrc, osrc, lsrc, d, T):
    nb = T // d // BLK
    aq = min(AQ, nb)
    nsteps = nb // aq
    scale = DH ** -0.5

    def body(k_ref, v_ref, q_ref, qn_ref, do_ref, don_ref, o_ref, on_ref, l_ref, ln_ref,
             dk_ref, dv_ref):
        j = pl.program_id(1)
        band_next, own_ok = _band_masks()
        for b in range(aq):
            rows = _blk(b)
            last = b == aq - 1
            pick = lambda cur, halo: ((lambda sl: halo[:, sl]) if last
                                      else (lambda sl, b=b: cur[_blk(b + 1), sl]))
            qb, dob, ob, lb = (pick(q_ref, qn_ref), pick(do_ref, don_ref), pick(o_ref, on_ref),
                               pick(l_ref, ln_ref))
            next_ok = band_next & (j < nsteps - 1) if last else band_next
            s = [(_dot(q_ref[rows, sl], k_ref[rows, sl], NT), _dot(qb(sl), k_ref[rows, sl], NT))
                 for sl in HEADS]
            dp = [(_dot(do_ref[rows, sl], v_ref[rows, sl], NT), _dot(dob(sl), v_ref[rows, sl], NT))
                  for sl in HEADS]
            delta = [(jnp.sum(do_ref[rows, sl].astype(F32) * o_ref[rows, sl].astype(F32), axis=1,
                              keepdims=True),
                      jnp.sum(dob(sl).astype(F32) * ob(sl).astype(F32), axis=1, keepdims=True))
                     for sl in HEADS]
            p = [(jnp.exp(jnp.where(own_ok, sa * scale - l_ref[rows, sl], NEG)),
                  jnp.exp(jnp.where(next_ok, sb * scale - lb(sl), NEG)))
                 for (sa, sb), sl in zip(s, HEADS)]
            dv = [_dot(pa.astype(BF16), do_ref[rows, sl], TN) + _dot(pb.astype(BF16), dob(sl), TN)
                  for (pa, pb), sl in zip(p, HEADS)]
            ds = [((pa * (dpa - da) * scale).astype(BF16), (pb * (dpb - db) * scale).astype(BF16))
                  for (pa, pb), (dpa, dpb), (da, db) in zip(p, dp, delta)]
            dk = [_dot(dsa, q_ref[rows, sl], TN) + _dot(dsb, qb(sl), TN)
                  for (dsa, dsb), sl in zip(ds, HEADS)]
            dk_ref[rows, :] = jnp.concatenate([v.astype(BF16) for v in dk], axis=1)
            dv_ref[rows, :] = jnp.concatenate([v.astype(BF16) for v in dv], axis=1)

    halo = lambda j: jnp.minimum(aq * (j + 1), nb - 1)
    own, own_n = _cm_spec(d, 0, aq), _cm_spec(d, 0, 1, halo)
    sh = _cm_shape(d, T, BF16)
    return pl.pallas_call(
        body, name=name, grid=(d, nsteps),
        in_specs=[_cm_spec(d, 1, aq), _cm_spec(d, 2, aq), own, own_n, own, own_n, own, own_n,
                  own, own_n],
        out_specs=[own, own], out_shape=[sh, sh],
        compiler_params=ATTN_PARAMS,
    )(zsrc, zsrc, zsrc, zsrc, dosrc, dosrc, osrc, osrc, lsrc, lsrc)


def _dz_assemble(dqs, dks, dvs, dcvg, T):
    tr = _rows(T)
    nb = len(DILATIONS)

    def body(*refs):
        cvg_ref, dz_ref, scr = refs[3 * nb], refs[3 * nb + 1], refs[3 * nb + 2]
        for g in range(3):
            parts = refs[g * nb:(g + 1) * nb]
            for c in range(DA // LANES):
                lt = _lane_tile(c)
                scr[g, c] = parts[0][:, lt].astype(F32)
                for w, d in enumerate(WIDE):
                    for r in range(d):
                        rows = pl.ds(r, tr // d, stride=d)
                        scr[g, c, rows, :] = scr[g, c, rows, :] + parts[1 + w][r, :, lt].astype(F32)
                dz_ref[:, g * DA + c * LANES:g * DA + (c + 1) * LANES] = scr[g, c].astype(BF16)
        dz_ref[:, 3 * DA:] = cvg_ref[...]

    specs = [_cm_tile(d, tr) for d in DILATIONS]
    return pl.pallas_call(
        body, name="dz_assemble", grid=(T // tr,),
        in_specs=specs * 3 + [pl.BlockSpec((tr, 2 * DC), lambda i: (i, 0))],
        out_specs=pl.BlockSpec((tr, DIN), lambda i: (i, 0)),
        out_shape=jax.ShapeDtypeStruct((T, DIN), BF16),
        scratch_shapes=[pltpu.VMEM((3, DA // LANES, tr, LANES), F32)],
        compiler_params=pltpu.CompilerParams(
            dimension_semantics=("parallel",), vmem_limit_bytes=VMEM_MID),
    )(*dqs, *dks, *dvs, dcvg)


CT = 256
HALO = 32
RC = 32


def _conv_fwd(z, w_dw, b_dw, g_ln, b_ln):
    T = z.shape[0]
    ct = min(CT, T)
    nt = T // ct
    hb = ct // HALO

    def body(cv_ref, cg_ref, cvp_ref, cgp_ref, w_ref, bdw_ref, g_ref, b_ref, oc_ref, y_ref, ubuf, ush):
        i = pl.program_id(0)
        up = cvp_ref[...].astype(F32) * _sigmoid(cgp_ref[...].astype(F32))
        ubuf[0:HALO, :] = jnp.where(i > 0, up, 0.0)
        ubuf[HALO:, :] = cv_ref[...].astype(F32) * _sigmoid(cg_ref[...].astype(F32))
        for b in range(8):
            ush[b] = ubuf[pl.ds(8 - b, ct + 24), :]

        def chunk(ci, carry):
            r0 = pl.multiple_of(ci * RC, RC)
            acc = jnp.broadcast_to(bdw_ref[...], (RC, DC))
            for s in range(CW):
                a, b = divmod(s, 8)
                acc = acc + w_ref[CW - 1 - s:CW - s, :] * ush[b, pl.ds(r0 + 24 - 8 * a, RC), :]
            y_ref[pl.ds(r0, RC), :] = acc
            mu = jnp.mean(acc, axis=-1, keepdims=True)
            cen = acc - mu
            var = jnp.mean(cen * cen, axis=-1, keepdims=True)
            ln = cen * lax.rsqrt(var + EPS) * g_ref[...] + b_ref[...]
            oc_ref[pl.ds(r0, RC), :] = (ln * _sigmoid(ln)).astype(BF16)
            return carry

        lax.fori_loop(0, ct // RC, chunk, 0)

    cur = lambda col: pl.BlockSpec((ct, DC), lambda i: (i, col))
    prv = lambda col: pl.BlockSpec((HALO, DC), lambda i: (jnp.maximum(i * hb - 1, 0), col))
    vec = pl.BlockSpec((1, DC), lambda i: (0, 0))
    return pl.pallas_call(
        body, name="conv_fwd", grid=(nt,),
        in_specs=[cur(3), cur(4), prv(3), prv(4), pl.BlockSpec((CW, DC), lambda i: (0, 0)),
                  vec, vec, vec],
        out_specs=[pl.BlockSpec((ct, DC), lambda i: (i, 0))] * 2,
        out_shape=[jax.ShapeDtypeStruct((T, DC), BF16), jax.ShapeDtypeStruct((T, DC), F32)],
        scratch_shapes=[pltpu.VMEM((ct + HALO, DC), F32), pltpu.VMEM((8, ct + 24, DC), F32)],
        compiler_params=pltpu.CompilerParams(
            dimension_semantics=("parallel",), vmem_limit_bytes=VMEM_MID),
    )(z, z, z, z, w_dw, b_dw, g_ln, b_ln)


def _conv_bwd(z, dom, y, w_dw, g_ln, b_ln):
    T = z.shape[0]
    ct = min(CT, T)
    nt = T // ct
    hb = ct // HALO
    last_halo = T // HALO - 1

    def ln_bwd(yv, dov, g_ref, b_ref):
        mu = jnp.mean(yv, axis=-1, keepdims=True)
        cen = yv - mu
        rstd = lax.rsqrt(jnp.mean(cen * cen, axis=-1, keepdims=True) + EPS)
        xhat = cen * rstd
        ln = xhat * g_ref[...] + b_ref[...]
        sg = _sigmoid(ln)
        dln = dov * (sg * (1.0 + ln * (1.0 - sg)))
        dxh = dln * g_ref[...]
        dy = rstd * (dxh - jnp.mean(dxh, axis=-1, keepdims=True)
                     - xhat * jnp.mean(dxh * xhat, axis=-1, keepdims=True))
        return dy, dln, xhat

    def body(do_ref, don_ref, y_ref, yn_ref, cv_ref, cg_ref, cvp_ref, cgp_ref, w_ref, g_ref, b_ref,
             dcvg_ref, dw_ref, dbdw_ref, dg_ref, db_ref,
             dybuf, dysh, ubuf, ush, dwacc, vacc):
        i = pl.program_id(0)

        @pl.when(i == 0)
        def _():
            dwacc[...] = jnp.zeros_like(dwacc)
            vacc[...] = jnp.zeros_like(vacc)

        def ln_chunk(ci, carry):
            r0 = pl.multiple_of(ci * RC, RC)
            dy, dln, xhat = ln_bwd(y_ref[pl.ds(r0, RC), :], do_ref[pl.ds(r0, RC), :].astype(F32),
                                   g_ref, b_ref)
            dybuf[pl.ds(r0, RC), :] = dy
            vacc[0] += _fold8(dy)
            vacc[1] += _fold8(dln * xhat)
            vacc[2] += _fold8(dln)
            return carry

        lax.fori_loop(0, ct // RC, ln_chunk, 0)
        dyn, _, _ = ln_bwd(yn_ref[...], don_ref[...].astype(F32), g_ref, b_ref)
        dybuf[ct:, :] = jnp.where(i < nt - 1, dyn, 0.0)
        for b in range(8):
            dysh[b] = dybuf[pl.ds(b, ct + 24), :]

        up = cvp_ref[...].astype(F32) * _sigmoid(cgp_ref[...].astype(F32))
        ubuf[0:HALO, :] = jnp.where(i > 0, up, 0.0)
        ubuf[HALO:, :] = cv_ref[...].astype(F32) * _sigmoid(cg_ref[...].astype(F32))
        for b in range(8):
            ush[b] = ubuf[pl.ds(8 - b, ct + 24), :]

        def chunk(ci, carry):
            r0 = pl.multiple_of(ci * RC, RC)
            dy = dybuf[pl.ds(r0, RC), :]
            du = jnp.zeros((RC, DC), F32)
            for s in range(CW):
                a, b = divmod(s, 8)
                du = du + w_ref[CW - 1 - s:CW - s, :] * dysh[b, pl.ds(r0 + 8 * a, RC), :]
                dwacc[CW - 1 - s] += _fold8(dy * ush[b, pl.ds(r0 + 24 - 8 * a, RC), :])
            cv = cv_ref[pl.ds(r0, RC), :].astype(F32)
            sg = _sigmoid(cg_ref[pl.ds(r0, RC), :].astype(F32))
            dcvg_ref[pl.ds(r0, RC), 0:DC] = (du * sg).astype(BF16)
            dcvg_ref[pl.ds(r0, RC), DC:2 * DC] = (du * cv * sg * (1.0 - sg)).astype(BF16)
            return carry

        lax.fori_loop(0, ct // RC, chunk, 0)

        @pl.when(i == nt - 1)
        def _():
            dw_ref[...] = jnp.sum(dwacc[...], axis=1)
            dbdw_ref[...] = jnp.sum(vacc[0], axis=0, keepdims=True)
            dg_ref[...] = jnp.sum(vacc[1], axis=0, keepdims=True)
            db_ref[...] = jnp.sum(vacc[2], axis=0, keepdims=True)

    cur = lambda col: pl.BlockSpec((ct, DC), lambda i: (i, col))
    prv = lambda col: pl.BlockSpec((HALO, DC), lambda i: (jnp.maximum(i * hb - 1, 0), col))
    nxt = lambda col: pl.BlockSpec((HALO, DC), lambda i: (jnp.minimum((i + 1) * hb, last_halo), col))
    vec = pl.BlockSpec((1, DC), lambda i: (0, 0))
    tile = pl.BlockSpec((ct, DC), lambda i: (i, 0))
    return pl.pallas_call(
        body, name="conv_bwd", grid=(nt,),
        in_specs=[cur(1), nxt(1), cur(0), nxt(0), cur(3), cur(4), prv(3), prv(4),
                  pl.BlockSpec((CW, DC), lambda i: (0, 0)), vec, vec],
        out_specs=[pl.BlockSpec((ct, 2 * DC), lambda i: (i, 0)),
                   pl.BlockSpec((CW, DC), lambda i: (0, 0)), vec, vec, vec],
        out_shape=[jax.ShapeDtypeStruct((T, 2 * DC), BF16),
                   jax.ShapeDtypeStruct((CW, DC), F32), jax.ShapeDtypeStruct((1, DC), F32),
                   jax.ShapeDtypeStruct((1, DC), F32), jax.ShapeDtypeStruct((1, DC), F32)],
        scratch_shapes=[pltpu.VMEM((ct + HALO, DC), F32), pltpu.VMEM((8, ct + 24, DC), F32),
                        pltpu.VMEM((ct + HALO, DC), F32), pltpu.VMEM((8, ct + 24, DC), F32),
                        pltpu.VMEM((CW, 8, DC), F32), pltpu.VMEM((3, 8, DC), F32)],
        compiler_params=pltpu.CompilerParams(
            dimension_semantics=("arbitrary",), vmem_limit_bytes=VMEM_BIG),
    )(dom, dom, y, y, z, z, z, z, w_dw, g_ln, b_ln)


def _adam_math(w, g, m, v):
    m = ADAM_B1 * m + (1.0 - ADAM_B1) * g
    v = ADAM_B2 * v + (1.0 - ADAM_B2) * (g * g)
    m_hat = m / (1.0 - ADAM_B1 ** ADAM_STEP)
    v_hat = v / (1.0 - ADAM_B2 ** ADAM_STEP)
    delta = -ADAM_LR * (m_hat / (jnp.sqrt(v_hat) + ADAM_EPS) + ADAM_WD * w)
    return delta, m, v


def _adam(name, slots, w, m, v):
    rows, cols = w.shape
    tr = next(t for t in (256, 176, 128, 64, 32, 16, 8, rows) if rows % t == 0)

    def body(s_ref, w_ref, m_ref, v_ref, g_out, d_out, m_out, v_out):
        g = s_ref[0].astype(F32)
        for s in range(1, NDEV):
            g = g + s_ref[s].astype(F32)
        delta, mn, vn = _adam_math(w_ref[...], g, m_ref[...], v_ref[...])
        g_out[...] = g
        d_out[...] = delta
        m_out[...] = mn
        v_out[...] = vn

    tile = pl.BlockSpec((tr, cols), lambda i: (i, 0))
    sh = jax.ShapeDtypeStruct((rows, cols), F32)
    return pl.pallas_call(
        body, name=name, grid=(rows // tr,),
        in_specs=[pl.BlockSpec((NDEV, tr, cols), lambda i: (0, i, 0)), tile, tile, tile],
        out_specs=[tile] * 4, out_shape=[sh] * 4,
        compiler_params=pltpu.CompilerParams(
            dimension_semantics=("parallel",), vmem_limit_bytes=VMEM_MID),
    )(slots, w, m, v)


SMALL_NAMES = ("g_mix", "b_dw", "g_conv_ln", "b_conv_ln", "g_ffn", "g_ple", "b_pgate", "g_final")


def _pack_small(vecs, w_dw_full):
    rows = [jnp.pad(v.reshape(1, -1), ((0, 0), (0, SMALL_W - v.size))) for v in vecs]
    rows.append(jnp.pad(w_dw_full, ((0, 0), (0, SMALL_W - DC))))
    rows.append(jnp.zeros((SMALL_ROWS - len(vecs) - CW, SMALL_W), F32))
    return jnp.concatenate(rows, axis=0)


def kernel(x, p, g_mix, w_in, w_dw, b_dw, g_conv_ln, b_conv_ln, w_out, g_ffn, w_gate, w_up, w_down, g_ple, w_pgate, b_pgate, w_ple, g_final, loss_target, m_g_mix, m_w_in, m_w_dw, m_b_dw, m_g_conv_ln, m_b_conv_ln, m_w_out, m_g_ffn, m_w_gate, m_w_up, m_w_down, m_g_ple, m_w_pgate, m_b_pgate, m_w_ple, m_g_final, v_g_mix, v_w_in, v_w_dw, v_b_dw, v_g_conv_ln, v_b_conv_ln, v_w_out, v_g_ffn, v_w_gate, v_w_up, v_w_down, v_g_ple, v_w_pgate, v_b_pgate, v_w_ple, v_g_final):
    T = x.shape[1]
    me = 4 * lax.axis_index("x") + 2 * lax.axis_index("y") + lax.axis_index("c")
    xs = x.reshape(T, D)
    ps = p.reshape(T, DPLE).astype(BF16)
    tgt = loss_target.reshape(T, D)
    g_final2 = g_final.reshape(1, D)

    tr_names = ("w_gate", "w_up")
    big = dict(w_in=w_in[0], w_out=w_out[0], w_gate=w_gate[0].T, w_up=w_up[0].T, w_down=w_down[0],
               w_pgate=w_pgate[0], w_ple=w_ple[0])
    order = ("w_in", "w_out", "w_gate", "w_up", "w_down", "w_pgate", "w_ple")
    w_dw_g, w_in_s = _gather_two_level(
        "gather_first", [w_dw.reshape(CW, DC // NDEV), big["w_in"].astype(BF16)])
    w_dw_f = w_dw_g.transpose(1, 0, 2).reshape(CW, DC)
    later = order[1:]
    lands = _place("gather_place", [(big[n], False) for n in later], dtype=BF16)
    g_handles, g_token = _xstart("gather_start", [(None, False)] * len(later), lands, deps=[w_in_s])
    w_in_f = _shards_to_cols("w_in_natural", w_in_s)
    G = dict(zip(later, g_handles))

    a, z, *z_wide = _mm_in(xs, g_mix, w_in_f, deps=[g_token])
    zsrc = dict(zip(DILATIONS, [z] + z_wide))
    br = [_attn_fwd(f"attn_fwd_d{d}", zsrc[d], d, T) for d in DILATIONS]
    comb = list(_attn_combine([b[0] for b in br], [b[1] for b in br], T))
    o_attn, lse = comb[0], comb[1]
    osrc = dict(zip(DILATIONS, [o_attn] + comb[2:2 + len(WIDE)]))
    lsrc = dict(zip(DILATIONS, [lse] + comb[2 + len(WIDE):]))
    o_conv, y_conv = _conv_fwd(z, w_dw_f, b_dw, g_conv_ln, b_conv_ln)
    w_out_f = _xwait("gather_wait_w_out", G["w_out"], o_conv).reshape(D, D)
    h1, f = _mm_out(o_attn, o_conv, w_out_f, xs, g_ffn)
    w_gate_f = _xwait("gather_wait_w_gate", G["w_gate"], f).reshape(DFF, D)
    w_up_f = _xwait("gather_wait_w_up", G["w_up"], f).reshape(DFF, D)
    gate, up, act = _mm_gate_up(f, w_gate_f, w_up_f)
    w_down_f = _xwait("gather_wait_w_down", G["w_down"], act).reshape(DFF, D)
    h2, r = _mm_down(act, w_down_f, h1, g_ple)
    w_pgate_f = _xwait("gather_wait_w_pgate", G["w_pgate"], r).reshape(D, D)
    w_ple_f = _xwait("gather_wait_w_ple", G["w_ple"], r).transpose(1, 0, 2).reshape(DPLE, D)

    loss_part, dh3, dpe, dpg, d_g_final, d_b_pgate = _ple_loss(
        r, w_pgate_f, b_pgate, ps, w_ple_f, h2, tgt, g_final2)
    H = {}

    def send_grads(tag, named):
        items = [(v, True) for _, v in named]
        handles, token = _xstart(f"grads_start_{tag}", items, _place(f"grads_place_{tag}", items))
        H.update(zip([n for n, _ in named], handles))
        return token

    gw_pgate = _mm_tn("gw_pgate", r, dpg).reshape(NDEV, D // NDEV, D)
    gw_ple = _mm_tn("gw_ple", ps, dpe).reshape(DPLE, NDEV, D // NDEV).transpose(1, 0, 2)
    tok = send_grads("ple", [("w_pgate", gw_pgate), ("w_ple", gw_ple)])
    dh2, dh2b, d_g_ple = _mm_pgate_bwd(dpg, w_pgate_f, h2, g_ple, dh3, deps=[tok])
    ff_shards = lambda g: g.reshape(NDEV, N_FF, D)
    gw_down = ff_shards(_mm_tn_ff("gw_down", act, dh2b))
    tok = send_grads("down", [("w_down", gw_down)])
    dgate, dup = _mm_down_bwd(dh2b, w_down_f, gate, up, deps=[tok])
    gw_gate = ff_shards(_mm_tn_ff("gw_gate", dgate, f))
    gw_up = ff_shards(_mm_tn_ff("gw_up", dup, f))
    tok = send_grads("ffn", [("w_gate", gw_gate), ("w_up", gw_up)])
    df = _mm_ffn_in_bwd(dgate, w_gate_f, dup, w_up_f, deps=[tok])
    dh1, dh1b, d_g_ffn = _rms_bwd("rms_ffn_bwd", df, h1, g_ffn, dh2, True)
    gw_out = jnp.concatenate(
        [_mm_tn("gw_out_attn", o_attn, dh1b), _mm_tn("gw_out_conv", o_conv, dh1b)], axis=0)
    tok = send_grads("out", [("w_out", gw_out.reshape(NDEV, D // NDEV, D))])
    dom, *do_wide = _mm_out_bwd(dh1b, w_out_f, deps=[tok])
    dosrc = dict(zip(DILATIONS, [dom] + do_wide))
    dcvg, d_w_dw, d_b_dw, d_g_ln, d_b_ln = _conv_bwd(z, dom, y_conv, w_dw_f, g_conv_ln, b_conv_ln)
    dqs, dks, dvs = [], [], []
    for d in DILATIONS:
        dqs.append(_attn_bwd_q(f"attn_bwd_q_d{d}", zsrc[d], dosrc[d], osrc[d], lsrc[d], d, T))
        dk, dv = _attn_bwd_kv(f"attn_bwd_kv_d{d}", zsrc[d], dosrc[d], osrc[d], lsrc[d], d, T)
        dks.append(dk)
        dvs.append(dv)
    dz = _dz_assemble(dqs, dks, dvs, dcvg, T)
    gw_in = _cols_to_shards("gw_in_shards", _mm_tn_wide("gw_in", a, dz), N_IN)
    tok = send_grads("in", [("w_in", gw_in)])
    grad_x, d_g_mix = _mm_in_bwd(dz, w_in_f, xs, g_mix, dh1, deps=[tok])

    small_part = _pack_small(
        [d_g_mix, d_b_dw, d_g_ln, d_b_ln, d_g_ffn, d_g_ple, d_b_pgate, d_g_final], d_w_dw)
    small_slots = _exchange("exchange_small_grads", [(small_part, False)])[0]
    S = {n: _xwait(f"grads_wait_{n}", H[n], small_slots)
         for n in ("w_pgate", "w_ple", "w_down", "w_gate", "w_up", "w_out", "w_in")}

    mom = dict(w_in=(m_w_in, v_w_in), w_out=(m_w_out, v_w_out), w_gate=(m_w_gate, v_w_gate),
               w_up=(m_w_up, v_w_up), w_down=(m_w_down, v_w_down), w_pgate=(m_w_pgate, v_w_pgate),
               w_ple=(m_w_ple, v_w_ple))
    upd = {}
    for n in order:
        m_n, v_n = mom[n][0][0], mom[n][1][0]
        if n in tr_names:
            res = _adam(f"adam_{n}", S[n], big[n], m_n.T, v_n.T)
            upd[n] = [t.T[None] for t in res]
        else:
            res = _adam(f"adam_{n}", S[n], big[n], m_n, v_n)
            upd[n] = [t[None] for t in res]

    def lanes(v):
        full = jnp.zeros((CW, NDEV, DC // NDEV), F32)
        full = lax.dynamic_update_slice(full, v.reshape(CW, 1, DC // NDEV), (0, me, 0))
        return full.reshape(CW, DC)

    small_w = _pack_small([g_mix, b_dw, g_conv_ln, b_conv_ln, g_ffn, g_ple, b_pgate, g_final2], lanes(w_dw))
    small_m = _pack_small([m_g_mix, m_b_dw, m_g_conv_ln, m_b_conv_ln, m_g_ffn, m_g_ple, m_b_pgate,
                           m_g_final.reshape(1, D)], lanes(m_w_dw))
    small_v = _pack_small([v_g_mix, v_b_dw, v_g_conv_ln, v_b_conv_ln, v_g_ffn, v_g_ple, v_b_pgate,
                           v_g_final.reshape(1, D)], lanes(v_w_dw))
    small_res = _adam("adam_small", small_slots, small_w, small_m, small_v)

    def unpack(t):
        out = {}
        widths = dict(g_mix=D, b_dw=DC, g_conv_ln=DC, b_conv_ln=DC, g_ffn=D, g_ple=D, b_pgate=D, g_final=D)
        for i, n in enumerate(SMALL_NAMES):
            out[n] = t[i:i + 1, :widths[n]]
        out["g_final"] = out["g_final"].reshape(D)
        taps = t[len(SMALL_NAMES):len(SMALL_NAMES) + CW, :DC].reshape(CW, NDEV, DC // NDEV)
        out["w_dw"] = lax.dynamic_slice(taps, (0, me, 0), (CW, 1, DC // NDEV))[None]
        return out

    small = [unpack(t) for t in small_res]

    loss = lax.psum(loss_part[0, 0], ("x", "y", "c"))
    names = ("g_mix", "w_in", "w_dw", "b_dw", "g_conv_ln", "b_conv_ln", "w_out", "g_ffn", "w_gate",
             "w_up", "w_down", "g_ple", "w_pgate", "b_pgate", "w_ple", "g_final")
    outs = [loss, grad_x.reshape(1, T, D)]
    for kind in range(4):
        for n in names:
            outs.append(upd[n][kind] if n in upd else small[kind][n])
    return tuple(outs)
```

```python
import jax
import jax.numpy as jnp
from jax import lax
from jax.experimental import pallas as pl
from jax.experimental.pallas import tpu as pltpu

F32 = jnp.float32
BF16 = jnp.bfloat16

NDEV = 8
D = 2048
NH = 8
DH = 128
DA = NH * DH
DC = D - DA
DIN = 3 * DA + 2 * DC
DFF = 5632
DPLE = 256
BLK = 128
DILATIONS = (1, 4, 16)
CW = 31
EPS = 1e-6
N_IN = DIN // NDEV
N_FF = DFF // NDEV
NEG = -1e30

ADAM_LR = 0.001
ADAM_B1 = 0.9
ADAM_B2 = 0.999
ADAM_EPS = 1e-08
ADAM_WD = 0.01
ADAM_STEP = 10

VMEM_CAP_V7X = 64 * 1024 * 1024
VMEM_BIG = VMEM_CAP_V7X - 12 * 1024 * 1024
VMEM_MID = 40 * 1024 * 1024

SMALL_W = 2048
SMALL_ROWS = 40


def _sigmoid(v):
    return 1.0 / (1.0 + jnp.exp(-v))


def _dot(a, b, contract):
    return lax.dot_general(a, b, (contract, ((), ())), preferred_element_type=F32)


NN = ((1,), (0,))
NT = ((1,), (1,))
TN = ((0,), (0,))


def _exchange(name, items):
    n = len(items)
    out_shape = [
        jax.ShapeDtypeStruct((NDEV,) + (a.shape[1:] if sc else a.shape), a.dtype)
        for a, sc in items
    ]
    scat = [sc for _, sc in items]

    def body(*refs):
        srcs = refs[:n]
        dsts = refs[n:2 * n]
        send_sems, recv_sems, loc_sems = refs[2 * n:]
        x = lax.axis_index("x")
        y = lax.axis_index("y")
        c = lax.axis_index("c")
        me = 4 * x + 2 * y + c

        local = []
        for i in range(n):
            src = srcs[i].at[me] if scat[i] else srcs[i]
            cp = pltpu.make_async_copy(src, dsts[i].at[me], loc_sems.at[i])
            cp.start()
            local.append(cp)

        remote = []
        for k in range(1, NDEV):
            px = (1 - x) if (k >> 2) & 1 else x
            py = (1 - y) if (k >> 1) & 1 else y
            pc = (1 - c) if k & 1 else c
            peer = 4 * px + 2 * py + pc
            for i in range(n):
                sem = i * (NDEV - 1) + k - 1
                src = srcs[i].at[peer] if scat[i] else srcs[i]
                send = pltpu.make_async_remote_copy(
                    src_ref=src, dst_ref=dsts[i].at[me],
                    send_sem=send_sems.at[sem], recv_sem=recv_sems.at[sem],
                    device_id=(px, py, pc), device_id_type=pl.DeviceIdType.MESH)
                send.start()
                recv = pltpu.make_async_remote_copy(
                    src_ref=src, dst_ref=dsts[i].at[peer],
                    send_sem=send_sems.at[sem], recv_sem=recv_sems.at[sem],
                    device_id=(px, py, pc), device_id_type=pl.DeviceIdType.MESH)
                remote.append((send, recv))
        for send, recv in remote:
            recv.wait_recv()
            send.wait_send()
        for cp in local:
            cp.wait()

    any_spec = pl.BlockSpec(memory_space=pl.ANY)
    return pl.pallas_call(
        body, name=name,
        in_specs=[any_spec] * n, out_specs=[any_spec] * n, out_shape=out_shape,
        scratch_shapes=[
            pltpu.SemaphoreType.DMA((n * (NDEV - 1),)),
            pltpu.SemaphoreType.DMA((n * (NDEV - 1),)),
            pltpu.SemaphoreType.DMA((n,)),
        ],
    )(*[a for a, _ in items])


def _gather_two_level(name, arrays):
    n = len(arrays)
    per = NDEV - 1

    def body(*refs):
        srcs = refs[:n]
        dsts = refs[n:2 * n]
        send_sems, recv_sems, loc_sems = refs[2 * n:]
        x = lax.axis_index("x")
        y = lax.axis_index("y")
        c = lax.axis_index("c")
        idx = lambda px, py, pc: 4 * px + 2 * py + pc
        me, sibling = (x, y, c), (x, y, 1 - c)
        chips = [(1 - x, y), (x, 1 - y), (1 - x, 1 - y)]

        def copy(i, k, block, to, src=None):
            slot = dsts[i].at[idx(*block)]
            return pltpu.make_async_remote_copy(
                src_ref=slot if src is None else src, dst_ref=slot,
                send_sem=send_sems.at[i * per + k], recv_sem=recv_sems.at[i * per + k],
                device_id=to, device_id_type=pl.DeviceIdType.MESH)

        mine, sent = [], []
        for i in range(n):
            cp = pltpu.make_async_copy(srcs[i], dsts[i].at[idx(*me)], loc_sems.at[i])
            cp.start()
            mine.append(cp)
            first = [copy(i, 0, me, sibling, src=srcs[i])]
            first += [copy(i, 1 + j, me, (*chip, c), src=srcs[i]) for j, chip in enumerate(chips)]
            for cp in first:
                cp.start()
            sent += first
        for j, chip in enumerate(chips):
            for i in range(n):
                copy(i, 1 + j, (*chip, c), me).wait_recv()
                fwd = copy(i, 4 + j, (*chip, c), sibling)
                fwd.start()
                sent.append(fwd)
        for i in range(n):
            copy(i, 0, sibling, me).wait_recv()
            for j, chip in enumerate(chips):
                copy(i, 4 + j, (*chip, 1 - c), me).wait_recv()
        for cp in sent:
            cp.wait_send()
        for cp in mine:
            cp.wait()

    any_spec = pl.BlockSpec(memory_space=pl.ANY)
    return pl.pallas_call(
        body, name=name, in_specs=[any_spec] * n, out_specs=[any_spec] * n,
        out_shape=[jax.ShapeDtypeStruct((NDEV,) + a.shape, a.dtype) for a in arrays],
        scratch_shapes=[pltpu.SemaphoreType.DMA((n * per,)), pltpu.SemaphoreType.DMA((n * per,)),
                        pltpu.SemaphoreType.DMA((n,))],
    )(*arrays)


HBM_SPEC = pl.BlockSpec(memory_space=pltpu.HBM)
SEM_SPEC = pl.BlockSpec(memory_space=pltpu.SEMAPHORE)
ANY_SPEC = pl.BlockSpec(memory_space=pl.ANY)
EFFECT = pltpu.SideEffectType.DATAFLOW_SIDE_EFFECTING


def _peer_of(k):
    x = lax.axis_index("x")
    y = lax.axis_index("y")
    c = lax.axis_index("c")
    px = (1 - x) if (k >> 2) & 1 else x
    py = (1 - y) if (k >> 1) & 1 else y
    pc = (1 - c) if k & 1 else c
    return (px, py, pc), 4 * px + 2 * py + pc


def _my_index():
    return 4 * lax.axis_index("x") + 2 * lax.axis_index("y") + lax.axis_index("c")


def _slot_shape(a, sc):
    return (NDEV,) + (a.shape[1:] if sc else a.shape)


def _divisor_tile(rows):
    return next((t for t in (512, 256, 176, 128, 64, 32, 16) if rows % t == 0), rows)


def _place(name, items, dtype=None):
    lands = []
    for idx, (a, sc) in enumerate(items):
        rows, cols = a.shape[-2:]
        tr = _divisor_tile(rows)
        out_dtype = a.dtype if dtype is None else dtype

        def body(s_ref, o_ref):
            o_ref[...] = s_ref[...].astype(o_ref.dtype)

        mine = pl.BlockSpec((None, tr, cols), lambda i: (_my_index(), i, 0))
        lands.append(pl.pallas_call(
            body, name=f"{name}_{idx}", grid=(rows // tr,),
            in_specs=[mine if sc else pl.BlockSpec((tr, cols), lambda i: (i, 0))],
            out_specs=mine,
            out_shape=jax.ShapeDtypeStruct(_slot_shape(a, sc), out_dtype),
            compiler_params=pltpu.CompilerParams(dimension_semantics=("parallel",)),
        )(a))
    return lands


def _xstart(name, items, lands, deps=()):
    n = len(items)
    scat = [sc for _, sc in items]
    srcs_in = [a for a, sc in items if sc]
    n_src = len(srcs_in)
    src_pos = {i: p for p, i in enumerate(i for i in range(n) if scat[i])}

    def body(*refs):
        srcs = refs[:n_src]
        lzs = refs[n_src:n_src + n]
        outs = refs[n_src + n + len(deps):]
        send_sems, recv_sems, token = outs[:n], outs[n:2 * n], outs[-1]
        me = _my_index()
        for i in range(n):
            for k in range(1, NDEV):
                peer_id, peer = _peer_of(k)
                src = srcs[src_pos[i]].at[peer] if scat[i] else lzs[i].at[me]
                pltpu.make_async_remote_copy(
                    src_ref=src, dst_ref=lzs[i].at[me],
                    send_sem=send_sems[i].at[k - 1], recv_sem=recv_sems[i].at[k - 1],
                    device_id=peer_id, device_id_type=pl.DeviceIdType.MESH).start()
        token[...] = jnp.zeros_like(token)

    sem = pltpu.SemaphoreType.DMA((NDEV - 1,))
    thru = srcs_in + list(lands)
    res = pl.pallas_call(
        body, name=name,
        in_specs=[HBM_SPEC] * len(thru) + [ANY_SPEC] * len(deps),
        out_specs=[SEM_SPEC] * (2 * n) + [HBM_SPEC] * len(thru) + [pl.BlockSpec(memory_space=pltpu.VMEM)],
        out_shape=[sem] * (2 * n) + [pltpu.HBM(t.shape, t.dtype) for t in thru]
        + [jax.ShapeDtypeStruct((8, 128), F32)],
        input_output_aliases={i: 2 * n + i for i in range(len(thru))},
        compiler_params=pltpu.CompilerParams(has_side_effects=EFFECT),
    )(*[pltpu.with_memory_space_constraint(t, pltpu.HBM) for t in thru], *deps)
    handles = [(res[i], res[n + i], res[2 * n + src_pos[i]] if scat[i] else None,
                res[2 * n + n_src + i]) for i in range(n)]
    return handles, res[-1]


def _xwait(name, handle, after):
    send_sem, recv_sem, src, land = handle
    sc = src is not None

    def body(*refs):
        land_ref = refs[1] if sc else refs[0]
        send_ref, recv_ref = (refs[2], refs[3]) if sc else (refs[1], refs[2])
        me = _my_index()
        for k in range(1, NDEV):
            peer_id, peer = _peer_of(k)
            cp = pltpu.make_async_remote_copy(
                src_ref=refs[0].at[peer] if sc else land_ref.at[me], dst_ref=land_ref.at[peer],
                send_sem=send_ref.at[k - 1], recv_sem=recv_ref.at[k - 1],
                device_id=peer_id, device_id_type=pl.DeviceIdType.MESH)
            cp.wait_send()
            cp.wait_recv()

    thru = ([src] if sc else []) + [land]
    return pl.pallas_call(
        body, name=name,
        in_specs=[HBM_SPEC] * len(thru) + [SEM_SPEC, SEM_SPEC, ANY_SPEC],
        out_specs=[HBM_SPEC] * len(thru),
        out_shape=[pltpu.HBM(t.shape, t.dtype) for t in thru],
        input_output_aliases={i: i for i in range(len(thru))},
        compiler_params=pltpu.CompilerParams(has_side_effects=EFFECT),
    )(*thru, send_sem, recv_sem, after)[-1]


def _mm(name, grid, in_specs, operands, out_specs, out_shape, contract, n_pairs, epilogue,
        acc_shape=None, vmem=VMEM_BIG, deps=(), group=1, a_cols=None, carry=()):
    nk = grid[2]
    n_carry = len(carry)

    def shard(ref, s, is_a):
        if group == 1:
            return ref[...]
        if is_a and a_cols is not None:
            return ref[:, s * a_cols:(s + 1) * a_cols]
        return ref[s]
    n_extra = len(operands) - 2 * n_pairs
    n_out = len(out_shape)
    n_in = len(operands) + len(deps)
    in_specs = list(in_specs) + [ANY_SPEC] * len(deps)
    operands = list(operands) + list(deps)

    def body(*refs):
        ab = refs[:2 * n_pairs]
        extras = refs[2 * n_pairs:2 * n_pairs + n_extra]
        outs = refs[n_in:n_in + n_out]
        kept = refs[n_in + n_out:n_in + n_out + n_carry]
        finish = (lambda acc: epilogue(acc, extras, outs, kept)) if n_carry else (
            lambda acc: epilogue(acc, extras, outs))
        dots = [(ab[2 * p], ab[2 * p + 1], s) for p in range(n_pairs) for s in range(group)]
        if nk == 1:
            part = None
            for a_ref, b_ref, s in dots:
                d = _dot(shard(a_ref, s, True), shard(b_ref, s, False), contract)
                part = d if part is None else part + d
            finish(part)
        else:
            acc_ref = refs[-1]
            k = pl.program_id(2)

            @pl.when(k == 0)
            def _():
                acc_ref[...] = jnp.zeros_like(acc_ref)

            for a_ref, b_ref, s in dots:
                acc_ref[...] += _dot(shard(a_ref, s, True), shard(b_ref, s, False), contract)

            @pl.when(k == nk - 1)
            def _():
                finish(acc_ref[...])

    scratch = list(carry) + ([pltpu.VMEM(acc_shape, F32)] if nk > 1 else [])
    semantics = ("arbitrary",) * 3 if n_carry else ("parallel", "parallel", "arbitrary")
    return pl.pallas_call(
        body, name=name, grid=grid, in_specs=in_specs, out_specs=out_specs, out_shape=out_shape,
        scratch_shapes=scratch,
        compiler_params=pltpu.CompilerParams(dimension_semantics=semantics, vmem_limit_bytes=vmem),
    )(*operands)


def _ep_cast(dtype):
    def ep(acc, extras, outs):
        outs[0][...] = acc.astype(dtype)
    return ep


def _ep_rms_bwd(n_rows_steps):
    def ep(acc, extras, outs, kept):
        h_ref, g_ref, dres_ref = extras
        gacc = kept[0]
        i = pl.program_id(0)
        v = h_ref[...]
        r = lax.rsqrt(jnp.mean(v * v, axis=-1, keepdims=True) + EPS)
        nrm = v * r
        dn = acc * g_ref[...]
        dh = dres_ref[...].astype(F32) + r * (dn - nrm * jnp.mean(dn * nrm, axis=-1, keepdims=True))
        outs[0][...] = dh.astype(outs[0].dtype)

        @pl.when(i == 0)
        def _():
            gacc[...] = jnp.zeros_like(gacc)

        gacc[...] += _fold8(acc * nrm)

        @pl.when(i == n_rows_steps - 1)
        def _():
            outs[-1][...] = jnp.sum(gacc[...], axis=0, keepdims=True)
    return ep


def _ep_resid_norm(acc, extras, outs):
    h = extras[0][...] + acc
    outs[0][...] = h
    r = lax.rsqrt(jnp.mean(h * h, axis=-1, keepdims=True) + EPS)
    outs[1][...] = (h * r * extras[1][...]).astype(BF16)


def _ep_swiglu_bwd(acc, extras, outs):
    outs[0][...] = (acc * extras[0][...].astype(F32)).astype(BF16)
    outs[1][...] = (acc * extras[1][...].astype(F32)).astype(BF16)


MXU_COLS_V7X = 256


def _col_chunks(n):
    return [slice(c, min(c + MXU_COLS_V7X, n)) for c in range(0, n, MXU_COLS_V7X)]


def _row_tile(T):
    return min(1024, T)


def _tn_rows(T):
    return min(2048, T)


WIDE = tuple(d for d in DILATIONS if d > 1)


LANES = 128


def _lane_tile(c):
    return slice(c * LANES, (c + 1) * LANES)


def _to_lane_tiles(scr, val):
    for c in range(scr.shape[0]):
        scr[c] = val[:, _lane_tile(c)]


def _emit_class_major(scr, refs, rows):
    for d, ref in zip(WIDE, refs):
        for r in range(d):
            for c in range(scr.shape[0]):
                ref[r, :, _lane_tile(c)] = scr[c, pl.ds(r, rows // d, stride=d), :].astype(ref.dtype)


def _shards_to_cols(name, w):
    _, rows, n = w.shape
    tr = rows

    def body(s_ref, o_ref):
        o_ref[...] = s_ref[...]

    return pl.pallas_call(
        body, name=name, grid=(rows // tr, NDEV),
        in_specs=[pl.BlockSpec((None, tr, n), lambda i, j: (j, i, 0))],
        out_specs=pl.BlockSpec((tr, n), lambda i, j: (i, j)),
        out_shape=jax.ShapeDtypeStruct((rows, NDEV * n), w.dtype),
        compiler_params=pltpu.CompilerParams(dimension_semantics=("parallel", "parallel")),
    )(w)


def _cols_to_shards(name, g, n):
    rows = g.shape[0]
    tr = rows

    def body(s_ref, o_ref):
        o_ref[...] = s_ref[...]

    return pl.pallas_call(
        body, name=name, grid=(rows // tr, NDEV),
        in_specs=[pl.BlockSpec((tr, n), lambda i, j: (i, j))],
        out_specs=pl.BlockSpec((None, tr, n), lambda i, j: (j, i, 0)),
        out_shape=jax.ShapeDtypeStruct((NDEV, rows, n), g.dtype),
        compiler_params=pltpu.CompilerParams(dimension_semantics=("parallel", "parallel")),
    )(g)


IN_TN = DA


def _mm_in(x, g, w_in, deps=()):
    T = x.shape[0]
    tm = min(TM_FULL_ROW, T)
    nq = 3

    def body(x_ref, g_ref, w_ref, *rest):
        a_ref, z_ref, *rest = rest[len(deps):]
        scr = rest[-1]
        j = pl.program_id(1)

        @pl.when(j == 0)
        def _():
            v = x_ref[...]
            r = lax.rsqrt(jnp.mean(v * v, axis=-1, keepdims=True) + EPS)
            a_ref[...] = (v * r * g_ref[...]).astype(BF16)

        @pl.when(j >= nq)
        def _():
            z_ref[...] = _dot(a_ref[...], w_ref[...], NN).astype(BF16)

        @pl.when(j < nq)
        def _():
            av = a_ref[...]
            chunks = _col_chunks(IN_TN)
            pending = _dot(av, w_ref[:, chunks[0]], NN)
            for ci, cols in enumerate(chunks):
                nxt = _dot(av, w_ref[:, chunks[ci + 1]], NN) if ci + 1 < len(chunks) else None
                z_ref[:, cols] = pending.astype(BF16)
                for c in range(cols.start // LANES, cols.stop // LANES):
                    scr[c] = pending[:, c * LANES - cols.start:(c + 1) * LANES - cols.start]
                    for d, ref in zip(WIDE, rest[:-1]):
                        for r in range(d):
                            ref[r, :, _lane_tile(c)] = scr[c, pl.ds(r, tm // d, stride=d), :].astype(BF16)
                pending = nxt

    cm_spec = lambda d: pl.BlockSpec((d, tm // d, IN_TN), lambda i, j: (0, i, jnp.minimum(j, nq - 1)))
    row = pl.BlockSpec((tm, D), lambda i, j: (i, 0))
    return pl.pallas_call(
        body, name="mm_in", grid=(T // tm, DIN // IN_TN),
        in_specs=[row, pl.BlockSpec((1, D), lambda i, j: (0, 0)),
                  pl.BlockSpec((D, IN_TN), lambda i, j: (0, j))] + [ANY_SPEC] * len(deps),
        out_specs=[row, pl.BlockSpec((tm, IN_TN), lambda i, j: (i, j))] + [cm_spec(d) for d in WIDE],
        out_shape=[jax.ShapeDtypeStruct((T, D), BF16), jax.ShapeDtypeStruct((T, DIN), BF16)]
        + [jax.ShapeDtypeStruct((d, T // d, nq * IN_TN), BF16) for d in WIDE],
        scratch_shapes=[pltpu.VMEM((IN_TN // LANES, tm, LANES), F32)],
        compiler_params=pltpu.CompilerParams(
            dimension_semantics=("parallel", "arbitrary"), vmem_limit_bytes=VMEM_BIG),
    )(x, g, w_in, *deps)


def _mm_out_bwd(dh1b, w_out, deps=()):
    T = dh1b.shape[0]
    tm = _row_tile(T)

    def body(dy_ref, w_ref, *rest):
        rest = rest[len(deps):]
        dom_ref, scr = rest[0], rest[-1]
        acc = _dot(dy_ref[...], w_ref[...], NT)
        dom_ref[...] = acc.astype(BF16)

        @pl.when(pl.program_id(1) == 0)
        def _():
            _to_lane_tiles(scr, acc)
            _emit_class_major(scr, rest[1:-1], tm)

    return pl.pallas_call(
        body, name="mm_out_bwd", grid=(T // tm, D // DA),
        in_specs=[pl.BlockSpec((tm, D), lambda i, j: (i, 0)),
                  pl.BlockSpec((DA, D), lambda i, j: (j, 0))] + [ANY_SPEC] * len(deps),
        out_specs=[pl.BlockSpec((tm, DA), lambda i, j: (i, j))]
        + [pl.BlockSpec((d, tm // d, DA), lambda i, j: (0, i, 0)) for d in WIDE],
        out_shape=[jax.ShapeDtypeStruct((T, D), BF16)]
        + [jax.ShapeDtypeStruct((d, T // d, DA), BF16) for d in WIDE],
        scratch_shapes=[pltpu.VMEM((DA // LANES, tm, LANES), F32)],
        compiler_params=pltpu.CompilerParams(
            dimension_semantics=("parallel", "arbitrary"), vmem_limit_bytes=VMEM_BIG),
    )(dh1b, w_out, *deps)


TM_FULL_ROW = 512


def _full_row_specs(tm):
    row = pl.BlockSpec((tm, D), lambda i, j, k: (i, 0))
    return row, pl.BlockSpec((1, D), lambda i, j, k: (0, 0))


def _mm_out(o_attn, o_conv, w_out, x, g_next):
    T = x.shape[0]
    tm = min(TM_FULL_ROW, T)
    row, vec = _full_row_specs(tm)
    return _mm(
        "mm_out", (T // tm, 1, 1),
        [pl.BlockSpec((tm, DA), lambda i, j, k: (i, 0)),
         pl.BlockSpec((DA, D), lambda i, j, k: (0, 0)),
         pl.BlockSpec((tm, DC), lambda i, j, k: (i, 0)),
         pl.BlockSpec((DC, D), lambda i, j, k: (1, 0)),
         row, vec],
        [o_attn, w_out, o_conv, w_out, x, g_next],
        [row, row],
        [jax.ShapeDtypeStruct((T, D), F32), jax.ShapeDtypeStruct((T, D), BF16)], NN, 2,
        _ep_resid_norm)


FF_TN = 512
FF_TK = 2 * N_FF


def _mm_gate_up(f, wg_t, wu_t):
    T = f.shape[0]
    tm = _row_tile(T)

    def body(f_ref, wg_ref, wu_ref, dg_ref, du_ref, a_ref):
        fv = f_ref[...]
        g = _dot(fv, wg_ref[...], NT)
        u = _dot(fv, wu_ref[...], NT)
        sg = _sigmoid(g)
        silu = g * sg
        dg_ref[...] = (u * (sg * (1.0 + g * (1.0 - sg)))).astype(BF16)
        du_ref[...] = silu.astype(BF16)
        a_ref[...] = (silu * u).astype(BF16)

    wspec = pl.BlockSpec((FF_TN, D), lambda i, j: (j, 0))
    ospec = pl.BlockSpec((tm, FF_TN), lambda i, j: (i, j))
    sh = jax.ShapeDtypeStruct((T, DFF), BF16)
    return pl.pallas_call(
        body, name="mm_gate_up", grid=(T // tm, DFF // FF_TN),
        in_specs=[pl.BlockSpec((tm, D), lambda i, j: (i, 0)), wspec, wspec],
        out_specs=[ospec, ospec, ospec], out_shape=[sh, sh, sh],
        compiler_params=pltpu.CompilerParams(
            dimension_semantics=("parallel", "parallel"), vmem_limit_bytes=VMEM_BIG),
    )(f, wg_t, wu_t)


def _mm_down(act, w_down, h1, g_next):
    T = h1.shape[0]
    tm = min(TM_FULL_ROW, T)
    row, vec = _full_row_specs(tm)
    return _mm(
        "mm_down", (T // tm, 1, DFF // FF_TK),
        [pl.BlockSpec((tm, FF_TK), lambda i, j, k: (i, k)),
         pl.BlockSpec((FF_TK, D), lambda i, j, k: (k, 0)),
         row, vec],
        [act, w_down, h1, g_next],
        [row, row],
        [jax.ShapeDtypeStruct((T, D), F32), jax.ShapeDtypeStruct((T, D), BF16)], NN, 1,
        _ep_resid_norm, acc_shape=(tm, D))


def _ple_loss(r, w_pgate, b_pgate, p, w_ple, h2, target, g_final):
    T = h2.shape[0]
    tm = min(256, T)
    nt = T // tm

    def body(r_ref, wg_ref, b_ref, p_ref, wp_ref, h2_ref, t_ref, g_ref,
             loss_ref, dh_ref, dpe_ref, dpg_ref, dgf_ref, dbp_ref, lacc, gacc, bacc):
        i = pl.program_id(0)
        gte = _sigmoid(_dot(r_ref[...], wg_ref[...], NN) + b_ref[...])
        pe = _dot(p_ref[...], wp_ref[...], NN)
        v = h2_ref[...] + pe * gte
        rr = lax.rsqrt(jnp.mean(v * v, axis=-1, keepdims=True) + EPS)
        nrm = v * rr
        g = g_ref[...]
        err = nrm * g - t_ref[...]
        dy = err * (1.0 / D)
        dn = dy * g
        dh = rr * (dn - nrm * jnp.mean(dn * nrm, axis=-1, keepdims=True))
        dh_ref[...] = dh.astype(BF16)
        dpe_ref[...] = (dh * gte).astype(BF16)
        dpg = dh * pe * gte * (1.0 - gte)
        dpg_ref[...] = dpg.astype(BF16)

        @pl.when(i == 0)
        def _():
            lacc[...] = jnp.zeros_like(lacc)
            gacc[...] = jnp.zeros_like(gacc)
            bacc[...] = jnp.zeros_like(bacc)

        lacc[...] += _fold8(err * err)
        gacc[...] += _fold8(dy * nrm)
        bacc[...] += _fold8(dpg)

        @pl.when(i == nt - 1)
        def _():
            tot = jnp.sum(jnp.sum(lacc[...], axis=0, keepdims=True), axis=1, keepdims=True)
            loss_ref[...] = jnp.broadcast_to(tot * (0.5 / D), (1, 128))
            dgf_ref[...] = jnp.sum(gacc[...], axis=0, keepdims=True)
            dbp_ref[...] = jnp.sum(bacc[...], axis=0, keepdims=True)

    row = pl.BlockSpec((tm, D), lambda i: (i, 0))
    vec = pl.BlockSpec((1, D), lambda i: (0, 0))
    return pl.pallas_call(
        body, name="ple_loss", grid=(nt,),
        in_specs=[row, pl.BlockSpec((D, D), lambda i: (0, 0)), vec,
                  pl.BlockSpec((tm, DPLE), lambda i: (i, 0)),
                  pl.BlockSpec((DPLE, D), lambda i: (0, 0)), row, row, vec],
        out_specs=[pl.BlockSpec((1, 128), lambda i: (0, 0)), row, row, row, vec, vec],
        out_shape=[jax.ShapeDtypeStruct((1, 128), F32), jax.ShapeDtypeStruct((T, D), BF16),
                   jax.ShapeDtypeStruct((T, D), BF16), jax.ShapeDtypeStruct((T, D), BF16),
                   jax.ShapeDtypeStruct((1, D), F32), jax.ShapeDtypeStruct((1, D), F32)],
        scratch_shapes=[pltpu.VMEM((8, D), F32)] * 3,
        compiler_params=pltpu.CompilerParams(
            dimension_semantics=("arbitrary",), vmem_limit_bytes=VMEM_BIG),
    )(r, w_pgate, b_pgate, p, w_ple, h2, target, g_final)


def _mm_down_bwd(dh2, w_down, g, u, deps=()):
    T = dh2.shape[0]
    tm = _tn_rows(T)
    gspec = pl.BlockSpec((tm, FF_TN), lambda i, j, k: (i, j))
    sh = jax.ShapeDtypeStruct((T, DFF), BF16)
    return _mm(
        "mm_down_bwd", (T // tm, DFF // FF_TN, 1),
        [pl.BlockSpec((tm, D), lambda i, j, k: (i, 0)),
         pl.BlockSpec((FF_TN, D), lambda i, j, k: (j, 0)),
         gspec, gspec],
        [dh2, w_down, g, u],
        [gspec, gspec], [sh, sh], NT, 1, _ep_swiglu_bwd, deps=deps)


def _mm_ffn_in_bwd(dg, wg_t, du, wu_t, deps=()):
    T = dg.shape[0]
    tm = _row_tile(T)
    tn = 1024
    aspec = pl.BlockSpec((tm, FF_TK), lambda i, j, k: (i, k))
    wspec = pl.BlockSpec((FF_TK, tn), lambda i, j, k: (k, j))
    return _mm(
        "mm_ffn_in_bwd", (T // tm, D // tn, DFF // FF_TK),
        [aspec, wspec, aspec, wspec], [dg, wg_t, du, wu_t],
        [pl.BlockSpec((tm, tn), lambda i, j, k: (i, j))],
        [jax.ShapeDtypeStruct((T, D), BF16)], NN, 2, _ep_cast(BF16), acc_shape=(tm, tn),
        deps=deps)[0]


def _mm_in_bwd(dz, w_in, x, g, dres, deps=()):
    T = dz.shape[0]
    tm = min(TM_FULL_ROW, T)
    tk = DIN // 4
    row, vec = _full_row_specs(tm)
    return _mm(
        "mm_in_bwd", (T // tm, 1, DIN // tk),
        [pl.BlockSpec((tm, tk), lambda i, j, k: (i, k)),
         pl.BlockSpec((D, tk), lambda i, j, k: (0, k)),
         row, vec, row],
        [dz, w_in, x, g, dres],
        [row, vec],
        [jax.ShapeDtypeStruct((T, D), F32), jax.ShapeDtypeStruct((1, D), F32)], NT, 1,
        _ep_rms_bwd(T // tm), acc_shape=(tm, D), deps=deps,
        carry=[pltpu.VMEM((8, D), F32)])


def _mm_pgate_bwd(dpg, w_pgate, h2, g, dres, deps=()):
    T = dpg.shape[0]
    tm = min(256, T)
    row, vec = _full_row_specs(tm)
    return _mm(
        "mm_pgate_bwd", (T // tm, 1, 1),
        [row, pl.BlockSpec((D, D), lambda i, j, k: (0, 0)), row, vec, row],
        [dpg, w_pgate, h2, g, dres],
        [row, vec],
        [jax.ShapeDtypeStruct((T, D), BF16), jax.ShapeDtypeStruct((1, D), F32)], NT, 1,
        _ep_rms_bwd(T // tm), deps=deps, carry=[pltpu.VMEM((8, D), F32)])


def _mm_tn(name, a, b, tj=None):
    T, idim = a.shape
    jdim = b.shape[1]
    tt = _row_tile(T)
    ti = min(idim, 1024)
    tj = jdim if tj is None else tj
    return _mm(
        name, (idim // ti, jdim // tj, T // tt),
        [pl.BlockSpec((tt, ti), lambda i, j, k: (k, i)),
         pl.BlockSpec((tt, tj), lambda i, j, k: (k, j))],
        [a, b],
        [pl.BlockSpec((ti, tj), lambda i, j, k: (i, j))],
        [jax.ShapeDtypeStruct((idim, jdim), BF16)], TN, 1, _ep_cast(BF16), acc_shape=(ti, tj))[0]


def _mm_tn_wide(name, a, b):
    T = a.shape[0]
    jdim = b.shape[1]
    tt = _row_tile(T)
    return _mm(
        name, (1, jdim // IN_TN, T // tt),
        [pl.BlockSpec((tt, D), lambda i, j, k: (k, 0)),
         pl.BlockSpec((tt, IN_TN), lambda i, j, k: (k, j))],
        [a, b],
        [pl.BlockSpec((D, IN_TN), lambda i, j, k: (0, j))],
        [jax.ShapeDtypeStruct((D, jdim), BF16)], TN, 1, _ep_cast(BF16),
        acc_shape=(D, IN_TN))[0]


def _mm_tn_ff(name, a, b):
    T = b.shape[0]
    tt = _tn_rows(T)
    return _mm(
        name, (DFF // FF_TN, 1, T // tt),
        [pl.BlockSpec((tt, FF_TN), lambda i, j, k: (k, i)),
         pl.BlockSpec((tt, D), lambda i, j, k: (k, 0))],
        [a, b],
        [pl.BlockSpec((FF_TN, D), lambda i, j, k: (i, 0))],
        [jax.ShapeDtypeStruct((DFF, D), BF16)], TN, 1, _ep_cast(BF16),
        acc_shape=(FF_TN, D))[0]


TR = 256


def _rows(T):
    return min(TR, T)


def _fold8(v):
    return jnp.sum(v.reshape(v.shape[0] // 8, 8, v.shape[1]), axis=0)


def _rms_bwd(name, dn_out, h, g, dres):
    T = h.shape[0]
    tr = _rows(T)
    nt = T // tr

    def body(dy_ref, h_ref, g_ref, dres_ref, dh_ref, dg_ref, acc):
        i = pl.program_id(0)
        v = h_ref[...]
        r = lax.rsqrt(jnp.mean(v * v, axis=-1, keepdims=True) + EPS)
        nrm = v * r
        dy = dy_ref[...].astype(F32)
        dn = dy * g_ref[...]
        dh = dres_ref[...].astype(F32) + r * (dn - nrm * jnp.mean(dn * nrm, axis=-1, keepdims=True))
        dh_ref[...] = dh.astype(BF16)

        @pl.when(i == 0)
        def _():
            acc[...] = jnp.zeros_like(acc)

        acc[...] += _fold8(dy * nrm)

        @pl.when(i == nt - 1)
        def _():
            dg_ref[...] = jnp.sum(acc[...], axis=0, keepdims=True)

    tile = pl.BlockSpec((tr, D), lambda i: (i, 0))
    vec = pl.BlockSpec((1, D), lambda i: (0, 0))
    return pl.pallas_call(
        body, name=name, grid=(nt,),
        in_specs=[tile, tile, vec, tile], out_specs=[tile, vec],
        out_shape=[jax.ShapeDtypeStruct((T, D), BF16), jax.ShapeDtypeStruct((1, D), F32)],
        scratch_shapes=[pltpu.VMEM((8, D), F32)],
        compiler_params=pltpu.CompilerParams(dimension_semantics=("arbitrary",)),
    )(dn_out, h, g, dres)


def _band_masks():
    qi = lax.broadcasted_iota(jnp.int32, (BLK, BLK), 0)
    kj = lax.broadcasted_iota(jnp.int32, (BLK, BLK), 1)
    return kj >= qi, kj <= qi


AQ = 4
ATTN_PARAMS = pltpu.CompilerParams(
    dimension_semantics=("parallel", "parallel"), vmem_limit_bytes=VMEM_BIG)


def _cm_spec(d, col, nblk, rowmap=lambda n: n):
    if d == 1:
        return pl.BlockSpec((nblk * BLK, DA), lambda r, n: (rowmap(n), col))
    return pl.BlockSpec((None, nblk * BLK, DA), lambda r, n: (r, rowmap(n), col))


def _cm_shape(d, T, dtype):
    return jax.ShapeDtypeStruct((T, DA) if d == 1 else (d, T // d, DA), dtype)


def _blk(b):
    return slice(b * BLK, (b + 1) * BLK)


HEADS = tuple(slice(h * DH, (h + 1) * DH) for h in range(NH))


def _attn_fwd(name, zsrc, d, T):
    nb = T // d // BLK
    aq = min(AQ, nb)
    scale = DH ** -0.5

    def body(q_ref, kp_ref, kc_ref, vp_ref, vc_ref, o_ref, l_ref):
        n = pl.program_id(1)
        band_prev, cur_ok = _band_masks()
        for b in range(aq):
            kp = (lambda sl: kp_ref[:, sl]) if b == 0 else (lambda sl, b=b: kc_ref[_blk(b - 1), sl])
            vp = (lambda sl: vp_ref[:, sl]) if b == 0 else (lambda sl, b=b: vc_ref[_blk(b - 1), sl])
            prev_ok = band_prev & (n > 0) if b == 0 else band_prev
            rows = _blk(b)
            s = [(jnp.where(prev_ok, _dot(q_ref[rows, sl], kp(sl), NT) * scale, NEG),
                  jnp.where(cur_ok, _dot(q_ref[rows, sl], kc_ref[rows, sl], NT) * scale, NEG))
                 for sl in HEADS]
            m = [jnp.maximum(jnp.max(sp, axis=1, keepdims=True), jnp.max(sc, axis=1, keepdims=True))
                 for sp, sc in s]
            p = [(jnp.exp(sp - mh), jnp.exp(sc - mh)) for (sp, sc), mh in zip(s, m)]
            den = [jnp.sum(pp, axis=1, keepdims=True) + jnp.sum(pc, axis=1, keepdims=True)
                   for pp, pc in p]
            o = [_dot(pp.astype(BF16), vp(sl), NN) + _dot(pc.astype(BF16), vc_ref[rows, sl], NN)
                 for (pp, pc), sl in zip(p, HEADS)]
            o_ref[rows, :] = jnp.concatenate(
                [(oh / dh).astype(BF16) for oh, dh in zip(o, den)], axis=1)
            l_ref[rows, :] = jnp.concatenate(
                [jnp.broadcast_to(mh + jnp.log(dh), (BLK, DH)) for mh, dh in zip(m, den)], axis=1)

    halo = lambda n: jnp.maximum(aq * n - 1, 0)
    return pl.pallas_call(
        body, name=name, grid=(d, nb // aq),
        in_specs=[_cm_spec(d, 0, aq), _cm_spec(d, 1, 1, halo), _cm_spec(d, 1, aq),
                  _cm_spec(d, 2, 1, halo), _cm_spec(d, 2, aq)],
        out_specs=[_cm_spec(d, 0, aq)] * 2,
        out_shape=[_cm_shape(d, T, BF16), _cm_shape(d, T, F32)],
        compiler_params=ATTN_PARAMS,
    )(zsrc, zsrc, zsrc, zsrc, zsrc)


def _cm_tile(d, tr):
    if d == 1:
        return pl.BlockSpec((tr, DA), lambda i: (i, 0))
    return pl.BlockSpec((d, tr // d, DA), lambda i: (0, i, 0))


def _attn_combine(outs, lses, T):
    tr = _rows(T)

    def body(*refs):
        o_in, l_in = refs[:3], refs[3:6]
        o_ref, l_ref = refs[6:8]
        o_cm, l_cm = refs[8:8 + len(WIDE)], refs[8 + len(WIDE):8 + 2 * len(WIDE)]
        so, sl, so_all, sl_all = refs[8 + 2 * len(WIDE):]
        for c in range(DA // LANES):
            lt = _lane_tile(c)
            os_, ls_ = [o_in[0][:, lt].astype(F32)], [l_in[0][:, lt]]
            for w, d in enumerate(WIDE):
                for r in range(d):
                    so[w, c, pl.ds(r, tr // d, stride=d), :] = o_in[1 + w][r, :, lt].astype(F32)
                    sl[w, c, pl.ds(r, tr // d, stride=d), :] = l_in[1 + w][r, :, lt]
                os_.append(so[w, c])
                ls_.append(sl[w, c])
            la, lb, lc = ls_
            m = jnp.maximum(jnp.maximum(la, lb), lc)
            ea, eb, ec = jnp.exp(la - m), jnp.exp(lb - m), jnp.exp(lc - m)
            s = ea + eb + ec
            o = (ea * os_[0] + eb * os_[1] + ec * os_[2]) / s
            lse = m + jnp.log(s)
            o_ref[:, lt] = o.astype(BF16)
            l_ref[:, lt] = lse
            so_all[c] = o
            sl_all[c] = lse
        _emit_class_major(so_all, o_cm, tr)
        _emit_class_major(sl_all, l_cm, tr)

    specs = [_cm_tile(d, tr) for d in DILATIONS]
    wide = [_cm_tile(d, tr) for d in WIDE]
    return pl.pallas_call(
        body, name="attn_combine", grid=(T // tr,),
        in_specs=specs + specs,
        out_specs=[specs[0], specs[0]] + wide + wide,
        out_shape=[_cm_shape(1, T, BF16), _cm_shape(1, T, F32)]
        + [_cm_shape(d, T, BF16) for d in WIDE] + [_cm_shape(d, T, F32) for d in WIDE],
        scratch_shapes=[pltpu.VMEM((len(WIDE), DA // LANES, tr, LANES), F32)] * 2
        + [pltpu.VMEM((DA // LANES, tr, LANES), F32)] * 2,
        compiler_params=pltpu.CompilerParams(
            dimension_semantics=("parallel",), vmem_limit_bytes=VMEM_MID),
    )(*outs, *lses)


def _attn_bwd_q(name, zsrc, dosrc, osrc, lsrc, d, T):
    nb = T // d // BLK
    aq = min(AQ, nb)
    scale = DH ** -0.5

    def body(q_ref, kp_ref, kc_ref, vp_ref, vc_ref, do_ref, o_ref, l_ref, dq_ref):
        n = pl.program_id(1)
        band_prev, cur_ok = _band_masks()
        for b in range(aq):
            kp = (lambda sl: kp_ref[:, sl]) if b == 0 else (lambda sl, b=b: kc_ref[_blk(b - 1), sl])
            vp = (lambda sl: vp_ref[:, sl]) if b == 0 else (lambda sl, b=b: vc_ref[_blk(b - 1), sl])
            prev_ok = band_prev & (n > 0) if b == 0 else band_prev
            rows = _blk(b)
            s = [(_dot(q_ref[rows, sl], kp(sl), NT), _dot(q_ref[rows, sl], kc_ref[rows, sl], NT))
                 for sl in HEADS]
            dp = [(_dot(do_ref[rows, sl], vp(sl), NT), _dot(do_ref[rows, sl], vc_ref[rows, sl], NT))
                  for sl in HEADS]
            delta = [jnp.sum(do_ref[rows, sl].astype(F32) * o_ref[rows, sl].astype(F32), axis=1,
                             keepdims=True) for sl in HEADS]
            p = [(jnp.exp(jnp.where(prev_ok, sp * scale - l_ref[rows, sl], NEG)),
                  jnp.exp(jnp.where(cur_ok, sc * scale - l_ref[rows, sl], NEG)))
                 for (sp, sc), sl in zip(s, HEADS)]
            ds = [((pp * (dpp - dl) * scale).astype(BF16), (pc * (dpc - dl) * scale).astype(BF16))
                  for (pp, pc), (dpp, dpc), dl in zip(p, dp, delta)]
            dq = [_dot(dsp, kp(sl), NN) + _dot(dsc, kc_ref[rows, sl], NN)
                  for (dsp, dsc), sl in zip(ds, HEADS)]
            dq_ref[rows, :] = jnp.concatenate([v.astype(BF16) for v in dq], axis=1)

    halo = lambda n: jnp.maximum(aq * n - 1, 0)
    own = _cm_spec(d, 0, aq)
    return pl.pallas_call(
        body, name=name, grid=(d, nb // aq),
        in_specs=[own, _cm_spec(d, 1, 1, halo), _cm_spec(d, 1, aq), _cm_spec(d, 2, 1, halo),
                  _cm_spec(d, 2, aq), own, own, own],
        out_specs=own, out_shape=_cm_shape(d, T, BF16),
        compiler_params=ATTN_PARAMS,
    )(zsrc, zsrc, zsrc, zsrc, zsrc, dosrc, osrc, lsrc)


def _attn_bwd_kv(name, zsrc, dosrc, osrc, lsrc, d, T):
    nb = T // d // BLK
    aq = min(AQ, nb)
    nsteps = nb // aq
    scale = DH ** -0.5

    def body(k_ref, v_ref, q_ref, qn_ref, do_ref, don_ref, o_ref, on_ref, l_ref, ln_ref,
             dk_ref, dv_ref):
        j = pl.program_id(1)
        band_next, own_ok = _band_masks()
        for b in range(aq):
            rows = _blk(b)
            last = b == aq - 1
            pick = lambda cur, halo: ((lambda sl: halo[:, sl]) if last
                                      else (lambda sl, b=b: cur[_blk(b + 1), sl]))
            qb, dob, ob, lb = (pick(q_ref, qn_ref), pick(do_ref, don_ref), pick(o_ref, on_ref),
                               pick(l_ref, ln_ref))
            next_ok = band_next & (j < nsteps - 1) if last else band_next
            s = [(_dot(q_ref[rows, sl], k_ref[rows, sl], NT), _dot(qb(sl), k_ref[rows, sl], NT))
                 for sl in HEADS]
            dp = [(_dot(do_ref[rows, sl], v_ref[rows, sl], NT), _dot(dob(sl), v_ref[rows, sl], NT))
                  for sl in HEADS]
            delta = [(jnp.sum(do_ref[rows, sl].astype(F32) * o_ref[rows, sl].astype(F32), axis=1,
                              keepdims=True),
                      jnp.sum(dob(sl).astype(F32) * ob(sl).astype(F32), axis=1, keepdims=True))
                     for sl in HEADS]
            p = [(jnp.exp(jnp.where(own_ok, sa * scale - l_ref[rows, sl], NEG)),
                  jnp.exp(jnp.where(next_ok, sb * scale - lb(sl), NEG)))
                 for (sa, sb), sl in zip(s, HEADS)]
            dv = [_dot(pa.astype(BF16), do_ref[rows, sl], TN) + _dot(pb.astype(BF16), dob(sl), TN)
                  for (pa, pb), sl in zip(p, HEADS)]
            ds = [((pa * (dpa - da) * scale).astype(BF16), (pb * (dpb - db) * scale).astype(BF16))
                  for (pa, pb), (dpa, dpb), (da, db) in zip(p, dp, delta)]
            dk = [_dot(dsa, q_ref[rows, sl], TN) + _dot(dsb, qb(sl), TN)
                  for (dsa, dsb), sl in zip(ds, HEADS)]
            dk_ref[rows, :] = jnp.concatenate([v.astype(BF16) for v in dk], axis=1)
            dv_ref[rows, :] = jnp.concatenate([v.astype(BF16) for v in dv], axis=1)

    halo = lambda j: jnp.minimum(aq * (j + 1), nb - 1)
    own, own_n = _cm_spec(d, 0, aq), _cm_spec(d, 0, 1, halo)
    sh = _cm_shape(d, T, BF16)
    return pl.pallas_call(
        body, name=name, grid=(d, nsteps),
        in_specs=[_cm_spec(d, 1, aq), _cm_spec(d, 2, aq), own, own_n, own, own_n, own, own_n,
                  own, own_n],
        out_specs=[own, own], out_shape=[sh, sh],
        compiler_params=ATTN_PARAMS,
    )(zsrc, zsrc, zsrc, zsrc, dosrc, dosrc, osrc, osrc, lsrc, lsrc)


def _dz_assemble(dqs, dks, dvs, dcvg, T):
    tr = _rows(T)
    nb = len(DILATIONS)

    def body(*refs):
        cvg_ref, dz_ref, scr = refs[3 * nb], refs[3 * nb + 1], refs[3 * nb + 2]
        for g in range(3):
            parts = refs[g * nb:(g + 1) * nb]
            for c in range(DA // LANES):
                lt = _lane_tile(c)
                scr[g, c] = parts[0][:, lt].astype(F32)
                for w, d in enumerate(WIDE):
                    for r in range(d):
                        rows = pl.ds(r, tr // d, stride=d)
                        scr[g, c, rows, :] = scr[g, c, rows, :] + parts[1 + w][r, :, lt].astype(F32)
                dz_ref[:, g * DA + c * LANES:g * DA + (c + 1) * LANES] = scr[g, c].astype(BF16)
        dz_ref[:, 3 * DA:] = cvg_ref[...]

    specs = [_cm_tile(d, tr) for d in DILATIONS]
    return pl.pallas_call(
        body, name="dz_assemble", grid=(T // tr,),
        in_specs=specs * 3 + [pl.BlockSpec((tr, 2 * DC), lambda i: (i, 0))],
        out_specs=pl.BlockSpec((tr, DIN), lambda i: (i, 0)),
        out_shape=jax.ShapeDtypeStruct((T, DIN), BF16),
        scratch_shapes=[pltpu.VMEM((3, DA // LANES, tr, LANES), F32)],
        compiler_params=pltpu.CompilerParams(
            dimension_semantics=("parallel",), vmem_limit_bytes=VMEM_MID),
    )(*dqs, *dks, *dvs, dcvg)


CT = 256
HALO = 32
RC = 32


def _conv_fwd(z, w_dw, b_dw, g_ln, b_ln):
    T = z.shape[0]
    ct = min(CT, T)
    nt = T // ct
    hb = ct // HALO

    def body(cv_ref, cg_ref, cvp_ref, cgp_ref, w_ref, bdw_ref, g_ref, b_ref, oc_ref, y_ref, ubuf, ush):
        i = pl.program_id(0)
        up = cvp_ref[...].astype(F32) * _sigmoid(cgp_ref[...].astype(F32))
        ubuf[0:HALO, :] = jnp.where(i > 0, up, 0.0)
        ubuf[HALO:, :] = cv_ref[...].astype(F32) * _sigmoid(cg_ref[...].astype(F32))
        for b in range(8):
            ush[b] = ubuf[pl.ds(8 - b, ct + 24), :]

        def chunk(ci, carry):
            r0 = pl.multiple_of(ci * RC, RC)
            acc = jnp.broadcast_to(bdw_ref[...], (RC, DC))
            for s in range(CW):
                a, b = divmod(s, 8)
                acc = acc + w_ref[CW - 1 - s:CW - s, :] * ush[b, pl.ds(r0 + 24 - 8 * a, RC), :]
            y_ref[pl.ds(r0, RC), :] = acc
            mu = jnp.mean(acc, axis=-1, keepdims=True)
            cen = acc - mu
            var = jnp.mean(cen * cen, axis=-1, keepdims=True)
            ln = cen * lax.rsqrt(var + EPS) * g_ref[...] + b_ref[...]
            oc_ref[pl.ds(r0, RC), :] = (ln * _sigmoid(ln)).astype(BF16)
            return carry

        lax.fori_loop(0, ct // RC, chunk, 0)

    cur = lambda col: pl.BlockSpec((ct, DC), lambda i: (i, col))
    prv = lambda col: pl.BlockSpec((HALO, DC), lambda i: (jnp.maximum(i * hb - 1, 0), col))
    vec = pl.BlockSpec((1, DC), lambda i: (0, 0))
    return pl.pallas_call(
        body, name="conv_fwd", grid=(nt,),
        in_specs=[cur(3), cur(4), prv(3), prv(4), pl.BlockSpec((CW, DC), lambda i: (0, 0)),
                  vec, vec, vec],
        out_specs=[pl.BlockSpec((ct, DC), lambda i: (i, 0))] * 2,
        out_shape=[jax.ShapeDtypeStruct((T, DC), BF16), jax.ShapeDtypeStruct((T, DC), F32)],
        scratch_shapes=[pltpu.VMEM((ct + HALO, DC), F32), pltpu.VMEM((8, ct + 24, DC), F32)],
        compiler_params=pltpu.CompilerParams(
            dimension_semantics=("parallel",), vmem_limit_bytes=VMEM_MID),
    )(z, z, z, z, w_dw, b_dw, g_ln, b_ln)


def _conv_bwd(z, dom, y, w_dw, g_ln, b_ln):
    T = z.shape[0]
    ct = min(CT, T)
    nt = T // ct
    hb = ct // HALO
    last_halo = T // HALO - 1

    def ln_bwd(yv, dov, g_ref, b_ref):
        mu = jnp.mean(yv, axis=-1, keepdims=True)
        cen = yv - mu
        rstd = lax.rsqrt(jnp.mean(cen * cen, axis=-1, keepdims=True) + EPS)
        xhat = cen * rstd
        ln = xhat * g_ref[...] + b_ref[...]
        sg = _sigmoid(ln)
        dln = dov * (sg * (1.0 + ln * (1.0 - sg)))
        dxh = dln * g_ref[...]
        dy = rstd * (dxh - jnp.mean(dxh, axis=-1, keepdims=True)
                     - xhat * jnp.mean(dxh * xhat, axis=-1, keepdims=True))
        return dy, dln, xhat

    def body(do_ref, don_ref, y_ref, yn_ref, cv_ref, cg_ref, cvp_ref, cgp_ref, w_ref, g_ref, b_ref,
             dcvg_ref, dw_ref, dbdw_ref, dg_ref, db_ref,
             dybuf, dysh, ubuf, ush, dwacc, vacc):
        i = pl.program_id(0)

        @pl.when(i == 0)
        def _():
            dwacc[...] = jnp.zeros_like(dwacc)
            vacc[...] = jnp.zeros_like(vacc)

        def ln_chunk(ci, carry):
            r0 = pl.multiple_of(ci * RC, RC)
            dy, dln, xhat = ln_bwd(y_ref[pl.ds(r0, RC), :], do_ref[pl.ds(r0, RC), :].astype(F32),
                                   g_ref, b_ref)
            dybuf[pl.ds(r0, RC), :] = dy
            vacc[0] += _fold8(dy)
            vacc[1] += _fold8(dln * xhat)
            vacc[2] += _fold8(dln)
            return carry

        lax.fori_loop(0, ct // RC, ln_chunk, 0)
        dyn, _, _ = ln_bwd(yn_ref[...], don_ref[...].astype(F32), g_ref, b_ref)
        dybuf[ct:, :] = jnp.where(i < nt - 1, dyn, 0.0)
        for b in range(8):
            dysh[b] = dybuf[pl.ds(b, ct + 24), :]

        up = cvp_ref[...].astype(F32) * _sigmoid(cgp_ref[...].astype(F32))
        ubuf[0:HALO, :] = jnp.where(i > 0, up, 0.0)
        ubuf[HALO:, :] = cv_ref[...].astype(F32) * _sigmoid(cg_ref[...].astype(F32))
        for b in range(8):
            ush[b] = ubuf[pl.ds(8 - b, ct + 24), :]

        def chunk(ci, carry):
            r0 = pl.multiple_of(ci * RC, RC)
            dy = dybuf[pl.ds(r0, RC), :]
            du = jnp.zeros((RC, DC), F32)
            for s in range(CW):
                a, b = divmod(s, 8)
                du = du + w_ref[CW - 1 - s:CW - s, :] * dysh[b, pl.ds(r0 + 8 * a, RC), :]
                dwacc[CW - 1 - s] += _fold8(dy * ush[b, pl.ds(r0 + 24 - 8 * a, RC), :])
            cv = cv_ref[pl.ds(r0, RC), :].astype(F32)
            sg = _sigmoid(cg_ref[pl.ds(r0, RC), :].astype(F32))
            dcvg_ref[pl.ds(r0, RC), 0:DC] = (du * sg).astype(BF16)
            dcvg_ref[pl.ds(r0, RC), DC:2 * DC] = (du * cv * sg * (1.0 - sg)).astype(BF16)
            return carry

        lax.fori_loop(0, ct // RC, chunk, 0)

        @pl.when(i == nt - 1)
        def _():
            dw_ref[...] = jnp.sum(dwacc[...], axis=1)
            dbdw_ref[...] = jnp.sum(vacc[0], axis=0, keepdims=True)
            dg_ref[...] = jnp.sum(vacc[1], axis=0, keepdims=True)
            db_ref[...] = jnp.sum(vacc[2], axis=0, keepdims=True)

    cur = lambda col: pl.BlockSpec((ct, DC), lambda i: (i, col))
    prv = lambda col: pl.BlockSpec((HALO, DC), lambda i: (jnp.maximum(i * hb - 1, 0), col))
    nxt = lambda col: pl.BlockSpec((HALO, DC), lambda i: (jnp.minimum((i + 1) * hb, last_halo), col))
    vec = pl.BlockSpec((1, DC), lambda i: (0, 0))
    tile = pl.BlockSpec((ct, DC), lambda i: (i, 0))
    return pl.pallas_call(
        body, name="conv_bwd", grid=(nt,),
        in_specs=[cur(1), nxt(1), cur(0), nxt(0), cur(3), cur(4), prv(3), prv(4),
                  pl.BlockSpec((CW, DC), lambda i: (0, 0)), vec, vec],
        out_specs=[pl.BlockSpec((ct, 2 * DC), lambda i: (i, 0)),
                   pl.BlockSpec((CW, DC), lambda i: (0, 0)), vec, vec, vec],
        out_shape=[jax.ShapeDtypeStruct((T, 2 * DC), BF16),
                   jax.ShapeDtypeStruct((CW, DC), F32), jax.ShapeDtypeStruct((1, DC), F32),
                   jax.ShapeDtypeStruct((1, DC), F32), jax.ShapeDtypeStruct((1, DC), F32)],
        scratch_shapes=[pltpu.VMEM((ct + HALO, DC), F32), pltpu.VMEM((8, ct + 24, DC), F32),
                        pltpu.VMEM((ct + HALO, DC), F32), pltpu.VMEM((8, ct + 24, DC), F32),
                        pltpu.VMEM((CW, 8, DC), F32), pltpu.VMEM((3, 8, DC), F32)],
        compiler_params=pltpu.CompilerParams(
            dimension_semantics=("arbitrary",), vmem_limit_bytes=VMEM_BIG),
    )(dom, dom, y, y, z, z, z, z, w_dw, g_ln, b_ln)


def _adam_math(w, g, m, v):
    m = ADAM_B1 * m + (1.0 - ADAM_B1) * g
    v = ADAM_B2 * v + (1.0 - ADAM_B2) * (g * g)
    m_hat = m / (1.0 - ADAM_B1 ** ADAM_STEP)
    v_hat = v / (1.0 - ADAM_B2 ** ADAM_STEP)
    delta = -ADAM_LR * (m_hat / (jnp.sqrt(v_hat) + ADAM_EPS) + ADAM_WD * w)
    return delta, m, v


def _adam(name, slots, w, m, v):
    rows, cols = w.shape
    tr = next(t for t in (256, 176, 128, 64, 32, 16, 8, rows) if rows % t == 0)

    def body(s_ref, w_ref, m_ref, v_ref, g_out, d_out, m_out, v_out):
        g = s_ref[0].astype(F32)
        for s in range(1, NDEV):
            g = g + s_ref[s].astype(F32)
        delta, mn, vn = _adam_math(w_ref[...], g, m_ref[...], v_ref[...])
        g_out[...] = g
        d_out[...] = delta
        m_out[...] = mn
        v_out[...] = vn

    tile = pl.BlockSpec((tr, cols), lambda i: (i, 0))
    sh = jax.ShapeDtypeStruct((rows, cols), F32)
    return pl.pallas_call(
        body, name=name, grid=(rows // tr,),
        in_specs=[pl.BlockSpec((NDEV, tr, cols), lambda i: (0, i, 0)), tile, tile, tile],
        out_specs=[tile] * 4, out_shape=[sh] * 4,
        compiler_params=pltpu.CompilerParams(
            dimension_semantics=("parallel",), vmem_limit_bytes=VMEM_MID),
    )(slots, w, m, v)


SMALL_NAMES = ("g_mix", "b_dw", "g_conv_ln", "b_conv_ln", "g_ffn", "g_ple", "b_pgate", "g_final")


def _pack_small(vecs, w_dw_full):
    rows = [jnp.pad(v.reshape(1, -1), ((0, 0), (0, SMALL_W - v.size))) for v in vecs]
    rows.append(jnp.pad(w_dw_full, ((0, 0), (0, SMALL_W - DC))))
    rows.append(jnp.zeros((SMALL_ROWS - len(vecs) - CW, SMALL_W), F32))
    return jnp.concatenate(rows, axis=0)


def kernel(x, p, g_mix, w_in, w_dw, b_dw, g_conv_ln, b_conv_ln, w_out, g_ffn, w_gate, w_up, w_down, g_ple, w_pgate, b_pgate, w_ple, g_final, loss_target, m_g_mix, m_w_in, m_w_dw, m_b_dw, m_g_conv_ln, m_b_conv_ln, m_w_out, m_g_ffn, m_w_gate, m_w_up, m_w_down, m_g_ple, m_w_pgate, m_b_pgate, m_w_ple, m_g_final, v_g_mix, v_w_in, v_w_dw, v_b_dw, v_g_conv_ln, v_b_conv_ln, v_w_out, v_g_ffn, v_w_gate, v_w_up, v_w_down, v_g_ple, v_w_pgate, v_b_pgate, v_w_ple, v_g_final):
    T = x.shape[1]
    me = 4 * lax.axis_index("x") + 2 * lax.axis_index("y") + lax.axis_index("c")
    xs = x.reshape(T, D)
    ps = p.reshape(T, DPLE).astype(BF16)
    tgt = loss_target.reshape(T, D)
    g_final2 = g_final.reshape(1, D)

    tr_names = ("w_gate", "w_up")
    big = dict(w_in=w_in[0], w_out=w_out[0], w_gate=w_gate[0].T, w_up=w_up[0].T, w_down=w_down[0],
               w_pgate=w_pgate[0], w_ple=w_ple[0])
    order = ("w_in", "w_out", "w_gate", "w_up", "w_down", "w_pgate", "w_ple")
    w_dw_g, w_in_s = _gather_two_level(
        "gather_first", [w_dw.reshape(CW, DC // NDEV), big["w_in"].astype(BF16)])
    w_dw_f = w_dw_g.transpose(1, 0, 2).reshape(CW, DC)
    later = order[1:]
    lands = _place("gather_place", [(big[n], False) for n in later], dtype=BF16)
    g_handles, g_token = _xstart("gather_start", [(None, False)] * len(later), lands, deps=[w_in_s])
    w_in_f = _shards_to_cols("w_in_natural", w_in_s)
    G = dict(zip(later, g_handles))

    a, z, *z_wide = _mm_in(xs, g_mix, w_in_f, deps=[g_token])
    zsrc = dict(zip(DILATIONS, [z] + z_wide))
    br = [_attn_fwd(f"attn_fwd_d{d}", zsrc[d], d, T) for d in DILATIONS]
    comb = list(_attn_combine([b[0] for b in br], [b[1] for b in br], T))
    o_attn, lse = comb[0], comb[1]
    osrc = dict(zip(DILATIONS, [o_attn] + comb[2:2 + len(WIDE)]))
    lsrc = dict(zip(DILATIONS, [lse] + comb[2 + len(WIDE):]))
    o_conv, y_conv = _conv_fwd(z, w_dw_f, b_dw, g_conv_ln, b_conv_ln)
    w_out_f = _xwait("gather_wait_w_out", G["w_out"], o_conv).reshape(D, D)
    h1, f = _mm_out(o_attn, o_conv, w_out_f, xs, g_ffn)
    w_gate_f = _xwait("gather_wait_w_gate", G["w_gate"], f).reshape(DFF, D)
    w_up_f = _xwait("gather_wait_w_up", G["w_up"], f).reshape(DFF, D)
    gate, up, act = _mm_gate_up(f, w_gate_f, w_up_f)
    w_down_f = _xwait("gather_wait_w_down", G["w_down"], act).reshape(DFF, D)
    h2, r = _mm_down(act, w_down_f, h1, g_ple)
    w_pgate_f = _xwait("gather_wait_w_pgate", G["w_pgate"], r).reshape(D, D)
    w_ple_f = _xwait("gather_wait_w_ple", G["w_ple"], r).transpose(1, 0, 2).reshape(DPLE, D)

    loss_part, dh3, dpe, dpg, d_g_final, d_b_pgate = _ple_loss(
        r, w_pgate_f, b_pgate, ps, w_ple_f, h2, tgt, g_final2)
    H = {}

    def send_grads(tag, named):
        items = [(v, True) for _, v in named]
        handles, token = _xstart(f"grads_start_{tag}", items, _place(f"grads_place_{tag}", items))
        H.update(zip([n for n, _ in named], handles))
        return token

    gw_pgate = _mm_tn("gw_pgate", r, dpg).reshape(NDEV, D // NDEV, D)
    gw_ple = _mm_tn("gw_ple", ps, dpe).reshape(DPLE, NDEV, D // NDEV).transpose(1, 0, 2)
    tok = send_grads("ple", [("w_pgate", gw_pgate), ("w_ple", gw_ple)])
    dh2b, d_g_ple = _mm_pgate_bwd(dpg, w_pgate_f, h2, g_ple, dh3, deps=[tok])
    ff_shards = lambda g: g.reshape(NDEV, N_FF, D)
    gw_down = ff_shards(_mm_tn_ff("gw_down", act, dh2b))
    tok = send_grads("down", [("w_down", gw_down)])
    dgate, dup = _mm_down_bwd(dh2b, w_down_f, gate, up, deps=[tok])
    gw_gate = ff_shards(_mm_tn_ff("gw_gate", dgate, f))
    gw_up = ff_shards(_mm_tn_ff("gw_up", dup, f))
    tok = send_grads("ffn", [("w_gate", gw_gate), ("w_up", gw_up)])
    df = _mm_ffn_in_bwd(dgate, w_gate_f, dup, w_up_f, deps=[tok])
    dh1b, d_g_ffn = _rms_bwd("rms_ffn_bwd", df, h1, g_ffn, dh2b)
    gw_out = jnp.concatenate(
        [_mm_tn("gw_out_attn", o_attn, dh1b), _mm_tn("gw_out_conv", o_conv, dh1b)], axis=0)
    tok = send_grads("out", [("w_out", gw_out.reshape(NDEV, D // NDEV, D))])
    dom, *do_wide = _mm_out_bwd(dh1b, w_out_f, deps=[tok])
    dosrc = dict(zip(DILATIONS, [dom] + do_wide))
    dcvg, d_w_dw, d_b_dw, d_g_ln, d_b_ln = _conv_bwd(z, dom, y_conv, w_dw_f, g_conv_ln, b_conv_ln)
    dqs, dks, dvs = [], [], []
    for d in DILATIONS:
        dqs.append(_attn_bwd_q(f"attn_bwd_q_d{d}", zsrc[d], dosrc[d], osrc[d], lsrc[d], d, T))
        dk, dv = _attn_bwd_kv(f"attn_bwd_kv_d{d}", zsrc[d], dosrc[d], osrc[d], lsrc[d], d, T)
        dks.append(dk)
        dvs.append(dv)
    dz = _dz_assemble(dqs, dks, dvs, dcvg, T)
    gw_in = _cols_to_shards("gw_in_shards", _mm_tn_wide("gw_in", a, dz), N_IN)
    tok = send_grads("in", [("w_in", gw_in)])
    grad_x, d_g_mix = _mm_in_bwd(dz, w_in_f, xs, g_mix, dh1b, deps=[tok])

    small_part = _pack_small(
        [d_g_mix, d_b_dw, d_g_ln, d_b_ln, d_g_ffn, d_g_ple, d_b_pgate, d_g_final], d_w_dw)
    small_slots = _exchange("exchange_small_grads", [(small_part, False)])[0]
    S = {n: _xwait(f"grads_wait_{n}", H[n], small_slots)
         for n in ("w_pgate", "w_ple", "w_down", "w_gate", "w_up", "w_out", "w_in")}

    mom = dict(w_in=(m_w_in, v_w_in), w_out=(m_w_out, v_w_out), w_gate=(m_w_gate, v_w_gate),
               w_up=(m_w_up, v_w_up), w_down=(m_w_down, v_w_down), w_pgate=(m_w_pgate, v_w_pgate),
               w_ple=(m_w_ple, v_w_ple))
    upd = {}
    for n in order:
        m_n, v_n = mom[n][0][0], mom[n][1][0]
        if n in tr_names:
            res = _adam(f"adam_{n}", S[n], big[n], m_n.T, v_n.T)
            upd[n] = [t.T[None] for t in res]
        else:
            res = _adam(f"adam_{n}", S[n], big[n], m_n, v_n)
            upd[n] = [t[None] for t in res]

    def lanes(v):
        full = jnp.zeros((CW, NDEV, DC // NDEV), F32)
        full = lax.dynamic_update_slice(full, v.reshape(CW, 1, DC // NDEV), (0, me, 0))
        return full.reshape(CW, DC)

    small_w = _pack_small([g_mix, b_dw, g_conv_ln, b_conv_ln, g_ffn, g_ple, b_pgate, g_final2], lanes(w_dw))
    small_m = _pack_small([m_g_mix, m_b_dw, m_g_conv_ln, m_b_conv_ln, m_g_ffn, m_g_ple, m_b_pgate,
                           m_g_final.reshape(1, D)], lanes(m_w_dw))
    small_v = _pack_small([v_g_mix, v_b_dw, v_g_conv_ln, v_b_conv_ln, v_g_ffn, v_g_ple, v_b_pgate,
                           v_g_final.reshape(1, D)], lanes(v_w_dw))
    small_res = _adam("adam_small", small_slots, small_w, small_m, small_v)

    def unpack(t):
        out = {}
        widths = dict(g_mix=D, b_dw=DC, g_conv_ln=DC, b_conv_ln=DC, g_ffn=D, g_ple=D, b_pgate=D, g_final=D)
        for i, n in enumerate(SMALL_NAMES):
            out[n] = t[i:i + 1, :widths[n]]
        out["g_final"] = out["g_final"].reshape(D)
        taps = t[len(SMALL_NAMES):len(SMALL_NAMES) + CW, :DC].reshape(CW, NDEV, DC // NDEV)
        out["w_dw"] = lax.dynamic_slice(taps, (0, me, 0), (CW, 1, DC // NDEV))[None]
        return out

    small = [unpack(t) for t in small_res]

    loss = lax.psum(loss_part[0, 0], ("x", "y", "c"))
    names = ("g_mix", "w_in", "w_dw", "b_dw", "g_conv_ln", "b_conv_ln", "w_out", "g_ffn", "w_gate",
             "w_up", "w_down", "g_ple", "w_pgate", "b_pgate", "w_ple", "g_final")
    outs = [loss, grad_x.reshape(1, T, D)]
    for kind in range(4):
        for n in names:
            outs.append(upd[n][kind] if n in upd else small[kind][n])
    return tuple(outs)
```

```python
import jax
import jax.numpy as jnp
from jax import lax
from jax.experimental import pallas as pl
from jax.experimental.pallas import tpu as pltpu

F32 = jnp.float32
BF16 = jnp.bfloat16

NDEV = 8
D = 2048
NH = 8
DH = 128
DA = NH * DH
DC = D - DA
DIN = 3 * DA + 2 * DC
DFF = 5632
DPLE = 256
BLK = 128
DILATIONS = (1, 4, 16)
CW = 31
EPS = 1e-6
N_IN = DIN // NDEV
N_FF = DFF // NDEV
NEG = -1e30

ADAM_LR = 0.001
ADAM_B1 = 0.9
ADAM_B2 = 0.999
ADAM_EPS = 1e-08
ADAM_WD = 0.01
ADAM_STEP = 10

VMEM_CAP_V7X = 64 * 1024 * 1024
VMEM_BIG = VMEM_CAP_V7X - 12 * 1024 * 1024
VMEM_MID = 40 * 1024 * 1024

SMALL_W = 2048
SMALL_ROWS = 40


def _sigmoid(v):
    return 1.0 / (1.0 + jnp.exp(-v))


def _dot(a, b, contract):
    return lax.dot_general(a, b, (contract, ((), ())), preferred_element_type=F32)


NN = ((1,), (0,))
NT = ((1,), (1,))
TN = ((0,), (0,))


def _exchange(name, items):
    n = len(items)
    out_shape = [
        jax.ShapeDtypeStruct((NDEV,) + (a.shape[1:] if sc else a.shape), a.dtype)
        for a, sc in items
    ]
    scat = [sc for _, sc in items]

    def body(*refs):
        srcs = refs[:n]
        dsts = refs[n:2 * n]
        send_sems, recv_sems, loc_sems = refs[2 * n:]
        x = lax.axis_index("x")
        y = lax.axis_index("y")
        c = lax.axis_index("c")
        me = 4 * x + 2 * y + c

        local = []
        for i in range(n):
            src = srcs[i].at[me] if scat[i] else srcs[i]
            cp = pltpu.make_async_copy(src, dsts[i].at[me], loc_sems.at[i])
            cp.start()
            local.append(cp)

        remote = []
        for k in range(1, NDEV):
            px = (1 - x) if (k >> 2) & 1 else x
            py = (1 - y) if (k >> 1) & 1 else y
            pc = (1 - c) if k & 1 else c
            peer = 4 * px + 2 * py + pc
            for i in range(n):
                sem = i * (NDEV - 1) + k - 1
                src = srcs[i].at[peer] if scat[i] else srcs[i]
                send = pltpu.make_async_remote_copy(
                    src_ref=src, dst_ref=dsts[i].at[me],
                    send_sem=send_sems.at[sem], recv_sem=recv_sems.at[sem],
                    device_id=(px, py, pc), device_id_type=pl.DeviceIdType.MESH)
                send.start()
                recv = pltpu.make_async_remote_copy(
                    src_ref=src, dst_ref=dsts[i].at[peer],
                    send_sem=send_sems.at[sem], recv_sem=recv_sems.at[sem],
                    device_id=(px, py, pc), device_id_type=pl.DeviceIdType.MESH)
                remote.append((send, recv))
        for send, recv in remote:
            recv.wait_recv()
            send.wait_send()
        for cp in local:
            cp.wait()

    any_spec = pl.BlockSpec(memory_space=pl.ANY)
    return pl.pallas_call(
        body, name=name,
        in_specs=[any_spec] * n, out_specs=[any_spec] * n, out_shape=out_shape,
        scratch_shapes=[
            pltpu.SemaphoreType.DMA((n * (NDEV - 1),)),
            pltpu.SemaphoreType.DMA((n * (NDEV - 1),)),
            pltpu.SemaphoreType.DMA((n,)),
        ],
    )(*[a for a, _ in items])


def _gather_two_level(name, arrays):
    n = len(arrays)
    per = NDEV - 1

    def body(*refs):
        srcs = refs[:n]
        dsts = refs[n:2 * n]
        send_sems, recv_sems, loc_sems = refs[2 * n:]
        x = lax.axis_index("x")
        y = lax.axis_index("y")
        c = lax.axis_index("c")
        idx = lambda px, py, pc: 4 * px + 2 * py + pc
        me, sibling = (x, y, c), (x, y, 1 - c)
        chips = [(1 - x, y), (x, 1 - y), (1 - x, 1 - y)]

        def copy(i, k, block, to, src=None):
            slot = dsts[i].at[idx(*block)]
            return pltpu.make_async_remote_copy(
                src_ref=slot if src is None else src, dst_ref=slot,
                send_sem=send_sems.at[i * per + k], recv_sem=recv_sems.at[i * per + k],
                device_id=to, device_id_type=pl.DeviceIdType.MESH)

        mine, sent = [], []
        for i in range(n):
            cp = pltpu.make_async_copy(srcs[i], dsts[i].at[idx(*me)], loc_sems.at[i])
            cp.start()
            mine.append(cp)
            first = [copy(i, 0, me, sibling, src=srcs[i])]
            first += [copy(i, 1 + j, me, (*chip, c), src=srcs[i]) for j, chip in enumerate(chips)]
            for cp in first:
                cp.start()
            sent += first
        for j, chip in enumerate(chips):
            for i in range(n):
                copy(i, 1 + j, (*chip, c), me).wait_recv()
                fwd = copy(i, 4 + j, (*chip, c), sibling)
                fwd.start()
                sent.append(fwd)
        for i in range(n):
            copy(i, 0, sibling, me).wait_recv()
            for j, chip in enumerate(chips):
                copy(i, 4 + j, (*chip, 1 - c), me).wait_recv()
        for cp in sent:
            cp.wait_send()
        for cp in mine:
            cp.wait()

    any_spec = pl.BlockSpec(memory_space=pl.ANY)
    return pl.pallas_call(
        body, name=name, in_specs=[any_spec] * n, out_specs=[any_spec] * n,
        out_shape=[jax.ShapeDtypeStruct((NDEV,) + a.shape, a.dtype) for a in arrays],
        scratch_shapes=[pltpu.SemaphoreType.DMA((n * per,)), pltpu.SemaphoreType.DMA((n * per,)),
                        pltpu.SemaphoreType.DMA((n,))],
    )(*arrays)


HBM_SPEC = pl.BlockSpec(memory_space=pltpu.HBM)
SEM_SPEC = pl.BlockSpec(memory_space=pltpu.SEMAPHORE)
ANY_SPEC = pl.BlockSpec(memory_space=pl.ANY)
EFFECT = pltpu.SideEffectType.DATAFLOW_SIDE_EFFECTING


def _peer_of(k):
    x = lax.axis_index("x")
    y = lax.axis_index("y")
    c = lax.axis_index("c")
    px = (1 - x) if (k >> 2) & 1 else x
    py = (1 - y) if (k >> 1) & 1 else y
    pc = (1 - c) if k & 1 else c
    return (px, py, pc), 4 * px + 2 * py + pc


def _my_index():
    return 4 * lax.axis_index("x") + 2 * lax.axis_index("y") + lax.axis_index("c")


def _slot_shape(a, sc):
    return (NDEV,) + (a.shape[1:] if sc else a.shape)


def _divisor_tile(rows):
    return next((t for t in (512, 256, 176, 128, 64, 32, 16) if rows % t == 0), rows)


def _place(name, items, dtype=None):
    lands = []
    for idx, (a, sc) in enumerate(items):
        rows, cols = a.shape[-2:]
        tr = _divisor_tile(rows)
        out_dtype = a.dtype if dtype is None else dtype

        def body(s_ref, o_ref):
            o_ref[...] = s_ref[...].astype(o_ref.dtype)

        mine = pl.BlockSpec((None, tr, cols), lambda i: (_my_index(), i, 0))
        lands.append(pl.pallas_call(
            body, name=f"{name}_{idx}", grid=(rows // tr,),
            in_specs=[mine if sc else pl.BlockSpec((tr, cols), lambda i: (i, 0))],
            out_specs=mine,
            out_shape=jax.ShapeDtypeStruct(_slot_shape(a, sc), out_dtype),
            compiler_params=pltpu.CompilerParams(dimension_semantics=("parallel",)),
        )(a))
    return lands


def _xstart(name, items, lands, deps=()):
    n = len(items)
    scat = [sc for _, sc in items]
    srcs_in = [a for a, sc in items if sc]
    n_src = len(srcs_in)
    src_pos = {i: p for p, i in enumerate(i for i in range(n) if scat[i])}

    def body(*refs):
        srcs = refs[:n_src]
        lzs = refs[n_src:n_src + n]
        outs = refs[n_src + n + len(deps):]
        send_sems, recv_sems, token = outs[:n], outs[n:2 * n], outs[-1]
        me = _my_index()
        for i in range(n):
            for k in range(1, NDEV):
                peer_id, peer = _peer_of(k)
                src = srcs[src_pos[i]].at[peer] if scat[i] else lzs[i].at[me]
                pltpu.make_async_remote_copy(
                    src_ref=src, dst_ref=lzs[i].at[me],
                    send_sem=send_sems[i].at[k - 1], recv_sem=recv_sems[i].at[k - 1],
                    device_id=peer_id, device_id_type=pl.DeviceIdType.MESH).start()
        token[...] = jnp.zeros_like(token)

    sem = pltpu.SemaphoreType.DMA((NDEV - 1,))
    thru = srcs_in + list(lands)
    res = pl.pallas_call(
        body, name=name,
        in_specs=[HBM_SPEC] * len(thru) + [ANY_SPEC] * len(deps),
        out_specs=[SEM_SPEC] * (2 * n) + [HBM_SPEC] * len(thru) + [pl.BlockSpec(memory_space=pltpu.VMEM)],
        out_shape=[sem] * (2 * n) + [pltpu.HBM(t.shape, t.dtype) for t in thru]
        + [jax.ShapeDtypeStruct((8, 128), F32)],
        input_output_aliases={i: 2 * n + i for i in range(len(thru))},
        compiler_params=pltpu.CompilerParams(has_side_effects=EFFECT),
    )(*[pltpu.with_memory_space_constraint(t, pltpu.HBM) for t in thru], *deps)
    handles = [(res[i], res[n + i], res[2 * n + src_pos[i]] if scat[i] else None,
                res[2 * n + n_src + i]) for i in range(n)]
    return handles, res[-1]


def _xwait(name, handle, after):
    send_sem, recv_sem, src, land = handle
    sc = src is not None

    def body(*refs):
        land_ref = refs[1] if sc else refs[0]
        send_ref, recv_ref = (refs[2], refs[3]) if sc else (refs[1], refs[2])
        me = _my_index()
        for k in range(1, NDEV):
            peer_id, peer = _peer_of(k)
            cp = pltpu.make_async_remote_copy(
                src_ref=refs[0].at[peer] if sc else land_ref.at[me], dst_ref=land_ref.at[peer],
                send_sem=send_ref.at[k - 1], recv_sem=recv_ref.at[k - 1],
                device_id=peer_id, device_id_type=pl.DeviceIdType.MESH)
            cp.wait_send()
            cp.wait_recv()

    thru = ([src] if sc else []) + [land]
    return pl.pallas_call(
        body, name=name,
        in_specs=[HBM_SPEC] * len(thru) + [SEM_SPEC, SEM_SPEC, ANY_SPEC],
        out_specs=[HBM_SPEC] * len(thru),
        out_shape=[pltpu.HBM(t.shape, t.dtype) for t in thru],
        input_output_aliases={i: i for i in range(len(thru))},
        compiler_params=pltpu.CompilerParams(has_side_effects=EFFECT),
    )(*thru, send_sem, recv_sem, after)[-1]


def _mm(name, grid, in_specs, operands, out_specs, out_shape, contract, n_pairs, epilogue,
        acc_shape=None, vmem=VMEM_BIG, deps=(), group=1, a_cols=None, carry=()):
    nk = grid[2]
    n_carry = len(carry)

    def shard(ref, s, is_a):
        if group == 1:
            return ref[...]
        if is_a and a_cols is not None:
            return ref[:, s * a_cols:(s + 1) * a_cols]
        return ref[s]
    n_extra = len(operands) - 2 * n_pairs
    n_out = len(out_shape)
    n_in = len(operands) + len(deps)
    in_specs = list(in_specs) + [ANY_SPEC] * len(deps)
    operands = list(operands) + list(deps)

    def body(*refs):
        ab = refs[:2 * n_pairs]
        extras = refs[2 * n_pairs:2 * n_pairs + n_extra]
        outs = refs[n_in:n_in + n_out]
        kept = refs[n_in + n_out:n_in + n_out + n_carry]
        finish = (lambda acc: epilogue(acc, extras, outs, kept)) if n_carry else (
            lambda acc: epilogue(acc, extras, outs))
        dots = [(ab[2 * p], ab[2 * p + 1], s) for p in range(n_pairs) for s in range(group)]
        if nk == 1:
            part = None
            for a_ref, b_ref, s in dots:
                d = _dot(shard(a_ref, s, True), shard(b_ref, s, False), contract)
                part = d if part is None else part + d
            finish(part)
        else:
            acc_ref = refs[-1]
            k = pl.program_id(2)

            @pl.when(k == 0)
            def _():
                acc_ref[...] = jnp.zeros_like(acc_ref)

            for a_ref, b_ref, s in dots:
                acc_ref[...] += _dot(shard(a_ref, s, True), shard(b_ref, s, False), contract)

            @pl.when(k == nk - 1)
            def _():
                finish(acc_ref[...])

    scratch = list(carry) + ([pltpu.VMEM(acc_shape, F32)] if nk > 1 else [])
    semantics = ("arbitrary",) * 3 if n_carry else ("parallel", "parallel", "arbitrary")
    return pl.pallas_call(
        body, name=name, grid=grid, in_specs=in_specs, out_specs=out_specs, out_shape=out_shape,
        scratch_shapes=scratch,
        compiler_params=pltpu.CompilerParams(dimension_semantics=semantics, vmem_limit_bytes=vmem),
    )(*operands)


def _ep_cast(dtype):
    def ep(acc, extras, outs):
        outs[0][...] = acc.astype(dtype)
    return ep


def _ep_rms_bwd(n_rows_steps):
    def ep(acc, extras, outs, kept):
        h_ref, g_ref, dres_ref = extras
        gacc = kept[0]
        i = pl.program_id(0)
        v = h_ref[...]
        r = lax.rsqrt(jnp.mean(v * v, axis=-1, keepdims=True) + EPS)
        nrm = v * r
        dn = acc * g_ref[...]
        dh = dres_ref[...].astype(F32) + r * (dn - nrm * jnp.mean(dn * nrm, axis=-1, keepdims=True))
        outs[0][...] = dh.astype(outs[0].dtype)

        @pl.when(i == 0)
        def _():
            gacc[...] = jnp.zeros_like(gacc)

        gacc[...] += _fold8(acc * nrm)

        @pl.when(i == n_rows_steps - 1)
        def _():
            outs[-1][...] = jnp.sum(gacc[...], axis=0, keepdims=True)
    return ep


def _ep_resid_norm(acc, extras, outs):
    h = extras[0][...] + acc
    outs[0][...] = h
    r = lax.rsqrt(jnp.mean(h * h, axis=-1, keepdims=True) + EPS)
    outs[1][...] = (h * r * extras[1][...]).astype(BF16)


def _ep_swiglu_bwd(acc, extras, outs):
    outs[0][...] = (acc * extras[0][...].astype(F32)).astype(BF16)
    outs[1][...] = (acc * extras[1][...].astype(F32)).astype(BF16)


MXU_COLS_V7X = 256


def _col_chunks(n):
    return [slice(c, min(c + MXU_COLS_V7X, n)) for c in range(0, n, MXU_COLS_V7X)]


def _row_tile(T):
    return min(1024, T)


def _tn_rows(T):
    return min(2048, T)


WIDE = tuple(d for d in DILATIONS if d > 1)


LANES = 128


def _lane_tile(c):
    return slice(c * LANES, (c + 1) * LANES)


def _to_lane_tiles(scr, val):
    for c in range(scr.shape[0]):
        scr[c] = val[:, _lane_tile(c)]


def _emit_class_major(scr, refs, rows):
    for d, ref in zip(WIDE, refs):
        for r in range(d):
            for c in range(scr.shape[0]):
                ref[r, :, _lane_tile(c)] = scr[c, pl.ds(r, rows // d, stride=d), :].astype(ref.dtype)


def _shards_to_cols(name, w):
    _, rows, n = w.shape
    tr = rows

    def body(s_ref, o_ref):
        o_ref[...] = s_ref[...]

    return pl.pallas_call(
        body, name=name, grid=(rows // tr, NDEV),
        in_specs=[pl.BlockSpec((None, tr, n), lambda i, j: (j, i, 0))],
        out_specs=pl.BlockSpec((tr, n), lambda i, j: (i, j)),
        out_shape=jax.ShapeDtypeStruct((rows, NDEV * n), w.dtype),
        compiler_params=pltpu.CompilerParams(dimension_semantics=("parallel", "parallel")),
    )(w)


def _cols_to_shards(name, g, n):
    rows = g.shape[0]
    tr = rows

    def body(s_ref, o_ref):
        o_ref[...] = s_ref[...]

    return pl.pallas_call(
        body, name=name, grid=(rows // tr, NDEV),
        in_specs=[pl.BlockSpec((tr, n), lambda i, j: (i, j))],
        out_specs=pl.BlockSpec((None, tr, n), lambda i, j: (j, i, 0)),
        out_shape=jax.ShapeDtypeStruct((NDEV, rows, n), g.dtype),
        compiler_params=pltpu.CompilerParams(dimension_semantics=("parallel", "parallel")),
    )(g)


IN_TN = DA


def _mm_in(x, g, w_in, deps=()):
    T = x.shape[0]
    tm = min(TM_FULL_ROW, T)
    nq = 3

    def body(x_ref, g_ref, w_ref, *rest):
        a_ref, z_ref, *rest = rest[len(deps):]
        scr = rest[-1]
        j = pl.program_id(1)

        @pl.when(j == 0)
        def _():
            v = x_ref[...]
            r = lax.rsqrt(jnp.mean(v * v, axis=-1, keepdims=True) + EPS)
            a_ref[...] = (v * r * g_ref[...]).astype(BF16)

        @pl.when(j >= nq)
        def _():
            z_ref[...] = _dot(a_ref[...], w_ref[...], NN).astype(BF16)

        @pl.when(j < nq)
        def _():
            av = a_ref[...]
            chunks = _col_chunks(IN_TN)
            pending = _dot(av, w_ref[:, chunks[0]], NN)
            for ci, cols in enumerate(chunks):
                nxt = _dot(av, w_ref[:, chunks[ci + 1]], NN) if ci + 1 < len(chunks) else None
                z_ref[:, cols] = pending.astype(BF16)
                for c in range(cols.start // LANES, cols.stop // LANES):
                    scr[c] = pending[:, c * LANES - cols.start:(c + 1) * LANES - cols.start]
                    for d, ref in zip(WIDE, rest[:-1]):
                        for r in range(d):
                            ref[r, :, _lane_tile(c)] = scr[c, pl.ds(r, tm // d, stride=d), :].astype(BF16)
                pending = nxt

    cm_spec = lambda d: pl.BlockSpec((d, tm // d, IN_TN), lambda i, j: (0, i, jnp.minimum(j, nq - 1)))
    row = pl.BlockSpec((tm, D), lambda i, j: (i, 0))
    return pl.pallas_call(
        body, name="mm_in", grid=(T // tm, DIN // IN_TN),
        in_specs=[row, pl.BlockSpec((1, D), lambda i, j: (0, 0)),
                  pl.BlockSpec((D, IN_TN), lambda i, j: (0, j))] + [ANY_SPEC] * len(deps),
        out_specs=[row, pl.BlockSpec((tm, IN_TN), lambda i, j: (i, j))] + [cm_spec(d) for d in WIDE],
        out_shape=[jax.ShapeDtypeStruct((T, D), BF16), jax.ShapeDtypeStruct((T, DIN), BF16)]
        + [jax.ShapeDtypeStruct((d, T // d, nq * IN_TN), BF16) for d in WIDE],
        scratch_shapes=[pltpu.VMEM((IN_TN // LANES, tm, LANES), F32)],
        compiler_params=pltpu.CompilerParams(
            dimension_semantics=("parallel", "arbitrary"), vmem_limit_bytes=VMEM_BIG),
    )(x, g, w_in, *deps)


def _mm_out_bwd(dh1b, w_out, deps=()):
    T = dh1b.shape[0]
    tm = _row_tile(T)

    def body(dy_ref, w_ref, *rest):
        rest = rest[len(deps):]
        dom_ref, scr = rest[0], rest[-1]
        acc = _dot(dy_ref[...], w_ref[...], NT)
        dom_ref[...] = acc.astype(BF16)

        @pl.when(pl.program_id(1) == 0)
        def _():
            _to_lane_tiles(scr, acc)
            _emit_class_major(scr, rest[1:-1], tm)

    return pl.pallas_call(
        body, name="mm_out_bwd", grid=(T // tm, D // DA),
        in_specs=[pl.BlockSpec((tm, D), lambda i, j: (i, 0)),
                  pl.BlockSpec((DA, D), lambda i, j: (j, 0))] + [ANY_SPEC] * len(deps),
        out_specs=[pl.BlockSpec((tm, DA), lambda i, j: (i, j))]
        + [pl.BlockSpec((d, tm // d, DA), lambda i, j: (0, i, 0)) for d in WIDE],
        out_shape=[jax.ShapeDtypeStruct((T, D), BF16)]
        + [jax.ShapeDtypeStruct((d, T // d, DA), BF16) for d in WIDE],
        scratch_shapes=[pltpu.VMEM((DA // LANES, tm, LANES), F32)],
        compiler_params=pltpu.CompilerParams(
            dimension_semantics=("parallel", "arbitrary"), vmem_limit_bytes=VMEM_BIG),
    )(dh1b, w_out, *deps)


TM_FULL_ROW = 512


def _full_row_specs(tm):
    row = pl.BlockSpec((tm, D), lambda i, j, k: (i, 0))
    return row, pl.BlockSpec((1, D), lambda i, j, k: (0, 0))


def _mm_out(o_attn, o_conv, w_out, x, g_next):
    T = x.shape[0]
    tm = min(TM_FULL_ROW, T)
    row, vec = _full_row_specs(tm)
    return _mm(
        "mm_out", (T // tm, 1, 1),
        [pl.BlockSpec((tm, DA), lambda i, j, k: (i, 0)),
         pl.BlockSpec((DA, D), lambda i, j, k: (0, 0)),
         pl.BlockSpec((tm, DC), lambda i, j, k: (i, 0)),
         pl.BlockSpec((DC, D), lambda i, j, k: (1, 0)),
         row, vec],
        [o_attn, w_out, o_conv, w_out, x, g_next],
        [row, row],
        [jax.ShapeDtypeStruct((T, D), F32), jax.ShapeDtypeStruct((T, D), BF16)], NN, 2,
        _ep_resid_norm)


FF_TN = 512
FF_TK = 2 * N_FF


def _mm_gate_up(f, wg_t, wu_t):
    T = f.shape[0]
    tm = _row_tile(T)

    def body(f_ref, wg_ref, wu_ref, dg_ref, du_ref, a_ref):
        fv = f_ref[...]
        g = _dot(fv, wg_ref[...], NT)
        u = _dot(fv, wu_ref[...], NT)
        sg = _sigmoid(g)
        silu = g * sg
        dg_ref[...] = (u * (sg * (1.0 + g * (1.0 - sg)))).astype(BF16)
        du_ref[...] = silu.astype(BF16)
        a_ref[...] = (silu * u).astype(BF16)

    wspec = pl.BlockSpec((FF_TN, D), lambda i, j: (j, 0))
    ospec = pl.BlockSpec((tm, FF_TN), lambda i, j: (i, j))
    sh = jax.ShapeDtypeStruct((T, DFF), BF16)
    return pl.pallas_call(
        body, name="mm_gate_up", grid=(T // tm, DFF // FF_TN),
        in_specs=[pl.BlockSpec((tm, D), lambda i, j: (i, 0)), wspec, wspec],
        out_specs=[ospec, ospec, ospec], out_shape=[sh, sh, sh],
        compiler_params=pltpu.CompilerParams(
            dimension_semantics=("parallel", "parallel"), vmem_limit_bytes=VMEM_BIG),
    )(f, wg_t, wu_t)


def _mm_down(act, w_down, h1, g_next):
    T = h1.shape[0]
    tm = min(TM_FULL_ROW, T)
    row, vec = _full_row_specs(tm)
    return _mm(
        "mm_down", (T // tm, 1, DFF // FF_TK),
        [pl.BlockSpec((tm, FF_TK), lambda i, j, k: (i, k)),
         pl.BlockSpec((FF_TK, D), lambda i, j, k: (k, 0)),
         row, vec],
        [act, w_down, h1, g_next],
        [row, row],
        [jax.ShapeDtypeStruct((T, D), F32), jax.ShapeDtypeStruct((T, D), BF16)], NN, 1,
        _ep_resid_norm, acc_shape=(tm, D))


def _ple_loss(r, w_pgate, b_pgate, p, w_ple, h2, target, g_final):
    T = h2.shape[0]
    tm = min(256, T)
    nt = T // tm

    def body(r_ref, wg_ref, b_ref, p_ref, wp_ref, h2_ref, t_ref, g_ref,
             loss_ref, dh_ref, dpe_ref, dpg_ref, dgf_ref, dbp_ref, lacc, gacc, bacc):
        i = pl.program_id(0)
        gte = _sigmoid(_dot(r_ref[...], wg_ref[...], NN) + b_ref[...])
        pe = _dot(p_ref[...], wp_ref[...], NN)
        v = h2_ref[...] + pe * gte
        rr = lax.rsqrt(jnp.mean(v * v, axis=-1, keepdims=True) + EPS)
        nrm = v * rr
        g = g_ref[...]
        err = nrm * g - t_ref[...]
        dy = err * (1.0 / D)
        dn = dy * g
        dh = rr * (dn - nrm * jnp.mean(dn * nrm, axis=-1, keepdims=True))
        dh_ref[...] = dh.astype(BF16)
        dpe_ref[...] = (dh * gte).astype(BF16)
        dpg = dh * pe * gte * (1.0 - gte)
        dpg_ref[...] = dpg.astype(BF16)

        @pl.when(i == 0)
        def _():
            lacc[...] = jnp.zeros_like(lacc)
            gacc[...] = jnp.zeros_like(gacc)
            bacc[...] = jnp.zeros_like(bacc)

        lacc[...] += _fold8(err * err)
        gacc[...] += _fold8(dy * nrm)
        bacc[...] += _fold8(dpg)

        @pl.when(i == nt - 1)
        def _():
            tot = jnp.sum(jnp.sum(lacc[...], axis=0, keepdims=True), axis=1, keepdims=True)
            loss_ref[...] = jnp.broadcast_to(tot * (0.5 / D), (1, 128))
            dgf_ref[...] = jnp.sum(gacc[...], axis=0, keepdims=True)
            dbp_ref[...] = jnp.sum(bacc[...], axis=0, keepdims=True)

    row = pl.BlockSpec((tm, D), lambda i: (i, 0))
    vec = pl.BlockSpec((1, D), lambda i: (0, 0))
    return pl.pallas_call(
        body, name="ple_loss", grid=(nt,),
        in_specs=[row, pl.BlockSpec((D, D), lambda i: (0, 0)), vec,
                  pl.BlockSpec((tm, DPLE), lambda i: (i, 0)),
                  pl.BlockSpec((DPLE, D), lambda i: (0, 0)), row, row, vec],
        out_specs=[pl.BlockSpec((1, 128), lambda i: (0, 0)), row, row, row, vec, vec],
        out_shape=[jax.ShapeDtypeStruct((1, 128), F32), jax.ShapeDtypeStruct((T, D), BF16),
                   jax.ShapeDtypeStruct((T, D), BF16), jax.ShapeDtypeStruct((T, D), BF16),
                   jax.ShapeDtypeStruct((1, D), F32), jax.ShapeDtypeStruct((1, D), F32)],
        scratch_shapes=[pltpu.VMEM((8, D), F32)] * 3,
        compiler_params=pltpu.CompilerParams(
            dimension_semantics=("arbitrary",), vmem_limit_bytes=VMEM_BIG),
    )(r, w_pgate, b_pgate, p, w_ple, h2, target, g_final)


def _mm_down_bwd(dh2, w_down, g, u, deps=()):
    T = dh2.shape[0]
    tm = _tn_rows(T)
    gspec = pl.BlockSpec((tm, FF_TN), lambda i, j, k: (i, j))
    sh = jax.ShapeDtypeStruct((T, DFF), BF16)
    return _mm(
        "mm_down_bwd", (T // tm, DFF // FF_TN, 1),
        [pl.BlockSpec((tm, D), lambda i, j, k: (i, 0)),
         pl.BlockSpec((FF_TN, D), lambda i, j, k: (j, 0)),
         gspec, gspec],
        [dh2, w_down, g, u],
        [gspec, gspec], [sh, sh], NT, 1, _ep_swiglu_bwd, deps=deps)


def _mm_ffn_in_bwd(dg, wg_t, du, wu_t, deps=()):
    T = dg.shape[0]
    tm = _row_tile(T)
    tn = 1024
    aspec = pl.BlockSpec((tm, FF_TK), lambda i, j, k: (i, k))
    wspec = pl.BlockSpec((FF_TK, tn), lambda i, j, k: (k, j))
    return _mm(
        "mm_ffn_in_bwd", (T // tm, D // tn, DFF // FF_TK),
        [aspec, wspec, aspec, wspec], [dg, wg_t, du, wu_t],
        [pl.BlockSpec((tm, tn), lambda i, j, k: (i, j))],
        [jax.ShapeDtypeStruct((T, D), BF16)], NN, 2, _ep_cast(BF16), acc_shape=(tm, tn),
        deps=deps)[0]


def _mm_in_bwd(dz, w_in, x, g, dres, deps=()):
    T = dz.shape[0]
    tm = min(TM_FULL_ROW, T)
    tk = DIN // 4
    row, vec = _full_row_specs(tm)
    return _mm(
        "mm_in_bwd", (T // tm, 1, DIN // tk),
        [pl.BlockSpec((tm, tk), lambda i, j, k: (i, k)),
         pl.BlockSpec((D, tk), lambda i, j, k: (0, k)),
         row, vec, row],
        [dz, w_in, x, g, dres],
        [row, vec],
        [jax.ShapeDtypeStruct((T, D), F32), jax.ShapeDtypeStruct((1, D), F32)], NT, 1,
        _ep_rms_bwd(T // tm), acc_shape=(tm, D), deps=deps,
        carry=[pltpu.VMEM((8, D), F32)])


def _mm_pgate_bwd(dpg, w_pgate, h2, g, dres, deps=()):
    T = dpg.shape[0]
    tm = min(256, T)
    row, vec = _full_row_specs(tm)
    return _mm(
        "mm_pgate_bwd", (T // tm, 1, 1),
        [row, pl.BlockSpec((D, D), lambda i, j, k: (0, 0)), row, vec, row],
        [dpg, w_pgate, h2, g, dres],
        [row, vec],
        [jax.ShapeDtypeStruct((T, D), BF16), jax.ShapeDtypeStruct((1, D), F32)], NT, 1,
        _ep_rms_bwd(T // tm), deps=deps, carry=[pltpu.VMEM((8, D), F32)])


def _mm_tn(name, a, b, tj=None):
    T, idim = a.shape
    jdim = b.shape[1]
    tt = _row_tile(T)
    ti = min(idim, 1024)
    tj = jdim if tj is None else tj
    return _mm(
        name, (idim // ti, jdim // tj, T // tt),
        [pl.BlockSpec((tt, ti), lambda i, j, k: (k, i)),
         pl.BlockSpec((tt, tj), lambda i, j, k: (k, j))],
        [a, b],
        [pl.BlockSpec((ti, tj), lambda i, j, k: (i, j))],
        [jax.ShapeDtypeStruct((idim, jdim), BF16)], TN, 1, _ep_cast(BF16), acc_shape=(ti, tj))[0]


def _mm_tn_wide(name, a, b):
    T = a.shape[0]
    jdim = b.shape[1]
    tt = _row_tile(T)
    return _mm(
        name, (1, jdim // IN_TN, T // tt),
        [pl.BlockSpec((tt, D), lambda i, j, k: (k, 0)),
         pl.BlockSpec((tt, IN_TN), lambda i, j, k: (k, j))],
        [a, b],
        [pl.BlockSpec((D, IN_TN), lambda i, j, k: (0, j))],
        [jax.ShapeDtypeStruct((D, jdim), BF16)], TN, 1, _ep_cast(BF16),
        acc_shape=(D, IN_TN))[0]


def _mm_tn_ff(name, a, b):
    T = b.shape[0]
    tt = _tn_rows(T)
    return _mm(
        name, (DFF // FF_TN, 1, T // tt),
        [pl.BlockSpec((tt, FF_TN), lambda i, j, k: (k, i)),
         pl.BlockSpec((tt, D), lambda i, j, k: (k, 0))],
        [a, b],
        [pl.BlockSpec((FF_TN, D), lambda i, j, k: (i, 0))],
        [jax.ShapeDtypeStruct((DFF, D), BF16)], TN, 1, _ep_cast(BF16),
        acc_shape=(FF_TN, D))[0]


TR = 256


def _rows(T):
    return min(TR, T)


def _fold8(v):
    return jnp.sum(v.reshape(v.shape[0] // 8, 8, v.shape[1]), axis=0)


def _rms_bwd(name, dn_out, h, g, dres):
    T = h.shape[0]
    tr = _rows(T)
    nt = T // tr

    def body(dy_ref, h_ref, g_ref, dres_ref, dh_ref, dg_ref, acc):
        i = pl.program_id(0)
        v = h_ref[...]
        r = lax.rsqrt(jnp.mean(v * v, axis=-1, keepdims=True) + EPS)
        nrm = v * r
        dy = dy_ref[...].astype(F32)
        dn = dy * g_ref[...]
        dh = dres_ref[...].astype(F32) + r * (dn - nrm * jnp.mean(dn * nrm, axis=-1, keepdims=True))
        dh_ref[...] = dh.astype(BF16)

        @pl.when(i == 0)
        def _():
            acc[...] = jnp.zeros_like(acc)

        acc[...] += _fold8(dy * nrm)

        @pl.when(i == nt - 1)
        def _():
            dg_ref[...] = jnp.sum(acc[...], axis=0, keepdims=True)

    tile = pl.BlockSpec((tr, D), lambda i: (i, 0))
    vec = pl.BlockSpec((1, D), lambda i: (0, 0))
    return pl.pallas_call(
        body, name=name, grid=(nt,),
        in_specs=[tile, tile, vec, tile], out_specs=[tile, vec],
        out_shape=[jax.ShapeDtypeStruct((T, D), BF16), jax.ShapeDtypeStruct((1, D), F32)],
        scratch_shapes=[pltpu.VMEM((8, D), F32)],
        compiler_params=pltpu.CompilerParams(dimension_semantics=("arbitrary",)),
    )(dn_out, h, g, dres)


def _band_masks():
    qi = lax.broadcasted_iota(jnp.int32, (BLK, BLK), 0)
    kj = lax.broadcasted_iota(jnp.int32, (BLK, BLK), 1)
    return kj >= qi, kj <= qi


AQ = 4
ATTN_PARAMS = pltpu.CompilerParams(
    dimension_semantics=("parallel", "parallel"), vmem_limit_bytes=VMEM_BIG)


def _cm_spec(d, col, nblk, rowmap=lambda n: n):
    if d == 1:
        return pl.BlockSpec((nblk * BLK, DA), lambda r, n: (rowmap(n), col))
    return pl.BlockSpec((None, nblk * BLK, DA), lambda r, n: (r, rowmap(n), col))


def _cm_shape(d, T, dtype):
    return jax.ShapeDtypeStruct((T, DA) if d == 1 else (d, T // d, DA), dtype)


def _blk(b):
    return slice(b * BLK, (b + 1) * BLK)


HEADS = tuple(slice(h * DH, (h + 1) * DH) for h in range(NH))


def _attn_fwd(name, zsrc, d, T):
    nb = T // d // BLK
    aq = min(AQ, nb)
    scale = DH ** -0.5

    def body(q_ref, kp_ref, kc_ref, vp_ref, vc_ref, o_ref, l_ref):
        n = pl.program_id(1)
        band_prev, cur_ok = _band_masks()
        for b in range(aq):
            kp = (lambda sl: kp_ref[:, sl]) if b == 0 else (lambda sl, b=b: kc_ref[_blk(b - 1), sl])
            vp = (lambda sl: vp_ref[:, sl]) if b == 0 else (lambda sl, b=b: vc_ref[_blk(b - 1), sl])
            prev_ok = band_prev & (n > 0) if b == 0 else band_prev
            rows = _blk(b)
            s = [(jnp.where(prev_ok, _dot(q_ref[rows, sl], kp(sl), NT) * scale, NEG),
                  jnp.where(cur_ok, _dot(q_ref[rows, sl], kc_ref[rows, sl], NT) * scale, NEG))
                 for sl in HEADS]
            m = [jnp.maximum(jnp.max(sp, axis=1, keepdims=True), jnp.max(sc, axis=1, keepdims=True))
                 for sp, sc in s]
            p = [(jnp.exp(sp - mh), jnp.exp(sc - mh)) for (sp, sc), mh in zip(s, m)]
            den = [jnp.sum(pp, axis=1, keepdims=True) + jnp.sum(pc, axis=1, keepdims=True)
                   for pp, pc in p]
            o = [_dot(pp.astype(BF16), vp(sl), NN) + _dot(pc.astype(BF16), vc_ref[rows, sl], NN)
                 for (pp, pc), sl in zip(p, HEADS)]
            o_ref[rows, :] = jnp.concatenate(
                [(oh / dh).astype(BF16) for oh, dh in zip(o, den)], axis=1)
            l_ref[rows, :] = jnp.concatenate(
                [jnp.broadcast_to(mh + jnp.log(dh), (BLK, DH)) for mh, dh in zip(m, den)], axis=1)

    halo = lambda n: jnp.maximum(aq * n - 1, 0)
    return pl.pallas_call(
        body, name=name, grid=(d, nb // aq),
        in_specs=[_cm_spec(d, 0, aq), _cm_spec(d, 1, 1, halo), _cm_spec(d, 1, aq),
                  _cm_spec(d, 2, 1, halo), _cm_spec(d, 2, aq)],
        out_specs=[_cm_spec(d, 0, aq)] * 2,
        out_shape=[_cm_shape(d, T, BF16), _cm_shape(d, T, F32)],
        compiler_params=ATTN_PARAMS,
    )(zsrc, zsrc, zsrc, zsrc, zsrc)


def _cm_tile(d, tr):
    if d == 1:
        return pl.BlockSpec((tr, DA), lambda i: (i, 0))
    return pl.BlockSpec((d, tr // d, DA), lambda i: (0, i, 0))


def _attn_combine(outs, lses, T):
    tr = _rows(T)

    def body(*refs):
        o_in, l_in = refs[:3], refs[3:6]
        o_ref, l_ref = refs[6:8]
        o_cm, l_cm = refs[8:8 + len(WIDE)], refs[8 + len(WIDE):8 + 2 * len(WIDE)]
        so, sl, so_all, sl_all = refs[8 + 2 * len(WIDE):]
        for c in range(DA // LANES):
            lt = _lane_tile(c)
            os_, ls_ = [o_in[0][:, lt].astype(F32)], [l_in[0][:, lt]]
            for w, d in enumerate(WIDE):
                for r in range(d):
                    so[w, c, pl.ds(r, tr // d, stride=d), :] = o_in[1 + w][r, :, lt].astype(F32)
                    sl[w, c, pl.ds(r, tr // d, stride=d), :] = l_in[1 + w][r, :, lt]
                os_.append(so[w, c])
                ls_.append(sl[w, c])
            la, lb, lc = ls_
            m = jnp.maximum(jnp.maximum(la, lb), lc)
            ea, eb, ec = jnp.exp(la - m), jnp.exp(lb - m), jnp.exp(lc - m)
            s = ea + eb + ec
            o = (ea * os_[0] + eb * os_[1] + ec * os_[2]) / s
            lse = m + jnp.log(s)
            o_ref[:, lt] = o.astype(BF16)
            l_ref[:, lt] = lse
            so_all[c] = o
            sl_all[c] = lse
        _emit_class_major(so_all, o_cm, tr)
        _emit_class_major(sl_all, l_cm, tr)

    specs = [_cm_tile(d, tr) for d in DILATIONS]
    wide = [_cm_tile(d, tr) for d in WIDE]
    return pl.pallas_call(
        body, name="attn_combine", grid=(T // tr,),
        in_specs=specs + specs,
        out_specs=[specs[0], specs[0]] + wide + wide,
        out_shape=[_cm_shape(1, T, BF16), _cm_shape(1, T, F32)]
        + [_cm_shape(d, T, BF16) for d in WIDE] + [_cm_shape(d, T, F32) for d in WIDE],
        scratch_shapes=[pltpu.VMEM((len(WIDE), DA // LANES, tr, LANES), F32)] * 2
        + [pltpu.VMEM((DA // LANES, tr, LANES), F32)] * 2,
        compiler_params=pltpu.CompilerParams(
            dimension_semantics=("parallel",), vmem_limit_bytes=VMEM_MID),
    )(*outs, *lses)


def _attn_bwd_q(name, zsrc, dosrc, osrc, lsrc, d, T):
    nb = T // d // BLK
    aq = min(AQ, nb)
    scale = DH ** -0.5

    def body(q_ref, kp_ref, kc_ref, vp_ref, vc_ref, do_ref, o_ref, l_ref, dq_ref):
        n = pl.program_id(1)
        band_prev, cur_ok = _band_masks()
        for b in range(aq):
            kp = (lambda sl: kp_ref[:, sl]) if b == 0 else (lambda sl, b=b: kc_ref[_blk(b - 1), sl])
            vp = (lambda sl: vp_ref[:, sl]) if b == 0 else (lambda sl, b=b: vc_ref[_blk(b - 1), sl])
            prev_ok = band_prev & (n > 0) if b == 0 else band_prev
            rows = _blk(b)
            s = [(_dot(q_ref[rows, sl], kp(sl), NT), _dot(q_ref[rows, sl], kc_ref[rows, sl], NT))
                 for sl in HEADS]
            dp = [(_dot(do_ref[rows, sl], vp(sl), NT), _dot(do_ref[rows, sl], vc_ref[rows, sl], NT))
                  for sl in HEADS]
            delta = [jnp.sum(do_ref[rows, sl].astype(F32) * o_ref[rows, sl].astype(F32), axis=1,
                             keepdims=True) for sl in HEADS]
            p = [(jnp.exp(jnp.where(prev_ok, sp * scale - l_ref[rows, sl], NEG)),
                  jnp.exp(jnp.where(cur_ok, sc * scale - l_ref[rows, sl], NEG)))
                 for (sp, sc), sl in zip(s, HEADS)]
            ds = [((pp * (dpp - dl) * scale).astype(BF16), (pc * (dpc - dl) * scale).astype(BF16))
                  for (pp, pc), (dpp, dpc), dl in zip(p, dp, delta)]
            dq = [_dot(dsp, kp(sl), NN) + _dot(dsc, kc_ref[rows, sl], NN)
                  for (dsp, dsc), sl in zip(ds, HEADS)]
            dq_ref[rows, :] = jnp.concatenate([v.astype(BF16) for v in dq], axis=1)

    halo = lambda n: jnp.maximum(aq * n - 1, 0)
    own = _cm_spec(d, 0, aq)
    return pl.pallas_call(
        body, name=name, grid=(d, nb // aq),
        in_specs=[own, _cm_spec(d, 1, 1, halo), _cm_spec(d, 1, aq), _cm_spec(d, 2, 1, halo),
                  _cm_spec(d, 2, aq), own, own, own],
        out_specs=own, out_shape=_cm_shape(d, T, BF16),
        compiler_params=ATTN_PARAMS,
    )(zsrc, zsrc, zsrc, zsrc, zsrc, dosrc, osrc, lsrc)


def _attn_bwd_kv(name, zsrc, dosrc, osrc, lsrc, d, T):
    nb = T // d // BLK
    aq = min(AQ, nb)
    nsteps = nb // aq
    scale = DH ** -0.5

    def body(k_ref, v_ref, q_ref, qn_ref, do_ref, don_ref, o_ref, on_ref, l_ref, ln_ref,
             dk_ref, dv_ref):
        j = pl.program_id(1)
        band_next, own_ok = _band_masks()
        for b in range(aq):
            rows = _blk(b)
            last = b == aq - 1
            pick = lambda cur, halo: ((lambda sl: halo[:, sl]) if last
                                      else (lambda sl, b=b: cur[_blk(b + 1), sl]))
            qb, dob, ob, lb = (pick(q_ref, qn_ref), pick(do_ref, don_ref), pick(o_ref, on_ref),
                               pick(l_ref, ln_ref))
            next_ok = band_next & (j < nsteps - 1) if last else band_next
            s = [(_dot(q_ref[rows, sl], k_ref[rows, sl], NT), _dot(qb(sl), k_ref[rows, sl], NT))
                 for sl in HEADS]
            dp = [(_dot(do_ref[rows, sl], v_ref[rows, sl], NT), _dot(dob(sl), v_ref[rows, sl], NT))
                  for sl in HEADS]
            delta = [(jnp.sum(do_ref[rows, sl].astype(F32) * o_ref[rows, sl].astype(F32), axis=1,
                              keepdims=True),
                      jnp.sum(dob(sl).astype(F32) * ob(sl).astype(F32), axis=1, keepdims=True))
                     for sl in HEADS]
            p = [(jnp.exp(jnp.where(own_ok, sa * scale - l_ref[rows, sl], NEG)),
                  jnp.exp(jnp.where(next_ok, sb * scale - lb(sl), NEG)))
                 for (sa, sb), sl in zip(s, HEADS)]
            dv = [_dot(pa.astype(BF16), do_ref[rows, sl], TN) + _dot(pb.astype(BF16), dob(sl), TN)
                  for (pa, pb), sl in zip(p, HEADS)]
            ds = [((pa * (dpa - da) * scale).astype(BF16), (pb * (dpb - db) * scale).astype(BF16))
                  for (pa, pb), (dpa, dpb), (da, db) in zip(p, dp, delta)]
            dk = [_dot(dsa, q_ref[rows, sl], TN) + _dot(dsb, qb(sl), TN)
                  for (dsa, dsb), sl in zip(ds, HEADS)]
            dk_ref[rows, :] = jnp.concatenate([v.astype(BF16) for v in dk], axis=1)
            dv_ref[rows, :] = jnp.concatenate([v.astype(BF16) for v in dv], axis=1)

    halo = lambda j: jnp.minimum(aq * (j + 1), nb - 1)
    own, own_n = _cm_spec(d, 0, aq), _cm_spec(d, 0, 1, halo)
    sh = _cm_shape(d, T, BF16)
    return pl.pallas_call(
        body, name=name, grid=(d, nsteps),
        in_specs=[_cm_spec(d, 1, aq), _cm_spec(d, 2, aq), own, own_n, own, own_n, own, own_n,
                  own, own_n],
        out_specs=[own, own], out_shape=[sh, sh],
        compiler_params=ATTN_PARAMS,
    )(zsrc, zsrc, zsrc, zsrc, dosrc, dosrc, osrc, osrc, lsrc, lsrc)


def _dz_assemble(dqs, dks, dvs, dcvg, T):
    tr = _rows(T)
    nb = len(DILATIONS)

    def body(*refs):
        cvg_ref, dz_ref, scr = refs[3 * nb], refs[3 * nb + 1], refs[3 * nb + 2]
        for g in range(3):
            parts = refs[g * nb:(g + 1) * nb]
            for c in range(DA // LANES):
                lt = _lane_tile(c)
                scr[g, c] = parts[0][:, lt].astype(F32)
                for w, d in enumerate(WIDE):
                    for r in range(d):
                        rows = pl.ds(r, tr // d, stride=d)
                        scr[g, c, rows, :] = scr[g, c, rows, :] + parts[1 + w][r, :, lt].astype(F32)
                dz_ref[:, g * DA + c * LANES:g * DA + (c + 1) * LANES] = scr[g, c].astype(BF16)
        dz_ref[:, 3 * DA:] = cvg_ref[...]

    specs = [_cm_tile(d, tr) for d in DILATIONS]
    return pl.pallas_call(
        body, name="dz_assemble", grid=(T // tr,),
        in_specs=specs * 3 + [pl.BlockSpec((tr, 2 * DC), lambda i: (i, 0))],
        out_specs=pl.BlockSpec((tr, DIN), lambda i: (i, 0)),
        out_shape=jax.ShapeDtypeStruct((T, DIN), BF16),
        scratch_shapes=[pltpu.VMEM((3, DA // LANES, tr, LANES), F32)],
        compiler_params=pltpu.CompilerParams(
            dimension_semantics=("parallel",), vmem_limit_bytes=VMEM_MID),
    )(*dqs, *dks, *dvs, dcvg)


CT = 256
HALO = 32
RC = 32


def _conv_fwd(z, w_dw, b_dw, g_ln, b_ln):
    T = z.shape[0]
    ct = min(CT, T)
    nt = T // ct
    hb = ct // HALO

    def body(cv_ref, cg_ref, cvp_ref, cgp_ref, w_ref, bdw_ref, g_ref, b_ref, oc_ref, y_ref, ubuf, ush):
        i = pl.program_id(0)
        up = cvp_ref[...].astype(F32) * _sigmoid(cgp_ref[...].astype(F32))
        ubuf[0:HALO, :] = jnp.where(i > 0, up, 0.0)
        ubuf[HALO:, :] = cv_ref[...].astype(F32) * _sigmoid(cg_ref[...].astype(F32))
        for b in range(8):
            ush[b] = ubuf[pl.ds(8 - b, ct + 24), :]

        def chunk(ci, carry):
            r0 = pl.multiple_of(ci * RC, RC)
            acc = jnp.broadcast_to(bdw_ref[...], (RC, DC))
            for s in range(CW):
                a, b = divmod(s, 8)
                acc = acc + w_ref[CW - 1 - s:CW - s, :] * ush[b, pl.ds(r0 + 24 - 8 * a, RC), :]
            y_ref[pl.ds(r0, RC), :] = acc
            mu = jnp.mean(acc, axis=-1, keepdims=True)
            cen = acc - mu
            var = jnp.mean(cen * cen, axis=-1, keepdims=True)
            ln = cen * lax.rsqrt(var + EPS) * g_ref[...] + b_ref[...]
            oc_ref[pl.ds(r0, RC), :] = (ln * _sigmoid(ln)).astype(BF16)
            return carry

        lax.fori_loop(0, ct // RC, chunk, 0)

    cur = lambda col: pl.BlockSpec((ct, DC), lambda i: (i, col))
    prv = lambda col: pl.BlockSpec((HALO, DC), lambda i: (jnp.maximum(i * hb - 1, 0), col))
    vec = pl.BlockSpec((1, DC), lambda i: (0, 0))
    return pl.pallas_call(
        body, name="conv_fwd", grid=(nt,),
        in_specs=[cur(3), cur(4), prv(3), prv(4), pl.BlockSpec((CW, DC), lambda i: (0, 0)),
                  vec, vec, vec],
        out_specs=[pl.BlockSpec((ct, DC), lambda i: (i, 0))] * 2,
        out_shape=[jax.ShapeDtypeStruct((T, DC), BF16), jax.ShapeDtypeStruct((T, DC), F32)],
        scratch_shapes=[pltpu.VMEM((ct + HALO, DC), F32), pltpu.VMEM((8, ct + 24, DC), F32)],
        compiler_params=pltpu.CompilerParams(
            dimension_semantics=("parallel",), vmem_limit_bytes=VMEM_MID),
    )(z, z, z, z, w_dw, b_dw, g_ln, b_ln)


def _conv_bwd(z, dom, y, w_dw, g_ln, b_ln):
    T = z.shape[0]
    ct = min(CT, T)
    nt = T // ct
    hb = ct // HALO
    last_halo = T // HALO - 1

    def ln_bwd(yv, dov, g_ref, b_ref):
        mu = jnp.mean(yv, axis=-1, keepdims=True)
        cen = yv - mu
        rstd = lax.rsqrt(jnp.mean(cen * cen, axis=-1, keepdims=True) + EPS)
        xhat = cen * rstd
        ln = xhat * g_ref[...] + b_ref[...]
        sg = _sigmoid(ln)
        dln = dov * (sg * (1.0 + ln * (1.0 - sg)))
        dxh = dln * g_ref[...]
        dy = rstd * (dxh - jnp.mean(dxh, axis=-1, keepdims=True)
                     - xhat * jnp.mean(dxh * xhat, axis=-1, keepdims=True))
        return dy, dln, xhat

    def body(do_ref, don_ref, y_ref, yn_ref, cv_ref, cg_ref, cvp_ref, cgp_ref, w_ref, g_ref, b_ref,
             dcvg_ref, dw_ref, dbdw_ref, dg_ref, db_ref,
             dybuf, dysh, ubuf, ush, dwacc, vacc):
        i = pl.program_id(0)

        @pl.when(i == 0)
        def _():
            dwacc[...] = jnp.zeros_like(dwacc)
            vacc[...] = jnp.zeros_like(vacc)

        def ln_chunk(ci, carry):
            r0 = pl.multiple_of(ci * RC, RC)
            dy, dln, xhat = ln_bwd(y_ref[pl.ds(r0, RC), :], do_ref[pl.ds(r0, RC), :].astype(F32),
                                   g_ref, b_ref)
            dybuf[pl.ds(r0, RC), :] = dy
            vacc[0] += _fold8(dy)
            vacc[1] += _fold8(dln * xhat)
            vacc[2] += _fold8(dln)
            return carry

        lax.fori_loop(0, ct // RC, ln_chunk, 0)
        dyn, _, _ = ln_bwd(yn_ref[...], don_ref[...].astype(F32), g_ref, b_ref)
        dybuf[ct:, :] = jnp.where(i < nt - 1, dyn, 0.0)
        for b in range(8):
            dysh[b] = dybuf[pl.ds(b, ct + 24), :]

        up = cvp_ref[...].astype(F32) * _sigmoid(cgp_ref[...].astype(F32))
        ubuf[0:HALO, :] = jnp.where(i > 0, up, 0.0)
        ubuf[HALO:, :] = cv_ref[...].astype(F32) * _sigmoid(cg_ref[...].astype(F32))
        for b in range(8):
            ush[b] = ubuf[pl.ds(8 - b, ct + 24), :]

        def chunk(ci, carry):
            r0 = pl.multiple_of(ci * RC, RC)
            dy = dybuf[pl.ds(r0, RC), :]
            du = jnp.zeros((RC, DC), F32)
            for s in range(CW):
                a, b = divmod(s, 8)
                du = du + w_ref[CW - 1 - s:CW - s, :] * dysh[b, pl.ds(r0 + 8 * a, RC), :]
                dwacc[CW - 1 - s] += _fold8(dy * ush[b, pl.ds(r0 + 24 - 8 * a, RC), :])
            cv = cv_ref[pl.ds(r0, RC), :].astype(F32)
            sg = _sigmoid(cg_ref[pl.ds(r0, RC), :].astype(F32))
            dcvg_ref[pl.ds(r0, RC), 0:DC] = (du * sg).astype(BF16)
            dcvg_ref[pl.ds(r0, RC), DC:2 * DC] = (du * cv * sg * (1.0 - sg)).astype(BF16)
            return carry

        lax.fori_loop(0, ct // RC, chunk, 0)

        @pl.when(i == nt - 1)
        def _():
            dw_ref[...] = jnp.sum(dwacc[...], axis=1)
            dbdw_ref[...] = jnp.sum(vacc[0], axis=0, keepdims=True)
            dg_ref[...] = jnp.sum(vacc[1], axis=0, keepdims=True)
            db_ref[...] = jnp.sum(vacc[2], axis=0, keepdims=True)

    cur = lambda col: pl.BlockSpec((ct, DC), lambda i: (i, col))
    prv = lambda col: pl.BlockSpec((HALO, DC), lambda i: (jnp.maximum(i * hb - 1, 0), col))
    nxt = lambda col: pl.BlockSpec((HALO, DC), lambda i: (jnp.minimum((i + 1) * hb, last_halo), col))
    vec = pl.BlockSpec((1, DC), lambda i: (0, 0))
    tile = pl.BlockSpec((ct, DC), lambda i: (i, 0))
    return pl.pallas_call(
        body, name="conv_bwd", grid=(nt,),
        in_specs=[cur(1), nxt(1), cur(0), nxt(0), cur(3), cur(4), prv(3), prv(4),
                  pl.BlockSpec((CW, DC), lambda i: (0, 0)), vec, vec],
        out_specs=[pl.BlockSpec((ct, 2 * DC), lambda i: (i, 0)),
                   pl.BlockSpec((CW, DC), lambda i: (0, 0)), vec, vec, vec],
        out_shape=[jax.ShapeDtypeStruct((T, 2 * DC), BF16),
                   jax.ShapeDtypeStruct((CW, DC), F32), jax.ShapeDtypeStruct((1, DC), F32),
                   jax.ShapeDtypeStruct((1, DC), F32), jax.ShapeDtypeStruct((1, DC), F32)],
        scratch_shapes=[pltpu.VMEM((ct + HALO, DC), F32), pltpu.VMEM((8, ct + 24, DC), F32),
                        pltpu.VMEM((ct + HALO, DC), F32), pltpu.VMEM((8, ct + 24, DC), F32),
                        pltpu.VMEM((CW, 8, DC), F32), pltpu.VMEM((3, 8, DC), F32)],
        compiler_params=pltpu.CompilerParams(
            dimension_semantics=("arbitrary",), vmem_limit_bytes=VMEM_BIG),
    )(dom, dom, y, y, z, z, z, z, w_dw, g_ln, b_ln)


def _adam_math(w, g, m, v):
    m = ADAM_B1 * m + (1.0 - ADAM_B1) * g
    v = ADAM_B2 * v + (1.0 - ADAM_B2) * (g * g)
    m_hat = m / (1.0 - ADAM_B1 ** ADAM_STEP)
    v_hat = v / (1.0 - ADAM_B2 ** ADAM_STEP)
    delta = -ADAM_LR * (m_hat / (jnp.sqrt(v_hat) + ADAM_EPS) + ADAM_WD * w)
    return delta, m, v


def _adam(name, slots, w, m, v):
    rows, cols = w.shape
    tr = next(t for t in (256, 176, 128, 64, 32, 16, 8, rows) if rows % t == 0)

    def body(s_ref, w_ref, m_ref, v_ref, g_out, d_out, m_out, v_out):
        g = s_ref[0].astype(F32)
        for s in range(1, NDEV):
            g = g + s_ref[s].astype(F32)
        delta, mn, vn = _adam_math(w_ref[...], g, m_ref[...], v_ref[...])
        g_out[...] = g
        d_out[...] = delta
        m_out[...] = mn
        v_out[...] = vn

    tile = pl.BlockSpec((tr, cols), lambda i: (i, 0))
    sh = jax.ShapeDtypeStruct((rows, cols), F32)
    return pl.pallas_call(
        body, name=name, grid=(rows // tr,),
        in_specs=[pl.BlockSpec((NDEV, tr, cols), lambda i: (0, i, 0)), tile, tile, tile],
        out_specs=[tile] * 4, out_shape=[sh] * 4,
        compiler_params=pltpu.CompilerParams(
            dimension_semantics=("parallel",), vmem_limit_bytes=VMEM_MID),
    )(slots, w, m, v)


SMALL_NAMES = ("g_mix", "b_dw", "g_conv_ln", "b_conv_ln", "g_ffn", "g_ple", "b_pgate", "g_final")


def _pack_small(vecs, w_dw_full, last=None):
    widen = lambda v: jnp.pad(v.reshape(1, -1), ((0, 0), (0, SMALL_W - v.size)))
    rows = [widen(v) for v in vecs]
    rows.append(jnp.pad(w_dw_full, ((0, 0), (0, SMALL_W - DC))))
    rows.append(jnp.zeros((SMALL_ROWS - len(vecs) - CW, SMALL_W), F32) if last is None else widen(last))
    return jnp.concatenate(rows, axis=0)


def kernel(x, p, g_mix, w_in, w_dw, b_dw, g_conv_ln, b_conv_ln, w_out, g_ffn, w_gate, w_up, w_down, g_ple, w_pgate, b_pgate, w_ple, g_final, loss_target, m_g_mix, m_w_in, m_w_dw, m_b_dw, m_g_conv_ln, m_b_conv_ln, m_w_out, m_g_ffn, m_w_gate, m_w_up, m_w_down, m_g_ple, m_w_pgate, m_b_pgate, m_w_ple, m_g_final, v_g_mix, v_w_in, v_w_dw, v_b_dw, v_g_conv_ln, v_b_conv_ln, v_w_out, v_g_ffn, v_w_gate, v_w_up, v_w_down, v_g_ple, v_w_pgate, v_b_pgate, v_w_ple, v_g_final):
    T = x.shape[1]
    me = 4 * lax.axis_index("x") + 2 * lax.axis_index("y") + lax.axis_index("c")
    xs = x.reshape(T, D)
    ps = p.reshape(T, DPLE).astype(BF16)
    tgt = loss_target.reshape(T, D)
    g_final2 = g_final.reshape(1, D)

    tr_names = ("w_gate", "w_up")
    big = dict(w_in=w_in[0], w_out=w_out[0], w_gate=w_gate[0].T, w_up=w_up[0].T, w_down=w_down[0],
               w_pgate=w_pgate[0], w_ple=w_ple[0])
    order = ("w_in", "w_out", "w_gate", "w_up", "w_down", "w_pgate", "w_ple")
    w_dw_g, w_in_s = _gather_two_level(
        "gather_first", [w_dw.reshape(CW, DC // NDEV), big["w_in"].astype(BF16)])
    w_dw_f = w_dw_g.transpose(1, 0, 2).reshape(CW, DC)
    later = order[1:]
    lands = _place("gather_place", [(big[n], False) for n in later], dtype=BF16)
    g_handles, g_token = _xstart("gather_start", [(None, False)] * len(later), lands, deps=[w_in_s])
    w_in_f = _shards_to_cols("w_in_natural", w_in_s)
    G = dict(zip(later, g_handles))

    a, z, *z_wide = _mm_in(xs, g_mix, w_in_f, deps=[g_token])
    zsrc = dict(zip(DILATIONS, [z] + z_wide))
    br = [_attn_fwd(f"attn_fwd_d{d}", zsrc[d], d, T) for d in DILATIONS]
    comb = list(_attn_combine([b[0] for b in br], [b[1] for b in br], T))
    o_attn, lse = comb[0], comb[1]
    osrc = dict(zip(DILATIONS, [o_attn] + comb[2:2 + len(WIDE)]))
    lsrc = dict(zip(DILATIONS, [lse] + comb[2 + len(WIDE):]))
    o_conv, y_conv = _conv_fwd(z, w_dw_f, b_dw, g_conv_ln, b_conv_ln)
    w_out_f = _xwait("gather_wait_w_out", G["w_out"], o_conv).reshape(D, D)
    h1, f = _mm_out(o_attn, o_conv, w_out_f, xs, g_ffn)
    w_gate_f = _xwait("gather_wait_w_gate", G["w_gate"], f).reshape(DFF, D)
    w_up_f = _xwait("gather_wait_w_up", G["w_up"], f).reshape(DFF, D)
    gate, up, act = _mm_gate_up(f, w_gate_f, w_up_f)
    w_down_f = _xwait("gather_wait_w_down", G["w_down"], act).reshape(DFF, D)
    h2, r = _mm_down(act, w_down_f, h1, g_ple)
    w_pgate_f = _xwait("gather_wait_w_pgate", G["w_pgate"], r).reshape(D, D)
    w_ple_f = _xwait("gather_wait_w_ple", G["w_ple"], r).transpose(1, 0, 2).reshape(DPLE, D)

    loss_part, dh3, dpe, dpg, d_g_final, d_b_pgate = _ple_loss(
        r, w_pgate_f, b_pgate, ps, w_ple_f, h2, tgt, g_final2)
    H = {}

    def send_grads(tag, named):
        items = [(v, True) for _, v in named]
        handles, token = _xstart(f"grads_start_{tag}", items, _place(f"grads_place_{tag}", items))
        H.update(zip([n for n, _ in named], handles))
        return token

    gw_pgate = _mm_tn("gw_pgate", r, dpg).reshape(NDEV, D // NDEV, D)
    gw_ple = _mm_tn("gw_ple", ps, dpe).reshape(DPLE, NDEV, D // NDEV).transpose(1, 0, 2)
    tok = send_grads("ple", [("w_pgate", gw_pgate), ("w_ple", gw_ple)])
    dh2b, d_g_ple = _mm_pgate_bwd(dpg, w_pgate_f, h2, g_ple, dh3, deps=[tok])
    ff_shards = lambda g: g.reshape(NDEV, N_FF, D)
    gw_down = ff_shards(_mm_tn_ff("gw_down", act, dh2b))
    tok = send_grads("down", [("w_down", gw_down)])
    dgate, dup = _mm_down_bwd(dh2b, w_down_f, gate, up, deps=[tok])
    gw_gate = ff_shards(_mm_tn_ff("gw_gate", dgate, f))
    gw_up = ff_shards(_mm_tn_ff("gw_up", dup, f))
    tok = send_grads("ffn", [("w_gate", gw_gate), ("w_up", gw_up)])
    df = _mm_ffn_in_bwd(dgate, w_gate_f, dup, w_up_f, deps=[tok])
    dh1b, d_g_ffn = _rms_bwd("rms_ffn_bwd", df, h1, g_ffn, dh2b)
    gw_out = jnp.concatenate(
        [_mm_tn("gw_out_attn", o_attn, dh1b), _mm_tn("gw_out_conv", o_conv, dh1b)], axis=0)
    tok = send_grads("out", [("w_out", gw_out.reshape(NDEV, D // NDEV, D))])
    dom, *do_wide = _mm_out_bwd(dh1b, w_out_f, deps=[tok])
    dosrc = dict(zip(DILATIONS, [dom] + do_wide))
    dcvg, d_w_dw, d_b_dw, d_g_ln, d_b_ln = _conv_bwd(z, dom, y_conv, w_dw_f, g_conv_ln, b_conv_ln)
    dqs, dks, dvs = [], [], []
    for d in DILATIONS:
        dqs.append(_attn_bwd_q(f"attn_bwd_q_d{d}", zsrc[d], dosrc[d], osrc[d], lsrc[d], d, T))
        dk, dv = _attn_bwd_kv(f"attn_bwd_kv_d{d}", zsrc[d], dosrc[d], osrc[d], lsrc[d], d, T)
        dks.append(dk)
        dvs.append(dv)
    dz = _dz_assemble(dqs, dks, dvs, dcvg, T)
    gw_in = _cols_to_shards("gw_in_shards", _mm_tn_wide("gw_in", a, dz), N_IN)
    tok = send_grads("in", [("w_in", gw_in)])
    grad_x, d_g_mix = _mm_in_bwd(dz, w_in_f, xs, g_mix, dh1b, deps=[tok])

    small_part = _pack_small(
        [d_g_mix, d_b_dw, d_g_ln, d_b_ln, d_g_ffn, d_g_ple, d_b_pgate, d_g_final], d_w_dw,
        last=loss_part)
    small_slots = _exchange("exchange_small_grads", [(small_part, False)])[0]
    S = {n: _xwait(f"grads_wait_{n}", H[n], small_slots)
         for n in ("w_pgate", "w_ple", "w_down", "w_gate", "w_up", "w_out", "w_in")}

    mom = dict(w_in=(m_w_in, v_w_in), w_out=(m_w_out, v_w_out), w_gate=(m_w_gate, v_w_gate),
               w_up=(m_w_up, v_w_up), w_down=(m_w_down, v_w_down), w_pgate=(m_w_pgate, v_w_pgate),
               w_ple=(m_w_ple, v_w_ple))
    upd = {}
    for n in order:
        m_n, v_n = mom[n][0][0], mom[n][1][0]
        if n in tr_names:
            res = _adam(f"adam_{n}", S[n], big[n], m_n.T, v_n.T)
            upd[n] = [t.T[None] for t in res]
        else:
            res = _adam(f"adam_{n}", S[n], big[n], m_n, v_n)
            upd[n] = [t[None] for t in res]

    def lanes(v):
        full = jnp.zeros((CW, NDEV, DC // NDEV), F32)
        full = lax.dynamic_update_slice(full, v.reshape(CW, 1, DC // NDEV), (0, me, 0))
        return full.reshape(CW, DC)

    small_w = _pack_small([g_mix, b_dw, g_conv_ln, b_conv_ln, g_ffn, g_ple, b_pgate, g_final2], lanes(w_dw))
    small_m = _pack_small([m_g_mix, m_b_dw, m_g_conv_ln, m_b_conv_ln, m_g_ffn, m_g_ple, m_b_pgate,
                           m_g_final.reshape(1, D)], lanes(m_w_dw))
    small_v = _pack_small([v_g_mix, v_b_dw, v_g_conv_ln, v_b_conv_ln, v_g_ffn, v_g_ple, v_b_pgate,
                           v_g_final.reshape(1, D)], lanes(v_w_dw))
    small_res = _adam("adam_small", small_slots, small_w, small_m, small_v)

    def unpack(t):
        out = {}
        widths = dict(g_mix=D, b_dw=DC, g_conv_ln=DC, b_conv_ln=DC, g_ffn=D, g_ple=D, b_pgate=D, g_final=D)
        for i, n in enumerate(SMALL_NAMES):
            out[n] = t[i:i + 1, :widths[n]]
        out["g_final"] = out["g_final"].reshape(D)
        taps = t[len(SMALL_NAMES):len(SMALL_NAMES) + CW, :DC].reshape(CW, NDEV, DC // NDEV)
        out["w_dw"] = lax.dynamic_slice(taps, (0, me, 0), (CW, 1, DC // NDEV))[None]
        return out

    small = [unpack(t) for t in small_res]

    loss = small_res[0][SMALL_ROWS - 1, 0]
    names = ("g_mix", "w_in", "w_dw", "b_dw", "g_conv_ln", "b_conv_ln", "w_out", "g_ffn", "w_gate",
             "w_up", "w_down", "g_ple", "w_pgate", "b_pgate", "w_ple", "g_final")
    outs = [loss, grad_x.reshape(1, T, D)]
    for kind in range(4):
        for n in names:
            outs.append(upd[n][kind] if n in upd else small[kind][n])
    return tuple(outs)
```

```python
import jax
import jax.numpy as jnp
from jax import lax
from jax.experimental import pallas as pl
from jax.experimental.pallas import tpu as pltpu

F32 = jnp.float32
BF16 = jnp.bfloat16

NDEV = 8
D = 2048
NH = 8
DH = 128
DA = NH * DH
DC = D - DA
DIN = 3 * DA + 2 * DC
DFF = 5632
DPLE = 256
BLK = 128
DILATIONS = (1, 4, 16)
CW = 31
EPS = 1e-6
N_IN = DIN // NDEV
N_FF = DFF // NDEV
NEG = -1e30

ADAM_LR = 0.001
ADAM_B1 = 0.9
ADAM_B2 = 0.999
ADAM_EPS = 1e-08
ADAM_WD = 0.01
ADAM_STEP = 10

VMEM_CAP_V7X = 64 * 1024 * 1024
VMEM_BIG = VMEM_CAP_V7X - 12 * 1024 * 1024
VMEM_MID = 40 * 1024 * 1024

SMALL_W = 2048
SMALL_ROWS = 40


def _sigmoid(v):
    return 1.0 / (1.0 + jnp.exp(-v))


def _dot(a, b, contract):
    return lax.dot_general(a, b, (contract, ((), ())), preferred_element_type=F32)


NN = ((1,), (0,))
NT = ((1,), (1,))
TN = ((0,), (0,))


def _exchange(name, items):
    n = len(items)
    out_shape = [
        jax.ShapeDtypeStruct((NDEV,) + (a.shape[1:] if sc else a.shape), a.dtype)
        for a, sc in items
    ]
    scat = [sc for _, sc in items]

    def body(*refs):
        srcs = refs[:n]
        dsts = refs[n:2 * n]
        send_sems, recv_sems, loc_sems = refs[2 * n:]
        x = lax.axis_index("x")
        y = lax.axis_index("y")
        c = lax.axis_index("c")
        me = 4 * x + 2 * y + c

        local = []
        for i in range(n):
            src = srcs[i].at[me] if scat[i] else srcs[i]
            cp = pltpu.make_async_copy(src, dsts[i].at[me], loc_sems.at[i])
            cp.start()
            local.append(cp)

        remote = []
        for k in range(1, NDEV):
            px = (1 - x) if (k >> 2) & 1 else x
            py = (1 - y) if (k >> 1) & 1 else y
            pc = (1 - c) if k & 1 else c
            peer = 4 * px + 2 * py + pc
            for i in range(n):
                sem = i * (NDEV - 1) + k - 1
                src = srcs[i].at[peer] if scat[i] else srcs[i]
                send = pltpu.make_async_remote_copy(
                    src_ref=src, dst_ref=dsts[i].at[me],
                    send_sem=send_sems.at[sem], recv_sem=recv_sems.at[sem],
                    device_id=(px, py, pc), device_id_type=pl.DeviceIdType.MESH)
                send.start()
                recv = pltpu.make_async_remote_copy(
                    src_ref=src, dst_ref=dsts[i].at[peer],
                    send_sem=send_sems.at[sem], recv_sem=recv_sems.at[sem],
                    device_id=(px, py, pc), device_id_type=pl.DeviceIdType.MESH)
                remote.append((send, recv))
        for send, recv in remote:
            recv.wait_recv()
            send.wait_send()
        for cp in local:
            cp.wait()

    any_spec = pl.BlockSpec(memory_space=pl.ANY)
    return pl.pallas_call(
        body, name=name,
        in_specs=[any_spec] * n, out_specs=[any_spec] * n, out_shape=out_shape,
        scratch_shapes=[
            pltpu.SemaphoreType.DMA((n * (NDEV - 1),)),
            pltpu.SemaphoreType.DMA((n * (NDEV - 1),)),
            pltpu.SemaphoreType.DMA((n,)),
        ],
    )(*[a for a, _ in items])


def _gather_two_level(name, arrays):
    n = len(arrays)
    per = NDEV - 1

    def body(*refs):
        srcs = refs[:n]
        dsts = refs[n:2 * n]
        send_sems, recv_sems, loc_sems = refs[2 * n:]
        x = lax.axis_index("x")
        y = lax.axis_index("y")
        c = lax.axis_index("c")
        idx = lambda px, py, pc: 4 * px + 2 * py + pc
        me, sibling = (x, y, c), (x, y, 1 - c)
        chips = [(1 - x, y), (x, 1 - y), (1 - x, 1 - y)]

        def copy(i, k, block, to, src=None):
            slot = dsts[i].at[idx(*block)]
            return pltpu.make_async_remote_copy(
                src_ref=slot if src is None else src, dst_ref=slot,
                send_sem=send_sems.at[i * per + k], recv_sem=recv_sems.at[i * per + k],
                device_id=to, device_id_type=pl.DeviceIdType.MESH)

        mine, sent = [], []
        for i in range(n):
            cp = pltpu.make_async_copy(srcs[i], dsts[i].at[idx(*me)], loc_sems.at[i])
            cp.start()
            mine.append(cp)
            first = [copy(i, 0, me, sibling, src=srcs[i])]
            first += [copy(i, 1 + j, me, (*chip, c), src=srcs[i]) for j, chip in enumerate(chips)]
            for cp in first:
                cp.start()
            sent += first
        for j, chip in enumerate(chips):
            for i in range(n):
                copy(i, 1 + j, (*chip, c), me).wait_recv()
                fwd = copy(i, 4 + j, (*chip, c), sibling)
                fwd.start()
                sent.append(fwd)
        for i in range(n):
            copy(i, 0, sibling, me).wait_recv()
            for j, chip in enumerate(chips):
                copy(i, 4 + j, (*chip, 1 - c), me).wait_recv()
        for cp in sent:
            cp.wait_send()
        for cp in mine:
            cp.wait()

    any_spec = pl.BlockSpec(memory_space=pl.ANY)
    return pl.pallas_call(
        body, name=name, in_specs=[any_spec] * n, out_specs=[any_spec] * n,
        out_shape=[jax.ShapeDtypeStruct((NDEV,) + a.shape, a.dtype) for a in arrays],
        scratch_shapes=[pltpu.SemaphoreType.DMA((n * per,)), pltpu.SemaphoreType.DMA((n * per,)),
                        pltpu.SemaphoreType.DMA((n,))],
    )(*arrays)


HBM_SPEC = pl.BlockSpec(memory_space=pltpu.HBM)
SEM_SPEC = pl.BlockSpec(memory_space=pltpu.SEMAPHORE)
ANY_SPEC = pl.BlockSpec(memory_space=pl.ANY)
EFFECT = pltpu.SideEffectType.DATAFLOW_SIDE_EFFECTING


def _peer_of(k):
    x = lax.axis_index("x")
    y = lax.axis_index("y")
    c = lax.axis_index("c")
    px = (1 - x) if (k >> 2) & 1 else x
    py = (1 - y) if (k >> 1) & 1 else y
    pc = (1 - c) if k & 1 else c
    return (px, py, pc), 4 * px + 2 * py + pc


def _my_index():
    return 4 * lax.axis_index("x") + 2 * lax.axis_index("y") + lax.axis_index("c")


def _slot_shape(a, sc):
    return (NDEV,) + (a.shape[1:] if sc else a.shape)


def _divisor_tile(rows):
    return next((t for t in (512, 256, 176, 128, 64, 32, 16) if rows % t == 0), rows)


def _place(name, items, dtype=None):
    lands = []
    for idx, (a, sc) in enumerate(items):
        rows, cols = a.shape[-2:]
        tr = _divisor_tile(rows)
        out_dtype = a.dtype if dtype is None else dtype

        def body(s_ref, o_ref):
            o_ref[...] = s_ref[...].astype(o_ref.dtype)

        mine = pl.BlockSpec((None, tr, cols), lambda i: (_my_index(), i, 0))
        lands.append(pl.pallas_call(
            body, name=f"{name}_{idx}", grid=(rows // tr,),
            in_specs=[mine if sc else pl.BlockSpec((tr, cols), lambda i: (i, 0))],
            out_specs=mine,
            out_shape=jax.ShapeDtypeStruct(_slot_shape(a, sc), out_dtype),
            compiler_params=pltpu.CompilerParams(dimension_semantics=("parallel",)),
        )(a))
    return lands


def _xstart(name, items, lands, deps=()):
    n = len(items)
    scat = [sc for _, sc in items]
    srcs_in = [a for a, sc in items if sc]
    n_src = len(srcs_in)
    src_pos = {i: p for p, i in enumerate(i for i in range(n) if scat[i])}

    def body(*refs):
        srcs = refs[:n_src]
        lzs = refs[n_src:n_src + n]
        outs = refs[n_src + n + len(deps):]
        send_sems, recv_sems, token = outs[:n], outs[n:2 * n], outs[-1]
        me = _my_index()
        for i in range(n):
            for k in range(1, NDEV):
                peer_id, peer = _peer_of(k)
                src = srcs[src_pos[i]].at[peer] if scat[i] else lzs[i].at[me]
                pltpu.make_async_remote_copy(
                    src_ref=src, dst_ref=lzs[i].at[me],
                    send_sem=send_sems[i].at[k - 1], recv_sem=recv_sems[i].at[k - 1],
                    device_id=peer_id, device_id_type=pl.DeviceIdType.MESH).start()
        token[...] = jnp.zeros_like(token)

    sem = pltpu.SemaphoreType.DMA((NDEV - 1,))
    thru = srcs_in + list(lands)
    res = pl.pallas_call(
        body, name=name,
        in_specs=[HBM_SPEC] * len(thru) + [ANY_SPEC] * len(deps),
        out_specs=[SEM_SPEC] * (2 * n) + [HBM_SPEC] * len(thru) + [pl.BlockSpec(memory_space=pltpu.VMEM)],
        out_shape=[sem] * (2 * n) + [pltpu.HBM(t.shape, t.dtype) for t in thru]
        + [jax.ShapeDtypeStruct((8, 128), F32)],
        input_output_aliases={i: 2 * n + i for i in range(len(thru))},
        compiler_params=pltpu.CompilerParams(has_side_effects=EFFECT),
    )(*[pltpu.with_memory_space_constraint(t, pltpu.HBM) for t in thru], *deps)
    handles = [(res[i], res[n + i], res[2 * n + src_pos[i]] if scat[i] else None,
                res[2 * n + n_src + i]) for i in range(n)]
    return handles, res[-1]


def _xwait(name, handle, after):
    send_sem, recv_sem, src, land = handle
    sc = src is not None

    def body(*refs):
        land_ref = refs[1] if sc else refs[0]
        send_ref, recv_ref = (refs[2], refs[3]) if sc else (refs[1], refs[2])
        me = _my_index()
        for k in range(1, NDEV):
            peer_id, peer = _peer_of(k)
            cp = pltpu.make_async_remote_copy(
                src_ref=refs[0].at[peer] if sc else land_ref.at[me], dst_ref=land_ref.at[peer],
                send_sem=send_ref.at[k - 1], recv_sem=recv_ref.at[k - 1],
                device_id=peer_id, device_id_type=pl.DeviceIdType.MESH)
            cp.wait_send()
            cp.wait_recv()

    thru = ([src] if sc else []) + [land]
    return pl.pallas_call(
        body, name=name,
        in_specs=[HBM_SPEC] * len(thru) + [SEM_SPEC, SEM_SPEC, ANY_SPEC],
        out_specs=[HBM_SPEC] * len(thru),
        out_shape=[pltpu.HBM(t.shape, t.dtype) for t in thru],
        input_output_aliases={i: i for i in range(len(thru))},
        compiler_params=pltpu.CompilerParams(has_side_effects=EFFECT),
    )(*thru, send_sem, recv_sem, after)[-1]


def _mm(name, grid, in_specs, operands, out_specs, out_shape, contract, n_pairs, epilogue,
        acc_shape=None, vmem=VMEM_BIG, deps=(), group=1, a_cols=None, carry=()):
    nk = grid[2]
    n_carry = len(carry)

    def shard(ref, s, is_a):
        if group == 1:
            return ref[...]
        if is_a and a_cols is not None:
            return ref[:, s * a_cols:(s + 1) * a_cols]
        return ref[s]
    n_extra = len(operands) - 2 * n_pairs
    n_out = len(out_shape)
    n_in = len(operands) + len(deps)
    in_specs = list(in_specs) + [ANY_SPEC] * len(deps)
    operands = list(operands) + list(deps)

    def body(*refs):
        ab = refs[:2 * n_pairs]
        extras = refs[2 * n_pairs:2 * n_pairs + n_extra]
        outs = refs[n_in:n_in + n_out]
        kept = refs[n_in + n_out:n_in + n_out + n_carry]
        finish = (lambda acc: epilogue(acc, extras, outs, kept)) if n_carry else (
            lambda acc: epilogue(acc, extras, outs))
        dots = [(ab[2 * p], ab[2 * p + 1], s) for p in range(n_pairs) for s in range(group)]
        if nk == 1:
            part = None
            for a_ref, b_ref, s in dots:
                d = _dot(shard(a_ref, s, True), shard(b_ref, s, False), contract)
                part = d if part is None else part + d
            finish(part)
        else:
            acc_ref = refs[-1]
            k = pl.program_id(2)

            @pl.when(k == 0)
            def _():
                acc_ref[...] = jnp.zeros_like(acc_ref)

            for a_ref, b_ref, s in dots:
                acc_ref[...] += _dot(shard(a_ref, s, True), shard(b_ref, s, False), contract)

            @pl.when(k == nk - 1)
            def _():
                finish(acc_ref[...])

    scratch = list(carry) + ([pltpu.VMEM(acc_shape, F32)] if nk > 1 else [])
    semantics = ("arbitrary",) * 3 if n_carry else ("parallel", "parallel", "arbitrary")
    return pl.pallas_call(
        body, name=name, grid=grid, in_specs=in_specs, out_specs=out_specs, out_shape=out_shape,
        scratch_shapes=scratch,
        compiler_params=pltpu.CompilerParams(dimension_semantics=semantics, vmem_limit_bytes=vmem),
    )(*operands)


def _ep_cast(dtype):
    def ep(acc, extras, outs):
        outs[0][...] = acc.astype(dtype)
    return ep


def _ep_rms_bwd(n_rows_steps):
    def ep(acc, extras, outs, kept):
        h_ref, g_ref, dres_ref = extras
        gacc = kept[0]
        i = pl.program_id(0)
        v = h_ref[...]
        r = lax.rsqrt(jnp.mean(v * v, axis=-1, keepdims=True) + EPS)
        nrm = v * r
        dn = acc * g_ref[...]
        dh = dres_ref[...].astype(F32) + r * (dn - nrm * jnp.mean(dn * nrm, axis=-1, keepdims=True))
        outs[0][...] = dh.astype(outs[0].dtype)

        @pl.when(i == 0)
        def _():
            gacc[...] = jnp.zeros_like(gacc)

        gacc[...] += _fold8(acc * nrm)

        @pl.when(i == n_rows_steps - 1)
        def _():
            outs[-1][...] = jnp.sum(gacc[...], axis=0, keepdims=True)
    return ep


def _ep_resid_norm(acc, extras, outs):
    h = extras[0][...] + acc
    outs[0][...] = h
    r = lax.rsqrt(jnp.mean(h * h, axis=-1, keepdims=True) + EPS)
    outs[1][...] = (h * r * extras[1][...]).astype(BF16)


def _ep_swiglu_bwd(acc, extras, outs):
    outs[0][...] = (acc * extras[0][...].astype(F32)).astype(BF16)
    outs[1][...] = (acc * extras[1][...].astype(F32)).astype(BF16)


MXU_COLS_V7X = 256


def _col_chunks(n):
    return [slice(c, min(c + MXU_COLS_V7X, n)) for c in range(0, n, MXU_COLS_V7X)]


def _row_tile(T):
    return min(1024, T)


def _tn_rows(T):
    return min(2048, T)


WIDE = tuple(d for d in DILATIONS if d > 1)


LANES = 128


def _lane_tile(c):
    return slice(c * LANES, (c + 1) * LANES)


def _to_lane_tiles(scr, val):
    for c in range(scr.shape[0]):
        scr[c] = val[:, _lane_tile(c)]


def _emit_class_major(scr, refs, rows):
    for d, ref in zip(WIDE, refs):
        for r in range(d):
            for c in range(scr.shape[0]):
                ref[r, :, _lane_tile(c)] = scr[c, pl.ds(r, rows // d, stride=d), :].astype(ref.dtype)


def _shards_to_cols(name, w):
    _, rows, n = w.shape
    tr = rows

    def body(s_ref, o_ref):
        o_ref[...] = s_ref[...]

    return pl.pallas_call(
        body, name=name, grid=(rows // tr, NDEV),
        in_specs=[pl.BlockSpec((None, tr, n), lambda i, j: (j, i, 0))],
        out_specs=pl.BlockSpec((tr, n), lambda i, j: (i, j)),
        out_shape=jax.ShapeDtypeStruct((rows, NDEV * n), w.dtype),
        compiler_params=pltpu.CompilerParams(dimension_semantics=("parallel", "parallel")),
    )(w)


def _cols_to_shards(name, g, n):
    rows = g.shape[0]
    tr = rows

    def body(s_ref, o_ref):
        o_ref[...] = s_ref[...]

    return pl.pallas_call(
        body, name=name, grid=(rows // tr, NDEV),
        in_specs=[pl.BlockSpec((tr, n), lambda i, j: (i, j))],
        out_specs=pl.BlockSpec((None, tr, n), lambda i, j: (j, i, 0)),
        out_shape=jax.ShapeDtypeStruct((NDEV, rows, n), g.dtype),
        compiler_params=pltpu.CompilerParams(dimension_semantics=("parallel", "parallel")),
    )(g)


IN_TN = DA


def _mm_in(x, g, w_in, deps=()):
    T = x.shape[0]
    tm = min(TM_FULL_ROW, T)
    nq = 3

    def body(x_ref, g_ref, w_ref, *rest):
        a_ref, z_ref, *rest = rest[len(deps):]
        scr = rest[-1]
        j = pl.program_id(1)

        @pl.when(j == 0)
        def _():
            v = x_ref[...]
            r = lax.rsqrt(jnp.mean(v * v, axis=-1, keepdims=True) + EPS)
            a_ref[...] = (v * r * g_ref[...]).astype(BF16)

        @pl.when(j >= nq)
        def _():
            z_ref[...] = _dot(a_ref[...], w_ref[...], NN).astype(BF16)

        @pl.when(j < nq)
        def _():
            av = a_ref[...]
            chunks = _col_chunks(IN_TN)
            pending = _dot(av, w_ref[:, chunks[0]], NN)
            for ci, cols in enumerate(chunks):
                nxt = _dot(av, w_ref[:, chunks[ci + 1]], NN) if ci + 1 < len(chunks) else None
                z_ref[:, cols] = pending.astype(BF16)
                for c in range(cols.start // LANES, cols.stop // LANES):
                    scr[c] = pending[:, c * LANES - cols.start:(c + 1) * LANES - cols.start]
                    for d, ref in zip(WIDE, rest[:-1]):
                        for r in range(d):
                            ref[r, :, _lane_tile(c)] = scr[c, pl.ds(r, tm // d, stride=d), :].astype(BF16)
                pending = nxt

    cm_spec = lambda d: pl.BlockSpec((d, tm // d, IN_TN), lambda i, j: (0, i, jnp.minimum(j, nq - 1)))
    row = pl.BlockSpec((tm, D), lambda i, j: (i, 0))
    return pl.pallas_call(
        body, name="mm_in", grid=(T // tm, DIN // IN_TN),
        in_specs=[row, pl.BlockSpec((1, D), lambda i, j: (0, 0)),
                  pl.BlockSpec((D, IN_TN), lambda i, j: (0, j))] + [ANY_SPEC] * len(deps),
        out_specs=[row, pl.BlockSpec((tm, IN_TN), lambda i, j: (i, j))] + [cm_spec(d) for d in WIDE],
        out_shape=[jax.ShapeDtypeStruct((T, D), BF16), jax.ShapeDtypeStruct((T, DIN), BF16)]
        + [jax.ShapeDtypeStruct((d, T // d, nq * IN_TN), BF16) for d in WIDE],
        scratch_shapes=[pltpu.VMEM((IN_TN // LANES, tm, LANES), F32)],
        compiler_params=pltpu.CompilerParams(
            dimension_semantics=("parallel", "arbitrary"), vmem_limit_bytes=VMEM_BIG),
    )(x, g, w_in, *deps)


def _mm_out_bwd(dh1b, w_out, deps=()):
    T = dh1b.shape[0]
    tm = _row_tile(T)

    def body(dy_ref, w_ref, *rest):
        rest = rest[len(deps):]
        dom_ref, scr = rest[0], rest[-1]
        acc = _dot(dy_ref[...], w_ref[...], NT)
        dom_ref[...] = acc.astype(BF16)

        @pl.when(pl.program_id(1) == 0)
        def _():
            _to_lane_tiles(scr, acc)
            _emit_class_major(scr, rest[1:-1], tm)

    return pl.pallas_call(
        body, name="mm_out_bwd", grid=(T // tm, D // DA),
        in_specs=[pl.BlockSpec((tm, D), lambda i, j: (i, 0)),
                  pl.BlockSpec((DA, D), lambda i, j: (j, 0))] + [ANY_SPEC] * len(deps),
        out_specs=[pl.BlockSpec((tm, DA), lambda i, j: (i, j))]
        + [pl.BlockSpec((d, tm // d, DA), lambda i, j: (0, i, 0)) for d in WIDE],
        out_shape=[jax.ShapeDtypeStruct((T, D), BF16)]
        + [jax.ShapeDtypeStruct((d, T // d, DA), BF16) for d in WIDE],
        scratch_shapes=[pltpu.VMEM((DA // LANES, tm, LANES), F32)],
        compiler_params=pltpu.CompilerParams(
            dimension_semantics=("parallel", "arbitrary"), vmem_limit_bytes=VMEM_BIG),
    )(dh1b, w_out, *deps)


TM_FULL_ROW = 512


def _full_row_specs(tm):
    row = pl.BlockSpec((tm, D), lambda i, j, k: (i, 0))
    return row, pl.BlockSpec((1, D), lambda i, j, k: (0, 0))


def _mm_out(o_attn, o_conv, w_out, x, g_next):
    T = x.shape[0]
    tm = min(TM_FULL_ROW, T)
    row, vec = _full_row_specs(tm)
    return _mm(
        "mm_out", (T // tm, 1, 1),
        [pl.BlockSpec((tm, DA), lambda i, j, k: (i, 0)),
         pl.BlockSpec((DA, D), lambda i, j, k: (0, 0)),
         pl.BlockSpec((tm, DC), lambda i, j, k: (i, 0)),
         pl.BlockSpec((DC, D), lambda i, j, k: (1, 0)),
         row, vec],
        [o_attn, w_out, o_conv, w_out, x, g_next],
        [row, row],
        [jax.ShapeDtypeStruct((T, D), F32), jax.ShapeDtypeStruct((T, D), BF16)], NN, 2,
        _ep_resid_norm)


FF_TN = 512
FF_TK = 2 * N_FF


def _mm_gate_up(f, wg_t, wu_t):
    T = f.shape[0]
    tm = _row_tile(T)

    def body(f_ref, wg_ref, wu_ref, dg_ref, du_ref, a_ref):
        fv = f_ref[...]
        g = _dot(fv, wg_ref[...], NT)
        u = _dot(fv, wu_ref[...], NT)
        sg = _sigmoid(g)
        silu = g * sg
        dg_ref[...] = (u * (sg * (1.0 + g * (1.0 - sg)))).astype(BF16)
        du_ref[...] = silu.astype(BF16)
        a_ref[...] = (silu * u).astype(BF16)

    wspec = pl.BlockSpec((FF_TN, D), lambda i, j: (j, 0))
    ospec = pl.BlockSpec((tm, FF_TN), lambda i, j: (i, j))
    sh = jax.ShapeDtypeStruct((T, DFF), BF16)
    return pl.pallas_call(
        body, name="mm_gate_up", grid=(T // tm, DFF // FF_TN),
        in_specs=[pl.BlockSpec((tm, D), lambda i, j: (i, 0)), wspec, wspec],
        out_specs=[ospec, ospec, ospec], out_shape=[sh, sh, sh],
        compiler_params=pltpu.CompilerParams(
            dimension_semantics=("parallel", "parallel"), vmem_limit_bytes=VMEM_BIG),
    )(f, wg_t, wu_t)


def _mm_down(act, w_down, h1, g_next):
    T = h1.shape[0]
    tm = min(TM_FULL_ROW, T)
    row, vec = _full_row_specs(tm)
    return _mm(
        "mm_down", (T // tm, 1, DFF // FF_TK),
        [pl.BlockSpec((tm, FF_TK), lambda i, j, k: (i, k)),
         pl.BlockSpec((FF_TK, D), lambda i, j, k: (k, 0)),
         row, vec],
        [act, w_down, h1, g_next],
        [row, row],
        [jax.ShapeDtypeStruct((T, D), F32), jax.ShapeDtypeStruct((T, D), BF16)], NN, 1,
        _ep_resid_norm, acc_shape=(tm, D))


def _ple_loss(r, w_pgate, b_pgate, p, w_ple, h2, target, g_final):
    T = h2.shape[0]
    tm = min(256, T)
    nt = T // tm

    def body(r_ref, wg_ref, b_ref, p_ref, wp_ref, h2_ref, t_ref, g_ref,
             loss_ref, dh_ref, dpe_ref, dpg_ref, dgf_ref, dbp_ref, lacc, gacc, bacc):
        i = pl.program_id(0)
        gte = _sigmoid(_dot(r_ref[...], wg_ref[...], NN) + b_ref[...])
        pe = _dot(p_ref[...], wp_ref[...], NN)
        v = h2_ref[...] + pe * gte
        rr = lax.rsqrt(jnp.mean(v * v, axis=-1, keepdims=True) + EPS)
        nrm = v * rr
        g = g_ref[...]
        err = nrm * g - t_ref[...]
        dy = err * (1.0 / D)
        dn = dy * g
        dh = rr * (dn - nrm * jnp.mean(dn * nrm, axis=-1, keepdims=True))
        dh_ref[...] = dh.astype(BF16)
        dpe_ref[...] = (dh * gte).astype(BF16)
        dpg = dh * pe * gte * (1.0 - gte)
        dpg_ref[...] = dpg.astype(BF16)

        @pl.when(i == 0)
        def _():
            lacc[...] = jnp.zeros_like(lacc)
            gacc[...] = jnp.zeros_like(gacc)
            bacc[...] = jnp.zeros_like(bacc)

        lacc[...] += _fold8(err * err)
        gacc[...] += _fold8(dy * nrm)
        bacc[...] += _fold8(dpg)

        @pl.when(i == nt - 1)
        def _():
            tot = jnp.sum(jnp.sum(lacc[...], axis=0, keepdims=True), axis=1, keepdims=True)
            loss_ref[...] = jnp.broadcast_to(tot * (0.5 / D), (1, 128))
            dgf_ref[...] = jnp.sum(gacc[...], axis=0, keepdims=True)
            dbp_ref[...] = jnp.sum(bacc[...], axis=0, keepdims=True)

    row = pl.BlockSpec((tm, D), lambda i: (i, 0))
    vec = pl.BlockSpec((1, D), lambda i: (0, 0))
    return pl.pallas_call(
        body, name="ple_loss", grid=(nt,),
        in_specs=[row, pl.BlockSpec((D, D), lambda i: (0, 0)), vec,
                  pl.BlockSpec((tm, DPLE), lambda i: (i, 0)),
                  pl.BlockSpec((DPLE, D), lambda i: (0, 0)), row, row, vec],
        out_specs=[pl.BlockSpec((1, 128), lambda i: (0, 0)), row, row, row, vec, vec],
        out_shape=[jax.ShapeDtypeStruct((1, 128), F32), jax.ShapeDtypeStruct((T, D), BF16),
                   jax.ShapeDtypeStruct((T, D), BF16), jax.ShapeDtypeStruct((T, D), BF16),
                   jax.ShapeDtypeStruct((1, D), F32), jax.ShapeDtypeStruct((1, D), F32)],
        scratch_shapes=[pltpu.VMEM((8, D), F32)] * 3,
        compiler_params=pltpu.CompilerParams(
            dimension_semantics=("arbitrary",), vmem_limit_bytes=VMEM_BIG),
    )(r, w_pgate, b_pgate, p, w_ple, h2, target, g_final)


def _mm_down_bwd(dh2, w_down, g, u, deps=()):
    T = dh2.shape[0]
    tm = _tn_rows(T)
    gspec = pl.BlockSpec((tm, FF_TN), lambda i, j, k: (i, j))
    sh = jax.ShapeDtypeStruct((T, DFF), BF16)
    return _mm(
        "mm_down_bwd", (T // tm, DFF // FF_TN, 1),
        [pl.BlockSpec((tm, D), lambda i, j, k: (i, 0)),
         pl.BlockSpec((FF_TN, D), lambda i, j, k: (j, 0)),
         gspec, gspec],
        [dh2, w_down, g, u],
        [gspec, gspec], [sh, sh], NT, 1, _ep_swiglu_bwd, deps=deps)


def _mm_ffn_in_bwd(dg, wg_t, du, wu_t, deps=()):
    T = dg.shape[0]
    tm = _row_tile(T)
    tn = 1024
    aspec = pl.BlockSpec((tm, FF_TK), lambda i, j, k: (i, k))
    wspec = pl.BlockSpec((FF_TK, tn), lambda i, j, k: (k, j))
    return _mm(
        "mm_ffn_in_bwd", (T // tm, D // tn, DFF // FF_TK),
        [aspec, wspec, aspec, wspec], [dg, wg_t, du, wu_t],
        [pl.BlockSpec((tm, tn), lambda i, j, k: (i, j))],
        [jax.ShapeDtypeStruct((T, D), BF16)], NN, 2, _ep_cast(BF16), acc_shape=(tm, tn),
        deps=deps)[0]


def _mm_in_bwd(dz, w_in, x, g, dres, deps=()):
    T = dz.shape[0]
    tm = min(TM_FULL_ROW, T)
    tk = DIN // 4
    row, vec = _full_row_specs(tm)
    return _mm(
        "mm_in_bwd", (T // tm, 1, DIN // tk),
        [pl.BlockSpec((tm, tk), lambda i, j, k: (i, k)),
         pl.BlockSpec((D, tk), lambda i, j, k: (0, k)),
         row, vec, row],
        [dz, w_in, x, g, dres],
        [row, vec],
        [jax.ShapeDtypeStruct((T, D), F32), jax.ShapeDtypeStruct((1, D), F32)], NT, 1,
        _ep_rms_bwd(T // tm), acc_shape=(tm, D), deps=deps,
        carry=[pltpu.VMEM((8, D), F32)])


def _mm_pgate_bwd(dpg, w_pgate, h2, g, dres, deps=()):
    T = dpg.shape[0]
    tm = min(256, T)
    row, vec = _full_row_specs(tm)
    return _mm(
        "mm_pgate_bwd", (T // tm, 1, 1),
        [row, pl.BlockSpec((D, D), lambda i, j, k: (0, 0)), row, vec, row],
        [dpg, w_pgate, h2, g, dres],
        [row, vec],
        [jax.ShapeDtypeStruct((T, D), BF16), jax.ShapeDtypeStruct((1, D), F32)], NT, 1,
        _ep_rms_bwd(T // tm), deps=deps, carry=[pltpu.VMEM((8, D), F32)])


def _mm_tn(name, a, b, tj=None):
    T, idim = a.shape
    jdim = b.shape[1]
    tt = _row_tile(T)
    ti = min(idim, 1024)
    tj = jdim if tj is None else tj
    return _mm(
        name, (idim // ti, jdim // tj, T // tt),
        [pl.BlockSpec((tt, ti), lambda i, j, k: (k, i)),
         pl.BlockSpec((tt, tj), lambda i, j, k: (k, j))],
        [a, b],
        [pl.BlockSpec((ti, tj), lambda i, j, k: (i, j))],
        [jax.ShapeDtypeStruct((idim, jdim), BF16)], TN, 1, _ep_cast(BF16), acc_shape=(ti, tj))[0]


def _mm_tn_wide(name, a, b):
    T = a.shape[0]
    jdim = b.shape[1]
    tt = _row_tile(T)
    return _mm(
        name, (1, jdim // IN_TN, T // tt),
        [pl.BlockSpec((tt, D), lambda i, j, k: (k, 0)),
         pl.BlockSpec((tt, IN_TN), lambda i, j, k: (k, j))],
        [a, b],
        [pl.BlockSpec((D, IN_TN), lambda i, j, k: (0, j))],
        [jax.ShapeDtypeStruct((D, jdim), BF16)], TN, 1, _ep_cast(BF16),
        acc_shape=(D, IN_TN))[0]


def _mm_tn_ff(name, a, b):
    T = b.shape[0]
    tt = _tn_rows(T)
    return _mm(
        name, (DFF // FF_TN, 1, T // tt),
        [pl.BlockSpec((tt, FF_TN), lambda i, j, k: (k, i)),
         pl.BlockSpec((tt, D), lambda i, j, k: (k, 0))],
        [a, b],
        [pl.BlockSpec((FF_TN, D), lambda i, j, k: (i, 0))],
        [jax.ShapeDtypeStruct((DFF, D), BF16)], TN, 1, _ep_cast(BF16),
        acc_shape=(FF_TN, D))[0]


TR = 256


def _rows(T):
    return min(TR, T)


def _fold8(v):
    return jnp.sum(v.reshape(v.shape[0] // 8, 8, v.shape[1]), axis=0)


def _rms_bwd(name, dn_out, h, g, dres):
    T = h.shape[0]
    tr = _rows(T)
    nt = T // tr

    def body(dy_ref, h_ref, g_ref, dres_ref, dh_ref, dg_ref, acc):
        i = pl.program_id(0)
        v = h_ref[...]
        r = lax.rsqrt(jnp.mean(v * v, axis=-1, keepdims=True) + EPS)
        nrm = v * r
        dy = dy_ref[...].astype(F32)
        dn = dy * g_ref[...]
        dh = dres_ref[...].astype(F32) + r * (dn - nrm * jnp.mean(dn * nrm, axis=-1, keepdims=True))
        dh_ref[...] = dh.astype(BF16)

        @pl.when(i == 0)
        def _():
            acc[...] = jnp.zeros_like(acc)

        acc[...] += _fold8(dy * nrm)

        @pl.when(i == nt - 1)
        def _():
            dg_ref[...] = jnp.sum(acc[...], axis=0, keepdims=True)

    tile = pl.BlockSpec((tr, D), lambda i: (i, 0))
    vec = pl.BlockSpec((1, D), lambda i: (0, 0))
    return pl.pallas_call(
        body, name=name, grid=(nt,),
        in_specs=[tile, tile, vec, tile], out_specs=[tile, vec],
        out_shape=[jax.ShapeDtypeStruct((T, D), BF16), jax.ShapeDtypeStruct((1, D), F32)],
        scratch_shapes=[pltpu.VMEM((8, D), F32)],
        compiler_params=pltpu.CompilerParams(dimension_semantics=("arbitrary",)),
    )(dn_out, h, g, dres)


def _band_masks():
    qi = lax.broadcasted_iota(jnp.int32, (BLK, BLK), 0)
    kj = lax.broadcasted_iota(jnp.int32, (BLK, BLK), 1)
    return kj >= qi, kj <= qi


AQ = 4
ATTN_PARAMS = pltpu.CompilerParams(
    dimension_semantics=("parallel", "parallel"), vmem_limit_bytes=VMEM_BIG)


def _cm_spec(d, col, nblk, rowmap=lambda n: n):
    if d == 1:
        return pl.BlockSpec((nblk * BLK, DA), lambda r, n: (rowmap(n), col))
    return pl.BlockSpec((None, nblk * BLK, DA), lambda r, n: (r, rowmap(n), col))


def _cm_shape(d, T, dtype):
    return jax.ShapeDtypeStruct((T, DA) if d == 1 else (d, T // d, DA), dtype)


def _blk(b):
    return slice(b * BLK, (b + 1) * BLK)


HEADS = tuple(slice(h * DH, (h + 1) * DH) for h in range(NH))


def _attn_fwd(name, zsrc, d, T):
    nb = T // d // BLK
    aq = min(AQ, nb)
    scale = DH ** -0.5

    def body(q_ref, kp_ref, kc_ref, vp_ref, vc_ref, o_ref, l_ref):
        n = pl.program_id(1)
        band_prev, cur_ok = _band_masks()
        for b in range(aq):
            kp = (lambda sl: kp_ref[:, sl]) if b == 0 else (lambda sl, b=b: kc_ref[_blk(b - 1), sl])
            vp = (lambda sl: vp_ref[:, sl]) if b == 0 else (lambda sl, b=b: vc_ref[_blk(b - 1), sl])
            prev_ok = band_prev & (n > 0) if b == 0 else band_prev
            rows = _blk(b)
            s = [(jnp.where(prev_ok, _dot(q_ref[rows, sl], kp(sl), NT) * scale, NEG),
                  jnp.where(cur_ok, _dot(q_ref[rows, sl], kc_ref[rows, sl], NT) * scale, NEG))
                 for sl in HEADS]
            m = [jnp.maximum(jnp.max(sp, axis=1, keepdims=True), jnp.max(sc, axis=1, keepdims=True))
                 for sp, sc in s]
            p = [(jnp.exp(sp - mh), jnp.exp(sc - mh)) for (sp, sc), mh in zip(s, m)]
            den = [jnp.sum(pp, axis=1, keepdims=True) + jnp.sum(pc, axis=1, keepdims=True)
                   for pp, pc in p]
            o = [_dot(pp.astype(BF16), vp(sl), NN) + _dot(pc.astype(BF16), vc_ref[rows, sl], NN)
                 for (pp, pc), sl in zip(p, HEADS)]
            o_ref[rows, :] = jnp.concatenate(
                [(oh / dh).astype(BF16) for oh, dh in zip(o, den)], axis=1)
            l_ref[rows, :] = jnp.concatenate(
                [jnp.broadcast_to(mh + jnp.log(dh), (BLK, DH)) for mh, dh in zip(m, den)], axis=1)

    halo = lambda n: jnp.maximum(aq * n - 1, 0)
    return pl.pallas_call(
        body, name=name, grid=(d, nb // aq),
        in_specs=[_cm_spec(d, 0, aq), _cm_spec(d, 1, 1, halo), _cm_spec(d, 1, aq),
                  _cm_spec(d, 2, 1, halo), _cm_spec(d, 2, aq)],
        out_specs=[_cm_spec(d, 0, aq)] * 2,
        out_shape=[_cm_shape(d, T, BF16), _cm_shape(d, T, F32)],
        compiler_params=ATTN_PARAMS,
    )(zsrc, zsrc, zsrc, zsrc, zsrc)


def _cm_tile(d, tr):
    if d == 1:
        return pl.BlockSpec((tr, DA), lambda i: (i, 0))
    return pl.BlockSpec((d, tr // d, DA), lambda i: (0, i, 0))


def _attn_combine(outs, lses, T):
    tr = _rows(T)

    def body(*refs):
        o_in, l_in = refs[:3], refs[3:6]
        o_ref, l_ref = refs[6:8]
        o_cm, l_cm = refs[8:8 + len(WIDE)], refs[8 + len(WIDE):8 + 2 * len(WIDE)]
        so, sl, so_all, sl_all = refs[8 + 2 * len(WIDE):]
        for c in range(DA // LANES):
            lt = _lane_tile(c)
            os_, ls_ = [o_in[0][:, lt].astype(F32)], [l_in[0][:, lt]]
            for w, d in enumerate(WIDE):
                for r in range(d):
                    so[w, c, pl.ds(r, tr // d, stride=d), :] = o_in[1 + w][r, :, lt].astype(F32)
                    sl[w, c, pl.ds(r, tr // d, stride=d), :] = l_in[1 + w][r, :, lt]
                os_.append(so[w, c])
                ls_.append(sl[w, c])
            la, lb, lc = ls_
            m = jnp.maximum(jnp.maximum(la, lb), lc)
            ea, eb, ec = jnp.exp(la - m), jnp.exp(lb - m), jnp.exp(lc - m)
            s = ea + eb + ec
            o = (ea * os_[0] + eb * os_[1] + ec * os_[2]) / s
            lse = m + jnp.log(s)
            o_ref[:, lt] = o.astype(BF16)
            l_ref[:, lt] = lse
            so_all[c] = o
            sl_all[c] = lse
        _emit_class_major(so_all, o_cm, tr)
        _emit_class_major(sl_all, l_cm, tr)

    specs = [_cm_tile(d, tr) for d in DILATIONS]
    wide = [_cm_tile(d, tr) for d in WIDE]
    return pl.pallas_call(
        body, name="attn_combine", grid=(T // tr,),
        in_specs=specs + specs,
        out_specs=[specs[0], specs[0]] + wide + wide,
        out_shape=[_cm_shape(1, T, BF16), _cm_shape(1, T, F32)]
        + [_cm_shape(d, T, BF16) for d in WIDE] + [_cm_shape(d, T, F32) for d in WIDE],
        scratch_shapes=[pltpu.VMEM((len(WIDE), DA // LANES, tr, LANES), F32)] * 2
        + [pltpu.VMEM((DA // LANES, tr, LANES), F32)] * 2,
        compiler_params=pltpu.CompilerParams(
            dimension_semantics=("parallel",), vmem_limit_bytes=VMEM_MID),
    )(*outs, *lses)


def _attn_bwd_q(name, zsrc, dosrc, osrc, lsrc, d, T):
    nb = T // d // BLK
    aq = min(AQ, nb)
    scale = DH ** -0.5

    def body(q_ref, kp_ref, kc_ref, vp_ref, vc_ref, do_ref, o_ref, l_ref, dq_ref):
        n = pl.program_id(1)
        band_prev, cur_ok = _band_masks()
        for b in range(aq):
            kp = (lambda sl: kp_ref[:, sl]) if b == 0 else (lambda sl, b=b: kc_ref[_blk(b - 1), sl])
            vp = (lambda sl: vp_ref[:, sl]) if b == 0 else (lambda sl, b=b: vc_ref[_blk(b - 1), sl])
            prev_ok = band_prev & (n > 0) if b == 0 else band_prev
            rows = _blk(b)
            s = [(_dot(q_ref[rows, sl], kp(sl), NT), _dot(q_ref[rows, sl], kc_ref[rows, sl], NT))
                 for sl in HEADS]
            dp = [(_dot(do_ref[rows, sl], vp(sl), NT), _dot(do_ref[rows, sl], vc_ref[rows, sl], NT))
                  for sl in HEADS]
            delta = [jnp.sum(do_ref[rows, sl].astype(F32) * o_ref[rows, sl].astype(F32), axis=1,
                             keepdims=True) for sl in HEADS]
            p = [(jnp.exp(jnp.where(prev_ok, sp * scale - l_ref[rows, sl], NEG)),
                  jnp.exp(jnp.where(cur_ok, sc * scale - l_ref[rows, sl], NEG)))
                 for (sp, sc), sl in zip(s, HEADS)]
            ds = [((pp * (dpp - dl) * scale).astype(BF16), (pc * (dpc - dl) * scale).astype(BF16))
                  for (pp, pc), (dpp, dpc), dl in zip(p, dp, delta)]
            dq = [_dot(dsp, kp(sl), NN) + _dot(dsc, kc_ref[rows, sl], NN)
                  for (dsp, dsc), sl in zip(ds, HEADS)]
            dq_ref[rows, :] = jnp.concatenate([v.astype(BF16) for v in dq], axis=1)

    halo = lambda n: jnp.maximum(aq * n - 1, 0)
    own = _cm_spec(d, 0, aq)
    return pl.pallas_call(
        body, name=name, grid=(d, nb // aq),
        in_specs=[own, _cm_spec(d, 1, 1, halo), _cm_spec(d, 1, aq), _cm_spec(d, 2, 1, halo),
                  _cm_spec(d, 2, aq), own, own, own],
        out_specs=own, out_shape=_cm_shape(d, T, BF16),
        compiler_params=ATTN_PARAMS,
    )(zsrc, zsrc, zsrc, zsrc, zsrc, dosrc, osrc, lsrc)


def _attn_bwd_kv(name, zsrc, dosrc, osrc, lsrc, d, T):
    nb = T // d // BLK
    aq = min(AQ, nb)
    nsteps = nb // aq
    scale = DH ** -0.5

    def body(k_ref, v_ref, q_ref, qn_ref, do_ref, don_ref, o_ref, on_ref, l_ref, ln_ref,
             dk_ref, dv_ref):
        j = pl.program_id(1)
        band_next, own_ok = _band_masks()
        for b in range(aq):
            rows = _blk(b)
            last = b == aq - 1
            pick = lambda cur, halo: ((lambda sl: halo[:, sl]) if last
                                      else (lambda sl, b=b: cur[_blk(b + 1), sl]))
            qb, dob, ob, lb = (pick(q_ref, qn_ref), pick(do_ref, don_ref), pick(o_ref, on_ref),
                               pick(l_ref, ln_ref))
            next_ok = band_next & (j < nsteps - 1) if last else band_next
            s = [(_dot(q_ref[rows, sl], k_ref[rows, sl], NT), _dot(qb(sl), k_ref[rows, sl], NT))
                 for sl in HEADS]
            dp = [(_dot(do_ref[rows, sl], v_ref[rows, sl], NT), _dot(dob(sl), v_ref[rows, sl], NT))
                  for sl in HEADS]
            delta = [(jnp.sum(do_ref[rows, sl].astype(F32) * o_ref[rows, sl].astype(F32), axis=1,
                              keepdims=True),
                      jnp.sum(dob(sl).astype(F32) * ob(sl).astype(F32), axis=1, keepdims=True))
                     for sl in HEADS]
            p = [(jnp.exp(jnp.where(own_ok, sa * scale - l_ref[rows, sl], NEG)),
                  jnp.exp(jnp.where(next_ok, sb * scale - lb(sl), NEG)))
                 for (sa, sb), sl in zip(s, HEADS)]
            dv = [_dot(pa.astype(BF16), do_ref[rows, sl], TN) + _dot(pb.astype(BF16), dob(sl), TN)
                  for (pa, pb), sl in zip(p, HEADS)]
            ds = [((pa * (dpa - da) * scale).astype(BF16), (pb * (dpb - db) * scale).astype(BF16))
                  for (pa, pb), (dpa, dpb), (da, db) in zip(p, dp, delta)]
            dk = [_dot(dsa, q_ref[rows, sl], TN) + _dot(dsb, qb(sl), TN)
                  for (dsa, dsb), sl in zip(ds, HEADS)]
            dk_ref[rows, :] = jnp.concatenate([v.astype(BF16) for v in dk], axis=1)
            dv_ref[rows, :] = jnp.concatenate([v.astype(BF16) for v in dv], axis=1)

    halo = lambda j: jnp.minimum(aq * (j + 1), nb - 1)
    own, own_n = _cm_spec(d, 0, aq), _cm_spec(d, 0, 1, halo)
    sh = _cm_shape(d, T, BF16)
    return pl.pallas_call(
        body, name=name, grid=(d, nsteps),
        in_specs=[_cm_spec(d, 1, aq), _cm_spec(d, 2, aq), own, own_n, own, own_n, own, own_n,
                  own, own_n],
        out_specs=[own, own], out_shape=[sh, sh],
        compiler_params=ATTN_PARAMS,
    )(zsrc, zsrc, zsrc, zsrc, dosrc, dosrc, osrc, osrc, lsrc, lsrc)


def _dz_assemble(dqs, dks, dvs, dcvg, T):
    tr = _rows(T)
    nb = len(DILATIONS)

    def body(*refs):
        cvg_ref, dz_ref, scr = refs[3 * nb], refs[3 * nb + 1], refs[3 * nb + 2]
        for g in range(3):
            parts = refs[g * nb:(g + 1) * nb]
            for c in range(DA // LANES):
                lt = _lane_tile(c)
                scr[g, c] = parts[0][:, lt].astype(F32)
                for w, d in enumerate(WIDE):
                    for r in range(d):
                        rows = pl.ds(r, tr // d, stride=d)
                        scr[g, c, rows, :] = scr[g, c, rows, :] + parts[1 + w][r, :, lt].astype(F32)
                dz_ref[:, g * DA + c * LANES:g * DA + (c + 1) * LANES] = scr[g, c].astype(BF16)
        dz_ref[:, 3 * DA:] = cvg_ref[...]

    specs = [_cm_tile(d, tr) for d in DILATIONS]
    return pl.pallas_call(
        body, name="dz_assemble", grid=(T // tr,),
        in_specs=specs * 3 + [pl.BlockSpec((tr, 2 * DC), lambda i: (i, 0))],
        out_specs=pl.BlockSpec((tr, DIN), lambda i: (i, 0)),
        out_shape=jax.ShapeDtypeStruct((T, DIN), BF16),
        scratch_shapes=[pltpu.VMEM((3, DA // LANES, tr, LANES), F32)],
        compiler_params=pltpu.CompilerParams(
            dimension_semantics=("parallel",), vmem_limit_bytes=VMEM_MID),
    )(*dqs, *dks, *dvs, dcvg)


CT = 256
HALO = 32
RC_FWD = 128
RC_BWD = 64


def _conv_fwd(z, w_dw, b_dw, g_ln, b_ln):
    T = z.shape[0]
    ct = min(CT, T)
    RC = RC_FWD
    nt = T // ct
    hb = ct // HALO

    def body(cv_ref, cg_ref, cvp_ref, cgp_ref, w_ref, bdw_ref, g_ref, b_ref, oc_ref, y_ref, ubuf, ush):
        i = pl.program_id(0)
        up = cvp_ref[...].astype(F32) * _sigmoid(cgp_ref[...].astype(F32))
        ubuf[0:HALO, :] = jnp.where(i > 0, up, 0.0)
        ubuf[HALO:, :] = cv_ref[...].astype(F32) * _sigmoid(cg_ref[...].astype(F32))
        for b in range(8):
            ush[b] = ubuf[pl.ds(8 - b, ct + 24), :]

        def chunk(ci, carry):
            r0 = pl.multiple_of(ci * RC, RC)
            acc = jnp.broadcast_to(bdw_ref[...], (RC, DC))
            for s in range(CW):
                a, b = divmod(s, 8)
                acc = acc + w_ref[CW - 1 - s:CW - s, :] * ush[b, pl.ds(r0 + 24 - 8 * a, RC), :]
            y_ref[pl.ds(r0, RC), :] = acc
            mu = jnp.mean(acc, axis=-1, keepdims=True)
            cen = acc - mu
            var = jnp.mean(cen * cen, axis=-1, keepdims=True)
            ln = cen * lax.rsqrt(var + EPS) * g_ref[...] + b_ref[...]
            oc_ref[pl.ds(r0, RC), :] = (ln * _sigmoid(ln)).astype(BF16)
            return carry

        lax.fori_loop(0, ct // RC, chunk, 0)

    cur = lambda col: pl.BlockSpec((ct, DC), lambda i: (i, col))
    prv = lambda col: pl.BlockSpec((HALO, DC), lambda i: (jnp.maximum(i * hb - 1, 0), col))
    vec = pl.BlockSpec((1, DC), lambda i: (0, 0))
    return pl.pallas_call(
        body, name="conv_fwd", grid=(nt,),
        in_specs=[cur(3), cur(4), prv(3), prv(4), pl.BlockSpec((CW, DC), lambda i: (0, 0)),
                  vec, vec, vec],
        out_specs=[pl.BlockSpec((ct, DC), lambda i: (i, 0))] * 2,
        out_shape=[jax.ShapeDtypeStruct((T, DC), BF16), jax.ShapeDtypeStruct((T, DC), F32)],
        scratch_shapes=[pltpu.VMEM((ct + HALO, DC), F32), pltpu.VMEM((8, ct + 24, DC), F32)],
        compiler_params=pltpu.CompilerParams(
            dimension_semantics=("parallel",), vmem_limit_bytes=VMEM_MID),
    )(z, z, z, z, w_dw, b_dw, g_ln, b_ln)


def _conv_bwd(z, dom, y, w_dw, g_ln, b_ln):
    T = z.shape[0]
    ct = min(CT, T)
    RC = RC_BWD
    nt = T // ct
    hb = ct // HALO
    last_halo = T // HALO - 1

    def ln_bwd(yv, dov, g_ref, b_ref):
        mu = jnp.mean(yv, axis=-1, keepdims=True)
        cen = yv - mu
        rstd = lax.rsqrt(jnp.mean(cen * cen, axis=-1, keepdims=True) + EPS)
        xhat = cen * rstd
        ln = xhat * g_ref[...] + b_ref[...]
        sg = _sigmoid(ln)
        dln = dov * (sg * (1.0 + ln * (1.0 - sg)))
        dxh = dln * g_ref[...]
        dy = rstd * (dxh - jnp.mean(dxh, axis=-1, keepdims=True)
                     - xhat * jnp.mean(dxh * xhat, axis=-1, keepdims=True))
        return dy, dln, xhat

    def body(do_ref, don_ref, y_ref, yn_ref, cv_ref, cg_ref, cvp_ref, cgp_ref, w_ref, g_ref, b_ref,
             dcvg_ref, dw_ref, dbdw_ref, dg_ref, db_ref,
             dybuf, dysh, ubuf, ush, dwacc, vacc):
        i = pl.program_id(0)

        @pl.when(i == 0)
        def _():
            dwacc[...] = jnp.zeros_like(dwacc)
            vacc[...] = jnp.zeros_like(vacc)

        def ln_chunk(ci, carry):
            r0 = pl.multiple_of(ci * RC, RC)
            dy, dln, xhat = ln_bwd(y_ref[pl.ds(r0, RC), :], do_ref[pl.ds(r0, RC), :].astype(F32),
                                   g_ref, b_ref)
            dybuf[pl.ds(r0, RC), :] = dy
            vacc[0] += _fold8(dy)
            vacc[1] += _fold8(dln * xhat)
            vacc[2] += _fold8(dln)
            return carry

        lax.fori_loop(0, ct // RC, ln_chunk, 0)
        dyn, _, _ = ln_bwd(yn_ref[...], don_ref[...].astype(F32), g_ref, b_ref)
        dybuf[ct:, :] = jnp.where(i < nt - 1, dyn, 0.0)
        for b in range(8):
            dysh[b] = dybuf[pl.ds(b, ct + 24), :]

        up = cvp_ref[...].astype(F32) * _sigmoid(cgp_ref[...].astype(F32))
        ubuf[0:HALO, :] = jnp.where(i > 0, up, 0.0)
        ubuf[HALO:, :] = cv_ref[...].astype(F32) * _sigmoid(cg_ref[...].astype(F32))
        for b in range(8):
            ush[b] = ubuf[pl.ds(8 - b, ct + 24), :]

        def chunk(ci, carry):
            r0 = pl.multiple_of(ci * RC, RC)
            dy = dybuf[pl.ds(r0, RC), :]
            du = jnp.zeros((RC, DC), F32)
            for s in range(CW):
                a, b = divmod(s, 8)
                du = du + w_ref[CW - 1 - s:CW - s, :] * dysh[b, pl.ds(r0 + 8 * a, RC), :]
                dwacc[CW - 1 - s] += _fold8(dy * ush[b, pl.ds(r0 + 24 - 8 * a, RC), :])
            cv = cv_ref[pl.ds(r0, RC), :].astype(F32)
            sg = _sigmoid(cg_ref[pl.ds(r0, RC), :].astype(F32))
            dcvg_ref[pl.ds(r0, RC), 0:DC] = (du * sg).astype(BF16)
            dcvg_ref[pl.ds(r0, RC), DC:2 * DC] = (du * cv * sg * (1.0 - sg)).astype(BF16)
            return carry

        lax.fori_loop(0, ct // RC, chunk, 0)

        @pl.when(i == nt - 1)
        def _():
            dw_ref[...] = jnp.sum(dwacc[...], axis=1)
            dbdw_ref[...] = jnp.sum(vacc[0], axis=0, keepdims=True)
            dg_ref[...] = jnp.sum(vacc[1], axis=0, keepdims=True)
            db_ref[...] = jnp.sum(vacc[2], axis=0, keepdims=True)

    cur = lambda col: pl.BlockSpec((ct, DC), lambda i: (i, col))
    prv = lambda col: pl.BlockSpec((HALO, DC), lambda i: (jnp.maximum(i * hb - 1, 0), col))
    nxt = lambda col: pl.BlockSpec((HALO, DC), lambda i: (jnp.minimum((i + 1) * hb, last_halo), col))
    vec = pl.BlockSpec((1, DC), lambda i: (0, 0))
    tile = pl.BlockSpec((ct, DC), lambda i: (i, 0))
    return pl.pallas_call(
        body, name="conv_bwd", grid=(nt,),
        in_specs=[cur(1), nxt(1), cur(0), nxt(0), cur(3), cur(4), prv(3), prv(4),
                  pl.BlockSpec((CW, DC), lambda i: (0, 0)), vec, vec],
        out_specs=[pl.BlockSpec((ct, 2 * DC), lambda i: (i, 0)),
                   pl.BlockSpec((CW, DC), lambda i: (0, 0)), vec, vec, vec],
        out_shape=[jax.ShapeDtypeStruct((T, 2 * DC), BF16),
                   jax.ShapeDtypeStruct((CW, DC), F32), jax.ShapeDtypeStruct((1, DC), F32),
                   jax.ShapeDtypeStruct((1, DC), F32), jax.ShapeDtypeStruct((1, DC), F32)],
        scratch_shapes=[pltpu.VMEM((ct + HALO, DC), F32), pltpu.VMEM((8, ct + 24, DC), F32),
                        pltpu.VMEM((ct + HALO, DC), F32), pltpu.VMEM((8, ct + 24, DC), F32),
                        pltpu.VMEM((CW, 8, DC), F32), pltpu.VMEM((3, 8, DC), F32)],
        compiler_params=pltpu.CompilerParams(
            dimension_semantics=("arbitrary",), vmem_limit_bytes=VMEM_BIG),
    )(dom, dom, y, y, z, z, z, z, w_dw, g_ln, b_ln)


def _adam_math(w, g, m, v):
    m = ADAM_B1 * m + (1.0 - ADAM_B1) * g
    v = ADAM_B2 * v + (1.0 - ADAM_B2) * (g * g)
    m_hat = m / (1.0 - ADAM_B1 ** ADAM_STEP)
    v_hat = v / (1.0 - ADAM_B2 ** ADAM_STEP)
    delta = -ADAM_LR * (m_hat / (jnp.sqrt(v_hat) + ADAM_EPS) + ADAM_WD * w)
    return delta, m, v


def _adam(name, slots, w, m, v):
    rows, cols = w.shape
    tr = next(t for t in (256, 176, 128, 64, 32, 16, 8, rows) if rows % t == 0)

    def body(s_ref, w_ref, m_ref, v_ref, g_out, d_out, m_out, v_out):
        g = s_ref[0].astype(F32)
        for s in range(1, NDEV):
            g = g + s_ref[s].astype(F32)
        delta, mn, vn = _adam_math(w_ref[...], g, m_ref[...], v_ref[...])
        g_out[...] = g
        d_out[...] = delta
        m_out[...] = mn
        v_out[...] = vn

    tile = pl.BlockSpec((tr, cols), lambda i: (i, 0))
    sh = jax.ShapeDtypeStruct((rows, cols), F32)
    return pl.pallas_call(
        body, name=name, grid=(rows // tr,),
        in_specs=[pl.BlockSpec((NDEV, tr, cols), lambda i: (0, i, 0)), tile, tile, tile],
        out_specs=[tile] * 4, out_shape=[sh] * 4,
        compiler_params=pltpu.CompilerParams(
            dimension_semantics=("parallel",), vmem_limit_bytes=VMEM_MID),
    )(slots, w, m, v)


SMALL_NAMES = ("g_mix", "b_dw", "g_conv_ln", "b_conv_ln", "g_ffn", "g_ple", "b_pgate", "g_final")


def _pack_small(vecs, w_dw_full, last=None):
    widen = lambda v: jnp.pad(v.reshape(1, -1), ((0, 0), (0, SMALL_W - v.size)))
    rows = [widen(v) for v in vecs]
    rows.append(jnp.pad(w_dw_full, ((0, 0), (0, SMALL_W - DC))))
    rows.append(jnp.zeros((SMALL_ROWS - len(vecs) - CW, SMALL_W), F32) if last is None else widen(last))
    return jnp.concatenate(rows, axis=0)


def kernel(x, p, g_mix, w_in, w_dw, b_dw, g_conv_ln, b_conv_ln, w_out, g_ffn, w_gate, w_up, w_down, g_ple, w_pgate, b_pgate, w_ple, g_final, loss_target, m_g_mix, m_w_in, m_w_dw, m_b_dw, m_g_conv_ln, m_b_conv_ln, m_w_out, m_g_ffn, m_w_gate, m_w_up, m_w_down, m_g_ple, m_w_pgate, m_b_pgate, m_w_ple, m_g_final, v_g_mix, v_w_in, v_w_dw, v_b_dw, v_g_conv_ln, v_b_conv_ln, v_w_out, v_g_ffn, v_w_gate, v_w_up, v_w_down, v_g_ple, v_w_pgate, v_b_pgate, v_w_ple, v_g_final):
    T = x.shape[1]
    me = 4 * lax.axis_index("x") + 2 * lax.axis_index("y") + lax.axis_index("c")
    xs = x.reshape(T, D)
    ps = p.reshape(T, DPLE).astype(BF16)
    tgt = loss_target.reshape(T, D)
    g_final2 = g_final.reshape(1, D)

    tr_names = ("w_gate", "w_up")
    big = dict(w_in=w_in[0], w_out=w_out[0], w_gate=w_gate[0].T, w_up=w_up[0].T, w_down=w_down[0],
               w_pgate=w_pgate[0], w_ple=w_ple[0])
    order = ("w_in", "w_out", "w_gate", "w_up", "w_down", "w_pgate", "w_ple")
    w_dw_g, w_in_s = _gather_two_level(
        "gather_first", [w_dw.reshape(CW, DC // NDEV), big["w_in"].astype(BF16)])
    w_dw_f = w_dw_g.transpose(1, 0, 2).reshape(CW, DC)
    later = order[1:]
    lands = _place("gather_place", [(big[n], False) for n in later], dtype=BF16)
    g_handles, g_token = _xstart("gather_start", [(None, False)] * len(later), lands, deps=[w_in_s])
    w_in_f = _shards_to_cols("w_in_natural", w_in_s)
    G = dict(zip(later, g_handles))

    a, z, *z_wide = _mm_in(xs, g_mix, w_in_f, deps=[g_token])
    zsrc = dict(zip(DILATIONS, [z] + z_wide))
    br = [_attn_fwd(f"attn_fwd_d{d}", zsrc[d], d, T) for d in DILATIONS]
    comb = list(_attn_combine([b[0] for b in br], [b[1] for b in br], T))
    o_attn, lse = comb[0], comb[1]
    osrc = dict(zip(DILATIONS, [o_attn] + comb[2:2 + len(WIDE)]))
    lsrc = dict(zip(DILATIONS, [lse] + comb[2 + len(WIDE):]))
    o_conv, y_conv = _conv_fwd(z, w_dw_f, b_dw, g_conv_ln, b_conv_ln)
    w_out_f = _xwait("gather_wait_w_out", G["w_out"], o_conv).reshape(D, D)
    h1, f = _mm_out(o_attn, o_conv, w_out_f, xs, g_ffn)
    w_gate_f = _xwait("gather_wait_w_gate", G["w_gate"], f).reshape(DFF, D)
    w_up_f = _xwait("gather_wait_w_up", G["w_up"], f).reshape(DFF, D)
    gate, up, act = _mm_gate_up(f, w_gate_f, w_up_f)
    w_down_f = _xwait("gather_wait_w_down", G["w_down"], act).reshape(DFF, D)
    h2, r = _mm_down(act, w_down_f, h1, g_ple)
    w_pgate_f = _xwait("gather_wait_w_pgate", G["w_pgate"], r).reshape(D, D)
    w_ple_f = _xwait("gather_wait_w_ple", G["w_ple"], r).transpose(1, 0, 2).reshape(DPLE, D)

    loss_part, dh3, dpe, dpg, d_g_final, d_b_pgate = _ple_loss(
        r, w_pgate_f, b_pgate, ps, w_ple_f, h2, tgt, g_final2)
    H = {}

    def send_grads(tag, named):
        items = [(v, True) for _, v in named]
        handles, token = _xstart(f"grads_start_{tag}", items, _place(f"grads_place_{tag}", items))
        H.update(zip([n for n, _ in named], handles))
        return token

    gw_pgate = _mm_tn("gw_pgate", r, dpg).reshape(NDEV, D // NDEV, D)
    gw_ple = _mm_tn("gw_ple", ps, dpe).reshape(DPLE, NDEV, D // NDEV).transpose(1, 0, 2)
    tok = send_grads("ple", [("w_pgate", gw_pgate), ("w_ple", gw_ple)])
    dh2b, d_g_ple = _mm_pgate_bwd(dpg, w_pgate_f, h2, g_ple, dh3, deps=[tok])
    ff_shards = lambda g: g.reshape(NDEV, N_FF, D)
    gw_down = ff_shards(_mm_tn_ff("gw_down", act, dh2b))
    tok = send_grads("down", [("w_down", gw_down)])
    dgate, dup = _mm_down_bwd(dh2b, w_down_f, gate, up, deps=[tok])
    gw_gate = ff_shards(_mm_tn_ff("gw_gate", dgate, f))
    gw_up = ff_shards(_mm_tn_ff("gw_up", dup, f))
    tok = send_grads("ffn", [("w_gate", gw_gate), ("w_up", gw_up)])
    df = _mm_ffn_in_bwd(dgate, w_gate_f, dup, w_up_f, deps=[tok])
    dh1b, d_g_ffn = _rms_bwd("rms_ffn_bwd", df, h1, g_ffn, dh2b)
    gw_out = jnp.concatenate(
        [_mm_tn("gw_out_attn", o_attn, dh1b), _mm_tn("gw_out_conv", o_conv, dh1b)], axis=0)
    tok = send_grads("out", [("w_out", gw_out.reshape(NDEV, D // NDEV, D))])
    dom, *do_wide = _mm_out_bwd(dh1b, w_out_f, deps=[tok])
    dosrc = dict(zip(DILATIONS, [dom] + do_wide))
    dcvg, d_w_dw, d_b_dw, d_g_ln, d_b_ln = _conv_bwd(z, dom, y_conv, w_dw_f, g_conv_ln, b_conv_ln)
    dqs, dks, dvs = [], [], []
    for d in DILATIONS:
        dqs.append(_attn_bwd_q(f"attn_bwd_q_d{d}", zsrc[d], dosrc[d], osrc[d], lsrc[d], d, T))
        dk, dv = _attn_bwd_kv(f"attn_bwd_kv_d{d}", zsrc[d], dosrc[d], osrc[d], lsrc[d], d, T)
        dks.append(dk)
        dvs.append(dv)
    dz = _dz_assemble(dqs, dks, dvs, dcvg, T)
    gw_in = _cols_to_shards("gw_in_shards", _mm_tn_wide("gw_in", a, dz), N_IN)
    tok = send_grads("in", [("w_in", gw_in)])
    grad_x, d_g_mix = _mm_in_bwd(dz, w_in_f, xs, g_mix, dh1b, deps=[tok])

    small_part = _pack_small(
        [d_g_mix, d_b_dw, d_g_ln, d_b_ln, d_g_ffn, d_g_ple, d_b_pgate, d_g_final], d_w_dw,
        last=loss_part)
    small_slots = _exchange("exchange_small_grads", [(small_part, False)])[0]
    S = {n: _xwait(f"grads_wait_{n}", H[n], small_slots)
         for n in ("w_pgate", "w_ple", "w_down", "w_gate", "w_up", "w_out", "w_in")}

    mom = dict(w_in=(m_w_in, v_w_in), w_out=(m_w_out, v_w_out), w_gate=(m_w_gate, v_w_gate),
               w_up=(m_w_up, v_w_up), w_down=(m_w_down, v_w_down), w_pgate=(m_w_pgate, v_w_pgate),
               w_ple=(m_w_ple, v_w_ple))
    upd = {}
    for n in order:
        m_n, v_n = mom[n][0][0], mom[n][1][0]
        if n in tr_names:
            res = _adam(f"adam_{n}", S[n], big[n], m_n.T, v_n.T)
            upd[n] = [t.T[None] for t in res]
        else:
            res = _adam(f"adam_{n}", S[n], big[n], m_n, v_n)
            upd[n] = [t[None] for t in res]

    def lanes(v):
        full = jnp.zeros((CW, NDEV, DC // NDEV), F32)
        full = lax.dynamic_update_slice(full, v.reshape(CW, 1, DC // NDEV), (0, me, 0))
        return full.reshape(CW, DC)

    small_w = _pack_small([g_mix, b_dw, g_conv_ln, b_conv_ln, g_ffn, g_ple, b_pgate, g_final2], lanes(w_dw))
    small_m = _pack_small([m_g_mix, m_b_dw, m_g_conv_ln, m_b_conv_ln, m_g_ffn, m_g_ple, m_b_pgate,
                           m_g_final.reshape(1, D)], lanes(m_w_dw))
    small_v = _pack_small([v_g_mix, v_b_dw, v_g_conv_ln, v_b_conv_ln, v_g_ffn, v_g_ple, v_b_pgate,
                           v_g_final.reshape(1, D)], lanes(v_w_dw))
    small_res = _adam("adam_small", small_slots, small_w, small_m, small_v)

    def unpack(t):
        out = {}
        widths = dict(g_mix=D, b_dw=DC, g_conv_ln=DC, b_conv_ln=DC, g_ffn=D, g_ple=D, b_pgate=D, g_final=D)
        for i, n in enumerate(SMALL_NAMES):
            out[n] = t[i:i + 1, :widths[n]]
        out["g_final"] = out["g_final"].reshape(D)
        taps = t[len(SMALL_NAMES):len(SMALL_NAMES) + CW, :DC].reshape(CW, NDEV, DC // NDEV)
        out["w_dw"] = lax.dynamic_slice(taps, (0, me, 0), (CW, 1, DC // NDEV))[None]
        return out

    small = [unpack(t) for t in small_res]

    loss = small_res[0][SMALL_ROWS - 1, 0]
    names = ("g_mix", "w_in", "w_dw", "b_dw", "g_conv_ln", "b_conv_ln", "w_out", "g_ffn", "w_gate",
             "w_up", "w_down", "g_ple", "w_pgate", "b_pgate", "w_ple", "g_final")
    outs = [loss, grad_x.reshape(1, T, D)]
    for kind in range(4):
        for n in names:
            outs.append(upd[n][kind] if n in upd else small[kind][n])
    return tuple(outs)
```

```python
import jax
import jax.numpy as jnp
from jax import lax
from jax.experimental import pallas as pl
from jax.experimental.pallas import tpu as pltpu

F32 = jnp.float32
BF16 = jnp.bfloat16

NDEV = 8
D = 2048
NH = 8
DH = 128
DA = NH * DH
DC = D - DA
DIN = 3 * DA + 2 * DC
DFF = 5632
DPLE = 256
BLK = 128
DILATIONS = (1, 4, 16)
CW = 31
EPS = 1e-6
N_IN = DIN // NDEV
N_FF = DFF // NDEV
NEG = -1e30

ADAM_LR = 0.001
ADAM_B1 = 0.9
ADAM_B2 = 0.999
ADAM_EPS = 1e-08
ADAM_WD = 0.01
ADAM_STEP = 10

VMEM_CAP_V7X = 64 * 1024 * 1024
VMEM_BIG = VMEM_CAP_V7X - 12 * 1024 * 1024
VMEM_MID = 40 * 1024 * 1024

SMALL_W = 2048
SMALL_ROWS = 40


def _sigmoid(v):
    return 1.0 / (1.0 + jnp.exp(-v))


def _dot(a, b, contract):
    return lax.dot_general(a, b, (contract, ((), ())), preferred_element_type=F32)


RESIDENT = pl.Buffered(1)

NN = ((1,), (0,))
NT = ((1,), (1,))
TN = ((0,), (0,))


def _exchange(name, items):
    n = len(items)
    out_shape = [
        jax.ShapeDtypeStruct((NDEV,) + (a.shape[1:] if sc else a.shape), a.dtype)
        for a, sc in items
    ]
    scat = [sc for _, sc in items]

    def body(*refs):
        srcs = refs[:n]
        dsts = refs[n:2 * n]
        send_sems, recv_sems, loc_sems = refs[2 * n:]
        x = lax.axis_index("x")
        y = lax.axis_index("y")
        c = lax.axis_index("c")
        me = 4 * x + 2 * y + c

        local = []
        for i in range(n):
            src = srcs[i].at[me] if scat[i] else srcs[i]
            cp = pltpu.make_async_copy(src, dsts[i].at[me], loc_sems.at[i])
            cp.start()
            local.append(cp)

        remote = []
        for k in range(1, NDEV):
            px = (1 - x) if (k >> 2) & 1 else x
            py = (1 - y) if (k >> 1) & 1 else y
            pc = (1 - c) if k & 1 else c
            peer = 4 * px + 2 * py + pc
            for i in range(n):
                sem = i * (NDEV - 1) + k - 1
                src = srcs[i].at[peer] if scat[i] else srcs[i]
                send = pltpu.make_async_remote_copy(
                    src_ref=src, dst_ref=dsts[i].at[me],
                    send_sem=send_sems.at[sem], recv_sem=recv_sems.at[sem],
                    device_id=(px, py, pc), device_id_type=pl.DeviceIdType.MESH)
                send.start()
                recv = pltpu.make_async_remote_copy(
                    src_ref=src, dst_ref=dsts[i].at[peer],
                    send_sem=send_sems.at[sem], recv_sem=recv_sems.at[sem],
                    device_id=(px, py, pc), device_id_type=pl.DeviceIdType.MESH)
                remote.append((send, recv))
        for send, recv in remote:
            recv.wait_recv()
            send.wait_send()
        for cp in local:
            cp.wait()

    any_spec = pl.BlockSpec(memory_space=pl.ANY)
    return pl.pallas_call(
        body, name=name,
        in_specs=[any_spec] * n, out_specs=[any_spec] * n, out_shape=out_shape,
        scratch_shapes=[
            pltpu.SemaphoreType.DMA((n * (NDEV - 1),)),
            pltpu.SemaphoreType.DMA((n * (NDEV - 1),)),
            pltpu.SemaphoreType.DMA((n,)),
        ],
    )(*[a for a, _ in items])


def _gather_two_level(name, arrays):
    n = len(arrays)
    per = NDEV - 1

    def body(*refs):
        srcs = refs[:n]
        dsts = refs[n:2 * n]
        send_sems, recv_sems, loc_sems = refs[2 * n:]
        x = lax.axis_index("x")
        y = lax.axis_index("y")
        c = lax.axis_index("c")
        idx = lambda px, py, pc: 4 * px + 2 * py + pc
        me, sibling = (x, y, c), (x, y, 1 - c)
        chips = [(1 - x, y), (x, 1 - y), (1 - x, 1 - y)]

        def copy(i, k, block, to, src=None):
            slot = dsts[i].at[idx(*block)]
            return pltpu.make_async_remote_copy(
                src_ref=slot if src is None else src, dst_ref=slot,
                send_sem=send_sems.at[i * per + k], recv_sem=recv_sems.at[i * per + k],
                device_id=to, device_id_type=pl.DeviceIdType.MESH)

        mine, sent = [], []
        for i in range(n):
            cp = pltpu.make_async_copy(srcs[i], dsts[i].at[idx(*me)], loc_sems.at[i])
            cp.start()
            mine.append(cp)
            first = [copy(i, 0, me, sibling, src=srcs[i])]
            first += [copy(i, 1 + j, me, (*chip, c), src=srcs[i]) for j, chip in enumerate(chips)]
            for cp in first:
                cp.start()
            sent += first
        for j, chip in enumerate(chips):
            for i in range(n):
                copy(i, 1 + j, (*chip, c), me).wait_recv()
                fwd = copy(i, 4 + j, (*chip, c), sibling)
                fwd.start()
                sent.append(fwd)
        for i in range(n):
            copy(i, 0, sibling, me).wait_recv()
            for j, chip in enumerate(chips):
                copy(i, 4 + j, (*chip, 1 - c), me).wait_recv()
        for cp in sent:
            cp.wait_send()
        for cp in mine:
            cp.wait()

    any_spec = pl.BlockSpec(memory_space=pl.ANY)
    return pl.pallas_call(
        body, name=name, in_specs=[any_spec] * n, out_specs=[any_spec] * n,
        out_shape=[jax.ShapeDtypeStruct((NDEV,) + a.shape, a.dtype) for a in arrays],
        scratch_shapes=[pltpu.SemaphoreType.DMA((n * per,)), pltpu.SemaphoreType.DMA((n * per,)),
                        pltpu.SemaphoreType.DMA((n,))],
    )(*arrays)


HBM_SPEC = pl.BlockSpec(memory_space=pltpu.HBM)
SEM_SPEC = pl.BlockSpec(memory_space=pltpu.SEMAPHORE)
ANY_SPEC = pl.BlockSpec(memory_space=pl.ANY)
EFFECT = pltpu.SideEffectType.DATAFLOW_SIDE_EFFECTING


def _peer_of(k):
    x = lax.axis_index("x")
    y = lax.axis_index("y")
    c = lax.axis_index("c")
    px = (1 - x) if (k >> 2) & 1 else x
    py = (1 - y) if (k >> 1) & 1 else y
    pc = (1 - c) if k & 1 else c
    return (px, py, pc), 4 * px + 2 * py + pc


def _my_index():
    return 4 * lax.axis_index("x") + 2 * lax.axis_index("y") + lax.axis_index("c")


def _slot_shape(a, sc):
    return (NDEV,) + (a.shape[1:] if sc else a.shape)


def _divisor_tile(rows):
    return next((t for t in (512, 256, 176, 128, 64, 32, 16) if rows % t == 0), rows)


def _place(name, items, dtype=None):
    lands = []
    for idx, (a, sc) in enumerate(items):
        rows, cols = a.shape[-2:]
        tr = _divisor_tile(rows)
        out_dtype = a.dtype if dtype is None else dtype

        def body(s_ref, o_ref):
            o_ref[...] = s_ref[...].astype(o_ref.dtype)

        mine = pl.BlockSpec((None, tr, cols), lambda i: (_my_index(), i, 0))
        lands.append(pl.pallas_call(
            body, name=f"{name}_{idx}", grid=(rows // tr,),
            in_specs=[mine if sc else pl.BlockSpec((tr, cols), lambda i: (i, 0))],
            out_specs=mine,
            out_shape=jax.ShapeDtypeStruct(_slot_shape(a, sc), out_dtype),
            compiler_params=pltpu.CompilerParams(dimension_semantics=("parallel",)),
        )(a))
    return lands


def _xstart(name, items, lands, deps=()):
    n = len(items)
    scat = [sc for _, sc in items]
    srcs_in = [a for a, sc in items if sc]
    n_src = len(srcs_in)
    src_pos = {i: p for p, i in enumerate(i for i in range(n) if scat[i])}

    def body(*refs):
        srcs = refs[:n_src]
        lzs = refs[n_src:n_src + n]
        outs = refs[n_src + n + len(deps):]
        send_sems, recv_sems, token = outs[:n], outs[n:2 * n], outs[-1]
        me = _my_index()
        for i in range(n):
            for k in range(1, NDEV):
                peer_id, peer = _peer_of(k)
                src = srcs[src_pos[i]].at[peer] if scat[i] else lzs[i].at[me]
                pltpu.make_async_remote_copy(
                    src_ref=src, dst_ref=lzs[i].at[me],
                    send_sem=send_sems[i].at[k - 1], recv_sem=recv_sems[i].at[k - 1],
                    device_id=peer_id, device_id_type=pl.DeviceIdType.MESH).start()
        token[...] = jnp.zeros_like(token)

    sem = pltpu.SemaphoreType.DMA((NDEV - 1,))
    thru = srcs_in + list(lands)
    res = pl.pallas_call(
        body, name=name,
        in_specs=[HBM_SPEC] * len(thru) + [ANY_SPEC] * len(deps),
        out_specs=[SEM_SPEC] * (2 * n) + [HBM_SPEC] * len(thru) + [pl.BlockSpec(memory_space=pltpu.VMEM)],
        out_shape=[sem] * (2 * n) + [pltpu.HBM(t.shape, t.dtype) for t in thru]
        + [jax.ShapeDtypeStruct((8, 128), F32)],
        input_output_aliases={i: 2 * n + i for i in range(len(thru))},
        compiler_params=pltpu.CompilerParams(has_side_effects=EFFECT),
    )(*[pltpu.with_memory_space_constraint(t, pltpu.HBM) for t in thru], *deps)
    handles = [(res[i], res[n + i], res[2 * n + src_pos[i]] if scat[i] else None,
                res[2 * n + n_src + i]) for i in range(n)]
    return handles, res[-1]


def _xwait(name, handle, after):
    send_sem, recv_sem, src, land = handle
    sc = src is not None

    def body(*refs):
        land_ref = refs[1] if sc else refs[0]
        send_ref, recv_ref = (refs[2], refs[3]) if sc else (refs[1], refs[2])
        me = _my_index()
        for k in range(1, NDEV):
            peer_id, peer = _peer_of(k)
            cp = pltpu.make_async_remote_copy(
                src_ref=refs[0].at[peer] if sc else land_ref.at[me], dst_ref=land_ref.at[peer],
                send_sem=send_ref.at[k - 1], recv_sem=recv_ref.at[k - 1],
                device_id=peer_id, device_id_type=pl.DeviceIdType.MESH)
            cp.wait_send()
            cp.wait_recv()

    thru = ([src] if sc else []) + [land]
    return pl.pallas_call(
        body, name=name,
        in_specs=[HBM_SPEC] * len(thru) + [SEM_SPEC, SEM_SPEC, ANY_SPEC],
        out_specs=[HBM_SPEC] * len(thru),
        out_shape=[pltpu.HBM(t.shape, t.dtype) for t in thru],
        input_output_aliases={i: i for i in range(len(thru))},
        compiler_params=pltpu.CompilerParams(has_side_effects=EFFECT),
    )(*thru, send_sem, recv_sem, after)[-1]


def _mm(name, grid, in_specs, operands, out_specs, out_shape, contract, n_pairs, epilogue,
        acc_shape=None, vmem=VMEM_BIG, deps=(), group=1, a_cols=None, carry=()):
    nk = grid[2]
    n_carry = len(carry)

    def shard(ref, s, is_a):
        if group == 1:
            return ref[...]
        if is_a and a_cols is not None:
            return ref[:, s * a_cols:(s + 1) * a_cols]
        return ref[s]
    n_extra = len(operands) - 2 * n_pairs
    n_out = len(out_shape)
    n_in = len(operands) + len(deps)
    in_specs = list(in_specs) + [ANY_SPEC] * len(deps)
    operands = list(operands) + list(deps)

    def body(*refs):
        ab = refs[:2 * n_pairs]
        extras = refs[2 * n_pairs:2 * n_pairs + n_extra]
        outs = refs[n_in:n_in + n_out]
        kept = refs[n_in + n_out:n_in + n_out + n_carry]
        finish = (lambda acc: epilogue(acc, extras, outs, kept)) if n_carry else (
            lambda acc: epilogue(acc, extras, outs))
        dots = [(ab[2 * p], ab[2 * p + 1], s) for p in range(n_pairs) for s in range(group)]
        if nk == 1:
            part = None
            for a_ref, b_ref, s in dots:
                d = _dot(shard(a_ref, s, True), shard(b_ref, s, False), contract)
                part = d if part is None else part + d
            finish(part)
        else:
            acc_ref = refs[-1]
            k = pl.program_id(2)

            @pl.when(k == 0)
            def _():
                acc_ref[...] = jnp.zeros_like(acc_ref)

            for a_ref, b_ref, s in dots:
                acc_ref[...] += _dot(shard(a_ref, s, True), shard(b_ref, s, False), contract)

            @pl.when(k == nk - 1)
            def _():
                finish(acc_ref[...])

    scratch = list(carry) + ([pltpu.VMEM(acc_shape, F32)] if nk > 1 else [])
    semantics = ("arbitrary",) * 3 if n_carry else ("parallel", "parallel", "arbitrary")
    return pl.pallas_call(
        body, name=name, grid=grid, in_specs=in_specs, out_specs=out_specs, out_shape=out_shape,
        scratch_shapes=scratch,
        compiler_params=pltpu.CompilerParams(dimension_semantics=semantics, vmem_limit_bytes=vmem),
    )(*operands)


def _ep_cast(dtype):
    def ep(acc, extras, outs):
        outs[0][...] = acc.astype(dtype)
    return ep


def _ep_rms_bwd(n_rows_steps):
    def ep(acc, extras, outs, kept):
        h_ref, g_ref, dres_ref = extras
        gacc = kept[0]
        i = pl.program_id(0)
        v = h_ref[...]
        r = lax.rsqrt(jnp.mean(v * v, axis=-1, keepdims=True) + EPS)
        nrm = v * r
        dn = acc * g_ref[...]
        dh = dres_ref[...].astype(F32) + r * (dn - nrm * jnp.mean(dn * nrm, axis=-1, keepdims=True))
        outs[0][...] = dh.astype(outs[0].dtype)

        @pl.when(i == 0)
        def _():
            gacc[...] = jnp.zeros_like(gacc)

        gacc[...] += _fold8(acc * nrm)

        @pl.when(i == n_rows_steps - 1)
        def _():
            outs[-1][...] = jnp.sum(gacc[...], axis=0, keepdims=True)
    return ep


def _ep_resid_norm(acc, extras, outs):
    h = extras[0][...] + acc
    outs[0][...] = h
    r = lax.rsqrt(jnp.mean(h * h, axis=-1, keepdims=True) + EPS)
    outs[1][...] = (h * r * extras[1][...]).astype(BF16)


def _ep_swiglu_bwd(acc, extras, outs):
    outs[0][...] = (acc * extras[0][...].astype(F32)).astype(BF16)
    outs[1][...] = (acc * extras[1][...].astype(F32)).astype(BF16)


MXU_COLS_V7X = 256


def _col_chunks(n):
    return [slice(c, min(c + MXU_COLS_V7X, n)) for c in range(0, n, MXU_COLS_V7X)]


def _row_tile(T):
    return min(1024, T)


def _tn_rows(T):
    return min(2048, T)


WIDE = tuple(d for d in DILATIONS if d > 1)


LANES = 128


def _lane_tile(c):
    return slice(c * LANES, (c + 1) * LANES)


def _to_lane_tiles(scr, val):
    for c in range(scr.shape[0]):
        scr[c] = val[:, _lane_tile(c)]


def _emit_class_major(scr, refs, rows):
    for d, ref in zip(WIDE, refs):
        for r in range(d):
            for c in range(scr.shape[0]):
                ref[r, :, _lane_tile(c)] = scr[c, pl.ds(r, rows // d, stride=d), :].astype(ref.dtype)


def _shards_to_cols(name, w):
    _, rows, n = w.shape
    tr = rows

    def body(s_ref, o_ref):
        o_ref[...] = s_ref[...]

    return pl.pallas_call(
        body, name=name, grid=(rows // tr, NDEV),
        in_specs=[pl.BlockSpec((None, tr, n), lambda i, j: (j, i, 0))],
        out_specs=pl.BlockSpec((tr, n), lambda i, j: (i, j)),
        out_shape=jax.ShapeDtypeStruct((rows, NDEV * n), w.dtype),
        compiler_params=pltpu.CompilerParams(dimension_semantics=("parallel", "parallel")),
    )(w)


def _cols_to_shards(name, g, n):
    rows = g.shape[0]
    tr = rows

    def body(s_ref, o_ref):
        o_ref[...] = s_ref[...]

    return pl.pallas_call(
        body, name=name, grid=(rows // tr, NDEV),
        in_specs=[pl.BlockSpec((tr, n), lambda i, j: (i, j))],
        out_specs=pl.BlockSpec((None, tr, n), lambda i, j: (j, i, 0)),
        out_shape=jax.ShapeDtypeStruct((NDEV, rows, n), g.dtype),
        compiler_params=pltpu.CompilerParams(dimension_semantics=("parallel", "parallel")),
    )(g)


IN_TN = DA


def _mm_in(x, g, w_in, deps=()):
    T = x.shape[0]
    tm = min(TM_FULL_ROW, T)
    nq = 3

    def body(x_ref, g_ref, w_ref, *rest):
        a_ref, z_ref, *rest = rest[len(deps):]
        scr = rest[-1]
        j = pl.program_id(1)

        @pl.when(j == 0)
        def _():
            v = x_ref[...]
            r = lax.rsqrt(jnp.mean(v * v, axis=-1, keepdims=True) + EPS)
            a_ref[...] = (v * r * g_ref[...]).astype(BF16)

        @pl.when(j >= nq)
        def _():
            z_ref[...] = _dot(a_ref[...], w_ref[...], NN).astype(BF16)

        @pl.when(j < nq)
        def _():
            av = a_ref[...]
            chunks = _col_chunks(IN_TN)
            pending = _dot(av, w_ref[:, chunks[0]], NN)
            for ci, cols in enumerate(chunks):
                nxt = _dot(av, w_ref[:, chunks[ci + 1]], NN) if ci + 1 < len(chunks) else None
                z_ref[:, cols] = pending.astype(BF16)
                for c in range(cols.start // LANES, cols.stop // LANES):
                    scr[c] = pending[:, c * LANES - cols.start:(c + 1) * LANES - cols.start]
                    for d, ref in zip(WIDE, rest[:-1]):
                        for r in range(d):
                            ref[r, :, _lane_tile(c)] = scr[c, pl.ds(r, tm // d, stride=d), :].astype(BF16)
                pending = nxt

    cm_spec = lambda d: pl.BlockSpec((d, tm // d, IN_TN), lambda i, j: (0, i, jnp.minimum(j, nq - 1)))
    row = pl.BlockSpec((tm, D), lambda i, j: (i, 0))
    return pl.pallas_call(
        body, name="mm_in", grid=(T // tm, DIN // IN_TN),
        in_specs=[row, pl.BlockSpec((1, D), lambda i, j: (0, 0)),
                  pl.BlockSpec((D, IN_TN), lambda i, j: (0, j))] + [ANY_SPEC] * len(deps),
        out_specs=[row, pl.BlockSpec((tm, IN_TN), lambda i, j: (i, j))] + [cm_spec(d) for d in WIDE],
        out_shape=[jax.ShapeDtypeStruct((T, D), BF16), jax.ShapeDtypeStruct((T, DIN), BF16)]
        + [jax.ShapeDtypeStruct((d, T // d, nq * IN_TN), BF16) for d in WIDE],
        scratch_shapes=[pltpu.VMEM((IN_TN // LANES, tm, LANES), F32)],
        compiler_params=pltpu.CompilerParams(
            dimension_semantics=("parallel", "arbitrary"), vmem_limit_bytes=VMEM_BIG),
    )(x, g, w_in, *deps)


def _mm_out_bwd(dh1b, w_out, deps=()):
    T = dh1b.shape[0]
    tm = _row_tile(T)

    def body(dy_ref, w_ref, *rest):
        rest = rest[len(deps):]
        dom_ref, scr = rest[0], rest[-1]
        acc = _dot(dy_ref[...], w_ref[...], NT)
        dom_ref[...] = acc.astype(BF16)

        @pl.when(pl.program_id(1) == 0)
        def _():
            _to_lane_tiles(scr, acc)
            _emit_class_major(scr, rest[1:-1], tm)

    return pl.pallas_call(
        body, name="mm_out_bwd", grid=(T // tm, D // DA),
        in_specs=[pl.BlockSpec((tm, D), lambda i, j: (i, 0)),
                  pl.BlockSpec((DA, D), lambda i, j: (j, 0))] + [ANY_SPEC] * len(deps),
        out_specs=[pl.BlockSpec((tm, DA), lambda i, j: (i, j))]
        + [pl.BlockSpec((d, tm // d, DA), lambda i, j: (0, i, 0)) for d in WIDE],
        out_shape=[jax.ShapeDtypeStruct((T, D), BF16)]
        + [jax.ShapeDtypeStruct((d, T // d, DA), BF16) for d in WIDE],
        scratch_shapes=[pltpu.VMEM((DA // LANES, tm, LANES), F32)],
        compiler_params=pltpu.CompilerParams(
            dimension_semantics=("parallel", "arbitrary"), vmem_limit_bytes=VMEM_BIG),
    )(dh1b, w_out, *deps)


TM_FULL_ROW = 512


def _full_row_specs(tm):
    row = pl.BlockSpec((tm, D), lambda i, j, k: (i, 0))
    return row, pl.BlockSpec((1, D), lambda i, j, k: (0, 0))


def _mm_out(o_attn, o_conv, w_out, x, g_next):
    T = x.shape[0]
    tm = min(TM_FULL_ROW, T)
    row, vec = _full_row_specs(tm)
    return _mm(
        "mm_out", (T // tm, 1, 1),
        [pl.BlockSpec((tm, DA), lambda i, j, k: (i, 0)),
         pl.BlockSpec((DA, D), lambda i, j, k: (0, 0), pipeline_mode=RESIDENT),
         pl.BlockSpec((tm, DC), lambda i, j, k: (i, 0)),
         pl.BlockSpec((DC, D), lambda i, j, k: (1, 0), pipeline_mode=RESIDENT),
         row, vec],
        [o_attn, w_out, o_conv, w_out, x, g_next],
        [row, row],
        [jax.ShapeDtypeStruct((T, D), F32), jax.ShapeDtypeStruct((T, D), BF16)], NN, 2,
        _ep_resid_norm)


FF_TN = 512
FF_TK = 2 * N_FF


def _mm_gate_up(f, wg_t, wu_t):
    T = f.shape[0]
    tm = _row_tile(T)

    def body(f_ref, wg_ref, wu_ref, dg_ref, du_ref, a_ref):
        fv = f_ref[...]
        g = _dot(fv, wg_ref[...], NT)
        u = _dot(fv, wu_ref[...], NT)
        sg = _sigmoid(g)
        silu = g * sg
        dg_ref[...] = (u * (sg * (1.0 + g * (1.0 - sg)))).astype(BF16)
        du_ref[...] = silu.astype(BF16)
        a_ref[...] = (silu * u).astype(BF16)

    wspec = pl.BlockSpec((FF_TN, D), lambda i, j: (j, 0))
    ospec = pl.BlockSpec((tm, FF_TN), lambda i, j: (i, j))
    sh = jax.ShapeDtypeStruct((T, DFF), BF16)
    return pl.pallas_call(
        body, name="mm_gate_up", grid=(T // tm, DFF // FF_TN),
        in_specs=[pl.BlockSpec((tm, D), lambda i, j: (i, 0)), wspec, wspec],
        out_specs=[ospec, ospec, ospec], out_shape=[sh, sh, sh],
        compiler_params=pltpu.CompilerParams(
            dimension_semantics=("parallel", "parallel"), vmem_limit_bytes=VMEM_BIG),
    )(f, wg_t, wu_t)


def _mm_down(act, w_down, h1, g_next):
    T = h1.shape[0]
    tm = min(TM_FULL_ROW, T)
    row, vec = _full_row_specs(tm)
    return _mm(
        "mm_down", (T // tm, 1, DFF // FF_TK),
        [pl.BlockSpec((tm, FF_TK), lambda i, j, k: (i, k)),
         pl.BlockSpec((FF_TK, D), lambda i, j, k: (k, 0)),
         row, vec],
        [act, w_down, h1, g_next],
        [row, row],
        [jax.ShapeDtypeStruct((T, D), F32), jax.ShapeDtypeStruct((T, D), BF16)], NN, 1,
        _ep_resid_norm, acc_shape=(tm, D))


def _ple_loss(r, w_pgate, b_pgate, p, w_ple, h2, target, g_final):
    T = h2.shape[0]
    tm = min(256, T)
    nt = T // tm

    def body(r_ref, wg_ref, b_ref, p_ref, wp_ref, h2_ref, t_ref, g_ref,
             loss_ref, dh_ref, dpe_ref, dpg_ref, dgf_ref, dbp_ref, lacc, gacc, bacc):
        i = pl.program_id(0)
        gte = _sigmoid(_dot(r_ref[...], wg_ref[...], NN) + b_ref[...])
        pe = _dot(p_ref[...], wp_ref[...], NN)
        v = h2_ref[...] + pe * gte
        rr = lax.rsqrt(jnp.mean(v * v, axis=-1, keepdims=True) + EPS)
        nrm = v * rr
        g = g_ref[...]
        err = nrm * g - t_ref[...]
        dy = err * (1.0 / D)
        dn = dy * g
        dh = rr * (dn - nrm * jnp.mean(dn * nrm, axis=-1, keepdims=True))
        dh_ref[...] = dh.astype(BF16)
        dpe_ref[...] = (dh * gte).astype(BF16)
        dpg = dh * pe * gte * (1.0 - gte)
        dpg_ref[...] = dpg.astype(BF16)

        @pl.when(i == 0)
        def _():
            lacc[...] = jnp.zeros_like(lacc)
            gacc[...] = jnp.zeros_like(gacc)
            bacc[...] = jnp.zeros_like(bacc)

        lacc[...] += _fold8(err * err)
        gacc[...] += _fold8(dy * nrm)
        bacc[...] += _fold8(dpg)

        @pl.when(i == nt - 1)
        def _():
            tot = jnp.sum(jnp.sum(lacc[...], axis=0, keepdims=True), axis=1, keepdims=True)
            loss_ref[...] = jnp.broadcast_to(tot * (0.5 / D), (1, 128))
            dgf_ref[...] = jnp.sum(gacc[...], axis=0, keepdims=True)
            dbp_ref[...] = jnp.sum(bacc[...], axis=0, keepdims=True)

    row = pl.BlockSpec((tm, D), lambda i: (i, 0))
    vec = pl.BlockSpec((1, D), lambda i: (0, 0))
    return pl.pallas_call(
        body, name="ple_loss", grid=(nt,),
        in_specs=[row, pl.BlockSpec((D, D), lambda i: (0, 0), pipeline_mode=RESIDENT), vec,
                  pl.BlockSpec((tm, DPLE), lambda i: (i, 0)),
                  pl.BlockSpec((DPLE, D), lambda i: (0, 0), pipeline_mode=RESIDENT), row, row, vec],
        out_specs=[pl.BlockSpec((1, 128), lambda i: (0, 0)), row, row, row, vec, vec],
        out_shape=[jax.ShapeDtypeStruct((1, 128), F32), jax.ShapeDtypeStruct((T, D), BF16),
                   jax.ShapeDtypeStruct((T, D), BF16), jax.ShapeDtypeStruct((T, D), BF16),
                   jax.ShapeDtypeStruct((1, D), F32), jax.ShapeDtypeStruct((1, D), F32)],
        scratch_shapes=[pltpu.VMEM((8, D), F32)] * 3,
        compiler_params=pltpu.CompilerParams(
            dimension_semantics=("arbitrary",), vmem_limit_bytes=VMEM_BIG),
    )(r, w_pgate, b_pgate, p, w_ple, h2, target, g_final)


def _mm_down_bwd(dh2, w_down, g, u, deps=()):
    T = dh2.shape[0]
    tm = _tn_rows(T)
    gspec = pl.BlockSpec((tm, FF_TN), lambda i, j, k: (i, j))
    sh = jax.ShapeDtypeStruct((T, DFF), BF16)
    return _mm(
        "mm_down_bwd", (T // tm, DFF // FF_TN, 1),
        [pl.BlockSpec((tm, D), lambda i, j, k: (i, 0)),
         pl.BlockSpec((FF_TN, D), lambda i, j, k: (j, 0)),
         gspec, gspec],
        [dh2, w_down, g, u],
        [gspec, gspec], [sh, sh], NT, 1, _ep_swiglu_bwd, deps=deps)


def _mm_ffn_in_bwd(dg, wg_t, du, wu_t, deps=()):
    T = dg.shape[0]
    tm = _row_tile(T)
    tn = 1024
    aspec = pl.BlockSpec((tm, FF_TK), lambda i, j, k: (i, k))
    wspec = pl.BlockSpec((FF_TK, tn), lambda i, j, k: (k, j))
    return _mm(
        "mm_ffn_in_bwd", (T // tm, D // tn, DFF // FF_TK),
        [aspec, wspec, aspec, wspec], [dg, wg_t, du, wu_t],
        [pl.BlockSpec((tm, tn), lambda i, j, k: (i, j))],
        [jax.ShapeDtypeStruct((T, D), BF16)], NN, 2, _ep_cast(BF16), acc_shape=(tm, tn),
        deps=deps)[0]


def _mm_in_bwd(dz, w_in, x, g, dres, deps=()):
    T = dz.shape[0]
    tm = min(TM_FULL_ROW, T)
    tk = DIN // 4
    row, vec = _full_row_specs(tm)
    return _mm(
        "mm_in_bwd", (T // tm, 1, DIN // tk),
        [pl.BlockSpec((tm, tk), lambda i, j, k: (i, k)),
         pl.BlockSpec((D, tk), lambda i, j, k: (0, k)),
         row, vec, row],
        [dz, w_in, x, g, dres],
        [row, vec],
        [jax.ShapeDtypeStruct((T, D), F32), jax.ShapeDtypeStruct((1, D), F32)], NT, 1,
        _ep_rms_bwd(T // tm), acc_shape=(tm, D), deps=deps,
        carry=[pltpu.VMEM((8, D), F32)])


def _mm_pgate_bwd(dpg, w_pgate, h2, g, dres, deps=()):
    T = dpg.shape[0]
    tm = min(TM_FULL_ROW, T)
    row, vec = _full_row_specs(tm)
    return _mm(
        "mm_pgate_bwd", (T // tm, 1, 1),
        [row, pl.BlockSpec((D, D), lambda i, j, k: (0, 0), pipeline_mode=RESIDENT), row, vec, row],
        [dpg, w_pgate, h2, g, dres],
        [row, vec],
        [jax.ShapeDtypeStruct((T, D), BF16), jax.ShapeDtypeStruct((1, D), F32)], NT, 1,
        _ep_rms_bwd(T // tm), deps=deps, carry=[pltpu.VMEM((8, D), F32)])


def _mm_tn(name, a, b, tj=None):
    T, idim = a.shape
    jdim = b.shape[1]
    tt = _row_tile(T)
    ti = min(idim, 1024)
    tj = jdim if tj is None else tj
    return _mm(
        name, (idim // ti, jdim // tj, T // tt),
        [pl.BlockSpec((tt, ti), lambda i, j, k: (k, i)),
         pl.BlockSpec((tt, tj), lambda i, j, k: (k, j))],
        [a, b],
        [pl.BlockSpec((ti, tj), lambda i, j, k: (i, j))],
        [jax.ShapeDtypeStruct((idim, jdim), BF16)], TN, 1, _ep_cast(BF16), acc_shape=(ti, tj))[0]


def _mm_tn_wide(name, a, b):
    T = a.shape[0]
    jdim = b.shape[1]
    tt = _row_tile(T)
    return _mm(
        name, (1, jdim // IN_TN, T // tt),
        [pl.BlockSpec((tt, D), lambda i, j, k: (k, 0)),
         pl.BlockSpec((tt, IN_TN), lambda i, j, k: (k, j))],
        [a, b],
        [pl.BlockSpec((D, IN_TN), lambda i, j, k: (0, j))],
        [jax.ShapeDtypeStruct((D, jdim), BF16)], TN, 1, _ep_cast(BF16),
        acc_shape=(D, IN_TN))[0]


def _mm_tn_ff(name, a, b):
    T = b.shape[0]
    tt = _tn_rows(T)
    return _mm(
        name, (DFF // FF_TN, 1, T // tt),
        [pl.BlockSpec((tt, FF_TN), lambda i, j, k: (k, i)),
         pl.BlockSpec((tt, D), lambda i, j, k: (k, 0))],
        [a, b],
        [pl.BlockSpec((FF_TN, D), lambda i, j, k: (i, 0))],
        [jax.ShapeDtypeStruct((DFF, D), BF16)], TN, 1, _ep_cast(BF16),
        acc_shape=(FF_TN, D))[0]


TR = 256


def _rows(T):
    return min(TR, T)


def _fold8(v):
    return jnp.sum(v.reshape(v.shape[0] // 8, 8, v.shape[1]), axis=0)


def _rms_bwd(name, dn_out, h, g, dres):
    T = h.shape[0]
    tr = _rows(T)
    nt = T // tr

    def body(dy_ref, h_ref, g_ref, dres_ref, dh_ref, dg_ref, acc):
        i = pl.program_id(0)
        v = h_ref[...]
        r = lax.rsqrt(jnp.mean(v * v, axis=-1, keepdims=True) + EPS)
        nrm = v * r
        dy = dy_ref[...].astype(F32)
        dn = dy * g_ref[...]
        dh = dres_ref[...].astype(F32) + r * (dn - nrm * jnp.mean(dn * nrm, axis=-1, keepdims=True))
        dh_ref[...] = dh.astype(BF16)

        @pl.when(i == 0)
        def _():
            acc[...] = jnp.zeros_like(acc)

        acc[...] += _fold8(dy * nrm)

        @pl.when(i == nt - 1)
        def _():
            dg_ref[...] = jnp.sum(acc[...], axis=0, keepdims=True)

    tile = pl.BlockSpec((tr, D), lambda i: (i, 0))
    vec = pl.BlockSpec((1, D), lambda i: (0, 0))
    return pl.pallas_call(
        body, name=name, grid=(nt,),
        in_specs=[tile, tile, vec, tile], out_specs=[tile, vec],
        out_shape=[jax.ShapeDtypeStruct((T, D), BF16), jax.ShapeDtypeStruct((1, D), F32)],
        scratch_shapes=[pltpu.VMEM((8, D), F32)],
        compiler_params=pltpu.CompilerParams(dimension_semantics=("arbitrary",)),
    )(dn_out, h, g, dres)


def _band_masks():
    qi = lax.broadcasted_iota(jnp.int32, (BLK, BLK), 0)
    kj = lax.broadcasted_iota(jnp.int32, (BLK, BLK), 1)
    return kj >= qi, kj <= qi


AQ = 4
ATTN_PARAMS = pltpu.CompilerParams(
    dimension_semantics=("parallel", "parallel"), vmem_limit_bytes=VMEM_BIG)


def _cm_spec(d, col, nblk, rowmap=lambda n: n):
    if d == 1:
        return pl.BlockSpec((nblk * BLK, DA), lambda r, n: (rowmap(n), col))
    return pl.BlockSpec((None, nblk * BLK, DA), lambda r, n: (r, rowmap(n), col))


def _cm_shape(d, T, dtype):
    return jax.ShapeDtypeStruct((T, DA) if d == 1 else (d, T // d, DA), dtype)


def _blk(b):
    return slice(b * BLK, (b + 1) * BLK)


HEADS = tuple(slice(h * DH, (h + 1) * DH) for h in range(NH))


def _attn_fwd(name, zsrc, d, T):
    nb = T // d // BLK
    aq = min(AQ, nb)
    scale = DH ** -0.5

    def body(q_ref, kp_ref, kc_ref, vp_ref, vc_ref, o_ref, l_ref):
        n = pl.program_id(1)
        band_prev, cur_ok = _band_masks()
        for b in range(aq):
            kp = (lambda sl: kp_ref[:, sl]) if b == 0 else (lambda sl, b=b: kc_ref[_blk(b - 1), sl])
            vp = (lambda sl: vp_ref[:, sl]) if b == 0 else (lambda sl, b=b: vc_ref[_blk(b - 1), sl])
            prev_ok = band_prev & (n > 0) if b == 0 else band_prev
            rows = _blk(b)
            s = [(jnp.where(prev_ok, _dot(q_ref[rows, sl], kp(sl), NT) * scale, NEG),
                  jnp.where(cur_ok, _dot(q_ref[rows, sl], kc_ref[rows, sl], NT) * scale, NEG))
                 for sl in HEADS]
            m = [jnp.maximum(jnp.max(sp, axis=1, keepdims=True), jnp.max(sc, axis=1, keepdims=True))
                 for sp, sc in s]
            p = [(jnp.exp(sp - mh), jnp.exp(sc - mh)) for (sp, sc), mh in zip(s, m)]
            den = [jnp.sum(pp, axis=1, keepdims=True) + jnp.sum(pc, axis=1, keepdims=True)
                   for pp, pc in p]
            o = [_dot(pp.astype(BF16), vp(sl), NN) + _dot(pc.astype(BF16), vc_ref[rows, sl], NN)
                 for (pp, pc), sl in zip(p, HEADS)]
            o_ref[rows, :] = jnp.concatenate(
                [(oh / dh).astype(BF16) for oh, dh in zip(o, den)], axis=1)
            l_ref[rows, :] = jnp.concatenate(
                [jnp.broadcast_to(mh + jnp.log(dh), (BLK, DH)) for mh, dh in zip(m, den)], axis=1)

    halo = lambda n: jnp.maximum(aq * n - 1, 0)
    return pl.pallas_call(
        body, name=name, grid=(d, nb // aq),
        in_specs=[_cm_spec(d, 0, aq), _cm_spec(d, 1, 1, halo), _cm_spec(d, 1, aq),
                  _cm_spec(d, 2, 1, halo), _cm_spec(d, 2, aq)],
        out_specs=[_cm_spec(d, 0, aq)] * 2,
        out_shape=[_cm_shape(d, T, BF16), _cm_shape(d, T, F32)],
        compiler_params=ATTN_PARAMS,
    )(zsrc, zsrc, zsrc, zsrc, zsrc)


def _cm_tile(d, tr):
    if d == 1:
        return pl.BlockSpec((tr, DA), lambda i: (i, 0))
    return pl.BlockSpec((d, tr // d, DA), lambda i: (0, i, 0))


def _attn_combine(outs, lses, T):
    tr = _rows(T)

    def body(*refs):
        o_in, l_in = refs[:3], refs[3:6]
        o_ref, l_ref = refs[6:8]
        o_cm, l_cm = refs[8:8 + len(WIDE)], refs[8 + len(WIDE):8 + 2 * len(WIDE)]
        so, sl, so_all, sl_all = refs[8 + 2 * len(WIDE):]
        for c in range(DA // LANES):
            lt = _lane_tile(c)
            os_, ls_ = [o_in[0][:, lt].astype(F32)], [l_in[0][:, lt]]
            for w, d in enumerate(WIDE):
                for r in range(d):
                    so[w, c, pl.ds(r, tr // d, stride=d), :] = o_in[1 + w][r, :, lt].astype(F32)
                    sl[w, c, pl.ds(r, tr // d, stride=d), :] = l_in[1 + w][r, :, lt]
                os_.append(so[w, c])
                ls_.append(sl[w, c])
            la, lb, lc = ls_
            m = jnp.maximum(jnp.maximum(la, lb), lc)
            ea, eb, ec = jnp.exp(la - m), jnp.exp(lb - m), jnp.exp(lc - m)
            s = ea + eb + ec
            o = (ea * os_[0] + eb * os_[1] + ec * os_[2]) / s
            lse = m + jnp.log(s)
            o_ref[:, lt] = o.astype(BF16)
            l_ref[:, lt] = lse
            so_all[c] = o
            sl_all[c] = lse
        _emit_class_major(so_all, o_cm, tr)
        _emit_class_major(sl_all, l_cm, tr)

    specs = [_cm_tile(d, tr) for d in DILATIONS]
    wide = [_cm_tile(d, tr) for d in WIDE]
    return pl.pallas_call(
        body, name="attn_combine", grid=(T // tr,),
        in_specs=specs + specs,
        out_specs=[specs[0], specs[0]] + wide + wide,
        out_shape=[_cm_shape(1, T, BF16), _cm_shape(1, T, F32)]
        + [_cm_shape(d, T, BF16) for d in WIDE] + [_cm_shape(d, T, F32) for d in WIDE],
        scratch_shapes=[pltpu.VMEM((len(WIDE), DA // LANES, tr, LANES), F32)] * 2
        + [pltpu.VMEM((DA // LANES, tr, LANES), F32)] * 2,
        compiler_params=pltpu.CompilerParams(
            dimension_semantics=("parallel",), vmem_limit_bytes=VMEM_MID),
    )(*outs, *lses)


def _attn_bwd_q(name, zsrc, dosrc, osrc, lsrc, d, T):
    nb = T // d // BLK
    aq = min(AQ, nb)
    scale = DH ** -0.5

    def body(q_ref, kp_ref, kc_ref, vp_ref, vc_ref, do_ref, o_ref, l_ref, dq_ref):
        n = pl.program_id(1)
        band_prev, cur_ok = _band_masks()
        for b in range(aq):
            kp = (lambda sl: kp_ref[:, sl]) if b == 0 else (lambda sl, b=b: kc_ref[_blk(b - 1), sl])
            vp = (lambda sl: vp_ref[:, sl]) if b == 0 else (lambda sl, b=b: vc_ref[_blk(b - 1), sl])
            prev_ok = band_prev & (n > 0) if b == 0 else band_prev
            rows = _blk(b)
            s = [(_dot(q_ref[rows, sl], kp(sl), NT), _dot(q_ref[rows, sl], kc_ref[rows, sl], NT))
                 for sl in HEADS]
            dp = [(_dot(do_ref[rows, sl], vp(sl), NT), _dot(do_ref[rows, sl], vc_ref[rows, sl], NT))
                  for sl in HEADS]
            delta = [jnp.sum(do_ref[rows, sl].astype(F32) * o_ref[rows, sl].astype(F32), axis=1,
                             keepdims=True) for sl in HEADS]
            p = [(jnp.exp(jnp.where(prev_ok, sp * scale - l_ref[rows, sl], NEG)),
                  jnp.exp(jnp.where(cur_ok, sc * scale - l_ref[rows, sl], NEG)))
                 for (sp, sc), sl in zip(s, HEADS)]
            ds = [((pp * (dpp - dl) * scale).astype(BF16), (pc * (dpc - dl) * scale).astype(BF16))
                  for (pp, pc), (dpp, dpc), dl in zip(p, dp, delta)]
            dq = [_dot(dsp, kp(sl), NN) + _dot(dsc, kc_ref[rows, sl], NN)
                  for (dsp, dsc), sl in zip(ds, HEADS)]
            dq_ref[rows, :] = jnp.concatenate([v.astype(BF16) for v in dq], axis=1)

    halo = lambda n: jnp.maximum(aq * n - 1, 0)
    own = _cm_spec(d, 0, aq)
    return pl.pallas_call(
        body, name=name, grid=(d, nb // aq),
        in_specs=[own, _cm_spec(d, 1, 1, halo), _cm_spec(d, 1, aq), _cm_spec(d, 2, 1, halo),
                  _cm_spec(d, 2, aq), own, own, own],
        out_specs=own, out_shape=_cm_shape(d, T, BF16),
        compiler_params=ATTN_PARAMS,
    )(zsrc, zsrc, zsrc, zsrc, zsrc, dosrc, osrc, lsrc)


def _attn_bwd_kv(name, zsrc, dosrc, osrc, lsrc, d, T):
    nb = T // d // BLK
    aq = min(AQ, nb)
    nsteps = nb // aq
    scale = DH ** -0.5

    def body(k_ref, v_ref, q_ref, qn_ref, do_ref, don_ref, o_ref, on_ref, l_ref, ln_ref,
             dk_ref, dv_ref):
        j = pl.program_id(1)
        band_next, own_ok = _band_masks()
        for b in range(aq):
            rows = _blk(b)
            last = b == aq - 1
            pick = lambda cur, halo: ((lambda sl: halo[:, sl]) if last
                                      else (lambda sl, b=b: cur[_blk(b + 1), sl]))
            qb, dob, ob, lb = (pick(q_ref, qn_ref), pick(do_ref, don_ref), pick(o_ref, on_ref),
                               pick(l_ref, ln_ref))
            next_ok = band_next & (j < nsteps - 1) if last else band_next
            s = [(_dot(q_ref[rows, sl], k_ref[rows, sl], NT), _dot(qb(sl), k_ref[rows, sl], NT))
                 for sl in HEADS]
            dp = [(_dot(do_ref[rows, sl], v_ref[rows, sl], NT), _dot(dob(sl), v_ref[rows, sl], NT))
                  for sl in HEADS]
            delta = [(jnp.sum(do_ref[rows, sl].astype(F32) * o_ref[rows, sl].astype(F32), axis=1,
                              keepdims=True),
                      jnp.sum(dob(sl).astype(F32) * ob(sl).astype(F32), axis=1, keepdims=True))
                     for sl in HEADS]
            p = [(jnp.exp(jnp.where(own_ok, sa * scale - l_ref[rows, sl], NEG)),
                  jnp.exp(jnp.where(next_ok, sb * scale - lb(sl), NEG)))
                 for (sa, sb), sl in zip(s, HEADS)]
            dv = [_dot(pa.astype(BF16), do_ref[rows, sl], TN) + _dot(pb.astype(BF16), dob(sl), TN)
                  for (pa, pb), sl in zip(p, HEADS)]
            ds = [((pa * (dpa - da) * scale).astype(BF16), (pb * (dpb - db) * scale).astype(BF16))
                  for (pa, pb), (dpa, dpb), (da, db) in zip(p, dp, delta)]
            dk = [_dot(dsa, q_ref[rows, sl], TN) + _dot(dsb, qb(sl), TN)
                  for (dsa, dsb), sl in zip(ds, HEADS)]
            dk_ref[rows, :] = jnp.concatenate([v.astype(BF16) for v in dk], axis=1)
            dv_ref[rows, :] = jnp.concatenate([v.astype(BF16) for v in dv], axis=1)

    halo = lambda j: jnp.minimum(aq * (j + 1), nb - 1)
    own, own_n = _cm_spec(d, 0, aq), _cm_spec(d, 0, 1, halo)
    sh = _cm_shape(d, T, BF16)
    return pl.pallas_call(
        body, name=name, grid=(d, nsteps),
        in_specs=[_cm_spec(d, 1, aq), _cm_spec(d, 2, aq), own, own_n, own, own_n, own, own_n,
                  own, own_n],
        out_specs=[own, own], out_shape=[sh, sh],
        compiler_params=ATTN_PARAMS,
    )(zsrc, zsrc, zsrc, zsrc, dosrc, dosrc, osrc, osrc, lsrc, lsrc)


def _dz_assemble(dqs, dks, dvs, dcvg, T):
    tr = _rows(T)
    nb = len(DILATIONS)

    def body(*refs):
        cvg_ref, dz_ref, scr = refs[3 * nb], refs[3 * nb + 1], refs[3 * nb + 2]
        for g in range(3):
            parts = refs[g * nb:(g + 1) * nb]
            for c in range(DA // LANES):
                lt = _lane_tile(c)
                scr[g, c] = parts[0][:, lt].astype(F32)
                for w, d in enumerate(WIDE):
                    for r in range(d):
                        rows = pl.ds(r, tr // d, stride=d)
                        scr[g, c, rows, :] = scr[g, c, rows, :] + parts[1 + w][r, :, lt].astype(F32)
                dz_ref[:, g * DA + c * LANES:g * DA + (c + 1) * LANES] = scr[g, c].astype(BF16)
        dz_ref[:, 3 * DA:] = cvg_ref[...]

    specs = [_cm_tile(d, tr) for d in DILATIONS]
    return pl.pallas_call(
        body, name="dz_assemble", grid=(T // tr,),
        in_specs=specs * 3 + [pl.BlockSpec((tr, 2 * DC), lambda i: (i, 0))],
        out_specs=pl.BlockSpec((tr, DIN), lambda i: (i, 0)),
        out_shape=jax.ShapeDtypeStruct((T, DIN), BF16),
        scratch_shapes=[pltpu.VMEM((3, DA // LANES, tr, LANES), F32)],
        compiler_params=pltpu.CompilerParams(
            dimension_semantics=("parallel",), vmem_limit_bytes=VMEM_MID),
    )(*dqs, *dks, *dvs, dcvg)


CT = 256
HALO = 32
RC_FWD = 128
RC_BWD = 64


def _conv_fwd(z, w_dw, b_dw, g_ln, b_ln):
    T = z.shape[0]
    ct = min(CT, T)
    RC = RC_FWD
    nt = T // ct
    hb = ct // HALO

    def body(cv_ref, cg_ref, cvp_ref, cgp_ref, w_ref, bdw_ref, g_ref, b_ref, oc_ref, y_ref, ubuf, ush):
        i = pl.program_id(0)
        up = cvp_ref[...].astype(F32) * _sigmoid(cgp_ref[...].astype(F32))
        ubuf[0:HALO, :] = jnp.where(i > 0, up, 0.0)
        ubuf[HALO:, :] = cv_ref[...].astype(F32) * _sigmoid(cg_ref[...].astype(F32))
        for b in range(8):
            ush[b] = ubuf[pl.ds(8 - b, ct + 24), :]

        def chunk(ci, carry):
            r0 = pl.multiple_of(ci * RC, RC)
            acc = jnp.broadcast_to(bdw_ref[...], (RC, DC))
            for s in range(CW):
                a, b = divmod(s, 8)
                acc = acc + w_ref[CW - 1 - s:CW - s, :] * ush[b, pl.ds(r0 + 24 - 8 * a, RC), :]
            y_ref[pl.ds(r0, RC), :] = acc
            mu = jnp.mean(acc, axis=-1, keepdims=True)
            cen = acc - mu
            var = jnp.mean(cen * cen, axis=-1, keepdims=True)
            ln = cen * lax.rsqrt(var + EPS) * g_ref[...] + b_ref[...]
            oc_ref[pl.ds(r0, RC), :] = (ln * _sigmoid(ln)).astype(BF16)
            return carry

        lax.fori_loop(0, ct // RC, chunk, 0)

    cur = lambda col: pl.BlockSpec((ct, DC), lambda i: (i, col))
    prv = lambda col: pl.BlockSpec((HALO, DC), lambda i: (jnp.maximum(i * hb - 1, 0), col))
    vec = pl.BlockSpec((1, DC), lambda i: (0, 0))
    return pl.pallas_call(
        body, name="conv_fwd", grid=(nt,),
        in_specs=[cur(3), cur(4), prv(3), prv(4), pl.BlockSpec((CW, DC), lambda i: (0, 0)),
                  vec, vec, vec],
        out_specs=[pl.BlockSpec((ct, DC), lambda i: (i, 0))] * 2,
        out_shape=[jax.ShapeDtypeStruct((T, DC), BF16), jax.ShapeDtypeStruct((T, DC), F32)],
        scratch_shapes=[pltpu.VMEM((ct + HALO, DC), F32), pltpu.VMEM((8, ct + 24, DC), F32)],
        compiler_params=pltpu.CompilerParams(
            dimension_semantics=("parallel",), vmem_limit_bytes=VMEM_MID),
    )(z, z, z, z, w_dw, b_dw, g_ln, b_ln)


def _conv_bwd(z, dom, y, w_dw, g_ln, b_ln):
    T = z.shape[0]
    ct = min(CT, T)
    RC = RC_BWD
    nt = T // ct
    hb = ct // HALO
    last_halo = T // HALO - 1

    def ln_bwd(yv, dov, g_ref, b_ref):
        mu = jnp.mean(yv, axis=-1, keepdims=True)
        cen = yv - mu
        rstd = lax.rsqrt(jnp.mean(cen * cen, axis=-1, keepdims=True) + EPS)
        xhat = cen * rstd
        ln = xhat * g_ref[...] + b_ref[...]
        sg = _sigmoid(ln)
        dln = dov * (sg * (1.0 + ln * (1.0 - sg)))
        dxh = dln * g_ref[...]
        dy = rstd * (dxh - jnp.mean(dxh, axis=-1, keepdims=True)
                     - xhat * jnp.mean(dxh * xhat, axis=-1, keepdims=True))
        return dy, dln, xhat

    def body(do_ref, don_ref, y_ref, yn_ref, cv_ref, cg_ref, cvp_ref, cgp_ref, w_ref, g_ref, b_ref,
             dcvg_ref, dw_ref, dbdw_ref, dg_ref, db_ref,
             dybuf, dysh, ubuf, ush, dwacc, vacc):
        i = pl.program_id(0)

        @pl.when(i == 0)
        def _():
            dwacc[...] = jnp.zeros_like(dwacc)
            vacc[...] = jnp.zeros_like(vacc)

        def ln_chunk(ci, carry):
            r0 = pl.multiple_of(ci * RC, RC)
            dy, dln, xhat = ln_bwd(y_ref[pl.ds(r0, RC), :], do_ref[pl.ds(r0, RC), :].astype(F32),
                                   g_ref, b_ref)
            dybuf[pl.ds(r0, RC), :] = dy
            vacc[0] += _fold8(dy)
            vacc[1] += _fold8(dln * xhat)
            vacc[2] += _fold8(dln)
            return carry

        lax.fori_loop(0, ct // RC, ln_chunk, 0)
        dyn, _, _ = ln_bwd(yn_ref[...], don_ref[...].astype(F32), g_ref, b_ref)
        dybuf[ct:, :] = jnp.where(i < nt - 1, dyn, 0.0)
        for b in range(8):
            dysh[b] = dybuf[pl.ds(b, ct + 24), :]

        up = cvp_ref[...].astype(F32) * _sigmoid(cgp_ref[...].astype(F32))
        ubuf[0:HALO, :] = jnp.where(i > 0, up, 0.0)
        ubuf[HALO:, :] = cv_ref[...].astype(F32) * _sigmoid(cg_ref[...].astype(F32))
        for b in range(8):
            ush[b] = ubuf[pl.ds(8 - b, ct + 24), :]

        def chunk(ci, carry):
            r0 = pl.multiple_of(ci * RC, RC)
            dy = dybuf[pl.ds(r0, RC), :]
            du = jnp.zeros((RC, DC), F32)
            for s in range(CW):
                a, b = divmod(s, 8)
                du = du + w_ref[CW - 1 - s:CW - s, :] * dysh[b, pl.ds(r0 + 8 * a, RC), :]
                dwacc[CW - 1 - s] += _fold8(dy * ush[b, pl.ds(r0 + 24 - 8 * a, RC), :])
            cv = cv_ref[pl.ds(r0, RC), :].astype(F32)
            sg = _sigmoid(cg_ref[pl.ds(r0, RC), :].astype(F32))
            dcvg_ref[pl.ds(r0, RC), 0:DC] = (du * sg).astype(BF16)
            dcvg_ref[pl.ds(r0, RC), DC:2 * DC] = (du * cv * sg * (1.0 - sg)).astype(BF16)
            return carry

        lax.fori_loop(0, ct // RC, chunk, 0)

        @pl.when(i == nt - 1)
        def _():
            dw_ref[...] = jnp.sum(dwacc[...], axis=1)
            dbdw_ref[...] = jnp.sum(vacc[0], axis=0, keepdims=True)
            dg_ref[...] = jnp.sum(vacc[1], axis=0, keepdims=True)
            db_ref[...] = jnp.sum(vacc[2], axis=0, keepdims=True)

    cur = lambda col: pl.BlockSpec((ct, DC), lambda i: (i, col))
    prv = lambda col: pl.BlockSpec((HALO, DC), lambda i: (jnp.maximum(i * hb - 1, 0), col))
    nxt = lambda col: pl.BlockSpec((HALO, DC), lambda i: (jnp.minimum((i + 1) * hb, last_halo), col))
    vec = pl.BlockSpec((1, DC), lambda i: (0, 0))
    tile = pl.BlockSpec((ct, DC), lambda i: (i, 0))
    return pl.pallas_call(
        body, name="conv_bwd", grid=(nt,),
        in_specs=[cur(1), nxt(1), cur(0), nxt(0), cur(3), cur(4), prv(3), prv(4),
                  pl.BlockSpec((CW, DC), lambda i: (0, 0)), vec, vec],
        out_specs=[pl.BlockSpec((ct, 2 * DC), lambda i: (i, 0)),
                   pl.BlockSpec((CW, DC), lambda i: (0, 0)), vec, vec, vec],
        out_shape=[jax.ShapeDtypeStruct((T, 2 * DC), BF16),
                   jax.ShapeDtypeStruct((CW, DC), F32), jax.ShapeDtypeStruct((1, DC), F32),
                   jax.ShapeDtypeStruct((1, DC), F32), jax.ShapeDtypeStruct((1, DC), F32)],
        scratch_shapes=[pltpu.VMEM((ct + HALO, DC), F32), pltpu.VMEM((8, ct + 24, DC), F32),
                        pltpu.VMEM((ct + HALO, DC), F32), pltpu.VMEM((8, ct + 24, DC), F32),
                        pltpu.VMEM((CW, 8, DC), F32), pltpu.VMEM((3, 8, DC), F32)],
        compiler_params=pltpu.CompilerParams(
            dimension_semantics=("arbitrary",), vmem_limit_bytes=VMEM_BIG),
    )(dom, dom, y, y, z, z, z, z, w_dw, g_ln, b_ln)


def _adam_math(w, g, m, v):
    m = ADAM_B1 * m + (1.0 - ADAM_B1) * g
    v = ADAM_B2 * v + (1.0 - ADAM_B2) * (g * g)
    m_hat = m / (1.0 - ADAM_B1 ** ADAM_STEP)
    v_hat = v / (1.0 - ADAM_B2 ** ADAM_STEP)
    delta = -ADAM_LR * (m_hat / (jnp.sqrt(v_hat) + ADAM_EPS) + ADAM_WD * w)
    return delta, m, v


def _adam(name, slots, w, m, v):
    rows, cols = w.shape
    tr = next(t for t in (256, 176, 128, 64, 32, 16, 8, rows) if rows % t == 0)

    def body(s_ref, w_ref, m_ref, v_ref, g_out, d_out, m_out, v_out):
        g = s_ref[0].astype(F32)
        for s in range(1, NDEV):
            g = g + s_ref[s].astype(F32)
        delta, mn, vn = _adam_math(w_ref[...], g, m_ref[...], v_ref[...])
        g_out[...] = g
        d_out[...] = delta
        m_out[...] = mn
        v_out[...] = vn

    tile = pl.BlockSpec((tr, cols), lambda i: (i, 0))
    sh = jax.ShapeDtypeStruct((rows, cols), F32)
    return pl.pallas_call(
        body, name=name, grid=(rows // tr,),
        in_specs=[pl.BlockSpec((NDEV, tr, cols), lambda i: (0, i, 0)), tile, tile, tile],
        out_specs=[tile] * 4, out_shape=[sh] * 4,
        compiler_params=pltpu.CompilerParams(
            dimension_semantics=("parallel",), vmem_limit_bytes=VMEM_MID),
    )(slots, w, m, v)


SMALL_NAMES = ("g_mix", "b_dw", "g_conv_ln", "b_conv_ln", "g_ffn", "g_ple", "b_pgate", "g_final")


def _pack_small(vecs, w_dw_full, last=None):
    widen = lambda v: jnp.pad(v.reshape(1, -1), ((0, 0), (0, SMALL_W - v.size)))
    rows = [widen(v) for v in vecs]
    rows.append(jnp.pad(w_dw_full, ((0, 0), (0, SMALL_W - DC))))
    rows.append(jnp.zeros((SMALL_ROWS - len(vecs) - CW, SMALL_W), F32) if last is None else widen(last))
    return jnp.concatenate(rows, axis=0)


def kernel(x, p, g_mix, w_in, w_dw, b_dw, g_conv_ln, b_conv_ln, w_out, g_ffn, w_gate, w_up, w_down, g_ple, w_pgate, b_pgate, w_ple, g_final, loss_target, m_g_mix, m_w_in, m_w_dw, m_b_dw, m_g_conv_ln, m_b_conv_ln, m_w_out, m_g_ffn, m_w_gate, m_w_up, m_w_down, m_g_ple, m_w_pgate, m_b_pgate, m_w_ple, m_g_final, v_g_mix, v_w_in, v_w_dw, v_b_dw, v_g_conv_ln, v_b_conv_ln, v_w_out, v_g_ffn, v_w_gate, v_w_up, v_w_down, v_g_ple, v_w_pgate, v_b_pgate, v_w_ple, v_g_final):
    T = x.shape[1]
    me = 4 * lax.axis_index("x") + 2 * lax.axis_index("y") + lax.axis_index("c")
    xs = x.reshape(T, D)
    ps = p.reshape(T, DPLE).astype(BF16)
    tgt = loss_target.reshape(T, D)
    g_final2 = g_final.reshape(1, D)

    tr_names = ("w_gate", "w_up")
    big = dict(w_in=w_in[0], w_out=w_out[0], w_gate=w_gate[0].T, w_up=w_up[0].T, w_down=w_down[0],
               w_pgate=w_pgate[0], w_ple=w_ple[0])
    order = ("w_in", "w_out", "w_gate", "w_up", "w_down", "w_pgate", "w_ple")
    w_dw_g, w_in_s = _gather_two_level(
        "gather_first", [w_dw.reshape(CW, DC // NDEV), big["w_in"].astype(BF16)])
    w_dw_f = w_dw_g.transpose(1, 0, 2).reshape(CW, DC)
    later = order[1:]
    lands = _place("gather_place", [(big[n], False) for n in later], dtype=BF16)
    g_handles, g_token = _xstart("gather_start", [(None, False)] * len(later), lands, deps=[w_in_s])
    w_in_f = _shards_to_cols("w_in_natural", w_in_s)
    G = dict(zip(later, g_handles))

    a, z, *z_wide = _mm_in(xs, g_mix, w_in_f, deps=[g_token])
    zsrc = dict(zip(DILATIONS, [z] + z_wide))
    br = [_attn_fwd(f"attn_fwd_d{d}", zsrc[d], d, T) for d in DILATIONS]
    comb = list(_attn_combine([b[0] for b in br], [b[1] for b in br], T))
    o_attn, lse = comb[0], comb[1]
    osrc = dict(zip(DILATIONS, [o_attn] + comb[2:2 + len(WIDE)]))
    lsrc = dict(zip(DILATIONS, [lse] + comb[2 + len(WIDE):]))
    o_conv, y_conv = _conv_fwd(z, w_dw_f, b_dw, g_conv_ln, b_conv_ln)
    w_out_f = _xwait("gather_wait_w_out", G["w_out"], o_conv).reshape(D, D)
    h1, f = _mm_out(o_attn, o_conv, w_out_f, xs, g_ffn)
    w_gate_f = _xwait("gather_wait_w_gate", G["w_gate"], f).reshape(DFF, D)
    w_up_f = _xwait("gather_wait_w_up", G["w_up"], f).reshape(DFF, D)
    gate, up, act = _mm_gate_up(f, w_gate_f, w_up_f)
    w_down_f = _xwait("gather_wait_w_down", G["w_down"], act).reshape(DFF, D)
    h2, r = _mm_down(act, w_down_f, h1, g_ple)
    w_pgate_f = _xwait("gather_wait_w_pgate", G["w_pgate"], r).reshape(D, D)
    w_ple_f = _xwait("gather_wait_w_ple", G["w_ple"], r).transpose(1, 0, 2).reshape(DPLE, D)

    loss_part, dh3, dpe, dpg, d_g_final, d_b_pgate = _ple_loss(
        r, w_pgate_f, b_pgate, ps, w_ple_f, h2, tgt, g_final2)
    H = {}

    def send_grads(tag, named):
        items = [(v, True) for _, v in named]
        handles, token = _xstart(f"grads_start_{tag}", items, _place(f"grads_place_{tag}", items))
        H.update(zip([n for n, _ in named], handles))
        return token

    gw_pgate = _mm_tn("gw_pgate", r, dpg).reshape(NDEV, D // NDEV, D)
    gw_ple = _mm_tn("gw_ple", ps, dpe).reshape(DPLE, NDEV, D // NDEV).transpose(1, 0, 2)
    tok = send_grads("ple", [("w_pgate", gw_pgate), ("w_ple", gw_ple)])
    dh2b, d_g_ple = _mm_pgate_bwd(dpg, w_pgate_f, h2, g_ple, dh3, deps=[tok])
    ff_shards = lambda g: g.reshape(NDEV, N_FF, D)
    gw_down = ff_shards(_mm_tn_ff("gw_down", act, dh2b))
    tok = send_grads("down", [("w_down", gw_down)])
    dgate, dup = _mm_down_bwd(dh2b, w_down_f, gate, up, deps=[tok])
    gw_gate = ff_shards(_mm_tn_ff("gw_gate", dgate, f))
    gw_up = ff_shards(_mm_tn_ff("gw_up", dup, f))
    tok = send_grads("ffn", [("w_gate", gw_gate), ("w_up", gw_up)])
    df = _mm_ffn_in_bwd(dgate, w_gate_f, dup, w_up_f, deps=[tok])
    dh1b, d_g_ffn = _rms_bwd("rms_ffn_bwd", df, h1, g_ffn, dh2b)
    gw_out = jnp.concatenate(
        [_mm_tn("gw_out_attn", o_attn, dh1b), _mm_tn("gw_out_conv", o_conv, dh1b)], axis=0)
    tok = send_grads("out", [("w_out", gw_out.reshape(NDEV, D // NDEV, D))])
    dom, *do_wide = _mm_out_bwd(dh1b, w_out_f, deps=[tok])
    dosrc = dict(zip(DILATIONS, [dom] + do_wide))
    dcvg, d_w_dw, d_b_dw, d_g_ln, d_b_ln = _conv_bwd(z, dom, y_conv, w_dw_f, g_conv_ln, b_conv_ln)
    dqs, dks, dvs = [], [], []
    for d in DILATIONS:
        dqs.append(_attn_bwd_q(f"attn_bwd_q_d{d}", zsrc[d], dosrc[d], osrc[d], lsrc[d], d, T))
        dk, dv = _attn_bwd_kv(f"attn_bwd_kv_d{d}", zsrc[d], dosrc[d], osrc[d], lsrc[d], d, T)
        dks.append(dk)
        dvs.append(dv)
    dz = _dz_assemble(dqs, dks, dvs, dcvg, T)
    gw_in = _cols_to_shards("gw_in_shards", _mm_tn_wide("gw_in", a, dz), N_IN)
    tok = send_grads("in", [("w_in", gw_in)])
    grad_x, d_g_mix = _mm_in_bwd(dz, w_in_f, xs, g_mix, dh1b, deps=[tok])

    small_part = _pack_small(
        [d_g_mix, d_b_dw, d_g_ln, d_b_ln, d_g_ffn, d_g_ple, d_b_pgate, d_g_final], d_w_dw,
        last=loss_part)
    small_slots = _exchange("exchange_small_grads", [(small_part, False)])[0]
    S = {n: _xwait(f"grads_wait_{n}", H[n], small_slots)
         for n in ("w_pgate", "w_ple", "w_down", "w_gate", "w_up", "w_out", "w_in")}

    mom = dict(w_in=(m_w_in, v_w_in), w_out=(m_w_out, v_w_out), w_gate=(m_w_gate, v_w_gate),
               w_up=(m_w_up, v_w_up), w_down=(m_w_down, v_w_down), w_pgate=(m_w_pgate, v_w_pgate),
               w_ple=(m_w_ple, v_w_ple))
    upd = {}
    for n in order:
        m_n, v_n = mom[n][0][0], mom[n][1][0]
        if n in tr_names:
            res = _adam(f"adam_{n}", S[n], big[n], m_n.T, v_n.T)
            upd[n] = [t.T[None] for t in res]
        else:
            res = _adam(f"adam_{n}", S[n], big[n], m_n, v_n)
            upd[n] = [t[None] for t in res]

    def lanes(v):
        full = jnp.zeros((CW, NDEV, DC // NDEV), F32)
        full = lax.dynamic_update_slice(full, v.reshape(CW, 1, DC // NDEV), (0, me, 0))
        return full.reshape(CW, DC)

    small_w = _pack_small([g_mix, b_dw, g_conv_ln, b_conv_ln, g_ffn, g_ple, b_pgate, g_final2], lanes(w_dw))
    small_m = _pack_small([m_g_mix, m_b_dw, m_g_conv_ln, m_b_conv_ln, m_g_ffn, m_g_ple, m_b_pgate,
                           m_g_final.reshape(1, D)], lanes(m_w_dw))
    small_v = _pack_small([v_g_mix, v_b_dw, v_g_conv_ln, v_b_conv_ln, v_g_ffn, v_g_ple, v_b_pgate,
                           v_g_final.reshape(1, D)], lanes(v_w_dw))
    small_res = _adam("adam_small", small_slots, small_w, small_m, small_v)

    def unpack(t):
        out = {}
        widths = dict(g_mix=D, b_dw=DC, g_conv_ln=DC, b_conv_ln=DC, g_ffn=D, g_ple=D, b_pgate=D, g_final=D)
        for i, n in enumerate(SMALL_NAMES):
            out[n] = t[i:i + 1, :widths[n]]
        out["g_final"] = out["g_final"].reshape(D)
        taps = t[len(SMALL_NAMES):len(SMALL_NAMES) + CW, :DC].reshape(CW, NDEV, DC // NDEV)
        out["w_dw"] = lax.dynamic_slice(taps, (0, me, 0), (CW, 1, DC // NDEV))[None]
        return out

    small = [unpack(t) for t in small_res]

    loss = small_res[0][SMALL_ROWS - 1, 0]
    names = ("g_mix", "w_in", "w_dw", "b_dw", "g_conv_ln", "b_conv_ln", "w_out", "g_ffn", "w_gate",
             "w_up", "w_down", "g_ple", "w_pgate", "b_pgate", "w_ple", "g_final")
    outs = [loss, grad_x.reshape(1, T, D)]
    for kind in range(4):
        for n in names:
            outs.append(upd[n][kind] if n in upd else small[kind][n])
    return tuple(outs)
```
